```python
import math
import jax, jax.numpy as jnp
from jax import lax
import numpy as np

D_MODEL = 1024
BATCH = 16
SEQ = 2048
DEPTH = 4

D_FF = 2816
NORM_EPS = 1e-6

HEAD_DIM = 64
ATT_PATTERNS = ((128, 1), (512, 4), (2048, 16))
N_ATT_GROUPS = len(ATT_PATTERNS)
HEADS_PER_GROUP = 8
N_ATT_HEADS = N_ATT_GROUPS * HEADS_PER_GROUP
ATT_OUT_DIM = HEADS_PER_GROUP * HEAD_DIM
ROPE_THETA = 10000.0

SSM_EXPAND = 2
D_INNER = SSM_EXPAND * D_MODEL
SSM_HEAD_DIM = 64
N_SSM_HEADS = D_INNER // SSM_HEAD_DIM
N_SSM_GROUPS = 4
HEADS_PER_SSM_GROUP = N_SSM_HEADS // N_SSM_GROUPS
D_STATE = 128
CONV_WIDTH = 4
SSD_CHUNK = 128
XBC_DIM = D_INNER + 2 * N_SSM_GROUPS * D_STATE

QKV_COLS = N_ATT_HEADS * HEAD_DIM
IN_SIZES = (QKV_COLS, QKV_COLS, QKV_COLS, D_INNER, XBC_DIM, N_SSM_HEADS, 2 * D_MODEL)
IN_SPLITS = tuple(int(v) for v in np.cumsum(IN_SIZES)[:-1])
N_IN = int(sum(IN_SIZES))

kernel_name = "hybrid_dilated_attn_ssd_macaron"


def rms_norm_f32(x, w):
    xf = x.astype(jnp.float32)
    y = xf * lax.rsqrt(jnp.mean(xf * xf, axis=-1, keepdims=True) + NORM_EPS)
    return y * w.astype(jnp.float32)


def swiglu(h, w_gate, w_up, w_down):
    return (jax.nn.silu(h @ w_gate) * (h @ w_up)) @ w_down


def rope(t, cos, sin):
    half = t.shape[-1] // 2
    t1, t2 = t[..., :half], t[..., half:]
    c = cos[None, :, None, None, :]
    s = sin[None, :, None, None, :]
    return jnp.concatenate([t1 * c - t2 * s, t2 * c + t1 * s], axis=-1)


def dilated_attention(q, k, v, dilation, band):
    bsz, s, h, hd = q.shape
    span = dilation * band
    s_pad = -(-s // span) * span
    sub_len = s_pad // dilation
    nb = sub_len // band

    def to_blocks(t):
        t = jnp.pad(t, ((0, 0), (0, s_pad - s), (0, 0), (0, 0)))
        t = t.reshape(bsz, sub_len, dilation, h, hd)
        t = t.transpose(0, 2, 3, 1, 4)
        return t.reshape(bsz, dilation, h, nb, band, hd)

    def with_prev(t):
        prev = jnp.pad(t, ((0, 0), (0, 0), (0, 0), (1, 0), (0, 0), (0, 0)))[:, :, :, :-1]
        return jnp.concatenate([prev, t], axis=4)

    qb = to_blocks(q)
    kk = with_prev(to_blocks(k))
    vv = with_prev(to_blocks(v))
    scores = jnp.einsum('bdhnqe,bdhnke->bdhnqk', qb, kk) * (hd ** -0.5)
    iq = jnp.arange(band)[:, None]
    ik = jnp.arange(2 * band)[None, :]
    rel = band + iq - ik
    band_ok = (rel >= 0) & (rel <= band)
    key_ok = (jnp.arange(nb)[:, None, None] * band - band + ik[None]) >= 0
    valid = band_ok[None] & key_ok
    scores = jnp.where(valid, scores, -jnp.inf)
    lse = jax.nn.logsumexp(scores, axis=-1)
    p = jnp.exp(scores - lse[..., None])
    o = jnp.einsum('bdhnqk,bdhnke->bdhnqe', p, vv)
    o = o.reshape(bsz, dilation, h, sub_len, hd).transpose(0, 3, 1, 2, 4)
    o = o.reshape(bsz, s_pad, h, hd)[:, :s]
    lse = lse.reshape(bsz, dilation, h, sub_len).transpose(0, 3, 1, 2)
    lse = lse.reshape(bsz, s_pad, h)[:, :s]
    return o, lse


def ssd_chunked_scan(x, dt, a, b_mat, c_mat):
    bsz, s, g, r, p = x.shape
    n = b_mat.shape[-1]
    nc = s // SSD_CHUNK

    def to_chunks(t):
        return jnp.moveaxis(t.reshape(bsz, nc, SSD_CHUNK, *t.shape[2:]), 1, 0)

    idx = jnp.arange(SSD_CHUNK)
    causal = idx[:, None] >= idx[None, :]

    def step(state, inp):
        xc, dtc, bc, cc = inp
        cum = jnp.cumsum(dtc * a, axis=1)
        cum_t = jnp.moveaxis(cum, 1, -1)
        diff = cum_t[..., :, None] - cum_t[..., None, :]
        decay = jnp.exp(jnp.where(causal, diff, -jnp.inf))
        cb = jnp.einsum('bign,bjgn->bgij', cc, bc)
        xdt = xc * dtc[..., None]
        y_diag = jnp.einsum('bgrij,bjgrp->bigrp', cb[:, :, None] * decay, xdt)
        y_off = jnp.einsum('bign,bgrpn->bigrp', cc, state) * jnp.exp(cum)[..., None]
        last = cum[:, -1]
        w_end = jnp.exp(last[:, None] - cum)
        new_state = state * jnp.exp(last)[..., None, None] + jnp.einsum(
            'bjgn,bjgrp->bgrpn', bc, xdt * w_end[..., None])
        return new_state, y_diag + y_off

    state0 = jnp.zeros((bsz, g, r, p, n), jnp.float32)
    xs = (to_chunks(x), to_chunks(dt), to_chunks(b_mat), to_chunks(c_mat))
    _, ys = lax.scan(step, state0, xs)
    return jnp.moveaxis(ys, 0, 1).reshape(bsz, s, g, r, p)


def _fwd_setup_inputs(seed: int = 0) -> dict:
    key = jax.random.key(seed)
    ks = jax.random.split(key, 24)

    def normal(k, shape, scale):
        return jax.random.normal(k, shape, jnp.float32) * scale

    def gain(k, shape):
        return 1.0 + 0.02 * jax.random.normal(k, shape, jnp.float32)

    dt0 = jnp.exp(jax.random.uniform(ks[13], (DEPTH, N_SSM_HEADS), jnp.float32)
                  * (math.log(0.1) - math.log(0.001)) + math.log(0.001))
    dt0 = jnp.maximum(dt0, 1e-4)
    return {
        'x': normal(ks[0], (BATCH, SEQ, D_MODEL), 1.0),
        'ffn1_norm_w': gain(ks[1], (DEPTH, D_MODEL)),
        'ffn1_w_gate': normal(ks[2], (DEPTH, D_MODEL, D_FF), D_MODEL ** -0.5),
        'ffn1_w_up': normal(ks[3], (DEPTH, D_MODEL, D_FF), D_MODEL ** -0.5),
        'ffn1_w_down': normal(ks[4], (DEPTH, D_FF, D_MODEL), D_FF ** -0.5),
        'mix_norm_w': gain(ks[5], (DEPTH, D_MODEL)),
        'w_in': normal(ks[6], (DEPTH, D_MODEL, N_IN), D_MODEL ** -0.5),
        'b_gates': normal(ks[7], (DEPTH, 2 * D_MODEL), 0.01),
        'q_norm_w': gain(ks[8], (DEPTH, HEAD_DIM)),
        'k_norm_w': gain(ks[9], (DEPTH, HEAD_DIM)),
        'conv_w': normal(ks[10], (DEPTH, CONV_WIDTH, XBC_DIM), CONV_WIDTH ** -0.5),
        'conv_b': normal(ks[11], (DEPTH, XBC_DIM), 0.01),
        'dt_bias': dt0 + jnp.log(-jnp.expm1(-dt0)),
        'a_log': jnp.log(jax.random.uniform(ks[12], (DEPTH, N_SSM_HEADS), jnp.float32, 1.0, 16.0)),
        'd_skip': gain(ks[14], (DEPTH, N_SSM_HEADS)),
        'ssm_norm_w': gain(ks[15], (DEPTH, D_INNER)),
        'w_att_proj': normal(ks[16], (DEPTH, ATT_OUT_DIM, D_MODEL), ATT_OUT_DIM ** -0.5),
        'w_ssm_proj': normal(ks[17], (DEPTH, D_INNER, D_MODEL), D_INNER ** -0.5),
        'w_out': normal(ks[18], (DEPTH, D_MODEL, D_MODEL), D_MODEL ** -0.5),
        'ffn2_norm_w': gain(ks[19], (DEPTH, D_MODEL)),
        'ffn2_w_gate': normal(ks[20], (DEPTH, D_MODEL, D_FF), D_MODEL ** -0.5),
        'ffn2_w_up': normal(ks[21], (DEPTH, D_MODEL, D_FF), D_MODEL ** -0.5),
        'ffn2_w_down': normal(ks[22], (DEPTH, D_FF, D_MODEL), D_FF ** -0.5),
    }


def _fwd_reference(x, ffn1_norm_w, ffn1_w_gate, ffn1_w_up, ffn1_w_down, mix_norm_w, w_in, b_gates,
              q_norm_w, k_norm_w, conv_w, conv_b, dt_bias, a_log, d_skip, ssm_norm_w,
              w_att_proj, w_ssm_proj, w_out, ffn2_norm_w, ffn2_w_gate, ffn2_w_up, ffn2_w_down):
    bsz, s, _ = x.shape
    dtype = x.dtype
    pos = jnp.arange(s, dtype=jnp.float32)
    inv_freq = 1.0 / (ROPE_THETA ** (jnp.arange(0, HEAD_DIM, 2, dtype=jnp.float32) / HEAD_DIM))
    ang = pos[:, None] * inv_freq[None, :]
    cos, sin = jnp.cos(ang), jnp.sin(ang)

    for i in range(DEPTH):
        h = rms_norm_f32(x, ffn1_norm_w[i]).astype(dtype)
        x = x + (0.5 * swiglu(h, ffn1_w_gate[i], ffn1_w_up[i], ffn1_w_down[i])).astype(dtype)

        h = rms_norm_f32(x, mix_norm_w[i]).astype(dtype)
        proj = h @ w_in[i]
        q, k, v, z, xbc, dt_raw, gate_pre = jnp.split(proj, IN_SPLITS, axis=-1)

        att_shape = (bsz, s, N_ATT_GROUPS, HEADS_PER_GROUP, HEAD_DIM)
        q = rope(rms_norm_f32(q.reshape(att_shape), q_norm_w[i]), cos, sin)
        k = rope(rms_norm_f32(k.reshape(att_shape), k_norm_w[i]), cos, sin)
        v = v.reshape(att_shape).astype(jnp.float32)
        outs, lses = [], []
        for g, (window, dilation) in enumerate(ATT_PATTERNS):
            o_g, lse_g = dilated_attention(q[:, :, g], k[:, :, g], v[:, :, g], dilation, window // dilation)
            outs.append(o_g)
            lses.append(lse_g)
        alpha = jax.nn.softmax(jnp.stack(lses), axis=0)
        att = jnp.sum(alpha[..., None] * jnp.stack(outs), axis=0)
        y_att = att.reshape(bsz, s, ATT_OUT_DIM).astype(dtype) @ w_att_proj[i]

        xbc = lax.conv_general_dilated(
            xbc.astype(jnp.float32), conv_w[i].astype(jnp.float32)[:, None, :],
            window_strides=(1,), padding=[(CONV_WIDTH - 1, 0)],
            dimension_numbers=('NWC', 'WIO', 'NWC'), feature_group_count=XBC_DIM)
        xbc = jax.nn.silu(xbc + conv_b[i].astype(jnp.float32))
        x_ssm, b_mat, c_mat = jnp.split(xbc, (D_INNER, D_INNER + N_SSM_GROUPS * D_STATE), axis=-1)
        x_ssm = x_ssm.reshape(bsz, s, N_SSM_GROUPS, HEADS_PER_SSM_GROUP, SSM_HEAD_DIM)
        b_mat = b_mat.reshape(bsz, s, N_SSM_GROUPS, D_STATE)
        c_mat = c_mat.reshape(bsz, s, N_SSM_GROUPS, D_STATE)
        dt = jax.nn.softplus(dt_raw.astype(jnp.float32) + dt_bias[i].astype(jnp.float32))
        dt = dt.reshape(bsz, s, N_SSM_GROUPS, HEADS_PER_SSM_GROUP)
        a = -jnp.exp(a_log[i].astype(jnp.float32)).reshape(N_SSM_GROUPS, HEADS_PER_SSM_GROUP)
        y = ssd_chunked_scan(x_ssm, dt, a, b_mat, c_mat)
        y = y + d_skip[i].astype(jnp.float32).reshape(N_SSM_GROUPS, HEADS_PER_SSM_GROUP)[..., None] * x_ssm
        y = y.reshape(bsz, s, D_INNER) * jax.nn.silu(z.astype(jnp.float32))
        y = rms_norm_f32(y.reshape(bsz, s, N_SSM_GROUPS, D_INNER // N_SSM_GROUPS),
                         ssm_norm_w[i].reshape(N_SSM_GROUPS, D_INNER // N_SSM_GROUPS))
        y_ssm = y.reshape(bsz, s, D_INNER).astype(dtype) @ w_ssm_proj[i]

        gates = jax.nn.sigmoid(gate_pre.astype(jnp.float32) + b_gates[i].astype(jnp.float32))
        g_att, g_ssm = jnp.split(gates, 2, axis=-1)
        mixed = (g_att * y_att.astype(jnp.float32) + g_ssm * y_ssm.astype(jnp.float32)).astype(dtype)
        x = x + (mixed @ w_out[i]).astype(dtype)

        h = rms_norm_f32(x, ffn2_norm_w[i]).astype(dtype)
        x = x + (0.5 * swiglu(h, ffn2_w_gate[i], ffn2_w_up[i], ffn2_w_down[i])).astype(dtype)
    return x


import jax as _jax
import jax.numpy as _jnp

TWIN_FORMAT = 'train_step'
FWD_PARAMS = ['x', 'ffn1_norm_w', 'ffn1_w_gate', 'ffn1_w_up', 'ffn1_w_down', 'mix_norm_w', 'w_in', 'b_gates', 'q_norm_w', 'k_norm_w', 'conv_w', 'conv_b', 'dt_bias', 'a_log', 'd_skip', 'ssm_norm_w', 'w_att_proj', 'w_ssm_proj', 'w_out', 'ffn2_norm_w', 'ffn2_w_gate', 'ffn2_w_up', 'ffn2_w_down']
TWIN_WEIGHTS = ['ffn1_norm_w', 'ffn1_w_gate', 'ffn1_w_up', 'ffn1_w_down', 'mix_norm_w', 'w_in', 'b_gates', 'q_norm_w', 'k_norm_w', 'conv_w', 'conv_b', 'dt_bias', 'a_log', 'd_skip', 'ssm_norm_w', 'w_att_proj', 'w_ssm_proj', 'w_out', 'ffn2_norm_w', 'ffn2_w_gate', 'ffn2_w_up', 'ffn2_w_down']
TWIN_DIFF_INPUT = 'x'
TWIN_INPUTS = ['x', 'ffn1_norm_w', 'ffn1_w_gate', 'ffn1_w_up', 'ffn1_w_down', 'mix_norm_w', 'w_in', 'b_gates', 'q_norm_w', 'k_norm_w', 'conv_w', 'conv_b', 'dt_bias', 'a_log', 'd_skip', 'ssm_norm_w', 'w_att_proj', 'w_ssm_proj', 'w_out', 'ffn2_norm_w', 'ffn2_w_gate', 'ffn2_w_up', 'ffn2_w_down', 'loss_target', 'm_ffn1_norm_w', 'm_ffn1_w_gate', 'm_ffn1_w_up', 'm_ffn1_w_down', 'm_mix_norm_w', 'm_w_in', 'm_b_gates', 'm_q_norm_w', 'm_k_norm_w', 'm_conv_w', 'm_conv_b', 'm_dt_bias', 'm_a_log', 'm_d_skip', 'm_ssm_norm_w', 'm_w_att_proj', 'm_w_ssm_proj', 'm_w_out', 'm_ffn2_norm_w', 'm_ffn2_w_gate', 'm_ffn2_w_up', 'm_ffn2_w_down', 'v_ffn1_norm_w', 'v_ffn1_w_gate', 'v_ffn1_w_up', 'v_ffn1_w_down', 'v_mix_norm_w', 'v_w_in', 'v_b_gates', 'v_q_norm_w', 'v_k_norm_w', 'v_conv_w', 'v_conv_b', 'v_dt_bias', 'v_a_log', 'v_d_skip', 'v_ssm_norm_w', 'v_w_att_proj', 'v_w_ssm_proj', 'v_w_out', 'v_ffn2_norm_w', 'v_ffn2_w_gate', 'v_ffn2_w_up', 'v_ffn2_w_down']
TWIN_OUTPUTS = ['loss', 'grad_x', 'grad_ffn1_norm_w', 'grad_ffn1_w_gate', 'grad_ffn1_w_up', 'grad_ffn1_w_down', 'grad_mix_norm_w', 'grad_w_in', 'grad_b_gates', 'grad_q_norm_w', 'grad_k_norm_w', 'grad_conv_w', 'grad_conv_b', 'grad_dt_bias', 'grad_a_log', 'grad_d_skip', 'grad_ssm_norm_w', 'grad_w_att_proj', 'grad_w_ssm_proj', 'grad_w_out', 'grad_ffn2_norm_w', 'grad_ffn2_w_gate', 'grad_ffn2_w_up', 'grad_ffn2_w_down', 'delta_ffn1_norm_w', 'delta_ffn1_w_gate', 'delta_ffn1_w_up', 'delta_ffn1_w_down', 'delta_mix_norm_w', 'delta_w_in', 'delta_b_gates', 'delta_q_norm_w', 'delta_k_norm_w', 'delta_conv_w', 'delta_conv_b', 'delta_dt_bias', 'delta_a_log', 'delta_d_skip', 'delta_ssm_norm_w', 'delta_w_att_proj', 'delta_w_ssm_proj', 'delta_w_out', 'delta_ffn2_norm_w', 'delta_ffn2_w_gate', 'delta_ffn2_w_up', 'delta_ffn2_w_down', 'new_m_ffn1_norm_w', 'new_m_ffn1_w_gate', 'new_m_ffn1_w_up', 'new_m_ffn1_w_down', 'new_m_mix_norm_w', 'new_m_w_in', 'new_m_b_gates', 'new_m_q_norm_w', 'new_m_k_norm_w', 'new_m_conv_w', 'new_m_conv_b', 'new_m_dt_bias', 'new_m_a_log', 'new_m_d_skip', 'new_m_ssm_norm_w', 'new_m_w_att_proj', 'new_m_w_ssm_proj', 'new_m_w_out', 'new_m_ffn2_norm_w', 'new_m_ffn2_w_gate', 'new_m_ffn2_w_up', 'new_m_ffn2_w_down', 'new_v_ffn1_norm_w', 'new_v_ffn1_w_gate', 'new_v_ffn1_w_up', 'new_v_ffn1_w_down', 'new_v_mix_norm_w', 'new_v_w_in', 'new_v_b_gates', 'new_v_q_norm_w', 'new_v_k_norm_w', 'new_v_conv_w', 'new_v_conv_b', 'new_v_dt_bias', 'new_v_a_log', 'new_v_d_skip', 'new_v_ssm_norm_w', 'new_v_w_att_proj', 'new_v_w_ssm_proj', 'new_v_w_out', 'new_v_ffn2_norm_w', 'new_v_ffn2_w_gate', 'new_v_ffn2_w_up', 'new_v_ffn2_w_down']
TWIN_LEAF_KINDS = {'loss': 'loss', 'grad_x': 'grad_x', 'grad_ffn1_norm_w': 'grad_w', 'grad_ffn1_w_gate': 'grad_w', 'grad_ffn1_w_up': 'grad_w', 'grad_ffn1_w_down': 'grad_w', 'grad_mix_norm_w': 'grad_w', 'grad_w_in': 'grad_w', 'grad_b_gates': 'grad_w', 'grad_q_norm_w': 'grad_w', 'grad_k_norm_w': 'grad_w', 'grad_conv_w': 'grad_w', 'grad_conv_b': 'grad_w', 'grad_dt_bias': 'grad_w', 'grad_a_log': 'grad_w', 'grad_d_skip': 'grad_w', 'grad_ssm_norm_w': 'grad_w', 'grad_w_att_proj': 'grad_w', 'grad_w_ssm_proj': 'grad_w', 'grad_w_out': 'grad_w', 'grad_ffn2_norm_w': 'grad_w', 'grad_ffn2_w_gate': 'grad_w', 'grad_ffn2_w_up': 'grad_w', 'grad_ffn2_w_down': 'grad_w', 'delta_ffn1_norm_w': 'delta_w', 'delta_ffn1_w_gate': 'delta_w', 'delta_ffn1_w_up': 'delta_w', 'delta_ffn1_w_down': 'delta_w', 'delta_mix_norm_w': 'delta_w', 'delta_w_in': 'delta_w', 'delta_b_gates': 'delta_w', 'delta_q_norm_w': 'delta_w', 'delta_k_norm_w': 'delta_w', 'delta_conv_w': 'delta_w', 'delta_conv_b': 'delta_w', 'delta_dt_bias': 'delta_w', 'delta_a_log': 'delta_w', 'delta_d_skip': 'delta_w', 'delta_ssm_norm_w': 'delta_w', 'delta_w_att_proj': 'delta_w', 'delta_w_ssm_proj': 'delta_w', 'delta_w_out': 'delta_w', 'delta_ffn2_norm_w': 'delta_w', 'delta_ffn2_w_gate': 'delta_w', 'delta_ffn2_w_up': 'delta_w', 'delta_ffn2_w_down': 'delta_w', 'new_m_ffn1_norm_w': 'new_m', 'new_m_ffn1_w_gate': 'new_m', 'new_m_ffn1_w_up': 'new_m', 'new_m_ffn1_w_down': 'new_m', 'new_m_mix_norm_w': 'new_m', 'new_m_w_in': 'new_m', 'new_m_b_gates': 'new_m', 'new_m_q_norm_w': 'new_m', 'new_m_k_norm_w': 'new_m', 'new_m_conv_w': 'new_m', 'new_m_conv_b': 'new_m', 'new_m_dt_bias': 'new_m', 'new_m_a_log': 'new_m', 'new_m_d_skip': 'new_m', 'new_m_ssm_norm_w': 'new_m', 'new_m_w_att_proj': 'new_m', 'new_m_w_ssm_proj': 'new_m', 'new_m_w_out': 'new_m', 'new_m_ffn2_norm_w': 'new_m', 'new_m_ffn2_w_gate': 'new_m', 'new_m_ffn2_w_up': 'new_m', 'new_m_ffn2_w_down': 'new_m', 'new_v_ffn1_norm_w': 'new_v', 'new_v_ffn1_w_gate': 'new_v', 'new_v_ffn1_w_up': 'new_v', 'new_v_ffn1_w_down': 'new_v', 'new_v_mix_norm_w': 'new_v', 'new_v_w_in': 'new_v', 'new_v_b_gates': 'new_v', 'new_v_q_norm_w': 'new_v', 'new_v_k_norm_w': 'new_v', 'new_v_conv_w': 'new_v', 'new_v_conv_b': 'new_v', 'new_v_dt_bias': 'new_v', 'new_v_a_log': 'new_v', 'new_v_d_skip': 'new_v', 'new_v_ssm_norm_w': 'new_v', 'new_v_w_att_proj': 'new_v', 'new_v_w_ssm_proj': 'new_v', 'new_v_w_out': 'new_v', 'new_v_ffn2_norm_w': 'new_v', 'new_v_ffn2_w_gate': 'new_v', 'new_v_ffn2_w_up': 'new_v', 'new_v_ffn2_w_down': 'new_v'}


def _forward(args):
    return _fwd_reference(*[args[k] for k in FWD_PARAMS])


def _output_shape():
    out = _jax.eval_shape(lambda: _forward(_fwd_setup_inputs(0)))
    return out.shape, out.dtype

N_MICROBATCH = 1
ADAM_LR = 0.001
ADAM_B1 = 0.9
ADAM_B2 = 0.999
ADAM_EPS = 1e-08
ADAM_WD = 0.01
ADAM_STEP = 10
PER_EXAMPLE_BATCH_AXIS = {'x': 0, 'loss_target': 0}
SHARED_INPUTS = []
_WEIGHT_DTYPES = {'ffn1_norm_w': _jnp.float32, 'ffn1_w_gate': _jnp.float32, 'ffn1_w_up': _jnp.float32, 'ffn1_w_down': _jnp.float32, 'mix_norm_w': _jnp.float32, 'w_in': _jnp.float32, 'b_gates': _jnp.float32, 'q_norm_w': _jnp.float32, 'k_norm_w': _jnp.float32, 'conv_w': _jnp.float32, 'conv_b': _jnp.float32, 'dt_bias': _jnp.float32, 'a_log': _jnp.float32, 'd_skip': _jnp.float32, 'ssm_norm_w': _jnp.float32, 'w_att_proj': _jnp.float32, 'w_ssm_proj': _jnp.float32, 'w_out': _jnp.float32, 'ffn2_norm_w': _jnp.float32, 'ffn2_w_gate': _jnp.float32, 'ffn2_w_up': _jnp.float32, 'ffn2_w_down': _jnp.float32}
MOMENT_SCALE = {'ffn1_norm_w': 6.038552e+00, 'ffn1_w_gate': 1.398871e-01, 'ffn1_w_up': 1.421885e-01, 'ffn1_w_down': 2.343381e-01, 'mix_norm_w': 8.808279e-01, 'w_in': 1.669555e-01, 'b_gates': 2.334049e+00, 'q_norm_w': 6.420836e-01, 'k_norm_w': 6.388857e-01, 'conv_w': 3.098849e-01, 'conv_b': 9.182262e-01, 'dt_bias': 7.634440e-01, 'a_log': 1.534815e+00, 'd_skip': 1.365349e+00, 'ssm_norm_w': 6.559199e+00, 'w_att_proj': 9.307539e-02, 'w_ssm_proj': 7.102762e-01, 'w_out': 6.643411e-01, 'ffn2_norm_w': 6.086554e+00, 'ffn2_w_gate': 1.166545e-01, 'ffn2_w_up': 1.216092e-01, 'ffn2_w_down': 2.001023e-01}


def _to_microbatches(a, axis):
    t = _jnp.moveaxis(a, axis, 0)
    t = t.reshape((N_MICROBATCH, t.shape[0] // N_MICROBATCH) + t.shape[1:])
    return _jnp.moveaxis(t, 1, axis + 1)


def setup_inputs(seed: int = 0) -> dict:
    inp = _fwd_setup_inputs(seed)
    key = _jax.random.fold_in(_jax.random.key(seed), 7919)
    shape, _ = _output_shape()
    out = dict(inp)
    out["loss_target"] = _jax.random.normal(_jax.random.fold_in(key, 0), shape, _jnp.float32)
    for i, name in enumerate(TWIN_WEIGHTS):
        w = inp[name].astype(_jnp.float32)
        if MOMENT_SCALE is None:
            s = _jnp.sqrt(_jnp.mean(_jnp.square(w)) + 1e-30)
        else:
            s = MOMENT_SCALE[name]
        km, kv = _jax.random.split(_jax.random.fold_in(key, i + 1))
        out[name] = w
        out["m_" + name] = s * _jax.random.normal(km, w.shape, _jnp.float32)
        out["v_" + name] = (s * s) * _jax.random.uniform(kv, w.shape, _jnp.float32, 0.5, 1.5)
    if N_MICROBATCH > 1:
        for name, axis in PER_EXAMPLE_BATCH_AXIS.items():
            out[name] = _to_microbatches(out[name], axis)
    return {'x': out['x'], 'ffn1_norm_w': out['ffn1_norm_w'], 'ffn1_w_gate': out['ffn1_w_gate'], 'ffn1_w_up': out['ffn1_w_up'], 'ffn1_w_down': out['ffn1_w_down'], 'mix_norm_w': out['mix_norm_w'], 'w_in': out['w_in'], 'b_gates': out['b_gates'], 'q_norm_w': out['q_norm_w'], 'k_norm_w': out['k_norm_w'], 'conv_w': out['conv_w'], 'conv_b': out['conv_b'], 'dt_bias': out['dt_bias'], 'a_log': out['a_log'], 'd_skip': out['d_skip'], 'ssm_norm_w': out['ssm_norm_w'], 'w_att_proj': out['w_att_proj'], 'w_ssm_proj': out['w_ssm_proj'], 'w_out': out['w_out'], 'ffn2_norm_w': out['ffn2_norm_w'], 'ffn2_w_gate': out['ffn2_w_gate'], 'ffn2_w_up': out['ffn2_w_up'], 'ffn2_w_down': out['ffn2_w_down'], 'loss_target': out['loss_target'], 'm_ffn1_norm_w': out['m_ffn1_norm_w'], 'm_ffn1_w_gate': out['m_ffn1_w_gate'], 'm_ffn1_w_up': out['m_ffn1_w_up'], 'm_ffn1_w_down': out['m_ffn1_w_down'], 'm_mix_norm_w': out['m_mix_norm_w'], 'm_w_in': out['m_w_in'], 'm_b_gates': out['m_b_gates'], 'm_q_norm_w': out['m_q_norm_w'], 'm_k_norm_w': out['m_k_norm_w'], 'm_conv_w': out['m_conv_w'], 'm_conv_b': out['m_conv_b'], 'm_dt_bias': out['m_dt_bias'], 'm_a_log': out['m_a_log'], 'm_d_skip': out['m_d_skip'], 'm_ssm_norm_w': out['m_ssm_norm_w'], 'm_w_att_proj': out['m_w_att_proj'], 'm_w_ssm_proj': out['m_w_ssm_proj'], 'm_w_out': out['m_w_out'], 'm_ffn2_norm_w': out['m_ffn2_norm_w'], 'm_ffn2_w_gate': out['m_ffn2_w_gate'], 'm_ffn2_w_up': out['m_ffn2_w_up'], 'm_ffn2_w_down': out['m_ffn2_w_down'], 'v_ffn1_norm_w': out['v_ffn1_norm_w'], 'v_ffn1_w_gate': out['v_ffn1_w_gate'], 'v_ffn1_w_up': out['v_ffn1_w_up'], 'v_ffn1_w_down': out['v_ffn1_w_down'], 'v_mix_norm_w': out['v_mix_norm_w'], 'v_w_in': out['v_w_in'], 'v_b_gates': out['v_b_gates'], 'v_q_norm_w': out['v_q_norm_w'], 'v_k_norm_w': out['v_k_norm_w'], 'v_conv_w': out['v_conv_w'], 'v_conv_b': out['v_conv_b'], 'v_dt_bias': out['v_dt_bias'], 'v_a_log': out['v_a_log'], 'v_d_skip': out['v_d_skip'], 'v_ssm_norm_w': out['v_ssm_norm_w'], 'v_w_att_proj': out['v_w_att_proj'], 'v_w_ssm_proj': out['v_w_ssm_proj'], 'v_w_out': out['v_w_out'], 'v_ffn2_norm_w': out['v_ffn2_norm_w'], 'v_ffn2_w_gate': out['v_ffn2_w_gate'], 'v_ffn2_w_up': out['v_ffn2_w_up'], 'v_ffn2_w_down': out['v_ffn2_w_down']}


def _loss(weights, diff, rest, loss_target):
    with _jax.named_scope("forward"):
        args = {**rest, TWIN_DIFF_INPUT: diff, **{k: w.astype(_WEIGHT_DTYPES[k]) for k, w in weights.items()}}
        y = _forward(args)
    with _jax.named_scope("loss_head"):
        err = _jnp.square(y.astype(_jnp.float32) - loss_target)
        return 0.5 * _jnp.sum(_jnp.mean(err, axis=-1)) if err.ndim else 0.5 * err


def _adamw(w, g, m, v):
    m = ADAM_B1 * m + (1.0 - ADAM_B1) * g
    v = ADAM_B2 * v + (1.0 - ADAM_B2) * _jnp.square(g)
    m_hat = m / (1.0 - ADAM_B1 ** ADAM_STEP)
    v_hat = v / (1.0 - ADAM_B2 ** ADAM_STEP)
    delta = -ADAM_LR * (m_hat / (_jnp.sqrt(v_hat) + ADAM_EPS) + ADAM_WD * w)
    return delta, m, v


def reference(x, ffn1_norm_w, ffn1_w_gate, ffn1_w_up, ffn1_w_down, mix_norm_w, w_in, b_gates, q_norm_w, k_norm_w, conv_w, conv_b, dt_bias, a_log, d_skip, ssm_norm_w, w_att_proj, w_ssm_proj, w_out, ffn2_norm_w, ffn2_w_gate, ffn2_w_up, ffn2_w_down, loss_target, m_ffn1_norm_w, m_ffn1_w_gate, m_ffn1_w_up, m_ffn1_w_down, m_mix_norm_w, m_w_in, m_b_gates, m_q_norm_w, m_k_norm_w, m_conv_w, m_conv_b, m_dt_bias, m_a_log, m_d_skip, m_ssm_norm_w, m_w_att_proj, m_w_ssm_proj, m_w_out, m_ffn2_norm_w, m_ffn2_w_gate, m_ffn2_w_up, m_ffn2_w_down, v_ffn1_norm_w, v_ffn1_w_gate, v_ffn1_w_up, v_ffn1_w_down, v_mix_norm_w, v_w_in, v_b_gates, v_q_norm_w, v_k_norm_w, v_conv_w, v_conv_b, v_dt_bias, v_a_log, v_d_skip, v_ssm_norm_w, v_w_att_proj, v_w_ssm_proj, v_w_out, v_ffn2_norm_w, v_ffn2_w_gate, v_ffn2_w_up, v_ffn2_w_down):
    given = dict(x=x, ffn1_norm_w=ffn1_norm_w, ffn1_w_gate=ffn1_w_gate, ffn1_w_up=ffn1_w_up, ffn1_w_down=ffn1_w_down, mix_norm_w=mix_norm_w, w_in=w_in, b_gates=b_gates, q_norm_w=q_norm_w, k_norm_w=k_norm_w, conv_w=conv_w, conv_b=conv_b, dt_bias=dt_bias, a_log=a_log, d_skip=d_skip, ssm_norm_w=ssm_norm_w, w_att_proj=w_att_proj, w_ssm_proj=w_ssm_proj, w_out=w_out, ffn2_norm_w=ffn2_norm_w, ffn2_w_gate=ffn2_w_gate, ffn2_w_up=ffn2_w_up, ffn2_w_down=ffn2_w_down, loss_target=loss_target, m_ffn1_norm_w=m_ffn1_norm_w, m_ffn1_w_gate=m_ffn1_w_gate, m_ffn1_w_up=m_ffn1_w_up, m_ffn1_w_down=m_ffn1_w_down, m_mix_norm_w=m_mix_norm_w, m_w_in=m_w_in, m_b_gates=m_b_gates, m_q_norm_w=m_q_norm_w, m_k_norm_w=m_k_norm_w, m_conv_w=m_conv_w, m_conv_b=m_conv_b, m_dt_bias=m_dt_bias, m_a_log=m_a_log, m_d_skip=m_d_skip, m_ssm_norm_w=m_ssm_norm_w, m_w_att_proj=m_w_att_proj, m_w_ssm_proj=m_w_ssm_proj, m_w_out=m_w_out, m_ffn2_norm_w=m_ffn2_norm_w, m_ffn2_w_gate=m_ffn2_w_gate, m_ffn2_w_up=m_ffn2_w_up, m_ffn2_w_down=m_ffn2_w_down, v_ffn1_norm_w=v_ffn1_norm_w, v_ffn1_w_gate=v_ffn1_w_gate, v_ffn1_w_up=v_ffn1_w_up, v_ffn1_w_down=v_ffn1_w_down, v_mix_norm_w=v_mix_norm_w, v_w_in=v_w_in, v_b_gates=v_b_gates, v_q_norm_w=v_q_norm_w, v_k_norm_w=v_k_norm_w, v_conv_w=v_conv_w, v_conv_b=v_conv_b, v_dt_bias=v_dt_bias, v_a_log=v_a_log, v_d_skip=v_d_skip, v_ssm_norm_w=v_ssm_norm_w, v_w_att_proj=v_w_att_proj, v_w_ssm_proj=v_w_ssm_proj, v_w_out=v_w_out, v_ffn2_norm_w=v_ffn2_norm_w, v_ffn2_w_gate=v_ffn2_w_gate, v_ffn2_w_up=v_ffn2_w_up, v_ffn2_w_down=v_ffn2_w_down)
    weights = {n: given[n] for n in TWIN_WEIGHTS}
    shared = {n: given[n] for n in SHARED_INPUTS}
    per_example = {n: given[n] for n in ['x']}
    grad_fn = _jax.value_and_grad(_loss, argnums=(0, 1))

    def one_microbatch(ex, loss_target):
        ex = dict(ex)
        diff = ex.pop(TWIN_DIFF_INPUT)
        return grad_fn(weights, diff, {**shared, **ex}, loss_target)

    if N_MICROBATCH == 1:
        loss, (grad_w, grad_x) = one_microbatch(per_example, given["loss_target"])
    else:
        def body(carry, xs):
            loss_sum, grad_sum = carry
            l_k, (gw_k, gx_k) = one_microbatch(xs[0], xs[1])
            with _jax.named_scope("update"):
                return (loss_sum + l_k, _jax.tree.map(_jnp.add, grad_sum, gw_k)), gx_k

        init = (_jnp.zeros((), _jnp.float32), _jax.tree.map(_jnp.zeros_like, weights))
        (loss, grad_w), grad_x = _jax.lax.scan(body, init, (per_example, given["loss_target"]))
    with _jax.named_scope("update"):
        delta_w, new_m, new_v = {}, {}, {}
        for n in TWIN_WEIGHTS:
            delta_w[n], new_m[n], new_v[n] = _adamw(weights[n], grad_w[n], given["m_" + n], given["v_" + n])
    return (loss, grad_x, *[grad_w[n] for n in TWIN_WEIGHTS], *[delta_w[n] for n in TWIN_WEIGHTS],
            *[new_m[n] for n in TWIN_WEIGHTS], *[new_v[n] for n in TWIN_WEIGHTS])
```

```python
import functools
import math

import numpy as np
import jax
import jax.numpy as jnp
from jax import lax
from jax.experimental import pallas as pl
from jax.experimental.pallas import tpu as pltpu

F32 = jnp.float32
BF16 = jnp.bfloat16
HI = lax.Precision.HIGHEST
MESH = pl.DeviceIdType.MESH

D_MODEL = 1024
SEQ = 2048
DEPTH = 4
D_FF = 2816
ATT_DILATIONS = (1, 4, 16)
BAND = 128
ATT_OUT = 512
QKV = 1536
D_INNER = 2048
N_SSM_HEADS = 32
N_SSM_GROUPS = 4
D_STATE = 128
XBC = 3072
CHUNK = 128
N_IN = 11808
EPS = 1e-6
ROPE_THETA = 10000.0
NP = 12288
Q0, K0, V0, Z0, G0, X0, DT0 = 0, 1536, 3072, 4608, 6656, 8704, 11776
DTW = 128
LR, B1, B2, ADAM_EPS, WD, STEP = 0.001, 0.9, 0.999, 1e-08, 0.01, 10

LANES = 128
VMEM_LIMIT = 48 * 1024 * 1024
NEG = -1e30


def _cp(sem=None, **kw):
    return pltpu.CompilerParams(dimension_semantics=sem, vmem_limit_bytes=VMEM_LIMIT, **kw)


def _dg(a, b, ca, cb):
    return lax.dot_general(a.astype(BF16), b.astype(BF16), (((ca,), (cb,)), ((), ())), preferred_element_type=F32)


@jax.custom_vjp
def dot_nn(a, b):
    return _dg(a, b, 1, 0)


def _dot_nn_fwd(a, b):
    return _dg(a, b, 1, 0), (a, b)


def _dot_nn_bwd(r, g):
    a, b = r
    return _dg(g, b, 1, 1).astype(a.dtype), _dg(a, g, 0, 0).astype(b.dtype)


dot_nn.defvjp(_dot_nn_fwd, _dot_nn_bwd)


@jax.custom_vjp
def dot_nt(a, b):
    return _dg(a, b, 1, 1)


def _dot_nt_fwd(a, b):
    return _dg(a, b, 1, 1), (a, b)


def _dot_nt_bwd(r, g):
    a, b = r
    return _dg(g, b, 1, 0).astype(a.dtype), _dg(g, a, 0, 0).astype(b.dtype)


dot_nt.defvjp(_dot_nt_fwd, _dot_nt_bwd)


@jax.custom_vjp
def dot_tn(a, b):
    return _dg(a, b, 0, 0)


def _dot_tn_fwd(a, b):
    return _dg(a, b, 0, 0), (a, b)


def _dot_tn_bwd(r, g):
    a, b = r
    return _dg(b, g, 1, 1).astype(a.dtype), _dg(a, g, 1, 0).astype(b.dtype)


dot_tn.defvjp(_dot_tn_fwd, _dot_tn_bwd)


def dot_hi(a, b):
    return jnp.dot(a, b, precision=HI, preferred_element_type=F32)


def _shift_rows_raw(x, s):
    n = x.shape[0]
    r = pltpu.roll(x, s % n, 0)
    rows = lax.broadcasted_iota(jnp.int32, x.shape, 0)
    keep = rows >= s if s > 0 else rows < n + s
    return jnp.where(keep, r, 0.0)


@functools.partial(jax.custom_vjp, nondiff_argnums=(1,))
def shift_rows(x, s):
    return _shift_rows_raw(x, s)


def _shift_fwd(x, s):
    return _shift_rows_raw(x, s), None


def _shift_bwd(s, _, g):
    return (_shift_rows_raw(g, -s),)


shift_rows.defvjp(_shift_fwd, _shift_bwd)


def _sigmoid(x):
    return 1.0 / (1.0 + jnp.exp(-x))


def _silu(x):
    return x * _sigmoid(x)


def _softplus(x):
    return jnp.maximum(x, 0.0) + jnp.log(1.0 + jnp.exp(-jnp.abs(x)))


def _head_mean_mat():
    i = np.arange(LANES)
    return jnp.asarray((i[:, None] // 64 == i[None, :] // 64).astype(np.float32) / 64.0)


def _rope_rot_mat():
    p = np.zeros((LANES, LANES), np.float32)
    for l in range(LANES):
        if (l % 64) < 32:
            p[l + 32, l] = -1.0
        else:
            p[l - 32, l] = 1.0
    return jnp.asarray(p)


def _tile64_mat():
    t = np.zeros((LANES, LANES), np.float32)
    for l in range(LANES):
        t[l % 64, l] = 1.0
    return jnp.asarray(t)


def _head_expand_mat():
    e = np.zeros((LANES, D_INNER), np.float32)
    for l in range(D_INNER):
        e[l // 64, l] = 1.0
    return jnp.asarray(e)


def _ltri_mat():
    i = np.arange(CHUNK)
    return jnp.asarray((i[:, None] >= i[None, :]).astype(np.float32))


def _rope_tables():
    pos = jnp.arange(SEQ, dtype=F32)
    inv_freq = 1.0 / (ROPE_THETA ** (jnp.arange(0, 64, 2, dtype=F32) / 64))
    ang = pos[:, None] * inv_freq[None, :]
    return jnp.tile(jnp.cos(ang), (1, 4)), jnp.tile(jnp.sin(ang), (1, 4))


def _pick(n, cap):
    best = None
    for t in range(LANES, min(n, cap) + 1, LANES):
        if n % t == 0:
            best = t
    return best if best is not None else n


def _mm(name, a, b, mode, out_dtype=F32, alpha=None, res=None):
    if mode == "nn":
        (m, k), n = a.shape, b.shape[1]
    elif mode == "nt":
        (m, k), n = a.shape, b.shape[0]
    else:
        (k, m), n = a.shape, b.shape[1]
    tm, tn, tk = _pick(m, 1408), _pick(n, 1408), _pick(k, 512)
    nk = k // tk
    ca, cb = {"nn": (1, 0), "nt": (1, 1), "tn": (0, 0)}[mode]
    a_spec = pl.BlockSpec((tk, tm), lambda i, j, kk: (kk, i)) if mode == "tn" else pl.BlockSpec((tm, tk), lambda i, j, kk: (i, kk))
    b_spec = pl.BlockSpec((tn, tk), lambda i, j, kk: (j, kk)) if mode == "nt" else pl.BlockSpec((tk, tn), lambda i, j, kk: (kk, j))
    o_spec = pl.BlockSpec((tm, tn), lambda i, j, kk: (i, j))
    has_res = res is not None

    def finish(acc, res_ref, o_ref):
        if alpha is not None:
            acc = acc * alpha
        if has_res:
            acc = acc + res_ref[...].astype(F32)
        o_ref[...] = acc.astype(o_ref.dtype)

    def body(*refs):
        a_ref, b_ref = refs[0], refs[1]
        res_ref = refs[2] if has_res else None
        o_ref = refs[3] if has_res else refs[2]
        part = _dg(a_ref[...], b_ref[...], ca, cb)
        if nk == 1:
            finish(part, res_ref, o_ref)
            return
        acc_ref = refs[-1]
        kk = pl.program_id(2)

        @pl.when(kk == 0)
        def _():
            acc_ref[...] = part

        @pl.when(kk > 0)
        def _():
            acc_ref[...] += part

        @pl.when(kk == nk - 1)
        def _():
            finish(acc_ref[...], res_ref, o_ref)

    ins = [a, b] + ([res] if has_res else [])
    in_specs = [a_spec, b_spec] + ([o_spec] if has_res else [])
    return pl.pallas_call(
        body, name=name, grid=(m // tm, n // tn, nk), in_specs=in_specs, out_specs=o_spec,
        out_shape=jax.ShapeDtypeStruct((m, n), out_dtype),
        scratch_shapes=[pltpu.VMEM((tm, tn), F32)] if nk > 1 else [],
        compiler_params=_cp(("parallel", "parallel", "arbitrary")),
    )(*ins)


def _ew(name, fn, grid, ins, outs, scratch=()):
    n_in, n_out = len(ins), len(outs)

    def body(*refs):
        vals = [r[...] for r in refs[:n_in]]
        res = fn(*vals, *refs[n_in + n_out:])
        for r, v in zip(refs[n_in:n_in + n_out], res):
            r[...] = v.astype(r.dtype)

    res = pl.pallas_call(
        body, name=name, grid=grid,
        in_specs=[pl.BlockSpec(b, m) for _, b, m in ins],
        out_specs=[pl.BlockSpec(b, m) for _, _, b, m in outs],
        out_shape=[jax.ShapeDtypeStruct(s, d) for s, d, _, _ in outs],
        scratch_shapes=list(scratch),
        compiler_params=_cp(("arbitrary",) * len(grid)),
    )(*[a for a, _, _ in ins])
    return res


def _ew_bwd(name, fn, grid, ins, cts, wrt, adds=(), ct_fn=None):
    n_in, n_ct, n_add = len(ins), len(cts), len(adds)
    idxs = [w["idx"] for w in wrt]

    def body(*refs):
        prim = [r[...] for r in refs[:n_in]]
        ct = [r[...].astype(F32) for r in refs[n_in:n_in + n_ct]]
        addv = [r[...] for r in refs[n_in + n_ct:n_in + n_ct + n_add]]
        orefs = refs[n_in + n_ct + n_add:]

        def f(*sel):
            full = list(prim)
            for i, s in zip(idxs, sel):
                full[i] = s
            return fn(*full)

        _, vjp = jax.vjp(f, *[prim[i].astype(F32) for i in idxs])
        grads = vjp(tuple(ct) if ct_fn is None else ct_fn(*ct))
        for w, g, r in zip(wrt, grads, orefs):
            if w["kind"] == "tile":
                if w.get("add") is not None:
                    g = g + addv[w["add"]].astype(F32)
                r[...] = g.astype(r.dtype)
            else:
                first = w["first"]()

                @pl.when(first)
                def _(r=r, g=g):
                    r[...] = g.astype(r.dtype)

                @pl.when(jnp.logical_not(first))
                def _(r=r, g=g):
                    r[...] += g.astype(r.dtype)

    allin = list(ins) + list(cts) + list(adds)
    return pl.pallas_call(
        body, name=name, grid=grid,
        in_specs=[pl.BlockSpec(b, m) for _, b, m in allin],
        out_specs=[pl.BlockSpec(w["block"], w["imap"]) for w in wrt],
        out_shape=[jax.ShapeDtypeStruct(w["shape"], w["dtype"]) for w in wrt],
        compiler_params=_cp(("arbitrary",) * len(grid)),
    )(*[a for a, _, _ in allin])


def _rmsnorm_fn(x, w):
    return (x * lax.rsqrt(jnp.mean(x * x, axis=-1, keepdims=True) + EPS) * w,)


def _swiglu_fn(g, u):
    return (_silu(g) * u,)


def _qkprep_fn(t, w64, cos, sin, hmean, rot, tile64):
    w = dot_hi(jnp.broadcast_to(w64, (8, LANES)), tile64)
    w = jnp.sum(w, axis=0, keepdims=True) * 0.125
    y = t * lax.rsqrt(dot_hi(t * t, hmean) + EPS) * w
    return (y * cos + dot_hi(y, rot) * sin,)


def _att_fn(q, kp, kc, vp, vc, first):
    iq = lax.broadcasted_iota(jnp.int32, (BAND, 2 * BAND), 0)
    ik = lax.broadcasted_iota(jnp.int32, (BAND, 2 * BAND), 1)
    rel = BAND + iq - ik
    ok = (rel >= 0) & (rel <= BAND) & ((ik >= BAND) | jnp.logical_not(first))
    lane = lax.broadcasted_iota(jnp.int32, (1, LANES), 1)
    os_, ls_ = [], []
    for p in range(4):
        sl = slice(p * LANES, (p + 1) * LANES)
        qp = q[:, sl]
        kcat = jnp.concatenate([kp[:, sl], kc[:, sl]], axis=0)
        vcat = jnp.concatenate([vp[:, sl], vc[:, sl]], axis=0)
        o_pair = jnp.zeros((BAND, LANES), F32)
        l_pair = jnp.zeros((BAND, LANES), F32)
        for hh in range(2):
            lm = (lane // 64 == hh).astype(F32)
            s = dot_nt(qp * lm, kcat) * 0.125
            s = jnp.where(ok, s, NEG)
            mx = jnp.max(s, axis=-1, keepdims=True)
            e = jnp.exp(s - mx)
            den = jnp.sum(e, axis=-1, keepdims=True)
            o_pair = o_pair + dot_nn(e / den, vcat) * lm
            l_pair = l_pair + (mx + jnp.log(den)) * lm
        os_.append(o_pair)
        ls_.append(l_pair)
    return jnp.concatenate(os_, axis=1), jnp.concatenate(ls_, axis=1)


def _attmix_fn(o0, o1, o2, l0, l1, l2):
    m = jnp.maximum(jnp.maximum(l0, l1), l2)
    e0, e1, e2 = jnp.exp(l0 - m), jnp.exp(l1 - m), jnp.exp(l2 - m)
    return ((e0 * o0 + e1 * o1 + e2 * o2) / (e0 + e1 + e2),)


def _conv_fn(x, w0, w1, w2, w3, b):
    pre = x * w3 + shift_rows(x, 1) * w2 + shift_rows(x, 2) * w1 + shift_rows(x, 3) * w0 + b
    return (_silu(pre),)


def _ssdpre_fn(dtraw, bias, alog, ex):
    dt = _softplus(dtraw + bias)
    da = dt * (-jnp.exp(alog))
    return dot_hi(dt, ex), dot_hi(da, ex)


def _ssd_step(st, x, dtb, dab, bm, cm, ltri):
    cum = dot_hi(ltri, dab)
    cum_t = cum.T
    xdt = x * dtb
    cb = dot_nt(cm, bm)
    ri = lax.broadcasted_iota(jnp.int32, (CHUNK, CHUNK), 0)
    ci = lax.broadcasted_iota(jnp.int32, (CHUNK, CHUNK), 1)
    causal = ri >= ci
    lane = lax.broadcasted_iota(jnp.int32, (1, LANES), 1)
    rowi = lax.broadcasted_iota(jnp.int32, (LANES, 1), 0)
    ys = []
    for p in range(4):
        sl = slice(p * LANES, (p + 1) * LANES)
        cum_p, cum_tp, xdt_p = cum[:, sl], cum_t[sl, :], xdt[:, sl]
        acc = jnp.zeros((CHUNK, LANES), F32)
        for hh in range(2):
            col = jnp.sum(cum_p * (lane == 64 * hh).astype(F32), axis=1, keepdims=True)
            row = jnp.sum(cum_tp * (rowi == 64 * hh).astype(F32), axis=0, keepdims=True)
            dec = jnp.exp(jnp.where(causal, col - row, NEG))
            acc = acc + dot_nn(cb * dec, xdt_p * (lane // 64 == hh).astype(F32))
        ys.append(acc)
    y_diag = jnp.concatenate(ys, axis=1)
    y_off = dot_nn(cm, st) * jnp.exp(cum)
    last_row = (lax.broadcasted_iota(jnp.int32, (CHUNK, 1), 0) == CHUNK - 1).astype(F32)
    last = jnp.sum(cum * last_row, axis=0, keepdims=True)
    new_st = st * jnp.exp(last) + dot_tn(bm, xdt * jnp.exp(last - cum))
    return new_st, y_diag + y_off


def _ssdpost_fn(y, xs, z, dskip, ex, nw):
    db = jnp.sum(dot_hi(jnp.broadcast_to(dskip, (8, LANES)), ex), axis=0, keepdims=True) * 0.125
    y2 = (y + db * xs) * _silu(z)
    return (y2 * lax.rsqrt(jnp.mean(y2 * y2, axis=-1, keepdims=True) + EPS) * nw,)


def _merge_fn(ya, ys, ga, gs, ba, bs):
    return (_sigmoid(ga + ba) * ya + _sigmoid(gs + bs) * ys,)


TM = 512


def _full(shape):
    nd = len(shape)
    return (shape, lambda *_: (0,) * nd)


def _rmsnorm(name, x, w):
    t = x.shape[0]
    return _ew(name, _rmsnorm_fn, (t // TM,),
               [(x, (TM, D_MODEL), lambda i: (i, 0)), (w, (1, D_MODEL), lambda i: (0, 0))],
               [((t, D_MODEL), BF16, (TM, D_MODEL), lambda i: (i, 0))])[0]


def _rmsnorm_bwd(name, x, w, dh, dres):
    t = x.shape[0]
    row = ((TM, D_MODEL), lambda i: (i, 0))
    return _ew_bwd(name, _rmsnorm_fn, (t // TM,),
                   [(x, *row), (w, (1, D_MODEL), lambda i: (0, 0))], [(dh, *row)],
                   [dict(idx=0, kind="tile", shape=(t, D_MODEL), dtype=F32, block=row[0], imap=row[1], add=0),
                    dict(idx=1, kind="acc", shape=(1, D_MODEL), dtype=F32, block=(1, D_MODEL), imap=lambda i: (0, 0),
                         first=lambda: pl.program_id(0) == 0)],
                   adds=[(dres, *row)])


def _swiglu(name, g, u):
    t = g.shape[0]
    tc = 256
    blk = ((TM, tc), lambda j, i: (i, j))
    return _ew(name, _swiglu_fn, (D_FF // tc, t // TM), [(g, *blk), (u, *blk)], [((t, D_FF), BF16, *blk)])[0]


def _swiglu_bwd(name, g, u, da):
    t = g.shape[0]
    tc = 256
    blk = ((TM, tc), lambda j, i: (i, j))
    return _ew_bwd(name, _swiglu_fn, (D_FF // tc, t // TM), [(g, *blk), (u, *blk)], [(da, *blk)],
                   [dict(idx=0, kind="tile", shape=(t, D_FF), dtype=BF16, block=blk[0], imap=blk[1]),
                    dict(idx=1, kind="tile", shape=(t, D_FF), dtype=BF16, block=blk[0], imap=blk[1])])


def _qk_operands(proj, qkw, cos, sin, consts, tm):
    nrow = SEQ // tm
    c = ((LANES, LANES), lambda j, i: (0, 0))
    return [(proj, (tm, LANES), lambda j, i: (i, j)),
            (qkw, (None, 1, LANES), lambda j, i: (j // 12, 0, 0)),
            (cos, (tm, LANES), lambda j, i: (i % nrow, 0)),
            (sin, (tm, LANES), lambda j, i: (i % nrow, 0)),
            (consts["hmean"], *c), (consts["rot"], *c), (consts["tile64"], *c)]


def _qkprep(name, proj, qkw, cos, sin, consts):
    t = proj.shape[0]
    return _ew(name, _qkprep_fn, (2 * QKV // LANES, t // TM), _qk_operands(proj, qkw, cos, sin, consts, TM),
               [((t, 2 * QKV), BF16, (TM, LANES), lambda j, i: (i, j))])[0]


def _qkprep_bwd(name, proj, qkw, cos, sin, consts, dqs, dks):
    t = proj.shape[0]

    def pick(*c):
        a = pl.program_id(0) // 4
        out = c[5]
        for k in range(4, -1, -1):
            out = jnp.where(a == k, c[k], out)
        return (out,)

    return _ew_bwd(name, _qkprep_fn, (2 * QKV // LANES, t // TM), _qk_operands(proj, qkw, cos, sin, consts, TM),
                   [(d, (TM, LANES), lambda j, i: (i, j % 4)) for d in (*dqs, *dks)],
                   [dict(idx=0, kind="tile", shape=(t, 2 * QKV), dtype=BF16, block=(TM, LANES), imap=lambda j, i: (i, j)),
                    dict(idx=1, kind="acc", shape=(2, 1, LANES), dtype=F32, block=(None, 1, LANES),
                         imap=lambda j, i: (j // 12, 0, 0),
                         first=lambda: (pl.program_id(0) % 12 == 0) & (pl.program_id(1) == 0))],
                   ct_fn=pick)


def _att_specs(bl, dil, g):
    nb = SEQ // dil // BAND
    blk = (None, BAND, ATT_OUT)

    def cur(n):
        return jnp.minimum(n, nb - 1)

    def prev(n):
        return jnp.maximum(jnp.minimum(n, nb - 1) - 1, 0)

    return nb, [
        pl.BlockSpec(blk, lambda b, r, n: (b, cur(n), r * 6 + g)),
        pl.BlockSpec(blk, lambda b, r, n: (b, prev(n), r * 6 + 3 + g)),
        pl.BlockSpec(blk, lambda b, r, n: (b, cur(n), r * 6 + 3 + g)),
        pl.BlockSpec(blk, lambda b, r, n: (b, prev(n), r * (NP // ATT_OUT) + V0 // ATT_OUT + g)),
        pl.BlockSpec(blk, lambda b, r, n: (b, cur(n), r * (NP // ATT_OUT) + V0 // ATT_OUT + g)),
    ]


def _att_fwd(name, qk, proj, g):
    bl = qk.shape[0] // SEQ
    dil = ATT_DILATIONS[g]
    nb, specs = _att_specs(bl, dil, g)
    qkv_ = qk.reshape(bl, SEQ // dil, dil * 2 * QKV)
    pv = proj.reshape(bl, SEQ // dil, dil * NP)
    o_spec = pl.BlockSpec((None, BAND, ATT_OUT), lambda b, r, n: (b, n, r))

    def body(q, kp, kc, vp, vc, o_ref, l_ref):
        o, l = _att_fn(q[...], kp[...], kc[...], vp[...], vc[...], pl.program_id(2) == 0)
        o_ref[...] = o
        l_ref[...] = l

    o, l = pl.pallas_call(
        body, name=name, grid=(bl, dil, nb), in_specs=specs, out_specs=[o_spec, o_spec],
        out_shape=[jax.ShapeDtypeStruct((bl, SEQ // dil, dil * ATT_OUT), F32)] * 2,
        compiler_params=_cp(("arbitrary",) * 3),
    )(qkv_, qkv_, qkv_, pv, pv)
    return o.reshape(bl * SEQ, ATT_OUT), l.reshape(bl * SEQ, ATT_OUT)


def _att_bwd(name, qk, proj, g, do, dl):
    bl = qk.shape[0] // SEQ
    dil = ATT_DILATIONS[g]
    nb, specs = _att_specs(bl, dil, g)
    qkv_ = qk.reshape(bl, SEQ // dil, dil * 2 * QKV)
    pv = proj.reshape(bl, SEQ // dil, dil * NP)
    blk = (None, BAND, ATT_OUT)
    ct_spec = pl.BlockSpec(blk, lambda b, r, n: (b, jnp.minimum(n, nb - 1), r))
    dov = do.reshape(bl, SEQ // dil, dil * ATT_OUT)
    dlv = dl.reshape(bl, SEQ // dil, dil * ATT_OUT)
    def body(q, kp, kc, vp, vc, do_ref, dl_ref, d_ref, dk_ref, dv_ref, ck, cv):
        n = pl.program_id(2)

        @pl.when(n < nb)
        def _():
            first = n == 0
            prim = [r[...].astype(F32) for r in (q, kp, kc, vp, vc)]
            _, vjp = jax.vjp(lambda *a: _att_fn(*a, first), *prim)
            dq, dkp, dkc, dvp, dvc = vjp((do_ref[...], dl_ref[...]))
            d_ref[...] = dq

            @pl.when(n > 0)
            def _():
                dk_ref[...] = ck[...] + dkp
                dv_ref[...] = cv[...] + dvp

            ck[...] = dkc
            cv[...] = dvc

        @pl.when(n == nb)
        def _():
            dk_ref[...] = ck[...]
            dv_ref[...] = cv[...]

    one = jax.ShapeDtypeStruct((bl, SEQ // dil, dil * ATT_OUT), F32)
    o_specs = [
        pl.BlockSpec(blk, lambda b, r, n: (b, jnp.minimum(n, nb - 1), r)),
        pl.BlockSpec(blk, lambda b, r, n: (b, jnp.maximum(n - 1, 0), r)),
        pl.BlockSpec(blk, lambda b, r, n: (b, jnp.maximum(n - 1, 0), r)),
    ]
    dq, dk, dv = pl.pallas_call(
        body, name=name, grid=(bl, dil, nb + 1), in_specs=specs + [ct_spec, ct_spec], out_specs=o_specs,
        out_shape=[one, one, one],
        scratch_shapes=[pltpu.VMEM((BAND, ATT_OUT), F32), pltpu.VMEM((BAND, ATT_OUT), F32)],
        compiler_params=_cp(("arbitrary",) * 3),
    )(qkv_, qkv_, qkv_, pv, pv, dov, dlv)
    t = bl * SEQ
    return dq.reshape(t, ATT_OUT), dk.reshape(t, ATT_OUT), dv.reshape(t, ATT_OUT)


def _attmix(name, os_, ls_):
    t = os_[0].shape[0]
    blk = ((TM, ATT_OUT), lambda i: (i, 0))
    return _ew(name, _attmix_fn, (t // TM,), [(a, *blk) for a in (*os_, *ls_)], [((t, ATT_OUT), BF16, *blk)])[0]


def _attmix_bwd(name, os_, ls_, datt):
    t = os_[0].shape[0]
    blk = ((TM, ATT_OUT), lambda i: (i, 0))
    return _ew_bwd(name, _attmix_fn, (t // TM,), [(a, *blk) for a in (*os_, *ls_)], [(datt, *blk)],
                   [dict(idx=k, kind="tile", shape=(t, ATT_OUT), dtype=F32, block=blk[0], imap=blk[1]) for k in range(6)])


CONV_TC = 256


def _conv_operands(proj3, conv_w, conv_b):
    c0 = X0 // CONV_TC
    ins = [(proj3, (None, SEQ, CONV_TC), lambda j, b: (b, 0, c0 + j))]
    for k in range(4):
        ins.append((conv_w, (None, 1, CONV_TC), lambda j, b, k=k: (k, 0, j)))
    ins.append((conv_b, (1, CONV_TC), lambda j, b: (0, j)))
    return ins


def _conv(name, proj3, conv_w, conv_b):
    bl = proj3.shape[0]
    return _ew(name, _conv_fn, (XBC // CONV_TC, bl), _conv_operands(proj3, conv_w, conv_b),
               [((bl, SEQ, XBC), F32, (None, SEQ, CONV_TC), lambda j, b: (b, 0, j))])[0]


def _conv_bwd(name, proj3, conv_w, conv_b, dxs3, db3, dc3):
    bl = proj3.shape[0]
    nx = D_INNER // CONV_TC
    nb_ = N_SSM_GROUPS * D_STATE // CONV_TC
    blk = (None, SEQ, CONV_TC)
    cts = [(dxs3, blk, lambda j, b: (b, 0, jnp.minimum(j, nx - 1))),
           (db3, blk, lambda j, b: (b, 0, jnp.clip(j - nx, 0, nb_ - 1))),
           (dc3, blk, lambda j, b: (b, 0, jnp.clip(j - nx - nb_, 0, nb_ - 1)))]

    def pick(cx, cb, cc):
        j = pl.program_id(0)
        return (jnp.where(j < nx, cx, jnp.where(j < nx + nb_, cb, cc)),)

    first = lambda: pl.program_id(1) == 0
    wrt = [dict(idx=0, kind="tile", shape=(bl, SEQ, XBC), dtype=BF16, block=blk, imap=lambda j, b: (b, 0, j))]
    for k in range(4):
        wrt.append(dict(idx=1 + k, kind="acc", shape=(1, XBC), dtype=F32, block=(1, CONV_TC),
                        imap=lambda j, b: (0, j), first=first))
    wrt.append(dict(idx=5, kind="acc", shape=(1, XBC), dtype=F32, block=(1, CONV_TC), imap=lambda j, b: (0, j), first=first))
    return _ew_bwd(name, _conv_fn, (XBC // CONV_TC, bl), _conv_operands(proj3, conv_w, conv_b), cts, wrt, ct_fn=pick)


SSD_TM = 256


def _ssdpre_operands(proj, dt_bias, a_log, ex):
    return [(proj, (SSD_TM, DTW), lambda i: (i, DT0 // DTW)), (dt_bias, *_full((1, DTW))), (a_log, *_full((1, DTW))),
            (ex, *_full((LANES, D_INNER)))]


def _ssdpre(name, proj, dt_bias, a_log, ex):
    t = proj.shape[0]
    blk = ((SSD_TM, D_INNER), lambda i: (i, 0))
    return _ew(name, _ssdpre_fn, (t // SSD_TM,), _ssdpre_operands(proj, dt_bias, a_log, ex),
               [((t, D_INNER), F32, *blk), ((t, D_INNER), F32, *blk)])


def _ssdpre_bwd(name, proj, dt_bias, a_log, ex, ddtb, ddab):
    t = proj.shape[0]
    blk = ((SSD_TM, D_INNER), lambda i: (i, 0))
    first = lambda: pl.program_id(0) == 0
    return _ew_bwd(name, _ssdpre_fn, (t // SSD_TM,), _ssdpre_operands(proj, dt_bias, a_log, ex),
                   [(ddtb, *blk), (ddab, *blk)],
                   [dict(idx=0, kind="tile", shape=(t, DTW), dtype=BF16, block=(SSD_TM, DTW), imap=lambda i: (i, 0)),
                    dict(idx=1, kind="acc", shape=(1, DTW), dtype=F32, block=(1, DTW), imap=lambda i: (0, 0), first=first),
                    dict(idx=2, kind="acc", shape=(1, DTW), dtype=F32, block=(1, DTW), imap=lambda i: (0, 0), first=first)])


def _ssd_in_specs(rev):
    nc = SEQ // CHUNK

    def c_(c):
        return nc - 1 - c if rev else c

    wide = (None, CHUNK, 4 * LANES)
    nar = (None, CHUNK, D_STATE)
    xb = D_INNER // D_STATE
    return [
        pl.BlockSpec(wide, lambda b, g, c: (b, c_(c), g)),
        pl.BlockSpec(wide, lambda b, g, c: (b, c_(c), g)),
        pl.BlockSpec(wide, lambda b, g, c: (b, c_(c), g)),
        pl.BlockSpec(nar, lambda b, g, c: (b, c_(c), xb + g)),
        pl.BlockSpec(nar, lambda b, g, c: (b, c_(c), xb + N_SSM_GROUPS + g)),
        pl.BlockSpec((CHUNK, CHUNK), lambda b, g, c: (0, 0)),
    ], c_


def _ssd_fwd(name, xc3, dtb3, dab3, ltri):
    bl = xc3.shape[0]
    nc = SEQ // CHUNK
    specs, _ = _ssd_in_specs(False)

    def body(x, dtb, dab, bm, cm, lt, y_ref, st_ref, st):
        @pl.when(pl.program_id(2) == 0)
        def _():
            st[...] = jnp.zeros_like(st)

        s0 = st[...]
        st_ref[...] = s0
        new_st, y = _ssd_step(s0, x[...], dtb[...], dab[...], bm[...], cm[...], lt[...])
        y_ref[...] = y
        st[...] = new_st

    return pl.pallas_call(
        body, name=name, grid=(bl, N_SSM_GROUPS, nc), in_specs=specs,
        out_specs=[pl.BlockSpec((None, CHUNK, 4 * LANES), lambda b, g, c: (b, c, g)),
                   pl.BlockSpec((None, None, None, D_STATE, 4 * LANES), lambda b, g, c: (b, g, c, 0, 0))],
        out_shape=[jax.ShapeDtypeStruct((bl, SEQ, D_INNER), F32),
                   jax.ShapeDtypeStruct((bl, N_SSM_GROUPS, nc, D_STATE, 4 * LANES), F32)],
        scratch_shapes=[pltpu.VMEM((D_STATE, 4 * LANES), F32)],
        compiler_params=_cp(("arbitrary",) * 3),
    )(xc3, dtb3, dab3, xc3, xc3, ltri)


def _ssd_bwd(name, xc3, dtb3, dab3, ltri, states, dy3, dxs_part3):
    bl = xc3.shape[0]
    nc = SEQ // CHUNK
    specs, c_ = _ssd_in_specs(True)
    wide = pl.BlockSpec((None, CHUNK, 4 * LANES), lambda b, g, c: (b, c_(c), g))
    nar = pl.BlockSpec((None, CHUNK, D_STATE), lambda b, g, c: (b, c_(c), g))
    st_spec = pl.BlockSpec((None, None, None, D_STATE, 4 * LANES), lambda b, g, c: (b, g, c_(c), 0, 0))

    def body(x, dtb, dab, bm, cm, lt, st_ref, dy, dxp, dx_ref, ddtb_ref, ddab_ref, dbm_ref, dcm_ref, dst):
        @pl.when(pl.program_id(2) == 0)
        def _():
            dst[...] = jnp.zeros_like(dst)

        ltv = lt[...]
        _, vjp = jax.vjp(lambda *a: _ssd_step(*a, ltv), st_ref[...], x[...], dtb[...], dab[...], bm[...], cm[...])
        d_st, d_x, d_dtb, d_dab, d_bm, d_cm = vjp((dst[...], dy[...]))
        dst[...] = d_st
        dx_ref[...] = d_x + dxp[...]
        ddtb_ref[...] = d_dtb
        ddab_ref[...] = d_dab
        dbm_ref[...] = d_bm
        dcm_ref[...] = d_cm

    big = jax.ShapeDtypeStruct((bl, SEQ, D_INNER), F32)
    small = jax.ShapeDtypeStruct((bl, SEQ, N_SSM_GROUPS * D_STATE), F32)
    return pl.pallas_call(
        body, name=name, grid=(bl, N_SSM_GROUPS, nc), in_specs=specs + [st_spec, wide, wide],
        out_specs=[wide, wide, wide, nar, nar], out_shape=[big, big, big, small, small],
        scratch_shapes=[pltpu.VMEM((D_STATE, 4 * LANES), F32)],
        compiler_params=_cp(("arbitrary",) * 3),
    )(xc3, dtb3, dab3, xc3, xc3, ltri, states, dy3, dxs_part3)


def _ssdpost_operands(y, xc, proj, d_skip, ex, nw):
    w = 4 * LANES
    return [(y, (SSD_TM, w), lambda j, i: (i, j)), (xc, (SSD_TM, w), lambda j, i: (i, j)),
            (proj, (SSD_TM, w), lambda j, i: (i, Z0 // w + j)), (d_skip, (1, DTW), lambda j, i: (0, 0)),
            (ex, (LANES, w), lambda j, i: (0, j)), (nw, (1, w), lambda j, i: (0, j))]


def _ssdpost(name, y, xc, proj, d_skip, ex, nw):
    t = y.shape[0]
    w = 4 * LANES
    return _ew(name, _ssdpost_fn, (D_INNER // w, t // SSD_TM), _ssdpost_operands(y, xc, proj, d_skip, ex, nw),
               [((t, D_INNER), BF16, (SSD_TM, w), lambda j, i: (i, j))])[0]


def _ssdpost_bwd(name, y, xc, proj, d_skip, ex, nw, dysn):
    t = y.shape[0]
    w = 4 * LANES
    blk = ((SSD_TM, w), lambda j, i: (i, j))
    return _ew_bwd(name, _ssdpost_fn, (D_INNER // w, t // SSD_TM), _ssdpost_operands(y, xc, proj, d_skip, ex, nw),
                   [(dysn, *blk)],
                   [dict(idx=0, kind="tile", shape=(t, D_INNER), dtype=F32, block=blk[0], imap=blk[1]),
                    dict(idx=1, kind="tile", shape=(t, D_INNER), dtype=F32, block=blk[0], imap=blk[1]),
                    dict(idx=2, kind="tile", shape=(t, D_INNER), dtype=BF16, block=blk[0], imap=blk[1]),
                    dict(idx=3, kind="acc", shape=(1, DTW), dtype=F32, block=(1, DTW), imap=lambda j, i: (0, 0),
                         first=lambda: (pl.program_id(0) == 0) & (pl.program_id(1) == 0)),
                    dict(idx=5, kind="acc", shape=(1, D_INNER), dtype=F32, block=(1, w), imap=lambda j, i: (0, j),
                         first=lambda: pl.program_id(1) == 0)])


def _merge_operands(ya, ys, proj, b_gates):
    w = 4 * LANES
    g0 = G0 // w
    nh = D_MODEL // w
    return [(ya, (TM, w), lambda j, i: (i, j)), (ys, (TM, w), lambda j, i: (i, j)),
            (proj, (TM, w), lambda j, i: (i, g0 + j)), (proj, (TM, w), lambda j, i: (i, g0 + nh + j)),
            (b_gates, (1, w), lambda j, i: (0, j)), (b_gates, (1, w), lambda j, i: (0, nh + j))]


def _merge(name, ya, ys, proj, b_gates):
    t = ya.shape[0]
    w = 4 * LANES
    return _ew(name, _merge_fn, (D_MODEL // w, t // TM), _merge_operands(ya, ys, proj, b_gates),
               [((t, D_MODEL), BF16, (TM, w), lambda j, i: (i, j))])[0]


def _merge_bwd(name, ya, ys, proj, b_gates, dmixed):
    t = ya.shape[0]
    w = 4 * LANES
    blk = ((TM, w), lambda j, i: (i, j))
    first = lambda: pl.program_id(1) == 0
    tile = lambda k, dt: dict(idx=k, kind="tile", shape=(t, D_MODEL), dtype=dt, block=blk[0], imap=blk[1])
    acc = lambda k: dict(idx=k, kind="acc", shape=(1, D_MODEL), dtype=F32, block=(1, w), imap=lambda j, i: (0, j), first=first)
    return _ew_bwd(name, _merge_fn, (D_MODEL // w, t // TM), _merge_operands(ya, ys, proj, b_gates), [(dmixed, *blk)],
                   [tile(0, BF16), tile(1, BF16), tile(2, BF16), tile(3, BF16), acc(4), acc(5)])


def _loss(name, y, tgt):
    t = y.shape[0]
    blk = pl.BlockSpec((TM, D_MODEL), lambda i: (i, 0))

    def body(y_ref, t_ref, dy_ref, l_ref):
        e = y_ref[...] - t_ref[...]
        dy_ref[...] = e * (1.0 / D_MODEL)
        part = jnp.sum(jnp.sum(e * e, axis=-1, keepdims=True), axis=0, keepdims=True) * (0.5 / D_MODEL)
        part = jnp.broadcast_to(part, (8, LANES))

        @pl.when(pl.program_id(0) == 0)
        def _():
            l_ref[...] = part

        @pl.when(pl.program_id(0) > 0)
        def _():
            l_ref[...] += part

    return pl.pallas_call(
        body, name=name, grid=(t // TM,), in_specs=[blk, blk],
        out_specs=[blk, pl.BlockSpec((8, LANES), lambda i: (0, 0))],
        out_shape=[jax.ShapeDtypeStruct((t, D_MODEL), F32), jax.ShapeDtypeStruct((8, LANES), F32)],
        compiler_params=_cp(("arbitrary",)),
    )(y, tgt)


def _adamw_fn(w, g, m, v):
    m2 = B1 * m + (1.0 - B1) * g
    v2 = B2 * v + (1.0 - B2) * (g * g)
    m_hat = m2 / (1.0 - B1 ** STEP)
    v_hat = v2 / (1.0 - B2 ** STEP)
    return -LR * (m_hat / (jnp.sqrt(v_hat) + ADAM_EPS) + WD * w), m2, v2


def _adamw(name, w, g, m, v):
    rows, cols = w.shape
    tm = rows
    for cand in (512, 256, 128, 64, 32, 16, 8):
        if rows % cand == 0 and cand * cols * 4 <= (1 << 21):
            tm = cand
            break
    blk = ((tm, cols), lambda i: (i, 0))
    return _ew(name, _adamw_fn, (rows // tm,), [(a, *blk) for a in (w, g, m, v)], [((rows, cols), F32, *blk)] * 3)


def _ffn_fwd(tag, x, nw, wg, wu, wd):
    h = _rmsnorm(tag + "_norm", x, nw)
    g = _mm(tag + "_gate", h, wg, "nn", out_dtype=BF16)
    u = _mm(tag + "_up", h, wu, "nn", out_dtype=BF16)
    a = _swiglu(tag + "_act", g, u)
    y = _mm(tag + "_down", a, wd, "nn", alpha=0.5, res=x)
    return y, (x, h, g, u, a)


def _ffn_bwd(tag, saved, nw, wg, wu, wd, dy):
    x, h, g, u, a = saved
    da = _mm(tag + "_down_dx", dy, wd, "nt", alpha=0.5)
    dwd = _mm(tag + "_down_dw", a, dy, "tn", out_dtype=BF16, alpha=0.5)
    dg, du = _swiglu_bwd(tag + "_act_bwd", g, u, da)
    dwg = _mm(tag + "_gate_dw", h, dg, "tn", out_dtype=BF16)
    dwu = _mm(tag + "_up_dw", h, du, "tn", out_dtype=BF16)
    dh = _mm(tag + "_gate_dx", dg, wg, "nt")
    dh = _mm(tag + "_up_dx", du, wu, "nt", res=dh)
    dx, dnw = _rmsnorm_bwd(tag + "_norm_bwd", x, nw, dh, dy)
    return dx, (dnw, dwg, dwu, dwd)


def _mixer_fwd(tag, x, p, c):
    t = x.shape[0]
    bl = t // SEQ
    h = _rmsnorm(tag + "_norm", x, p["mix_norm_w"])
    proj = _mm(tag + "_in", h, p["w_in"], "nn")
    qk = _qkprep(tag + "_qk", proj, p["qkw"], c["cos"], c["sin"], c)
    os_, ls_ = [], []
    for g in range(3):
        o, l = _att_fwd(f"{tag}_att{g}", qk, proj, g)
        os_.append(o)
        ls_.append(l)
    att = _attmix(tag + "_attmix", os_, ls_)
    ya = _mm(tag + "_attproj", att, p["w_att_proj"], "nn")
    proj3 = proj.reshape(bl, SEQ, NP)
    xc3 = _conv(tag + "_conv", proj3, p["conv_w"], p["conv_b"])
    xc = xc3.reshape(t, XBC)
    dtb, dab = _ssdpre(tag + "_ssdpre", proj, p["dt_bias"], p["a_log"], c["ex"])
    dtb3, dab3 = dtb.reshape(bl, SEQ, D_INNER), dab.reshape(bl, SEQ, D_INNER)
    y3, states = _ssd_fwd(tag + "_ssd", xc3, dtb3, dab3, c["ltri"])
    y = y3.reshape(t, D_INNER)
    ysn = _ssdpost(tag + "_ssdpost", y, xc, proj, p["d_skip"], c["ex"], p["ssm_norm_w"])
    ys = _mm(tag + "_ssmproj", ysn, p["w_ssm_proj"], "nn")
    mixed = _merge(tag + "_merge", ya, ys, proj, p["b_gates"])
    out = _mm(tag + "_out", mixed, p["w_out"], "nn", res=x)
    return out, (x, h, proj, qk, os_, ls_, att, ya, xc3, dtb3, dab3, states, y, ysn, ys, mixed)


def _mixer_bwd(tag, saved, p, c, dout):
    x, h, proj, qk, os_, ls_, att, ya, xc3, dtb3, dab3, states, y, ysn, ys, mixed = saved
    t = x.shape[0]
    bl = t // SEQ
    xc = xc3.reshape(t, XBC)
    proj3 = proj.reshape(bl, SEQ, NP)
    gr = {}
    dmixed = _mm(tag + "_out_dx", dout, p["w_out"], "nt")
    gr["w_out"] = _mm(tag + "_out_dw", mixed, dout, "tn", out_dtype=BF16)
    dya, dys, dga, dgs, dba, dbs = _merge_bwd(tag + "_merge_bwd", ya, ys, proj, p["b_gates"], dmixed)
    gr["b_gates"] = jnp.concatenate([dba, dbs], axis=1)
    datt = _mm(tag + "_attproj_dx", dya, p["w_att_proj"], "nt")
    gr["w_att_proj"] = _mm(tag + "_attproj_dw", att, dya, "tn", out_dtype=BF16)
    dysn = _mm(tag + "_ssmproj_dx", dys, p["w_ssm_proj"], "nt")
    gr["w_ssm_proj"] = _mm(tag + "_ssmproj_dw", ysn, dys, "tn", out_dtype=BF16)
    dmix = _attmix_bwd(tag + "_attmix_bwd", os_, ls_, datt)
    dqs, dks, dvs = [], [], []
    for g in range(3):
        dq, dk, dv = _att_bwd(f"{tag}_att{g}_bwd", qk, proj, g, dmix[g], dmix[3 + g])
        dqs.append(dq)
        dks.append(dk)
        dvs.append(dv)
    dqk, gr["qkw"] = _qkprep_bwd(tag + "_qk_bwd", proj, p["qkw"], c["cos"], c["sin"], c, dqs, dks)
    dy, dxs_part, dz, gr["d_skip"], gr["ssm_norm_w"] = _ssdpost_bwd(
        tag + "_ssdpost_bwd", y, xc, proj, p["d_skip"], c["ex"], p["ssm_norm_w"], dysn)
    dxs3, ddtb3, ddab3, db3, dc3 = _ssd_bwd(
        tag + "_ssd_bwd", xc3, dtb3, dab3, c["ltri"], states, dy.reshape(bl, SEQ, D_INNER), dxs_part.reshape(bl, SEQ, D_INNER))
    ddt, gr["dt_bias"], gr["a_log"] = _ssdpre_bwd(
        tag + "_ssdpre_bwd", proj, p["dt_bias"], p["a_log"], c["ex"], ddtb3.reshape(t, D_INNER), ddab3.reshape(t, D_INNER))
    dxbc3, dcw0, dcw1, dcw2, dcw3, gr["conv_b"] = _conv_bwd(tag + "_conv_bwd", proj3, p["conv_w"], p["conv_b"], dxs3, db3, dc3)
    gr["conv_w"] = jnp.concatenate([dcw0, dcw1, dcw2, dcw3], axis=0)
    dproj = jnp.concatenate(
        [dqk] + [d.astype(BF16) for d in dvs] + [dz, dga, dgs, dxbc3.reshape(t, XBC), ddt,
                                                  jnp.zeros((t, NP - DT0 - DTW), BF16)], axis=1)
    gr["w_in"] = _mm(tag + "_in_dw", h, dproj, "tn", out_dtype=BF16)
    dh = _mm(tag + "_in_dx", dproj, p["w_in"], "nt")
    dx, gr["mix_norm_w"] = _rmsnorm_bwd(tag + "_norm_bwd", x, p["mix_norm_w"], dh, dout)
    return dx, gr


def _constants():
    cos, sin = _rope_tables()
    return dict(cos=cos, sin=sin, hmean=_head_mean_mat(), rot=_rope_rot_mat(), tile64=_tile64_mat(),
                ex=_head_expand_mat(), ltri=_ltri_mat())


FLAT_W = 1024
ANY = pl.BlockSpec(memory_space=pl.ANY)


def _mesh_pos():
    return lax.axis_index("x"), lax.axis_index("y"), lax.axis_index("c")


def _other_chips(x, y):
    return [(1 - x, y), (x, 1 - y), (1 - x, 1 - y)]


def _all_gather_shards(flat):
    r = flat.shape[0]
    h = r // 2

    def body(src, out, send_sems, recv_sems, local_sem):
        x, y, c = _mesh_pos()
        chips = _other_chips(x, y)

        def half(chip, hf):
            return out.at[2 * chip[0] + chip[1], pl.ds(hf * h, h), :]

        def copy(k, src_ref, dst_ref, to):
            return pltpu.make_async_remote_copy(src_ref=src_ref, dst_ref=dst_ref, send_sem=send_sems.at[k],
                                                recv_sem=recv_sems.at[k], device_id=to, device_id_type=MESH)

        mine = pltpu.make_async_copy(src, out.at[2 * x + y], local_sem)
        mine.start()
        first = [copy(j, src.at[pl.ds(c * h, h), :], half((x, y), c), (*chip, c)) for j, chip in enumerate(chips)]
        for cp in first:
            cp.start()
        passed = [copy(3 + j, half(chip, c), half(chip, c), (x, y, 1 - c)) for j, chip in enumerate(chips)]
        for j, chip in enumerate(chips):
            copy(j, half(chip, c), half(chip, c), (x, y, c)).wait_recv()
            passed[j].start()
        for j, chip in enumerate(chips):
            copy(3 + j, half(chip, 1 - c), half(chip, 1 - c), (x, y, c)).wait_recv()
        for cp in first + passed:
            cp.wait_send()
        mine.wait()

    return pl.pallas_call(
        body, name="all_gather_weights", out_shape=jax.ShapeDtypeStruct((4, r, FLAT_W), flat.dtype),
        in_specs=[ANY], out_specs=ANY,
        scratch_shapes=[pltpu.SemaphoreType.DMA((6,)), pltpu.SemaphoreType.DMA((6,)), pltpu.SemaphoreType.DMA],
    )(flat)


def _pair_exchange(g):
    r = g.shape[1]
    h = r // 2

    def body(src, out, send_sems, recv_sems):
        x, y, c = _mesh_pos()
        cps = [pltpu.make_async_remote_copy(src_ref=src.at[k, pl.ds((1 - c) * h, h), :], dst_ref=out.at[k],
                                            send_sem=send_sems.at[k], recv_sem=recv_sems.at[k],
                                            device_id=(x, y, 1 - c), device_id_type=MESH) for k in range(4)]
        for cp in cps:
            cp.start()
        for cp in cps:
            cp.wait()

    return pl.pallas_call(
        body, name="grad_pair_exchange", out_shape=jax.ShapeDtypeStruct((4, h, FLAT_W), g.dtype),
        in_specs=[ANY], out_specs=ANY,
        scratch_shapes=[pltpu.SemaphoreType.DMA((4,)), pltpu.SemaphoreType.DMA((4,))],
    )(g)


def _chip_exchange(ha):
    h = ha.shape[1]

    def body(src, out, send_sems, recv_sems):
        x, y, c = _mesh_pos()
        cps = [pltpu.make_async_remote_copy(src_ref=src.at[2 * chip[0] + chip[1]], dst_ref=out.at[j],
                                            send_sem=send_sems.at[j], recv_sem=recv_sems.at[j],
                                            device_id=(*chip, c), device_id_type=MESH)
               for j, chip in enumerate(_other_chips(x, y))]
        for cp in cps:
            cp.start()
        for cp in cps:
            cp.wait()

    return pl.pallas_call(
        body, name="grad_chip_exchange", out_shape=jax.ShapeDtypeStruct((3, h, FLAT_W), ha.dtype),
        in_specs=[ANY], out_specs=ANY,
        scratch_shapes=[pltpu.SemaphoreType.DMA((3,)), pltpu.SemaphoreType.DMA((3,))],
    )(ha)


def _pair_share(red):
    h = red.shape[0]

    def body(src, out, send_sem, recv_sem, local_sem):
        x, y, c = _mesh_pos()
        mine = pltpu.make_async_copy(src, out.at[pl.ds(c * h, h), :], local_sem)
        mine.start()
        cp = pltpu.make_async_remote_copy(src_ref=src, dst_ref=out.at[pl.ds(c * h, h), :], send_sem=send_sem,
                                          recv_sem=recv_sem, device_id=(x, y, 1 - c), device_id_type=MESH)
        cp.start()
        pltpu.make_async_remote_copy(src_ref=src, dst_ref=out.at[pl.ds((1 - c) * h, h), :], send_sem=send_sem,
                                     recv_sem=recv_sem, device_id=(x, y, c), device_id_type=MESH).wait_recv()
        cp.wait_send()
        mine.wait()

    return pl.pallas_call(
        body, name="grad_pair_share", out_shape=jax.ShapeDtypeStruct((2 * h, FLAT_W), red.dtype),
        in_specs=[ANY], out_specs=ANY,
        scratch_shapes=[pltpu.SemaphoreType.DMA, pltpu.SemaphoreType.DMA, pltpu.SemaphoreType.DMA],
    )(red)


ADD_TR = 512


def _pair_sum(g, recv, c_idx):
    h = recv.shape[1]
    g4 = g.reshape(4, 2, h, FLAT_W)

    def body(c_ref, a_ref, b_ref, o_ref):
        o_ref[...] = (a_ref[...].astype(F32) + b_ref[...].astype(F32)).astype(o_ref.dtype)

    return pl.pallas_call(
        body, name="grad_pair_sum",
        grid_spec=pltpu.PrefetchScalarGridSpec(
            num_scalar_prefetch=1, grid=(4, h // ADD_TR),
            in_specs=[pl.BlockSpec((None, None, ADD_TR, FLAT_W), lambda k, i, c: (k, c[0], i, 0)),
                      pl.BlockSpec((None, ADD_TR, FLAT_W), lambda k, i, c: (k, i, 0))],
            out_specs=pl.BlockSpec((None, ADD_TR, FLAT_W), lambda k, i, c: (k, i, 0))),
        out_shape=jax.ShapeDtypeStruct((4, h, FLAT_W), BF16),
        compiler_params=_cp(("arbitrary", "arbitrary")),
    )(c_idx, g4, recv)


def _chip_sum(ha, recv, chip_idx):
    h = ha.shape[1]

    def body(k_ref, a_ref, r0, r1, r2, o_ref):
        o_ref[...] = ((a_ref[...].astype(F32) + r0[...].astype(F32)) + r1[...].astype(F32)) + r2[...].astype(F32)

    blk = (None, ADD_TR, FLAT_W)
    return pl.pallas_call(
        body, name="grad_chip_sum",
        grid_spec=pltpu.PrefetchScalarGridSpec(
            num_scalar_prefetch=1, grid=(h // ADD_TR,),
            in_specs=[pl.BlockSpec(blk, lambda i, k: (k[0], i, 0))] +
                     [pl.BlockSpec(blk, lambda i, k, j=j: (j, i, 0)) for j in range(3)],
            out_specs=pl.BlockSpec((ADD_TR, FLAT_W), lambda i, k: (i, 0))),
        out_shape=jax.ShapeDtypeStruct((h, FLAT_W), F32),
        compiler_params=_cp(("arbitrary",)),
    )(chip_idx, ha, recv, recv, recv)


def _all_sum_small(vec):
    rows = vec.shape[0]

    def body(v_ref, o_ref, buf, send_sems, recv_sems):
        x, y, c = _mesh_pos()
        me, sibling = (x, y, c), (x, y, 1 - c)
        chips = _other_chips(x, y)

        def slot(p):
            return buf.at[4 * p[0] + 2 * p[1] + p[2]]

        def copy(k, block, to, src=None):
            return pltpu.make_async_remote_copy(src_ref=slot(block) if src is None else src, dst_ref=slot(block),
                                                send_sem=send_sems.at[k], recv_sem=recv_sems.at[k],
                                                device_id=to, device_id_type=MESH)

        first = [copy(0, me, sibling, src=v_ref)]
        first += [copy(1 + j, me, (*chip, c), src=v_ref) for j, chip in enumerate(chips)]
        for cp in first:
            cp.start()
        passed = [copy(4 + j, (*chip, c), sibling) for j, chip in enumerate(chips)]
        for j, chip in enumerate(chips):
            copy(1 + j, (*chip, c), me).wait_recv()
            passed[j].start()
        copy(0, sibling, me).wait_recv()
        for j, chip in enumerate(chips):
            copy(4 + j, (*chip, 1 - c), me).wait_recv()
        for cp in first + passed:
            cp.wait_send()
        slot(me)[...] = v_ref[...]
        acc = buf[0]
        for k in range(1, 8):
            acc = acc + buf[k]
        o_ref[...] = acc

    vm = pl.BlockSpec(memory_space=pltpu.VMEM)
    return pl.pallas_call(
        body, name="small_all_sum", out_shape=jax.ShapeDtypeStruct((rows, LANES), F32),
        in_specs=[vm], out_specs=vm,
        scratch_shapes=[pltpu.VMEM((8, rows, LANES), F32), pltpu.SemaphoreType.DMA((7,)), pltpu.SemaphoreType.DMA((7,))],
    )(vec)


def _pad_lanes(v, n=LANES):
    return jnp.pad(v, (0, n - v.shape[0]))[None, :]


def _w_in_to_kernel(w):
    return jnp.concatenate([w[:, :6656], w[:, 9760:N_IN], w[:, 6656:9728], w[:, 9728:9760],
                            jnp.zeros((w.shape[0], NP - N_IN), w.dtype)], axis=1)


def _w_in_from_kernel(w):
    return jnp.concatenate([w[:, :6656], w[:, X0:DT0], w[:, DT0:DT0 + 32], w[:, G0:X0]], axis=1)


def _layer_params(big, small, i):
    p = {k: big[k][i] for k in ("ffn1_w_gate", "ffn1_w_up", "ffn1_w_down", "w_att_proj", "w_ssm_proj", "w_out",
                                 "ffn2_w_gate", "ffn2_w_up", "ffn2_w_down")}
    p["w_in"] = _w_in_to_kernel(big["w_in"][i])
    p["conv_w"] = big["conv_w"][i][:, None, :]
    for k in ("ffn1_norm_w", "mix_norm_w", "ffn2_norm_w", "b_gates", "conv_b", "ssm_norm_w"):
        p[k] = small[k][i][None, :]
    for k in ("dt_bias", "a_log", "d_skip"):
        p[k] = _pad_lanes(small[k][i])
    p["qkw"] = jnp.stack([_pad_lanes(small["q_norm_w"][i]), _pad_lanes(small["k_norm_w"][i])])
    return p


def _local_step(x, tgt, layers, c):
    saved = []
    for i, p in enumerate(layers):
        x, s1 = _ffn_fwd(f"L{i}_ffn1", x, p["ffn1_norm_w"], p["ffn1_w_gate"], p["ffn1_w_up"], p["ffn1_w_down"])
        x, s2 = _mixer_fwd(f"L{i}_mix", x, p, c)
        x, s3 = _ffn_fwd(f"L{i}_ffn2", x, p["ffn2_norm_w"], p["ffn2_w_gate"], p["ffn2_w_up"], p["ffn2_w_down"])
        saved.append((s1, s2, s3))
    dx, loss_blk = _loss("loss", x, tgt)
    grads = [None] * len(layers)
    for i in reversed(range(len(layers))):
        p = layers[i]
        s1, s2, s3 = saved[i]
        dx, (dn2, dg2, du2, dd2) = _ffn_bwd(f"L{i}_ffn2", s3, p["ffn2_norm_w"], p["ffn2_w_gate"], p["ffn2_w_up"], p["ffn2_w_down"], dx)
        dx, gr = _mixer_bwd(f"L{i}_mix", s2, p, c, dx)
        dx, (dn1, dg1, du1, dd1) = _ffn_bwd(f"L{i}_ffn1", s1, p["ffn1_norm_w"], p["ffn1_w_gate"], p["ffn1_w_up"], p["ffn1_w_down"], dx)
        gr.update(ffn1_norm_w=dn1, ffn1_w_gate=dg1, ffn1_w_up=du1, ffn1_w_down=dd1,
                  ffn2_norm_w=dn2, ffn2_w_gate=dg2, ffn2_w_up=du2, ffn2_w_down=dd2)
        grads[i] = gr
    return loss_blk, dx, grads


WEIGHTS = ["ffn1_norm_w", "ffn1_w_gate", "ffn1_w_up", "ffn1_w_down", "mix_norm_w", "w_in", "b_gates", "q_norm_w",
           "k_norm_w", "conv_w", "conv_b", "dt_bias", "a_log", "d_skip", "ssm_norm_w", "w_att_proj", "w_ssm_proj",
           "w_out", "ffn2_norm_w", "ffn2_w_gate", "ffn2_w_up", "ffn2_w_down"]
SHARD_AXIS = {"ffn1_w_gate": 2, "ffn1_w_up": 2, "ffn1_w_down": 1, "w_in": 2, "conv_w": 2, "w_att_proj": 2,
              "w_ssm_proj": 1, "w_out": 1, "ffn2_w_gate": 2, "ffn2_w_up": 2, "ffn2_w_down": 1}
BIG = [n for n in WEIGHTS if n in SHARD_AXIS]
SMALL = [n for n in WEIGHTS if n not in SHARD_AXIS]
FLAT_ROWS = 32768
SMALL_ROWS = 328


def _to_flat(parts, dtype):
    v = jnp.concatenate([p.astype(dtype).reshape(-1) for p in parts])
    return jnp.pad(v, (0, FLAT_ROWS * FLAT_W - v.shape[0])).reshape(FLAT_ROWS, FLAT_W)


def _from_flat(flat, shapes):
    v = flat.reshape(-1)
    out, off = [], 0
    for s in shapes:
        n = math.prod(s)
        out.append(v[off:off + n].reshape(s))
        off += n
    return out


def _pack_small(parts):
    v = jnp.concatenate([p.astype(F32).reshape(-1) for p in parts])
    return jnp.pad(v, (0, SMALL_ROWS * LANES - v.shape[0])).reshape(SMALL_ROWS, LANES)


def _bf16_split(w):
    hi = w.astype(BF16)
    return hi, (w - hi.astype(F32)).astype(BF16)


def kernel(x, ffn1_norm_w, ffn1_w_gate, ffn1_w_up, ffn1_w_down, mix_norm_w, w_in, b_gates, q_norm_w, k_norm_w, conv_w, conv_b, dt_bias, a_log, d_skip, ssm_norm_w, w_att_proj, w_ssm_proj, w_out, ffn2_norm_w, ffn2_w_gate, ffn2_w_up, ffn2_w_down, loss_target, m_ffn1_norm_w, m_ffn1_w_gate, m_ffn1_w_up, m_ffn1_w_down, m_mix_norm_w, m_w_in, m_b_gates, m_q_norm_w, m_k_norm_w, m_conv_w, m_conv_b, m_dt_bias, m_a_log, m_d_skip, m_ssm_norm_w, m_w_att_proj, m_w_ssm_proj, m_w_out, m_ffn2_norm_w, m_ffn2_w_gate, m_ffn2_w_up, m_ffn2_w_down, v_ffn1_norm_w, v_ffn1_w_gate, v_ffn1_w_up, v_ffn1_w_down, v_mix_norm_w, v_w_in, v_b_gates, v_q_norm_w, v_k_norm_w, v_conv_w, v_conv_b, v_dt_bias, v_a_log, v_d_skip, v_ssm_norm_w, v_w_att_proj, v_w_ssm_proj, v_w_out, v_ffn2_norm_w, v_ffn2_w_gate, v_ffn2_w_up, v_ffn2_w_down):
    w = dict(zip(WEIGHTS, (ffn1_norm_w, ffn1_w_gate, ffn1_w_up, ffn1_w_down, mix_norm_w, w_in, b_gates, q_norm_w, k_norm_w, conv_w, conv_b, dt_bias, a_log, d_skip, ssm_norm_w, w_att_proj, w_ssm_proj, w_out, ffn2_norm_w, ffn2_w_gate, ffn2_w_up, ffn2_w_down)))
    m = dict(zip(WEIGHTS, (m_ffn1_norm_w, m_ffn1_w_gate, m_ffn1_w_up, m_ffn1_w_down, m_mix_norm_w, m_w_in, m_b_gates, m_q_norm_w, m_k_norm_w, m_conv_w, m_conv_b, m_dt_bias, m_a_log, m_d_skip, m_ssm_norm_w, m_w_att_proj, m_w_ssm_proj, m_w_out, m_ffn2_norm_w, m_ffn2_w_gate, m_ffn2_w_up, m_ffn2_w_down)))
    v = dict(zip(WEIGHTS, (v_ffn1_norm_w, v_ffn1_w_gate, v_ffn1_w_up, v_ffn1_w_down, v_mix_norm_w, v_w_in, v_b_gates, v_q_norm_w, v_k_norm_w, v_conv_w, v_conv_b, v_dt_bias, v_a_log, v_d_skip, v_ssm_norm_w, v_w_att_proj, v_w_ssm_proj, v_w_out, v_ffn2_norm_w, v_ffn2_w_gate, v_ffn2_w_up, v_ffn2_w_down)))
    depth = ffn1_norm_w.shape[0]
    bl = x.shape[0]
    t = bl * SEQ
    mx, my, mc = lax.axis_index("x"), lax.axis_index("y"), lax.axis_index("c")
    c_idx = mc.astype(jnp.int32).reshape(1)
    chip_idx = (2 * mx + my).astype(jnp.int32).reshape(1)

    conv_hi, conv_lo = _bf16_split(conv_w)
    shard_shapes = [w[n].shape for n in BIG] + [conv_w.shape]
    gathered = _all_gather_shards(_to_flat([w[n] if n != "conv_w" else conv_hi for n in BIG] + [conv_lo], BF16))
    pieces = [_from_flat(gathered[k], shard_shapes) for k in range(4)]
    big = {}
    for j, n in enumerate(BIG):
        big[n] = jnp.concatenate([pieces[k][j] for k in range(4)], axis=SHARD_AXIS[n])
    lo = jnp.concatenate([pieces[k][len(BIG)] for k in range(4)], axis=2)
    big["conv_w"] = big["conv_w"].astype(F32) + lo.astype(F32)
    small = {n: w[n] for n in SMALL}

    c = _constants()
    layers = [_layer_params(big, small, i) for i in range(depth)]
    loss_blk, dx, grads = _local_step(x.reshape(t, D_MODEL), loss_target.reshape(t, D_MODEL), layers, c)
    grad_x = dx.reshape(bl, SEQ, D_MODEL)

    def stacked(n):
        if n == "w_in":
            return jnp.stack([_w_in_from_kernel(g["w_in"]) for g in grads])
        return jnp.stack([g[n] for g in grads])

    full = {n: stacked(n) for n in BIG}
    widths = {n: w[n].shape[SHARD_AXIS[n]] for n in BIG}
    by_shard = jnp.stack([
        _to_flat([lax.slice_in_dim(full[n], k * widths[n], (k + 1) * widths[n], axis=SHARD_AXIS[n]) for n in BIG], BF16)
        for k in range(4)])
    from_sibling = _pair_exchange(by_shard)
    pair = _pair_sum(by_shard, from_sibling, c_idx)
    from_chips = _chip_exchange(pair)
    reduced_half = _chip_sum(pair, from_chips, chip_idx)
    reduced = _pair_share(reduced_half)
    g_big = dict(zip(BIG, _from_flat(reduced, [w[n].shape for n in BIG])))

    def small_grad(n):
        if n == "q_norm_w":
            return jnp.stack([g["qkw"][0, 0, :64] for g in grads])
        if n == "k_norm_w":
            return jnp.stack([g["qkw"][1, 0, :64] for g in grads])
        return jnp.stack([g[n][0, :w[n].shape[1]] for g in grads])

    small_shapes = [w[n].shape for n in SMALL]
    tot = _all_sum_small(_pack_small([small_grad(n) for n in SMALL] + [loss_blk[0, :1]]))
    unpacked = _from_flat(tot, small_shapes + [(1,)])
    g_small = dict(zip(SMALL, unpacked[:-1]))
    loss = unpacked[-1][0]

    grad, delta, new_m, new_v = {}, {}, {}, {}
    for n in BIG:
        shp = w[n].shape
        two_d = (shp[0] * shp[1], shp[2])
        d_, m_, v_ = _adamw("adamw_" + n, w[n].reshape(two_d), g_big[n].reshape(two_d), m[n].reshape(two_d), v[n].reshape(two_d))
        grad[n], delta[n], new_m[n], new_v[n] = g_big[n], d_.reshape(shp), m_.reshape(shp), v_.reshape(shp)
    d_, m_, v_ = _adamw("adamw_small", _pack_small([w[n] for n in SMALL]), _pack_small([g_small[n] for n in SMALL]),
                        _pack_small([m[n] for n in SMALL]), _pack_small([v[n] for n in SMALL]))
    for n, a, b, c_ in zip(SMALL, _from_flat(d_, small_shapes), _from_flat(m_, small_shapes), _from_flat(v_, small_shapes)):
        grad[n], delta[n], new_m[n], new_v[n] = g_small[n], a, b, c_
    return (loss, grad_x, *[grad[n] for n in WEIGHTS], *[delta[n] for n in WEIGHTS],
            *[new_m[n] for n in WEIGHTS], *[new_v[n] for n in WEIGHTS])
```

```python
import functools
import math

import numpy as np
import jax
import jax.numpy as jnp
from jax import lax
from jax.experimental import pallas as pl
from jax.experimental.pallas import tpu as pltpu

F32 = jnp.float32
BF16 = jnp.bfloat16
HI = lax.Precision.HIGHEST
MESH = pl.DeviceIdType.MESH

D_MODEL = 1024
SEQ = 2048
DEPTH = 4
D_FF = 2816
ATT_DILATIONS = (1, 4, 16)
BAND = 128
ATT_OUT = 512
QKV = 1536
D_INNER = 2048
N_SSM_HEADS = 32
N_SSM_GROUPS = 4
D_STATE = 128
XBC = 3072
CHUNK = 128
N_IN = 11808
EPS = 1e-6
ROPE_THETA = 10000.0
NP = 12288
Q0, K0, V0, Z0, G0, X0, DT0 = 0, 1536, 3072, 4608, 6656, 8704, 11776
DTW = 128
LR, B1, B2, ADAM_EPS, WD, STEP = 0.001, 0.9, 0.999, 1e-08, 0.01, 10

LANES = 128
VMEM_LIMIT = 48 * 1024 * 1024
NEG = -1e30


def _cp(sem=None, **kw):
    return pltpu.CompilerParams(dimension_semantics=sem, vmem_limit_bytes=VMEM_LIMIT, **kw)


def _dg(a, b, ca, cb):
    return lax.dot_general(a.astype(BF16), b.astype(BF16), (((ca,), (cb,)), ((), ())), preferred_element_type=F32)


@jax.custom_vjp
def dot_nn(a, b):
    return _dg(a, b, 1, 0)


def _dot_nn_fwd(a, b):
    return _dg(a, b, 1, 0), (a, b)


def _dot_nn_bwd(r, g):
    a, b = r
    return _dg(g, b, 1, 1).astype(a.dtype), _dg(a, g, 0, 0).astype(b.dtype)


dot_nn.defvjp(_dot_nn_fwd, _dot_nn_bwd)


@jax.custom_vjp
def dot_nt(a, b):
    return _dg(a, b, 1, 1)


def _dot_nt_fwd(a, b):
    return _dg(a, b, 1, 1), (a, b)


def _dot_nt_bwd(r, g):
    a, b = r
    return _dg(g, b, 1, 0).astype(a.dtype), _dg(g, a, 0, 0).astype(b.dtype)


dot_nt.defvjp(_dot_nt_fwd, _dot_nt_bwd)


@jax.custom_vjp
def dot_tn(a, b):
    return _dg(a, b, 0, 0)


def _dot_tn_fwd(a, b):
    return _dg(a, b, 0, 0), (a, b)


def _dot_tn_bwd(r, g):
    a, b = r
    return _dg(b, g, 1, 1).astype(a.dtype), _dg(a, g, 1, 0).astype(b.dtype)


dot_tn.defvjp(_dot_tn_fwd, _dot_tn_bwd)


def dot_hi(a, b):
    return jnp.dot(a, b, precision=HI, preferred_element_type=F32)


def _dot2_raw(a, e, ce):
    hi = a.astype(BF16)
    lo = (a - hi.astype(F32)).astype(BF16)
    return _dg(hi, e, 1, ce) + _dg(lo, e, 1, ce)


@jax.custom_vjp
def dot2(a, e):
    return _dot2_raw(a, e, 0)


def _dot2_fwd(a, e):
    return _dot2_raw(a, e, 0), e


def _dot2_bwd(e, g):
    return _dot2_raw(g, e, 1), jnp.zeros_like(e)


dot2.defvjp(_dot2_fwd, _dot2_bwd)


def _rope_rot_raw(y, sign):
    lane = lax.broadcasted_iota(jnp.int32, y.shape, 1)
    first_half = (lane & 32) == 0
    return sign * jnp.where(first_half, -pltpu.roll(y, LANES - 32, 1), pltpu.roll(y, 32, 1))


@jax.custom_vjp
def rope_rot(y):
    return _rope_rot_raw(y, 1.0)


def _rope_rot_fwd(y):
    return _rope_rot_raw(y, 1.0), None


def _rope_rot_bwd(_, g):
    return (_rope_rot_raw(g, -1.0),)


rope_rot.defvjp(_rope_rot_fwd, _rope_rot_bwd)


def _shift_rows_raw(x, s):
    n = x.shape[0]
    r = pltpu.roll(x, s % n, 0)
    rows = lax.broadcasted_iota(jnp.int32, x.shape, 0)
    keep = rows >= s if s > 0 else rows < n + s
    return jnp.where(keep, r, 0.0)


@functools.partial(jax.custom_vjp, nondiff_argnums=(1,))
def shift_rows(x, s):
    return _shift_rows_raw(x, s)


def _shift_fwd(x, s):
    return _shift_rows_raw(x, s), None


def _shift_bwd(s, _, g):
    return (_shift_rows_raw(g, -s),)


shift_rows.defvjp(_shift_fwd, _shift_bwd)


def _sigmoid(x):
    return 1.0 / (1.0 + jnp.exp(-x))


def _silu(x):
    return x * _sigmoid(x)


def _softplus(x):
    return jnp.maximum(x, 0.0) + jnp.log(1.0 + jnp.exp(-jnp.abs(x)))


def _head_mean_mat():
    i = np.arange(LANES)
    return jnp.asarray((i[:, None] // 64 == i[None, :] // 64).astype(np.float32) / 64.0)


def _tile64_mat():
    t = np.zeros((LANES, LANES), np.float32)
    for l in range(LANES):
        t[l % 64, l] = 1.0
    return jnp.asarray(t)


def _head_expand_mat():
    e = np.zeros((LANES, D_INNER), np.float32)
    for l in range(D_INNER):
        e[l // 64, l] = 1.0
    return jnp.asarray(e)


def _ltri_mat():
    i = np.arange(CHUNK)
    return jnp.asarray((i[:, None] >= i[None, :]).astype(np.float32))


def _rope_tables():
    pos = jnp.arange(SEQ, dtype=F32)
    inv_freq = 1.0 / (ROPE_THETA ** (jnp.arange(0, 64, 2, dtype=F32) / 64))
    ang = pos[:, None] * inv_freq[None, :]
    return jnp.tile(jnp.cos(ang), (1, 4)), jnp.tile(jnp.sin(ang), (1, 4))


def _pick(n, cap):
    best = None
    for t in range(LANES, min(n, cap) + 1, LANES):
        if n % t == 0:
            best = t
    return best if best is not None else n


def _mm(name, a, b, mode, out_dtype=F32, alpha=None, res=None):
    if mode == "nn":
        (m, k), n = a.shape, b.shape[1]
    elif mode == "nt":
        (m, k), n = a.shape, b.shape[0]
    else:
        (k, m), n = a.shape, b.shape[1]
    tm, tn, tk = _pick(m, 1408), _pick(n, 1408), _pick(k, 512)
    nk = k // tk
    ca, cb = {"nn": (1, 0), "nt": (1, 1), "tn": (0, 0)}[mode]
    a_spec = pl.BlockSpec((tk, tm), lambda i, j, kk: (kk, i)) if mode == "tn" else pl.BlockSpec((tm, tk), lambda i, j, kk: (i, kk))
    b_spec = pl.BlockSpec((tn, tk), lambda i, j, kk: (j, kk)) if mode == "nt" else pl.BlockSpec((tk, tn), lambda i, j, kk: (kk, j))
    o_spec = pl.BlockSpec((tm, tn), lambda i, j, kk: (i, j))
    has_res = res is not None

    def finish(acc, res_ref, o_ref):
        if alpha is not None:
            acc = acc * alpha
        if has_res:
            acc = acc + res_ref[...].astype(F32)
        o_ref[...] = acc.astype(o_ref.dtype)

    def body(*refs):
        a_ref, b_ref = refs[0], refs[1]
        res_ref = refs[2] if has_res else None
        o_ref = refs[3] if has_res else refs[2]
        part = _dg(a_ref[...], b_ref[...], ca, cb)
        if nk == 1:
            finish(part, res_ref, o_ref)
            return
        acc_ref = refs[-1]
        kk = pl.program_id(2)

        @pl.when(kk == 0)
        def _():
            acc_ref[...] = part

        @pl.when(kk > 0)
        def _():
            acc_ref[...] += part

        @pl.when(kk == nk - 1)
        def _():
            finish(acc_ref[...], res_ref, o_ref)

    ins = [a, b] + ([res] if has_res else [])
    in_specs = [a_spec, b_spec] + ([o_spec] if has_res else [])
    return pl.pallas_call(
        body, name=name, grid=(m // tm, n // tn, nk), in_specs=in_specs, out_specs=o_spec,
        out_shape=jax.ShapeDtypeStruct((m, n), out_dtype),
        scratch_shapes=[pltpu.VMEM((tm, tn), F32)] if nk > 1 else [],
        compiler_params=_cp(("parallel", "parallel", "arbitrary")),
    )(*ins)


def _ew(name, fn, grid, ins, outs, scratch=()):
    n_in, n_out = len(ins), len(outs)

    def body(*refs):
        vals = [r[...] for r in refs[:n_in]]
        res = fn(*vals, *refs[n_in + n_out:])
        for r, v in zip(refs[n_in:n_in + n_out], res):
            r[...] = v.astype(r.dtype)

    res = pl.pallas_call(
        body, name=name, grid=grid,
        in_specs=[pl.BlockSpec(b, m) for _, b, m in ins],
        out_specs=[pl.BlockSpec(b, m) for _, _, b, m in outs],
        out_shape=[jax.ShapeDtypeStruct(s, d) for s, d, _, _ in outs],
        scratch_shapes=list(scratch),
        compiler_params=_cp(("arbitrary",) * len(grid)),
    )(*[a for a, _, _ in ins])
    return res


def _ew_bwd(name, fn, grid, ins, cts, wrt, adds=(), ct_fn=None):
    n_in, n_ct, n_add = len(ins), len(cts), len(adds)
    idxs = [w["idx"] for w in wrt]

    def body(*refs):
        prim = [r[...] for r in refs[:n_in]]
        ct = [r[...].astype(F32) for r in refs[n_in:n_in + n_ct]]
        addv = [r[...] for r in refs[n_in + n_ct:n_in + n_ct + n_add]]
        orefs = refs[n_in + n_ct + n_add:]

        def f(*sel):
            full = list(prim)
            for i, s in zip(idxs, sel):
                full[i] = s
            return fn(*full)

        _, vjp = jax.vjp(f, *[prim[i].astype(F32) for i in idxs])
        grads = vjp(tuple(ct) if ct_fn is None else ct_fn(*ct))
        for w, g, r in zip(wrt, grads, orefs):
            if w["kind"] == "tile":
                if w.get("add") is not None:
                    g = g + addv[w["add"]].astype(F32)
                r[...] = g.astype(r.dtype)
            else:
                first = w["first"]()

                @pl.when(first)
                def _(r=r, g=g):
                    r[...] = g.astype(r.dtype)

                @pl.when(jnp.logical_not(first))
                def _(r=r, g=g):
                    r[...] += g.astype(r.dtype)

    allin = list(ins) + list(cts) + list(adds)
    return pl.pallas_call(
        body, name=name, grid=grid,
        in_specs=[pl.BlockSpec(b, m) for _, b, m in allin],
        out_specs=[pl.BlockSpec(w["block"], w["imap"]) for w in wrt],
        out_shape=[jax.ShapeDtypeStruct(w["shape"], w["dtype"]) for w in wrt],
        compiler_params=_cp(("arbitrary",) * len(grid)),
    )(*[a for a, _, _ in allin])


def _rmsnorm_fn(x, w):
    return (x * lax.rsqrt(jnp.mean(x * x, axis=-1, keepdims=True) + EPS) * w,)


def _swiglu_fn(g, u):
    return (_silu(g) * u,)


def _qkprep_fn(t, w64, cos, sin, hmean, tile64):
    w = dot_hi(jnp.broadcast_to(w64, (8, LANES)), tile64)
    w = jnp.sum(w, axis=0, keepdims=True) * 0.125
    y = t * lax.rsqrt(dot2(t * t, hmean) + EPS) * w
    return (y * cos + rope_rot(y) * sin,)


def _att_fn(q, kp, kc, vp, vc, first):
    iq = lax.broadcasted_iota(jnp.int32, (BAND, 2 * BAND), 0)
    ik = lax.broadcasted_iota(jnp.int32, (BAND, 2 * BAND), 1)
    rel = BAND + iq - ik
    ok = (rel >= 0) & (rel <= BAND) & ((ik >= BAND) | jnp.logical_not(first))
    lane = lax.broadcasted_iota(jnp.int32, (1, LANES), 1)
    kcat = jnp.concatenate([kp, kc], axis=0)
    vcat = jnp.concatenate([vp, vc], axis=0)
    o_pair = jnp.zeros((BAND, LANES), F32)
    l_pair = jnp.zeros((BAND, LANES), F32)
    for hh in range(2):
        lm = (lane // 64 == hh).astype(F32)
        s = dot_nt(q * lm, kcat) * 0.125
        s = jnp.where(ok, s, NEG)
        mx = jnp.max(s, axis=-1, keepdims=True)
        e = jnp.exp(s - mx)
        den = jnp.sum(e, axis=-1, keepdims=True)
        o_pair = o_pair + dot_nn(e / den, vcat) * lm
        l_pair = l_pair + (mx + jnp.log(den)) * lm
    return o_pair, l_pair


def _attmix_fn(o0, o1, o2, l0, l1, l2):
    m = jnp.maximum(jnp.maximum(l0, l1), l2)
    e0, e1, e2 = jnp.exp(l0 - m), jnp.exp(l1 - m), jnp.exp(l2 - m)
    return ((e0 * o0 + e1 * o1 + e2 * o2) / (e0 + e1 + e2),)


def _conv_fn(x, w0, w1, w2, w3, b):
    pre = x * w3 + shift_rows(x, 1) * w2 + shift_rows(x, 2) * w1 + shift_rows(x, 3) * w0 + b
    return (_silu(pre),)


def _ssdpre_fn(dtraw, bias, alog, ex):
    dt = _softplus(dtraw + bias)
    da = dt * (-jnp.exp(alog))
    return dot2(dt, ex), dot2(da, ex)


def _ssd_step(st, x, dtb, dab, bm, cm, ltri):
    cum = dot_hi(ltri, dab)
    cum_t = cum.T
    xdt = x * dtb
    cb = dot_nt(cm, bm)
    ri = lax.broadcasted_iota(jnp.int32, (CHUNK, CHUNK), 0)
    ci = lax.broadcasted_iota(jnp.int32, (CHUNK, CHUNK), 1)
    causal = ri >= ci
    lane = lax.broadcasted_iota(jnp.int32, (1, LANES), 1)
    rowi = lax.broadcasted_iota(jnp.int32, (LANES, 1), 0)
    ys = []
    for p in range(4):
        sl = slice(p * LANES, (p + 1) * LANES)
        cum_p, cum_tp, xdt_p = cum[:, sl], cum_t[sl, :], xdt[:, sl]
        acc = jnp.zeros((CHUNK, LANES), F32)
        for hh in range(2):
            col = jnp.sum(cum_p * (lane == 64 * hh).astype(F32), axis=1, keepdims=True)
            row = jnp.sum(cum_tp * (rowi == 64 * hh).astype(F32), axis=0, keepdims=True)
            dec = jnp.exp(jnp.where(causal, col - row, NEG))
            acc = acc + dot_nn(cb * dec, xdt_p * (lane // 64 == hh).astype(F32))
        ys.append(acc)
    y_diag = jnp.concatenate(ys, axis=1)
    y_off = dot_nn(cm, st) * jnp.exp(cum)
    last_row = (lax.broadcasted_iota(jnp.int32, (CHUNK, 1), 0) == CHUNK - 1).astype(F32)
    last = jnp.sum(cum * last_row, axis=0, keepdims=True)
    new_st = st * jnp.exp(last) + dot_tn(bm, xdt * jnp.exp(last - cum))
    return new_st, y_diag + y_off


def _ssdpost_fn(y, xs, z, dskip, ex, nw):
    db = jnp.sum(dot_hi(jnp.broadcast_to(dskip, (8, LANES)), ex), axis=0, keepdims=True) * 0.125
    y2 = (y + db * xs) * _silu(z)
    return (y2 * lax.rsqrt(jnp.mean(y2 * y2, axis=-1, keepdims=True) + EPS) * nw,)


def _merge_fn(ya, ys, ga, gs, ba, bs):
    return (_sigmoid(ga + ba) * ya + _sigmoid(gs + bs) * ys,)


TM = 512


def _full(shape):
    nd = len(shape)
    return (shape, lambda *_: (0,) * nd)


def _rmsnorm(name, x, w):
    t = x.shape[0]
    return _ew(name, _rmsnorm_fn, (t // TM,),
               [(x, (TM, D_MODEL), lambda i: (i, 0)), (w, (1, D_MODEL), lambda i: (0, 0))],
               [((t, D_MODEL), BF16, (TM, D_MODEL), lambda i: (i, 0))])[0]


def _rmsnorm_bwd(name, x, w, dh, dres):
    t = x.shape[0]
    row = ((TM, D_MODEL), lambda i: (i, 0))
    return _ew_bwd(name, _rmsnorm_fn, (t // TM,),
                   [(x, *row), (w, (1, D_MODEL), lambda i: (0, 0))], [(dh, *row)],
                   [dict(idx=0, kind="tile", shape=(t, D_MODEL), dtype=F32, block=row[0], imap=row[1], add=0),
                    dict(idx=1, kind="acc", shape=(1, D_MODEL), dtype=F32, block=(1, D_MODEL), imap=lambda i: (0, 0),
                         first=lambda: pl.program_id(0) == 0)],
                   adds=[(dres, *row)])


def _swiglu(name, g, u):
    t = g.shape[0]
    tc = 256
    blk = ((TM, tc), lambda j, i: (i, j))
    return _ew(name, _swiglu_fn, (D_FF // tc, t // TM), [(g, *blk), (u, *blk)], [((t, D_FF), BF16, *blk)])[0]


def _swiglu_bwd(name, g, u, da):
    t = g.shape[0]
    tc = 256
    blk = ((TM, tc), lambda j, i: (i, j))
    return _ew_bwd(name, _swiglu_fn, (D_FF // tc, t // TM), [(g, *blk), (u, *blk)], [(da, *blk)],
                   [dict(idx=0, kind="tile", shape=(t, D_FF), dtype=BF16, block=blk[0], imap=blk[1]),
                    dict(idx=1, kind="tile", shape=(t, D_FF), dtype=BF16, block=blk[0], imap=blk[1])])


def _qk_operands(proj, qkw, cos, sin, consts, tm):
    nrow = SEQ // tm
    c = ((LANES, LANES), lambda j, i: (0, 0))
    return [(proj, (tm, LANES), lambda j, i: (i, j)),
            (qkw, (None, 1, LANES), lambda j, i: (j // 12, 0, 0)),
            (cos, (tm, LANES), lambda j, i: (i % nrow, 0)),
            (sin, (tm, LANES), lambda j, i: (i % nrow, 0)),
            (consts["hmean"], *c), (consts["tile64"], *c)]


def _qkprep(name, proj, qkw, cos, sin, consts):
    t = proj.shape[0]
    return _ew(name, _qkprep_fn, (2 * QKV // LANES, t // TM), _qk_operands(proj, qkw, cos, sin, consts, TM),
               [((t, 2 * QKV), F32, (TM, LANES), lambda j, i: (i, j))])[0]


def _qkprep_bwd(name, proj, qkw, cos, sin, consts, dqs, dks):
    t = proj.shape[0]

    def pick(*c):
        a = pl.program_id(0) // 4
        out = c[5]
        for k in range(4, -1, -1):
            out = jnp.where(a == k, c[k], out)
        return (out,)

    return _ew_bwd(name, _qkprep_fn, (2 * QKV // LANES, t // TM), _qk_operands(proj, qkw, cos, sin, consts, TM),
                   [(d, (TM, LANES), lambda j, i: (i, j % 4)) for d in (*dqs, *dks)],
                   [dict(idx=0, kind="tile", shape=(t, 2 * QKV), dtype=BF16, block=(TM, LANES), imap=lambda j, i: (i, j)),
                    dict(idx=1, kind="acc", shape=(2, 1, LANES), dtype=F32, block=(None, 1, LANES),
                         imap=lambda j, i: (j // 12, 0, 0),
                         first=lambda: (pl.program_id(0) % 12 == 0) & (pl.program_id(1) == 0))],
                   ct_fn=pick)


def _att_specs(dil, g):
    nb = SEQ // dil // BAND
    blk = (None, BAND * dil, LANES)
    kq, kk, kv = g * 4, QKV // LANES + g * 4, V0 // LANES + g * 4

    def cur(n):
        return jnp.minimum(n, nb - 1)

    def prev(n):
        return jnp.maximum(jnp.minimum(n, nb - 1) - 1, 0)

    return nb, blk, [
        pl.BlockSpec(blk, lambda b, p, n: (b, cur(n), kq + p)),
        pl.BlockSpec(blk, lambda b, p, n: (b, prev(n), kk + p)),
        pl.BlockSpec(blk, lambda b, p, n: (b, cur(n), kk + p)),
        pl.BlockSpec(blk, lambda b, p, n: (b, prev(n), kv + p)),
        pl.BlockSpec(blk, lambda b, p, n: (b, cur(n), kv + p)),
    ]


def _att_fwd(name, qk, proj, g):
    bl = qk.shape[0] // SEQ
    dil = ATT_DILATIONS[g]
    nb, blk, specs = _att_specs(dil, g)
    qk3 = qk.reshape(bl, SEQ, 2 * QKV)
    proj3 = proj.reshape(bl, SEQ, NP)
    o_spec = pl.BlockSpec(blk, lambda b, p, n: (b, n, p))

    def body(q, kp, kc, vp, vc, o_ref, l_ref):
        first = pl.program_id(2) == 0

        def residue(r, carry):
            sl = pl.ds(r, BAND, stride=dil) if dil > 1 else pl.ds(0, BAND)
            o, l = _att_fn(q[sl, :], kp[sl, :], kc[sl, :], vp[sl, :], vc[sl, :], first)
            o_ref[sl, :] = o
            l_ref[sl, :] = l
            return carry

        lax.fori_loop(0, dil, residue, 0)

    o, l = pl.pallas_call(
        body, name=name, grid=(bl, ATT_OUT // LANES, nb), in_specs=specs, out_specs=[o_spec, o_spec],
        out_shape=[jax.ShapeDtypeStruct((bl, SEQ, ATT_OUT), F32)] * 2,
        compiler_params=_cp(("arbitrary",) * 3),
    )(qk3, qk3, qk3, proj3, proj3)
    return o.reshape(bl * SEQ, ATT_OUT), l.reshape(bl * SEQ, ATT_OUT)


def _att_bwd(name, qk, proj, g, do, dl):
    bl = qk.shape[0] // SEQ
    dil = ATT_DILATIONS[g]
    nb, blk, specs = _att_specs(dil, g)
    qk3 = qk.reshape(bl, SEQ, 2 * QKV)
    proj3 = proj.reshape(bl, SEQ, NP)
    ct_spec = pl.BlockSpec(blk, lambda b, p, n: (b, jnp.minimum(n, nb - 1), p))
    do3 = do.reshape(bl, SEQ, ATT_OUT)
    dl3 = dl.reshape(bl, SEQ, ATT_OUT)

    def body(q, kp, kc, vp, vc, do_ref, dl_ref, d_ref, dk_ref, dv_ref, ck, cv):
        n = pl.program_id(2)

        def residue(r, carry):
            sl = pl.ds(r, BAND, stride=dil) if dil > 1 else pl.ds(0, BAND)

            @pl.when(n < nb)
            def _():
                first = n == 0
                prim = [ref[sl, :] for ref in (q, kp, kc, vp, vc)]
                _, vjp = jax.vjp(lambda *a: _att_fn(*a, first), *prim)
                dq, dkp, dkc, dvp, dvc = vjp((do_ref[sl, :], dl_ref[sl, :]))
                d_ref[sl, :] = dq

                @pl.when(n > 0)
                def _():
                    dk_ref[sl, :] = ck[sl, :] + dkp
                    dv_ref[sl, :] = cv[sl, :] + dvp

                ck[sl, :] = dkc
                cv[sl, :] = dvc

            @pl.when(n == nb)
            def _():
                dk_ref[sl, :] = ck[sl, :]
                dv_ref[sl, :] = cv[sl, :]

            return carry

        lax.fori_loop(0, dil, residue, 0)

    one = jax.ShapeDtypeStruct((bl, SEQ, ATT_OUT), F32)
    o_specs = [
        pl.BlockSpec(blk, lambda b, p, n: (b, jnp.minimum(n, nb - 1), p)),
        pl.BlockSpec(blk, lambda b, p, n: (b, jnp.maximum(n - 1, 0), p)),
        pl.BlockSpec(blk, lambda b, p, n: (b, jnp.maximum(n - 1, 0), p)),
    ]
    dq, dk, dv = pl.pallas_call(
        body, name=name, grid=(bl, ATT_OUT // LANES, nb + 1), in_specs=specs + [ct_spec, ct_spec], out_specs=o_specs,
        out_shape=[one, one, one],
        scratch_shapes=[pltpu.VMEM((BAND * dil, LANES), F32), pltpu.VMEM((BAND * dil, LANES), F32)],
        compiler_params=_cp(("arbitrary",) * 3),
    )(qk3, qk3, qk3, proj3, proj3, do3, dl3)
    t = bl * SEQ
    return dq.reshape(t, ATT_OUT), dk.reshape(t, ATT_OUT), dv.reshape(t, ATT_OUT)


def _attmix(name, os_, ls_):
    t = os_[0].shape[0]
    blk = ((TM, ATT_OUT), lambda i: (i, 0))
    return _ew(name, _attmix_fn, (t // TM,), [(a, *blk) for a in (*os_, *ls_)], [((t, ATT_OUT), BF16, *blk)])[0]


def _attmix_bwd(name, os_, ls_, datt):
    t = os_[0].shape[0]
    blk = ((TM, ATT_OUT), lambda i: (i, 0))
    return _ew_bwd(name, _attmix_fn, (t // TM,), [(a, *blk) for a in (*os_, *ls_)], [(datt, *blk)],
                   [dict(idx=k, kind="tile", shape=(t, ATT_OUT), dtype=F32, block=blk[0], imap=blk[1]) for k in range(6)])


CONV_TC = 256


def _conv_operands(proj3, conv_w, conv_b):
    c0 = X0 // CONV_TC
    ins = [(proj3, (None, SEQ, CONV_TC), lambda j, b: (b, 0, c0 + j))]
    for k in range(4):
        ins.append((conv_w, (None, 1, CONV_TC), lambda j, b, k=k: (k, 0, j)))
    ins.append((conv_b, (1, CONV_TC), lambda j, b: (0, j)))
    return ins


def _conv(name, proj3, conv_w, conv_b):
    bl = proj3.shape[0]
    return _ew(name, _conv_fn, (XBC // CONV_TC, bl), _conv_operands(proj3, conv_w, conv_b),
               [((bl, SEQ, XBC), F32, (None, SEQ, CONV_TC), lambda j, b: (b, 0, j))])[0]


def _conv_bwd(name, proj3, conv_w, conv_b, dxs3, db3, dc3):
    bl = proj3.shape[0]
    nx = D_INNER // CONV_TC
    nb_ = N_SSM_GROUPS * D_STATE // CONV_TC
    blk = (None, SEQ, CONV_TC)
    cts = [(dxs3, blk, lambda j, b: (b, 0, jnp.minimum(j, nx - 1))),
           (db3, blk, lambda j, b: (b, 0, jnp.clip(j - nx, 0, nb_ - 1))),
           (dc3, blk, lambda j, b: (b, 0, jnp.clip(j - nx - nb_, 0, nb_ - 1)))]

    def pick(cx, cb, cc):
        j = pl.program_id(0)
        return (jnp.where(j < nx, cx, jnp.where(j < nx + nb_, cb, cc)),)

    first = lambda: pl.program_id(1) == 0
    wrt = [dict(idx=0, kind="tile", shape=(bl, SEQ, XBC), dtype=BF16, block=blk, imap=lambda j, b: (b, 0, j))]
    for k in range(4):
        wrt.append(dict(idx=1 + k, kind="acc", shape=(1, XBC), dtype=F32, block=(1, CONV_TC),
                        imap=lambda j, b: (0, j), first=first))
    wrt.append(dict(idx=5, kind="acc", shape=(1, XBC), dtype=F32, block=(1, CONV_TC), imap=lambda j, b: (0, j), first=first))
    return _ew_bwd(name, _conv_fn, (XBC // CONV_TC, bl), _conv_operands(proj3, conv_w, conv_b), cts, wrt, ct_fn=pick)


SSD_TM = 256


def _ssdpre_operands(proj, dt_bias, a_log, ex):
    return [(proj, (SSD_TM, DTW), lambda i: (i, DT0 // DTW)), (dt_bias, *_full((1, DTW))), (a_log, *_full((1, DTW))),
            (ex, *_full((LANES, D_INNER)))]


def _ssdpre(name, proj, dt_bias, a_log, ex):
    t = proj.shape[0]
    blk = ((SSD_TM, D_INNER), lambda i: (i, 0))
    return _ew(name, _ssdpre_fn, (t // SSD_TM,), _ssdpre_operands(proj, dt_bias, a_log, ex),
               [((t, D_INNER), F32, *blk), ((t, D_INNER), F32, *blk)])


def _ssdpre_bwd(name, proj, dt_bias, a_log, ex, ddtb, ddab):
    t = proj.shape[0]
    blk = ((SSD_TM, D_INNER), lambda i: (i, 0))
    first = lambda: pl.program_id(0) == 0
    return _ew_bwd(name, _ssdpre_fn, (t // SSD_TM,), _ssdpre_operands(proj, dt_bias, a_log, ex),
                   [(ddtb, *blk), (ddab, *blk)],
                   [dict(idx=0, kind="tile", shape=(t, DTW), dtype=BF16, block=(SSD_TM, DTW), imap=lambda i: (i, 0)),
                    dict(idx=1, kind="acc", shape=(1, DTW), dtype=F32, block=(1, DTW), imap=lambda i: (0, 0), first=first),
                    dict(idx=2, kind="acc", shape=(1, DTW), dtype=F32, block=(1, DTW), imap=lambda i: (0, 0), first=first)])


def _ssd_in_specs(rev):
    nc = SEQ // CHUNK

    def c_(c):
        return nc - 1 - c if rev else c

    wide = (None, CHUNK, 4 * LANES)
    nar = (None, CHUNK, D_STATE)
    xb = D_INNER // D_STATE
    return [
        pl.BlockSpec(wide, lambda b, g, c: (b, c_(c), g)),
        pl.BlockSpec(wide, lambda b, g, c: (b, c_(c), g)),
        pl.BlockSpec(wide, lambda b, g, c: (b, c_(c), g)),
        pl.BlockSpec(nar, lambda b, g, c: (b, c_(c), xb + g)),
        pl.BlockSpec(nar, lambda b, g, c: (b, c_(c), xb + N_SSM_GROUPS + g)),
        pl.BlockSpec((CHUNK, CHUNK), lambda b, g, c: (0, 0)),
    ], c_


def _ssd_fwd(name, xc3, dtb3, dab3, ltri):
    bl = xc3.shape[0]
    nc = SEQ // CHUNK
    specs, _ = _ssd_in_specs(False)

    def body(x, dtb, dab, bm, cm, lt, y_ref, st_ref, st):
        @pl.when(pl.program_id(2) == 0)
        def _():
            st[...] = jnp.zeros_like(st)

        s0 = st[...]
        st_ref[...] = s0
        new_st, y = _ssd_step(s0, x[...], dtb[...], dab[...], bm[...], cm[...], lt[...])
        y_ref[...] = y
        st[...] = new_st

    return pl.pallas_call(
        body, name=name, grid=(bl, N_SSM_GROUPS, nc), in_specs=specs,
        out_specs=[pl.BlockSpec((None, CHUNK, 4 * LANES), lambda b, g, c: (b, c, g)),
                   pl.BlockSpec((None, None, None, D_STATE, 4 * LANES), lambda b, g, c: (b, g, c, 0, 0))],
        out_shape=[jax.ShapeDtypeStruct((bl, SEQ, D_INNER), F32),
                   jax.ShapeDtypeStruct((bl, N_SSM_GROUPS, nc, D_STATE, 4 * LANES), F32)],
        scratch_shapes=[pltpu.VMEM((D_STATE, 4 * LANES), F32)],
        compiler_params=_cp(("arbitrary",) * 3),
    )(xc3, dtb3, dab3, xc3, xc3, ltri)


def _ssd_bwd(name, xc3, dtb3, dab3, ltri, states, dy3, dxs_part3):
    bl = xc3.shape[0]
    nc = SEQ // CHUNK
    specs, c_ = _ssd_in_specs(True)
    wide = pl.BlockSpec((None, CHUNK, 4 * LANES), lambda b, g, c: (b, c_(c), g))
    nar = pl.BlockSpec((None, CHUNK, D_STATE), lambda b, g, c: (b, c_(c), g))
    st_spec = pl.BlockSpec((None, None, None, D_STATE, 4 * LANES), lambda b, g, c: (b, g, c_(c), 0, 0))

    def body(x, dtb, dab, bm, cm, lt, st_ref, dy, dxp, dx_ref, ddtb_ref, ddab_ref, dbm_ref, dcm_ref, dst):
        @pl.when(pl.program_id(2) == 0)
        def _():
            dst[...] = jnp.zeros_like(dst)

        ltv = lt[...]
        _, vjp = jax.vjp(lambda *a: _ssd_step(*a, ltv), st_ref[...], x[...], dtb[...], dab[...], bm[...], cm[...])
        d_st, d_x, d_dtb, d_dab, d_bm, d_cm = vjp((dst[...], dy[...]))
        dst[...] = d_st
        dx_ref[...] = d_x + dxp[...]
        ddtb_ref[...] = d_dtb
        ddab_ref[...] = d_dab
        dbm_ref[...] = d_bm
        dcm_ref[...] = d_cm

    big = jax.ShapeDtypeStruct((bl, SEQ, D_INNER), F32)
    small = jax.ShapeDtypeStruct((bl, SEQ, N_SSM_GROUPS * D_STATE), F32)
    return pl.pallas_call(
        body, name=name, grid=(bl, N_SSM_GROUPS, nc), in_specs=specs + [st_spec, wide, wide],
        out_specs=[wide, wide, wide, nar, nar], out_shape=[big, big, big, small, small],
        scratch_shapes=[pltpu.VMEM((D_STATE, 4 * LANES), F32)],
        compiler_params=_cp(("arbitrary",) * 3),
    )(xc3, dtb3, dab3, xc3, xc3, ltri, states, dy3, dxs_part3)


def _ssdpost_operands(y, xc, proj, d_skip, ex, nw):
    w = 4 * LANES
    return [(y, (SSD_TM, w), lambda j, i: (i, j)), (xc, (SSD_TM, w), lambda j, i: (i, j)),
            (proj, (SSD_TM, w), lambda j, i: (i, Z0 // w + j)), (d_skip, (1, DTW), lambda j, i: (0, 0)),
            (ex, (LANES, w), lambda j, i: (0, j)), (nw, (1, w), lambda j, i: (0, j))]


def _ssdpost(name, y, xc, proj, d_skip, ex, nw):
    t = y.shape[0]
    w = 4 * LANES
    return _ew(name, _ssdpost_fn, (D_INNER // w, t // SSD_TM), _ssdpost_operands(y, xc, proj, d_skip, ex, nw),
               [((t, D_INNER), BF16, (SSD_TM, w), lambda j, i: (i, j))])[0]


def _ssdpost_bwd(name, y, xc, proj, d_skip, ex, nw, dysn):
    t = y.shape[0]
    w = 4 * LANES
    blk = ((SSD_TM, w), lambda j, i: (i, j))
    return _ew_bwd(name, _ssdpost_fn, (D_INNER // w, t // SSD_TM), _ssdpost_operands(y, xc, proj, d_skip, ex, nw),
                   [(dysn, *blk)],
                   [dict(idx=0, kind="tile", shape=(t, D_INNER), dtype=F32, block=blk[0], imap=blk[1]),
                    dict(idx=1, kind="tile", shape=(t, D_INNER), dtype=F32, block=blk[0], imap=blk[1]),
                    dict(idx=2, kind="tile", shape=(t, D_INNER), dtype=BF16, block=blk[0], imap=blk[1]),
                    dict(idx=3, kind="acc", shape=(1, DTW), dtype=F32, block=(1, DTW), imap=lambda j, i: (0, 0),
                         first=lambda: (pl.program_id(0) == 0) & (pl.program_id(1) == 0)),
                    dict(idx=5, kind="acc", shape=(1, D_INNER), dtype=F32, block=(1, w), imap=lambda j, i: (0, j),
                         first=lambda: pl.program_id(1) == 0)])


def _merge_operands(ya, ys, proj, b_gates):
    w = 4 * LANES
    g0 = G0 // w
    nh = D_MODEL // w
    return [(ya, (TM, w), lambda j, i: (i, j)), (ys, (TM, w), lambda j, i: (i, j)),
            (proj, (TM, w), lambda j, i: (i, g0 + j)), (proj, (TM, w), lambda j, i: (i, g0 + nh + j)),
            (b_gates, (1, w), lambda j, i: (0, j)), (b_gates, (1, w), lambda j, i: (0, nh + j))]


def _merge(name, ya, ys, proj, b_gates):
    t = ya.shape[0]
    w = 4 * LANES
    return _ew(name, _merge_fn, (D_MODEL // w, t // TM), _merge_operands(ya, ys, proj, b_gates),
               [((t, D_MODEL), BF16, (TM, w), lambda j, i: (i, j))])[0]


def _merge_bwd(name, ya, ys, proj, b_gates, dmixed):
    t = ya.shape[0]
    w = 4 * LANES
    blk = ((TM, w), lambda j, i: (i, j))
    first = lambda: pl.program_id(1) == 0
    tile = lambda k, dt: dict(idx=k, kind="tile", shape=(t, D_MODEL), dtype=dt, block=blk[0], imap=blk[1])
    acc = lambda k: dict(idx=k, kind="acc", shape=(1, D_MODEL), dtype=F32, block=(1, w), imap=lambda j, i: (0, j), first=first)
    return _ew_bwd(name, _merge_fn, (D_MODEL // w, t // TM), _merge_operands(ya, ys, proj, b_gates), [(dmixed, *blk)],
                   [tile(0, BF16), tile(1, BF16), tile(2, BF16), tile(3, BF16), acc(4), acc(5)])


def _loss(name, y, tgt):
    t = y.shape[0]
    blk = pl.BlockSpec((TM, D_MODEL), lambda i: (i, 0))

    def body(y_ref, t_ref, dy_ref, l_ref):
        e = y_ref[...] - t_ref[...]
        dy_ref[...] = e * (1.0 / D_MODEL)
        part = jnp.sum(jnp.sum(e * e, axis=-1, keepdims=True), axis=0, keepdims=True) * (0.5 / D_MODEL)
        part = jnp.broadcast_to(part, (8, LANES))

        @pl.when(pl.program_id(0) == 0)
        def _():
            l_ref[...] = part

        @pl.when(pl.program_id(0) > 0)
        def _():
            l_ref[...] += part

    return pl.pallas_call(
        body, name=name, grid=(t // TM,), in_specs=[blk, blk],
        out_specs=[blk, pl.BlockSpec((8, LANES), lambda i: (0, 0))],
        out_shape=[jax.ShapeDtypeStruct((t, D_MODEL), F32), jax.ShapeDtypeStruct((8, LANES), F32)],
        compiler_params=_cp(("arbitrary",)),
    )(y, tgt)


def _adamw_fn(w, g, m, v):
    m2 = B1 * m + (1.0 - B1) * g
    v2 = B2 * v + (1.0 - B2) * (g * g)
    m_hat = m2 / (1.0 - B1 ** STEP)
    v_hat = v2 / (1.0 - B2 ** STEP)
    return -LR * (m_hat / (jnp.sqrt(v_hat) + ADAM_EPS) + WD * w), m2, v2


def _adamw(name, w, g, m, v):
    rows, cols = w.shape
    tm = rows
    for cand in (512, 256, 128, 64, 32, 16, 8):
        if rows % cand == 0 and cand * cols * 4 <= (1 << 21):
            tm = cand
            break
    blk = ((tm, cols), lambda i: (i, 0))
    return _ew(name, _adamw_fn, (rows // tm,), [(a, *blk) for a in (w, g, m, v)], [((rows, cols), F32, *blk)] * 3)


def _ffn_fwd(tag, x, nw, wg, wu, wd):
    h = _rmsnorm(tag + "_norm", x, nw)
    g = _mm(tag + "_gate", h, wg, "nn", out_dtype=BF16)
    u = _mm(tag + "_up", h, wu, "nn", out_dtype=BF16)
    a = _swiglu(tag + "_act", g, u)
    y = _mm(tag + "_down", a, wd, "nn", alpha=0.5, res=x)
    return y, (x, h, g, u, a)


def _ffn_bwd(tag, saved, nw, wg, wu, wd, dy):
    x, h, g, u, a = saved
    da = _mm(tag + "_down_dx", dy, wd, "nt", alpha=0.5)
    dwd = _mm(tag + "_down_dw", a, dy, "tn", out_dtype=BF16, alpha=0.5)
    dg, du = _swiglu_bwd(tag + "_act_bwd", g, u, da)
    dwg = _mm(tag + "_gate_dw", h, dg, "tn", out_dtype=BF16)
    dwu = _mm(tag + "_up_dw", h, du, "tn", out_dtype=BF16)
    dh = _mm(tag + "_gate_dx", dg, wg, "nt")
    dh = _mm(tag + "_up_dx", du, wu, "nt", res=dh)
    dx, dnw = _rmsnorm_bwd(tag + "_norm_bwd", x, nw, dh, dy)
    return dx, (dnw, dwg, dwu, dwd)


def _mixer_fwd(tag, x, p, c):
    t = x.shape[0]
    bl = t // SEQ
    h = _rmsnorm(tag + "_norm", x, p["mix_norm_w"])
    proj = _mm(tag + "_in", h, p["w_in"], "nn")
    qk = _qkprep(tag + "_qk", proj, p["qkw"], c["cos"], c["sin"], c)
    os_, ls_ = [], []
    for g in range(3):
        o, l = _att_fwd(f"{tag}_att{g}", qk, proj, g)
        os_.append(o)
        ls_.append(l)
    att = _attmix(tag + "_attmix", os_, ls_)
    ya = _mm(tag + "_attproj", att, p["w_att_proj"], "nn")
    proj3 = proj.reshape(bl, SEQ, NP)
    xc3 = _conv(tag + "_conv", proj3, p["conv_w"], p["conv_b"])
    xc = xc3.reshape(t, XBC)
    dtb, dab = _ssdpre(tag + "_ssdpre", proj, p["dt_bias"], p["a_log"], c["ex"])
    dtb3, dab3 = dtb.reshape(bl, SEQ, D_INNER), dab.reshape(bl, SEQ, D_INNER)
    y3, states = _ssd_fwd(tag + "_ssd", xc3, dtb3, dab3, c["ltri"])
    y = y3.reshape(t, D_INNER)
    ysn = _ssdpost(tag + "_ssdpost", y, xc, proj, p["d_skip"], c["ex"], p["ssm_norm_w"])
    ys = _mm(tag + "_ssmproj", ysn, p["w_ssm_proj"], "nn")
    mixed = _merge(tag + "_merge", ya, ys, proj, p["b_gates"])
    out = _mm(tag + "_out", mixed, p["w_out"], "nn", res=x)
    return out, (x, h, proj, qk, os_, ls_, att, ya, xc3, dtb3, dab3, states, y, ysn, ys, mixed)


def _mixer_bwd(tag, saved, p, c, dout):
    x, h, proj, qk, os_, ls_, att, ya, xc3, dtb3, dab3, states, y, ysn, ys, mixed = saved
    t = x.shape[0]
    bl = t // SEQ
    xc = xc3.reshape(t, XBC)
    proj3 = proj.reshape(bl, SEQ, NP)
    gr = {}
    dmixed = _mm(tag + "_out_dx", dout, p["w_out"], "nt")
    gr["w_out"] = _mm(tag + "_out_dw", mixed, dout, "tn", out_dtype=BF16)
    dya, dys, dga, dgs, dba, dbs = _merge_bwd(tag + "_merge_bwd", ya, ys, proj, p["b_gates"], dmixed)
    gr["b_gates"] = jnp.concatenate([dba, dbs], axis=1)
    datt = _mm(tag + "_attproj_dx", dya, p["w_att_proj"], "nt")
    gr["w_att_proj"] = _mm(tag + "_attproj_dw", att, dya, "tn", out_dtype=BF16)
    dysn = _mm(tag + "_ssmproj_dx", dys, p["w_ssm_proj"], "nt")
    gr["w_ssm_proj"] = _mm(tag + "_ssmproj_dw", ysn, dys, "tn", out_dtype=BF16)
    dmix = _attmix_bwd(tag + "_attmix_bwd", os_, ls_, datt)
    dqs, dks, dvs = [], [], []
    for g in range(3):
        dq, dk, dv = _att_bwd(f"{tag}_att{g}_bwd", qk, proj, g, dmix[g], dmix[3 + g])
        dqs.append(dq)
        dks.append(dk)
        dvs.append(dv)
    dqk, gr["qkw"] = _qkprep_bwd(tag + "_qk_bwd", proj, p["qkw"], c["cos"], c["sin"], c, dqs, dks)
    dy, dxs_part, dz, gr["d_skip"], gr["ssm_norm_w"] = _ssdpost_bwd(
        tag + "_ssdpost_bwd", y, xc, proj, p["d_skip"], c["ex"], p["ssm_norm_w"], dysn)
    dxs3, ddtb3, ddab3, db3, dc3 = _ssd_bwd(
        tag + "_ssd_bwd", xc3, dtb3, dab3, c["ltri"], states, dy.reshape(bl, SEQ, D_INNER), dxs_part.reshape(bl, SEQ, D_INNER))
    ddt, gr["dt_bias"], gr["a_log"] = _ssdpre_bwd(
        tag + "_ssdpre_bwd", proj, p["dt_bias"], p["a_log"], c["ex"], ddtb3.reshape(t, D_INNER), ddab3.reshape(t, D_INNER))
    dxbc3, dcw0, dcw1, dcw2, dcw3, gr["conv_b"] = _conv_bwd(tag + "_conv_bwd", proj3, p["conv_w"], p["conv_b"], dxs3, db3, dc3)
    gr["conv_w"] = jnp.concatenate([dcw0, dcw1, dcw2, dcw3], axis=0)
    dproj = jnp.concatenate(
        [dqk] + [d.astype(BF16) for d in dvs] + [dz, dga, dgs, dxbc3.reshape(t, XBC), ddt,
                                                  jnp.zeros((t, NP - DT0 - DTW), BF16)], axis=1)
    gr["w_in"] = _mm(tag + "_in_dw", h, dproj, "tn", out_dtype=BF16)
    dh = _mm(tag + "_in_dx", dproj, p["w_in"], "nt")
    dx, gr["mix_norm_w"] = _rmsnorm_bwd(tag + "_norm_bwd", x, p["mix_norm_w"], dh, dout)
    return dx, gr


def _constants():
    cos, sin = _rope_tables()
    return dict(cos=cos, sin=sin, hmean=_head_mean_mat(), tile64=_tile64_mat(),
                ex=_head_expand_mat(), ltri=_ltri_mat())


FLAT_W = 1024
ANY = pl.BlockSpec(memory_space=pl.ANY)


def _mesh_pos():
    return lax.axis_index("x"), lax.axis_index("y"), lax.axis_index("c")


def _other_chips(x, y):
    return [(1 - x, y), (x, 1 - y), (1 - x, 1 - y)]


def _all_gather_shards(flat, chip_idx):
    r = flat.shape[0]
    h = r // 2
    init = lax.dynamic_update_slice(jnp.zeros((4, r, FLAT_W), flat.dtype), flat[None], (chip_idx[0], 0, 0))

    def body(src, _, out, send_sems, recv_sems):
        x, y, c = _mesh_pos()
        chips = _other_chips(x, y)

        def half(chip, hf):
            return out.at[2 * chip[0] + chip[1], pl.ds(hf * h, h), :]

        def copy(k, src_ref, dst_ref, to):
            return pltpu.make_async_remote_copy(src_ref=src_ref, dst_ref=dst_ref, send_sem=send_sems.at[k],
                                                recv_sem=recv_sems.at[k], device_id=to, device_id_type=MESH)

        first = [copy(j, src.at[pl.ds(c * h, h), :], half((x, y), c), (*chip, c)) for j, chip in enumerate(chips)]
        for cp in first:
            cp.start()
        passed = [copy(3 + j, half(chip, c), half(chip, c), (x, y, 1 - c)) for j, chip in enumerate(chips)]
        for j, chip in enumerate(chips):
            copy(j, half(chip, c), half(chip, c), (x, y, c)).wait_recv()
            passed[j].start()
        for j, chip in enumerate(chips):
            copy(3 + j, half(chip, 1 - c), half(chip, 1 - c), (x, y, c)).wait_recv()
        for cp in first + passed:
            cp.wait_send()

    return pl.pallas_call(
        body, name="all_gather_weights", out_shape=jax.ShapeDtypeStruct((4, r, FLAT_W), flat.dtype),
        in_specs=[ANY, ANY], out_specs=ANY, input_output_aliases={1: 0},
        scratch_shapes=[pltpu.SemaphoreType.DMA((6,)), pltpu.SemaphoreType.DMA((6,))],
    )(flat, init)


def _pair_exchange(g):
    r = g.shape[1]
    h = r // 2

    def body(src, out, send_sems, recv_sems):
        x, y, c = _mesh_pos()
        cps = [pltpu.make_async_remote_copy(src_ref=src.at[k, pl.ds((1 - c) * h, h), :], dst_ref=out.at[k],
                                            send_sem=send_sems.at[k], recv_sem=recv_sems.at[k],
                                            device_id=(x, y, 1 - c), device_id_type=MESH) for k in range(4)]
        for cp in cps:
            cp.start()
        for cp in cps:
            cp.wait()

    return pl.pallas_call(
        body, name="grad_pair_exchange", out_shape=jax.ShapeDtypeStruct((4, h, FLAT_W), g.dtype),
        in_specs=[ANY], out_specs=ANY,
        scratch_shapes=[pltpu.SemaphoreType.DMA((4,)), pltpu.SemaphoreType.DMA((4,))],
    )(g)


def _chip_exchange(ha):
    h = ha.shape[1]

    def body(src, out, send_sems, recv_sems):
        x, y, c = _mesh_pos()
        cps = [pltpu.make_async_remote_copy(src_ref=src.at[2 * chip[0] + chip[1]], dst_ref=out.at[j],
                                            send_sem=send_sems.at[j], recv_sem=recv_sems.at[j],
                                            device_id=(*chip, c), device_id_type=MESH)
               for j, chip in enumerate(_other_chips(x, y))]
        for cp in cps:
            cp.start()
        for cp in cps:
            cp.wait()

    return pl.pallas_call(
        body, name="grad_chip_exchange", out_shape=jax.ShapeDtypeStruct((3, h, FLAT_W), ha.dtype),
        in_specs=[ANY], out_specs=ANY,
        scratch_shapes=[pltpu.SemaphoreType.DMA((3,)), pltpu.SemaphoreType.DMA((3,))],
    )(ha)


def _pair_share(red):
    h = red.shape[0]

    def body(src, out, send_sem, recv_sem):
        x, y, c = _mesh_pos()
        cp = pltpu.make_async_remote_copy(src_ref=src, dst_ref=out, send_sem=send_sem, recv_sem=recv_sem,
                                          device_id=(x, y, 1 - c), device_id_type=MESH)
        cp.start()
        cp.wait()

    return pl.pallas_call(
        body, name="grad_pair_share", out_shape=jax.ShapeDtypeStruct((h, FLAT_W), red.dtype),
        in_specs=[ANY], out_specs=ANY,
        scratch_shapes=[pltpu.SemaphoreType.DMA, pltpu.SemaphoreType.DMA],
    )(red)


ADD_TR = 512


def _pair_sum(g, recv, c_idx):
    h = recv.shape[1]
    g4 = g.reshape(4, 2, h, FLAT_W)

    def body(c_ref, a_ref, b_ref, o_ref):
        o_ref[...] = (a_ref[...].astype(F32) + b_ref[...].astype(F32)).astype(o_ref.dtype)

    return pl.pallas_call(
        body, name="grad_pair_sum",
        grid_spec=pltpu.PrefetchScalarGridSpec(
            num_scalar_prefetch=1, grid=(4, h // ADD_TR),
            in_specs=[pl.BlockSpec((None, None, ADD_TR, FLAT_W), lambda k, i, c: (k, c[0], i, 0)),
                      pl.BlockSpec((None, ADD_TR, FLAT_W), lambda k, i, c: (k, i, 0))],
            out_specs=pl.BlockSpec((None, ADD_TR, FLAT_W), lambda k, i, c: (k, i, 0))),
        out_shape=jax.ShapeDtypeStruct((4, h, FLAT_W), BF16),
        compiler_params=_cp(("arbitrary", "arbitrary")),
    )(c_idx, g4, recv)


def _chip_sum(ha, recv, chip_idx):
    h = ha.shape[1]

    def body(k_ref, a_ref, r0, r1, r2, o_ref):
        o_ref[...] = ((a_ref[...].astype(F32) + r0[...].astype(F32)) + r1[...].astype(F32)) + r2[...].astype(F32)

    blk = (None, ADD_TR, FLAT_W)
    return pl.pallas_call(
        body, name="grad_chip_sum",
        grid_spec=pltpu.PrefetchScalarGridSpec(
            num_scalar_prefetch=1, grid=(h // ADD_TR,),
            in_specs=[pl.BlockSpec(blk, lambda i, k: (k[0], i, 0))] +
                     [pl.BlockSpec(blk, lambda i, k, j=j: (j, i, 0)) for j in range(3)],
            out_specs=pl.BlockSpec((ADD_TR, FLAT_W), lambda i, k: (i, 0))),
        out_shape=jax.ShapeDtypeStruct((h, FLAT_W), F32),
        compiler_params=_cp(("arbitrary",)),
    )(chip_idx, ha, recv, recv, recv)


def _all_sum_small(vec):
    rows = vec.shape[0]

    def body(v_ref, o_ref, buf, send_sems, recv_sems):
        x, y, c = _mesh_pos()
        me, sibling = (x, y, c), (x, y, 1 - c)
        chips = _other_chips(x, y)

        def slot(p):
            return buf.at[4 * p[0] + 2 * p[1] + p[2]]

        def copy(k, block, to, src=None):
            return pltpu.make_async_remote_copy(src_ref=slot(block) if src is None else src, dst_ref=slot(block),
                                                send_sem=send_sems.at[k], recv_sem=recv_sems.at[k],
                                                device_id=to, device_id_type=MESH)

        first = [copy(0, me, sibling, src=v_ref)]
        first += [copy(1 + j, me, (*chip, c), src=v_ref) for j, chip in enumerate(chips)]
        for cp in first:
            cp.start()
        passed = [copy(4 + j, (*chip, c), sibling) for j, chip in enumerate(chips)]
        for j, chip in enumerate(chips):
            copy(1 + j, (*chip, c), me).wait_recv()
            passed[j].start()
        copy(0, sibling, me).wait_recv()
        for j, chip in enumerate(chips):
            copy(4 + j, (*chip, 1 - c), me).wait_recv()
        for cp in first + passed:
            cp.wait_send()
        slot(me)[...] = v_ref[...]
        acc = buf[0]
        for k in range(1, 8):
            acc = acc + buf[k]
        o_ref[...] = acc

    vm = pl.BlockSpec(memory_space=pltpu.VMEM)
    return pl.pallas_call(
        body, name="small_all_sum", out_shape=jax.ShapeDtypeStruct((rows, LANES), F32),
        in_specs=[vm], out_specs=vm,
        scratch_shapes=[pltpu.VMEM((8, rows, LANES), F32), pltpu.SemaphoreType.DMA((7,)), pltpu.SemaphoreType.DMA((7,))],
    )(vec)


def _pad_lanes(v, n=LANES):
    return jnp.pad(v, (0, n - v.shape[0]))[None, :]


def _w_in_to_kernel(w):
    return jnp.concatenate([w[:, :6656], w[:, 9760:N_IN], w[:, 6656:9728], w[:, 9728:9760],
                            jnp.zeros((w.shape[0], NP - N_IN), w.dtype)], axis=1)


def _w_in_from_kernel(w):
    return jnp.concatenate([w[:, :6656], w[:, X0:DT0], w[:, DT0:DT0 + 32], w[:, G0:X0]], axis=1)


def _layer_params(big, small, i):
    p = {k: big[k][i] for k in ("ffn1_w_gate", "ffn1_w_up", "ffn1_w_down", "w_att_proj", "w_ssm_proj", "w_out",
                                 "ffn2_w_gate", "ffn2_w_up", "ffn2_w_down")}
    p["w_in"] = _w_in_to_kernel(big["w_in"][i])
    p["conv_w"] = big["conv_w"][i][:, None, :]
    for k in ("ffn1_norm_w", "mix_norm_w", "ffn2_norm_w", "b_gates", "conv_b", "ssm_norm_w"):
        p[k] = small[k][i][None, :]
    for k in ("dt_bias", "a_log", "d_skip"):
        p[k] = _pad_lanes(small[k][i])
    p["qkw"] = jnp.stack([_pad_lanes(small["q_norm_w"][i]), _pad_lanes(small["k_norm_w"][i])])
    return p


def _local_step(x, tgt, layers, c):
    saved = []
    for i, p in enumerate(layers):
        x, s1 = _ffn_fwd(f"L{i}_ffn1", x, p["ffn1_norm_w"], p["ffn1_w_gate"], p["ffn1_w_up"], p["ffn1_w_down"])
        x, s2 = _mixer_fwd(f"L{i}_mix", x, p, c)
        x, s3 = _ffn_fwd(f"L{i}_ffn2", x, p["ffn2_norm_w"], p["ffn2_w_gate"], p["ffn2_w_up"], p["ffn2_w_down"])
        saved.append((s1, s2, s3))
    dx, loss_blk = _loss("loss", x, tgt)
    grads = [None] * len(layers)
    for i in reversed(range(len(layers))):
        p = layers[i]
        s1, s2, s3 = saved[i]
        dx, (dn2, dg2, du2, dd2) = _ffn_bwd(f"L{i}_ffn2", s3, p["ffn2_norm_w"], p["ffn2_w_gate"], p["ffn2_w_up"], p["ffn2_w_down"], dx)
        dx, gr = _mixer_bwd(f"L{i}_mix", s2, p, c, dx)
        dx, (dn1, dg1, du1, dd1) = _ffn_bwd(f"L{i}_ffn1", s1, p["ffn1_norm_w"], p["ffn1_w_gate"], p["ffn1_w_up"], p["ffn1_w_down"], dx)
        gr.update(ffn1_norm_w=dn1, ffn1_w_gate=dg1, ffn1_w_up=du1, ffn1_w_down=dd1,
                  ffn2_norm_w=dn2, ffn2_w_gate=dg2, ffn2_w_up=du2, ffn2_w_down=dd2)
        grads[i] = gr
    return loss_blk, dx, grads


WEIGHTS = ["ffn1_norm_w", "ffn1_w_gate", "ffn1_w_up", "ffn1_w_down", "mix_norm_w", "w_in", "b_gates", "q_norm_w",
           "k_norm_w", "conv_w", "conv_b", "dt_bias", "a_log", "d_skip", "ssm_norm_w", "w_att_proj", "w_ssm_proj",
           "w_out", "ffn2_norm_w", "ffn2_w_gate", "ffn2_w_up", "ffn2_w_down"]
SHARD_AXIS = {"ffn1_w_gate": 2, "ffn1_w_up": 2, "ffn1_w_down": 1, "w_in": 2, "conv_w": 2, "w_att_proj": 2,
              "w_ssm_proj": 1, "w_out": 1, "ffn2_w_gate": 2, "ffn2_w_up": 2, "ffn2_w_down": 1}
BIG = [n for n in WEIGHTS if n in SHARD_AXIS]
SMALL = [n for n in WEIGHTS if n not in SHARD_AXIS]
FLAT_ROWS = 32768
SMALL_ROWS = 328


def _to_flat(parts, dtype):
    v = jnp.concatenate([p.astype(dtype).reshape(-1) for p in parts])
    return jnp.pad(v, (0, FLAT_ROWS * FLAT_W - v.shape[0])).reshape(FLAT_ROWS, FLAT_W)


def _from_flat(flat, shapes):
    v = flat.reshape(-1)
    out, off = [], 0
    for s in shapes:
        n = math.prod(s)
        out.append(v[off:off + n].reshape(s))
        off += n
    return out


def _pack_small(parts):
    v = jnp.concatenate([p.astype(F32).reshape(-1) for p in parts])
    return jnp.pad(v, (0, SMALL_ROWS * LANES - v.shape[0])).reshape(SMALL_ROWS, LANES)


def _bf16_split(w):
    hi = w.astype(BF16)
    return hi, (w - hi.astype(F32)).astype(BF16)


def kernel(x, ffn1_norm_w, ffn1_w_gate, ffn1_w_up, ffn1_w_down, mix_norm_w, w_in, b_gates, q_norm_w, k_norm_w, conv_w, conv_b, dt_bias, a_log, d_skip, ssm_norm_w, w_att_proj, w_ssm_proj, w_out, ffn2_norm_w, ffn2_w_gate, ffn2_w_up, ffn2_w_down, loss_target, m_ffn1_norm_w, m_ffn1_w_gate, m_ffn1_w_up, m_ffn1_w_down, m_mix_norm_w, m_w_in, m_b_gates, m_q_norm_w, m_k_norm_w, m_conv_w, m_conv_b, m_dt_bias, m_a_log, m_d_skip, m_ssm_norm_w, m_w_att_proj, m_w_ssm_proj, m_w_out, m_ffn2_norm_w, m_ffn2_w_gate, m_ffn2_w_up, m_ffn2_w_down, v_ffn1_norm_w, v_ffn1_w_gate, v_ffn1_w_up, v_ffn1_w_down, v_mix_norm_w, v_w_in, v_b_gates, v_q_norm_w, v_k_norm_w, v_conv_w, v_conv_b, v_dt_bias, v_a_log, v_d_skip, v_ssm_norm_w, v_w_att_proj, v_w_ssm_proj, v_w_out, v_ffn2_norm_w, v_ffn2_w_gate, v_ffn2_w_up, v_ffn2_w_down):
    w = dict(zip(WEIGHTS, (ffn1_norm_w, ffn1_w_gate, ffn1_w_up, ffn1_w_down, mix_norm_w, w_in, b_gates, q_norm_w, k_norm_w, conv_w, conv_b, dt_bias, a_log, d_skip, ssm_norm_w, w_att_proj, w_ssm_proj, w_out, ffn2_norm_w, ffn2_w_gate, ffn2_w_up, ffn2_w_down)))
    m = dict(zip(WEIGHTS, (m_ffn1_norm_w, m_ffn1_w_gate, m_ffn1_w_up, m_ffn1_w_down, m_mix_norm_w, m_w_in, m_b_gates, m_q_norm_w, m_k_norm_w, m_conv_w, m_conv_b, m_dt_bias, m_a_log, m_d_skip, m_ssm_norm_w, m_w_att_proj, m_w_ssm_proj, m_w_out, m_ffn2_norm_w, m_ffn2_w_gate, m_ffn2_w_up, m_ffn2_w_down)))
    v = dict(zip(WEIGHTS, (v_ffn1_norm_w, v_ffn1_w_gate, v_ffn1_w_up, v_ffn1_w_down, v_mix_norm_w, v_w_in, v_b_gates, v_q_norm_w, v_k_norm_w, v_conv_w, v_conv_b, v_dt_bias, v_a_log, v_d_skip, v_ssm_norm_w, v_w_att_proj, v_w_ssm_proj, v_w_out, v_ffn2_norm_w, v_ffn2_w_gate, v_ffn2_w_up, v_ffn2_w_down)))
    depth = ffn1_norm_w.shape[0]
    bl = x.shape[0]
    t = bl * SEQ
    mx, my, mc = lax.axis_index("x"), lax.axis_index("y"), lax.axis_index("c")
    c_idx = mc.astype(jnp.int32).reshape(1)
    chip_idx = (2 * mx + my).astype(jnp.int32).reshape(1)

    conv_hi, conv_lo = _bf16_split(conv_w)
    shard_shapes = [w[n].shape for n in BIG] + [conv_w.shape]
    gathered = _all_gather_shards(_to_flat([w[n] if n != "conv_w" else conv_hi for n in BIG] + [conv_lo], BF16), chip_idx)
    pieces = [_from_flat(gathered[k], shard_shapes) for k in range(4)]
    big = {}
    for j, n in enumerate(BIG):
        big[n] = jnp.concatenate([pieces[k][j] for k in range(4)], axis=SHARD_AXIS[n])
    lo = jnp.concatenate([pieces[k][len(BIG)] for k in range(4)], axis=2)
    big["conv_w"] = big["conv_w"].astype(F32) + lo.astype(F32)
    small = {n: w[n] for n in SMALL}

    c = _constants()
    layers = [_layer_params(big, small, i) for i in range(depth)]
    loss_blk, dx, grads = _local_step(x.reshape(t, D_MODEL), loss_target.reshape(t, D_MODEL), layers, c)
    grad_x = dx.reshape(bl, SEQ, D_MODEL)

    def stacked(n):
        if n == "w_in":
            return jnp.stack([_w_in_from_kernel(g["w_in"]) for g in grads])
        return jnp.stack([g[n] for g in grads])

    full = {n: stacked(n) for n in BIG}
    widths = {n: w[n].shape[SHARD_AXIS[n]] for n in BIG}
    by_shard = jnp.stack([
        _to_flat([lax.slice_in_dim(full[n], k * widths[n], (k + 1) * widths[n], axis=SHARD_AXIS[n]) for n in BIG], BF16)
        for k in range(4)])
    from_sibling = _pair_exchange(by_shard)
    pair = _pair_sum(by_shard, from_sibling, c_idx)
    from_chips = _chip_exchange(pair)
    reduced_half = _chip_sum(pair, from_chips, chip_idx)
    sibling_half = _pair_share(reduced_half)
    reduced = jnp.where(mc == 0, jnp.concatenate([reduced_half, sibling_half]), jnp.concatenate([sibling_half, reduced_half]))
    g_big = dict(zip(BIG, _from_flat(reduced, [w[n].shape for n in BIG])))

    def small_grad(n):
        if n == "q_norm_w":
            return jnp.stack([g["qkw"][0, 0, :64] for g in grads])
        if n == "k_norm_w":
            return jnp.stack([g["qkw"][1, 0, :64] for g in grads])
        return jnp.stack([g[n][0, :w[n].shape[1]] for g in grads])

    small_shapes = [w[n].shape for n in SMALL]
    tot = _all_sum_small(_pack_small([small_grad(n) for n in SMALL] + [loss_blk[0, :1]]))
    unpacked = _from_flat(tot, small_shapes + [(1,)])
    g_small = dict(zip(SMALL, unpacked[:-1]))
    loss = unpacked[-1][0]

    grad, delta, new_m, new_v = {}, {}, {}, {}
    for n in BIG:
        shp = w[n].shape
        two_d = (shp[0] * shp[1], shp[2])
        d_, m_, v_ = _adamw("adamw_" + n, w[n].reshape(two_d), g_big[n].reshape(two_d), m[n].reshape(two_d), v[n].reshape(two_d))
        grad[n], delta[n], new_m[n], new_v[n] = g_big[n], d_.reshape(shp), m_.reshape(shp), v_.reshape(shp)
    d_, m_, v_ = _adamw("adamw_small", _pack_small([w[n] for n in SMALL]), _pack_small([g_small[n] for n in SMALL]),
                        _pack_small([m[n] for n in SMALL]), _pack_small([v[n] for n in SMALL]))
    for n, a, b, c_ in zip(SMALL, _from_flat(d_, small_shapes), _from_flat(m_, small_shapes), _from_flat(v_, small_shapes)):
        grad[n], delta[n], new_m[n], new_v[n] = g_small[n], a, b, c_
    return (loss, grad_x, *[grad[n] for n in WEIGHTS], *[delta[n] for n in WEIGHTS],
            *[new_m[n] for n in WEIGHTS], *[new_v[n] for n in WEIGHTS])
```

```python
import functools
import math

import numpy as np
import jax
import jax.numpy as jnp
from jax import lax
from jax.experimental import pallas as pl
from jax.experimental.pallas import tpu as pltpu

F32 = jnp.float32
BF16 = jnp.bfloat16
HI = lax.Precision.HIGHEST
MESH = pl.DeviceIdType.MESH

D_MODEL = 1024
SEQ = 2048
DEPTH = 4
D_FF = 2816
ATT_DILATIONS = (1, 4, 16)
BAND = 128
ATT_OUT = 512
QKV = 1536
D_INNER = 2048
N_SSM_HEADS = 32
N_SSM_GROUPS = 4
D_STATE = 128
XBC = 3072
CHUNK = 128
N_IN = 11808
EPS = 1e-6
ROPE_THETA = 10000.0
NP = 12288
Q0, K0, V0, Z0, G0, X0, DT0 = 0, 1536, 3072, 4608, 6656, 8704, 11776
DTW = 128
LR, B1, B2, ADAM_EPS, WD, STEP = 0.001, 0.9, 0.999, 1e-08, 0.01, 10

LANES = 128
VMEM_LIMIT = 48 * 1024 * 1024
NEG = -1e30


def _cp(sem=None, **kw):
    return pltpu.CompilerParams(dimension_semantics=sem, vmem_limit_bytes=VMEM_LIMIT, **kw)


def _dg(a, b, ca, cb):
    return lax.dot_general(a.astype(BF16), b.astype(BF16), (((ca,), (cb,)), ((), ())), preferred_element_type=F32)


@jax.custom_vjp
def dot_nn(a, b):
    return _dg(a, b, 1, 0)


def _dot_nn_fwd(a, b):
    return _dg(a, b, 1, 0), (a, b)


def _dot_nn_bwd(r, g):
    a, b = r
    return _dg(g, b, 1, 1).astype(a.dtype), _dg(a, g, 0, 0).astype(b.dtype)


dot_nn.defvjp(_dot_nn_fwd, _dot_nn_bwd)


@jax.custom_vjp
def dot_nt(a, b):
    return _dg(a, b, 1, 1)


def _dot_nt_fwd(a, b):
    return _dg(a, b, 1, 1), (a, b)


def _dot_nt_bwd(r, g):
    a, b = r
    return _dg(g, b, 1, 0).astype(a.dtype), _dg(g, a, 0, 0).astype(b.dtype)


dot_nt.defvjp(_dot_nt_fwd, _dot_nt_bwd)


@jax.custom_vjp
def dot_tn(a, b):
    return _dg(a, b, 0, 0)


def _dot_tn_fwd(a, b):
    return _dg(a, b, 0, 0), (a, b)


def _dot_tn_bwd(r, g):
    a, b = r
    return _dg(b, g, 1, 1).astype(a.dtype), _dg(a, g, 1, 0).astype(b.dtype)


dot_tn.defvjp(_dot_tn_fwd, _dot_tn_bwd)


def dot_hi(a, b):
    return jnp.dot(a, b, precision=HI, preferred_element_type=F32)


def _dot2_raw(a, e, ce):
    hi = a.astype(BF16)
    lo = (a - hi.astype(F32)).astype(BF16)
    return _dg(hi, e, 1, ce) + _dg(lo, e, 1, ce)


@jax.custom_vjp
def dot2(a, e):
    return _dot2_raw(a, e, 0)


def _dot2_fwd(a, e):
    return _dot2_raw(a, e, 0), e


def _dot2_bwd(e, g):
    return _dot2_raw(g, e, 1), jnp.zeros_like(e)


dot2.defvjp(_dot2_fwd, _dot2_bwd)


def _rope_rot_raw(y, sign):
    lane = lax.broadcasted_iota(jnp.int32, y.shape, 1)
    first_half = (lane & 32) == 0
    return sign * jnp.where(first_half, -pltpu.roll(y, LANES - 32, 1), pltpu.roll(y, 32, 1))


@jax.custom_vjp
def rope_rot(y):
    return _rope_rot_raw(y, 1.0)


def _rope_rot_fwd(y):
    return _rope_rot_raw(y, 1.0), None


def _rope_rot_bwd(_, g):
    return (_rope_rot_raw(g, -1.0),)


rope_rot.defvjp(_rope_rot_fwd, _rope_rot_bwd)


def _shift_rows_raw(x, s):
    n = x.shape[0]
    r = pltpu.roll(x, s % n, 0)
    rows = lax.broadcasted_iota(jnp.int32, x.shape, 0)
    keep = rows >= s if s > 0 else rows < n + s
    return jnp.where(keep, r, 0.0)


@functools.partial(jax.custom_vjp, nondiff_argnums=(1,))
def shift_rows(x, s):
    return _shift_rows_raw(x, s)


def _shift_fwd(x, s):
    return _shift_rows_raw(x, s), None


def _shift_bwd(s, _, g):
    return (_shift_rows_raw(g, -s),)


shift_rows.defvjp(_shift_fwd, _shift_bwd)


def _sigmoid(x):
    return 1.0 / (1.0 + jnp.exp(-x))


def _silu(x):
    return x * _sigmoid(x)


def _softplus(x):
    return jnp.maximum(x, 0.0) + jnp.log(1.0 + jnp.exp(-jnp.abs(x)))


def _head_mean_mat():
    i = np.arange(LANES)
    return jnp.asarray((i[:, None] // 64 == i[None, :] // 64).astype(np.float32) / 64.0)


def _tile64_mat():
    t = np.zeros((LANES, LANES), np.float32)
    for l in range(LANES):
        t[l % 64, l] = 1.0
    return jnp.asarray(t)


def _head_expand_mat():
    e = np.zeros((LANES, D_INNER), np.float32)
    for l in range(D_INNER):
        e[l // 64, l] = 1.0
    return jnp.asarray(e)


def _ltri_mat():
    i = np.arange(CHUNK)
    return jnp.asarray((i[:, None] >= i[None, :]).astype(np.float32))


def _rope_tables():
    pos = jnp.arange(SEQ, dtype=F32)
    inv_freq = 1.0 / (ROPE_THETA ** (jnp.arange(0, 64, 2, dtype=F32) / 64))
    ang = pos[:, None] * inv_freq[None, :]
    return jnp.tile(jnp.cos(ang), (1, 4)), jnp.tile(jnp.sin(ang), (1, 4))


def _pick(n, cap):
    best = None
    for t in range(LANES, min(n, cap) + 1, LANES):
        if n % t == 0:
            best = t
    return best if best is not None else n


def _mm(name, a, b, mode, out_dtype=F32, alpha=None, res=None):
    if mode == "nn":
        (m, k), n = a.shape, b.shape[1]
    elif mode == "nt":
        (m, k), n = a.shape, b.shape[0]
    else:
        (k, m), n = a.shape, b.shape[1]
    tm, tn, tk = _pick(m, 1408), _pick(n, 1408), _pick(k, 512)
    nk = k // tk
    ca, cb = {"nn": (1, 0), "nt": (1, 1), "tn": (0, 0)}[mode]
    a_spec = pl.BlockSpec((tk, tm), lambda i, j, kk: (kk, i)) if mode == "tn" else pl.BlockSpec((tm, tk), lambda i, j, kk: (i, kk))
    b_spec = pl.BlockSpec((tn, tk), lambda i, j, kk: (j, kk)) if mode == "nt" else pl.BlockSpec((tk, tn), lambda i, j, kk: (kk, j))
    o_spec = pl.BlockSpec((tm, tn), lambda i, j, kk: (i, j))
    has_res = res is not None

    def finish(acc, res_ref, o_ref):
        if alpha is not None:
            acc = acc * alpha
        if has_res:
            acc = acc + res_ref[...].astype(F32)
        o_ref[...] = acc.astype(o_ref.dtype)

    def body(*refs):
        a_ref, b_ref = refs[0], refs[1]
        res_ref = refs[2] if has_res else None
        o_ref = refs[3] if has_res else refs[2]
        part = _dg(a_ref[...], b_ref[...], ca, cb)
        if nk == 1:
            finish(part, res_ref, o_ref)
            return
        acc_ref = refs[-1]
        kk = pl.program_id(2)

        @pl.when(kk == 0)
        def _():
            acc_ref[...] = part

        @pl.when(kk > 0)
        def _():
            acc_ref[...] += part

        @pl.when(kk == nk - 1)
        def _():
            finish(acc_ref[...], res_ref, o_ref)

    ins = [a, b] + ([res] if has_res else [])
    in_specs = [a_spec, b_spec] + ([o_spec] if has_res else [])
    return pl.pallas_call(
        body, name=name, grid=(m // tm, n // tn, nk), in_specs=in_specs, out_specs=o_spec,
        out_shape=jax.ShapeDtypeStruct((m, n), out_dtype),
        scratch_shapes=[pltpu.VMEM((tm, tn), F32)] if nk > 1 else [],
        compiler_params=_cp(("parallel", "parallel", "arbitrary")),
    )(*ins)


def _mmx(name, grid, a, b, out, contract, *, alpha=None, res=None, into=None):
    nk = grid[-1]
    has_res, has_into = res is not None, into is not None
    n_in = 2 + has_res + has_into

    def finish(acc, res_ref, o_ref):
        if alpha is not None:
            acc = acc * alpha
        if has_res:
            acc = acc + res_ref[...].astype(F32)
        o_ref[...] = acc.astype(o_ref.dtype)

    def body(*refs):
        res_ref = refs[2] if has_res else None
        o_ref = refs[n_in]
        part = _dg(refs[0][...], refs[1][...], *contract)
        if nk == 1:
            finish(part, res_ref, o_ref)
            return
        acc_ref = refs[-1]
        kk = pl.program_id(len(grid) - 1)

        @pl.when(kk == 0)
        def _():
            acc_ref[...] = part

        @pl.when(kk > 0)
        def _():
            acc_ref[...] += part

        @pl.when(kk == nk - 1)
        def _():
            finish(acc_ref[...], res_ref, o_ref)

    operands = [a, b] + ([res] if has_res else [])
    in_specs = [pl.BlockSpec(blk, im) for _, blk, im in operands] + ([ANY] if has_into else [])
    acc_shape = tuple(d for d in out[2] if d is not None)
    return pl.pallas_call(
        body, name=name, grid=grid, in_specs=in_specs, out_specs=pl.BlockSpec(out[2], out[3]),
        out_shape=jax.ShapeDtypeStruct(out[0], out[1]),
        scratch_shapes=[pltpu.VMEM(acc_shape, F32)] if nk > 1 else [],
        input_output_aliases={n_in - 1: 0} if has_into else {},
        compiler_params=_cp(("parallel",) * (len(grid) - 1) + ("arbitrary",)),
    )(*[o[0] for o in operands], *([into] if has_into else []))


def _ew(name, fn, grid, ins, outs, scratch=()):
    n_in, n_out = len(ins), len(outs)

    def body(*refs):
        vals = [r[...] for r in refs[:n_in]]
        res = fn(*vals, *refs[n_in + n_out:])
        for r, v in zip(refs[n_in:n_in + n_out], res):
            r[...] = v.astype(r.dtype)

    res = pl.pallas_call(
        body, name=name, grid=grid,
        in_specs=[pl.BlockSpec(b, m) for _, b, m in ins],
        out_specs=[pl.BlockSpec(b, m) for _, _, b, m in outs],
        out_shape=[jax.ShapeDtypeStruct(s, d) for s, d, _, _ in outs],
        scratch_shapes=list(scratch),
        compiler_params=_cp(("arbitrary",) * len(grid)),
    )(*[a for a, _, _ in ins])
    return res


def _ew_bwd(name, fn, grid, ins, cts, wrt, adds=(), ct_fn=None):
    n_in, n_ct, n_add = len(ins), len(cts), len(adds)
    idxs = [w["idx"] for w in wrt]

    def body(*refs):
        prim = [r[...] for r in refs[:n_in]]
        ct = [r[...].astype(F32) for r in refs[n_in:n_in + n_ct]]
        addv = [r[...] for r in refs[n_in + n_ct:n_in + n_ct + n_add]]
        orefs = refs[n_in + n_ct + n_add:]

        def f(*sel):
            full = list(prim)
            for i, s in zip(idxs, sel):
                full[i] = s
            return fn(*full)

        _, vjp = jax.vjp(f, *[prim[i].astype(F32) for i in idxs])
        grads = vjp(tuple(ct) if ct_fn is None else ct_fn(*ct))
        for w, g, r in zip(wrt, grads, orefs):
            if w["kind"] == "tile":
                if w.get("add") is not None:
                    g = g + addv[w["add"]].astype(F32)
                r[...] = g.astype(r.dtype)
            else:
                first = w["first"]()

                @pl.when(first)
                def _(r=r, g=g):
                    r[...] = g.astype(r.dtype)

                @pl.when(jnp.logical_not(first))
                def _(r=r, g=g):
                    r[...] += g.astype(r.dtype)

    allin = list(ins) + list(cts) + list(adds)
    return pl.pallas_call(
        body, name=name, grid=grid,
        in_specs=[pl.BlockSpec(b, m) for _, b, m in allin],
        out_specs=[pl.BlockSpec(w["block"], w["imap"]) for w in wrt],
        out_shape=[jax.ShapeDtypeStruct(w["shape"], w["dtype"]) for w in wrt],
        compiler_params=_cp(("arbitrary",) * len(grid)),
    )(*[a for a, _, _ in allin])


def _rmsnorm_fn(x, w):
    return (x * lax.rsqrt(jnp.mean(x * x, axis=-1, keepdims=True) + EPS) * w,)


def _swiglu_fn(g, u):
    return (_silu(g) * u,)


def _qkprep_fn(t, w64, cos, sin, hmean, tile64):
    w = dot_hi(jnp.broadcast_to(w64, (8, LANES)), tile64)
    w = jnp.sum(w, axis=0, keepdims=True) * 0.125
    y = t * lax.rsqrt(dot2(t * t, hmean) + EPS) * w
    return (y * cos + rope_rot(y) * sin,)


def _att_fn(q, kp, kc, vp, vc, first):
    iq = lax.broadcasted_iota(jnp.int32, (BAND, 2 * BAND), 0)
    ik = lax.broadcasted_iota(jnp.int32, (BAND, 2 * BAND), 1)
    rel = BAND + iq - ik
    ok = (rel >= 0) & (rel <= BAND) & ((ik >= BAND) | jnp.logical_not(first))
    lane = lax.broadcasted_iota(jnp.int32, (1, LANES), 1)
    kcat = jnp.concatenate([kp, kc], axis=0)
    vcat = jnp.concatenate([vp, vc], axis=0)
    o_pair = jnp.zeros((BAND, LANES), F32)
    l_pair = jnp.zeros((BAND, LANES), F32)
    for hh in range(2):
        lm = (lane // 64 == hh).astype(F32)
        s = dot_nt(q * lm, kcat) * 0.125
        s = jnp.where(ok, s, NEG)
        mx = jnp.max(s, axis=-1, keepdims=True)
        e = jnp.exp(s - mx)
        den = jnp.sum(e, axis=-1, keepdims=True)
        o_pair = o_pair + dot_nn(e / den, vcat) * lm
        l_pair = l_pair + (mx + jnp.log(den)) * lm
    return o_pair, l_pair


def _attmix_fn(o0, o1, o2, l0, l1, l2):
    m = jnp.maximum(jnp.maximum(l0, l1), l2)
    e0, e1, e2 = jnp.exp(l0 - m), jnp.exp(l1 - m), jnp.exp(l2 - m)
    return ((e0 * o0 + e1 * o1 + e2 * o2) / (e0 + e1 + e2),)


def _conv_fn(x, w0, w1, w2, w3, b):
    pre = x * w3 + shift_rows(x, 1) * w2 + shift_rows(x, 2) * w1 + shift_rows(x, 3) * w0 + b
    return (_silu(pre),)


def _ssdpre_fn(dtraw, bias, alog, ex):
    dt = _softplus(dtraw + bias)
    da = dt * (-jnp.exp(alog))
    return dot2(dt, ex), dot2(da, ex)


def _ssd_step(st, x, dtb, dab, bm, cm, ltri):
    cum = dot_hi(ltri, dab)
    cum_t = cum.T
    xdt = x * dtb
    cb = dot_nt(cm, bm)
    ri = lax.broadcasted_iota(jnp.int32, (CHUNK, CHUNK), 0)
    ci = lax.broadcasted_iota(jnp.int32, (CHUNK, CHUNK), 1)
    causal = ri >= ci
    lane = lax.broadcasted_iota(jnp.int32, (1, LANES), 1)
    rowi = lax.broadcasted_iota(jnp.int32, (LANES, 1), 0)
    ys = []
    for p in range(4):
        sl = slice(p * LANES, (p + 1) * LANES)
        cum_p, cum_tp, xdt_p = cum[:, sl], cum_t[sl, :], xdt[:, sl]
        acc = jnp.zeros((CHUNK, LANES), F32)
        for hh in range(2):
            col = jnp.sum(cum_p * (lane == 64 * hh).astype(F32), axis=1, keepdims=True)
            row = jnp.sum(cum_tp * (rowi == 64 * hh).astype(F32), axis=0, keepdims=True)
            dec = jnp.exp(jnp.where(causal, col - row, NEG))
            acc = acc + dot_nn(cb * dec, xdt_p * (lane // 64 == hh).astype(F32))
        ys.append(acc)
    y_diag = jnp.concatenate(ys, axis=1)
    y_off = dot_nn(cm, st) * jnp.exp(cum)
    last_row = (lax.broadcasted_iota(jnp.int32, (CHUNK, 1), 0) == CHUNK - 1).astype(F32)
    last = jnp.sum(cum * last_row, axis=0, keepdims=True)
    new_st = st * jnp.exp(last) + dot_tn(bm, xdt * jnp.exp(last - cum))
    return new_st, y_diag + y_off


def _ssdpost_fn(y, xs, z, dskip, ex, nw):
    db = jnp.sum(dot_hi(jnp.broadcast_to(dskip, (8, LANES)), ex), axis=0, keepdims=True) * 0.125
    y2 = (y + db * xs) * _silu(z)
    return (y2 * lax.rsqrt(jnp.mean(y2 * y2, axis=-1, keepdims=True) + EPS) * nw,)


def _merge_fn(ya, ys, ga, gs, ba, bs):
    return (_sigmoid(ga + ba) * ya + _sigmoid(gs + bs) * ys,)


TM = 512


def _full(shape):
    nd = len(shape)
    return (shape, lambda *_: (0,) * nd)


def _rmsnorm(name, x, w):
    t = x.shape[0]
    return _ew(name, _rmsnorm_fn, (t // TM,),
               [(x, (TM, D_MODEL), lambda i: (i, 0)), (w, (1, D_MODEL), lambda i: (0, 0))],
               [((t, D_MODEL), BF16, (TM, D_MODEL), lambda i: (i, 0))])[0]


def _rmsnorm_bwd(name, x, w, dh, dres):
    t = x.shape[0]
    row = ((TM, D_MODEL), lambda i: (i, 0))
    return _ew_bwd(name, _rmsnorm_fn, (t // TM,),
                   [(x, *row), (w, (1, D_MODEL), lambda i: (0, 0))], [(dh, *row)],
                   [dict(idx=0, kind="tile", shape=(t, D_MODEL), dtype=F32, block=row[0], imap=row[1], add=0),
                    dict(idx=1, kind="acc", shape=(1, D_MODEL), dtype=F32, block=(1, D_MODEL), imap=lambda i: (0, 0),
                         first=lambda: pl.program_id(0) == 0)],
                   adds=[(dres, *row)])


def _swiglu(name, g, u):
    r, cols = g.shape
    blk = ((TM, cols), lambda i: (i, 0))
    return _ew(name, _swiglu_fn, (r // TM,), [(g, *blk), (u, *blk)], [((r, cols), BF16, *blk)])[0]


def _swiglu_bwd(name, g, u, da):
    r, cols = g.shape
    blk = ((TM, cols), lambda i: (i, 0))
    return _ew_bwd(name, _swiglu_fn, (r // TM,), [(g, *blk), (u, *blk)], [(da, *blk)],
                   [dict(idx=0, kind="tile", shape=(r, cols), dtype=BF16, block=blk[0], imap=blk[1]),
                    dict(idx=1, kind="tile", shape=(r, cols), dtype=BF16, block=blk[0], imap=blk[1])])


def _qk_operands(proj, qkw, cos, sin, consts, tm):
    nrow = SEQ // tm
    c = ((LANES, LANES), lambda j, i: (0, 0))
    return [(proj, (tm, LANES), lambda j, i: (i, j)),
            (qkw, (None, 1, LANES), lambda j, i: (j // 12, 0, 0)),
            (cos, (tm, LANES), lambda j, i: (i % nrow, 0)),
            (sin, (tm, LANES), lambda j, i: (i % nrow, 0)),
            (consts["hmean"], *c), (consts["tile64"], *c)]


def _qkprep(name, proj, qkw, cos, sin, consts):
    t = proj.shape[0]
    return _ew(name, _qkprep_fn, (2 * QKV // LANES, t // TM), _qk_operands(proj, qkw, cos, sin, consts, TM),
               [((t, 2 * QKV), F32, (TM, LANES), lambda j, i: (i, j))])[0]


def _qkprep_bwd(name, proj, qkw, cos, sin, consts, dqs, dks):
    t = proj.shape[0]

    def pick(*c):
        a = pl.program_id(0) // 4
        out = c[5]
        for k in range(4, -1, -1):
            out = jnp.where(a == k, c[k], out)
        return (out,)

    return _ew_bwd(name, _qkprep_fn, (2 * QKV // LANES, t // TM), _qk_operands(proj, qkw, cos, sin, consts, TM),
                   [(d, (TM, LANES), lambda j, i: (i, j % 4)) for d in (*dqs, *dks)],
                   [dict(idx=0, kind="tile", shape=(t, 2 * QKV), dtype=BF16, block=(TM, LANES), imap=lambda j, i: (i, j)),
                    dict(idx=1, kind="acc", shape=(2, 1, LANES), dtype=F32, block=(None, 1, LANES),
                         imap=lambda j, i: (j // 12, 0, 0),
                         first=lambda: (pl.program_id(0) % 12 == 0) & (pl.program_id(1) == 0))],
                   ct_fn=pick)


def _att_specs(dil, g):
    nb = SEQ // dil // BAND
    blk = (None, BAND * dil, LANES)
    kq, kk, kv = g * 4, QKV // LANES + g * 4, V0 // LANES + g * 4

    def cur(n):
        return jnp.minimum(n, nb - 1)

    def prev(n):
        return jnp.maximum(jnp.minimum(n, nb - 1) - 1, 0)

    return nb, blk, [
        pl.BlockSpec(blk, lambda b, p, n: (b, cur(n), kq + p)),
        pl.BlockSpec(blk, lambda b, p, n: (b, prev(n), kk + p)),
        pl.BlockSpec(blk, lambda b, p, n: (b, cur(n), kk + p)),
        pl.BlockSpec(blk, lambda b, p, n: (b, prev(n), kv + p)),
        pl.BlockSpec(blk, lambda b, p, n: (b, cur(n), kv + p)),
    ]


def _att_fwd(name, qk, proj, g):
    bl = qk.shape[0] // SEQ
    dil = ATT_DILATIONS[g]
    nb, blk, specs = _att_specs(dil, g)
    qk3 = qk.reshape(bl, SEQ, 2 * QKV)
    proj3 = proj.reshape(bl, SEQ, NP)
    o_spec = pl.BlockSpec(blk, lambda b, p, n: (b, n, p))

    def body(q, kp, kc, vp, vc, o_ref, l_ref):
        first = pl.program_id(2) == 0

        def residue(r, carry):
            sl = pl.ds(r, BAND, stride=dil) if dil > 1 else pl.ds(0, BAND)
            o, l = _att_fn(q[sl, :], kp[sl, :], kc[sl, :], vp[sl, :], vc[sl, :], first)
            o_ref[sl, :] = o
            l_ref[sl, :] = l
            return carry

        lax.fori_loop(0, dil, residue, 0)

    o, l = pl.pallas_call(
        body, name=name, grid=(bl, ATT_OUT // LANES, nb), in_specs=specs, out_specs=[o_spec, o_spec],
        out_shape=[jax.ShapeDtypeStruct((bl, SEQ, ATT_OUT), F32)] * 2,
        compiler_params=_cp(("arbitrary",) * 3),
    )(qk3, qk3, qk3, proj3, proj3)
    return o.reshape(bl * SEQ, ATT_OUT), l.reshape(bl * SEQ, ATT_OUT)


def _att_bwd(name, qk, proj, g, do, dl):
    bl = qk.shape[0] // SEQ
    dil = ATT_DILATIONS[g]
    nb, blk, specs = _att_specs(dil, g)
    qk3 = qk.reshape(bl, SEQ, 2 * QKV)
    proj3 = proj.reshape(bl, SEQ, NP)
    ct_spec = pl.BlockSpec(blk, lambda b, p, n: (b, jnp.minimum(n, nb - 1), p))
    do3 = do.reshape(bl, SEQ, ATT_OUT)
    dl3 = dl.reshape(bl, SEQ, ATT_OUT)

    def body(q, kp, kc, vp, vc, do_ref, dl_ref, d_ref, dk_ref, dv_ref, ck, cv):
        n = pl.program_id(2)

        def residue(r, carry):
            sl = pl.ds(r, BAND, stride=dil) if dil > 1 else pl.ds(0, BAND)

            @pl.when(n < nb)
            def _():
                first = n == 0
                prim = [ref[sl, :] for ref in (q, kp, kc, vp, vc)]
                _, vjp = jax.vjp(lambda *a: _att_fn(*a, first), *prim)
                dq, dkp, dkc, dvp, dvc = vjp((do_ref[sl, :], dl_ref[sl, :]))
                d_ref[sl, :] = dq

                @pl.when(n > 0)
                def _():
                    dk_ref[sl, :] = ck[sl, :] + dkp
                    dv_ref[sl, :] = cv[sl, :] + dvp

                ck[sl, :] = dkc
                cv[sl, :] = dvc

            @pl.when(n == nb)
            def _():
                dk_ref[sl, :] = ck[sl, :]
                dv_ref[sl, :] = cv[sl, :]

            return carry

        lax.fori_loop(0, dil, residue, 0)

    one = jax.ShapeDtypeStruct((bl, SEQ, ATT_OUT), F32)
    o_specs = [
        pl.BlockSpec(blk, lambda b, p, n: (b, jnp.minimum(n, nb - 1), p)),
        pl.BlockSpec(blk, lambda b, p, n: (b, jnp.maximum(n - 1, 0), p)),
        pl.BlockSpec(blk, lambda b, p, n: (b, jnp.maximum(n - 1, 0), p)),
    ]
    dq, dk, dv = pl.pallas_call(
        body, name=name, grid=(bl, ATT_OUT // LANES, nb + 1), in_specs=specs + [ct_spec, ct_spec], out_specs=o_specs,
        out_shape=[one, one, one],
        scratch_shapes=[pltpu.VMEM((BAND * dil, LANES), F32), pltpu.VMEM((BAND * dil, LANES), F32)],
        compiler_params=_cp(("arbitrary",) * 3),
    )(qk3, qk3, qk3, proj3, proj3, do3, dl3)
    t = bl * SEQ
    return dq.reshape(t, ATT_OUT), dk.reshape(t, ATT_OUT), dv.reshape(t, ATT_OUT)


def _attmix(name, os_, ls_):
    t = os_[0].shape[0]
    blk = ((TM, ATT_OUT), lambda i: (i, 0))
    return _ew(name, _attmix_fn, (t // TM,), [(a, *blk) for a in (*os_, *ls_)], [((t, ATT_OUT), BF16, *blk)])[0]


def _attmix_bwd(name, os_, ls_, datt):
    t = os_[0].shape[0]
    blk = ((TM, ATT_OUT), lambda i: (i, 0))
    return _ew_bwd(name, _attmix_fn, (t // TM,), [(a, *blk) for a in (*os_, *ls_)], [(datt, *blk)],
                   [dict(idx=k, kind="tile", shape=(t, ATT_OUT), dtype=F32, block=blk[0], imap=blk[1]) for k in range(6)])


CONV_TC = 256


def _conv_operands(proj3, conv_w, conv_b):
    c0 = X0 // CONV_TC
    ins = [(proj3, (None, SEQ, CONV_TC), lambda j, b: (b, 0, c0 + j))]
    for k in range(4):
        ins.append((conv_w, (None, 1, CONV_TC), lambda j, b, k=k: (k, 0, j)))
    ins.append((conv_b, (1, CONV_TC), lambda j, b: (0, j)))
    return ins


def _conv(name, proj3, conv_w, conv_b):
    bl = proj3.shape[0]
    return _ew(name, _conv_fn, (XBC // CONV_TC, bl), _conv_operands(proj3, conv_w, conv_b),
               [((bl, SEQ, XBC), F32, (None, SEQ, CONV_TC), lambda j, b: (b, 0, j))])[0]


def _conv_bwd(name, proj3, conv_w, conv_b, dxs3, db3, dc3):
    bl = proj3.shape[0]
    nx = D_INNER // CONV_TC
    nb_ = N_SSM_GROUPS * D_STATE // CONV_TC
    blk = (None, SEQ, CONV_TC)
    cts = [(dxs3, blk, lambda j, b: (b, 0, jnp.minimum(j, nx - 1))),
           (db3, blk, lambda j, b: (b, 0, jnp.clip(j - nx, 0, nb_ - 1))),
           (dc3, blk, lambda j, b: (b, 0, jnp.clip(j - nx - nb_, 0, nb_ - 1)))]

    def pick(cx, cb, cc):
        j = pl.program_id(0)
        return (jnp.where(j < nx, cx, jnp.where(j < nx + nb_, cb, cc)),)

    first = lambda: pl.program_id(1) == 0
    wrt = [dict(idx=0, kind="tile", shape=(bl, SEQ, XBC), dtype=BF16, block=blk, imap=lambda j, b: (b, 0, j))]
    for k in range(4):
        wrt.append(dict(idx=1 + k, kind="acc", shape=(1, XBC), dtype=F32, block=(1, CONV_TC),
                        imap=lambda j, b: (0, j), first=first))
    wrt.append(dict(idx=5, kind="acc", shape=(1, XBC), dtype=F32, block=(1, CONV_TC), imap=lambda j, b: (0, j), first=first))
    return _ew_bwd(name, _conv_fn, (XBC // CONV_TC, bl), _conv_operands(proj3, conv_w, conv_b), cts, wrt, ct_fn=pick)


SSD_TM = 256


def _ssdpre_operands(proj, dt_bias, a_log, ex):
    return [(proj, (SSD_TM, DTW), lambda i: (i, DT0 // DTW)), (dt_bias, *_full((1, DTW))), (a_log, *_full((1, DTW))),
            (ex, *_full((LANES, D_INNER)))]


def _ssdpre(name, proj, dt_bias, a_log, ex):
    t = proj.shape[0]
    blk = ((SSD_TM, D_INNER), lambda i: (i, 0))
    return _ew(name, _ssdpre_fn, (t // SSD_TM,), _ssdpre_operands(proj, dt_bias, a_log, ex),
               [((t, D_INNER), F32, *blk), ((t, D_INNER), F32, *blk)])


def _ssdpre_bwd(name, proj, dt_bias, a_log, ex, ddtb, ddab):
    t = proj.shape[0]
    blk = ((SSD_TM, D_INNER), lambda i: (i, 0))
    first = lambda: pl.program_id(0) == 0
    return _ew_bwd(name, _ssdpre_fn, (t // SSD_TM,), _ssdpre_operands(proj, dt_bias, a_log, ex),
                   [(ddtb, *blk), (ddab, *blk)],
                   [dict(idx=0, kind="tile", shape=(t, DTW), dtype=BF16, block=(SSD_TM, DTW), imap=lambda i: (i, 0)),
                    dict(idx=1, kind="acc", shape=(1, DTW), dtype=F32, block=(1, DTW), imap=lambda i: (0, 0), first=first),
                    dict(idx=2, kind="acc", shape=(1, DTW), dtype=F32, block=(1, DTW), imap=lambda i: (0, 0), first=first)])


def _ssd_in_specs(rev):
    nc = SEQ // CHUNK

    def c_(c):
        return nc - 1 - c if rev else c

    wide = (None, CHUNK, 4 * LANES)
    nar = (None, CHUNK, D_STATE)
    xb = D_INNER // D_STATE
    return [
        pl.BlockSpec(wide, lambda b, g, c: (b, c_(c), g)),
        pl.BlockSpec(wide, lambda b, g, c: (b, c_(c), g)),
        pl.BlockSpec(wide, lambda b, g, c: (b, c_(c), g)),
        pl.BlockSpec(nar, lambda b, g, c: (b, c_(c), xb + g)),
        pl.BlockSpec(nar, lambda b, g, c: (b, c_(c), xb + N_SSM_GROUPS + g)),
        pl.BlockSpec((CHUNK, CHUNK), lambda b, g, c: (0, 0)),
    ], c_


def _ssd_fwd(name, xc3, dtb3, dab3, ltri):
    bl = xc3.shape[0]
    nc = SEQ // CHUNK
    specs, _ = _ssd_in_specs(False)

    def body(x, dtb, dab, bm, cm, lt, y_ref, st_ref, st):
        @pl.when(pl.program_id(2) == 0)
        def _():
            st[...] = jnp.zeros_like(st)

        s0 = st[...]
        st_ref[...] = s0
        new_st, y = _ssd_step(s0, x[...], dtb[...], dab[...], bm[...], cm[...], lt[...])
        y_ref[...] = y
        st[...] = new_st

    return pl.pallas_call(
        body, name=name, grid=(bl, N_SSM_GROUPS, nc), in_specs=specs,
        out_specs=[pl.BlockSpec((None, CHUNK, 4 * LANES), lambda b, g, c: (b, c, g)),
                   pl.BlockSpec((None, None, None, D_STATE, 4 * LANES), lambda b, g, c: (b, g, c, 0, 0))],
        out_shape=[jax.ShapeDtypeStruct((bl, SEQ, D_INNER), F32),
                   jax.ShapeDtypeStruct((bl, N_SSM_GROUPS, nc, D_STATE, 4 * LANES), F32)],
        scratch_shapes=[pltpu.VMEM((D_STATE, 4 * LANES), F32)],
        compiler_params=_cp(("arbitrary",) * 3),
    )(xc3, dtb3, dab3, xc3, xc3, ltri)


def _ssd_bwd(name, xc3, dtb3, dab3, ltri, states, dy3, dxs_part3):
    bl = xc3.shape[0]
    nc = SEQ // CHUNK
    specs, c_ = _ssd_in_specs(True)
    wide = pl.BlockSpec((None, CHUNK, 4 * LANES), lambda b, g, c: (b, c_(c), g))
    nar = pl.BlockSpec((None, CHUNK, D_STATE), lambda b, g, c: (b, c_(c), g))
    st_spec = pl.BlockSpec((None, None, None, D_STATE, 4 * LANES), lambda b, g, c: (b, g, c_(c), 0, 0))

    def body(x, dtb, dab, bm, cm, lt, st_ref, dy, dxp, dx_ref, ddtb_ref, ddab_ref, dbm_ref, dcm_ref, dst):
        @pl.when(pl.program_id(2) == 0)
        def _():
            dst[...] = jnp.zeros_like(dst)

        ltv = lt[...]
        _, vjp = jax.vjp(lambda *a: _ssd_step(*a, ltv), st_ref[...], x[...], dtb[...], dab[...], bm[...], cm[...])
        d_st, d_x, d_dtb, d_dab, d_bm, d_cm = vjp((dst[...], dy[...]))
        dst[...] = d_st
        dx_ref[...] = d_x + dxp[...]
        ddtb_ref[...] = d_dtb
        ddab_ref[...] = d_dab
        dbm_ref[...] = d_bm
        dcm_ref[...] = d_cm

    big = jax.ShapeDtypeStruct((bl, SEQ, D_INNER), F32)
    small = jax.ShapeDtypeStruct((bl, SEQ, N_SSM_GROUPS * D_STATE), F32)
    return pl.pallas_call(
        body, name=name, grid=(bl, N_SSM_GROUPS, nc), in_specs=specs + [st_spec, wide, wide],
        out_specs=[wide, wide, wide, nar, nar], out_shape=[big, big, big, small, small],
        scratch_shapes=[pltpu.VMEM((D_STATE, 4 * LANES), F32)],
        compiler_params=_cp(("arbitrary",) * 3),
    )(xc3, dtb3, dab3, xc3, xc3, ltri, states, dy3, dxs_part3)


def _ssdpost_operands(y, xc, proj, d_skip, ex, nw):
    w = 4 * LANES
    return [(y, (SSD_TM, w), lambda j, i: (i, j)), (xc, (SSD_TM, w), lambda j, i: (i, j)),
            (proj, (SSD_TM, w), lambda j, i: (i, Z0 // w + j)), (d_skip, (1, DTW), lambda j, i: (0, 0)),
            (ex, (LANES, w), lambda j, i: (0, j)), (nw, (1, w), lambda j, i: (0, j))]


def _ssdpost(name, y, xc, proj, d_skip, ex, nw):
    t = y.shape[0]
    w = 4 * LANES
    return _ew(name, _ssdpost_fn, (D_INNER // w, t // SSD_TM), _ssdpost_operands(y, xc, proj, d_skip, ex, nw),
               [((t, D_INNER), BF16, (SSD_TM, w), lambda j, i: (i, j))])[0]


def _ssdpost_bwd(name, y, xc, proj, d_skip, ex, nw, dysn):
    t = y.shape[0]
    w = 4 * LANES
    blk = ((SSD_TM, w), lambda j, i: (i, j))
    return _ew_bwd(name, _ssdpost_fn, (D_INNER // w, t // SSD_TM), _ssdpost_operands(y, xc, proj, d_skip, ex, nw),
                   [(dysn, *blk)],
                   [dict(idx=0, kind="tile", shape=(t, D_INNER), dtype=F32, block=blk[0], imap=blk[1]),
                    dict(idx=1, kind="tile", shape=(t, D_INNER), dtype=F32, block=blk[0], imap=blk[1]),
                    dict(idx=2, kind="tile", shape=(t, D_INNER), dtype=BF16, block=blk[0], imap=blk[1]),
                    dict(idx=3, kind="acc", shape=(1, DTW), dtype=F32, block=(1, DTW), imap=lambda j, i: (0, 0),
                         first=lambda: (pl.program_id(0) == 0) & (pl.program_id(1) == 0)),
                    dict(idx=5, kind="acc", shape=(1, D_INNER), dtype=F32, block=(1, w), imap=lambda j, i: (0, j),
                         first=lambda: pl.program_id(1) == 0)])


def _merge_operands(ya, ys, proj, b_gates):
    w = 4 * LANES
    g0 = G0 // w
    nh = D_MODEL // w
    return [(ya, (TM, w), lambda j, i: (i, j)), (ys, (TM, w), lambda j, i: (i, j)),
            (proj, (TM, w), lambda j, i: (i, g0 + j)), (proj, (TM, w), lambda j, i: (i, g0 + nh + j)),
            (b_gates, (1, w), lambda j, i: (0, j)), (b_gates, (1, w), lambda j, i: (0, nh + j))]


def _merge(name, ya, ys, proj, b_gates):
    t = ya.shape[0]
    w = 4 * LANES
    return _ew(name, _merge_fn, (D_MODEL // w, t // TM), _merge_operands(ya, ys, proj, b_gates),
               [((t, D_MODEL), BF16, (TM, w), lambda j, i: (i, j))])[0]


def _merge_bwd(name, ya, ys, proj, b_gates, dmixed):
    t = ya.shape[0]
    w = 4 * LANES
    blk = ((TM, w), lambda j, i: (i, j))
    first = lambda: pl.program_id(1) == 0
    tile = lambda k, dt: dict(idx=k, kind="tile", shape=(t, D_MODEL), dtype=dt, block=blk[0], imap=blk[1])
    acc = lambda k: dict(idx=k, kind="acc", shape=(1, D_MODEL), dtype=F32, block=(1, w), imap=lambda j, i: (0, j), first=first)
    return _ew_bwd(name, _merge_fn, (D_MODEL // w, t // TM), _merge_operands(ya, ys, proj, b_gates), [(dmixed, *blk)],
                   [tile(0, BF16), tile(1, BF16), tile(2, BF16), tile(3, BF16), acc(4), acc(5)])


def _loss(name, y, tgt):
    t = y.shape[0]
    blk = pl.BlockSpec((TM, D_MODEL), lambda i: (i, 0))

    def body(y_ref, t_ref, dy_ref, l_ref):
        e = y_ref[...] - t_ref[...]
        dy_ref[...] = e * (1.0 / D_MODEL)
        part = jnp.sum(jnp.sum(e * e, axis=-1, keepdims=True), axis=0, keepdims=True) * (0.5 / D_MODEL)
        part = jnp.broadcast_to(part, (8, LANES))

        @pl.when(pl.program_id(0) == 0)
        def _():
            l_ref[...] = part

        @pl.when(pl.program_id(0) > 0)
        def _():
            l_ref[...] += part

    return pl.pallas_call(
        body, name=name, grid=(t // TM,), in_specs=[blk, blk],
        out_specs=[blk, pl.BlockSpec((8, LANES), lambda i: (0, 0))],
        out_shape=[jax.ShapeDtypeStruct((t, D_MODEL), F32), jax.ShapeDtypeStruct((8, LANES), F32)],
        compiler_params=_cp(("arbitrary",)),
    )(y, tgt)


def _adamw_fn(w, g, m, v):
    m2 = B1 * m + (1.0 - B1) * g
    v2 = B2 * v + (1.0 - B2) * (g * g)
    m_hat = m2 / (1.0 - B1 ** STEP)
    v_hat = v2 / (1.0 - B2 ** STEP)
    return -LR * (m_hat / (jnp.sqrt(v_hat) + ADAM_EPS) + WD * w), m2, v2


def _adamw(name, w, g, m, v):
    rows, cols = w.shape
    tm = rows
    for cand in (512, 256, 128, 64, 32, 16, 8):
        if rows % cand == 0 and cand * cols * 4 <= (1 << 21):
            tm = cand
            break
    blk = ((tm, cols), lambda i: (i, 0))
    return _ew(name, _adamw_fn, (rows // tm,), [(a, *blk) for a in (w, g, m, v)], [((rows, cols), F32, *blk)] * 3)


NCH = 4
TMM = 1024
TKK = 512


def _ffn_fwd(tag, x, nw, wg, wu, wd, li):
    t, fc = x.shape[0], wg.shape[-1]
    h = _rmsnorm(tag + "_norm", x, nw)

    def up(name, w):
        return _mmx(name, (NCH, t // TMM, D_MODEL // TKK),
                    (h, (TMM, TKK), lambda k, i, kk: (i, kk)),
                    (w, (None, None, TKK, fc), lambda k, i, kk: (k, li, kk, 0)),
                    ((NCH, t, fc), BF16, (None, TMM, fc), lambda k, i, kk: (k, i, 0)), (1, 0))

    g, u = up(tag + "_gate", wg), up(tag + "_up", wu)
    a = _swiglu(tag + "_act", g.reshape(NCH * t, fc), u.reshape(NCH * t, fc)).reshape(NCH, t, fc)
    row = ((TMM, D_MODEL), lambda i, j, k: (i, 0))
    y = _mmx(tag + "_down", (t // TMM, 1, NCH),
             (a, (None, TMM, fc), lambda i, j, k: (k, i, 0)),
             (wd, (None, None, fc, D_MODEL), lambda i, j, k: (k, li, 0, 0)),
             ((t, D_MODEL), F32, *row), (1, 0), alpha=0.5, res=(x, *row))
    return y, (x, h, g, u, a)


def _ffn_bwd(tag, saved, nw, wg, wu, wd, li, dy, bufs):
    x, h, g, u, a = saved
    t, fc = x.shape[0], wg.shape[-1]
    bg, bu, bd = bufs
    da = _mmx(tag + "_down_dx", (NCH, t // TMM, D_MODEL // TKK),
              (dy, (TMM, TKK), lambda k, i, kk: (i, kk)),
              (wd, (None, None, fc, TKK), lambda k, i, kk: (k, li, 0, kk)),
              ((NCH, t, fc), F32, (None, TMM, fc), lambda k, i, kk: (k, i, 0)), (1, 1), alpha=0.5)
    bd = _mmx(tag + "_down_dw", (NCH, 1, t // TKK),
              (a, (None, TKK, fc), lambda k, j, kk: (k, kk, 0)),
              (dy, (TKK, D_MODEL), lambda k, j, kk: (kk, 0)),
              (bd.shape, BF16, (None, None, fc, D_MODEL), lambda k, j, kk: (li, k, 0, 0)), (0, 0), alpha=0.5, into=bd)
    dg, du = _swiglu_bwd(tag + "_act_bwd", g.reshape(NCH * t, fc), u.reshape(NCH * t, fc), da.reshape(NCH * t, fc))
    dg, du = dg.reshape(NCH, t, fc), du.reshape(NCH, t, fc)

    def dw(name, d, buf):
        return _mmx(name, (NCH, 1, t // TKK),
                    (h, (TKK, D_MODEL), lambda k, i, kk: (kk, 0)),
                    (d, (None, TKK, fc), lambda k, i, kk: (k, kk, 0)),
                    (buf.shape, BF16, (None, None, D_MODEL, fc), lambda k, i, kk: (li, k, 0, 0)), (0, 0), into=buf)

    bg, bu = dw(tag + "_gate_dw", dg, bg), dw(tag + "_up_dw", du, bu)
    row = ((TMM, D_MODEL), lambda i, j, k: (i, 0))

    def dx_(name, d, w, res):
        return _mmx(name, (t // TMM, 1, NCH),
                    (d, (None, TMM, fc), lambda i, j, k: (k, i, 0)),
                    (w, (None, None, D_MODEL, fc), lambda i, j, k: (k, li, 0, 0)),
                    ((t, D_MODEL), F32, *row), (1, 1), res=None if res is None else (res, *row))

    dh = dx_(tag + "_up_dx", du, wu, dx_(tag + "_gate_dx", dg, wg, None))
    dx, dnw = _rmsnorm_bwd(tag + "_norm_bwd", x, nw, dh, dy)
    return dx, dnw, (bg, bu, bd)


def _mixer_fwd(tag, x, p, c):
    t = x.shape[0]
    bl = t // SEQ
    h = _rmsnorm(tag + "_norm", x, p["mix_norm_w"])
    proj = _mm(tag + "_in", h, p["w_in"], "nn")
    qk = _qkprep(tag + "_qk", proj, p["qkw"], c["cos"], c["sin"], c)
    os_, ls_ = [], []
    for g in range(3):
        o, l = _att_fwd(f"{tag}_att{g}", qk, proj, g)
        os_.append(o)
        ls_.append(l)
    att = _attmix(tag + "_attmix", os_, ls_)
    li = p["layer"]
    wa, ws, wo = p["w_att_proj"], p["w_ssm_proj"], p["w_out"]
    ca, cs, co = wa.shape[-1], ws.shape[-2], wo.shape[-2]
    row = ((TMM, D_MODEL), lambda i, j, k: (i, 0))
    ya = _mmx(tag + "_attproj", (t // TMM, NCH, 1),
              (att, (TMM, ATT_OUT), lambda i, k, kk: (i, 0)),
              (wa, (None, None, ATT_OUT, ca), lambda i, k, kk: (k, li, 0, 0)),
              ((t, D_MODEL), F32, (TMM, ca), lambda i, k, kk: (i, k)), (1, 0))
    proj3 = proj.reshape(bl, SEQ, NP)
    xc3 = _conv(tag + "_conv", proj3, p["conv_w"], p["conv_b"])
    xc = xc3.reshape(t, XBC)
    dtb, dab = _ssdpre(tag + "_ssdpre", proj, p["dt_bias"], p["a_log"], c["ex"])
    dtb3, dab3 = dtb.reshape(bl, SEQ, D_INNER), dab.reshape(bl, SEQ, D_INNER)
    y3, states = _ssd_fwd(tag + "_ssd", xc3, dtb3, dab3, c["ltri"])
    y = y3.reshape(t, D_INNER)
    ysn = _ssdpost(tag + "_ssdpost", y, xc, proj, p["d_skip"], c["ex"], p["ssm_norm_w"])
    ys = _mmx(tag + "_ssmproj", (t // TMM, 1, NCH),
              (ysn, (TMM, cs), lambda i, j, k: (i, k)),
              (ws, (None, None, cs, D_MODEL), lambda i, j, k: (k, li, 0, 0)),
              ((t, D_MODEL), F32, *row), (1, 0))
    mixed = _merge(tag + "_merge", ya, ys, proj, p["b_gates"])
    out = _mmx(tag + "_out", (t // TMM, 1, NCH),
               (mixed, (TMM, co), lambda i, j, k: (i, k)),
               (wo, (None, None, co, D_MODEL), lambda i, j, k: (k, li, 0, 0)),
               ((t, D_MODEL), F32, *row), (1, 0), res=(x, *row))
    return out, (x, h, proj, qk, os_, ls_, att, ya, xc3, dtb3, dab3, states, y, ysn, ys, mixed)


def _mixer_bwd(tag, saved, p, c, dout, bufs):
    x, h, proj, qk, os_, ls_, att, ya, xc3, dtb3, dab3, states, y, ysn, ys, mixed = saved
    t = x.shape[0]
    bl = t // SEQ
    xc = xc3.reshape(t, XBC)
    proj3 = proj.reshape(bl, SEQ, NP)
    gr = {}
    li = p["layer"]
    wa, ws, wo = p["w_att_proj"], p["w_ssm_proj"], p["w_out"]
    ca, cs, co = wa.shape[-1], ws.shape[-2], wo.shape[-2]
    b_att, b_ssm, b_out = bufs

    def chunk_dx(name, d, w, cw):
        return _mmx(name, (t // TMM, NCH, D_MODEL // TKK),
                    (d, (TMM, TKK), lambda i, k, kk: (i, kk)),
                    (w, (None, None, cw, TKK), lambda i, k, kk: (k, li, 0, kk)),
                    ((t, NCH * cw), F32, (TMM, cw), lambda i, k, kk: (i, k)), (1, 1))

    def full_dw(name, a_, d, buf):
        kdim = a_.shape[1]
        tm = min(kdim, 1024)
        return _mmx(name, (kdim // tm, 1, t // TKK),
                    (a_, (TKK, tm), lambda i, j, kk: (kk, i)),
                    (d, (TKK, D_MODEL), lambda i, j, kk: (kk, 0)),
                    (buf.shape, BF16, (None, tm, D_MODEL), lambda i, j, kk: (li, i, 0)), (0, 0), into=buf)

    dmixed = chunk_dx(tag + "_out_dx", dout, wo, co)
    b_out = full_dw(tag + "_out_dw", mixed, dout, b_out)
    dya, dys, dga, dgs, dba, dbs = _merge_bwd(tag + "_merge_bwd", ya, ys, proj, p["b_gates"], dmixed)
    gr["b_gates"] = jnp.concatenate([dba, dbs], axis=1)
    datt = _mmx(tag + "_attproj_dx", (t // TMM, 1, NCH),
                (dya, (TMM, ca), lambda i, j, k: (i, k)),
                (wa, (None, None, ATT_OUT, ca), lambda i, j, k: (k, li, 0, 0)),
                ((t, ATT_OUT), F32, (TMM, ATT_OUT), lambda i, j, k: (i, 0)), (1, 1))
    b_att = _mmx(tag + "_attproj_dw", (NCH, 1, t // TKK),
                 (att, (TKK, ATT_OUT), lambda k, j, kk: (kk, 0)),
                 (dya, (TKK, ca), lambda k, j, kk: (kk, k)),
                 (b_att.shape, BF16, (None, None, ATT_OUT, ca), lambda k, j, kk: (li, k, 0, 0)), (0, 0), into=b_att)
    dysn = chunk_dx(tag + "_ssmproj_dx", dys, ws, cs)
    b_ssm = full_dw(tag + "_ssmproj_dw", ysn, dys, b_ssm)
    gr["bufs"] = (b_att, b_ssm, b_out)
    dmix = _attmix_bwd(tag + "_attmix_bwd", os_, ls_, datt)
    dqs, dks, dvs = [], [], []
    for g in range(3):
        dq, dk, dv = _att_bwd(f"{tag}_att{g}_bwd", qk, proj, g, dmix[g], dmix[3 + g])
        dqs.append(dq)
        dks.append(dk)
        dvs.append(dv)
    dqk, gr["qkw"] = _qkprep_bwd(tag + "_qk_bwd", proj, p["qkw"], c["cos"], c["sin"], c, dqs, dks)
    dy, dxs_part, dz, gr["d_skip"], gr["ssm_norm_w"] = _ssdpost_bwd(
        tag + "_ssdpost_bwd", y, xc, proj, p["d_skip"], c["ex"], p["ssm_norm_w"], dysn)
    dxs3, ddtb3, ddab3, db3, dc3 = _ssd_bwd(
        tag + "_ssd_bwd", xc3, dtb3, dab3, c["ltri"], states, dy.reshape(bl, SEQ, D_INNER), dxs_part.reshape(bl, SEQ, D_INNER))
    ddt, gr["dt_bias"], gr["a_log"] = _ssdpre_bwd(
        tag + "_ssdpre_bwd", proj, p["dt_bias"], p["a_log"], c["ex"], ddtb3.reshape(t, D_INNER), ddab3.reshape(t, D_INNER))
    dxbc3, dcw0, dcw1, dcw2, dcw3, gr["conv_b"] = _conv_bwd(tag + "_conv_bwd", proj3, p["conv_w"], p["conv_b"], dxs3, db3, dc3)
    gr["conv_w"] = jnp.concatenate([dcw0, dcw1, dcw2, dcw3], axis=0)
    dproj = jnp.concatenate(
        [dqk] + [d.astype(BF16) for d in dvs] + [dz, dga, dgs, dxbc3.reshape(t, XBC), ddt,
                                                  jnp.zeros((t, NP - DT0 - DTW), BF16)], axis=1)
    gr["w_in"] = _mm(tag + "_in_dw", h, dproj, "tn", out_dtype=BF16)
    dh = _mm(tag + "_in_dx", dproj, p["w_in"], "nt")
    dx, gr["mix_norm_w"] = _rmsnorm_bwd(tag + "_norm_bwd", x, p["mix_norm_w"], dh, dout)
    return dx, gr


def _constants():
    cos, sin = _rope_tables()
    return dict(cos=cos, sin=sin, hmean=_head_mean_mat(), tile64=_tile64_mat(),
                ex=_head_expand_mat(), ltri=_ltri_mat())


ANY = pl.BlockSpec(memory_space=pl.ANY)


def _mesh_pos():
    return lax.axis_index("x"), lax.axis_index("y"), lax.axis_index("c")


def _other_chips(x, y):
    return [(1 - x, y), (x, 1 - y), (1 - x, 1 - y)]


def _all_gather_weights(ws, chip_idx):
    n = len(ws)
    hl = ws[0].shape[0] // 2
    inits = [lax.dynamic_update_slice(jnp.zeros((NCH, *w.shape), w.dtype), w[None], (chip_idx[0], 0, 0, 0)) for w in ws]

    def body(*refs):
        srcs, outs = refs[:n], refs[2 * n:3 * n]
        send_sems, recv_sems = refs[3 * n], refs[3 * n + 1]
        x, y, c = _mesh_pos()
        chips = _other_chips(x, y)

        def part(a, chip, hf):
            return outs[a].at[2 * chip[0] + chip[1], pl.ds(hf * hl, hl)]

        def copy(a, k, src_ref, dst_ref, to):
            return pltpu.make_async_remote_copy(src_ref=src_ref, dst_ref=dst_ref, send_sem=send_sems.at[6 * a + k],
                                                recv_sem=recv_sems.at[6 * a + k], device_id=to, device_id_type=MESH)

        first = [copy(a, j, srcs[a].at[pl.ds(c * hl, hl)], part(a, (x, y), c), (*chip, c))
                 for a in range(n) for j, chip in enumerate(chips)]
        for cp in first:
            cp.start()
        passed = []
        for a in range(n):
            for j, chip in enumerate(chips):
                copy(a, j, part(a, chip, c), part(a, chip, c), (x, y, c)).wait_recv()
                fw = copy(a, 3 + j, part(a, chip, c), part(a, chip, c), (x, y, 1 - c))
                fw.start()
                passed.append(fw)
        for a in range(n):
            for j, chip in enumerate(chips):
                copy(a, 3 + j, part(a, chip, 1 - c), part(a, chip, 1 - c), (x, y, c)).wait_recv()
        for cp in first + passed:
            cp.wait_send()

    return pl.pallas_call(
        body, name="all_gather_weights", out_shape=[jax.ShapeDtypeStruct(i.shape, i.dtype) for i in inits],
        in_specs=[ANY] * (2 * n), out_specs=[ANY] * n, input_output_aliases={n + a: a for a in range(n)},
        scratch_shapes=[pltpu.SemaphoreType.DMA((6 * n,)), pltpu.SemaphoreType.DMA((6 * n,))],
    )(*ws, *inits)


def _pair_exchange(gs):
    n = len(gs)

    def body(*refs):
        srcs, outs, send_sems, recv_sems = refs[:n], refs[n:2 * n], refs[2 * n], refs[2 * n + 1]
        x, y, c = _mesh_pos()
        cps = []
        for a in range(n):
            h = gs[a].shape[2] // 2
            cps.append(pltpu.make_async_remote_copy(
                src_ref=srcs[a].at[:, :, pl.ds((1 - c) * h, h), :], dst_ref=outs[a], send_sem=send_sems.at[a],
                recv_sem=recv_sems.at[a], device_id=(x, y, 1 - c), device_id_type=MESH))
        for cp in cps:
            cp.start()
        for cp in cps:
            cp.wait()

    return pl.pallas_call(
        body, name="grad_pair_exchange",
        out_shape=[jax.ShapeDtypeStruct((g.shape[0], g.shape[1], g.shape[2] // 2, g.shape[3]), g.dtype) for g in gs],
        in_specs=[ANY] * n, out_specs=[ANY] * n,
        scratch_shapes=[pltpu.SemaphoreType.DMA((n,)), pltpu.SemaphoreType.DMA((n,))],
    )(*gs)


def _chip_exchange(hs):
    n = len(hs)

    def body(*refs):
        srcs, outs, send_sems, recv_sems = refs[:n], refs[n:2 * n], refs[2 * n], refs[2 * n + 1]
        x, y, c = _mesh_pos()
        cps = [pltpu.make_async_remote_copy(
            src_ref=srcs[a].at[:, 2 * chip[0] + chip[1]], dst_ref=outs[a].at[j], send_sem=send_sems.at[3 * a + j],
            recv_sem=recv_sems.at[3 * a + j], device_id=(*chip, c), device_id_type=MESH)
            for a in range(n) for j, chip in enumerate(_other_chips(x, y))]
        for cp in cps:
            cp.start()
        for cp in cps:
            cp.wait()

    return pl.pallas_call(
        body, name="grad_chip_exchange",
        out_shape=[jax.ShapeDtypeStruct((3, h.shape[0], h.shape[2], h.shape[3]), h.dtype) for h in hs],
        in_specs=[ANY] * n, out_specs=[ANY] * n,
        scratch_shapes=[pltpu.SemaphoreType.DMA((3 * n,)), pltpu.SemaphoreType.DMA((3 * n,))],
    )(*hs)


def _pair_share(rs):
    n = len(rs)

    def body(*refs):
        outs, send_sems, recv_sems = refs[n:2 * n], refs[2 * n], refs[2 * n + 1]
        x, y, c = _mesh_pos()
        cps = [pltpu.make_async_remote_copy(src_ref=outs[a].at[:, c], dst_ref=outs[a].at[:, c], send_sem=send_sems.at[a],
                                            recv_sem=recv_sems.at[a], device_id=(x, y, 1 - c), device_id_type=MESH)
               for a in range(n)]
        for cp in cps:
            cp.start()
        for a in range(n):
            pltpu.make_async_remote_copy(src_ref=outs[a].at[:, 1 - c], dst_ref=outs[a].at[:, 1 - c],
                                         send_sem=send_sems.at[a], recv_sem=recv_sems.at[a], device_id=(x, y, c),
                                         device_id_type=MESH).wait_recv()
        for cp in cps:
            cp.wait_send()

    return pl.pallas_call(
        body, name="grad_pair_share", out_shape=[jax.ShapeDtypeStruct(r.shape, r.dtype) for r in rs],
        in_specs=[ANY] * n, out_specs=[ANY] * n, input_output_aliases={a: a for a in range(n)},
        scratch_shapes=[pltpu.SemaphoreType.DMA((n,)), pltpu.SemaphoreType.DMA((n,))],
    )(*rs)


def _pair_sum(name, g, recv, c_idx):
    d, k, h, b = recv.shape
    g5 = g.reshape(d * k, 2, h, b)

    def body(c_ref, a_ref, b_ref, o_ref):
        o_ref[...] = (a_ref[...].astype(F32) + b_ref[...].astype(F32)).astype(o_ref.dtype)

    out = pl.pallas_call(
        body, name=name,
        grid_spec=pltpu.PrefetchScalarGridSpec(
            num_scalar_prefetch=1, grid=(d * k,),
            in_specs=[pl.BlockSpec((None, None, h, b), lambda i, c: (i, c[0], 0, 0)),
                      pl.BlockSpec((None, h, b), lambda i, c: (i, 0, 0))],
            out_specs=pl.BlockSpec((None, h, b), lambda i, c: (i, 0, 0))),
        out_shape=jax.ShapeDtypeStruct((d * k, h, b), BF16),
        compiler_params=_cp(("arbitrary",)),
    )(c_idx, g5, recv.reshape(d * k, h, b))
    return out.reshape(d, k, h, b)


def _chip_sum(name, ha, recv, chip_idx, c_idx):
    d, _, h, b = ha.shape

    def body(k_ref, c_ref, a_ref, r0, r1, r2, o_ref):
        o_ref[...] = ((a_ref[...].astype(F32) + r0[...].astype(F32)) + r1[...].astype(F32)) + r2[...].astype(F32)

    blk = (None, None, h, b)
    return pl.pallas_call(
        body, name=name,
        grid_spec=pltpu.PrefetchScalarGridSpec(
            num_scalar_prefetch=2, grid=(d,),
            in_specs=[pl.BlockSpec(blk, lambda l, k, c: (l, k[0], 0, 0))] +
                     [pl.BlockSpec(blk, lambda l, k, c, j=j: (j, l, 0, 0)) for j in range(3)],
            out_specs=pl.BlockSpec(blk, lambda l, k, c: (l, c[0], 0, 0))),
        out_shape=jax.ShapeDtypeStruct((d, 2, h, b), F32),
        compiler_params=_cp(("arbitrary",)),
    )(chip_idx, c_idx, ha, recv, recv, recv)


def _all_sum_small(name, vec):
    rows = vec.shape[0]

    def body(v_ref, o_ref, buf, send_sems, recv_sems):
        x, y, c = _mesh_pos()
        me, sibling = (x, y, c), (x, y, 1 - c)
        chips = _other_chips(x, y)

        def slot(p):
            return buf.at[4 * p[0] + 2 * p[1] + p[2]]

        def copy(k, block, to, src=None):
            return pltpu.make_async_remote_copy(src_ref=slot(block) if src is None else src, dst_ref=slot(block),
                                                send_sem=send_sems.at[k], recv_sem=recv_sems.at[k],
                                                device_id=to, device_id_type=MESH)

        first = [copy(0, me, sibling, src=v_ref)]
        first += [copy(1 + j, me, (*chip, c), src=v_ref) for j, chip in enumerate(chips)]
        for cp in first:
            cp.start()
        passed = [copy(4 + j, (*chip, c), sibling) for j, chip in enumerate(chips)]
        for j, chip in enumerate(chips):
            copy(1 + j, (*chip, c), me).wait_recv()
            passed[j].start()
        copy(0, sibling, me).wait_recv()
        for j, chip in enumerate(chips):
            copy(4 + j, (*chip, 1 - c), me).wait_recv()
        for cp in first + passed:
            cp.wait_send()
        slot(me)[...] = v_ref[...]
        acc = buf[0]
        for k in range(1, 8):
            acc = acc + buf[k]
        o_ref[...] = acc

    vm = pl.BlockSpec(memory_space=pltpu.VMEM)
    return pl.pallas_call(
        body, name=name, out_shape=jax.ShapeDtypeStruct((rows, LANES), F32),
        in_specs=[vm], out_specs=vm, compiler_params=pltpu.CompilerParams(vmem_limit_bytes=VMEM_LIMIT),
        scratch_shapes=[pltpu.VMEM((8, rows, LANES), F32), pltpu.SemaphoreType.DMA((7,)), pltpu.SemaphoreType.DMA((7,))],
    )(vec)


def _pad_lanes(v, n=LANES):
    return jnp.pad(v, (0, n - v.shape[0]))[None, :]


def _w_in_to_kernel(w):
    return jnp.concatenate([w[:, :6656], w[:, 9760:N_IN], w[:, 6656:9728], w[:, 9728:9760],
                            jnp.zeros((w.shape[0], NP - N_IN), w.dtype)], axis=1)


def _w_in_from_kernel(w):
    return jnp.concatenate([w[:, :6656], w[:, X0:DT0], w[:, DT0:DT0 + 32], w[:, G0:X0]], axis=1)


def _layer_params(big, small, i):
    p = {k: big[k] for k in ("w_att_proj", "w_ssm_proj", "w_out")}
    p["layer"] = i
    w_in = big["w_in"]
    p["w_in"] = _w_in_to_kernel(jnp.concatenate([w_in[k, i] for k in range(NCH)], axis=1))
    p["conv_w"] = big["conv_w"][i][:, None, :]
    for k in ("ffn1_norm_w", "mix_norm_w", "ffn2_norm_w", "b_gates", "conv_b", "ssm_norm_w"):
        p[k] = small[k][i][None, :]
    for k in ("dt_bias", "a_log", "d_skip"):
        p[k] = _pad_lanes(small[k][i])
    p["qkw"] = jnp.stack([_pad_lanes(small["q_norm_w"][i]), _pad_lanes(small["k_norm_w"][i])])
    return p


GRAD_BUFS = ("ffn1_w_gate", "ffn1_w_up", "ffn1_w_down", "w_att_proj", "w_ssm_proj", "w_out",
             "ffn2_w_gate", "ffn2_w_up", "ffn2_w_down")


def _local_step(x, tgt, layers, big, c):
    depth = len(layers)
    saved = []
    for i, p in enumerate(layers):
        x, s1 = _ffn_fwd(f"L{i}_ffn1", x, p["ffn1_norm_w"], big["ffn1_w_gate"], big["ffn1_w_up"], big["ffn1_w_down"], i)
        x, s2 = _mixer_fwd(f"L{i}_mix", x, p, c)
        x, s3 = _ffn_fwd(f"L{i}_ffn2", x, p["ffn2_norm_w"], big["ffn2_w_gate"], big["ffn2_w_up"], big["ffn2_w_down"], i)
        saved.append((s1, s2, s3))
    dx, loss_blk = _loss("loss", x, tgt)
    buf = {}
    for n in GRAD_BUFS:
        _, _, a, b = big[n].shape
        shape = (depth, NCH * a, b) if n in ("w_ssm_proj", "w_out") else (depth, NCH, a, b)
        buf[n] = jnp.zeros(shape, BF16)
    grads = [None] * depth
    for i in reversed(range(depth)):
        p = layers[i]
        s1, s2, s3 = saved[i]
        dx, dn2, (buf["ffn2_w_gate"], buf["ffn2_w_up"], buf["ffn2_w_down"]) = _ffn_bwd(
            f"L{i}_ffn2", s3, p["ffn2_norm_w"], big["ffn2_w_gate"], big["ffn2_w_up"], big["ffn2_w_down"], i, dx,
            (buf["ffn2_w_gate"], buf["ffn2_w_up"], buf["ffn2_w_down"]))
        dx, gr = _mixer_bwd(f"L{i}_mix", s2, p, c, dx, (buf["w_att_proj"], buf["w_ssm_proj"], buf["w_out"]))
        buf["w_att_proj"], buf["w_ssm_proj"], buf["w_out"] = gr.pop("bufs")
        dx, dn1, (buf["ffn1_w_gate"], buf["ffn1_w_up"], buf["ffn1_w_down"]) = _ffn_bwd(
            f"L{i}_ffn1", s1, p["ffn1_norm_w"], big["ffn1_w_gate"], big["ffn1_w_up"], big["ffn1_w_down"], i, dx,
            (buf["ffn1_w_gate"], buf["ffn1_w_up"], buf["ffn1_w_down"]))
        gr.update(ffn1_norm_w=dn1, ffn2_norm_w=dn2)
        grads[i] = gr
    for n in ("w_ssm_proj", "w_out"):
        _, _, a, b = big[n].shape
        buf[n] = buf[n].reshape(depth, NCH, a, b)
    return loss_blk, dx, grads, buf


WEIGHTS = ["ffn1_norm_w", "ffn1_w_gate", "ffn1_w_up", "ffn1_w_down", "mix_norm_w", "w_in", "b_gates", "q_norm_w",
           "k_norm_w", "conv_w", "conv_b", "dt_bias", "a_log", "d_skip", "ssm_norm_w", "w_att_proj", "w_ssm_proj",
           "w_out", "ffn2_norm_w", "ffn2_w_gate", "ffn2_w_up", "ffn2_w_down"]
SHARD_AXIS = {"ffn1_w_gate": 2, "ffn1_w_up": 2, "ffn1_w_down": 1, "w_in": 2, "conv_w": 2, "w_att_proj": 2,
              "w_ssm_proj": 1, "w_out": 1, "ffn2_w_gate": 2, "ffn2_w_up": 2, "ffn2_w_down": 1}
BIG = [n for n in WEIGHTS if n in SHARD_AXIS]
SMALL = [n for n in WEIGHTS if n not in SHARD_AXIS]
def _from_flat(flat, shapes):
    v = flat.reshape(-1)
    out, off = [], 0
    for s in shapes:
        n = math.prod(s)
        out.append(v[off:off + n].reshape(s))
        off += n
    return out


def _pack_small(parts):
    v = jnp.concatenate([p.astype(F32).reshape(-1) for p in parts])
    rows = -(-v.shape[0] // (8 * LANES)) * 8
    return jnp.pad(v, (0, rows * LANES - v.shape[0])).reshape(rows, LANES)


def kernel(x, ffn1_norm_w, ffn1_w_gate, ffn1_w_up, ffn1_w_down, mix_norm_w, w_in, b_gates, q_norm_w, k_norm_w, conv_w, conv_b, dt_bias, a_log, d_skip, ssm_norm_w, w_att_proj, w_ssm_proj, w_out, ffn2_norm_w, ffn2_w_gate, ffn2_w_up, ffn2_w_down, loss_target, m_ffn1_norm_w, m_ffn1_w_gate, m_ffn1_w_up, m_ffn1_w_down, m_mix_norm_w, m_w_in, m_b_gates, m_q_norm_w, m_k_norm_w, m_conv_w, m_conv_b, m_dt_bias, m_a_log, m_d_skip, m_ssm_norm_w, m_w_att_proj, m_w_ssm_proj, m_w_out, m_ffn2_norm_w, m_ffn2_w_gate, m_ffn2_w_up, m_ffn2_w_down, v_ffn1_norm_w, v_ffn1_w_gate, v_ffn1_w_up, v_ffn1_w_down, v_mix_norm_w, v_w_in, v_b_gates, v_q_norm_w, v_k_norm_w, v_conv_w, v_conv_b, v_dt_bias, v_a_log, v_d_skip, v_ssm_norm_w, v_w_att_proj, v_w_ssm_proj, v_w_out, v_ffn2_norm_w, v_ffn2_w_gate, v_ffn2_w_up, v_ffn2_w_down):
    w = dict(zip(WEIGHTS, (ffn1_norm_w, ffn1_w_gate, ffn1_w_up, ffn1_w_down, mix_norm_w, w_in, b_gates, q_norm_w, k_norm_w, conv_w, conv_b, dt_bias, a_log, d_skip, ssm_norm_w, w_att_proj, w_ssm_proj, w_out, ffn2_norm_w, ffn2_w_gate, ffn2_w_up, ffn2_w_down)))
    m = dict(zip(WEIGHTS, (m_ffn1_norm_w, m_ffn1_w_gate, m_ffn1_w_up, m_ffn1_w_down, m_mix_norm_w, m_w_in, m_b_gates, m_q_norm_w, m_k_norm_w, m_conv_w, m_conv_b, m_dt_bias, m_a_log, m_d_skip, m_ssm_norm_w, m_w_att_proj, m_w_ssm_proj, m_w_out, m_ffn2_norm_w, m_ffn2_w_gate, m_ffn2_w_up, m_ffn2_w_down)))
    v = dict(zip(WEIGHTS, (v_ffn1_norm_w, v_ffn1_w_gate, v_ffn1_w_up, v_ffn1_w_down, v_mix_norm_w, v_w_in, v_b_gates, v_q_norm_w, v_k_norm_w, v_conv_w, v_conv_b, v_dt_bias, v_a_log, v_d_skip, v_ssm_norm_w, v_w_att_proj, v_w_ssm_proj, v_w_out, v_ffn2_norm_w, v_ffn2_w_gate, v_ffn2_w_up, v_ffn2_w_down)))
    depth = ffn1_norm_w.shape[0]
    bl = x.shape[0]
    t = bl * SEQ
    mx, my, mc = lax.axis_index("x"), lax.axis_index("y"), lax.axis_index("c")
    c_idx = mc.astype(jnp.int32).reshape(1)
    chip_idx = (2 * mx + my).astype(jnp.int32).reshape(1)

    cw_width = conv_w.shape[2]
    slots = lax.dynamic_update_slice(jnp.zeros((NCH, *conv_w.shape), F32), jnp.where(mc == 0, conv_w, 0.0)[None],
                                     (chip_idx[0], 0, 0, 0))
    conv_all = _all_sum_small("conv_gather", slots.reshape(-1, LANES)).reshape(NCH, *conv_w.shape)
    big = {"conv_w": jnp.concatenate([conv_all[k] for k in range(NCH)], axis=2)}

    mm_names = [n for n in BIG if n != "conv_w"]
    big.update(zip(mm_names, _all_gather_weights([w[n].astype(BF16) for n in mm_names], chip_idx)))
    small = {n: w[n] for n in SMALL}

    c = _constants()
    layers = [_layer_params(big, small, i) for i in range(depth)]
    loss_blk, dx, grads, buf = _local_step(x.reshape(t, D_MODEL), loss_target.reshape(t, D_MODEL), layers, big, c)
    grad_x = dx.reshape(bl, SEQ, D_MODEL)

    buf["w_in"] = jnp.stack([_w_in_from_kernel(g["w_in"]).reshape(D_MODEL, NCH, -1).transpose(1, 0, 2) for g in grads])
    mine = [buf[n] for n in mm_names]
    from_sibling = _pair_exchange(mine)
    pairs = [_pair_sum("grad_pair_sum_" + n, g, r, c_idx) for n, g, r in zip(mm_names, mine, from_sibling)]
    from_chips = _chip_exchange(pairs)
    halves = [_chip_sum("grad_chip_sum_" + n, h, r, chip_idx, c_idx) for n, h, r in zip(mm_names, pairs, from_chips)]
    g_big = {n: r.reshape(w[n].shape) for n, r in zip(mm_names, _pair_share(halves))}

    def small_grad(n):
        if n == "q_norm_w":
            return jnp.stack([g["qkw"][0, 0, :64] for g in grads])
        if n == "k_norm_w":
            return jnp.stack([g["qkw"][1, 0, :64] for g in grads])
        return jnp.stack([g[n][0, :w[n].shape[1]] for g in grads])

    small_shapes = [w[n].shape for n in SMALL]
    conv_shape = (depth, conv_w.shape[1], NCH * cw_width)
    tot = _all_sum_small("small_all_sum", _pack_small(
        [small_grad(n) for n in SMALL] + [jnp.stack([g["conv_w"] for g in grads]), loss_blk[0, :1]]))
    unpacked = _from_flat(tot, small_shapes + [conv_shape, (1,)])
    g_small = dict(zip(SMALL, unpacked[:-2]))
    g_big["conv_w"] = lax.dynamic_slice_in_dim(unpacked[-2], chip_idx[0] * cw_width, cw_width, axis=2)
    loss = unpacked[-1][0]

    grad, delta, new_m, new_v = {}, {}, {}, {}
    for n in BIG:
        shp = w[n].shape
        two_d = (shp[0] * shp[1], shp[2])
        d_, m_, v_ = _adamw("adamw_" + n, w[n].reshape(two_d), g_big[n].reshape(two_d), m[n].reshape(two_d), v[n].reshape(two_d))
        grad[n], delta[n], new_m[n], new_v[n] = g_big[n], d_.reshape(shp), m_.reshape(shp), v_.reshape(shp)
    d_, m_, v_ = _adamw("adamw_small", _pack_small([w[n] for n in SMALL]), _pack_small([g_small[n] for n in SMALL]),
                        _pack_small([m[n] for n in SMALL]), _pack_small([v[n] for n in SMALL]))
    for n, a, b, c_ in zip(SMALL, _from_flat(d_, small_shapes), _from_flat(m_, small_shapes), _from_flat(v_, small_shapes)):
        grad[n], delta[n], new_m[n], new_v[n] = g_small[n], a, b, c_
    return (loss, grad_x, *[grad[n] for n in WEIGHTS], *[delta[n] for n in WEIGHTS],
            *[new_m[n] for n in WEIGHTS], *[new_v[n] for n in WEIGHTS])
```

```python
import functools
import math

import numpy as np
import jax
import jax.numpy as jnp
from jax import lax
from jax.experimental import pallas as pl
from jax.experimental.pallas import tpu as pltpu

F32 = jnp.float32
BF16 = jnp.bfloat16
HI = lax.Precision.HIGHEST
MESH = pl.DeviceIdType.MESH

D_MODEL = 1024
SEQ = 2048
DEPTH = 4
D_FF = 2816
ATT_DILATIONS = (1, 4, 16)
BAND = 128
ATT_OUT = 512
QKV = 1536
D_INNER = 2048
N_SSM_HEADS = 32
N_SSM_GROUPS = 4
D_STATE = 128
XBC = 3072
CHUNK = 128
N_IN = 11808
EPS = 1e-6
ROPE_THETA = 10000.0
NP = 12288
Q0, K0, V0, Z0, G0, X0, DT0 = 0, 1536, 3072, 4608, 6656, 8704, 11776
DTW = 128
LR, B1, B2, ADAM_EPS, WD, STEP = 0.001, 0.9, 0.999, 1e-08, 0.01, 10

LANES = 128
VMEM_LIMIT = 48 * 1024 * 1024
NEG = -1e30


def _cp(sem=None, **kw):
    return pltpu.CompilerParams(dimension_semantics=sem, vmem_limit_bytes=VMEM_LIMIT, **kw)


def _dg(a, b, ca, cb):
    return lax.dot_general(a.astype(BF16), b.astype(BF16), (((ca,), (cb,)), ((), ())), preferred_element_type=F32)


@jax.custom_vjp
def dot_nn(a, b):
    return _dg(a, b, 1, 0)


def _dot_nn_fwd(a, b):
    return _dg(a, b, 1, 0), (a, b)


def _dot_nn_bwd(r, g):
    a, b = r
    return _dg(g, b, 1, 1).astype(a.dtype), _dg(a, g, 0, 0).astype(b.dtype)


dot_nn.defvjp(_dot_nn_fwd, _dot_nn_bwd)


@jax.custom_vjp
def dot_nt(a, b):
    return _dg(a, b, 1, 1)


def _dot_nt_fwd(a, b):
    return _dg(a, b, 1, 1), (a, b)


def _dot_nt_bwd(r, g):
    a, b = r
    return _dg(g, b, 1, 0).astype(a.dtype), _dg(g, a, 0, 0).astype(b.dtype)


dot_nt.defvjp(_dot_nt_fwd, _dot_nt_bwd)


@jax.custom_vjp
def dot_tn(a, b):
    return _dg(a, b, 0, 0)


def _dot_tn_fwd(a, b):
    return _dg(a, b, 0, 0), (a, b)


def _dot_tn_bwd(r, g):
    a, b = r
    return _dg(b, g, 1, 1).astype(a.dtype), _dg(a, g, 1, 0).astype(b.dtype)


dot_tn.defvjp(_dot_tn_fwd, _dot_tn_bwd)


def dot_hi(a, b):
    return jnp.dot(a, b, precision=HI, preferred_element_type=F32)


def _dot2_raw(a, e, ce):
    hi = a.astype(BF16)
    lo = (a - hi.astype(F32)).astype(BF16)
    return _dg(hi, e, 1, ce) + _dg(lo, e, 1, ce)


@jax.custom_vjp
def dot2(a, e):
    return _dot2_raw(a, e, 0)


def _dot2_fwd(a, e):
    return _dot2_raw(a, e, 0), e


def _dot2_bwd(e, g):
    return _dot2_raw(g, e, 1), jnp.zeros_like(e)


dot2.defvjp(_dot2_fwd, _dot2_bwd)


def _rope_rot_raw(y, sign):
    lane = lax.broadcasted_iota(jnp.int32, y.shape, 1)
    first_half = (lane & 32) == 0
    return sign * jnp.where(first_half, -pltpu.roll(y, LANES - 32, 1), pltpu.roll(y, 32, 1))


@jax.custom_vjp
def rope_rot(y):
    return _rope_rot_raw(y, 1.0)


def _rope_rot_fwd(y):
    return _rope_rot_raw(y, 1.0), None


def _rope_rot_bwd(_, g):
    return (_rope_rot_raw(g, -1.0),)


rope_rot.defvjp(_rope_rot_fwd, _rope_rot_bwd)


def _shift_rows_raw(x, s):
    n = x.shape[0]
    r = pltpu.roll(x, s % n, 0)
    rows = lax.broadcasted_iota(jnp.int32, x.shape, 0)
    keep = rows >= s if s > 0 else rows < n + s
    return jnp.where(keep, r, 0.0)


@functools.partial(jax.custom_vjp, nondiff_argnums=(1,))
def shift_rows(x, s):
    return _shift_rows_raw(x, s)


def _shift_fwd(x, s):
    return _shift_rows_raw(x, s), None


def _shift_bwd(s, _, g):
    return (_shift_rows_raw(g, -s),)


shift_rows.defvjp(_shift_fwd, _shift_bwd)


def _sigmoid(x):
    return 1.0 / (1.0 + jnp.exp(-x))


def _silu(x):
    return x * _sigmoid(x)


def _softplus(x):
    return jnp.maximum(x, 0.0) + jnp.log(1.0 + jnp.exp(-jnp.abs(x)))


def _head_mean_mat():
    i = np.arange(LANES)
    return jnp.asarray((i[:, None] // 64 == i[None, :] // 64).astype(np.float32) / 64.0)


def _tile64_mat():
    t = np.zeros((LANES, LANES), np.float32)
    for l in range(LANES):
        t[l % 64, l] = 1.0
    return jnp.asarray(t)


def _head_expand_mat():
    e = np.zeros((LANES, D_INNER), np.float32)
    for l in range(D_INNER):
        e[l // 64, l] = 1.0
    return jnp.asarray(e)


def _ltri_mat():
    i = np.arange(CHUNK)
    return jnp.asarray((i[:, None] >= i[None, :]).astype(np.float32))


def _rope_tables():
    pos = jnp.arange(SEQ, dtype=F32)
    inv_freq = 1.0 / (ROPE_THETA ** (jnp.arange(0, 64, 2, dtype=F32) / 64))
    ang = pos[:, None] * inv_freq[None, :]
    return jnp.tile(jnp.cos(ang), (1, 4)), jnp.tile(jnp.sin(ang), (1, 4))


def _pick(n, cap):
    best = None
    for t in range(LANES, min(n, cap) + 1, LANES):
        if n % t == 0:
            best = t
    return best if best is not None else n


def _mm(name, a, b, mode, out_dtype=F32, alpha=None, res=None):
    if mode == "nn":
        (m, k), n = a.shape, b.shape[1]
    elif mode == "nt":
        (m, k), n = a.shape, b.shape[0]
    else:
        (k, m), n = a.shape, b.shape[1]
    tm, tn, tk = _pick(m, 1408), _pick(n, 1408), _pick(k, 1024)
    nk = k // tk
    ca, cb = {"nn": (1, 0), "nt": (1, 1), "tn": (0, 0)}[mode]
    a_spec = pl.BlockSpec((tk, tm), lambda i, j, kk: (kk, i)) if mode == "tn" else pl.BlockSpec((tm, tk), lambda i, j, kk: (i, kk))
    b_spec = pl.BlockSpec((tn, tk), lambda i, j, kk: (j, kk)) if mode == "nt" else pl.BlockSpec((tk, tn), lambda i, j, kk: (kk, j))
    o_spec = pl.BlockSpec((tm, tn), lambda i, j, kk: (i, j))
    has_res = res is not None

    def finish(acc, res_ref, o_ref):
        if alpha is not None:
            acc = acc * alpha
        if has_res:
            acc = acc + res_ref[...].astype(F32)
        o_ref[...] = acc.astype(o_ref.dtype)

    def body(*refs):
        a_ref, b_ref = refs[0], refs[1]
        res_ref = refs[2] if has_res else None
        o_ref = refs[3] if has_res else refs[2]
        part = _dg(a_ref[...], b_ref[...], ca, cb)
        if nk == 1:
            finish(part, res_ref, o_ref)
            return
        acc_ref = refs[-1]
        kk = pl.program_id(2)

        @pl.when(kk == 0)
        def _():
            acc_ref[...] = part

        @pl.when(kk > 0)
        def _():
            acc_ref[...] += part

        @pl.when(kk == nk - 1)
        def _():
            finish(acc_ref[...], res_ref, o_ref)

    ins = [a, b] + ([res] if has_res else [])
    in_specs = [a_spec, b_spec] + ([o_spec] if has_res else [])
    return pl.pallas_call(
        body, name=name, grid=(m // tm, n // tn, nk), in_specs=in_specs, out_specs=o_spec,
        out_shape=jax.ShapeDtypeStruct((m, n), out_dtype),
        scratch_shapes=[pltpu.VMEM((tm, tn), F32)] if nk > 1 else [],
        compiler_params=_cp(("parallel", "parallel", "arbitrary")),
    )(*ins)


def _mmx(name, grid, a, b, out, contract, *, alpha=None, res=None, into=None):
    nk = grid[-1]
    has_res, has_into = res is not None, into is not None
    n_in = 2 + has_res + has_into

    def finish(acc, res_ref, o_ref):
        if alpha is not None:
            acc = acc * alpha
        if has_res:
            acc = acc + res_ref[...].astype(F32)
        o_ref[...] = acc.astype(o_ref.dtype)

    def body(*refs):
        res_ref = refs[2] if has_res else None
        o_ref = refs[n_in]
        part = _dg(refs[0][...], refs[1][...], *contract)
        if nk == 1:
            finish(part, res_ref, o_ref)
            return
        acc_ref = refs[-1]
        kk = pl.program_id(len(grid) - 1)

        @pl.when(kk == 0)
        def _():
            acc_ref[...] = part

        @pl.when(kk > 0)
        def _():
            acc_ref[...] += part

        @pl.when(kk == nk - 1)
        def _():
            finish(acc_ref[...], res_ref, o_ref)

    operands = [a, b] + ([res] if has_res else [])
    in_specs = [pl.BlockSpec(blk, im) for _, blk, im in operands] + ([ANY] if has_into else [])
    acc_shape = tuple(d for d in out[2] if d is not None)
    return pl.pallas_call(
        body, name=name, grid=grid, in_specs=in_specs, out_specs=pl.BlockSpec(out[2], out[3]),
        out_shape=jax.ShapeDtypeStruct(out[0], out[1]),
        scratch_shapes=[pltpu.VMEM(acc_shape, F32)] if nk > 1 else [],
        input_output_aliases={n_in - 1: 0} if has_into else {},
        compiler_params=_cp(("parallel",) * (len(grid) - 1) + ("arbitrary",)),
    )(*[o[0] for o in operands], *([into] if has_into else []))


def _ew(name, fn, grid, ins, outs, scratch=()):
    n_in, n_out = len(ins), len(outs)

    def body(*refs):
        vals = [r[...] for r in refs[:n_in]]
        res = fn(*vals, *refs[n_in + n_out:])
        for r, v in zip(refs[n_in:n_in + n_out], res):
            r[...] = v.astype(r.dtype)

    res = pl.pallas_call(
        body, name=name, grid=grid,
        in_specs=[pl.BlockSpec(b, m) for _, b, m in ins],
        out_specs=[pl.BlockSpec(b, m) for _, _, b, m in outs],
        out_shape=[jax.ShapeDtypeStruct(s, d) for s, d, _, _ in outs],
        scratch_shapes=list(scratch),
        compiler_params=_cp(("arbitrary",) * len(grid)),
    )(*[a for a, _, _ in ins])
    return res


def _ew_bwd(name, fn, grid, ins, cts, wrt, adds=(), ct_fn=None):
    n_in, n_ct, n_add = len(ins), len(cts), len(adds)
    idxs = [w["idx"] for w in wrt]
    intos = [(k, w["into"]) for k, w in enumerate(wrt) if w.get("into") is not None]

    def body(*refs):
        prim = [r[...] for r in refs[:n_in]]
        ct = [r[...].astype(F32) for r in refs[n_in:n_in + n_ct]]
        addv = [r[...] for r in refs[n_in + n_ct:n_in + n_ct + n_add]]
        orefs = refs[n_in + n_ct + n_add + len(intos):]

        def f(*sel):
            full = list(prim)
            for i, s in zip(idxs, sel):
                full[i] = s
            return fn(*full)

        _, vjp = jax.vjp(f, *[prim[i].astype(F32) for i in idxs])
        grads = vjp(tuple(ct) if ct_fn is None else ct_fn(*ct))
        for w, g, r in zip(wrt, grads, orefs):
            if w["kind"] == "tile":
                if w.get("add") is not None:
                    g = g + addv[w["add"]].astype(F32)
                r[...] = g.astype(r.dtype)
            else:
                first = w["first"]()

                @pl.when(first)
                def _(r=r, g=g):
                    r[...] = g.astype(r.dtype)

                @pl.when(jnp.logical_not(first))
                def _(r=r, g=g):
                    r[...] += g.astype(r.dtype)

    allin = list(ins) + list(cts) + list(adds)
    return pl.pallas_call(
        body, name=name, grid=grid,
        in_specs=[pl.BlockSpec(b, m) for _, b, m in allin] + [ANY] * len(intos),
        out_specs=[pl.BlockSpec(w["block"], w["imap"]) for w in wrt],
        out_shape=[jax.ShapeDtypeStruct(w["shape"], w["dtype"]) for w in wrt],
        input_output_aliases={len(allin) + q: k for q, (k, _) in enumerate(intos)},
        compiler_params=_cp(("arbitrary",) * len(grid)),
    )(*[a for a, _, _ in allin], *[a for _, a in intos])


def _rmsnorm_fn(x, w):
    return (x * lax.rsqrt(jnp.mean(x * x, axis=-1, keepdims=True) + EPS) * w,)


def _swiglu_fn(g, u):
    return (_silu(g) * u,)


def _qkprep_fn(t, w64, cos, sin, hmean, tile64):
    w = dot_hi(jnp.broadcast_to(w64, (8, LANES)), tile64)
    w = jnp.sum(w, axis=0, keepdims=True) * 0.125
    y = t * lax.rsqrt(dot2(t * t, hmean) + EPS) * w
    return (y * cos + rope_rot(y) * sin,)


def _att_fn(q, kp, kc, vp, vc, first):
    iq = lax.broadcasted_iota(jnp.int32, (BAND, 2 * BAND), 0)
    ik = lax.broadcasted_iota(jnp.int32, (BAND, 2 * BAND), 1)
    rel = BAND + iq - ik
    ok = (rel >= 0) & (rel <= BAND) & ((ik >= BAND) | jnp.logical_not(first))
    lane = lax.broadcasted_iota(jnp.int32, (1, LANES), 1)
    kcat = jnp.concatenate([kp, kc], axis=0)
    vcat = jnp.concatenate([vp, vc], axis=0)
    o_pair = jnp.zeros((BAND, LANES), F32)
    l_pair = jnp.zeros((BAND, LANES), F32)
    for hh in range(2):
        lm = (lane // 64 == hh).astype(F32)
        s = dot_nt(q * lm, kcat) * 0.125
        s = jnp.where(ok, s, NEG)
        mx = jnp.max(s, axis=-1, keepdims=True)
        e = jnp.exp(s - mx)
        den = jnp.sum(e, axis=-1, keepdims=True)
        o_pair = o_pair + dot_nn(e / den, vcat) * lm
        l_pair = l_pair + (mx + jnp.log(den)) * lm
    return o_pair, l_pair


def _attmix_fn(o0, o1, o2, l0, l1, l2):
    m = jnp.maximum(jnp.maximum(l0, l1), l2)
    e0, e1, e2 = jnp.exp(l0 - m), jnp.exp(l1 - m), jnp.exp(l2 - m)
    return ((e0 * o0 + e1 * o1 + e2 * o2) / (e0 + e1 + e2),)


def _conv_fn(x, w0, w1, w2, w3, b):
    pre = x * w3 + shift_rows(x, 1) * w2 + shift_rows(x, 2) * w1 + shift_rows(x, 3) * w0 + b
    return (_silu(pre),)


def _ssdpre_fn(dtraw, bias, alog, ex):
    dt = _softplus(dtraw + bias)
    da = dt * (-jnp.exp(alog))
    return dot2(dt, ex), dot2(da, ex)


def _ssd_step(st, x, dtb, dab, bm, cm, ltri):
    cum = dot_hi(ltri, dab)
    cum_t = cum.T
    xdt = x * dtb
    cb = dot_nt(cm, bm)
    ri = lax.broadcasted_iota(jnp.int32, (CHUNK, CHUNK), 0)
    ci = lax.broadcasted_iota(jnp.int32, (CHUNK, CHUNK), 1)
    causal = ri >= ci
    lane = lax.broadcasted_iota(jnp.int32, (1, LANES), 1)
    rowi = lax.broadcasted_iota(jnp.int32, (LANES, 1), 0)
    ys = []
    for p in range(4):
        sl = slice(p * LANES, (p + 1) * LANES)
        cum_p, cum_tp, xdt_p = cum[:, sl], cum_t[sl, :], xdt[:, sl]
        acc = jnp.zeros((CHUNK, LANES), F32)
        for hh in range(2):
            col = jnp.sum(cum_p * (lane == 64 * hh).astype(F32), axis=1, keepdims=True)
            row = jnp.sum(cum_tp * (rowi == 64 * hh).astype(F32), axis=0, keepdims=True)
            dec = jnp.exp(jnp.where(causal, col - row, NEG))
            acc = acc + dot_nn(cb * dec, xdt_p * (lane // 64 == hh).astype(F32))
        ys.append(acc)
    y_diag = jnp.concatenate(ys, axis=1)
    y_off = dot_nn(cm, st) * jnp.exp(cum)
    last_row = (lax.broadcasted_iota(jnp.int32, (CHUNK, 1), 0) == CHUNK - 1).astype(F32)
    last = jnp.sum(cum * last_row, axis=0, keepdims=True)
    new_st = st * jnp.exp(last) + dot_tn(bm, xdt * jnp.exp(last - cum))
    return new_st, y_diag + y_off


def _ssdpost_fn(y, xs, z, dskip, ex, nw):
    db = jnp.sum(dot_hi(jnp.broadcast_to(dskip, (8, LANES)), ex), axis=0, keepdims=True) * 0.125
    y2 = (y + db * xs) * _silu(z)
    return (y2 * lax.rsqrt(jnp.mean(y2 * y2, axis=-1, keepdims=True) + EPS) * nw,)


def _merge_fn(ya, ys, ga, gs, ba, bs):
    return (_sigmoid(ga + ba) * ya + _sigmoid(gs + bs) * ys,)


TM = 512


def _full(shape):
    nd = len(shape)
    return (shape, lambda *_: (0,) * nd)


def _rmsnorm(name, x, w):
    t = x.shape[0]
    return _ew(name, _rmsnorm_fn, (t // TM,),
               [(x, (TM, D_MODEL), lambda i: (i, 0)), (w, (1, D_MODEL), lambda i: (0, 0))],
               [((t, D_MODEL), BF16, (TM, D_MODEL), lambda i: (i, 0))])[0]


def _rmsnorm_bwd(name, x, w, dh, dres):
    t = x.shape[0]
    row = ((TM, D_MODEL), lambda i: (i, 0))
    return _ew_bwd(name, _rmsnorm_fn, (t // TM,),
                   [(x, *row), (w, (1, D_MODEL), lambda i: (0, 0))], [(dh, *row)],
                   [dict(idx=0, kind="tile", shape=(t, D_MODEL), dtype=F32, block=row[0], imap=row[1], add=0),
                    dict(idx=1, kind="acc", shape=(1, D_MODEL), dtype=F32, block=(1, D_MODEL), imap=lambda i: (0, 0),
                         first=lambda: pl.program_id(0) == 0)],
                   adds=[(dres, *row)])


def _swiglu(name, g, u):
    r, cols = g.shape
    blk = ((TM, cols), lambda i: (i, 0))
    return _ew(name, _swiglu_fn, (r // TM,), [(g, *blk), (u, *blk)], [((r, cols), BF16, *blk)])[0]


def _swiglu_bwd(name, g, u, da):
    r, cols = g.shape
    blk = ((TM, cols), lambda i: (i, 0))
    return _ew_bwd(name, _swiglu_fn, (r // TM,), [(g, *blk), (u, *blk)], [(da, *blk)],
                   [dict(idx=0, kind="tile", shape=(r, cols), dtype=BF16, block=blk[0], imap=blk[1]),
                    dict(idx=1, kind="tile", shape=(r, cols), dtype=BF16, block=blk[0], imap=blk[1])])


def _qk_operands(proj, qkw, cos, sin, consts, tm):
    nrow = SEQ // tm
    c = ((LANES, LANES), lambda j, i: (0, 0))
    return [(proj, (tm, LANES), lambda j, i: (i, j)),
            (qkw, (None, 1, LANES), lambda j, i: (j // 12, 0, 0)),
            (cos, (tm, LANES), lambda j, i: (i % nrow, 0)),
            (sin, (tm, LANES), lambda j, i: (i % nrow, 0)),
            (consts["hmean"], *c), (consts["tile64"], *c)]


def _qkprep(name, proj, qkw, cos, sin, consts):
    t = proj.shape[0]
    return _ew(name, _qkprep_fn, (2 * QKV // LANES, t // TM), _qk_operands(proj, qkw, cos, sin, consts, TM),
               [((t, 2 * QKV), F32, (TM, LANES), lambda j, i: (i, j))])[0]


def _qkprep_bwd(name, proj, qkw, cos, sin, consts, dq, dk, dproj):
    t = proj.shape[0]
    nq = QKV // LANES

    def pick(cq, ck):
        return (jnp.where(pl.program_id(0) < nq, cq, ck),)

    return _ew_bwd(name, _qkprep_fn, (2 * nq, t // TM), _qk_operands(proj, qkw, cos, sin, consts, TM),
                   [(d, (TM, LANES), lambda j, i: (i, j % nq)) for d in (dq, dk)],
                   [dict(idx=0, kind="tile", shape=dproj.shape, dtype=dproj.dtype, block=(TM, LANES),
                         imap=lambda j, i: (i, j), into=dproj),
                    dict(idx=1, kind="acc", shape=(2, 1, LANES), dtype=F32, block=(None, 1, LANES),
                         imap=lambda j, i: (j // 12, 0, 0),
                         first=lambda: (pl.program_id(0) % 12 == 0) & (pl.program_id(1) == 0))],
                   ct_fn=pick)


def _att_specs(dil, g):
    nb = SEQ // dil // BAND
    pt = 4 if dil == 1 else 1
    w = pt * LANES
    blk = (None, BAND * dil, w)
    kq, kk, kv = g * ATT_OUT // w, (QKV + g * ATT_OUT) // w, (V0 + g * ATT_OUT) // w

    def cur(n):
        return jnp.minimum(n, nb - 1)

    def prev(n):
        return jnp.maximum(jnp.minimum(n, nb - 1) - 1, 0)

    return nb, pt, blk, [
        pl.BlockSpec(blk, lambda b, p, n: (b, cur(n), kq + p)),
        pl.BlockSpec(blk, lambda b, p, n: (b, prev(n), kk + p)),
        pl.BlockSpec(blk, lambda b, p, n: (b, cur(n), kk + p)),
        pl.BlockSpec(blk, lambda b, p, n: (b, prev(n), kv + p)),
        pl.BlockSpec(blk, lambda b, p, n: (b, cur(n), kv + p)),
    ]


def _att_rows(r, dil):
    return pl.ds(r, BAND, stride=dil) if dil > 1 else pl.ds(0, BAND)


def _att_fwd(name, qk, proj, g):
    bl = qk.shape[0] // SEQ
    dil = ATT_DILATIONS[g]
    nb, pt, blk, specs = _att_specs(dil, g)
    qk3 = qk.reshape(bl, SEQ, 2 * QKV)
    proj3 = proj.reshape(bl, SEQ, NP)
    o_spec = pl.BlockSpec(blk, lambda b, p, n: (b, n, p))

    def body(q, kp, kc, vp, vc, o_ref, l_ref):
        first = pl.program_id(2) == 0

        def residue(r, carry):
            sl = _att_rows(r, dil)
            for p in range(pt):
                ln = pl.ds(p * LANES, LANES)
                o, l = _att_fn(q[sl, ln], kp[sl, ln], kc[sl, ln], vp[sl, ln], vc[sl, ln], first)
                o_ref[sl, ln] = o
                l_ref[sl, ln] = l
            return carry

        lax.fori_loop(0, dil, residue, 0)

    o, l = pl.pallas_call(
        body, name=name, grid=(bl, ATT_OUT // (pt * LANES), nb), in_specs=specs, out_specs=[o_spec, o_spec],
        out_shape=[jax.ShapeDtypeStruct((bl, SEQ, ATT_OUT), F32)] * 2,
        compiler_params=_cp(("arbitrary",) * 3),
    )(qk3, qk3, qk3, proj3, proj3)
    return o.reshape(bl * SEQ, ATT_OUT), l.reshape(bl * SEQ, ATT_OUT)


def _att_bwd(name, qk, proj, g, do, dl, dq_buf, dk_buf, dv_buf):
    bl = qk.shape[0] // SEQ
    dil = ATT_DILATIONS[g]
    nb, pt, blk, specs = _att_specs(dil, g)
    w = pt * LANES
    qk3 = qk.reshape(bl, SEQ, 2 * QKV)
    proj3 = proj.reshape(bl, SEQ, NP)
    ct_spec = pl.BlockSpec(blk, lambda b, p, n: (b, jnp.minimum(n, nb - 1), p))
    do3 = do.reshape(bl, SEQ, ATT_OUT)
    dl3 = dl.reshape(bl, SEQ, ATT_OUT)
    kg = g * ATT_OUT // w

    def body(q, kp, kc, vp, vc, do_ref, dl_ref, _a, _b, _c, d_ref, dk_ref, dv_ref, ck, cv):
        n = pl.program_id(2)

        def residue(r, carry):
            sl = _att_rows(r, dil)
            for p in range(pt):
                ln = pl.ds(p * LANES, LANES)

                @pl.when(n < nb)
                def _(ln=ln):
                    first = n == 0
                    prim = [ref[sl, ln] for ref in (q, kp, kc, vp, vc)]
                    _, vjp = jax.vjp(lambda *a: _att_fn(*a, first), *prim)
                    dq, dkp, dkc, dvp, dvc = vjp((do_ref[sl, ln], dl_ref[sl, ln]))
                    d_ref[sl, ln] = dq

                    @pl.when(n > 0)
                    def _():
                        dk_ref[sl, ln] = ck[sl, ln] + dkp
                        dv_ref[sl, ln] = cv[sl, ln] + dvp

                    ck[sl, ln] = dkc
                    cv[sl, ln] = dvc

                @pl.when(n == nb)
                def _(ln=ln):
                    dk_ref[sl, ln] = ck[sl, ln]
                    dv_ref[sl, ln] = cv[sl, ln]

            return carry

        lax.fori_loop(0, dil, residue, 0)

    bufs = [a.reshape(bl, SEQ, QKV) for a in (dq_buf, dk_buf, dv_buf)]
    o_specs = [
        pl.BlockSpec(blk, lambda b, p, n: (b, jnp.minimum(n, nb - 1), kg + p)),
        pl.BlockSpec(blk, lambda b, p, n: (b, jnp.maximum(n - 1, 0), kg + p)),
        pl.BlockSpec(blk, lambda b, p, n: (b, jnp.maximum(n - 1, 0), kg + p)),
    ]
    dq, dk, dv = pl.pallas_call(
        body, name=name, grid=(bl, ATT_OUT // w, nb + 1), in_specs=specs + [ct_spec, ct_spec, ANY, ANY, ANY],
        out_specs=o_specs, out_shape=[jax.ShapeDtypeStruct(a.shape, a.dtype) for a in bufs],
        input_output_aliases={7: 0, 8: 1, 9: 2},
        scratch_shapes=[pltpu.VMEM((BAND * dil, w), F32), pltpu.VMEM((BAND * dil, w), F32)],
        compiler_params=_cp(("arbitrary",) * 3),
    )(qk3, qk3, qk3, proj3, proj3, do3, dl3, *bufs)
    return dq.reshape(dq_buf.shape), dk.reshape(dk_buf.shape), dv.reshape(dv_buf.shape)


def _attmix(name, os_, ls_):
    t = os_[0].shape[0]
    blk = ((TM, ATT_OUT), lambda i: (i, 0))
    return _ew(name, _attmix_fn, (t // TM,), [(a, *blk) for a in (*os_, *ls_)], [((t, ATT_OUT), BF16, *blk)])[0]


def _attmix_bwd(name, os_, ls_, datt):
    t = os_[0].shape[0]
    blk = ((TM, ATT_OUT), lambda i: (i, 0))
    return _ew_bwd(name, _attmix_fn, (t // TM,), [(a, *blk) for a in (*os_, *ls_)], [(datt, *blk)],
                   [dict(idx=k, kind="tile", shape=(t, ATT_OUT), dtype=F32, block=blk[0], imap=blk[1]) for k in range(6)])


CONV_TC = 256


def _conv_operands(proj3, conv_w, conv_b):
    c0 = X0 // CONV_TC
    ins = [(proj3, (None, SEQ, CONV_TC), lambda j, b: (b, 0, c0 + j))]
    for k in range(4):
        ins.append((conv_w, (None, 1, CONV_TC), lambda j, b, k=k: (k, 0, j)))
    ins.append((conv_b, (1, CONV_TC), lambda j, b: (0, j)))
    return ins


def _conv(name, proj3, conv_w, conv_b):
    bl = proj3.shape[0]
    return _ew(name, _conv_fn, (XBC // CONV_TC, bl), _conv_operands(proj3, conv_w, conv_b),
               [((bl, SEQ, XBC), F32, (None, SEQ, CONV_TC), lambda j, b: (b, 0, j))])[0]


def _conv_bwd(name, proj3, conv_w, conv_b, dxs3, db3, dc3, dproj3):
    bl = proj3.shape[0]
    nx = D_INNER // CONV_TC
    nb_ = N_SSM_GROUPS * D_STATE // CONV_TC
    blk = (None, SEQ, CONV_TC)
    cts = [(dxs3, blk, lambda j, b: (b, 0, jnp.minimum(j, nx - 1))),
           (db3, blk, lambda j, b: (b, 0, jnp.clip(j - nx, 0, nb_ - 1))),
           (dc3, blk, lambda j, b: (b, 0, jnp.clip(j - nx - nb_, 0, nb_ - 1)))]

    def pick(cx, cb, cc):
        j = pl.program_id(0)
        return (jnp.where(j < nx, cx, jnp.where(j < nx + nb_, cb, cc)),)

    first = lambda: pl.program_id(1) == 0
    wrt = [dict(idx=0, kind="tile", shape=dproj3.shape, dtype=dproj3.dtype, block=blk,
                imap=lambda j, b: (b, 0, X0 // CONV_TC + j), into=dproj3)]
    for k in range(4):
        wrt.append(dict(idx=1 + k, kind="acc", shape=(1, XBC), dtype=F32, block=(1, CONV_TC),
                        imap=lambda j, b: (0, j), first=first))
    wrt.append(dict(idx=5, kind="acc", shape=(1, XBC), dtype=F32, block=(1, CONV_TC), imap=lambda j, b: (0, j), first=first))
    return _ew_bwd(name, _conv_fn, (XBC // CONV_TC, bl), _conv_operands(proj3, conv_w, conv_b), cts, wrt, ct_fn=pick)


SSD_TM = 256


def _ssdpre_operands(proj, dt_bias, a_log, ex):
    return [(proj, (SSD_TM, DTW), lambda i: (i, DT0 // DTW)), (dt_bias, *_full((1, DTW))), (a_log, *_full((1, DTW))),
            (ex, *_full((LANES, D_INNER)))]


def _ssdpre(name, proj, dt_bias, a_log, ex):
    t = proj.shape[0]
    blk = ((SSD_TM, D_INNER), lambda i: (i, 0))
    return _ew(name, _ssdpre_fn, (t // SSD_TM,), _ssdpre_operands(proj, dt_bias, a_log, ex),
               [((t, D_INNER), F32, *blk), ((t, D_INNER), F32, *blk)])


def _ssdpre_bwd(name, proj, dt_bias, a_log, ex, ddtb, ddab):
    t = proj.shape[0]
    blk = ((SSD_TM, D_INNER), lambda i: (i, 0))
    first = lambda: pl.program_id(0) == 0
    return _ew_bwd(name, _ssdpre_fn, (t // SSD_TM,), _ssdpre_operands(proj, dt_bias, a_log, ex),
                   [(ddtb, *blk), (ddab, *blk)],
                   [dict(idx=0, kind="tile", shape=(t, DTW), dtype=BF16, block=(SSD_TM, DTW), imap=lambda i: (i, 0)),
                    dict(idx=1, kind="acc", shape=(1, DTW), dtype=F32, block=(1, DTW), imap=lambda i: (0, 0), first=first),
                    dict(idx=2, kind="acc", shape=(1, DTW), dtype=F32, block=(1, DTW), imap=lambda i: (0, 0), first=first)])


def _ssd_in_specs(rev):
    nc = SEQ // CHUNK

    def c_(c):
        return nc - 1 - c if rev else c

    wide = (None, CHUNK, 4 * LANES)
    nar = (None, CHUNK, D_STATE)
    xb = D_INNER // D_STATE
    return [
        pl.BlockSpec(wide, lambda b, g, c: (b, c_(c), g)),
        pl.BlockSpec(wide, lambda b, g, c: (b, c_(c), g)),
        pl.BlockSpec(wide, lambda b, g, c: (b, c_(c), g)),
        pl.BlockSpec(nar, lambda b, g, c: (b, c_(c), xb + g)),
        pl.BlockSpec(nar, lambda b, g, c: (b, c_(c), xb + N_SSM_GROUPS + g)),
        pl.BlockSpec((CHUNK, CHUNK), lambda b, g, c: (0, 0)),
    ], c_


def _ssd_fwd(name, xc3, dtb3, dab3, ltri):
    bl = xc3.shape[0]
    nc = SEQ // CHUNK
    specs, _ = _ssd_in_specs(False)

    def body(x, dtb, dab, bm, cm, lt, y_ref, st_ref, st):
        @pl.when(pl.program_id(2) == 0)
        def _():
            st[...] = jnp.zeros_like(st)

        s0 = st[...]
        st_ref[...] = s0
        new_st, y = _ssd_step(s0, x[...], dtb[...], dab[...], bm[...], cm[...], lt[...])
        y_ref[...] = y
        st[...] = new_st

    return pl.pallas_call(
        body, name=name, grid=(bl, N_SSM_GROUPS, nc), in_specs=specs,
        out_specs=[pl.BlockSpec((None, CHUNK, 4 * LANES), lambda b, g, c: (b, c, g)),
                   pl.BlockSpec((None, None, None, D_STATE, 4 * LANES), lambda b, g, c: (b, g, c, 0, 0))],
        out_shape=[jax.ShapeDtypeStruct((bl, SEQ, D_INNER), F32),
                   jax.ShapeDtypeStruct((bl, N_SSM_GROUPS, nc, D_STATE, 4 * LANES), F32)],
        scratch_shapes=[pltpu.VMEM((D_STATE, 4 * LANES), F32)],
        compiler_params=_cp(("arbitrary",) * 3),
    )(xc3, dtb3, dab3, xc3, xc3, ltri)


def _ssd_bwd(name, xc3, dtb3, dab3, ltri, states, dy3, dxs_part3):
    bl = xc3.shape[0]
    nc = SEQ // CHUNK
    specs, c_ = _ssd_in_specs(True)
    wide = pl.BlockSpec((None, CHUNK, 4 * LANES), lambda b, g, c: (b, c_(c), g))
    nar = pl.BlockSpec((None, CHUNK, D_STATE), lambda b, g, c: (b, c_(c), g))
    st_spec = pl.BlockSpec((None, None, None, D_STATE, 4 * LANES), lambda b, g, c: (b, g, c_(c), 0, 0))

    def body(x, dtb, dab, bm, cm, lt, st_ref, dy, dxp, dx_ref, ddtb_ref, ddab_ref, dbm_ref, dcm_ref, dst):
        @pl.when(pl.program_id(2) == 0)
        def _():
            dst[...] = jnp.zeros_like(dst)

        ltv = lt[...]
        _, vjp = jax.vjp(lambda *a: _ssd_step(*a, ltv), st_ref[...], x[...], dtb[...], dab[...], bm[...], cm[...])
        d_st, d_x, d_dtb, d_dab, d_bm, d_cm = vjp((dst[...], dy[...]))
        dst[...] = d_st
        dx_ref[...] = d_x + dxp[...]
        ddtb_ref[...] = d_dtb
        ddab_ref[...] = d_dab
        dbm_ref[...] = d_bm
        dcm_ref[...] = d_cm

    big = jax.ShapeDtypeStruct((bl, SEQ, D_INNER), F32)
    small = jax.ShapeDtypeStruct((bl, SEQ, N_SSM_GROUPS * D_STATE), F32)
    return pl.pallas_call(
        body, name=name, grid=(bl, N_SSM_GROUPS, nc), in_specs=specs + [st_spec, wide, wide],
        out_specs=[wide, wide, wide, nar, nar], out_shape=[big, big, big, small, small],
        scratch_shapes=[pltpu.VMEM((D_STATE, 4 * LANES), F32)],
        compiler_params=_cp(("arbitrary",) * 3),
    )(xc3, dtb3, dab3, xc3, xc3, ltri, states, dy3, dxs_part3)


def _ssdpost_operands(y, xc, proj, d_skip, ex, nw):
    w = 4 * LANES
    return [(y, (SSD_TM, w), lambda j, i: (i, j)), (xc, (SSD_TM, w), lambda j, i: (i, j)),
            (proj, (SSD_TM, w), lambda j, i: (i, Z0 // w + j)), (d_skip, (1, DTW), lambda j, i: (0, 0)),
            (ex, (LANES, w), lambda j, i: (0, j)), (nw, (1, w), lambda j, i: (0, j))]


def _ssdpost(name, y, xc, proj, d_skip, ex, nw):
    t = y.shape[0]
    w = 4 * LANES
    return _ew(name, _ssdpost_fn, (D_INNER // w, t // SSD_TM), _ssdpost_operands(y, xc, proj, d_skip, ex, nw),
               [((t, D_INNER), BF16, (SSD_TM, w), lambda j, i: (i, j))])[0]


def _ssdpost_bwd(name, y, xc, proj, d_skip, ex, nw, dysn, dproj):
    t = y.shape[0]
    w = 4 * LANES
    blk = ((SSD_TM, w), lambda j, i: (i, j))
    return _ew_bwd(name, _ssdpost_fn, (D_INNER // w, t // SSD_TM), _ssdpost_operands(y, xc, proj, d_skip, ex, nw),
                   [(dysn, *blk)],
                   [dict(idx=0, kind="tile", shape=(t, D_INNER), dtype=F32, block=blk[0], imap=blk[1]),
                    dict(idx=1, kind="tile", shape=(t, D_INNER), dtype=F32, block=blk[0], imap=blk[1]),
                    dict(idx=2, kind="tile", shape=dproj.shape, dtype=dproj.dtype, block=blk[0],
                         imap=lambda j, i: (i, Z0 // w + j), into=dproj),
                    dict(idx=3, kind="acc", shape=(1, DTW), dtype=F32, block=(1, DTW), imap=lambda j, i: (0, 0),
                         first=lambda: (pl.program_id(0) == 0) & (pl.program_id(1) == 0)),
                    dict(idx=5, kind="acc", shape=(1, D_INNER), dtype=F32, block=(1, w), imap=lambda j, i: (0, j),
                         first=lambda: pl.program_id(1) == 0)])


def _merge_operands(ya, ys, proj, b_gates):
    w = 4 * LANES
    g0 = G0 // w
    nh = D_MODEL // w
    return [(ya, (TM, w), lambda j, i: (i, j)), (ys, (TM, w), lambda j, i: (i, j)),
            (proj, (TM, w), lambda j, i: (i, g0 + j)), (proj, (TM, w), lambda j, i: (i, g0 + nh + j)),
            (b_gates, (1, w), lambda j, i: (0, j)), (b_gates, (1, w), lambda j, i: (0, nh + j))]


def _merge(name, ya, ys, proj, b_gates):
    t = ya.shape[0]
    w = 4 * LANES
    return _ew(name, _merge_fn, (D_MODEL // w, t // TM), _merge_operands(ya, ys, proj, b_gates),
               [((t, D_MODEL), BF16, (TM, w), lambda j, i: (i, j))])[0]


def _merge_bwd(name, ya, ys, proj, b_gates, dmixed):
    t = ya.shape[0]
    w = 4 * LANES
    blk = ((TM, w), lambda j, i: (i, j))
    first = lambda: pl.program_id(1) == 0
    tile = lambda k, dt: dict(idx=k, kind="tile", shape=(t, D_MODEL), dtype=dt, block=blk[0], imap=blk[1])
    acc = lambda k: dict(idx=k, kind="acc", shape=(1, D_MODEL), dtype=F32, block=(1, w), imap=lambda j, i: (0, j), first=first)
    return _ew_bwd(name, _merge_fn, (D_MODEL // w, t // TM), _merge_operands(ya, ys, proj, b_gates), [(dmixed, *blk)],
                   [tile(0, BF16), tile(1, BF16), tile(2, BF16), tile(3, BF16), acc(4), acc(5)])


def _loss(name, y, tgt):
    t = y.shape[0]
    blk = pl.BlockSpec((TM, D_MODEL), lambda i: (i, 0))

    def body(y_ref, t_ref, dy_ref, l_ref):
        e = y_ref[...] - t_ref[...]
        dy_ref[...] = e * (1.0 / D_MODEL)
        part = jnp.sum(jnp.sum(e * e, axis=-1, keepdims=True), axis=0, keepdims=True) * (0.5 / D_MODEL)
        part = jnp.broadcast_to(part, (8, LANES))

        @pl.when(pl.program_id(0) == 0)
        def _():
            l_ref[...] = part

        @pl.when(pl.program_id(0) > 0)
        def _():
            l_ref[...] += part

    return pl.pallas_call(
        body, name=name, grid=(t // TM,), in_specs=[blk, blk],
        out_specs=[blk, pl.BlockSpec((8, LANES), lambda i: (0, 0))],
        out_shape=[jax.ShapeDtypeStruct((t, D_MODEL), F32), jax.ShapeDtypeStruct((8, LANES), F32)],
        compiler_params=_cp(("arbitrary",)),
    )(y, tgt)


def _adamw_fn(w, g, m, v):
    m2 = B1 * m + (1.0 - B1) * g
    v2 = B2 * v + (1.0 - B2) * (g * g)
    m_hat = m2 / (1.0 - B1 ** STEP)
    v_hat = v2 / (1.0 - B2 ** STEP)
    return -LR * (m_hat / (jnp.sqrt(v_hat) + ADAM_EPS) + WD * w), m2, v2


def _adamw(name, w, g, m, v):
    rows, cols = w.shape
    tm = rows
    for cand in (512, 256, 128, 64, 32, 16, 8):
        if rows % cand == 0 and cand * cols * 4 <= (1 << 21):
            tm = cand
            break
    blk = ((tm, cols), lambda i: (i, 0))
    return _ew(name, _adamw_fn, (rows // tm,), [(a, *blk) for a in (w, g, m, v)], [((rows, cols), F32, *blk)] * 3)


NCH = 4
TMM = 1024
TKK = 1024


def _ffn_fwd(tag, x, nw, wg, wu, wd, li):
    t, fc = x.shape[0], wg.shape[-1]
    h = _rmsnorm(tag + "_norm", x, nw)

    def up(name, w):
        return _mmx(name, (NCH, t // TMM, D_MODEL // TKK),
                    (h, (TMM, TKK), lambda k, i, kk: (i, kk)),
                    (w, (None, None, TKK, fc), lambda k, i, kk: (k, li, kk, 0)),
                    ((NCH, t, fc), BF16, (None, TMM, fc), lambda k, i, kk: (k, i, 0)), (1, 0))

    g, u = up(tag + "_gate", wg), up(tag + "_up", wu)
    a = _swiglu(tag + "_act", g.reshape(NCH * t, fc), u.reshape(NCH * t, fc)).reshape(NCH, t, fc)
    row = ((TMM, D_MODEL), lambda i, j, k: (i, 0))
    y = _mmx(tag + "_down", (t // TMM, 1, NCH),
             (a, (None, TMM, fc), lambda i, j, k: (k, i, 0)),
             (wd, (None, None, fc, D_MODEL), lambda i, j, k: (k, li, 0, 0)),
             ((t, D_MODEL), F32, *row), (1, 0), alpha=0.5, res=(x, *row))
    return y, (x, h, g, u, a)


def _ffn_bwd(tag, saved, nw, wg, wu, wd, li, dy, bufs):
    x, h, g, u, a = saved
    t, fc = x.shape[0], wg.shape[-1]
    bg, bu, bd = bufs
    da = _mmx(tag + "_down_dx", (NCH, t // TMM, D_MODEL // TKK),
              (dy, (TMM, TKK), lambda k, i, kk: (i, kk)),
              (wd, (None, None, fc, TKK), lambda k, i, kk: (k, li, 0, kk)),
              ((NCH, t, fc), F32, (None, TMM, fc), lambda k, i, kk: (k, i, 0)), (1, 1), alpha=0.5)
    bd = _mmx(tag + "_down_dw", (NCH, 1, t // TKK),
              (a, (None, TKK, fc), lambda k, j, kk: (k, kk, 0)),
              (dy, (TKK, D_MODEL), lambda k, j, kk: (kk, 0)),
              (bd.shape, BF16, (None, None, fc, D_MODEL), lambda k, j, kk: (li, k, 0, 0)), (0, 0), alpha=0.5, into=bd)
    dg, du = _swiglu_bwd(tag + "_act_bwd", g.reshape(NCH * t, fc), u.reshape(NCH * t, fc), da.reshape(NCH * t, fc))
    dg, du = dg.reshape(NCH, t, fc), du.reshape(NCH, t, fc)

    def dw(name, d, buf):
        return _mmx(name, (NCH, 1, t // TKK),
                    (h, (TKK, D_MODEL), lambda k, i, kk: (kk, 0)),
                    (d, (None, TKK, fc), lambda k, i, kk: (k, kk, 0)),
                    (buf.shape, BF16, (None, None, D_MODEL, fc), lambda k, i, kk: (li, k, 0, 0)), (0, 0), into=buf)

    bg, bu = dw(tag + "_gate_dw", dg, bg), dw(tag + "_up_dw", du, bu)
    row = ((TMM, D_MODEL), lambda i, j, k: (i, 0))

    def dx_(name, d, w, res):
        return _mmx(name, (t // TMM, 1, NCH),
                    (d, (None, TMM, fc), lambda i, j, k: (k, i, 0)),
                    (w, (None, None, D_MODEL, fc), lambda i, j, k: (k, li, 0, 0)),
                    ((t, D_MODEL), F32, *row), (1, 1), res=None if res is None else (res, *row))

    dh = dx_(tag + "_up_dx", du, wu, dx_(tag + "_gate_dx", dg, wg, None))
    dx, dnw = _rmsnorm_bwd(tag + "_norm_bwd", x, nw, dh, dy)
    return dx, dnw, (bg, bu, bd)


def _mixer_fwd(tag, x, p, c):
    t = x.shape[0]
    bl = t // SEQ
    h = _rmsnorm(tag + "_norm", x, p["mix_norm_w"])
    proj = _mm(tag + "_in", h, p["w_in"], "nn")
    qk = _qkprep(tag + "_qk", proj, p["qkw"], c["cos"], c["sin"], c)
    os_, ls_ = [], []
    for g in range(3):
        o, l = _att_fwd(f"{tag}_att{g}", qk, proj, g)
        os_.append(o)
        ls_.append(l)
    att = _attmix(tag + "_attmix", os_, ls_)
    li = p["layer"]
    wa, ws, wo = p["w_att_proj"], p["w_ssm_proj"], p["w_out"]
    ca, cs, co = wa.shape[-1], ws.shape[-2], wo.shape[-2]
    row = ((TMM, D_MODEL), lambda i, j, k: (i, 0))
    ya = _mmx(tag + "_attproj", (t // TMM, NCH, 1),
              (att, (TMM, ATT_OUT), lambda i, k, kk: (i, 0)),
              (wa, (None, None, ATT_OUT, ca), lambda i, k, kk: (k, li, 0, 0)),
              ((t, D_MODEL), F32, (TMM, ca), lambda i, k, kk: (i, k)), (1, 0))
    proj3 = proj.reshape(bl, SEQ, NP)
    xc3 = _conv(tag + "_conv", proj3, p["conv_w"], p["conv_b"])
    xc = xc3.reshape(t, XBC)
    dtb, dab = _ssdpre(tag + "_ssdpre", proj, p["dt_bias"], p["a_log"], c["ex"])
    dtb3, dab3 = dtb.reshape(bl, SEQ, D_INNER), dab.reshape(bl, SEQ, D_INNER)
    y3, states = _ssd_fwd(tag + "_ssd", xc3, dtb3, dab3, c["ltri"])
    y = y3.reshape(t, D_INNER)
    ysn = _ssdpost(tag + "_ssdpost", y, xc, proj, p["d_skip"], c["ex"], p["ssm_norm_w"])
    ys = _mmx(tag + "_ssmproj", (t // TMM, 1, NCH),
              (ysn, (TMM, cs), lambda i, j, k: (i, k)),
              (ws, (None, None, cs, D_MODEL), lambda i, j, k: (k, li, 0, 0)),
              ((t, D_MODEL), F32, *row), (1, 0))
    mixed = _merge(tag + "_merge", ya, ys, proj, p["b_gates"])
    out = _mmx(tag + "_out", (t // TMM, 1, NCH),
               (mixed, (TMM, co), lambda i, j, k: (i, k)),
               (wo, (None, None, co, D_MODEL), lambda i, j, k: (k, li, 0, 0)),
               ((t, D_MODEL), F32, *row), (1, 0), res=(x, *row))
    return out, (x, h, proj, qk, os_, ls_, att, ya, xc3, dtb3, dab3, states, y, ysn, ys, mixed)


def _mixer_bwd(tag, saved, p, c, dout, bufs):
    x, h, proj, qk, os_, ls_, att, ya, xc3, dtb3, dab3, states, y, ysn, ys, mixed = saved
    t = x.shape[0]
    bl = t // SEQ
    xc = xc3.reshape(t, XBC)
    proj3 = proj.reshape(bl, SEQ, NP)
    gr = {}
    li = p["layer"]
    wa, ws, wo = p["w_att_proj"], p["w_ssm_proj"], p["w_out"]
    ca, cs, co = wa.shape[-1], ws.shape[-2], wo.shape[-2]
    b_att, b_ssm, b_out = bufs

    def chunk_dx(name, d, w, cw):
        return _mmx(name, (t // TMM, NCH, D_MODEL // TKK),
                    (d, (TMM, TKK), lambda i, k, kk: (i, kk)),
                    (w, (None, None, cw, TKK), lambda i, k, kk: (k, li, 0, kk)),
                    ((t, NCH * cw), F32, (TMM, cw), lambda i, k, kk: (i, k)), (1, 1))

    def full_dw(name, a_, d, buf):
        kdim = a_.shape[1]
        tm = min(kdim, 1024)
        return _mmx(name, (kdim // tm, 1, t // TKK),
                    (a_, (TKK, tm), lambda i, j, kk: (kk, i)),
                    (d, (TKK, D_MODEL), lambda i, j, kk: (kk, 0)),
                    (buf.shape, BF16, (None, tm, D_MODEL), lambda i, j, kk: (li, i, 0)), (0, 0), into=buf)

    dmixed = chunk_dx(tag + "_out_dx", dout, wo, co)
    b_out = full_dw(tag + "_out_dw", mixed, dout, b_out)
    dya, dys, dga, dgs, dba, dbs = _merge_bwd(tag + "_merge_bwd", ya, ys, proj, p["b_gates"], dmixed)
    gr["b_gates"] = jnp.concatenate([dba, dbs], axis=1)
    datt = _mmx(tag + "_attproj_dx", (t // TMM, 1, NCH),
                (dya, (TMM, ca), lambda i, j, k: (i, k)),
                (wa, (None, None, ATT_OUT, ca), lambda i, j, k: (k, li, 0, 0)),
                ((t, ATT_OUT), F32, (TMM, ATT_OUT), lambda i, j, k: (i, 0)), (1, 1))
    b_att = _mmx(tag + "_attproj_dw", (NCH, 1, t // TKK),
                 (att, (TKK, ATT_OUT), lambda k, j, kk: (kk, 0)),
                 (dya, (TKK, ca), lambda k, j, kk: (kk, k)),
                 (b_att.shape, BF16, (None, None, ATT_OUT, ca), lambda k, j, kk: (li, k, 0, 0)), (0, 0), into=b_att)
    dysn = chunk_dx(tag + "_ssmproj_dx", dys, ws, cs)
    b_ssm = full_dw(tag + "_ssmproj_dw", ysn, dys, b_ssm)
    gr["bufs"] = (b_att, b_ssm, b_out)
    dmix = _attmix_bwd(tag + "_attmix_bwd", os_, ls_, datt)
    dq = dk = dv = jnp.zeros((t, QKV), F32)
    for g in range(3):
        dq, dk, dv = _att_bwd(f"{tag}_att{g}_bwd", qk, proj, g, dmix[g], dmix[3 + g], dq, dk, dv)
    dproj = jnp.zeros((t, NP), BF16)
    dproj = lax.dynamic_update_slice(dproj, dv.astype(BF16), (0, V0))
    dproj = lax.dynamic_update_slice(dproj, dga, (0, G0))
    dproj = lax.dynamic_update_slice(dproj, dgs, (0, G0 + D_MODEL))
    dproj, gr["qkw"] = _qkprep_bwd(tag + "_qk_bwd", proj, p["qkw"], c["cos"], c["sin"], c, dq, dk, dproj)
    dy, dxs_part, dproj, gr["d_skip"], gr["ssm_norm_w"] = _ssdpost_bwd(
        tag + "_ssdpost_bwd", y, xc, proj, p["d_skip"], c["ex"], p["ssm_norm_w"], dysn, dproj)
    dxs3, ddtb3, ddab3, db3, dc3 = _ssd_bwd(
        tag + "_ssd_bwd", xc3, dtb3, dab3, c["ltri"], states, dy.reshape(bl, SEQ, D_INNER), dxs_part.reshape(bl, SEQ, D_INNER))
    ddt, gr["dt_bias"], gr["a_log"] = _ssdpre_bwd(
        tag + "_ssdpre_bwd", proj, p["dt_bias"], p["a_log"], c["ex"], ddtb3.reshape(t, D_INNER), ddab3.reshape(t, D_INNER))
    dproj = lax.dynamic_update_slice(dproj, ddt, (0, DT0))
    dproj3, dcw0, dcw1, dcw2, dcw3, gr["conv_b"] = _conv_bwd(
        tag + "_conv_bwd", proj3, p["conv_w"], p["conv_b"], dxs3, db3, dc3, dproj.reshape(bl, SEQ, NP))
    dproj = dproj3.reshape(t, NP)
    gr["conv_w"] = jnp.concatenate([dcw0, dcw1, dcw2, dcw3], axis=0)
    gr["w_in"] = _mm(tag + "_in_dw", h, dproj, "tn", out_dtype=BF16)
    dh = _mm(tag + "_in_dx", dproj, p["w_in"], "nt")
    dx, gr["mix_norm_w"] = _rmsnorm_bwd(tag + "_norm_bwd", x, p["mix_norm_w"], dh, dout)
    return dx, gr


def _constants():
    cos, sin = _rope_tables()
    return dict(cos=cos, sin=sin, hmean=_head_mean_mat(), tile64=_tile64_mat(),
                ex=_head_expand_mat(), ltri=_ltri_mat())


ANY = pl.BlockSpec(memory_space=pl.ANY)


def _mesh_pos():
    return lax.axis_index("x"), lax.axis_index("y"), lax.axis_index("c")


def _other_chips(x, y):
    return [(1 - x, y), (x, 1 - y), (1 - x, 1 - y)]


def _all_gather_weights(ws, chip_idx):
    n = len(ws)
    hl = ws[0].shape[0] // 2
    inits = [lax.dynamic_update_slice(jnp.zeros((NCH, *w.shape), w.dtype), w[None], (chip_idx[0], 0, 0, 0)) for w in ws]

    def body(*refs):
        srcs, outs = refs[:n], refs[2 * n:3 * n]
        send_sems, recv_sems = refs[3 * n], refs[3 * n + 1]
        x, y, c = _mesh_pos()
        chips = _other_chips(x, y)

        def part(a, chip, hf):
            return outs[a].at[2 * chip[0] + chip[1], pl.ds(hf * hl, hl)]

        def copy(a, k, src_ref, dst_ref, to):
            return pltpu.make_async_remote_copy(src_ref=src_ref, dst_ref=dst_ref, send_sem=send_sems.at[6 * a + k],
                                                recv_sem=recv_sems.at[6 * a + k], device_id=to, device_id_type=MESH)

        first = [copy(a, j, srcs[a].at[pl.ds(c * hl, hl)], part(a, (x, y), c), (*chip, c))
                 for a in range(n) for j, chip in enumerate(chips)]
        for cp in first:
            cp.start()
        passed = []
        for a in range(n):
            for j, chip in enumerate(chips):
                copy(a, j, part(a, chip, c), part(a, chip, c), (x, y, c)).wait_recv()
                fw = copy(a, 3 + j, part(a, chip, c), part(a, chip, c), (x, y, 1 - c))
                fw.start()
                passed.append(fw)
        for a in range(n):
            for j, chip in enumerate(chips):
                copy(a, 3 + j, part(a, chip, 1 - c), part(a, chip, 1 - c), (x, y, c)).wait_recv()
        for cp in first + passed:
            cp.wait_send()

    return pl.pallas_call(
        body, name="all_gather_weights", out_shape=[jax.ShapeDtypeStruct(i.shape, i.dtype) for i in inits],
        in_specs=[ANY] * (2 * n), out_specs=[ANY] * n, input_output_aliases={n + a: a for a in range(n)},
        scratch_shapes=[pltpu.SemaphoreType.DMA((6 * n,)), pltpu.SemaphoreType.DMA((6 * n,))],
    )(*ws, *inits)


def _pair_exchange(gs):
    n = len(gs)

    def body(*refs):
        srcs, outs, send_sems, recv_sems = refs[:n], refs[n:2 * n], refs[2 * n], refs[2 * n + 1]
        x, y, c = _mesh_pos()
        cps = []
        for a in range(n):
            h = gs[a].shape[2] // 2
            cps.append(pltpu.make_async_remote_copy(
                src_ref=srcs[a].at[:, :, pl.ds((1 - c) * h, h), :], dst_ref=outs[a], send_sem=send_sems.at[a],
                recv_sem=recv_sems.at[a], device_id=(x, y, 1 - c), device_id_type=MESH))
        for cp in cps:
            cp.start()
        for cp in cps:
            cp.wait()

    return pl.pallas_call(
        body, name="grad_pair_exchange",
        out_shape=[jax.ShapeDtypeStruct((g.shape[0], g.shape[1], g.shape[2] // 2, g.shape[3]), g.dtype) for g in gs],
        in_specs=[ANY] * n, out_specs=[ANY] * n,
        scratch_shapes=[pltpu.SemaphoreType.DMA((n,)), pltpu.SemaphoreType.DMA((n,))],
    )(*gs)


def _chip_exchange(hs):
    n = len(hs)

    def body(*refs):
        srcs, outs, send_sems, recv_sems = refs[:n], refs[n:2 * n], refs[2 * n], refs[2 * n + 1]
        x, y, c = _mesh_pos()
        cps = [pltpu.make_async_remote_copy(
            src_ref=srcs[a].at[:, 2 * chip[0] + chip[1]], dst_ref=outs[a].at[j], send_sem=send_sems.at[3 * a + j],
            recv_sem=recv_sems.at[3 * a + j], device_id=(*chip, c), device_id_type=MESH)
            for a in range(n) for j, chip in enumerate(_other_chips(x, y))]
        for cp in cps:
            cp.start()
        for cp in cps:
            cp.wait()

    return pl.pallas_call(
        body, name="grad_chip_exchange",
        out_shape=[jax.ShapeDtypeStruct((3, h.shape[0], h.shape[2], h.shape[3]), h.dtype) for h in hs],
        in_specs=[ANY] * n, out_specs=[ANY] * n,
        scratch_shapes=[pltpu.SemaphoreType.DMA((3 * n,)), pltpu.SemaphoreType.DMA((3 * n,))],
    )(*hs)


def _pair_share(rs):
    n = len(rs)

    def body(*refs):
        outs, send_sems, recv_sems = refs[n:2 * n], refs[2 * n], refs[2 * n + 1]
        x, y, c = _mesh_pos()
        cps = [pltpu.make_async_remote_copy(src_ref=outs[a].at[:, c], dst_ref=outs[a].at[:, c], send_sem=send_sems.at[a],
                                            recv_sem=recv_sems.at[a], device_id=(x, y, 1 - c), device_id_type=MESH)
               for a in range(n)]
        for cp in cps:
            cp.start()
        for a in range(n):
            pltpu.make_async_remote_copy(src_ref=outs[a].at[:, 1 - c], dst_ref=outs[a].at[:, 1 - c],
                                         send_sem=send_sems.at[a], recv_sem=recv_sems.at[a], device_id=(x, y, c),
                                         device_id_type=MESH).wait_recv()
        for cp in cps:
            cp.wait_send()

    return pl.pallas_call(
        body, name="grad_pair_share", out_shape=[jax.ShapeDtypeStruct(r.shape, r.dtype) for r in rs],
        in_specs=[ANY] * n, out_specs=[ANY] * n, input_output_aliases={a: a for a in range(n)},
        scratch_shapes=[pltpu.SemaphoreType.DMA((n,)), pltpu.SemaphoreType.DMA((n,))],
    )(*rs)


def _pair_sum(name, g, recv, c_idx):
    d, k, h, b = recv.shape
    g5 = g.reshape(d * k, 2, h, b)

    def body(c_ref, a_ref, b_ref, o_ref):
        o_ref[...] = (a_ref[...].astype(F32) + b_ref[...].astype(F32)).astype(o_ref.dtype)

    out = pl.pallas_call(
        body, name=name,
        grid_spec=pltpu.PrefetchScalarGridSpec(
            num_scalar_prefetch=1, grid=(d * k,),
            in_specs=[pl.BlockSpec((None, None, h, b), lambda i, c: (i, c[0], 0, 0)),
                      pl.BlockSpec((None, h, b), lambda i, c: (i, 0, 0))],
            out_specs=pl.BlockSpec((None, h, b), lambda i, c: (i, 0, 0))),
        out_shape=jax.ShapeDtypeStruct((d * k, h, b), BF16),
        compiler_params=_cp(("arbitrary",)),
    )(c_idx, g5, recv.reshape(d * k, h, b))
    return out.reshape(d, k, h, b)


def _chip_sum(name, ha, recv, chip_idx, c_idx):
    d, _, h, b = ha.shape

    def body(k_ref, c_ref, a_ref, r0, r1, r2, o_ref):
        o_ref[...] = ((a_ref[...].astype(F32) + r0[...].astype(F32)) + r1[...].astype(F32)) + r2[...].astype(F32)

    blk = (None, None, h, b)
    return pl.pallas_call(
        body, name=name,
        grid_spec=pltpu.PrefetchScalarGridSpec(
            num_scalar_prefetch=2, grid=(d,),
            in_specs=[pl.BlockSpec(blk, lambda l, k, c: (l, k[0], 0, 0))] +
                     [pl.BlockSpec(blk, lambda l, k, c, j=j: (j, l, 0, 0)) for j in range(3)],
            out_specs=pl.BlockSpec(blk, lambda l, k, c: (l, c[0], 0, 0))),
        out_shape=jax.ShapeDtypeStruct((d, 2, h, b), F32),
        compiler_params=_cp(("arbitrary",)),
    )(chip_idx, c_idx, ha, recv, recv, recv)


def _all_sum_small(name, vec):
    rows = vec.shape[0]

    def body(v_ref, o_ref, buf, send_sems, recv_sems):
        x, y, c = _mesh_pos()
        me, sibling = (x, y, c), (x, y, 1 - c)
        chips = _other_chips(x, y)

        def slot(p):
            return buf.at[4 * p[0] + 2 * p[1] + p[2]]

        def copy(k, block, to, src=None):
            return pltpu.make_async_remote_copy(src_ref=slot(block) if src is None else src, dst_ref=slot(block),
                                                send_sem=send_sems.at[k], recv_sem=recv_sems.at[k],
                                                device_id=to, device_id_type=MESH)

        first = [copy(0, me, sibling, src=v_ref)]
        first += [copy(1 + j, me, (*chip, c), src=v_ref) for j, chip in enumerate(chips)]
        for cp in first:
            cp.start()
        passed = [copy(4 + j, (*chip, c), sibling) for j, chip in enumerate(chips)]
        for j, chip in enumerate(chips):
            copy(1 + j, (*chip, c), me).wait_recv()
            passed[j].start()
        copy(0, sibling, me).wait_recv()
        for j, chip in enumerate(chips):
            copy(4 + j, (*chip, 1 - c), me).wait_recv()
        for cp in first + passed:
            cp.wait_send()
        slot(me)[...] = v_ref[...]
        acc = buf[0]
        for k in range(1, 8):
            acc = acc + buf[k]
        o_ref[...] = acc

    vm = pl.BlockSpec(memory_space=pltpu.VMEM)
    return pl.pallas_call(
        body, name=name, out_shape=jax.ShapeDtypeStruct((rows, LANES), F32),
        in_specs=[vm], out_specs=vm, compiler_params=pltpu.CompilerParams(vmem_limit_bytes=VMEM_LIMIT),
        scratch_shapes=[pltpu.VMEM((8, rows, LANES), F32), pltpu.SemaphoreType.DMA((7,)), pltpu.SemaphoreType.DMA((7,))],
    )(vec)


def _pad_lanes(v, n=LANES):
    return jnp.pad(v, (0, n - v.shape[0]))[None, :]


def _w_in_to_kernel(w):
    return jnp.concatenate([w[:, :6656], w[:, 9760:N_IN], w[:, 6656:9728], w[:, 9728:9760],
                            jnp.zeros((w.shape[0], NP - N_IN), w.dtype)], axis=1)


def _w_in_from_kernel(w):
    return jnp.concatenate([w[:, :6656], w[:, X0:DT0], w[:, DT0:DT0 + 32], w[:, G0:X0]], axis=1)


def _layer_params(big, small, i):
    p = {k: big[k] for k in ("w_att_proj", "w_ssm_proj", "w_out")}
    p["layer"] = i
    w_in = big["w_in"]
    p["w_in"] = _w_in_to_kernel(jnp.concatenate([w_in[k, i] for k in range(NCH)], axis=1))
    p["conv_w"] = big["conv_w"][i][:, None, :]
    for k in ("ffn1_norm_w", "mix_norm_w", "ffn2_norm_w", "b_gates", "conv_b", "ssm_norm_w"):
        p[k] = small[k][i][None, :]
    for k in ("dt_bias", "a_log", "d_skip"):
        p[k] = _pad_lanes(small[k][i])
    p["qkw"] = jnp.stack([_pad_lanes(small["q_norm_w"][i]), _pad_lanes(small["k_norm_w"][i])])
    return p


GRAD_BUFS = ("ffn1_w_gate", "ffn1_w_up", "ffn1_w_down", "w_att_proj", "w_ssm_proj", "w_out",
             "ffn2_w_gate", "ffn2_w_up", "ffn2_w_down")


def _local_step(x, tgt, layers, big, c):
    depth = len(layers)
    saved = []
    for i, p in enumerate(layers):
        x, s1 = _ffn_fwd(f"L{i}_ffn1", x, p["ffn1_norm_w"], big["ffn1_w_gate"], big["ffn1_w_up"], big["ffn1_w_down"], i)
        x, s2 = _mixer_fwd(f"L{i}_mix", x, p, c)
        x, s3 = _ffn_fwd(f"L{i}_ffn2", x, p["ffn2_norm_w"], big["ffn2_w_gate"], big["ffn2_w_up"], big["ffn2_w_down"], i)
        saved.append((s1, s2, s3))
    dx, loss_blk = _loss("loss", x, tgt)
    buf = {}
    for n in GRAD_BUFS:
        _, _, a, b = big[n].shape
        shape = (depth, NCH * a, b) if n in ("w_ssm_proj", "w_out") else (depth, NCH, a, b)
        buf[n] = jnp.zeros(shape, BF16)
    grads = [None] * depth
    for i in reversed(range(depth)):
        p = layers[i]
        s1, s2, s3 = saved[i]
        dx, dn2, (buf["ffn2_w_gate"], buf["ffn2_w_up"], buf["ffn2_w_down"]) = _ffn_bwd(
            f"L{i}_ffn2", s3, p["ffn2_norm_w"], big["ffn2_w_gate"], big["ffn2_w_up"], big["ffn2_w_down"], i, dx,
            (buf["ffn2_w_gate"], buf["ffn2_w_up"], buf["ffn2_w_down"]))
        dx, gr = _mixer_bwd(f"L{i}_mix", s2, p, c, dx, (buf["w_att_proj"], buf["w_ssm_proj"], buf["w_out"]))
        buf["w_att_proj"], buf["w_ssm_proj"], buf["w_out"] = gr.pop("bufs")
        dx, dn1, (buf["ffn1_w_gate"], buf["ffn1_w_up"], buf["ffn1_w_down"]) = _ffn_bwd(
            f"L{i}_ffn1", s1, p["ffn1_norm_w"], big["ffn1_w_gate"], big["ffn1_w_up"], big["ffn1_w_down"], i, dx,
            (buf["ffn1_w_gate"], buf["ffn1_w_up"], buf["ffn1_w_down"]))
        gr.update(ffn1_norm_w=dn1, ffn2_norm_w=dn2)
        grads[i] = gr
    for n in ("w_ssm_proj", "w_out"):
        _, _, a, b = big[n].shape
        buf[n] = buf[n].reshape(depth, NCH, a, b)
    return loss_blk, dx, grads, buf


WEIGHTS = ["ffn1_norm_w", "ffn1_w_gate", "ffn1_w_up", "ffn1_w_down", "mix_norm_w", "w_in", "b_gates", "q_norm_w",
           "k_norm_w", "conv_w", "conv_b", "dt_bias", "a_log", "d_skip", "ssm_norm_w", "w_att_proj", "w_ssm_proj",
           "w_out", "ffn2_norm_w", "ffn2_w_gate", "ffn2_w_up", "ffn2_w_down"]
SHARD_AXIS = {"ffn1_w_gate": 2, "ffn1_w_up": 2, "ffn1_w_down": 1, "w_in": 2, "conv_w": 2, "w_att_proj": 2,
              "w_ssm_proj": 1, "w_out": 1, "ffn2_w_gate": 2, "ffn2_w_up": 2, "ffn2_w_down": 1}
BIG = [n for n in WEIGHTS if n in SHARD_AXIS]
SMALL = [n for n in WEIGHTS if n not in SHARD_AXIS]
def _from_flat(flat, shapes):
    v = flat.reshape(-1)
    out, off = [], 0
    for s in shapes:
        n = math.prod(s)
        out.append(v[off:off + n].reshape(s))
        off += n
    return out


def _pack_small(parts):
    v = jnp.concatenate([p.astype(F32).reshape(-1) for p in parts])
    rows = -(-v.shape[0] // (8 * LANES)) * 8
    return jnp.pad(v, (0, rows * LANES - v.shape[0])).reshape(rows, LANES)


def kernel(x, ffn1_norm_w, ffn1_w_gate, ffn1_w_up, ffn1_w_down, mix_norm_w, w_in, b_gates, q_norm_w, k_norm_w, conv_w, conv_b, dt_bias, a_log, d_skip, ssm_norm_w, w_att_proj, w_ssm_proj, w_out, ffn2_norm_w, ffn2_w_gate, ffn2_w_up, ffn2_w_down, loss_target, m_ffn1_norm_w, m_ffn1_w_gate, m_ffn1_w_up, m_ffn1_w_down, m_mix_norm_w, m_w_in, m_b_gates, m_q_norm_w, m_k_norm_w, m_conv_w, m_conv_b, m_dt_bias, m_a_log, m_d_skip, m_ssm_norm_w, m_w_att_proj, m_w_ssm_proj, m_w_out, m_ffn2_norm_w, m_ffn2_w_gate, m_ffn2_w_up, m_ffn2_w_down, v_ffn1_norm_w, v_ffn1_w_gate, v_ffn1_w_up, v_ffn1_w_down, v_mix_norm_w, v_w_in, v_b_gates, v_q_norm_w, v_k_norm_w, v_conv_w, v_conv_b, v_dt_bias, v_a_log, v_d_skip, v_ssm_norm_w, v_w_att_proj, v_w_ssm_proj, v_w_out, v_ffn2_norm_w, v_ffn2_w_gate, v_ffn2_w_up, v_ffn2_w_down):
    w = dict(zip(WEIGHTS, (ffn1_norm_w, ffn1_w_gate, ffn1_w_up, ffn1_w_down, mix_norm_w, w_in, b_gates, q_norm_w, k_norm_w, conv_w, conv_b, dt_bias, a_log, d_skip, ssm_norm_w, w_att_proj, w_ssm_proj, w_out, ffn2_norm_w, ffn2_w_gate, ffn2_w_up, ffn2_w_down)))
    m = dict(zip(WEIGHTS, (m_ffn1_norm_w, m_ffn1_w_gate, m_ffn1_w_up, m_ffn1_w_down, m_mix_norm_w, m_w_in, m_b_gates, m_q_norm_w, m_k_norm_w, m_conv_w, m_conv_b, m_dt_bias, m_a_log, m_d_skip, m_ssm_norm_w, m_w_att_proj, m_w_ssm_proj, m_w_out, m_ffn2_norm_w, m_ffn2_w_gate, m_ffn2_w_up, m_ffn2_w_down)))
    v = dict(zip(WEIGHTS, (v_ffn1_norm_w, v_ffn1_w_gate, v_ffn1_w_up, v_ffn1_w_down, v_mix_norm_w, v_w_in, v_b_gates, v_q_norm_w, v_k_norm_w, v_conv_w, v_conv_b, v_dt_bias, v_a_log, v_d_skip, v_ssm_norm_w, v_w_att_proj, v_w_ssm_proj, v_w_out, v_ffn2_norm_w, v_ffn2_w_gate, v_ffn2_w_up, v_ffn2_w_down)))
    depth = ffn1_norm_w.shape[0]
    bl = x.shape[0]
    t = bl * SEQ
    mx, my, mc = lax.axis_index("x"), lax.axis_index("y"), lax.axis_index("c")
    c_idx = mc.astype(jnp.int32).reshape(1)
    chip_idx = (2 * mx + my).astype(jnp.int32).reshape(1)

    cw_width = conv_w.shape[2]
    slots = lax.dynamic_update_slice(jnp.zeros((NCH, *conv_w.shape), F32), jnp.where(mc == 0, conv_w, 0.0)[None],
                                     (chip_idx[0], 0, 0, 0))
    conv_all = _all_sum_small("conv_gather", slots.reshape(-1, LANES)).reshape(NCH, *conv_w.shape)
    big = {"conv_w": jnp.concatenate([conv_all[k] for k in range(NCH)], axis=2)}

    mm_names = [n for n in BIG if n != "conv_w"]
    big.update(zip(mm_names, _all_gather_weights([w[n].astype(BF16) for n in mm_names], chip_idx)))
    small = {n: w[n] for n in SMALL}

    c = _constants()
    layers = [_layer_params(big, small, i) for i in range(depth)]
    loss_blk, dx, grads, buf = _local_step(x.reshape(t, D_MODEL), loss_target.reshape(t, D_MODEL), layers, big, c)
    grad_x = dx.reshape(bl, SEQ, D_MODEL)

    buf["w_in"] = jnp.stack([_w_in_from_kernel(g["w_in"]).reshape(D_MODEL, NCH, -1).transpose(1, 0, 2) for g in grads])
    mine = [buf[n] for n in mm_names]
    from_sibling = _pair_exchange(mine)
    pairs = [_pair_sum("grad_pair_sum_" + n, g, r, c_idx) for n, g, r in zip(mm_names, mine, from_sibling)]
    from_chips = _chip_exchange(pairs)
    halves = [_chip_sum("grad_chip_sum_" + n, h, r, chip_idx, c_idx) for n, h, r in zip(mm_names, pairs, from_chips)]
    g_big = {n: r.reshape(w[n].shape) for n, r in zip(mm_names, _pair_share(halves))}

    def small_grad(n):
        if n == "q_norm_w":
            return jnp.stack([g["qkw"][0, 0, :64] for g in grads])
        if n == "k_norm_w":
            return jnp.stack([g["qkw"][1, 0, :64] for g in grads])
        return jnp.stack([g[n][0, :w[n].shape[1]] for g in grads])

    small_shapes = [w[n].shape for n in SMALL]
    conv_shape = (depth, conv_w.shape[1], NCH * cw_width)
    tot = _all_sum_small("small_all_sum", _pack_small(
        [small_grad(n) for n in SMALL] + [jnp.stack([g["conv_w"] for g in grads]), loss_blk[0, :1]]))
    unpacked = _from_flat(tot, small_shapes + [conv_shape, (1,)])
    g_small = dict(zip(SMALL, unpacked[:-2]))
    g_big["conv_w"] = lax.dynamic_slice_in_dim(unpacked[-2], chip_idx[0] * cw_width, cw_width, axis=2)
    loss = unpacked[-1][0]

    grad, delta, new_m, new_v = {}, {}, {}, {}
    for n in BIG:
        shp = w[n].shape
        two_d = (shp[0] * shp[1], shp[2])
        d_, m_, v_ = _adamw("adamw_" + n, w[n].reshape(two_d), g_big[n].reshape(two_d), m[n].reshape(two_d), v[n].reshape(two_d))
        grad[n], delta[n], new_m[n], new_v[n] = g_big[n], d_.reshape(shp), m_.reshape(shp), v_.reshape(shp)
    d_, m_, v_ = _adamw("adamw_small", _pack_small([w[n] for n in SMALL]), _pack_small([g_small[n] for n in SMALL]),
                        _pack_small([m[n] for n in SMALL]), _pack_small([v[n] for n in SMALL]))
    for n, a, b, c_ in zip(SMALL, _from_flat(d_, small_shapes), _from_flat(m_, small_shapes), _from_flat(v_, small_shapes)):
        grad[n], delta[n], new_m[n], new_v[n] = g_small[n], a, b, c_
    return (loss, grad_x, *[grad[n] for n in WEIGHTS], *[delta[n] for n in WEIGHTS],
            *[new_m[n] for n in WEIGHTS], *[new_v[n] for n in WEIGHTS])
```

```python
import functools
import math

import numpy as np
import jax
import jax.numpy as jnp
from jax import lax
from jax.experimental import pallas as pl
from jax.experimental.pallas import tpu as pltpu

F32 = jnp.float32
BF16 = jnp.bfloat16
HI = lax.Precision.HIGHEST
MESH = pl.DeviceIdType.MESH

D_MODEL = 1024
SEQ = 2048
DEPTH = 4
D_FF = 2816
ATT_DILATIONS = (1, 4, 16)
BAND = 128
ATT_OUT = 512
QKV = 1536
D_INNER = 2048
N_SSM_HEADS = 32
N_SSM_GROUPS = 4
D_STATE = 128
XBC = 3072
CHUNK = 128
N_IN = 11808
EPS = 1e-6
ROPE_THETA = 10000.0
NP = 12288
Q0, K0, V0, Z0, G0, X0, DT0 = 0, 1536, 3072, 4608, 6656, 8704, 11776
DTW = 128
LR, B1, B2, ADAM_EPS, WD, STEP = 0.001, 0.9, 0.999, 1e-08, 0.01, 10

LANES = 128
VMEM_LIMIT = 48 * 1024 * 1024
NEG = -1e30


def _cp(sem=None, **kw):
    return pltpu.CompilerParams(dimension_semantics=sem, vmem_limit_bytes=VMEM_LIMIT, **kw)


def _dg(a, b, ca, cb):
    return lax.dot_general(a.astype(BF16), b.astype(BF16), (((ca,), (cb,)), ((), ())), preferred_element_type=F32)


@jax.custom_vjp
def dot_nn(a, b):
    return _dg(a, b, 1, 0)


def _dot_nn_fwd(a, b):
    return _dg(a, b, 1, 0), (a, b)


def _dot_nn_bwd(r, g):
    a, b = r
    return _dg(g, b, 1, 1).astype(a.dtype), _dg(a, g, 0, 0).astype(b.dtype)


dot_nn.defvjp(_dot_nn_fwd, _dot_nn_bwd)


@jax.custom_vjp
def dot_nt(a, b):
    return _dg(a, b, 1, 1)


def _dot_nt_fwd(a, b):
    return _dg(a, b, 1, 1), (a, b)


def _dot_nt_bwd(r, g):
    a, b = r
    return _dg(g, b, 1, 0).astype(a.dtype), _dg(g, a, 0, 0).astype(b.dtype)


dot_nt.defvjp(_dot_nt_fwd, _dot_nt_bwd)


@jax.custom_vjp
def dot_tn(a, b):
    return _dg(a, b, 0, 0)


def _dot_tn_fwd(a, b):
    return _dg(a, b, 0, 0), (a, b)


def _dot_tn_bwd(r, g):
    a, b = r
    return _dg(b, g, 1, 1).astype(a.dtype), _dg(a, g, 1, 0).astype(b.dtype)


dot_tn.defvjp(_dot_tn_fwd, _dot_tn_bwd)


def _dot2_raw(a, e, ce):
    hi = a.astype(BF16)
    lo = (a - hi.astype(F32)).astype(BF16)
    return _dg(hi, e, 1, ce) + _dg(lo, e, 1, ce)


@jax.custom_vjp
def dot2(a, e):
    return _dot2_raw(a, e, 0)


def _dot2_fwd(a, e):
    return _dot2_raw(a, e, 0), e


def _dot2_bwd(e, g):
    return _dot2_raw(g, e, 1), jnp.zeros_like(e)


dot2.defvjp(_dot2_fwd, _dot2_bwd)


def _tri2_raw(l, x, cl):
    hi = x.astype(BF16)
    lo = (x - hi.astype(F32)).astype(BF16)
    return _dg(l, hi, cl, 0) + _dg(l, lo, cl, 0)


@jax.custom_vjp
def tri_matmul(l, x):
    return _tri2_raw(l, x, 1)


def _tri_fwd(l, x):
    return _tri2_raw(l, x, 1), l


def _tri_bwd(l, g):
    return jnp.zeros_like(l), _tri2_raw(l, g, 0)


tri_matmul.defvjp(_tri_fwd, _tri_bwd)


def _dup64_raw(w):
    return w + pltpu.roll(w, 64, 1)


@jax.custom_vjp
def dup64(w):
    return _dup64_raw(w)


def _dup64_fwd(w):
    return _dup64_raw(w), None


def _dup64_bwd(_, g):
    lane = lax.broadcasted_iota(jnp.int32, g.shape, 1)
    return (jnp.where(lane < 64, _dup64_raw(g), 0.0),)


dup64.defvjp(_dup64_fwd, _dup64_bwd)


def _rope_rot_raw(y, sign):
    lane = lax.broadcasted_iota(jnp.int32, y.shape, 1)
    first_half = (lane & 32) == 0
    return sign * jnp.where(first_half, -pltpu.roll(y, LANES - 32, 1), pltpu.roll(y, 32, 1))


@jax.custom_vjp
def rope_rot(y):
    return _rope_rot_raw(y, 1.0)


def _rope_rot_fwd(y):
    return _rope_rot_raw(y, 1.0), None


def _rope_rot_bwd(_, g):
    return (_rope_rot_raw(g, -1.0),)


rope_rot.defvjp(_rope_rot_fwd, _rope_rot_bwd)


def _shift_rows_raw(x, s):
    n = x.shape[0]
    r = pltpu.roll(x, s % n, 0)
    rows = lax.broadcasted_iota(jnp.int32, x.shape, 0)
    keep = rows >= s if s > 0 else rows < n + s
    return jnp.where(keep, r, 0.0)


@functools.partial(jax.custom_vjp, nondiff_argnums=(1,))
def shift_rows(x, s):
    return _shift_rows_raw(x, s)


def _shift_fwd(x, s):
    return _shift_rows_raw(x, s), None


def _shift_bwd(s, _, g):
    return (_shift_rows_raw(g, -s),)


shift_rows.defvjp(_shift_fwd, _shift_bwd)


def _sigmoid(x):
    return 1.0 / (1.0 + jnp.exp(-x))


def _silu(x):
    return x * _sigmoid(x)


def _softplus(x):
    return jnp.maximum(x, 0.0) + jnp.log(1.0 + jnp.exp(-jnp.abs(x)))


def _head_mean_mat():
    i = np.arange(LANES)
    return jnp.asarray((i[:, None] // 64 == i[None, :] // 64).astype(np.float32) / 64.0)


def _head_expand_mat():
    e = np.zeros((LANES, D_INNER), np.float32)
    for l in range(D_INNER):
        e[l // 64, l] = 1.0
    return jnp.asarray(e)


def _ltri_mat():
    i = np.arange(CHUNK)
    return jnp.asarray((i[:, None] >= i[None, :]).astype(np.float32))


def _rope_tables():
    pos = jnp.arange(SEQ, dtype=F32)
    inv_freq = 1.0 / (ROPE_THETA ** (jnp.arange(0, 64, 2, dtype=F32) / 64))
    ang = pos[:, None] * inv_freq[None, :]
    return jnp.tile(jnp.cos(ang), (1, 4)), jnp.tile(jnp.sin(ang), (1, 4))


def _pick(n, cap):
    best = None
    for t in range(LANES, min(n, cap) + 1, LANES):
        if n % t == 0:
            best = t
    return best if best is not None else n


def _mm(name, a, b, mode, out_dtype=F32, alpha=None, res=None):
    if mode == "nn":
        (m, k), n = a.shape, b.shape[1]
    elif mode == "nt":
        (m, k), n = a.shape, b.shape[0]
    else:
        (k, m), n = a.shape, b.shape[1]
    tm, tn, tk = _pick(m, 1408), _pick(n, 1408), _pick(k, 1024)
    nk = k // tk
    ca, cb = {"nn": (1, 0), "nt": (1, 1), "tn": (0, 0)}[mode]
    a_spec = pl.BlockSpec((tk, tm), lambda i, j, kk: (kk, i)) if mode == "tn" else pl.BlockSpec((tm, tk), lambda i, j, kk: (i, kk))
    b_spec = pl.BlockSpec((tn, tk), lambda i, j, kk: (j, kk)) if mode == "nt" else pl.BlockSpec((tk, tn), lambda i, j, kk: (kk, j))
    o_spec = pl.BlockSpec((tm, tn), lambda i, j, kk: (i, j))
    has_res = res is not None

    def finish(acc, res_ref, o_ref):
        if alpha is not None:
            acc = acc * alpha
        if has_res:
            acc = acc + res_ref[...].astype(F32)
        o_ref[...] = acc.astype(o_ref.dtype)

    def body(*refs):
        a_ref, b_ref = refs[0], refs[1]
        res_ref = refs[2] if has_res else None
        o_ref = refs[3] if has_res else refs[2]
        part = _dg(a_ref[...], b_ref[...], ca, cb)
        if nk == 1:
            finish(part, res_ref, o_ref)
            return
        acc_ref = refs[-1]
        kk = pl.program_id(2)

        @pl.when(kk == 0)
        def _():
            acc_ref[...] = part

        @pl.when(kk > 0)
        def _():
            acc_ref[...] += part

        @pl.when(kk == nk - 1)
        def _():
            finish(acc_ref[...], res_ref, o_ref)

    ins = [a, b] + ([res] if has_res else [])
    in_specs = [a_spec, b_spec] + ([o_spec] if has_res else [])
    return pl.pallas_call(
        body, name=name, grid=(m // tm, n // tn, nk), in_specs=in_specs, out_specs=o_spec,
        out_shape=jax.ShapeDtypeStruct((m, n), out_dtype),
        scratch_shapes=[pltpu.VMEM((tm, tn), F32)] if nk > 1 else [],
        compiler_params=_cp(("parallel", "parallel", "arbitrary")),
    )(*ins)


def _mmx(name, grid, a, b, out, contract, *, alpha=None, res=None, into=None):
    nk = grid[-1]
    has_res, has_into = res is not None, into is not None
    n_in = 2 + has_res + has_into

    def finish(acc, res_ref, o_ref):
        if alpha is not None:
            acc = acc * alpha
        if has_res:
            acc = acc + res_ref[...].astype(F32)
        o_ref[...] = acc.astype(o_ref.dtype)

    def body(*refs):
        res_ref = refs[2] if has_res else None
        o_ref = refs[n_in]
        part = _dg(refs[0][...], refs[1][...], *contract)
        if nk == 1:
            finish(part, res_ref, o_ref)
            return
        acc_ref = refs[-1]
        kk = pl.program_id(len(grid) - 1)

        @pl.when(kk == 0)
        def _():
            acc_ref[...] = part

        @pl.when(kk > 0)
        def _():
            acc_ref[...] += part

        @pl.when(kk == nk - 1)
        def _():
            finish(acc_ref[...], res_ref, o_ref)

    operands = [a, b] + ([res] if has_res else [])
    in_specs = [pl.BlockSpec(blk, im) for _, blk, im in operands] + ([ANY] if has_into else [])
    acc_shape = tuple(d for d in out[2] if d is not None)
    return pl.pallas_call(
        body, name=name, grid=grid, in_specs=in_specs, out_specs=pl.BlockSpec(out[2], out[3]),
        out_shape=jax.ShapeDtypeStruct(out[0], out[1]),
        scratch_shapes=[pltpu.VMEM(acc_shape, F32)] if nk > 1 else [],
        input_output_aliases={n_in - 1: 0} if has_into else {},
        compiler_params=_cp(("parallel",) * (len(grid) - 1) + ("arbitrary",)),
    )(*[o[0] for o in operands], *([into] if has_into else []))


def _ew(name, fn, grid, ins, outs, scratch=()):
    n_in, n_out = len(ins), len(outs)

    def body(*refs):
        vals = [r[...] for r in refs[:n_in]]
        res = fn(*vals, *refs[n_in + n_out:])
        for r, v in zip(refs[n_in:n_in + n_out], res):
            r[...] = v.astype(r.dtype)

    res = pl.pallas_call(
        body, name=name, grid=grid,
        in_specs=[pl.BlockSpec(b, m) for _, b, m in ins],
        out_specs=[pl.BlockSpec(b, m) for _, _, b, m in outs],
        out_shape=[jax.ShapeDtypeStruct(s, d) for s, d, _, _ in outs],
        scratch_shapes=list(scratch),
        compiler_params=_cp(("arbitrary",) * len(grid)),
    )(*[a for a, _, _ in ins])
    return res


def _ew_bwd(name, fn, grid, ins, cts, wrt, adds=(), ct_fn=None):
    n_in, n_ct, n_add = len(ins), len(cts), len(adds)
    idxs = [w["idx"] for w in wrt]
    intos = [(k, w["into"]) for k, w in enumerate(wrt) if w.get("into") is not None]

    def body(*refs):
        prim = [r[...] for r in refs[:n_in]]
        ct = [r[...].astype(F32) for r in refs[n_in:n_in + n_ct]]
        addv = [r[...] for r in refs[n_in + n_ct:n_in + n_ct + n_add]]
        orefs = refs[n_in + n_ct + n_add + len(intos):]

        def f(*sel):
            full = list(prim)
            for i, s in zip(idxs, sel):
                full[i] = s
            return fn(*full)

        _, vjp = jax.vjp(f, *[prim[i].astype(F32) for i in idxs])
        grads = vjp(tuple(ct) if ct_fn is None else ct_fn(*ct))
        for w, g, r in zip(wrt, grads, orefs):
            if w["kind"] == "tile":
                if w.get("add") is not None:
                    g = g + addv[w["add"]].astype(F32)
                r[...] = g.astype(r.dtype)
            else:
                first = w["first"]()

                @pl.when(first)
                def _(r=r, g=g):
                    r[...] = g.astype(r.dtype)

                @pl.when(jnp.logical_not(first))
                def _(r=r, g=g):
                    r[...] += g.astype(r.dtype)

    allin = list(ins) + list(cts) + list(adds)
    return pl.pallas_call(
        body, name=name, grid=grid,
        in_specs=[pl.BlockSpec(b, m) for _, b, m in allin] + [ANY] * len(intos),
        out_specs=[pl.BlockSpec(w["block"], w["imap"]) for w in wrt],
        out_shape=[jax.ShapeDtypeStruct(w["shape"], w["dtype"]) for w in wrt],
        input_output_aliases={len(allin) + q: k for q, (k, _) in enumerate(intos)},
        compiler_params=_cp(("arbitrary",) * len(grid)),
    )(*[a for a, _, _ in allin], *[a for _, a in intos])


def _rmsnorm_fn(x, w):
    return (x * lax.rsqrt(jnp.mean(x * x, axis=-1, keepdims=True) + EPS) * w,)


def _swiglu_fn(g, u):
    return (_silu(g) * u,)


def _qkprep_fn(t, w64, cos, sin, hmean):
    w = jnp.sum(dup64(jnp.broadcast_to(w64, (8, LANES))), axis=0, keepdims=True) * 0.125
    y = t * lax.rsqrt(dot2(t * t, hmean) + EPS) * w
    return (y * cos + rope_rot(y) * sin,)


def _att_fn(q, kp, kc, vp, vc, first):
    iq = lax.broadcasted_iota(jnp.int32, (BAND, 2 * BAND), 0)
    ik = lax.broadcasted_iota(jnp.int32, (BAND, 2 * BAND), 1)
    rel = BAND + iq - ik
    ok = (rel >= 0) & (rel <= BAND) & ((ik >= BAND) | jnp.logical_not(first))
    lane = lax.broadcasted_iota(jnp.int32, (1, LANES), 1)
    kcat = jnp.concatenate([kp, kc], axis=0)
    vcat = jnp.concatenate([vp, vc], axis=0)
    o_pair = jnp.zeros((BAND, LANES), F32)
    l_pair = jnp.zeros((BAND, LANES), F32)
    for hh in range(2):
        lm = (lane // 64 == hh).astype(F32)
        s = dot_nt(q * lm, kcat) * 0.125
        s = jnp.where(ok, s, NEG)
        mx = jnp.max(s, axis=-1, keepdims=True)
        e = jnp.exp(s - mx)
        den = jnp.sum(e, axis=-1, keepdims=True)
        o_pair = o_pair + dot_nn(e / den, vcat) * lm
        l_pair = l_pair + (mx + jnp.log(den)) * lm
    return o_pair, l_pair


def _attmix_fn(o0, o1, o2, l0, l1, l2):
    m = jnp.maximum(jnp.maximum(l0, l1), l2)
    e0, e1, e2 = jnp.exp(l0 - m), jnp.exp(l1 - m), jnp.exp(l2 - m)
    return ((e0 * o0 + e1 * o1 + e2 * o2) / (e0 + e1 + e2),)


def _conv_fn(x, w0, w1, w2, w3, b):
    pre = x * w3 + shift_rows(x, 1) * w2 + shift_rows(x, 2) * w1 + shift_rows(x, 3) * w0 + b
    return (_silu(pre),)


def _ssdpre_fn(dtraw, bias, alog, ex):
    dt = _softplus(dtraw + bias)
    da = dt * (-jnp.exp(alog))
    return dot2(dt, ex), dot2(da, ex)


def _ssd_step(st, x, dtb, dab, bm, cm, ltri):
    cum = tri_matmul(ltri, dab)
    cum_t = cum.T
    xdt = x * dtb
    cb = dot_nt(cm, bm)
    ri = lax.broadcasted_iota(jnp.int32, (CHUNK, CHUNK), 0)
    ci = lax.broadcasted_iota(jnp.int32, (CHUNK, CHUNK), 1)
    causal = ri >= ci
    lane = lax.broadcasted_iota(jnp.int32, (1, LANES), 1)
    rowi = lax.broadcasted_iota(jnp.int32, (LANES, 1), 0)
    ys = []
    for p in range(4):
        sl = slice(p * LANES, (p + 1) * LANES)
        cum_p, cum_tp, xdt_p = cum[:, sl], cum_t[sl, :], xdt[:, sl]
        acc = jnp.zeros((CHUNK, LANES), F32)
        for hh in range(2):
            col = jnp.sum(cum_p * (lane == 64 * hh).astype(F32), axis=1, keepdims=True)
            row = jnp.sum(cum_tp * (rowi == 64 * hh).astype(F32), axis=0, keepdims=True)
            dec = jnp.exp(jnp.where(causal, col - row, NEG))
            acc = acc + dot_nn(cb * dec, xdt_p * (lane // 64 == hh).astype(F32))
        ys.append(acc)
    y_diag = jnp.concatenate(ys, axis=1)
    y_off = dot_nn(cm, st) * jnp.exp(cum)
    last_row = (lax.broadcasted_iota(jnp.int32, (CHUNK, 1), 0) == CHUNK - 1).astype(F32)
    last = jnp.sum(cum * last_row, axis=0, keepdims=True)
    new_st = st * jnp.exp(last) + dot_tn(bm, xdt * jnp.exp(last - cum))
    return new_st, y_diag + y_off


def _ssdpost_fn(y, xs, z, dskip, ex, nw):
    db = jnp.sum(dot2(jnp.broadcast_to(dskip, (8, LANES)), ex), axis=0, keepdims=True) * 0.125
    y2 = (y + db * xs) * _silu(z)
    return (y2 * lax.rsqrt(jnp.mean(y2 * y2, axis=-1, keepdims=True) + EPS) * nw,)


def _merge_fn(ya, ys, ga, gs, ba, bs):
    return (_sigmoid(ga + ba) * ya + _sigmoid(gs + bs) * ys,)


TM = 512


def _full(shape):
    nd = len(shape)
    return (shape, lambda *_: (0,) * nd)


def _rmsnorm(name, x, w):
    t = x.shape[0]
    return _ew(name, _rmsnorm_fn, (t // TM,),
               [(x, (TM, D_MODEL), lambda i: (i, 0)), (w, (1, D_MODEL), lambda i: (0, 0))],
               [((t, D_MODEL), BF16, (TM, D_MODEL), lambda i: (i, 0))])[0]


def _rmsnorm_bwd(name, x, w, dh, dres):
    t = x.shape[0]
    row = ((TM, D_MODEL), lambda i: (i, 0))
    return _ew_bwd(name, _rmsnorm_fn, (t // TM,),
                   [(x, *row), (w, (1, D_MODEL), lambda i: (0, 0))], [(dh, *row)],
                   [dict(idx=0, kind="tile", shape=(t, D_MODEL), dtype=F32, block=row[0], imap=row[1], add=0),
                    dict(idx=1, kind="acc", shape=(1, D_MODEL), dtype=F32, block=(1, D_MODEL), imap=lambda i: (0, 0),
                         first=lambda: pl.program_id(0) == 0)],
                   adds=[(dres, *row)])


def _qk_operands(proj, qkw, cos, sin, consts, tm):
    nrow = SEQ // tm
    c = ((LANES, LANES), lambda j, i: (0, 0))
    return [(proj, (tm, LANES), lambda j, i: (i, j)),
            (qkw, (None, 1, LANES), lambda j, i: (j // 12, 0, 0)),
            (cos, (tm, LANES), lambda j, i: (i % nrow, 0)),
            (sin, (tm, LANES), lambda j, i: (i % nrow, 0)),
            (consts["hmean"], *c)]


def _qkprep(name, proj, qkw, cos, sin, consts):
    t = proj.shape[0]
    return _ew(name, _qkprep_fn, (2 * QKV // LANES, t // TM), _qk_operands(proj, qkw, cos, sin, consts, TM),
               [((t, 2 * QKV), F32, (TM, LANES), lambda j, i: (i, j))])[0]


def _qkprep_bwd(name, proj, qkw, cos, sin, consts, dq, dk, dproj):
    t = proj.shape[0]
    nq = QKV // LANES

    def pick(cq, ck):
        return (jnp.where(pl.program_id(0) < nq, cq, ck),)

    return _ew_bwd(name, _qkprep_fn, (2 * nq, t // TM), _qk_operands(proj, qkw, cos, sin, consts, TM),
                   [(d, (TM, LANES), lambda j, i: (i, j % nq)) for d in (dq, dk)],
                   [dict(idx=0, kind="tile", shape=dproj.shape, dtype=dproj.dtype, block=(TM, LANES),
                         imap=lambda j, i: (i, j), into=dproj),
                    dict(idx=1, kind="acc", shape=(2, 1, LANES), dtype=F32, block=(None, 1, LANES),
                         imap=lambda j, i: (j // 12, 0, 0),
                         first=lambda: (pl.program_id(0) % 12 == 0) & (pl.program_id(1) == 0))],
                   ct_fn=pick)


def _att_specs(dil, g):
    nb = SEQ // dil // BAND
    pt = 4 if dil == 1 else 1
    w = pt * LANES
    blk = (None, BAND * dil, w)
    kq, kk, kv = g * ATT_OUT // w, (QKV + g * ATT_OUT) // w, (V0 + g * ATT_OUT) // w

    def cur(n):
        return jnp.minimum(n, nb - 1)

    def prev(n):
        return jnp.maximum(jnp.minimum(n, nb - 1) - 1, 0)

    return nb, pt, blk, [
        pl.BlockSpec(blk, lambda b, p, n: (b, cur(n), kq + p)),
        pl.BlockSpec(blk, lambda b, p, n: (b, prev(n), kk + p)),
        pl.BlockSpec(blk, lambda b, p, n: (b, cur(n), kk + p)),
        pl.BlockSpec(blk, lambda b, p, n: (b, prev(n), kv + p)),
        pl.BlockSpec(blk, lambda b, p, n: (b, cur(n), kv + p)),
    ]


def _att_rows(r, dil):
    return pl.ds(r, BAND, stride=dil) if dil > 1 else pl.ds(0, BAND)


def _att_fwd(name, qk, proj, g):
    bl = qk.shape[0] // SEQ
    dil = ATT_DILATIONS[g]
    nb, pt, blk, specs = _att_specs(dil, g)
    qk3 = qk.reshape(bl, SEQ, 2 * QKV)
    proj3 = proj.reshape(bl, SEQ, NP)
    o_spec = pl.BlockSpec(blk, lambda b, p, n: (b, n, p))

    def body(q, kp, kc, vp, vc, o_ref, l_ref):
        first = pl.program_id(2) == 0

        def residue(r, carry):
            sl = _att_rows(r, dil)
            for p in range(pt):
                ln = pl.ds(p * LANES, LANES)
                o, l = _att_fn(q[sl, ln], kp[sl, ln], kc[sl, ln], vp[sl, ln], vc[sl, ln], first)
                o_ref[sl, ln] = o
                l_ref[sl, ln] = l
            return carry

        lax.fori_loop(0, dil, residue, 0)

    o, l = pl.pallas_call(
        body, name=name, grid=(bl, ATT_OUT // (pt * LANES), nb), in_specs=specs, out_specs=[o_spec, o_spec],
        out_shape=[jax.ShapeDtypeStruct((bl, SEQ, ATT_OUT), F32)] * 2,
        compiler_params=_cp(("arbitrary",) * 3),
    )(qk3, qk3, qk3, proj3, proj3)
    return o.reshape(bl * SEQ, ATT_OUT), l.reshape(bl * SEQ, ATT_OUT)


def _att_bwd(name, qk, proj, g, do, dl, dq_buf, dk_buf, dv_buf):
    bl = qk.shape[0] // SEQ
    dil = ATT_DILATIONS[g]
    nb, pt, blk, specs = _att_specs(dil, g)
    w = pt * LANES
    qk3 = qk.reshape(bl, SEQ, 2 * QKV)
    proj3 = proj.reshape(bl, SEQ, NP)
    ct_spec = pl.BlockSpec(blk, lambda b, p, n: (b, jnp.minimum(n, nb - 1), p))
    do3 = do.reshape(bl, SEQ, ATT_OUT)
    dl3 = dl.reshape(bl, SEQ, ATT_OUT)
    kg = g * ATT_OUT // w

    def body(q, kp, kc, vp, vc, do_ref, dl_ref, _a, _b, _c, d_ref, dk_ref, dv_ref, ck, cv):
        n = pl.program_id(2)

        def residue(r, carry):
            sl = _att_rows(r, dil)
            for p in range(pt):
                ln = pl.ds(p * LANES, LANES)

                @pl.when(n < nb)
                def _(ln=ln):
                    first = n == 0
                    prim = [ref[sl, ln] for ref in (q, kp, kc, vp, vc)]
                    _, vjp = jax.vjp(lambda *a: _att_fn(*a, first), *prim)
                    dq, dkp, dkc, dvp, dvc = vjp((do_ref[sl, ln], dl_ref[sl, ln]))
                    d_ref[sl, ln] = dq

                    @pl.when(n > 0)
                    def _():
                        dk_ref[sl, ln] = ck[sl, ln] + dkp
                        dv_ref[sl, ln] = cv[sl, ln] + dvp

                    ck[sl, ln] = dkc
                    cv[sl, ln] = dvc

                @pl.when(n == nb)
                def _(ln=ln):
                    dk_ref[sl, ln] = ck[sl, ln]
                    dv_ref[sl, ln] = cv[sl, ln]

            return carry

        lax.fori_loop(0, dil, residue, 0)

    bufs = [a.reshape(bl, SEQ, QKV) for a in (dq_buf, dk_buf, dv_buf)]
    o_specs = [
        pl.BlockSpec(blk, lambda b, p, n: (b, jnp.minimum(n, nb - 1), kg + p)),
        pl.BlockSpec(blk, lambda b, p, n: (b, jnp.maximum(n - 1, 0), kg + p)),
        pl.BlockSpec(blk, lambda b, p, n: (b, jnp.maximum(n - 1, 0), kg + p)),
    ]
    dq, dk, dv = pl.pallas_call(
        body, name=name, grid=(bl, ATT_OUT // w, nb + 1), in_specs=specs + [ct_spec, ct_spec, ANY, ANY, ANY],
        out_specs=o_specs, out_shape=[jax.ShapeDtypeStruct(a.shape, a.dtype) for a in bufs],
        input_output_aliases={7: 0, 8: 1, 9: 2},
        scratch_shapes=[pltpu.VMEM((BAND * dil, w), F32), pltpu.VMEM((BAND * dil, w), F32)],
        compiler_params=_cp(("arbitrary",) * 3),
    )(qk3, qk3, qk3, proj3, proj3, do3, dl3, *bufs)
    return dq.reshape(dq_buf.shape), dk.reshape(dk_buf.shape), dv.reshape(dv_buf.shape)


def _attmix(name, os_, ls_):
    t = os_[0].shape[0]
    blk = ((TM, ATT_OUT), lambda i: (i, 0))
    return _ew(name, _attmix_fn, (t // TM,), [(a, *blk) for a in (*os_, *ls_)], [((t, ATT_OUT), BF16, *blk)])[0]


def _attmix_bwd(name, os_, ls_, datt):
    t = os_[0].shape[0]
    blk = ((TM, ATT_OUT), lambda i: (i, 0))
    return _ew_bwd(name, _attmix_fn, (t // TM,), [(a, *blk) for a in (*os_, *ls_)], [(datt, *blk)],
                   [dict(idx=k, kind="tile", shape=(t, ATT_OUT), dtype=F32, block=blk[0], imap=blk[1]) for k in range(6)])


CONV_TC = 256


def _conv_operands(proj3, conv_w, conv_b):
    c0 = X0 // CONV_TC
    ins = [(proj3, (None, SEQ, CONV_TC), lambda j, b: (b, 0, c0 + j))]
    for k in range(4):
        ins.append((conv_w, (None, 1, CONV_TC), lambda j, b, k=k: (k, 0, j)))
    ins.append((conv_b, (1, CONV_TC), lambda j, b: (0, j)))
    return ins


def _conv(name, proj3, conv_w, conv_b):
    bl = proj3.shape[0]
    return _ew(name, _conv_fn, (XBC // CONV_TC, bl), _conv_operands(proj3, conv_w, conv_b),
               [((bl, SEQ, XBC), F32, (None, SEQ, CONV_TC), lambda j, b: (b, 0, j))])[0]


def _conv_bwd(name, proj3, conv_w, conv_b, dxs3, db3, dc3, dproj3):
    bl = proj3.shape[0]
    nx = D_INNER // CONV_TC
    nb_ = N_SSM_GROUPS * D_STATE // CONV_TC
    blk = (None, SEQ, CONV_TC)
    cts = [(dxs3, blk, lambda j, b: (b, 0, jnp.minimum(j, nx - 1))),
           (db3, blk, lambda j, b: (b, 0, jnp.clip(j - nx, 0, nb_ - 1))),
           (dc3, blk, lambda j, b: (b, 0, jnp.clip(j - nx - nb_, 0, nb_ - 1)))]

    def pick(cx, cb, cc):
        j = pl.program_id(0)
        return (jnp.where(j < nx, cx, jnp.where(j < nx + nb_, cb, cc)),)

    first = lambda: pl.program_id(1) == 0
    wrt = [dict(idx=0, kind="tile", shape=dproj3.shape, dtype=dproj3.dtype, block=blk,
                imap=lambda j, b: (b, 0, X0 // CONV_TC + j), into=dproj3)]
    for k in range(4):
        wrt.append(dict(idx=1 + k, kind="acc", shape=(1, XBC), dtype=F32, block=(1, CONV_TC),
                        imap=lambda j, b: (0, j), first=first))
    wrt.append(dict(idx=5, kind="acc", shape=(1, XBC), dtype=F32, block=(1, CONV_TC), imap=lambda j, b: (0, j), first=first))
    return _ew_bwd(name, _conv_fn, (XBC // CONV_TC, bl), _conv_operands(proj3, conv_w, conv_b), cts, wrt, ct_fn=pick)


SSD_TM = 256


def _ssdpre_operands(proj, dt_bias, a_log, ex):
    return [(proj, (SSD_TM, DTW), lambda i: (i, DT0 // DTW)), (dt_bias, *_full((1, DTW))), (a_log, *_full((1, DTW))),
            (ex, *_full((LANES, D_INNER)))]


def _ssdpre(name, proj, dt_bias, a_log, ex):
    t = proj.shape[0]
    blk = ((SSD_TM, D_INNER), lambda i: (i, 0))
    return _ew(name, _ssdpre_fn, (t // SSD_TM,), _ssdpre_operands(proj, dt_bias, a_log, ex),
               [((t, D_INNER), F32, *blk), ((t, D_INNER), F32, *blk)])


def _ssdpre_bwd(name, proj, dt_bias, a_log, ex, ddtb, ddab):
    t = proj.shape[0]
    blk = ((SSD_TM, D_INNER), lambda i: (i, 0))
    first = lambda: pl.program_id(0) == 0
    return _ew_bwd(name, _ssdpre_fn, (t // SSD_TM,), _ssdpre_operands(proj, dt_bias, a_log, ex),
                   [(ddtb, *blk), (ddab, *blk)],
                   [dict(idx=0, kind="tile", shape=(t, DTW), dtype=BF16, block=(SSD_TM, DTW), imap=lambda i: (i, 0)),
                    dict(idx=1, kind="acc", shape=(1, DTW), dtype=F32, block=(1, DTW), imap=lambda i: (0, 0), first=first),
                    dict(idx=2, kind="acc", shape=(1, DTW), dtype=F32, block=(1, DTW), imap=lambda i: (0, 0), first=first)])


def _ssd_in_specs(rev):
    nc = SEQ // CHUNK

    def c_(c):
        return nc - 1 - c if rev else c

    wide = (None, CHUNK, 4 * LANES)
    nar = (None, CHUNK, D_STATE)
    xb = D_INNER // D_STATE
    return [
        pl.BlockSpec(wide, lambda b, g, c: (b, c_(c), g)),
        pl.BlockSpec(wide, lambda b, g, c: (b, c_(c), g)),
        pl.BlockSpec(wide, lambda b, g, c: (b, c_(c), g)),
        pl.BlockSpec(nar, lambda b, g, c: (b, c_(c), xb + g)),
        pl.BlockSpec(nar, lambda b, g, c: (b, c_(c), xb + N_SSM_GROUPS + g)),
        pl.BlockSpec((CHUNK, CHUNK), lambda b, g, c: (0, 0)),
    ], c_


def _ssd_fwd(name, xc3, dtb3, dab3, ltri):
    bl = xc3.shape[0]
    nc = SEQ // CHUNK
    specs, _ = _ssd_in_specs(False)

    def body(x, dtb, dab, bm, cm, lt, y_ref, st_ref, st):
        @pl.when(pl.program_id(2) == 0)
        def _():
            st[...] = jnp.zeros_like(st)

        s0 = st[...]
        st_ref[...] = s0
        new_st, y = _ssd_step(s0, x[...], dtb[...], dab[...], bm[...], cm[...], lt[...])
        y_ref[...] = y
        st[...] = new_st

    return pl.pallas_call(
        body, name=name, grid=(bl, N_SSM_GROUPS, nc), in_specs=specs,
        out_specs=[pl.BlockSpec((None, CHUNK, 4 * LANES), lambda b, g, c: (b, c, g)),
                   pl.BlockSpec((None, None, None, D_STATE, 4 * LANES), lambda b, g, c: (b, g, c, 0, 0))],
        out_shape=[jax.ShapeDtypeStruct((bl, SEQ, D_INNER), F32),
                   jax.ShapeDtypeStruct((bl, N_SSM_GROUPS, nc, D_STATE, 4 * LANES), F32)],
        scratch_shapes=[pltpu.VMEM((D_STATE, 4 * LANES), F32)],
        compiler_params=_cp(("arbitrary",) * 3),
    )(xc3, dtb3, dab3, xc3, xc3, ltri)


def _ssd_bwd(name, xc3, dtb3, dab3, ltri, states, dy3, dxs_part3):
    bl = xc3.shape[0]
    nc = SEQ // CHUNK
    specs, c_ = _ssd_in_specs(True)
    wide = pl.BlockSpec((None, CHUNK, 4 * LANES), lambda b, g, c: (b, c_(c), g))
    nar = pl.BlockSpec((None, CHUNK, D_STATE), lambda b, g, c: (b, c_(c), g))
    st_spec = pl.BlockSpec((None, None, None, D_STATE, 4 * LANES), lambda b, g, c: (b, g, c_(c), 0, 0))

    def body(x, dtb, dab, bm, cm, lt, st_ref, dy, dxp, dx_ref, ddtb_ref, ddab_ref, dbm_ref, dcm_ref, dst):
        @pl.when(pl.program_id(2) == 0)
        def _():
            dst[...] = jnp.zeros_like(dst)

        ltv = lt[...]
        _, vjp = jax.vjp(lambda *a: _ssd_step(*a, ltv), st_ref[...], x[...], dtb[...], dab[...], bm[...], cm[...])
        d_st, d_x, d_dtb, d_dab, d_bm, d_cm = vjp((dst[...], dy[...]))
        dst[...] = d_st
        dx_ref[...] = d_x + dxp[...]
        ddtb_ref[...] = d_dtb
        ddab_ref[...] = d_dab
        dbm_ref[...] = d_bm
        dcm_ref[...] = d_cm

    big = jax.ShapeDtypeStruct((bl, SEQ, D_INNER), F32)
    small = jax.ShapeDtypeStruct((bl, SEQ, N_SSM_GROUPS * D_STATE), F32)
    return pl.pallas_call(
        body, name=name, grid=(bl, N_SSM_GROUPS, nc), in_specs=specs + [st_spec, wide, wide],
        out_specs=[wide, wide, wide, nar, nar], out_shape=[big, big, big, small, small],
        scratch_shapes=[pltpu.VMEM((D_STATE, 4 * LANES), F32)],
        compiler_params=_cp(("arbitrary",) * 3),
    )(xc3, dtb3, dab3, xc3, xc3, ltri, states, dy3, dxs_part3)


def _ssdpost_operands(y, xc, proj, d_skip, ex, nw):
    w = 4 * LANES
    return [(y, (SSD_TM, w), lambda j, i: (i, j)), (xc, (SSD_TM, w), lambda j, i: (i, j)),
            (proj, (SSD_TM, w), lambda j, i: (i, Z0 // w + j)), (d_skip, (1, DTW), lambda j, i: (0, 0)),
            (ex, (LANES, w), lambda j, i: (0, j)), (nw, (1, w), lambda j, i: (0, j))]


def _ssdpost(name, y, xc, proj, d_skip, ex, nw):
    t = y.shape[0]
    w = 4 * LANES
    return _ew(name, _ssdpost_fn, (D_INNER // w, t // SSD_TM), _ssdpost_operands(y, xc, proj, d_skip, ex, nw),
               [((t, D_INNER), BF16, (SSD_TM, w), lambda j, i: (i, j))])[0]


def _ssdpost_bwd(name, y, xc, proj, d_skip, ex, nw, dysn, dproj):
    t = y.shape[0]
    w = 4 * LANES
    blk = ((SSD_TM, w), lambda j, i: (i, j))
    return _ew_bwd(name, _ssdpost_fn, (D_INNER // w, t // SSD_TM), _ssdpost_operands(y, xc, proj, d_skip, ex, nw),
                   [(dysn, *blk)],
                   [dict(idx=0, kind="tile", shape=(t, D_INNER), dtype=F32, block=blk[0], imap=blk[1]),
                    dict(idx=1, kind="tile", shape=(t, D_INNER), dtype=F32, block=blk[0], imap=blk[1]),
                    dict(idx=2, kind="tile", shape=dproj.shape, dtype=dproj.dtype, block=blk[0],
                         imap=lambda j, i: (i, Z0 // w + j), into=dproj),
                    dict(idx=3, kind="acc", shape=(1, DTW), dtype=F32, block=(1, DTW), imap=lambda j, i: (0, 0),
                         first=lambda: (pl.program_id(0) == 0) & (pl.program_id(1) == 0)),
                    dict(idx=5, kind="acc", shape=(1, D_INNER), dtype=F32, block=(1, w), imap=lambda j, i: (0, j),
                         first=lambda: pl.program_id(1) == 0)])


def _merge_operands(ya, ys, proj, b_gates):
    w = 4 * LANES
    g0 = G0 // w
    nh = D_MODEL // w
    return [(ya, (TM, w), lambda j, i: (i, j)), (ys, (TM, w), lambda j, i: (i, j)),
            (proj, (TM, w), lambda j, i: (i, g0 + j)), (proj, (TM, w), lambda j, i: (i, g0 + nh + j)),
            (b_gates, (1, w), lambda j, i: (0, j)), (b_gates, (1, w), lambda j, i: (0, nh + j))]


def _merge(name, ya, ys, proj, b_gates):
    t = ya.shape[0]
    w = 4 * LANES
    return _ew(name, _merge_fn, (D_MODEL // w, t // TM), _merge_operands(ya, ys, proj, b_gates),
               [((t, D_MODEL), BF16, (TM, w), lambda j, i: (i, j))])[0]


def _merge_bwd(name, ya, ys, proj, b_gates, dmixed):
    t = ya.shape[0]
    w = 4 * LANES
    blk = ((TM, w), lambda j, i: (i, j))
    first = lambda: pl.program_id(1) == 0
    tile = lambda k, dt: dict(idx=k, kind="tile", shape=(t, D_MODEL), dtype=dt, block=blk[0], imap=blk[1])
    acc = lambda k: dict(idx=k, kind="acc", shape=(1, D_MODEL), dtype=F32, block=(1, w), imap=lambda j, i: (0, j), first=first)
    return _ew_bwd(name, _merge_fn, (D_MODEL // w, t // TM), _merge_operands(ya, ys, proj, b_gates), [(dmixed, *blk)],
                   [tile(0, BF16), tile(1, BF16), tile(2, BF16), tile(3, BF16), acc(4), acc(5)])


def _loss(name, y, tgt):
    t = y.shape[0]
    blk = pl.BlockSpec((TM, D_MODEL), lambda i: (i, 0))

    def body(y_ref, t_ref, dy_ref, l_ref):
        e = y_ref[...] - t_ref[...]
        dy_ref[...] = e * (1.0 / D_MODEL)
        part = jnp.sum(jnp.sum(e * e, axis=-1, keepdims=True), axis=0, keepdims=True) * (0.5 / D_MODEL)
        part = jnp.broadcast_to(part, (8, LANES))

        @pl.when(pl.program_id(0) == 0)
        def _():
            l_ref[...] = part

        @pl.when(pl.program_id(0) > 0)
        def _():
            l_ref[...] += part

    return pl.pallas_call(
        body, name=name, grid=(t // TM,), in_specs=[blk, blk],
        out_specs=[blk, pl.BlockSpec((8, LANES), lambda i: (0, 0))],
        out_shape=[jax.ShapeDtypeStruct((t, D_MODEL), F32), jax.ShapeDtypeStruct((8, LANES), F32)],
        compiler_params=_cp(("arbitrary",)),
    )(y, tgt)


def _adamw_fn(w, g, m, v):
    m2 = B1 * m + (1.0 - B1) * g
    v2 = B2 * v + (1.0 - B2) * (g * g)
    m_hat = m2 / (1.0 - B1 ** STEP)
    v_hat = v2 / (1.0 - B2 ** STEP)
    return -LR * (m_hat / (jnp.sqrt(v_hat) + ADAM_EPS) + WD * w), m2, v2


def _adamw(name, w, g, m, v):
    rows, cols = w.shape
    tm = rows
    for cand in (512, 256, 128, 64, 32, 16, 8):
        if rows % cand == 0 and cand * cols * 4 <= (1 << 21):
            tm = cand
            break
    blk = ((tm, cols), lambda i: (i, 0))
    return _ew(name, _adamw_fn, (rows // tm,), [(a, *blk) for a in (w, g, m, v)], [((rows, cols), F32, *blk)] * 3)


NCH = 4
TMM = 1024
TKK = 1024


def _ffn_fwd(tag, x, nw, wg, wu, wd, li):
    t, fc = x.shape[0], wg.shape[-1]
    h = _rmsnorm(tag + "_norm", x, nw)

    def up_body(h_ref, wg_ref, wu_ref, g_ref, u_ref, a_ref):
        hv = h_ref[...]
        g = _dg(hv, wg_ref[...], 1, 0).astype(BF16)
        u = _dg(hv, wu_ref[...], 1, 0).astype(BF16)
        g_ref[...] = g
        u_ref[...] = u
        a_ref[...] = _swiglu_fn(g.astype(F32), u.astype(F32))[0].astype(BF16)

    w_spec = pl.BlockSpec((None, None, D_MODEL, fc), lambda k, i: (k, li, 0, 0))
    o_spec = pl.BlockSpec((None, TMM, fc), lambda k, i: (k, i, 0))
    g, u, a = pl.pallas_call(
        up_body, name=tag + "_up_act", grid=(NCH, t // TMM),
        in_specs=[pl.BlockSpec((TMM, D_MODEL), lambda k, i: (i, 0)), w_spec, w_spec], out_specs=[o_spec] * 3,
        out_shape=[jax.ShapeDtypeStruct((NCH, t, fc), BF16)] * 3, compiler_params=_cp(("parallel", "parallel")),
    )(h, wg, wu)
    row = ((TMM, D_MODEL), lambda i, j, k: (i, 0))
    y = _mmx(tag + "_down", (t // TMM, 1, NCH),
             (a, (None, TMM, fc), lambda i, j, k: (k, i, 0)),
             (wd, (None, None, fc, D_MODEL), lambda i, j, k: (k, li, 0, 0)),
             ((t, D_MODEL), F32, *row), (1, 0), alpha=0.5, res=(x, *row))
    return y, (x, h, g, u, a)


def _ffn_bwd(tag, saved, nw, wg, wu, wd, li, dy, bufs):
    x, h, g, u, a = saved
    t, fc = x.shape[0], wg.shape[-1]
    bg, bu, bd = bufs
    def dact_body(dy_ref, wd_ref, g_ref, u_ref, dg_ref, du_ref):
        da = _dg(dy_ref[...], wd_ref[...], 1, 1) * 0.5
        _, vjp = jax.vjp(_swiglu_fn, g_ref[...].astype(F32), u_ref[...].astype(F32))
        dg, du = vjp((da,))
        dg_ref[...] = dg.astype(BF16)
        du_ref[...] = du.astype(BF16)

    c_spec = pl.BlockSpec((None, TMM, fc), lambda k, i: (k, i, 0))
    dg, du = pl.pallas_call(
        dact_body, name=tag + "_down_dx_act", grid=(NCH, t // TMM),
        in_specs=[pl.BlockSpec((TMM, D_MODEL), lambda k, i: (i, 0)),
                  pl.BlockSpec((None, None, fc, D_MODEL), lambda k, i: (k, li, 0, 0)), c_spec, c_spec],
        out_specs=[c_spec] * 2, out_shape=[jax.ShapeDtypeStruct((NCH, t, fc), BF16)] * 2,
        compiler_params=_cp(("parallel", "parallel")),
    )(dy, wd, g, u)
    bd = _mmx(tag + "_down_dw", (NCH, 1, t // TKK),
              (a, (None, TKK, fc), lambda k, j, kk: (k, kk, 0)),
              (dy, (TKK, D_MODEL), lambda k, j, kk: (kk, 0)),
              (bd.shape, BF16, (None, None, fc, D_MODEL), lambda k, j, kk: (li, k, 0, 0)), (0, 0), alpha=0.5, into=bd)
    def dw(name, d, buf):
        return _mmx(name, (NCH, 1, t // TKK),
                    (h, (TKK, D_MODEL), lambda k, i, kk: (kk, 0)),
                    (d, (None, TKK, fc), lambda k, i, kk: (k, kk, 0)),
                    (buf.shape, BF16, (None, None, D_MODEL, fc), lambda k, i, kk: (li, k, 0, 0)), (0, 0), into=buf)

    bg, bu = dw(tag + "_gate_dw", dg, bg), dw(tag + "_up_dw", du, bu)
    row = ((TMM, D_MODEL), lambda i, j, k: (i, 0))

    def dx_(name, d, w, res):
        return _mmx(name, (t // TMM, 1, NCH),
                    (d, (None, TMM, fc), lambda i, j, k: (k, i, 0)),
                    (w, (None, None, D_MODEL, fc), lambda i, j, k: (k, li, 0, 0)),
                    ((t, D_MODEL), F32, *row), (1, 1), res=None if res is None else (res, *row))

    dh = dx_(tag + "_up_dx", du, wu, dx_(tag + "_gate_dx", dg, wg, None))
    dx, dnw = _rmsnorm_bwd(tag + "_norm_bwd", x, nw, dh, dy)
    return dx, dnw, (bg, bu, bd)


def _mixer_fwd(tag, x, p, c):
    t = x.shape[0]
    bl = t // SEQ
    h = _rmsnorm(tag + "_norm", x, p["mix_norm_w"])
    proj = _mm(tag + "_in", h, p["w_in"], "nn")
    qk = _qkprep(tag + "_qk", proj, p["qkw"], c["cos"], c["sin"], c)
    os_, ls_ = [], []
    for g in range(3):
        o, l = _att_fwd(f"{tag}_att{g}", qk, proj, g)
        os_.append(o)
        ls_.append(l)
    att = _attmix(tag + "_attmix", os_, ls_)
    li = p["layer"]
    wa, ws, wo = p["w_att_proj"], p["w_ssm_proj"], p["w_out"]
    ca, cs, co = wa.shape[-1], ws.shape[-2], wo.shape[-2]
    row = ((TMM, D_MODEL), lambda i, j, k: (i, 0))
    ya = _mmx(tag + "_attproj", (t // TMM, NCH, 1),
              (att, (TMM, ATT_OUT), lambda i, k, kk: (i, 0)),
              (wa, (None, None, ATT_OUT, ca), lambda i, k, kk: (k, li, 0, 0)),
              ((t, D_MODEL), F32, (TMM, ca), lambda i, k, kk: (i, k)), (1, 0))
    proj3 = proj.reshape(bl, SEQ, NP)
    xc3 = _conv(tag + "_conv", proj3, p["conv_w"], p["conv_b"])
    xc = xc3.reshape(t, XBC)
    dtb, dab = _ssdpre(tag + "_ssdpre", proj, p["dt_bias"], p["a_log"], c["ex"])
    dtb3, dab3 = dtb.reshape(bl, SEQ, D_INNER), dab.reshape(bl, SEQ, D_INNER)
    y3, states = _ssd_fwd(tag + "_ssd", xc3, dtb3, dab3, c["ltri"])
    y = y3.reshape(t, D_INNER)
    ysn = _ssdpost(tag + "_ssdpost", y, xc, proj, p["d_skip"], c["ex"], p["ssm_norm_w"])
    ys = _mmx(tag + "_ssmproj", (t // TMM, 1, NCH),
              (ysn, (TMM, cs), lambda i, j, k: (i, k)),
              (ws, (None, None, cs, D_MODEL), lambda i, j, k: (k, li, 0, 0)),
              ((t, D_MODEL), F32, *row), (1, 0))
    mixed = _merge(tag + "_merge", ya, ys, proj, p["b_gates"])
    out = _mmx(tag + "_out", (t // TMM, 1, NCH),
               (mixed, (TMM, co), lambda i, j, k: (i, k)),
               (wo, (None, None, co, D_MODEL), lambda i, j, k: (k, li, 0, 0)),
               ((t, D_MODEL), F32, *row), (1, 0), res=(x, *row))
    return out, (x, h, proj, qk, os_, ls_, att, ya, xc3, dtb3, dab3, states, y, ysn, ys, mixed)


def _mixer_bwd(tag, saved, p, c, dout, bufs):
    x, h, proj, qk, os_, ls_, att, ya, xc3, dtb3, dab3, states, y, ysn, ys, mixed = saved
    t = x.shape[0]
    bl = t // SEQ
    xc = xc3.reshape(t, XBC)
    proj3 = proj.reshape(bl, SEQ, NP)
    gr = {}
    li = p["layer"]
    wa, ws, wo = p["w_att_proj"], p["w_ssm_proj"], p["w_out"]
    ca, cs, co = wa.shape[-1], ws.shape[-2], wo.shape[-2]
    b_att, b_ssm, b_out = bufs

    def chunk_dx(name, d, w, cw):
        return _mmx(name, (t // TMM, NCH, D_MODEL // TKK),
                    (d, (TMM, TKK), lambda i, k, kk: (i, kk)),
                    (w, (None, None, cw, TKK), lambda i, k, kk: (k, li, 0, kk)),
                    ((t, NCH * cw), F32, (TMM, cw), lambda i, k, kk: (i, k)), (1, 1))

    def full_dw(name, a_, d, buf):
        kdim = a_.shape[1]
        tm = min(kdim, 1024)
        return _mmx(name, (kdim // tm, 1, t // TKK),
                    (a_, (TKK, tm), lambda i, j, kk: (kk, i)),
                    (d, (TKK, D_MODEL), lambda i, j, kk: (kk, 0)),
                    (buf.shape, BF16, (None, tm, D_MODEL), lambda i, j, kk: (li, i, 0)), (0, 0), into=buf)

    dmixed = chunk_dx(tag + "_out_dx", dout, wo, co)
    b_out = full_dw(tag + "_out_dw", mixed, dout, b_out)
    dya, dys, dga, dgs, dba, dbs = _merge_bwd(tag + "_merge_bwd", ya, ys, proj, p["b_gates"], dmixed)
    gr["b_gates"] = jnp.concatenate([dba, dbs], axis=1)
    datt = _mmx(tag + "_attproj_dx", (t // TMM, 1, NCH),
                (dya, (TMM, ca), lambda i, j, k: (i, k)),
                (wa, (None, None, ATT_OUT, ca), lambda i, j, k: (k, li, 0, 0)),
                ((t, ATT_OUT), F32, (TMM, ATT_OUT), lambda i, j, k: (i, 0)), (1, 1))
    b_att = _mmx(tag + "_attproj_dw", (NCH, 1, t // TKK),
                 (att, (TKK, ATT_OUT), lambda k, j, kk: (kk, 0)),
                 (dya, (TKK, ca), lambda k, j, kk: (kk, k)),
                 (b_att.shape, BF16, (None, None, ATT_OUT, ca), lambda k, j, kk: (li, k, 0, 0)), (0, 0), into=b_att)
    dysn = chunk_dx(tag + "_ssmproj_dx", dys, ws, cs)
    b_ssm = full_dw(tag + "_ssmproj_dw", ysn, dys, b_ssm)
    gr["bufs"] = (b_att, b_ssm, b_out)
    dmix = _attmix_bwd(tag + "_attmix_bwd", os_, ls_, datt)
    dq = dk = dv = jnp.zeros((t, QKV), F32)
    for g in range(3):
        dq, dk, dv = _att_bwd(f"{tag}_att{g}_bwd", qk, proj, g, dmix[g], dmix[3 + g], dq, dk, dv)
    dproj = jnp.zeros((t, NP), BF16)
    dproj = lax.dynamic_update_slice(dproj, dv.astype(BF16), (0, V0))
    dproj = lax.dynamic_update_slice(dproj, dga, (0, G0))
    dproj = lax.dynamic_update_slice(dproj, dgs, (0, G0 + D_MODEL))
    dproj, gr["qkw"] = _qkprep_bwd(tag + "_qk_bwd", proj, p["qkw"], c["cos"], c["sin"], c, dq, dk, dproj)
    dy, dxs_part, dproj, gr["d_skip"], gr["ssm_norm_w"] = _ssdpost_bwd(
        tag + "_ssdpost_bwd", y, xc, proj, p["d_skip"], c["ex"], p["ssm_norm_w"], dysn, dproj)
    dxs3, ddtb3, ddab3, db3, dc3 = _ssd_bwd(
        tag + "_ssd_bwd", xc3, dtb3, dab3, c["ltri"], states, dy.reshape(bl, SEQ, D_INNER), dxs_part.reshape(bl, SEQ, D_INNER))
    ddt, gr["dt_bias"], gr["a_log"] = _ssdpre_bwd(
        tag + "_ssdpre_bwd", proj, p["dt_bias"], p["a_log"], c["ex"], ddtb3.reshape(t, D_INNER), ddab3.reshape(t, D_INNER))
    dproj = lax.dynamic_update_slice(dproj, ddt, (0, DT0))
    dproj3, dcw0, dcw1, dcw2, dcw3, gr["conv_b"] = _conv_bwd(
        tag + "_conv_bwd", proj3, p["conv_w"], p["conv_b"], dxs3, db3, dc3, dproj.reshape(bl, SEQ, NP))
    dproj = dproj3.reshape(t, NP)
    gr["conv_w"] = jnp.concatenate([dcw0, dcw1, dcw2, dcw3], axis=0)
    gr["w_in"] = _mm(tag + "_in_dw", h, dproj, "tn", out_dtype=BF16)
    dh = _mm(tag + "_in_dx", dproj, p["w_in"], "nt")
    dx, gr["mix_norm_w"] = _rmsnorm_bwd(tag + "_norm_bwd", x, p["mix_norm_w"], dh, dout)
    return dx, gr


def _constants():
    cos, sin = _rope_tables()
    return dict(cos=cos, sin=sin, hmean=_head_mean_mat(),
                ex=_head_expand_mat(), ltri=_ltri_mat())


ANY = pl.BlockSpec(memory_space=pl.ANY)


def _mesh_pos():
    return lax.axis_index("x"), lax.axis_index("y"), lax.axis_index("c")


def _other_chips(x, y):
    return [(1 - x, y), (x, 1 - y), (1 - x, 1 - y)]


def _all_gather_weights(ws, chip_idx):
    n = len(ws)
    hl = ws[0].shape[0] // 2
    inits = [lax.dynamic_update_slice(jnp.zeros((NCH, *w.shape), w.dtype), w[None], (chip_idx[0], 0, 0, 0)) for w in ws]

    def body(*refs):
        srcs, outs = refs[:n], refs[2 * n:3 * n]
        send_sems, recv_sems = refs[3 * n], refs[3 * n + 1]
        x, y, c = _mesh_pos()
        chips = _other_chips(x, y)

        def part(a, chip, hf):
            return outs[a].at[2 * chip[0] + chip[1], pl.ds(hf * hl, hl)]

        def copy(a, k, src_ref, dst_ref, to):
            return pltpu.make_async_remote_copy(src_ref=src_ref, dst_ref=dst_ref, send_sem=send_sems.at[6 * a + k],
                                                recv_sem=recv_sems.at[6 * a + k], device_id=to, device_id_type=MESH)

        first = [copy(a, j, srcs[a].at[pl.ds(c * hl, hl)], part(a, (x, y), c), (*chip, c))
                 for a in range(n) for j, chip in enumerate(chips)]
        for cp in first:
            cp.start()
        passed = []
        for a in range(n):
            for j, chip in enumerate(chips):
                copy(a, j, part(a, chip, c), part(a, chip, c), (x, y, c)).wait_recv()
                fw = copy(a, 3 + j, part(a, chip, c), part(a, chip, c), (x, y, 1 - c))
                fw.start()
                passed.append(fw)
        for a in range(n):
            for j, chip in enumerate(chips):
                copy(a, 3 + j, part(a, chip, 1 - c), part(a, chip, 1 - c), (x, y, c)).wait_recv()
        for cp in first + passed:
            cp.wait_send()

    return pl.pallas_call(
        body, name="all_gather_weights", out_shape=[jax.ShapeDtypeStruct(i.shape, i.dtype) for i in inits],
        in_specs=[ANY] * (2 * n), out_specs=[ANY] * n, input_output_aliases={n + a: a for a in range(n)},
        scratch_shapes=[pltpu.SemaphoreType.DMA((6 * n,)), pltpu.SemaphoreType.DMA((6 * n,))],
    )(*ws, *inits)


def _pair_exchange(gs):
    n = len(gs)

    def body(*refs):
        srcs, outs, send_sems, recv_sems = refs[:n], refs[n:2 * n], refs[2 * n], refs[2 * n + 1]
        x, y, c = _mesh_pos()
        cps = []
        for a in range(n):
            h = gs[a].shape[2] // 2
            cps.append(pltpu.make_async_remote_copy(
                src_ref=srcs[a].at[:, :, pl.ds((1 - c) * h, h), :], dst_ref=outs[a], send_sem=send_sems.at[a],
                recv_sem=recv_sems.at[a], device_id=(x, y, 1 - c), device_id_type=MESH))
        for cp in cps:
            cp.start()
        for cp in cps:
            cp.wait()

    return pl.pallas_call(
        body, name="grad_pair_exchange",
        out_shape=[jax.ShapeDtypeStruct((g.shape[0], g.shape[1], g.shape[2] // 2, g.shape[3]), g.dtype) for g in gs],
        in_specs=[ANY] * n, out_specs=[ANY] * n,
        scratch_shapes=[pltpu.SemaphoreType.DMA((n,)), pltpu.SemaphoreType.DMA((n,))],
    )(*gs)


def _chip_exchange(hs):
    n = len(hs)

    def body(*refs):
        srcs, outs, send_sems, recv_sems = refs[:n], refs[n:2 * n], refs[2 * n], refs[2 * n + 1]
        x, y, c = _mesh_pos()
        cps = [pltpu.make_async_remote_copy(
            src_ref=srcs[a].at[:, 2 * chip[0] + chip[1]], dst_ref=outs[a].at[j], send_sem=send_sems.at[3 * a + j],
            recv_sem=recv_sems.at[3 * a + j], device_id=(*chip, c), device_id_type=MESH)
            for a in range(n) for j, chip in enumerate(_other_chips(x, y))]
        for cp in cps:
            cp.start()
        for cp in cps:
            cp.wait()

    return pl.pallas_call(
        body, name="grad_chip_exchange",
        out_shape=[jax.ShapeDtypeStruct((3, h.shape[0], h.shape[2], h.shape[3]), h.dtype) for h in hs],
        in_specs=[ANY] * n, out_specs=[ANY] * n,
        scratch_shapes=[pltpu.SemaphoreType.DMA((3 * n,)), pltpu.SemaphoreType.DMA((3 * n,))],
    )(*hs)


def _pair_share(rs):
    n = len(rs)

    def body(*refs):
        outs, send_sems, recv_sems = refs[n:2 * n], refs[2 * n], refs[2 * n + 1]
        x, y, c = _mesh_pos()
        cps = [pltpu.make_async_remote_copy(src_ref=outs[a].at[:, c], dst_ref=outs[a].at[:, c], send_sem=send_sems.at[a],
                                            recv_sem=recv_sems.at[a], device_id=(x, y, 1 - c), device_id_type=MESH)
               for a in range(n)]
        for cp in cps:
            cp.start()
        for a in range(n):
            pltpu.make_async_remote_copy(src_ref=outs[a].at[:, 1 - c], dst_ref=outs[a].at[:, 1 - c],
                                         send_sem=send_sems.at[a], recv_sem=recv_sems.at[a], device_id=(x, y, c),
                                         device_id_type=MESH).wait_recv()
        for cp in cps:
            cp.wait_send()

    return pl.pallas_call(
        body, name="grad_pair_share", out_shape=[jax.ShapeDtypeStruct(r.shape, r.dtype) for r in rs],
        in_specs=[ANY] * n, out_specs=[ANY] * n, input_output_aliases={a: a for a in range(n)},
        scratch_shapes=[pltpu.SemaphoreType.DMA((n,)), pltpu.SemaphoreType.DMA((n,))],
    )(*rs)


def _pair_sum(name, g, recv, c_idx):
    d, k, h, b = recv.shape
    g5 = g.reshape(d * k, 2, h, b)

    def body(c_ref, a_ref, b_ref, o_ref):
        o_ref[...] = (a_ref[...].astype(F32) + b_ref[...].astype(F32)).astype(o_ref.dtype)

    out = pl.pallas_call(
        body, name=name,
        grid_spec=pltpu.PrefetchScalarGridSpec(
            num_scalar_prefetch=1, grid=(d * k,),
            in_specs=[pl.BlockSpec((None, None, h, b), lambda i, c: (i, c[0], 0, 0)),
                      pl.BlockSpec((None, h, b), lambda i, c: (i, 0, 0))],
            out_specs=pl.BlockSpec((None, h, b), lambda i, c: (i, 0, 0))),
        out_shape=jax.ShapeDtypeStruct((d * k, h, b), BF16),
        compiler_params=_cp(("arbitrary",)),
    )(c_idx, g5, recv.reshape(d * k, h, b))
    return out.reshape(d, k, h, b)


def _chip_sum(name, ha, recv, chip_idx, c_idx):
    d, _, h, b = ha.shape

    def body(k_ref, c_ref, a_ref, r0, r1, r2, o_ref):
        o_ref[...] = ((a_ref[...].astype(F32) + r0[...].astype(F32)) + r1[...].astype(F32)) + r2[...].astype(F32)

    blk = (None, None, h, b)
    return pl.pallas_call(
        body, name=name,
        grid_spec=pltpu.PrefetchScalarGridSpec(
            num_scalar_prefetch=2, grid=(d,),
            in_specs=[pl.BlockSpec(blk, lambda l, k, c: (l, k[0], 0, 0))] +
                     [pl.BlockSpec(blk, lambda l, k, c, j=j: (j, l, 0, 0)) for j in range(3)],
            out_specs=pl.BlockSpec(blk, lambda l, k, c: (l, c[0], 0, 0))),
        out_shape=jax.ShapeDtypeStruct((d, 2, h, b), F32),
        compiler_params=_cp(("arbitrary",)),
    )(chip_idx, c_idx, ha, recv, recv, recv)


def _all_sum_small(name, vec):
    rows = vec.shape[0]

    def body(v_ref, o_ref, buf, send_sems, recv_sems):
        x, y, c = _mesh_pos()
        me, sibling = (x, y, c), (x, y, 1 - c)
        chips = _other_chips(x, y)

        def slot(p):
            return buf.at[4 * p[0] + 2 * p[1] + p[2]]

        def copy(k, block, to, src=None):
            return pltpu.make_async_remote_copy(src_ref=slot(block) if src is None else src, dst_ref=slot(block),
                                                send_sem=send_sems.at[k], recv_sem=recv_sems.at[k],
                                                device_id=to, device_id_type=MESH)

        first = [copy(0, me, sibling, src=v_ref)]
        first += [copy(1 + j, me, (*chip, c), src=v_ref) for j, chip in enumerate(chips)]
        for cp in first:
            cp.start()
        passed = [copy(4 + j, (*chip, c), sibling) for j, chip in enumerate(chips)]
        for j, chip in enumerate(chips):
            copy(1 + j, (*chip, c), me).wait_recv()
            passed[j].start()
        copy(0, sibling, me).wait_recv()
        for j, chip in enumerate(chips):
            copy(4 + j, (*chip, 1 - c), me).wait_recv()
        for cp in first + passed:
            cp.wait_send()
        slot(me)[...] = v_ref[...]
        acc = buf[0]
        for k in range(1, 8):
            acc = acc + buf[k]
        o_ref[...] = acc

    vm = pl.BlockSpec(memory_space=pltpu.VMEM)
    return pl.pallas_call(
        body, name=name, out_shape=jax.ShapeDtypeStruct((rows, LANES), F32),
        in_specs=[vm], out_specs=vm, compiler_params=pltpu.CompilerParams(vmem_limit_bytes=VMEM_LIMIT),
        scratch_shapes=[pltpu.VMEM((8, rows, LANES), F32), pltpu.SemaphoreType.DMA((7,)), pltpu.SemaphoreType.DMA((7,))],
    )(vec)


def _pad_lanes(v, n=LANES):
    return jnp.pad(v, (0, n - v.shape[0]))[None, :]


def _w_in_to_kernel(w):
    return jnp.concatenate([w[:, :6656], w[:, 9760:N_IN], w[:, 6656:9728], w[:, 9728:9760],
                            jnp.zeros((w.shape[0], NP - N_IN), w.dtype)], axis=1)


def _w_in_from_kernel(w):
    return jnp.concatenate([w[:, :6656], w[:, X0:DT0], w[:, DT0:DT0 + 32], w[:, G0:X0]], axis=1)


def _layer_params(big, small, i):
    p = {k: big[k] for k in ("w_att_proj", "w_ssm_proj", "w_out")}
    p["layer"] = i
    w_in = big["w_in"]
    cw = w_in.shape[-1]
    pieces = []
    for lo, hi in ((0, 6656), (9760, N_IN), (6656, 9728), (9728, 9760)):
        for k in range(NCH):
            a, b = max(lo, k * cw), min(hi, (k + 1) * cw)
            if a < b:
                pieces.append(w_in[k, i, :, a - k * cw:b - k * cw])
    p["w_in"] = jnp.concatenate(pieces + [jnp.zeros((D_MODEL, NP - N_IN), w_in.dtype)], axis=1)
    p["conv_w"] = big["conv_w"][i][:, None, :]
    for k in ("ffn1_norm_w", "mix_norm_w", "ffn2_norm_w", "b_gates", "conv_b", "ssm_norm_w"):
        p[k] = small[k][i][None, :]
    for k in ("dt_bias", "a_log", "d_skip"):
        p[k] = _pad_lanes(small[k][i])
    p["qkw"] = jnp.stack([_pad_lanes(small["q_norm_w"][i]), _pad_lanes(small["k_norm_w"][i])])
    return p


GRAD_BUFS = ("ffn1_w_gate", "ffn1_w_up", "ffn1_w_down", "w_att_proj", "w_ssm_proj", "w_out",
             "ffn2_w_gate", "ffn2_w_up", "ffn2_w_down")


def _local_step(x, tgt, layers, big, c):
    depth = len(layers)
    saved = []
    for i, p in enumerate(layers):
        x, s1 = _ffn_fwd(f"L{i}_ffn1", x, p["ffn1_norm_w"], big["ffn1_w_gate"], big["ffn1_w_up"], big["ffn1_w_down"], i)
        x, s2 = _mixer_fwd(f"L{i}_mix", x, p, c)
        x, s3 = _ffn_fwd(f"L{i}_ffn2", x, p["ffn2_norm_w"], big["ffn2_w_gate"], big["ffn2_w_up"], big["ffn2_w_down"], i)
        saved.append((s1, s2, s3))
    dx, loss_blk = _loss("loss", x, tgt)
    buf = {}
    for n in GRAD_BUFS:
        _, _, a, b = big[n].shape
        shape = (depth, NCH * a, b) if n in ("w_ssm_proj", "w_out") else (depth, NCH, a, b)
        buf[n] = jnp.zeros(shape, BF16)
    grads = [None] * depth
    for i in reversed(range(depth)):
        p = layers[i]
        s1, s2, s3 = saved[i]
        dx, dn2, (buf["ffn2_w_gate"], buf["ffn2_w_up"], buf["ffn2_w_down"]) = _ffn_bwd(
            f"L{i}_ffn2", s3, p["ffn2_norm_w"], big["ffn2_w_gate"], big["ffn2_w_up"], big["ffn2_w_down"], i, dx,
            (buf["ffn2_w_gate"], buf["ffn2_w_up"], buf["ffn2_w_down"]))
        dx, gr = _mixer_bwd(f"L{i}_mix", s2, p, c, dx, (buf["w_att_proj"], buf["w_ssm_proj"], buf["w_out"]))
        buf["w_att_proj"], buf["w_ssm_proj"], buf["w_out"] = gr.pop("bufs")
        dx, dn1, (buf["ffn1_w_gate"], buf["ffn1_w_up"], buf["ffn1_w_down"]) = _ffn_bwd(
            f"L{i}_ffn1", s1, p["ffn1_norm_w"], big["ffn1_w_gate"], big["ffn1_w_up"], big["ffn1_w_down"], i, dx,
            (buf["ffn1_w_gate"], buf["ffn1_w_up"], buf["ffn1_w_down"]))
        gr.update(ffn1_norm_w=dn1, ffn2_norm_w=dn2)
        grads[i] = gr
    for n in ("w_ssm_proj", "w_out"):
        _, _, a, b = big[n].shape
        buf[n] = buf[n].reshape(depth, NCH, a, b)
    return loss_blk, dx, grads, buf


WEIGHTS = ["ffn1_norm_w", "ffn1_w_gate", "ffn1_w_up", "ffn1_w_down", "mix_norm_w", "w_in", "b_gates", "q_norm_w",
           "k_norm_w", "conv_w", "conv_b", "dt_bias", "a_log", "d_skip", "ssm_norm_w", "w_att_proj", "w_ssm_proj",
           "w_out", "ffn2_norm_w", "ffn2_w_gate", "ffn2_w_up", "ffn2_w_down"]
SHARD_AXIS = {"ffn1_w_gate": 2, "ffn1_w_up": 2, "ffn1_w_down": 1, "w_in": 2, "conv_w": 2, "w_att_proj": 2,
              "w_ssm_proj": 1, "w_out": 1, "ffn2_w_gate": 2, "ffn2_w_up": 2, "ffn2_w_down": 1}
BIG = [n for n in WEIGHTS if n in SHARD_AXIS]
SMALL = [n for n in WEIGHTS if n not in SHARD_AXIS]
def _from_flat(flat, shapes):
    v = flat.reshape(-1)
    out, off = [], 0
    for s in shapes:
        n = math.prod(s)
        out.append(v[off:off + n].reshape(s))
        off += n
    return out


def _pack_small(parts):
    v = jnp.concatenate([p.astype(F32).reshape(-1) for p in parts])
    rows = -(-v.shape[0] // (8 * LANES)) * 8
    return jnp.pad(v, (0, rows * LANES - v.shape[0])).reshape(rows, LANES)


def kernel(x, ffn1_norm_w, ffn1_w_gate, ffn1_w_up, ffn1_w_down, mix_norm_w, w_in, b_gates, q_norm_w, k_norm_w, conv_w, conv_b, dt_bias, a_log, d_skip, ssm_norm_w, w_att_proj, w_ssm_proj, w_out, ffn2_norm_w, ffn2_w_gate, ffn2_w_up, ffn2_w_down, loss_target, m_ffn1_norm_w, m_ffn1_w_gate, m_ffn1_w_up, m_ffn1_w_down, m_mix_norm_w, m_w_in, m_b_gates, m_q_norm_w, m_k_norm_w, m_conv_w, m_conv_b, m_dt_bias, m_a_log, m_d_skip, m_ssm_norm_w, m_w_att_proj, m_w_ssm_proj, m_w_out, m_ffn2_norm_w, m_ffn2_w_gate, m_ffn2_w_up, m_ffn2_w_down, v_ffn1_norm_w, v_ffn1_w_gate, v_ffn1_w_up, v_ffn1_w_down, v_mix_norm_w, v_w_in, v_b_gates, v_q_norm_w, v_k_norm_w, v_conv_w, v_conv_b, v_dt_bias, v_a_log, v_d_skip, v_ssm_norm_w, v_w_att_proj, v_w_ssm_proj, v_w_out, v_ffn2_norm_w, v_ffn2_w_gate, v_ffn2_w_up, v_ffn2_w_down):
    w = dict(zip(WEIGHTS, (ffn1_norm_w, ffn1_w_gate, ffn1_w_up, ffn1_w_down, mix_norm_w, w_in, b_gates, q_norm_w, k_norm_w, conv_w, conv_b, dt_bias, a_log, d_skip, ssm_norm_w, w_att_proj, w_ssm_proj, w_out, ffn2_norm_w, ffn2_w_gate, ffn2_w_up, ffn2_w_down)))
    m = dict(zip(WEIGHTS, (m_ffn1_norm_w, m_ffn1_w_gate, m_ffn1_w_up, m_ffn1_w_down, m_mix_norm_w, m_w_in, m_b_gates, m_q_norm_w, m_k_norm_w, m_conv_w, m_conv_b, m_dt_bias, m_a_log, m_d_skip, m_ssm_norm_w, m_w_att_proj, m_w_ssm_proj, m_w_out, m_ffn2_norm_w, m_ffn2_w_gate, m_ffn2_w_up, m_ffn2_w_down)))
    v = dict(zip(WEIGHTS, (v_ffn1_norm_w, v_ffn1_w_gate, v_ffn1_w_up, v_ffn1_w_down, v_mix_norm_w, v_w_in, v_b_gates, v_q_norm_w, v_k_norm_w, v_conv_w, v_conv_b, v_dt_bias, v_a_log, v_d_skip, v_ssm_norm_w, v_w_att_proj, v_w_ssm_proj, v_w_out, v_ffn2_norm_w, v_ffn2_w_gate, v_ffn2_w_up, v_ffn2_w_down)))
    depth = ffn1_norm_w.shape[0]
    bl = x.shape[0]
    t = bl * SEQ
    mx, my, mc = lax.axis_index("x"), lax.axis_index("y"), lax.axis_index("c")
    c_idx = mc.astype(jnp.int32).reshape(1)
    chip_idx = (2 * mx + my).astype(jnp.int32).reshape(1)

    cw_width = conv_w.shape[2]
    slots = lax.dynamic_update_slice(jnp.zeros((NCH, *conv_w.shape), F32), jnp.where(mc == 0, conv_w, 0.0)[None],
                                     (chip_idx[0], 0, 0, 0))
    conv_all = _all_sum_small("conv_gather", slots.reshape(-1, LANES)).reshape(NCH, *conv_w.shape)
    big = {"conv_w": jnp.concatenate([conv_all[k] for k in range(NCH)], axis=2)}

    mm_names = [n for n in BIG if n != "conv_w"]
    big.update(zip(mm_names, _all_gather_weights([w[n].astype(BF16) for n in mm_names], chip_idx)))
    small = {n: w[n] for n in SMALL}

    c = _constants()
    layers = [_layer_params(big, small, i) for i in range(depth)]
    loss_blk, dx, grads, buf = _local_step(x.reshape(t, D_MODEL), loss_target.reshape(t, D_MODEL), layers, big, c)
    grad_x = dx.reshape(bl, SEQ, D_MODEL)

    buf["w_in"] = jnp.stack([_w_in_from_kernel(g["w_in"]).reshape(D_MODEL, NCH, -1).transpose(1, 0, 2) for g in grads])
    mine = [buf[n] for n in mm_names]
    from_sibling = _pair_exchange(mine)
    pairs = [_pair_sum("grad_pair_sum_" + n, g, r, c_idx) for n, g, r in zip(mm_names, mine, from_sibling)]
    from_chips = _chip_exchange(pairs)
    halves = [_chip_sum("grad_chip_sum_" + n, h, r, chip_idx, c_idx) for n, h, r in zip(mm_names, pairs, from_chips)]
    g_big = {n: r.reshape(w[n].shape) for n, r in zip(mm_names, _pair_share(halves))}

    def small_grad(n):
        if n == "q_norm_w":
            return jnp.stack([g["qkw"][0, 0, :64] for g in grads])
        if n == "k_norm_w":
            return jnp.stack([g["qkw"][1, 0, :64] for g in grads])
        return jnp.stack([g[n][0, :w[n].shape[1]] for g in grads])

    small_shapes = [w[n].shape for n in SMALL]
    conv_shape = (depth, conv_w.shape[1], NCH * cw_width)
    tot = _all_sum_small("small_all_sum", _pack_small(
        [small_grad(n) for n in SMALL] + [jnp.stack([g["conv_w"] for g in grads]), loss_blk[0, :1]]))
    unpacked = _from_flat(tot, small_shapes + [conv_shape, (1,)])
    g_small = dict(zip(SMALL, unpacked[:-2]))
    g_big["conv_w"] = lax.dynamic_slice_in_dim(unpacked[-2], chip_idx[0] * cw_width, cw_width, axis=2)
    loss = unpacked[-1][0]

    grad, delta, new_m, new_v = {}, {}, {}, {}
    for n in BIG:
        shp = w[n].shape
        two_d = (shp[0] * shp[1], shp[2])
        d_, m_, v_ = _adamw("adamw_" + n, w[n].reshape(two_d), g_big[n].reshape(two_d), m[n].reshape(two_d), v[n].reshape(two_d))
        grad[n], delta[n], new_m[n], new_v[n] = g_big[n], d_.reshape(shp), m_.reshape(shp), v_.reshape(shp)
    d_, m_, v_ = _adamw("adamw_small", _pack_small([w[n] for n in SMALL]), _pack_small([g_small[n] for n in SMALL]),
                        _pack_small([m[n] for n in SMALL]), _pack_small([v[n] for n in SMALL]))
    for n, a, b, c_ in zip(SMALL, _from_flat(d_, small_shapes), _from_flat(m_, small_shapes), _from_flat(v_, small_shapes)):
        grad[n], delta[n], new_m[n], new_v[n] = g_small[n], a, b, c_
    return (loss, grad_x, *[grad[n] for n in WEIGHTS], *[delta[n] for n in WEIGHTS],
            *[new_m[n] for n in WEIGHTS], *[new_v[n] for n in WEIGHTS])
```

```python
import functools
import math

import numpy as np
import jax
import jax.numpy as jnp
from jax import lax
from jax.experimental import pallas as pl
from jax.experimental.pallas import tpu as pltpu
from jax.experimental.pallas import tpu_sc as plsc

F32 = jnp.float32
BF16 = jnp.bfloat16
HI = lax.Precision.HIGHEST
MESH = pl.DeviceIdType.MESH

D_MODEL = 1024
SEQ = 2048
DEPTH = 4
D_FF = 2816
ATT_DILATIONS = (1, 4, 16)
BAND = 128
ATT_OUT = 512
QKV = 1536
D_INNER = 2048
N_SSM_HEADS = 32
N_SSM_GROUPS = 4
D_STATE = 128
XBC = 3072
CHUNK = 128
N_IN = 11808
EPS = 1e-6
ROPE_THETA = 10000.0
NP = 12288
Q0, K0, V0, Z0, G0, X0, DT0 = 0, 1536, 3072, 4608, 6656, 8704, 11776
DTW = 128
LR, B1, B2, ADAM_EPS, WD, STEP = 0.001, 0.9, 0.999, 1e-08, 0.01, 10

LANES = 128
VMEM_LIMIT = 48 * 1024 * 1024
NEG = -1e30


def _cp(sem=None, **kw):
    return pltpu.CompilerParams(dimension_semantics=sem, vmem_limit_bytes=VMEM_LIMIT, **kw)


def _dg(a, b, ca, cb):
    return lax.dot_general(a.astype(BF16), b.astype(BF16), (((ca,), (cb,)), ((), ())), preferred_element_type=F32)


@jax.custom_vjp
def dot_nn(a, b):
    return _dg(a, b, 1, 0)


def _dot_nn_fwd(a, b):
    return _dg(a, b, 1, 0), (a, b)


def _dot_nn_bwd(r, g):
    a, b = r
    return _dg(g, b, 1, 1).astype(a.dtype), _dg(a, g, 0, 0).astype(b.dtype)


dot_nn.defvjp(_dot_nn_fwd, _dot_nn_bwd)


@jax.custom_vjp
def dot_nt(a, b):
    return _dg(a, b, 1, 1)


def _dot_nt_fwd(a, b):
    return _dg(a, b, 1, 1), (a, b)


def _dot_nt_bwd(r, g):
    a, b = r
    return _dg(g, b, 1, 0).astype(a.dtype), _dg(g, a, 0, 0).astype(b.dtype)


dot_nt.defvjp(_dot_nt_fwd, _dot_nt_bwd)


@jax.custom_vjp
def dot_tn(a, b):
    return _dg(a, b, 0, 0)


def _dot_tn_fwd(a, b):
    return _dg(a, b, 0, 0), (a, b)


def _dot_tn_bwd(r, g):
    a, b = r
    return _dg(b, g, 1, 1).astype(a.dtype), _dg(a, g, 1, 0).astype(b.dtype)


dot_tn.defvjp(_dot_tn_fwd, _dot_tn_bwd)


def _dot2_raw(a, e, ce):
    hi = a.astype(BF16)
    lo = (a - hi.astype(F32)).astype(BF16)
    return _dg(hi, e, 1, ce) + _dg(lo, e, 1, ce)


@jax.custom_vjp
def dot2(a, e):
    return _dot2_raw(a, e, 0)


def _dot2_fwd(a, e):
    return _dot2_raw(a, e, 0), e


def _dot2_bwd(e, g):
    return _dot2_raw(g, e, 1), jnp.zeros_like(e)


dot2.defvjp(_dot2_fwd, _dot2_bwd)


def _tri2_raw(l, x, cl):
    hi = x.astype(BF16)
    lo = (x - hi.astype(F32)).astype(BF16)
    return _dg(l, hi, cl, 0) + _dg(l, lo, cl, 0)


@jax.custom_vjp
def tri_matmul(l, x):
    return _tri2_raw(l, x, 1)


def _tri_fwd(l, x):
    return _tri2_raw(l, x, 1), l


def _tri_bwd(l, g):
    return jnp.zeros_like(l), _tri2_raw(l, g, 0)


tri_matmul.defvjp(_tri_fwd, _tri_bwd)


def _dup64_raw(w):
    return w + pltpu.roll(w, 64, 1)


@jax.custom_vjp
def dup64(w):
    return _dup64_raw(w)


def _dup64_fwd(w):
    return _dup64_raw(w), None


def _dup64_bwd(_, g):
    lane = lax.broadcasted_iota(jnp.int32, g.shape, 1)
    return (jnp.where(lane < 64, _dup64_raw(g), 0.0),)


dup64.defvjp(_dup64_fwd, _dup64_bwd)


def _rope_rot_raw(y, sign):
    lane = lax.broadcasted_iota(jnp.int32, y.shape, 1)
    first_half = (lane & 32) == 0
    return sign * jnp.where(first_half, -pltpu.roll(y, LANES - 32, 1), pltpu.roll(y, 32, 1))


@jax.custom_vjp
def rope_rot(y):
    return _rope_rot_raw(y, 1.0)


def _rope_rot_fwd(y):
    return _rope_rot_raw(y, 1.0), None


def _rope_rot_bwd(_, g):
    return (_rope_rot_raw(g, -1.0),)


rope_rot.defvjp(_rope_rot_fwd, _rope_rot_bwd)


def _shift_rows_raw(x, s):
    n = x.shape[0]
    r = pltpu.roll(x, s % n, 0)
    rows = lax.broadcasted_iota(jnp.int32, x.shape, 0)
    keep = rows >= s if s > 0 else rows < n + s
    return jnp.where(keep, r, 0.0)


@functools.partial(jax.custom_vjp, nondiff_argnums=(1,))
def shift_rows(x, s):
    return _shift_rows_raw(x, s)


def _shift_fwd(x, s):
    return _shift_rows_raw(x, s), None


def _shift_bwd(s, _, g):
    return (_shift_rows_raw(g, -s),)


shift_rows.defvjp(_shift_fwd, _shift_bwd)


def _sigmoid(x):
    return 1.0 / (1.0 + jnp.exp(-x))


def _silu(x):
    return x * _sigmoid(x)


def _softplus(x):
    return jnp.maximum(x, 0.0) + jnp.log(1.0 + jnp.exp(-jnp.abs(x)))


def _head_mean_mat():
    i = np.arange(LANES)
    return jnp.asarray((i[:, None] // 64 == i[None, :] // 64).astype(np.float32) / 64.0)


def _head_expand_mat():
    e = np.zeros((LANES, D_INNER), np.float32)
    for l in range(D_INNER):
        e[l // 64, l] = 1.0
    return jnp.asarray(e)


def _ltri_mat():
    i = np.arange(CHUNK)
    return jnp.asarray((i[:, None] >= i[None, :]).astype(np.float32))


def _rope_tables():
    pos = jnp.arange(SEQ, dtype=F32)
    inv_freq = 1.0 / (ROPE_THETA ** (jnp.arange(0, 64, 2, dtype=F32) / 64))
    ang = pos[:, None] * inv_freq[None, :]
    return jnp.tile(jnp.cos(ang), (1, 4)), jnp.tile(jnp.sin(ang), (1, 4))


def _pick(n, cap):
    best = None
    for t in range(LANES, min(n, cap) + 1, LANES):
        if n % t == 0:
            best = t
    return best if best is not None else n


def _mm(name, a, b, mode, out_dtype=F32, alpha=None, res=None):
    if mode == "nn":
        (m, k), n = a.shape, b.shape[1]
    elif mode == "nt":
        (m, k), n = a.shape, b.shape[0]
    else:
        (k, m), n = a.shape, b.shape[1]
    tm, tn, tk = _pick(m, 1408), _pick(n, 1408), _pick(k, 1024)
    nk = k // tk
    ca, cb = {"nn": (1, 0), "nt": (1, 1), "tn": (0, 0)}[mode]
    a_spec = pl.BlockSpec((tk, tm), lambda i, j, kk: (kk, i)) if mode == "tn" else pl.BlockSpec((tm, tk), lambda i, j, kk: (i, kk))
    b_spec = pl.BlockSpec((tn, tk), lambda i, j, kk: (j, kk)) if mode == "nt" else pl.BlockSpec((tk, tn), lambda i, j, kk: (kk, j))
    o_spec = pl.BlockSpec((tm, tn), lambda i, j, kk: (i, j))
    has_res = res is not None

    def finish(acc, res_ref, o_ref):
        if alpha is not None:
            acc = acc * alpha
        if has_res:
            acc = acc + res_ref[...].astype(F32)
        o_ref[...] = acc.astype(o_ref.dtype)

    def body(*refs):
        a_ref, b_ref = refs[0], refs[1]
        res_ref = refs[2] if has_res else None
        o_ref = refs[3] if has_res else refs[2]
        part = _dg(a_ref[...], b_ref[...], ca, cb)
        if nk == 1:
            finish(part, res_ref, o_ref)
            return
        acc_ref = refs[-1]
        kk = pl.program_id(2)

        @pl.when(kk == 0)
        def _():
            acc_ref[...] = part

        @pl.when(kk > 0)
        def _():
            acc_ref[...] += part

        @pl.when(kk == nk - 1)
        def _():
            finish(acc_ref[...], res_ref, o_ref)

    ins = [a, b] + ([res] if has_res else [])
    in_specs = [a_spec, b_spec] + ([o_spec] if has_res else [])
    return pl.pallas_call(
        body, name=name, grid=(m // tm, n // tn, nk), in_specs=in_specs, out_specs=o_spec,
        out_shape=jax.ShapeDtypeStruct((m, n), out_dtype),
        scratch_shapes=[pltpu.VMEM((tm, tn), F32)] if nk > 1 else [],
        compiler_params=_cp(("parallel", "parallel", "arbitrary")),
    )(*ins)


def _mmx(name, grid, a, b, out, contract, *, alpha=None, res=None, into=None):
    nk = grid[-1]
    has_res, has_into = res is not None, into is not None
    n_in = 2 + has_res + has_into

    def finish(acc, res_ref, o_ref):
        if alpha is not None:
            acc = acc * alpha
        if has_res:
            acc = acc + res_ref[...].astype(F32)
        o_ref[...] = acc.astype(o_ref.dtype)

    def body(*refs):
        res_ref = refs[2] if has_res else None
        o_ref = refs[n_in]
        part = _dg(refs[0][...], refs[1][...], *contract)
        if nk == 1:
            finish(part, res_ref, o_ref)
            return
        acc_ref = refs[-1]
        kk = pl.program_id(len(grid) - 1)

        @pl.when(kk == 0)
        def _():
            acc_ref[...] = part

        @pl.when(kk > 0)
        def _():
            acc_ref[...] += part

        @pl.when(kk == nk - 1)
        def _():
            finish(acc_ref[...], res_ref, o_ref)

    operands = [a, b] + ([res] if has_res else [])
    in_specs = [pl.BlockSpec(blk, im) for _, blk, im in operands] + ([ANY] if has_into else [])
    acc_shape = tuple(d for d in out[2] if d is not None)
    return pl.pallas_call(
        body, name=name, grid=grid, in_specs=in_specs, out_specs=pl.BlockSpec(out[2], out[3]),
        out_shape=jax.ShapeDtypeStruct(out[0], out[1]),
        scratch_shapes=[pltpu.VMEM(acc_shape, F32)] if nk > 1 else [],
        input_output_aliases={n_in - 1: 0} if has_into else {},
        compiler_params=_cp(("parallel",) * (len(grid) - 1) + ("arbitrary",)),
    )(*[o[0] for o in operands], *([into] if has_into else []))


def _ew(name, fn, grid, ins, outs, scratch=()):
    n_in, n_out = len(ins), len(outs)

    def body(*refs):
        vals = [r[...] for r in refs[:n_in]]
        res = fn(*vals, *refs[n_in + n_out:])
        for r, v in zip(refs[n_in:n_in + n_out], res):
            r[...] = v.astype(r.dtype)

    res = pl.pallas_call(
        body, name=name, grid=grid,
        in_specs=[pl.BlockSpec(b, m) for _, b, m in ins],
        out_specs=[pl.BlockSpec(b, m) for _, _, b, m in outs],
        out_shape=[jax.ShapeDtypeStruct(s, d) for s, d, _, _ in outs],
        scratch_shapes=list(scratch),
        compiler_params=_cp(("arbitrary",) * len(grid)),
    )(*[a for a, _, _ in ins])
    return res


def _ew_bwd(name, fn, grid, ins, cts, wrt, adds=(), ct_fn=None):
    n_in, n_ct, n_add = len(ins), len(cts), len(adds)
    idxs = [w["idx"] for w in wrt]
    intos = [(k, w["into"]) for k, w in enumerate(wrt) if w.get("into") is not None]

    def body(*refs):
        prim = [r[...] for r in refs[:n_in]]
        ct = [r[...].astype(F32) for r in refs[n_in:n_in + n_ct]]
        addv = [r[...] for r in refs[n_in + n_ct:n_in + n_ct + n_add]]
        orefs = refs[n_in + n_ct + n_add + len(intos):]

        def f(*sel):
            full = list(prim)
            for i, s in zip(idxs, sel):
                full[i] = s
            return fn(*full)

        _, vjp = jax.vjp(f, *[prim[i].astype(F32) for i in idxs])
        grads = vjp(tuple(ct) if ct_fn is None else ct_fn(*ct))
        for w, g, r in zip(wrt, grads, orefs):
            if w["kind"] == "tile":
                if w.get("add") is not None:
                    g = g + addv[w["add"]].astype(F32)
                r[...] = g.astype(r.dtype)
            else:
                first = w["first"]()

                @pl.when(first)
                def _(r=r, g=g):
                    r[...] = g.astype(r.dtype)

                @pl.when(jnp.logical_not(first))
                def _(r=r, g=g):
                    r[...] += g.astype(r.dtype)

    allin = list(ins) + list(cts) + list(adds)
    return pl.pallas_call(
        body, name=name, grid=grid,
        in_specs=[pl.BlockSpec(b, m) for _, b, m in allin] + [ANY] * len(intos),
        out_specs=[pl.BlockSpec(w["block"], w["imap"]) for w in wrt],
        out_shape=[jax.ShapeDtypeStruct(w["shape"], w["dtype"]) for w in wrt],
        input_output_aliases={len(allin) + q: k for q, (k, _) in enumerate(intos)},
        compiler_params=_cp(("arbitrary",) * len(grid)),
    )(*[a for a, _, _ in allin], *[a for _, a in intos])


def _rmsnorm_fn(x, w):
    return (x * lax.rsqrt(jnp.mean(x * x, axis=-1, keepdims=True) + EPS) * w,)


def _swiglu_fn(g, u):
    return (_silu(g) * u,)


def _qkprep_fn(t, w64, cos, sin, hmean):
    w = jnp.sum(dup64(jnp.broadcast_to(w64, (8, LANES))), axis=0, keepdims=True) * 0.125
    y = t * lax.rsqrt(dot2(t * t, hmean) + EPS) * w
    return (y * cos + rope_rot(y) * sin,)


def _att_fn(q, kp, kc, vp, vc, first):
    iq = lax.broadcasted_iota(jnp.int32, (BAND, 2 * BAND), 0)
    ik = lax.broadcasted_iota(jnp.int32, (BAND, 2 * BAND), 1)
    rel = BAND + iq - ik
    ok = (rel >= 0) & (rel <= BAND) & ((ik >= BAND) | jnp.logical_not(first))
    lane = lax.broadcasted_iota(jnp.int32, (1, LANES), 1)
    kcat = jnp.concatenate([kp, kc], axis=0)
    vcat = jnp.concatenate([vp, vc], axis=0)
    o_pair = jnp.zeros((BAND, LANES), F32)
    l_pair = jnp.zeros((BAND, LANES), F32)
    for hh in range(2):
        lm = (lane // 64 == hh).astype(F32)
        s = dot_nt(q * lm, kcat) * 0.125
        s = jnp.where(ok, s, NEG)
        mx = jnp.max(s, axis=-1, keepdims=True)
        e = jnp.exp(s - mx)
        den = jnp.sum(e, axis=-1, keepdims=True)
        o_pair = o_pair + dot_nn(e / den, vcat) * lm
        l_pair = l_pair + (mx + jnp.log(den)) * lm
    return o_pair, l_pair


def _attmix_fn(o0, o1, o2, l0, l1, l2):
    m = jnp.maximum(jnp.maximum(l0, l1), l2)
    e0, e1, e2 = jnp.exp(l0 - m), jnp.exp(l1 - m), jnp.exp(l2 - m)
    return ((e0 * o0 + e1 * o1 + e2 * o2) / (e0 + e1 + e2),)


def _conv_fn(x, w0, w1, w2, w3, b):
    pre = x * w3 + shift_rows(x, 1) * w2 + shift_rows(x, 2) * w1 + shift_rows(x, 3) * w0 + b
    return (_silu(pre),)


def _ssdpre_fn(dtraw, bias, alog, ex):
    dt = _softplus(dtraw + bias)
    da = dt * (-jnp.exp(alog))
    return dot2(dt, ex), dot2(da, ex)


def _ssd_step(st, x, dtb, dab, bm, cm, ltri):
    cum = tri_matmul(ltri, dab)
    cum_t = cum.T
    xdt = x * dtb
    cb = dot_nt(cm, bm)
    ri = lax.broadcasted_iota(jnp.int32, (CHUNK, CHUNK), 0)
    ci = lax.broadcasted_iota(jnp.int32, (CHUNK, CHUNK), 1)
    causal = ri >= ci
    lane = lax.broadcasted_iota(jnp.int32, (1, LANES), 1)
    rowi = lax.broadcasted_iota(jnp.int32, (LANES, 1), 0)
    ys = []
    for p in range(4):
        sl = slice(p * LANES, (p + 1) * LANES)
        cum_p, cum_tp, xdt_p = cum[:, sl], cum_t[sl, :], xdt[:, sl]
        acc = jnp.zeros((CHUNK, LANES), F32)
        for hh in range(2):
            col = jnp.sum(cum_p * (lane == 64 * hh).astype(F32), axis=1, keepdims=True)
            row = jnp.sum(cum_tp * (rowi == 64 * hh).astype(F32), axis=0, keepdims=True)
            dec = jnp.exp(jnp.where(causal, col - row, NEG))
            acc = acc + dot_nn(cb * dec, xdt_p * (lane // 64 == hh).astype(F32))
        ys.append(acc)
    y_diag = jnp.concatenate(ys, axis=1)
    y_off = dot_nn(cm, st) * jnp.exp(cum)
    last_row = (lax.broadcasted_iota(jnp.int32, (CHUNK, 1), 0) == CHUNK - 1).astype(F32)
    last = jnp.sum(cum * last_row, axis=0, keepdims=True)
    new_st = st * jnp.exp(last) + dot_tn(bm, xdt * jnp.exp(last - cum))
    return new_st, y_diag + y_off


def _ssdpost_fn(y, xs, z, dskip, ex, nw):
    db = jnp.sum(dot2(jnp.broadcast_to(dskip, (8, LANES)), ex), axis=0, keepdims=True) * 0.125
    y2 = (y + db * xs) * _silu(z)
    return (y2 * lax.rsqrt(jnp.mean(y2 * y2, axis=-1, keepdims=True) + EPS) * nw,)


def _merge_fn(ya, ys, ga, gs, ba, bs):
    return (_sigmoid(ga + ba) * ya + _sigmoid(gs + bs) * ys,)


TM = 512


def _full(shape):
    nd = len(shape)
    return (shape, lambda *_: (0,) * nd)


def _rmsnorm(name, x, w):
    t = x.shape[0]
    return _ew(name, _rmsnorm_fn, (t // TM,),
               [(x, (TM, D_MODEL), lambda i: (i, 0)), (w, (1, D_MODEL), lambda i: (0, 0))],
               [((t, D_MODEL), BF16, (TM, D_MODEL), lambda i: (i, 0))])[0]


def _rmsnorm_bwd(name, x, w, dh, dres):
    t = x.shape[0]
    row = ((TM, D_MODEL), lambda i: (i, 0))
    return _ew_bwd(name, _rmsnorm_fn, (t // TM,),
                   [(x, *row), (w, (1, D_MODEL), lambda i: (0, 0))], [(dh, *row)],
                   [dict(idx=0, kind="tile", shape=(t, D_MODEL), dtype=F32, block=row[0], imap=row[1], add=0),
                    dict(idx=1, kind="acc", shape=(1, D_MODEL), dtype=F32, block=(1, D_MODEL), imap=lambda i: (0, 0),
                         first=lambda: pl.program_id(0) == 0)],
                   adds=[(dres, *row)])


def _qk_operands(proj, qkw, cos, sin, consts, tm):
    nrow = SEQ // tm
    c = ((LANES, LANES), lambda j, i: (0, 0))
    return [(proj, (tm, LANES), lambda j, i: (i, j)),
            (qkw, (None, 1, LANES), lambda j, i: (j // 12, 0, 0)),
            (cos, (tm, LANES), lambda j, i: (i % nrow, 0)),
            (sin, (tm, LANES), lambda j, i: (i % nrow, 0)),
            (consts["hmean"], *c)]


def _qkprep(name, proj, qkw, cos, sin, consts):
    t = proj.shape[0]
    return _ew(name, _qkprep_fn, (2 * QKV // LANES, t // TM), _qk_operands(proj, qkw, cos, sin, consts, TM),
               [((t, 2 * QKV), F32, (TM, LANES), lambda j, i: (i, j))])[0]


def _qkprep_bwd(name, proj, qkw, cos, sin, consts, dq, dk, dproj):
    t = proj.shape[0]
    nq = QKV // LANES

    def pick(cq, ck):
        return (jnp.where(pl.program_id(0) < nq, cq, ck),)

    return _ew_bwd(name, _qkprep_fn, (2 * nq, t // TM), _qk_operands(proj, qkw, cos, sin, consts, TM),
                   [(d, (TM, LANES), lambda j, i: (i, j % nq)) for d in (dq, dk)],
                   [dict(idx=0, kind="tile", shape=dproj.shape, dtype=dproj.dtype, block=(TM, LANES),
                         imap=lambda j, i: (i, j), into=dproj),
                    dict(idx=1, kind="acc", shape=(2, 1, LANES), dtype=F32, block=(None, 1, LANES),
                         imap=lambda j, i: (j // 12, 0, 0),
                         first=lambda: (pl.program_id(0) % 12 == 0) & (pl.program_id(1) == 0))],
                   ct_fn=pick)


def _att_specs(dil, g):
    nb = SEQ // dil // BAND
    pt = 4 if dil == 1 else 1
    w = pt * LANES
    blk = (None, BAND * dil, w)
    kq, kk, kv = g * ATT_OUT // w, (QKV + g * ATT_OUT) // w, (V0 + g * ATT_OUT) // w

    def cur(n):
        return jnp.minimum(n, nb - 1)

    def prev(n):
        return jnp.maximum(jnp.minimum(n, nb - 1) - 1, 0)

    return nb, pt, blk, [
        pl.BlockSpec(blk, lambda b, p, n: (b, cur(n), kq + p)),
        pl.BlockSpec(blk, lambda b, p, n: (b, prev(n), kk + p)),
        pl.BlockSpec(blk, lambda b, p, n: (b, cur(n), kk + p)),
        pl.BlockSpec(blk, lambda b, p, n: (b, prev(n), kv + p)),
        pl.BlockSpec(blk, lambda b, p, n: (b, cur(n), kv + p)),
    ]


def _att_rows(r, dil):
    return pl.ds(r, BAND, stride=dil) if dil > 1 else pl.ds(0, BAND)


def _att_fwd(name, qk, proj, g):
    bl = qk.shape[0] // SEQ
    dil = ATT_DILATIONS[g]
    nb, pt, blk, specs = _att_specs(dil, g)
    qk3 = qk.reshape(bl, SEQ, 2 * QKV)
    proj3 = proj.reshape(bl, SEQ, NP)
    o_spec = pl.BlockSpec(blk, lambda b, p, n: (b, n, p))

    def body(q, kp, kc, vp, vc, o_ref, l_ref):
        first = pl.program_id(2) == 0

        def residue(r, carry):
            sl = _att_rows(r, dil)
            for p in range(pt):
                ln = pl.ds(p * LANES, LANES)
                o, l = _att_fn(q[sl, ln], kp[sl, ln], kc[sl, ln], vp[sl, ln], vc[sl, ln], first)
                o_ref[sl, ln] = o
                l_ref[sl, ln] = l
            return carry

        lax.fori_loop(0, dil, residue, 0)

    o, l = pl.pallas_call(
        body, name=name, grid=(bl, ATT_OUT // (pt * LANES), nb), in_specs=specs, out_specs=[o_spec, o_spec],
        out_shape=[jax.ShapeDtypeStruct((bl, SEQ, ATT_OUT), F32)] * 2,
        compiler_params=_cp(("arbitrary",) * 3),
    )(qk3, qk3, qk3, proj3, proj3)
    return o.reshape(bl * SEQ, ATT_OUT), l.reshape(bl * SEQ, ATT_OUT)


def _att_bwd(name, qk, proj, g, do, dl, dq_buf, dk_buf, dv_buf):
    bl = qk.shape[0] // SEQ
    dil = ATT_DILATIONS[g]
    nb, pt, blk, specs = _att_specs(dil, g)
    w = pt * LANES
    qk3 = qk.reshape(bl, SEQ, 2 * QKV)
    proj3 = proj.reshape(bl, SEQ, NP)
    ct_spec = pl.BlockSpec(blk, lambda b, p, n: (b, jnp.minimum(n, nb - 1), p))
    do3 = do.reshape(bl, SEQ, ATT_OUT)
    dl3 = dl.reshape(bl, SEQ, ATT_OUT)
    kg = g * ATT_OUT // w

    def body(q, kp, kc, vp, vc, do_ref, dl_ref, _a, _b, _c, d_ref, dk_ref, dv_ref, ck, cv):
        n = pl.program_id(2)

        def residue(r, carry):
            sl = _att_rows(r, dil)
            for p in range(pt):
                ln = pl.ds(p * LANES, LANES)

                @pl.when(n < nb)
                def _(ln=ln):
                    first = n == 0
                    prim = [ref[sl, ln] for ref in (q, kp, kc, vp, vc)]
                    _, vjp = jax.vjp(lambda *a: _att_fn(*a, first), *prim)
                    dq, dkp, dkc, dvp, dvc = vjp((do_ref[sl, ln], dl_ref[sl, ln]))
                    d_ref[sl, ln] = dq

                    @pl.when(n > 0)
                    def _():
                        dk_ref[sl, ln] = ck[sl, ln] + dkp
                        dv_ref[sl, ln] = cv[sl, ln] + dvp

                    ck[sl, ln] = dkc
                    cv[sl, ln] = dvc

                @pl.when(n == nb)
                def _(ln=ln):
                    dk_ref[sl, ln] = ck[sl, ln]
                    dv_ref[sl, ln] = cv[sl, ln]

            return carry

        lax.fori_loop(0, dil, residue, 0)

    bufs = [a.reshape(bl, SEQ, QKV) for a in (dq_buf, dk_buf, dv_buf)]
    o_specs = [
        pl.BlockSpec(blk, lambda b, p, n: (b, jnp.minimum(n, nb - 1), kg + p)),
        pl.BlockSpec(blk, lambda b, p, n: (b, jnp.maximum(n - 1, 0), kg + p)),
        pl.BlockSpec(blk, lambda b, p, n: (b, jnp.maximum(n - 1, 0), kg + p)),
    ]
    dq, dk, dv = pl.pallas_call(
        body, name=name, grid=(bl, ATT_OUT // w, nb + 1), in_specs=specs + [ct_spec, ct_spec, ANY, ANY, ANY],
        out_specs=o_specs, out_shape=[jax.ShapeDtypeStruct(a.shape, a.dtype) for a in bufs],
        input_output_aliases={7: 0, 8: 1, 9: 2},
        scratch_shapes=[pltpu.VMEM((BAND * dil, w), F32), pltpu.VMEM((BAND * dil, w), F32)],
        compiler_params=_cp(("arbitrary",) * 3),
    )(qk3, qk3, qk3, proj3, proj3, do3, dl3, *bufs)
    return dq.reshape(dq_buf.shape), dk.reshape(dk_buf.shape), dv.reshape(dv_buf.shape)


def _attmix(name, os_, ls_):
    t = os_[0].shape[0]
    blk = ((TM, ATT_OUT), lambda i: (i, 0))
    return _ew(name, _attmix_fn, (t // TM,), [(a, *blk) for a in (*os_, *ls_)], [((t, ATT_OUT), BF16, *blk)])[0]


def _attmix_bwd(name, os_, ls_, datt):
    t = os_[0].shape[0]
    blk = ((TM, ATT_OUT), lambda i: (i, 0))
    return _ew_bwd(name, _attmix_fn, (t // TM,), [(a, *blk) for a in (*os_, *ls_)], [(datt, *blk)],
                   [dict(idx=k, kind="tile", shape=(t, ATT_OUT), dtype=F32, block=blk[0], imap=blk[1]) for k in range(6)])


CONV_TC = 256


def _conv_operands(proj3, conv_w, conv_b):
    c0 = X0 // CONV_TC
    ins = [(proj3, (None, SEQ, CONV_TC), lambda j, b: (b, 0, c0 + j))]
    for k in range(4):
        ins.append((conv_w, (None, 1, CONV_TC), lambda j, b, k=k: (k, 0, j)))
    ins.append((conv_b, (1, CONV_TC), lambda j, b: (0, j)))
    return ins


def _conv(name, proj3, conv_w, conv_b):
    bl = proj3.shape[0]
    return _ew(name, _conv_fn, (XBC // CONV_TC, bl), _conv_operands(proj3, conv_w, conv_b),
               [((bl, SEQ, XBC), F32, (None, SEQ, CONV_TC), lambda j, b: (b, 0, j))])[0]


def _conv_bwd(name, proj3, conv_w, conv_b, dxs3, db3, dc3, dproj3):
    bl = proj3.shape[0]
    nx = D_INNER // CONV_TC
    nb_ = N_SSM_GROUPS * D_STATE // CONV_TC
    blk = (None, SEQ, CONV_TC)
    cts = [(dxs3, blk, lambda j, b: (b, 0, jnp.minimum(j, nx - 1))),
           (db3, blk, lambda j, b: (b, 0, jnp.clip(j - nx, 0, nb_ - 1))),
           (dc3, blk, lambda j, b: (b, 0, jnp.clip(j - nx - nb_, 0, nb_ - 1)))]

    def pick(cx, cb, cc):
        j = pl.program_id(0)
        return (jnp.where(j < nx, cx, jnp.where(j < nx + nb_, cb, cc)),)

    first = lambda: pl.program_id(1) == 0
    wrt = [dict(idx=0, kind="tile", shape=dproj3.shape, dtype=dproj3.dtype, block=blk,
                imap=lambda j, b: (b, 0, X0 // CONV_TC + j), into=dproj3)]
    for k in range(4):
        wrt.append(dict(idx=1 + k, kind="acc", shape=(1, XBC), dtype=F32, block=(1, CONV_TC),
                        imap=lambda j, b: (0, j), first=first))
    wrt.append(dict(idx=5, kind="acc", shape=(1, XBC), dtype=F32, block=(1, CONV_TC), imap=lambda j, b: (0, j), first=first))
    return _ew_bwd(name, _conv_fn, (XBC // CONV_TC, bl), _conv_operands(proj3, conv_w, conv_b), cts, wrt, ct_fn=pick)


SSD_TM = 256


def _ssdpre_operands(proj, dt_bias, a_log, ex):
    return [(proj, (SSD_TM, DTW), lambda i: (i, DT0 // DTW)), (dt_bias, *_full((1, DTW))), (a_log, *_full((1, DTW))),
            (ex, *_full((LANES, D_INNER)))]


def _ssdpre(name, proj, dt_bias, a_log, ex):
    t = proj.shape[0]
    blk = ((SSD_TM, D_INNER), lambda i: (i, 0))
    return _ew(name, _ssdpre_fn, (t // SSD_TM,), _ssdpre_operands(proj, dt_bias, a_log, ex),
               [((t, D_INNER), F32, *blk), ((t, D_INNER), F32, *blk)])


def _ssdpre_bwd(name, proj, dt_bias, a_log, ex, ddtb, ddab):
    t = proj.shape[0]
    blk = ((SSD_TM, D_INNER), lambda i: (i, 0))
    first = lambda: pl.program_id(0) == 0
    return _ew_bwd(name, _ssdpre_fn, (t // SSD_TM,), _ssdpre_operands(proj, dt_bias, a_log, ex),
                   [(ddtb, *blk), (ddab, *blk)],
                   [dict(idx=0, kind="tile", shape=(t, DTW), dtype=BF16, block=(SSD_TM, DTW), imap=lambda i: (i, 0)),
                    dict(idx=1, kind="acc", shape=(1, DTW), dtype=F32, block=(1, DTW), imap=lambda i: (0, 0), first=first),
                    dict(idx=2, kind="acc", shape=(1, DTW), dtype=F32, block=(1, DTW), imap=lambda i: (0, 0), first=first)])


def _ssd_in_specs(rev):
    nc = SEQ // CHUNK

    def c_(c):
        return nc - 1 - c if rev else c

    wide = (None, CHUNK, 4 * LANES)
    nar = (None, CHUNK, D_STATE)
    xb = D_INNER // D_STATE
    return [
        pl.BlockSpec(wide, lambda b, g, c: (b, c_(c), g)),
        pl.BlockSpec(wide, lambda b, g, c: (b, c_(c), g)),
        pl.BlockSpec(wide, lambda b, g, c: (b, c_(c), g)),
        pl.BlockSpec(nar, lambda b, g, c: (b, c_(c), xb + g)),
        pl.BlockSpec(nar, lambda b, g, c: (b, c_(c), xb + N_SSM_GROUPS + g)),
        pl.BlockSpec((CHUNK, CHUNK), lambda b, g, c: (0, 0)),
    ], c_


def _ssd_fwd(name, xc3, dtb3, dab3, ltri):
    bl = xc3.shape[0]
    nc = SEQ // CHUNK
    specs, _ = _ssd_in_specs(False)

    def body(x, dtb, dab, bm, cm, lt, y_ref, st_ref, st):
        @pl.when(pl.program_id(2) == 0)
        def _():
            st[...] = jnp.zeros_like(st)

        s0 = st[...]
        st_ref[...] = s0
        new_st, y = _ssd_step(s0, x[...], dtb[...], dab[...], bm[...], cm[...], lt[...])
        y_ref[...] = y
        st[...] = new_st

    return pl.pallas_call(
        body, name=name, grid=(bl, N_SSM_GROUPS, nc), in_specs=specs,
        out_specs=[pl.BlockSpec((None, CHUNK, 4 * LANES), lambda b, g, c: (b, c, g)),
                   pl.BlockSpec((None, None, None, D_STATE, 4 * LANES), lambda b, g, c: (b, g, c, 0, 0))],
        out_shape=[jax.ShapeDtypeStruct((bl, SEQ, D_INNER), F32),
                   jax.ShapeDtypeStruct((bl, N_SSM_GROUPS, nc, D_STATE, 4 * LANES), F32)],
        scratch_shapes=[pltpu.VMEM((D_STATE, 4 * LANES), F32)],
        compiler_params=_cp(("arbitrary",) * 3),
    )(xc3, dtb3, dab3, xc3, xc3, ltri)


def _ssd_bwd(name, xc3, dtb3, dab3, ltri, states, dy3, dxs_part3):
    bl = xc3.shape[0]
    nc = SEQ // CHUNK
    specs, c_ = _ssd_in_specs(True)
    wide = pl.BlockSpec((None, CHUNK, 4 * LANES), lambda b, g, c: (b, c_(c), g))
    nar = pl.BlockSpec((None, CHUNK, D_STATE), lambda b, g, c: (b, c_(c), g))
    st_spec = pl.BlockSpec((None, None, None, D_STATE, 4 * LANES), lambda b, g, c: (b, g, c_(c), 0, 0))

    def body(x, dtb, dab, bm, cm, lt, st_ref, dy, dxp, dx_ref, ddtb_ref, ddab_ref, dbm_ref, dcm_ref, dst):
        @pl.when(pl.program_id(2) == 0)
        def _():
            dst[...] = jnp.zeros_like(dst)

        ltv = lt[...]
        _, vjp = jax.vjp(lambda *a: _ssd_step(*a, ltv), st_ref[...], x[...], dtb[...], dab[...], bm[...], cm[...])
        d_st, d_x, d_dtb, d_dab, d_bm, d_cm = vjp((dst[...], dy[...]))
        dst[...] = d_st
        dx_ref[...] = d_x + dxp[...]
        ddtb_ref[...] = d_dtb
        ddab_ref[...] = d_dab
        dbm_ref[...] = d_bm
        dcm_ref[...] = d_cm

    big = jax.ShapeDtypeStruct((bl, SEQ, D_INNER), F32)
    small = jax.ShapeDtypeStruct((bl, SEQ, N_SSM_GROUPS * D_STATE), F32)
    return pl.pallas_call(
        body, name=name, grid=(bl, N_SSM_GROUPS, nc), in_specs=specs + [st_spec, wide, wide],
        out_specs=[wide, wide, wide, nar, nar], out_shape=[big, big, big, small, small],
        scratch_shapes=[pltpu.VMEM((D_STATE, 4 * LANES), F32)],
        compiler_params=_cp(("arbitrary",) * 3),
    )(xc3, dtb3, dab3, xc3, xc3, ltri, states, dy3, dxs_part3)


def _ssdpost_operands(y, xc, proj, d_skip, ex, nw):
    w = 4 * LANES
    return [(y, (SSD_TM, w), lambda j, i: (i, j)), (xc, (SSD_TM, w), lambda j, i: (i, j)),
            (proj, (SSD_TM, w), lambda j, i: (i, Z0 // w + j)), (d_skip, (1, DTW), lambda j, i: (0, 0)),
            (ex, (LANES, w), lambda j, i: (0, j)), (nw, (1, w), lambda j, i: (0, j))]


def _ssdpost(name, y, xc, proj, d_skip, ex, nw):
    t = y.shape[0]
    w = 4 * LANES
    return _ew(name, _ssdpost_fn, (D_INNER // w, t // SSD_TM), _ssdpost_operands(y, xc, proj, d_skip, ex, nw),
               [((t, D_INNER), BF16, (SSD_TM, w), lambda j, i: (i, j))])[0]


def _ssdpost_bwd(name, y, xc, proj, d_skip, ex, nw, dysn, dproj):
    t = y.shape[0]
    w = 4 * LANES
    blk = ((SSD_TM, w), lambda j, i: (i, j))
    return _ew_bwd(name, _ssdpost_fn, (D_INNER // w, t // SSD_TM), _ssdpost_operands(y, xc, proj, d_skip, ex, nw),
                   [(dysn, *blk)],
                   [dict(idx=0, kind="tile", shape=(t, D_INNER), dtype=F32, block=blk[0], imap=blk[1]),
                    dict(idx=1, kind="tile", shape=(t, D_INNER), dtype=F32, block=blk[0], imap=blk[1]),
                    dict(idx=2, kind="tile", shape=dproj.shape, dtype=dproj.dtype, block=blk[0],
                         imap=lambda j, i: (i, Z0 // w + j), into=dproj),
                    dict(idx=3, kind="acc", shape=(1, DTW), dtype=F32, block=(1, DTW), imap=lambda j, i: (0, 0),
                         first=lambda: (pl.program_id(0) == 0) & (pl.program_id(1) == 0)),
                    dict(idx=5, kind="acc", shape=(1, D_INNER), dtype=F32, block=(1, w), imap=lambda j, i: (0, j),
                         first=lambda: pl.program_id(1) == 0)])


def _merge_operands(ya, ys, proj, b_gates):
    w = 4 * LANES
    g0 = G0 // w
    nh = D_MODEL // w
    return [(ya, (TM, w), lambda j, i: (i, j)), (ys, (TM, w), lambda j, i: (i, j)),
            (proj, (TM, w), lambda j, i: (i, g0 + j)), (proj, (TM, w), lambda j, i: (i, g0 + nh + j)),
            (b_gates, (1, w), lambda j, i: (0, j)), (b_gates, (1, w), lambda j, i: (0, nh + j))]


def _merge(name, ya, ys, proj, b_gates):
    t = ya.shape[0]
    w = 4 * LANES
    return _ew(name, _merge_fn, (D_MODEL // w, t // TM), _merge_operands(ya, ys, proj, b_gates),
               [((t, D_MODEL), BF16, (TM, w), lambda j, i: (i, j))])[0]


def _merge_bwd(name, ya, ys, proj, b_gates, dmixed):
    t = ya.shape[0]
    w = 4 * LANES
    blk = ((TM, w), lambda j, i: (i, j))
    first = lambda: pl.program_id(1) == 0
    tile = lambda k, dt: dict(idx=k, kind="tile", shape=(t, D_MODEL), dtype=dt, block=blk[0], imap=blk[1])
    acc = lambda k: dict(idx=k, kind="acc", shape=(1, D_MODEL), dtype=F32, block=(1, w), imap=lambda j, i: (0, j), first=first)
    return _ew_bwd(name, _merge_fn, (D_MODEL // w, t // TM), _merge_operands(ya, ys, proj, b_gates), [(dmixed, *blk)],
                   [tile(0, BF16), tile(1, BF16), tile(2, BF16), tile(3, BF16), acc(4), acc(5)])


def _loss(name, y, tgt):
    t = y.shape[0]
    blk = pl.BlockSpec((TM, D_MODEL), lambda i: (i, 0))

    def body(y_ref, t_ref, dy_ref, l_ref):
        e = y_ref[...] - t_ref[...]
        dy_ref[...] = e * (1.0 / D_MODEL)
        part = jnp.sum(jnp.sum(e * e, axis=-1, keepdims=True), axis=0, keepdims=True) * (0.5 / D_MODEL)
        part = jnp.broadcast_to(part, (8, LANES))

        @pl.when(pl.program_id(0) == 0)
        def _():
            l_ref[...] = part

        @pl.when(pl.program_id(0) > 0)
        def _():
            l_ref[...] += part

    return pl.pallas_call(
        body, name=name, grid=(t // TM,), in_specs=[blk, blk],
        out_specs=[blk, pl.BlockSpec((8, LANES), lambda i: (0, 0))],
        out_shape=[jax.ShapeDtypeStruct((t, D_MODEL), F32), jax.ShapeDtypeStruct((8, LANES), F32)],
        compiler_params=_cp(("arbitrary",)),
    )(y, tgt)


def _adamw_fn(w, g, m, v):
    m2 = B1 * m + (1.0 - B1) * g
    v2 = B2 * v + (1.0 - B2) * (g * g)
    m_hat = m2 / (1.0 - B1 ** STEP)
    v_hat = v2 / (1.0 - B2 ** STEP)
    return -LR * (m_hat / (jnp.sqrt(v_hat) + ADAM_EPS) + WD * w), m2, v2


def _adamw(name, w, g, m, v):
    rows, cols = w.shape
    tm = rows
    for cand in (512, 256, 128, 64, 32, 16, 8):
        if rows % cand == 0 and cand * cols * 4 <= (1 << 21):
            tm = cand
            break
    blk = ((tm, cols), lambda i: (i, 0))
    return _ew(name, _adamw_fn, (rows // tm,), [(a, *blk) for a in (w, g, m, v)], [((rows, cols), F32, *blk)] * 3)


NCH = 4
TMM = 1024
TKK = 1024


def _ffn_fwd(tag, x, nw, wg, wu, wd, li):
    t, fc = x.shape[0], wg.shape[-1]
    h = _rmsnorm(tag + "_norm", x, nw)

    def up_body(h_ref, wg_ref, wu_ref, g_ref, u_ref, a_ref):
        hv = h_ref[...]
        g = _dg(hv, wg_ref[...], 1, 0).astype(BF16)
        u = _dg(hv, wu_ref[...], 1, 0).astype(BF16)
        g_ref[...] = g
        u_ref[...] = u
        a_ref[...] = _swiglu_fn(g.astype(F32), u.astype(F32))[0].astype(BF16)

    w_spec = pl.BlockSpec((None, None, D_MODEL, fc), lambda k, i: (k, li, 0, 0))
    o_spec = pl.BlockSpec((None, TMM, fc), lambda k, i: (k, i, 0))
    g, u, a = pl.pallas_call(
        up_body, name=tag + "_up_act", grid=(NCH, t // TMM),
        in_specs=[pl.BlockSpec((TMM, D_MODEL), lambda k, i: (i, 0)), w_spec, w_spec], out_specs=[o_spec] * 3,
        out_shape=[jax.ShapeDtypeStruct((NCH, t, fc), BF16)] * 3, compiler_params=_cp(("parallel", "parallel")),
    )(h, wg, wu)
    row = ((TMM, D_MODEL), lambda i, j, k: (i, 0))
    y = _mmx(tag + "_down", (t // TMM, 1, NCH),
             (a, (None, TMM, fc), lambda i, j, k: (k, i, 0)),
             (wd, (None, None, fc, D_MODEL), lambda i, j, k: (k, li, 0, 0)),
             ((t, D_MODEL), F32, *row), (1, 0), alpha=0.5, res=(x, *row))
    return y, (x, h, g, u, a)


def _ffn_bwd(tag, saved, nw, wg, wu, wd, li, dy, bufs, gl):
    x, h, g, u, a = saved
    t, fc = x.shape[0], wg.shape[-1]
    bg, bu, bd = bufs
    def dact_body(dy_ref, wd_ref, g_ref, u_ref, dg_ref, du_ref):
        da = _dg(dy_ref[...], wd_ref[...], 1, 1) * 0.5
        _, vjp = jax.vjp(_swiglu_fn, g_ref[...].astype(F32), u_ref[...].astype(F32))
        dg, du = vjp((da,))
        dg_ref[...] = dg.astype(BF16)
        du_ref[...] = du.astype(BF16)

    c_spec = pl.BlockSpec((None, TMM, fc), lambda k, i: (k, i, 0))
    dg, du = pl.pallas_call(
        dact_body, name=tag + "_down_dx_act", grid=(NCH, t // TMM),
        in_specs=[pl.BlockSpec((TMM, D_MODEL), lambda k, i: (i, 0)),
                  pl.BlockSpec((None, None, fc, D_MODEL), lambda k, i: (k, li, 0, 0)), c_spec, c_spec],
        out_specs=[c_spec] * 2, out_shape=[jax.ShapeDtypeStruct((NCH, t, fc), BF16)] * 2,
        compiler_params=_cp(("parallel", "parallel")),
    )(dy, wd, g, u)
    bd = _mmx(tag + "_down_dw", (NCH, 1, t // TKK),
              (a, (None, TKK, fc), lambda k, j, kk: (k, kk, 0)),
              (dy, (TKK, D_MODEL), lambda k, j, kk: (kk, 0)),
              (bd.shape, BF16, (None, None, fc, D_MODEL), lambda k, j, kk: (gl, k, 0, 0)), (0, 0), alpha=0.5, into=bd)
    def dw(name, d, buf):
        return _mmx(name, (NCH, 1, t // TKK),
                    (h, (TKK, D_MODEL), lambda k, i, kk: (kk, 0)),
                    (d, (None, TKK, fc), lambda k, i, kk: (k, kk, 0)),
                    (buf.shape, BF16, (None, None, D_MODEL, fc), lambda k, i, kk: (gl, k, 0, 0)), (0, 0), into=buf)

    bg, bu = dw(tag + "_gate_dw", dg, bg), dw(tag + "_up_dw", du, bu)
    row = ((TMM, D_MODEL), lambda i, j, k: (i, 0))

    def dx_(name, d, w, res):
        return _mmx(name, (t // TMM, 1, NCH),
                    (d, (None, TMM, fc), lambda i, j, k: (k, i, 0)),
                    (w, (None, None, D_MODEL, fc), lambda i, j, k: (k, li, 0, 0)),
                    ((t, D_MODEL), F32, *row), (1, 1), res=None if res is None else (res, *row))

    dh = dx_(tag + "_up_dx", du, wu, dx_(tag + "_gate_dx", dg, wg, None))
    dx, dnw = _rmsnorm_bwd(tag + "_norm_bwd", x, nw, dh, dy)
    return dx, dnw, (bg, bu, bd)


def _mixer_fwd(tag, x, p, c):
    t = x.shape[0]
    bl = t // SEQ
    h = _rmsnorm(tag + "_norm", x, p["mix_norm_w"])
    proj = _mm(tag + "_in", h, p["w_in"], "nn")
    qk = _qkprep(tag + "_qk", proj, p["qkw"], c["cos"], c["sin"], c)
    os_, ls_ = [], []
    for g in range(3):
        o, l = _att_fwd(f"{tag}_att{g}", qk, proj, g)
        os_.append(o)
        ls_.append(l)
    att = _attmix(tag + "_attmix", os_, ls_)
    li = p["layer"]
    wa, ws, wo = p["w_att_proj"], p["w_ssm_proj"], p["w_out"]
    ca, cs, co = wa.shape[-1], ws.shape[-2], wo.shape[-2]
    row = ((TMM, D_MODEL), lambda i, j, k: (i, 0))
    ya = _mmx(tag + "_attproj", (t // TMM, NCH, 1),
              (att, (TMM, ATT_OUT), lambda i, k, kk: (i, 0)),
              (wa, (None, None, ATT_OUT, ca), lambda i, k, kk: (k, li, 0, 0)),
              ((t, D_MODEL), F32, (TMM, ca), lambda i, k, kk: (i, k)), (1, 0))
    proj3 = proj.reshape(bl, SEQ, NP)
    xc3 = _conv(tag + "_conv", proj3, p["conv_w"], p["conv_b"])
    xc = xc3.reshape(t, XBC)
    dtb, dab = _ssdpre(tag + "_ssdpre", proj, p["dt_bias"], p["a_log"], c["ex"])
    dtb3, dab3 = dtb.reshape(bl, SEQ, D_INNER), dab.reshape(bl, SEQ, D_INNER)
    y3, states = _ssd_fwd(tag + "_ssd", xc3, dtb3, dab3, c["ltri"])
    y = y3.reshape(t, D_INNER)
    ysn = _ssdpost(tag + "_ssdpost", y, xc, proj, p["d_skip"], c["ex"], p["ssm_norm_w"])
    ys = _mmx(tag + "_ssmproj", (t // TMM, 1, NCH),
              (ysn, (TMM, cs), lambda i, j, k: (i, k)),
              (ws, (None, None, cs, D_MODEL), lambda i, j, k: (k, li, 0, 0)),
              ((t, D_MODEL), F32, *row), (1, 0))
    mixed = _merge(tag + "_merge", ya, ys, proj, p["b_gates"])
    out = _mmx(tag + "_out", (t // TMM, 1, NCH),
               (mixed, (TMM, co), lambda i, j, k: (i, k)),
               (wo, (None, None, co, D_MODEL), lambda i, j, k: (k, li, 0, 0)),
               ((t, D_MODEL), F32, *row), (1, 0), res=(x, *row))
    return out, (x, h, proj, qk, os_, ls_, att, ya, xc3, dtb3, dab3, states, y, ysn, ys, mixed)


def _mixer_bwd(tag, saved, p, c, dout, bufs):
    x, h, proj, qk, os_, ls_, att, ya, xc3, dtb3, dab3, states, y, ysn, ys, mixed = saved
    t = x.shape[0]
    bl = t // SEQ
    xc = xc3.reshape(t, XBC)
    proj3 = proj.reshape(bl, SEQ, NP)
    gr = {}
    li, gl = p["layer"], p["global_layer"]
    wa, ws, wo = p["w_att_proj"], p["w_ssm_proj"], p["w_out"]
    ca, cs, co = wa.shape[-1], ws.shape[-2], wo.shape[-2]
    b_att, b_ssm, b_out = bufs

    def chunk_dx(name, d, w, cw):
        return _mmx(name, (t // TMM, NCH, D_MODEL // TKK),
                    (d, (TMM, TKK), lambda i, k, kk: (i, kk)),
                    (w, (None, None, cw, TKK), lambda i, k, kk: (k, li, 0, kk)),
                    ((t, NCH * cw), F32, (TMM, cw), lambda i, k, kk: (i, k)), (1, 1))

    def full_dw(name, a_, d, buf):
        kdim = a_.shape[1]
        tm = min(kdim, 1024)
        return _mmx(name, (kdim // tm, 1, t // TKK),
                    (a_, (TKK, tm), lambda i, j, kk: (kk, i)),
                    (d, (TKK, D_MODEL), lambda i, j, kk: (kk, 0)),
                    (buf.shape, BF16, (None, tm, D_MODEL), lambda i, j, kk: (gl, i, 0)), (0, 0), into=buf)

    dmixed = chunk_dx(tag + "_out_dx", dout, wo, co)
    b_out = full_dw(tag + "_out_dw", mixed, dout, b_out)
    dya, dys, dga, dgs, dba, dbs = _merge_bwd(tag + "_merge_bwd", ya, ys, proj, p["b_gates"], dmixed)
    gr["b_gates"] = jnp.concatenate([dba, dbs], axis=1)
    datt = _mmx(tag + "_attproj_dx", (t // TMM, 1, NCH),
                (dya, (TMM, ca), lambda i, j, k: (i, k)),
                (wa, (None, None, ATT_OUT, ca), lambda i, j, k: (k, li, 0, 0)),
                ((t, ATT_OUT), F32, (TMM, ATT_OUT), lambda i, j, k: (i, 0)), (1, 1))
    b_att = _mmx(tag + "_attproj_dw", (NCH, 1, t // TKK),
                 (att, (TKK, ATT_OUT), lambda k, j, kk: (kk, 0)),
                 (dya, (TKK, ca), lambda k, j, kk: (kk, k)),
                 (b_att.shape, BF16, (None, None, ATT_OUT, ca), lambda k, j, kk: (gl, k, 0, 0)), (0, 0), into=b_att)
    dysn = chunk_dx(tag + "_ssmproj_dx", dys, ws, cs)
    b_ssm = full_dw(tag + "_ssmproj_dw", ysn, dys, b_ssm)
    gr["bufs"] = (b_att, b_ssm, b_out)
    dmix = _attmix_bwd(tag + "_attmix_bwd", os_, ls_, datt)
    dq = dk = dv = jnp.zeros((t, QKV), F32)
    for g in range(3):
        dq, dk, dv = _att_bwd(f"{tag}_att{g}_bwd", qk, proj, g, dmix[g], dmix[3 + g], dq, dk, dv)
    dproj = jnp.zeros((t, NP), BF16)
    dproj = lax.dynamic_update_slice(dproj, dv.astype(BF16), (0, V0))
    dproj = lax.dynamic_update_slice(dproj, dga, (0, G0))
    dproj = lax.dynamic_update_slice(dproj, dgs, (0, G0 + D_MODEL))
    dproj, gr["qkw"] = _qkprep_bwd(tag + "_qk_bwd", proj, p["qkw"], c["cos"], c["sin"], c, dq, dk, dproj)
    dy, dxs_part, dproj, gr["d_skip"], gr["ssm_norm_w"] = _ssdpost_bwd(
        tag + "_ssdpost_bwd", y, xc, proj, p["d_skip"], c["ex"], p["ssm_norm_w"], dysn, dproj)
    dxs3, ddtb3, ddab3, db3, dc3 = _ssd_bwd(
        tag + "_ssd_bwd", xc3, dtb3, dab3, c["ltri"], states, dy.reshape(bl, SEQ, D_INNER), dxs_part.reshape(bl, SEQ, D_INNER))
    ddt, gr["dt_bias"], gr["a_log"] = _ssdpre_bwd(
        tag + "_ssdpre_bwd", proj, p["dt_bias"], p["a_log"], c["ex"], ddtb3.reshape(t, D_INNER), ddab3.reshape(t, D_INNER))
    dproj = lax.dynamic_update_slice(dproj, ddt, (0, DT0))
    dproj3, dcw0, dcw1, dcw2, dcw3, gr["conv_b"] = _conv_bwd(
        tag + "_conv_bwd", proj3, p["conv_w"], p["conv_b"], dxs3, db3, dc3, dproj.reshape(bl, SEQ, NP))
    dproj = dproj3.reshape(t, NP)
    gr["conv_w"] = jnp.concatenate([dcw0, dcw1, dcw2, dcw3], axis=0)
    gr["w_in"] = _mm(tag + "_in_dw", h, dproj, "tn", out_dtype=BF16)
    dh = _mm(tag + "_in_dx", dproj, p["w_in"], "nt")
    dx, gr["mix_norm_w"] = _rmsnorm_bwd(tag + "_norm_bwd", x, p["mix_norm_w"], dh, dout)
    return dx, gr


def _constants():
    cos, sin = _rope_tables()
    return dict(cos=cos, sin=sin, hmean=_head_mean_mat(),
                ex=_head_expand_mat(), ltri=_ltri_mat())


ANY = pl.BlockSpec(memory_space=pl.ANY)


def _mesh_pos():
    return lax.axis_index("x"), lax.axis_index("y"), lax.axis_index("c")


def _other_chips(x, y):
    return [(1 - x, y), (x, 1 - y), (1 - x, 1 - y)]


def _gather_exchange(srcs, outs, send_sems, recv_sems):
    n = len(srcs)
    x, y, c = _mesh_pos()
    chips = _other_chips(x, y)

    def part(a, chip, hf):
        h = srcs[a].shape[1] // 2
        return outs[a].at[2 * chip[0] + chip[1], :, pl.ds(hf * h, h), :]

    def mine(a):
        h = srcs[a].shape[1] // 2
        return srcs[a].at[:, pl.ds(c * h, h), :]

    def copy(a, k, src_ref, dst_ref, to):
        return pltpu.make_async_remote_copy(src_ref=src_ref, dst_ref=dst_ref, send_sem=send_sems.at[6 * a + k],
                                            recv_sem=recv_sems.at[6 * a + k], device_id=to, device_id_type=MESH)

    first = [copy(a, j, mine(a), part(a, (x, y), c), (*chip, c)) for a in range(n) for j, chip in enumerate(chips)]
    for cp in first:
        cp.start()
    passed = []
    for a in range(n):
        for j, chip in enumerate(chips):
            copy(a, j, part(a, chip, c), part(a, chip, c), (x, y, c)).wait_recv()
            fw = copy(a, 3 + j, part(a, chip, c), part(a, chip, c), (x, y, 1 - c))
            fw.start()
            passed.append(fw)
    for a in range(n):
        for j, chip in enumerate(chips):
            copy(a, 3 + j, part(a, chip, 1 - c), part(a, chip, 1 - c), (x, y, c)).wait_recv()
    for cp in first + passed:
        cp.wait_send()


def _gather_inits(ws, chip_idx):
    return [lax.dynamic_update_slice(jnp.zeros((NCH, *w.shape), w.dtype), w[None], (chip_idx[0], 0, 0, 0)) for w in ws]


def _all_gather_weights(ws, chip_idx):
    n = len(ws)
    inits = _gather_inits(ws, chip_idx)

    def body(*refs):
        _gather_exchange(refs[:n], refs[2 * n:3 * n], refs[3 * n], refs[3 * n + 1])

    return pl.pallas_call(
        body, name="all_gather_weights", out_shape=[jax.ShapeDtypeStruct(i.shape, i.dtype) for i in inits],
        in_specs=[ANY] * (2 * n), out_specs=[ANY] * n, input_output_aliases={n + a: a for a in range(n)},
        scratch_shapes=[pltpu.SemaphoreType.DMA((6 * n,)), pltpu.SemaphoreType.DMA((6 * n,))],
    )(*ws, *inits)


def _all_gather_weights_beside(ws, chip_idx):
    n = len(ws)
    src_refs = [jax.new_ref(w, memory_space=pltpu.MemorySpace.HBM) for w in ws]
    out_refs = [jax.new_ref(i, memory_space=pltpu.MemorySpace.HBM) for i in _gather_inits(ws, chip_idx)]

    @pl.kernel(mesh=plsc.ScalarSubcoreMesh(axis_name="sequencer", num_cores=1), name="all_gather_weights_beside",
               scratch_types=(pltpu.SemaphoreType.DMA((6 * n,)), pltpu.SemaphoreType.DMA((6 * n,))),
               compiler_params=pltpu.CompilerParams(collective_id=1))
    def launch(send_sems, recv_sems):
        x, y, c = _mesh_pos()
        barrier = pltpu.get_barrier_semaphore()
        for peer in [(x, y, 1 - c)] + [(*chip, c) for chip in _other_chips(x, y)]:
            pl.semaphore_signal(barrier, inc=1, device_id=peer, device_id_type=MESH)
        pl.semaphore_wait(barrier, 4)
        _gather_exchange(src_refs, out_refs, send_sems, recv_sems)

    launch()
    return [r[...] for r in out_refs]


def _pair_exchange(gs):
    n = len(gs)

    def body(*refs):
        srcs, outs, send_sems, recv_sems = refs[:n], refs[n:2 * n], refs[2 * n], refs[2 * n + 1]
        x, y, c = _mesh_pos()
        cps = []
        for a in range(n):
            h = gs[a].shape[2] // 2
            cps.append(pltpu.make_async_remote_copy(
                src_ref=srcs[a].at[:, :, pl.ds((1 - c) * h, h), :], dst_ref=outs[a], send_sem=send_sems.at[a],
                recv_sem=recv_sems.at[a], device_id=(x, y, 1 - c), device_id_type=MESH))
        for cp in cps:
            cp.start()
        for cp in cps:
            cp.wait()

    return pl.pallas_call(
        body, name="grad_pair_exchange",
        out_shape=[jax.ShapeDtypeStruct((g.shape[0], g.shape[1], g.shape[2] // 2, g.shape[3]), g.dtype) for g in gs],
        in_specs=[ANY] * n, out_specs=[ANY] * n,
        scratch_shapes=[pltpu.SemaphoreType.DMA((n,)), pltpu.SemaphoreType.DMA((n,))],
    )(*gs)


def _chip_exchange(hs):
    n = len(hs)

    def body(*refs):
        srcs, outs, send_sems, recv_sems = refs[:n], refs[n:2 * n], refs[2 * n], refs[2 * n + 1]
        x, y, c = _mesh_pos()
        cps = [pltpu.make_async_remote_copy(
            src_ref=srcs[a].at[:, 2 * chip[0] + chip[1]], dst_ref=outs[a].at[j], send_sem=send_sems.at[3 * a + j],
            recv_sem=recv_sems.at[3 * a + j], device_id=(*chip, c), device_id_type=MESH)
            for a in range(n) for j, chip in enumerate(_other_chips(x, y))]
        for cp in cps:
            cp.start()
        for cp in cps:
            cp.wait()

    return pl.pallas_call(
        body, name="grad_chip_exchange",
        out_shape=[jax.ShapeDtypeStruct((3, h.shape[0], h.shape[2], h.shape[3]), h.dtype) for h in hs],
        in_specs=[ANY] * n, out_specs=[ANY] * n,
        scratch_shapes=[pltpu.SemaphoreType.DMA((3 * n,)), pltpu.SemaphoreType.DMA((3 * n,))],
    )(*hs)


def _pair_share(rs):
    n = len(rs)

    def body(*refs):
        outs, send_sems, recv_sems = refs[n:2 * n], refs[2 * n], refs[2 * n + 1]
        x, y, c = _mesh_pos()
        cps = [pltpu.make_async_remote_copy(src_ref=outs[a].at[:, c], dst_ref=outs[a].at[:, c], send_sem=send_sems.at[a],
                                            recv_sem=recv_sems.at[a], device_id=(x, y, 1 - c), device_id_type=MESH)
               for a in range(n)]
        for cp in cps:
            cp.start()
        for a in range(n):
            pltpu.make_async_remote_copy(src_ref=outs[a].at[:, 1 - c], dst_ref=outs[a].at[:, 1 - c],
                                         send_sem=send_sems.at[a], recv_sem=recv_sems.at[a], device_id=(x, y, c),
                                         device_id_type=MESH).wait_recv()
        for cp in cps:
            cp.wait_send()

    return pl.pallas_call(
        body, name="grad_pair_share", out_shape=[jax.ShapeDtypeStruct(r.shape, r.dtype) for r in rs],
        in_specs=[ANY] * n, out_specs=[ANY] * n, input_output_aliases={a: a for a in range(n)},
        scratch_shapes=[pltpu.SemaphoreType.DMA((n,)), pltpu.SemaphoreType.DMA((n,))],
    )(*rs)


def _pair_sum(name, g, recv, c_idx):
    d, k, h, b = recv.shape
    g5 = g.reshape(d * k, 2, h, b)

    def body(c_ref, a_ref, b_ref, o_ref):
        o_ref[...] = (a_ref[...].astype(F32) + b_ref[...].astype(F32)).astype(o_ref.dtype)

    out = pl.pallas_call(
        body, name=name,
        grid_spec=pltpu.PrefetchScalarGridSpec(
            num_scalar_prefetch=1, grid=(d * k,),
            in_specs=[pl.BlockSpec((None, None, h, b), lambda i, c: (i, c[0], 0, 0)),
                      pl.BlockSpec((None, h, b), lambda i, c: (i, 0, 0))],
            out_specs=pl.BlockSpec((None, h, b), lambda i, c: (i, 0, 0))),
        out_shape=jax.ShapeDtypeStruct((d * k, h, b), BF16),
        compiler_params=_cp(("arbitrary",)),
    )(c_idx, g5, recv.reshape(d * k, h, b))
    return out.reshape(d, k, h, b)


def _chip_sum(name, ha, recv, chip_idx, c_idx):
    d, _, h, b = ha.shape

    def body(k_ref, c_ref, a_ref, r0, r1, r2, o_ref):
        o_ref[...] = ((a_ref[...].astype(F32) + r0[...].astype(F32)) + r1[...].astype(F32)) + r2[...].astype(F32)

    blk = (None, None, h, b)
    return pl.pallas_call(
        body, name=name,
        grid_spec=pltpu.PrefetchScalarGridSpec(
            num_scalar_prefetch=2, grid=(d,),
            in_specs=[pl.BlockSpec(blk, lambda l, k, c: (l, k[0], 0, 0))] +
                     [pl.BlockSpec(blk, lambda l, k, c, j=j: (j, l, 0, 0)) for j in range(3)],
            out_specs=pl.BlockSpec(blk, lambda l, k, c: (l, c[0], 0, 0))),
        out_shape=jax.ShapeDtypeStruct((d, 2, h, b), F32),
        compiler_params=_cp(("arbitrary",)),
    )(chip_idx, c_idx, ha, recv, recv, recv)


def _all_sum_small(name, vec):
    rows = vec.shape[0]

    def body(v_ref, o_ref, buf, send_sems, recv_sems):
        x, y, c = _mesh_pos()
        me, sibling = (x, y, c), (x, y, 1 - c)
        chips = _other_chips(x, y)

        def slot(p):
            return buf.at[4 * p[0] + 2 * p[1] + p[2]]

        def copy(k, block, to, src=None):
            return pltpu.make_async_remote_copy(src_ref=slot(block) if src is None else src, dst_ref=slot(block),
                                                send_sem=send_sems.at[k], recv_sem=recv_sems.at[k],
                                                device_id=to, device_id_type=MESH)

        first = [copy(0, me, sibling, src=v_ref)]
        first += [copy(1 + j, me, (*chip, c), src=v_ref) for j, chip in enumerate(chips)]
        for cp in first:
            cp.start()
        passed = [copy(4 + j, (*chip, c), sibling) for j, chip in enumerate(chips)]
        for j, chip in enumerate(chips):
            copy(1 + j, (*chip, c), me).wait_recv()
            passed[j].start()
        copy(0, sibling, me).wait_recv()
        for j, chip in enumerate(chips):
            copy(4 + j, (*chip, 1 - c), me).wait_recv()
        for cp in first + passed:
            cp.wait_send()
        slot(me)[...] = v_ref[...]
        acc = buf[0]
        for k in range(1, 8):
            acc = acc + buf[k]
        o_ref[...] = acc

    vm = pl.BlockSpec(memory_space=pltpu.VMEM)
    return pl.pallas_call(
        body, name=name, out_shape=jax.ShapeDtypeStruct((rows, LANES), F32),
        in_specs=[vm], out_specs=vm, compiler_params=pltpu.CompilerParams(vmem_limit_bytes=VMEM_LIMIT),
        scratch_shapes=[pltpu.VMEM((8, rows, LANES), F32), pltpu.SemaphoreType.DMA((7,)), pltpu.SemaphoreType.DMA((7,))],
    )(vec)


def _pad_lanes(v, n=LANES):
    return jnp.pad(v, (0, n - v.shape[0]))[None, :]


def _w_in_to_kernel(w):
    return jnp.concatenate([w[:, :6656], w[:, 9760:N_IN], w[:, 6656:9728], w[:, 9728:9760],
                            jnp.zeros((w.shape[0], NP - N_IN), w.dtype)], axis=1)


def _w_in_from_kernel(w):
    return jnp.concatenate([w[:, :6656], w[:, X0:DT0], w[:, DT0:DT0 + 32], w[:, G0:X0]], axis=1)


def _layer_params(big, small, i):
    p = {k: big[k][i][0] for k in GRAD_BUFS}
    li = big["w_in"][i][1]
    p["layer"], p["global_layer"] = li, i
    w_in = big["w_in"][i][0]
    cw = w_in.shape[-1]
    pieces = []
    for lo, hi in ((0, 6656), (9760, N_IN), (6656, 9728), (9728, 9760)):
        for k in range(NCH):
            a, b = max(lo, k * cw), min(hi, (k + 1) * cw)
            if a < b:
                pieces.append(w_in[k, li, :, a - k * cw:b - k * cw])
    p["w_in"] = jnp.concatenate(pieces + [jnp.zeros((D_MODEL, NP - N_IN), w_in.dtype)], axis=1)
    p["conv_w"] = big["conv_w"][i][:, None, :]
    for k in ("ffn1_norm_w", "mix_norm_w", "ffn2_norm_w", "b_gates", "conv_b", "ssm_norm_w"):
        p[k] = small[k][i][None, :]
    for k in ("dt_bias", "a_log", "d_skip"):
        p[k] = _pad_lanes(small[k][i])
    p["qkw"] = jnp.stack([_pad_lanes(small["q_norm_w"][i]), _pad_lanes(small["k_norm_w"][i])])
    return p


GRAD_BUFS = ("ffn1_w_gate", "ffn1_w_up", "ffn1_w_down", "w_att_proj", "w_ssm_proj", "w_out",
             "ffn2_w_gate", "ffn2_w_up", "ffn2_w_down")


def _local_step(x, tgt, layers, c):
    depth = len(layers)
    ffn = {f: tuple(f + s for s in ("_w_gate", "_w_up", "_w_down")) for f in ("ffn1", "ffn2")}
    saved = []
    for i, p in enumerate(layers):
        x, s1 = _ffn_fwd(f"L{i}_ffn1", x, p["ffn1_norm_w"], *[p[n] for n in ffn["ffn1"]], p["layer"])
        x, s2 = _mixer_fwd(f"L{i}_mix", x, p, c)
        x, s3 = _ffn_fwd(f"L{i}_ffn2", x, p["ffn2_norm_w"], *[p[n] for n in ffn["ffn2"]], p["layer"])
        saved.append((s1, s2, s3))
    dx, loss_blk = _loss("loss", x, tgt)
    buf = {}
    for n in GRAD_BUFS:
        a, b = layers[0][n].shape[-2:]
        shape = (depth, NCH * a, b) if n in ("w_ssm_proj", "w_out") else (depth, NCH, a, b)
        buf[n] = jnp.zeros(shape, BF16)
    grads = [None] * depth
    for i in reversed(range(depth)):
        p = layers[i]
        s1, s2, s3 = saved[i]

        def ffn_bwd(f, s, dy):
            names = ffn[f]
            d, dn, new = _ffn_bwd(f"L{i}_{f}", s, p[f + "_norm_w"], *[p[n] for n in names], p["layer"], dy,
                                  tuple(buf[n] for n in names), i)
            buf.update(zip(names, new))
            return d, dn

        dx, dn2 = ffn_bwd("ffn2", s3, dx)
        dx, gr = _mixer_bwd(f"L{i}_mix", s2, p, c, dx, (buf["w_att_proj"], buf["w_ssm_proj"], buf["w_out"]))
        buf["w_att_proj"], buf["w_ssm_proj"], buf["w_out"] = gr.pop("bufs")
        dx, dn1 = ffn_bwd("ffn1", s1, dx)
        gr.update(ffn1_norm_w=dn1, ffn2_norm_w=dn2)
        grads[i] = gr
    for n in ("w_ssm_proj", "w_out"):
        a, b = layers[0][n].shape[-2:]
        buf[n] = buf[n].reshape(depth, NCH, a, b)
    return loss_blk, dx, grads, buf


WEIGHTS = ["ffn1_norm_w", "ffn1_w_gate", "ffn1_w_up", "ffn1_w_down", "mix_norm_w", "w_in", "b_gates", "q_norm_w",
           "k_norm_w", "conv_w", "conv_b", "dt_bias", "a_log", "d_skip", "ssm_norm_w", "w_att_proj", "w_ssm_proj",
           "w_out", "ffn2_norm_w", "ffn2_w_gate", "ffn2_w_up", "ffn2_w_down"]
SHARD_AXIS = {"ffn1_w_gate": 2, "ffn1_w_up": 2, "ffn1_w_down": 1, "w_in": 2, "conv_w": 2, "w_att_proj": 2,
              "w_ssm_proj": 1, "w_out": 1, "ffn2_w_gate": 2, "ffn2_w_up": 2, "ffn2_w_down": 1}
BIG = [n for n in WEIGHTS if n in SHARD_AXIS]
SMALL = [n for n in WEIGHTS if n not in SHARD_AXIS]
def _from_flat(flat, shapes):
    v = flat.reshape(-1)
    out, off = [], 0
    for s in shapes:
        n = math.prod(s)
        out.append(v[off:off + n].reshape(s))
        off += n
    return out


def _pack_small(parts):
    v = jnp.concatenate([p.astype(F32).reshape(-1) for p in parts])
    rows = -(-v.shape[0] // (8 * LANES)) * 8
    return jnp.pad(v, (0, rows * LANES - v.shape[0])).reshape(rows, LANES)


def kernel(x, ffn1_norm_w, ffn1_w_gate, ffn1_w_up, ffn1_w_down, mix_norm_w, w_in, b_gates, q_norm_w, k_norm_w, conv_w, conv_b, dt_bias, a_log, d_skip, ssm_norm_w, w_att_proj, w_ssm_proj, w_out, ffn2_norm_w, ffn2_w_gate, ffn2_w_up, ffn2_w_down, loss_target, m_ffn1_norm_w, m_ffn1_w_gate, m_ffn1_w_up, m_ffn1_w_down, m_mix_norm_w, m_w_in, m_b_gates, m_q_norm_w, m_k_norm_w, m_conv_w, m_conv_b, m_dt_bias, m_a_log, m_d_skip, m_ssm_norm_w, m_w_att_proj, m_w_ssm_proj, m_w_out, m_ffn2_norm_w, m_ffn2_w_gate, m_ffn2_w_up, m_ffn2_w_down, v_ffn1_norm_w, v_ffn1_w_gate, v_ffn1_w_up, v_ffn1_w_down, v_mix_norm_w, v_w_in, v_b_gates, v_q_norm_w, v_k_norm_w, v_conv_w, v_conv_b, v_dt_bias, v_a_log, v_d_skip, v_ssm_norm_w, v_w_att_proj, v_w_ssm_proj, v_w_out, v_ffn2_norm_w, v_ffn2_w_gate, v_ffn2_w_up, v_ffn2_w_down):
    w = dict(zip(WEIGHTS, (ffn1_norm_w, ffn1_w_gate, ffn1_w_up, ffn1_w_down, mix_norm_w, w_in, b_gates, q_norm_w, k_norm_w, conv_w, conv_b, dt_bias, a_log, d_skip, ssm_norm_w, w_att_proj, w_ssm_proj, w_out, ffn2_norm_w, ffn2_w_gate, ffn2_w_up, ffn2_w_down)))
    m = dict(zip(WEIGHTS, (m_ffn1_norm_w, m_ffn1_w_gate, m_ffn1_w_up, m_ffn1_w_down, m_mix_norm_w, m_w_in, m_b_gates, m_q_norm_w, m_k_norm_w, m_conv_w, m_conv_b, m_dt_bias, m_a_log, m_d_skip, m_ssm_norm_w, m_w_att_proj, m_w_ssm_proj, m_w_out, m_ffn2_norm_w, m_ffn2_w_gate, m_ffn2_w_up, m_ffn2_w_down)))
    v = dict(zip(WEIGHTS, (v_ffn1_norm_w, v_ffn1_w_gate, v_ffn1_w_up, v_ffn1_w_down, v_mix_norm_w, v_w_in, v_b_gates, v_q_norm_w, v_k_norm_w, v_conv_w, v_conv_b, v_dt_bias, v_a_log, v_d_skip, v_ssm_norm_w, v_w_att_proj, v_w_ssm_proj, v_w_out, v_ffn2_norm_w, v_ffn2_w_gate, v_ffn2_w_up, v_ffn2_w_down)))
    depth = ffn1_norm_w.shape[0]
    bl = x.shape[0]
    t = bl * SEQ
    mx, my, mc = lax.axis_index("x"), lax.axis_index("y"), lax.axis_index("c")
    c_idx = mc.astype(jnp.int32).reshape(1)
    chip_idx = (2 * mx + my).astype(jnp.int32).reshape(1)

    cw_width = conv_w.shape[2]
    slots = lax.dynamic_update_slice(jnp.zeros((NCH, *conv_w.shape), F32), jnp.where(mc == 0, conv_w, 0.0)[None],
                                     (chip_idx[0], 0, 0, 0))
    conv_all = _all_sum_small("conv_gather", slots.reshape(-1, LANES)).reshape(NCH, *conv_w.shape)
    big = {"conv_w": jnp.concatenate([conv_all[k] for k in range(NCH)], axis=2)}

    mm_names = [n for n in BIG if n != "conv_w"]
    shards = [w[n].astype(BF16) for n in mm_names]
    first = _all_gather_weights([s[:1] for s in shards], chip_idx)
    rest = _all_gather_weights_beside([s[1:] for s in shards], chip_idx) if depth > 1 else first
    for n, a0, a1 in zip(mm_names, first, rest):
        big[n] = [(a0, 0)] + [(a1, i - 1) for i in range(1, depth)]
    small = {n: w[n] for n in SMALL}

    c = _constants()
    layers = [_layer_params(big, small, i) for i in range(depth)]
    loss_blk, dx, grads, buf = _local_step(x.reshape(t, D_MODEL), loss_target.reshape(t, D_MODEL), layers, c)
    grad_x = dx.reshape(bl, SEQ, D_MODEL)

    buf["w_in"] = jnp.stack([_w_in_from_kernel(g["w_in"]).reshape(D_MODEL, NCH, -1).transpose(1, 0, 2) for g in grads])
    mine = [buf[n] for n in mm_names]
    from_sibling = _pair_exchange(mine)
    pairs = [_pair_sum("grad_pair_sum_" + n, g, r, c_idx) for n, g, r in zip(mm_names, mine, from_sibling)]
    from_chips = _chip_exchange(pairs)
    halves = [_chip_sum("grad_chip_sum_" + n, h, r, chip_idx, c_idx) for n, h, r in zip(mm_names, pairs, from_chips)]
    g_big = {n: r.reshape(w[n].shape) for n, r in zip(mm_names, _pair_share(halves))}

    def small_grad(n):
        if n == "q_norm_w":
            return jnp.stack([g["qkw"][0, 0, :64] for g in grads])
        if n == "k_norm_w":
            return jnp.stack([g["qkw"][1, 0, :64] for g in grads])
        return jnp.stack([g[n][0, :w[n].shape[1]] for g in grads])

    small_shapes = [w[n].shape for n in SMALL]
    conv_shape = (depth, conv_w.shape[1], NCH * cw_width)
    tot = _all_sum_small("small_all_sum", _pack_small(
        [small_grad(n) for n in SMALL] + [jnp.stack([g["conv_w"] for g in grads]), loss_blk[0, :1]]))
    unpacked = _from_flat(tot, small_shapes + [conv_shape, (1,)])
    g_small = dict(zip(SMALL, unpacked[:-2]))
    g_big["conv_w"] = lax.dynamic_slice_in_dim(unpacked[-2], chip_idx[0] * cw_width, cw_width, axis=2)
    loss = unpacked[-1][0]

    grad, delta, new_m, new_v = {}, {}, {}, {}
    for n in BIG:
        shp = w[n].shape
        two_d = (shp[0] * shp[1], shp[2])
        d_, m_, v_ = _adamw("adamw_" + n, w[n].reshape(two_d), g_big[n].reshape(two_d), m[n].reshape(two_d), v[n].reshape(two_d))
        grad[n], delta[n], new_m[n], new_v[n] = g_big[n], d_.reshape(shp), m_.reshape(shp), v_.reshape(shp)
    d_, m_, v_ = _adamw("adamw_small", _pack_small([w[n] for n in SMALL]), _pack_small([g_small[n] for n in SMALL]),
                        _pack_small([m[n] for n in SMALL]), _pack_small([v[n] for n in SMALL]))
    for n, a, b, c_ in zip(SMALL, _from_flat(d_, small_shapes), _from_flat(m_, small_shapes), _from_flat(v_, small_shapes)):
        grad[n], delta[n], new_m[n], new_v[n] = g_small[n], a, b, c_
    return (loss, grad_x, *[grad[n] for n in WEIGHTS], *[delta[n] for n in WEIGHTS],
            *[new_m[n] for n in WEIGHTS], *[new_v[n] for n in WEIGHTS])
```

```python
import functools
import math

import numpy as np
import jax
import jax.numpy as jnp
from jax import lax
from jax.experimental import pallas as pl
from jax.experimental.pallas import tpu as pltpu
from jax.experimental.pallas import tpu_sc as plsc

F32 = jnp.float32
BF16 = jnp.bfloat16
HI = lax.Precision.HIGHEST
MESH = pl.DeviceIdType.MESH

D_MODEL = 1024
SEQ = 2048
DEPTH = 4
D_FF = 2816
ATT_DILATIONS = (1, 4, 16)
BAND = 128
ATT_OUT = 512
QKV = 1536
D_INNER = 2048
N_SSM_HEADS = 32
N_SSM_GROUPS = 4
D_STATE = 128
XBC = 3072
CHUNK = 128
N_IN = 11808
EPS = 1e-6
ROPE_THETA = 10000.0
NP = 12288
Q0, K0, V0, Z0, G0, X0, DT0 = 0, 1536, 3072, 4608, 6656, 8704, 11776
DTW = 128
LR, B1, B2, ADAM_EPS, WD, STEP = 0.001, 0.9, 0.999, 1e-08, 0.01, 10

LANES = 128
VMEM_LIMIT = 48 * 1024 * 1024
NEG = -1e30


def _cp(sem=None, **kw):
    return pltpu.CompilerParams(dimension_semantics=sem, vmem_limit_bytes=VMEM_LIMIT, **kw)


def _dg(a, b, ca, cb):
    return lax.dot_general(a.astype(BF16), b.astype(BF16), (((ca,), (cb,)), ((), ())), preferred_element_type=F32)


@jax.custom_vjp
def dot_nn(a, b):
    return _dg(a, b, 1, 0)


def _dot_nn_fwd(a, b):
    return _dg(a, b, 1, 0), (a, b)


def _dot_nn_bwd(r, g):
    a, b = r
    return _dg(g, b, 1, 1).astype(a.dtype), _dg(a, g, 0, 0).astype(b.dtype)


dot_nn.defvjp(_dot_nn_fwd, _dot_nn_bwd)


@jax.custom_vjp
def dot_nt(a, b):
    return _dg(a, b, 1, 1)


def _dot_nt_fwd(a, b):
    return _dg(a, b, 1, 1), (a, b)


def _dot_nt_bwd(r, g):
    a, b = r
    return _dg(g, b, 1, 0).astype(a.dtype), _dg(g, a, 0, 0).astype(b.dtype)


dot_nt.defvjp(_dot_nt_fwd, _dot_nt_bwd)


@jax.custom_vjp
def dot_tn(a, b):
    return _dg(a, b, 0, 0)


def _dot_tn_fwd(a, b):
    return _dg(a, b, 0, 0), (a, b)


def _dot_tn_bwd(r, g):
    a, b = r
    return _dg(b, g, 1, 1).astype(a.dtype), _dg(a, g, 1, 0).astype(b.dtype)


dot_tn.defvjp(_dot_tn_fwd, _dot_tn_bwd)


def _dot2_raw(a, e, ce):
    hi = a.astype(BF16)
    lo = (a - hi.astype(F32)).astype(BF16)
    return _dg(hi, e, 1, ce) + _dg(lo, e, 1, ce)


@jax.custom_vjp
def dot2(a, e):
    return _dot2_raw(a, e, 0)


def _dot2_fwd(a, e):
    return _dot2_raw(a, e, 0), e


def _dot2_bwd(e, g):
    return _dot2_raw(g, e, 1), jnp.zeros_like(e)


dot2.defvjp(_dot2_fwd, _dot2_bwd)


def _tri2_raw(l, x, cl):
    hi = x.astype(BF16)
    lo = (x - hi.astype(F32)).astype(BF16)
    return _dg(l, hi, cl, 0) + _dg(l, lo, cl, 0)


@jax.custom_vjp
def tri_matmul(l, x):
    return _tri2_raw(l, x, 1)


def _tri_fwd(l, x):
    return _tri2_raw(l, x, 1), l


def _tri_bwd(l, g):
    return jnp.zeros_like(l), _tri2_raw(l, g, 0)


tri_matmul.defvjp(_tri_fwd, _tri_bwd)


def _dup64_raw(w):
    return w + pltpu.roll(w, 64, 1)


@jax.custom_vjp
def dup64(w):
    return _dup64_raw(w)


def _dup64_fwd(w):
    return _dup64_raw(w), None


def _dup64_bwd(_, g):
    lane = lax.broadcasted_iota(jnp.int32, g.shape, 1)
    return (jnp.where(lane < 64, _dup64_raw(g), 0.0),)


dup64.defvjp(_dup64_fwd, _dup64_bwd)


def _rope_rot_raw(y, sign):
    lane = lax.broadcasted_iota(jnp.int32, y.shape, 1)
    first_half = (lane & 32) == 0
    return sign * jnp.where(first_half, -pltpu.roll(y, LANES - 32, 1), pltpu.roll(y, 32, 1))


@jax.custom_vjp
def rope_rot(y):
    return _rope_rot_raw(y, 1.0)


def _rope_rot_fwd(y):
    return _rope_rot_raw(y, 1.0), None


def _rope_rot_bwd(_, g):
    return (_rope_rot_raw(g, -1.0),)


rope_rot.defvjp(_rope_rot_fwd, _rope_rot_bwd)


def _shift_rows_raw(x, s):
    n = x.shape[0]
    r = pltpu.roll(x, s % n, 0)
    rows = lax.broadcasted_iota(jnp.int32, x.shape, 0)
    keep = rows >= s if s > 0 else rows < n + s
    return jnp.where(keep, r, 0.0)


@functools.partial(jax.custom_vjp, nondiff_argnums=(1,))
def shift_rows(x, s):
    return _shift_rows_raw(x, s)


def _shift_fwd(x, s):
    return _shift_rows_raw(x, s), None


def _shift_bwd(s, _, g):
    return (_shift_rows_raw(g, -s),)


shift_rows.defvjp(_shift_fwd, _shift_bwd)


def _sigmoid(x):
    return 1.0 / (1.0 + jnp.exp(-x))


def _silu(x):
    return x * _sigmoid(x)


def _softplus(x):
    return jnp.maximum(x, 0.0) + jnp.log(1.0 + jnp.exp(-jnp.abs(x)))


def _head_mean_mat():
    i = np.arange(LANES)
    return jnp.asarray((i[:, None] // 64 == i[None, :] // 64).astype(np.float32) / 64.0)


def _head_expand_mat():
    e = np.zeros((LANES, D_INNER), np.float32)
    for l in range(D_INNER):
        e[l // 64, l] = 1.0
    return jnp.asarray(e)


def _ltri_mat():
    i = np.arange(CHUNK)
    return jnp.asarray((i[:, None] >= i[None, :]).astype(np.float32))


def _rope_tables():
    pos = jnp.arange(SEQ, dtype=F32)
    inv_freq = 1.0 / (ROPE_THETA ** (jnp.arange(0, 64, 2, dtype=F32) / 64))
    ang = pos[:, None] * inv_freq[None, :]
    return jnp.tile(jnp.cos(ang), (1, 4)), jnp.tile(jnp.sin(ang), (1, 4))


def _pick(n, cap):
    best = None
    for t in range(LANES, min(n, cap) + 1, LANES):
        if n % t == 0:
            best = t
    return best if best is not None else n


def _mm(name, a, b, mode, out_dtype=F32, alpha=None, res=None):
    if mode == "nn":
        (m, k), n = a.shape, b.shape[1]
    elif mode == "nt":
        (m, k), n = a.shape, b.shape[0]
    else:
        (k, m), n = a.shape, b.shape[1]
    tm, tn, tk = _pick(m, 1408), _pick(n, 1408), _pick(k, 1024)
    nk = k // tk
    ca, cb = {"nn": (1, 0), "nt": (1, 1), "tn": (0, 0)}[mode]
    a_spec = pl.BlockSpec((tk, tm), lambda i, j, kk: (kk, i)) if mode == "tn" else pl.BlockSpec((tm, tk), lambda i, j, kk: (i, kk))
    b_spec = pl.BlockSpec((tn, tk), lambda i, j, kk: (j, kk)) if mode == "nt" else pl.BlockSpec((tk, tn), lambda i, j, kk: (kk, j))
    o_spec = pl.BlockSpec((tm, tn), lambda i, j, kk: (i, j))
    has_res = res is not None

    def finish(acc, res_ref, o_ref):
        if alpha is not None:
            acc = acc * alpha
        if has_res:
            acc = acc + res_ref[...].astype(F32)
        o_ref[...] = acc.astype(o_ref.dtype)

    def body(*refs):
        a_ref, b_ref = refs[0], refs[1]
        res_ref = refs[2] if has_res else None
        o_ref = refs[3] if has_res else refs[2]
        part = _dg(a_ref[...], b_ref[...], ca, cb)
        if nk == 1:
            finish(part, res_ref, o_ref)
            return
        acc_ref = refs[-1]
        kk = pl.program_id(2)

        @pl.when(kk == 0)
        def _():
            acc_ref[...] = part

        @pl.when(kk > 0)
        def _():
            acc_ref[...] += part

        @pl.when(kk == nk - 1)
        def _():
            finish(acc_ref[...], res_ref, o_ref)

    ins = [a, b] + ([res] if has_res else [])
    in_specs = [a_spec, b_spec] + ([o_spec] if has_res else [])
    return pl.pallas_call(
        body, name=name, grid=(m // tm, n // tn, nk), in_specs=in_specs, out_specs=o_spec,
        out_shape=jax.ShapeDtypeStruct((m, n), out_dtype),
        scratch_shapes=[pltpu.VMEM((tm, tn), F32)] if nk > 1 else [],
        compiler_params=_cp(("parallel", "parallel", "arbitrary")),
    )(*ins)


def _mmx(name, grid, a, b, out, contract, *, alpha=None, res=None, into=None):
    nk = grid[-1]
    has_res, has_into = res is not None, into is not None
    n_in = 2 + has_res + has_into

    def finish(acc, res_ref, o_ref):
        if alpha is not None:
            acc = acc * alpha
        if has_res:
            acc = acc + res_ref[...].astype(F32)
        o_ref[...] = acc.astype(o_ref.dtype)

    def body(*refs):
        res_ref = refs[2] if has_res else None
        o_ref = refs[n_in]
        part = _dg(refs[0][...], refs[1][...], *contract)
        if nk == 1:
            finish(part, res_ref, o_ref)
            return
        acc_ref = refs[-1]
        kk = pl.program_id(len(grid) - 1)

        @pl.when(kk == 0)
        def _():
            acc_ref[...] = part

        @pl.when(kk > 0)
        def _():
            acc_ref[...] += part

        @pl.when(kk == nk - 1)
        def _():
            finish(acc_ref[...], res_ref, o_ref)

    operands = [a, b] + ([res] if has_res else [])
    in_specs = [pl.BlockSpec(blk, im) for _, blk, im in operands] + ([ANY] if has_into else [])
    acc_shape = tuple(d for d in out[2] if d is not None)
    return pl.pallas_call(
        body, name=name, grid=grid, in_specs=in_specs, out_specs=pl.BlockSpec(out[2], out[3]),
        out_shape=jax.ShapeDtypeStruct(out[0], out[1]),
        scratch_shapes=[pltpu.VMEM(acc_shape, F32)] if nk > 1 else [],
        input_output_aliases={n_in - 1: 0} if has_into else {},
        compiler_params=_cp(("parallel",) * (len(grid) - 1) + ("arbitrary",)),
    )(*[o[0] for o in operands], *([into] if has_into else []))


def _ew(name, fn, grid, ins, outs, scratch=()):
    n_in, n_out = len(ins), len(outs)

    def body(*refs):
        vals = [r[...] for r in refs[:n_in]]
        res = fn(*vals, *refs[n_in + n_out:])
        for r, v in zip(refs[n_in:n_in + n_out], res):
            r[...] = v.astype(r.dtype)

    res = pl.pallas_call(
        body, name=name, grid=grid,
        in_specs=[pl.BlockSpec(b, m) for _, b, m in ins],
        out_specs=[pl.BlockSpec(b, m) for _, _, b, m in outs],
        out_shape=[jax.ShapeDtypeStruct(s, d) for s, d, _, _ in outs],
        scratch_shapes=list(scratch),
        compiler_params=_cp(("arbitrary",) * len(grid)),
    )(*[a for a, _, _ in ins])
    return res


def _ew_bwd(name, fn, grid, ins, cts, wrt, adds=(), ct_fn=None):
    n_in, n_ct, n_add = len(ins), len(cts), len(adds)
    idxs = [w["idx"] for w in wrt]
    intos = [(k, w["into"]) for k, w in enumerate(wrt) if w.get("into") is not None]

    def body(*refs):
        prim = [r[...] for r in refs[:n_in]]
        ct = [r[...].astype(F32) for r in refs[n_in:n_in + n_ct]]
        addv = [r[...] for r in refs[n_in + n_ct:n_in + n_ct + n_add]]
        orefs = refs[n_in + n_ct + n_add + len(intos):]

        def f(*sel):
            full = list(prim)
            for i, s in zip(idxs, sel):
                full[i] = s
            return fn(*full)

        _, vjp = jax.vjp(f, *[prim[i].astype(F32) for i in idxs])
        grads = vjp(tuple(ct) if ct_fn is None else ct_fn(*ct))
        for w, g, r in zip(wrt, grads, orefs):
            if w["kind"] == "tile":
                if w.get("add") is not None:
                    g = g + addv[w["add"]].astype(F32)
                r[...] = g.astype(r.dtype)
            else:
                first = w["first"]()

                @pl.when(first)
                def _(r=r, g=g):
                    r[...] = g.astype(r.dtype)

                @pl.when(jnp.logical_not(first))
                def _(r=r, g=g):
                    r[...] += g.astype(r.dtype)

    allin = list(ins) + list(cts) + list(adds)
    return pl.pallas_call(
        body, name=name, grid=grid,
        in_specs=[pl.BlockSpec(b, m) for _, b, m in allin] + [ANY] * len(intos),
        out_specs=[pl.BlockSpec(w["block"], w["imap"]) for w in wrt],
        out_shape=[jax.ShapeDtypeStruct(w["shape"], w["dtype"]) for w in wrt],
        input_output_aliases={len(allin) + q: k for q, (k, _) in enumerate(intos)},
        compiler_params=_cp(("arbitrary",) * len(grid)),
    )(*[a for a, _, _ in allin], *[a for _, a in intos])


def _rmsnorm_fn(x, w):
    return (x * lax.rsqrt(jnp.mean(x * x, axis=-1, keepdims=True) + EPS) * w,)


def _swiglu_fn(g, u):
    return (_silu(g) * u,)


def _qkprep_fn(t, w64, cos, sin, hmean):
    w = jnp.sum(dup64(jnp.broadcast_to(w64, (8, LANES))), axis=0, keepdims=True) * 0.125
    y = t * lax.rsqrt(dot2(t * t, hmean) + EPS) * w
    return (y * cos + rope_rot(y) * sin,)


def _att_fn(q, kp, kc, vp, vc, first):
    iq = lax.broadcasted_iota(jnp.int32, (BAND, 2 * BAND), 0)
    ik = lax.broadcasted_iota(jnp.int32, (BAND, 2 * BAND), 1)
    rel = BAND + iq - ik
    ok = (rel >= 0) & (rel <= BAND) & ((ik >= BAND) | jnp.logical_not(first))
    lane = lax.broadcasted_iota(jnp.int32, (1, LANES), 1)
    kcat = jnp.concatenate([kp, kc], axis=0)
    vcat = jnp.concatenate([vp, vc], axis=0)
    o_pair = jnp.zeros((BAND, LANES), F32)
    l_pair = jnp.zeros((BAND, LANES), F32)
    for hh in range(2):
        lm = (lane // 64 == hh).astype(F32)
        s = dot_nt(q * lm, kcat) * 0.125
        s = jnp.where(ok, s, NEG)
        mx = jnp.max(s, axis=-1, keepdims=True)
        e = jnp.exp(s - mx)
        den = jnp.sum(e, axis=-1, keepdims=True)
        o_pair = o_pair + dot_nn(e / den, vcat) * lm
        l_pair = l_pair + (mx + jnp.log(den)) * lm
    return o_pair, l_pair


def _attmix_fn(o0, o1, o2, l0, l1, l2):
    m = jnp.maximum(jnp.maximum(l0, l1), l2)
    e0, e1, e2 = jnp.exp(l0 - m), jnp.exp(l1 - m), jnp.exp(l2 - m)
    return ((e0 * o0 + e1 * o1 + e2 * o2) / (e0 + e1 + e2),)


def _conv_fn(x, w0, w1, w2, w3, b):
    pre = x * w3 + shift_rows(x, 1) * w2 + shift_rows(x, 2) * w1 + shift_rows(x, 3) * w0 + b
    return (_silu(pre),)


def _ssdpre_fn(dtraw, bias, alog, ex):
    dt = _softplus(dtraw + bias)
    da = dt * (-jnp.exp(alog))
    return dot2(dt, ex), dot2(da, ex)


def _ssd_step(st, x, dtb, dab, bm, cm, ltri):
    cum = tri_matmul(ltri, dab)
    cum_t = cum.T
    xdt = x * dtb
    cb = dot_nt(cm, bm)
    ri = lax.broadcasted_iota(jnp.int32, (CHUNK, CHUNK), 0)
    ci = lax.broadcasted_iota(jnp.int32, (CHUNK, CHUNK), 1)
    causal = ri >= ci
    lane = lax.broadcasted_iota(jnp.int32, (1, LANES), 1)
    rowi = lax.broadcasted_iota(jnp.int32, (LANES, 1), 0)
    ys = []
    for p in range(4):
        sl = slice(p * LANES, (p + 1) * LANES)
        cum_p, cum_tp, xdt_p = cum[:, sl], cum_t[sl, :], xdt[:, sl]
        acc = jnp.zeros((CHUNK, LANES), F32)
        for hh in range(2):
            col = jnp.sum(cum_p * (lane == 64 * hh).astype(F32), axis=1, keepdims=True)
            row = jnp.sum(cum_tp * (rowi == 64 * hh).astype(F32), axis=0, keepdims=True)
            dec = jnp.exp(jnp.where(causal, col - row, NEG))
            acc = acc + dot_nn(cb * dec, xdt_p * (lane // 64 == hh).astype(F32))
        ys.append(acc)
    y_diag = jnp.concatenate(ys, axis=1)
    y_off = dot_nn(cm, st) * jnp.exp(cum)
    last_row = (lax.broadcasted_iota(jnp.int32, (CHUNK, 1), 0) == CHUNK - 1).astype(F32)
    last = jnp.sum(cum * last_row, axis=0, keepdims=True)
    new_st = st * jnp.exp(last) + dot_tn(bm, xdt * jnp.exp(last - cum))
    return new_st, y_diag + y_off


def _ssdpost_fn(y, xs, z, dskip, ex, nw):
    db = jnp.sum(dot2(jnp.broadcast_to(dskip, (8, LANES)), ex), axis=0, keepdims=True) * 0.125
    y2 = (y + db * xs) * _silu(z)
    return (y2 * lax.rsqrt(jnp.mean(y2 * y2, axis=-1, keepdims=True) + EPS) * nw,)


def _merge_fn(ya, ys, ga, gs, ba, bs):
    return (_sigmoid(ga + ba) * ya + _sigmoid(gs + bs) * ys,)


TM = 512


def _full(shape):
    nd = len(shape)
    return (shape, lambda *_: (0,) * nd)


def _rmsnorm(name, x, w):
    t = x.shape[0]
    return _ew(name, _rmsnorm_fn, (t // TM,),
               [(x, (TM, D_MODEL), lambda i: (i, 0)), (w, (1, D_MODEL), lambda i: (0, 0))],
               [((t, D_MODEL), BF16, (TM, D_MODEL), lambda i: (i, 0))])[0]


def _rmsnorm_bwd(name, x, w, dh, dres):
    t = x.shape[0]
    row = ((TM, D_MODEL), lambda i: (i, 0))
    return _ew_bwd(name, _rmsnorm_fn, (t // TM,),
                   [(x, *row), (w, (1, D_MODEL), lambda i: (0, 0))], [(dh, *row)],
                   [dict(idx=0, kind="tile", shape=(t, D_MODEL), dtype=F32, block=row[0], imap=row[1], add=0),
                    dict(idx=1, kind="acc", shape=(1, D_MODEL), dtype=F32, block=(1, D_MODEL), imap=lambda i: (0, 0),
                         first=lambda: pl.program_id(0) == 0)],
                   adds=[(dres, *row)])


def _qk_operands(proj, qkw, cos, sin, consts, tm):
    nrow = SEQ // tm
    c = ((LANES, LANES), lambda j, i: (0, 0))
    return [(proj, (tm, LANES), lambda j, i: (i, j)),
            (qkw, (None, 1, LANES), lambda j, i: (j // 12, 0, 0)),
            (cos, (tm, LANES), lambda j, i: (i % nrow, 0)),
            (sin, (tm, LANES), lambda j, i: (i % nrow, 0)),
            (consts["hmean"], *c)]


def _qkprep(name, proj, qkw, cos, sin, consts):
    t = proj.shape[0]
    return _ew(name, _qkprep_fn, (2 * QKV // LANES, t // TM), _qk_operands(proj, qkw, cos, sin, consts, TM),
               [((t, 2 * QKV), F32, (TM, LANES), lambda j, i: (i, j))])[0]


def _qkprep_bwd(name, proj, qkw, cos, sin, consts, dq, dk, dproj):
    t = proj.shape[0]
    nq = QKV // LANES

    def pick(cq, ck):
        return (jnp.where(pl.program_id(0) < nq, cq, ck),)

    return _ew_bwd(name, _qkprep_fn, (2 * nq, t // TM), _qk_operands(proj, qkw, cos, sin, consts, TM),
                   [(d, (TM, LANES), lambda j, i: (i, j % nq)) for d in (dq, dk)],
                   [dict(idx=0, kind="tile", shape=dproj.shape, dtype=dproj.dtype, block=(TM, LANES),
                         imap=lambda j, i: (i, j), into=dproj),
                    dict(idx=1, kind="acc", shape=(2, 1, LANES), dtype=F32, block=(None, 1, LANES),
                         imap=lambda j, i: (j // 12, 0, 0),
                         first=lambda: (pl.program_id(0) % 12 == 0) & (pl.program_id(1) == 0))],
                   ct_fn=pick)


def _att_specs(dil, g):
    nb = SEQ // dil // BAND
    pt = 4 if dil == 1 else 1
    w = pt * LANES
    blk = (None, BAND * dil, w)
    kq, kk, kv = g * ATT_OUT // w, (QKV + g * ATT_OUT) // w, (V0 + g * ATT_OUT) // w

    def cur(n):
        return jnp.minimum(n, nb - 1)

    def prev(n):
        return jnp.maximum(jnp.minimum(n, nb - 1) - 1, 0)

    return nb, pt, blk, [
        pl.BlockSpec(blk, lambda b, p, n: (b, cur(n), kq + p)),
        pl.BlockSpec(blk, lambda b, p, n: (b, prev(n), kk + p)),
        pl.BlockSpec(blk, lambda b, p, n: (b, cur(n), kk + p)),
        pl.BlockSpec(blk, lambda b, p, n: (b, prev(n), kv + p)),
        pl.BlockSpec(blk, lambda b, p, n: (b, cur(n), kv + p)),
    ]


def _att_rows(r, dil):
    return pl.ds(r, BAND, stride=dil) if dil > 1 else pl.ds(0, BAND)


def _att_fwd(name, qk, proj, g):
    bl = qk.shape[0] // SEQ
    dil = ATT_DILATIONS[g]
    nb, pt, blk, specs = _att_specs(dil, g)
    qk3 = qk.reshape(bl, SEQ, 2 * QKV)
    proj3 = proj.reshape(bl, SEQ, NP)
    o_spec = pl.BlockSpec(blk, lambda b, p, n: (b, n, p))

    def body(q, kp, kc, vp, vc, o_ref, l_ref):
        first = pl.program_id(2) == 0

        def residue(r, carry):
            sl = _att_rows(r, dil)
            for p in range(pt):
                ln = pl.ds(p * LANES, LANES)
                o, l = _att_fn(q[sl, ln], kp[sl, ln], kc[sl, ln], vp[sl, ln], vc[sl, ln], first)
                o_ref[sl, ln] = o
                l_ref[sl, ln] = l
            return carry

        lax.fori_loop(0, dil, residue, 0)

    o, l = pl.pallas_call(
        body, name=name, grid=(bl, ATT_OUT // (pt * LANES), nb), in_specs=specs, out_specs=[o_spec, o_spec],
        out_shape=[jax.ShapeDtypeStruct((bl, SEQ, ATT_OUT), F32)] * 2,
        compiler_params=_cp(("arbitrary",) * 3),
    )(qk3, qk3, qk3, proj3, proj3)
    return o.reshape(bl * SEQ, ATT_OUT), l.reshape(bl * SEQ, ATT_OUT)


def _att_bwd(name, qk, proj, g, do, dl, dq_buf, dk_buf, dv_buf):
    bl = qk.shape[0] // SEQ
    dil = ATT_DILATIONS[g]
    nb, pt, blk, specs = _att_specs(dil, g)
    w = pt * LANES
    qk3 = qk.reshape(bl, SEQ, 2 * QKV)
    proj3 = proj.reshape(bl, SEQ, NP)
    ct_spec = pl.BlockSpec(blk, lambda b, p, n: (b, jnp.minimum(n, nb - 1), p))
    do3 = do.reshape(bl, SEQ, ATT_OUT)
    dl3 = dl.reshape(bl, SEQ, ATT_OUT)
    kg = g * ATT_OUT // w

    def body(q, kp, kc, vp, vc, do_ref, dl_ref, _a, _b, _c, d_ref, dk_ref, dv_ref, ck, cv):
        n = pl.program_id(2)

        def residue(r, carry):
            sl = _att_rows(r, dil)
            for p in range(pt):
                ln = pl.ds(p * LANES, LANES)

                @pl.when(n < nb)
                def _(ln=ln):
                    first = n == 0
                    prim = [ref[sl, ln] for ref in (q, kp, kc, vp, vc)]
                    _, vjp = jax.vjp(lambda *a: _att_fn(*a, first), *prim)
                    dq, dkp, dkc, dvp, dvc = vjp((do_ref[sl, ln], dl_ref[sl, ln]))
                    d_ref[sl, ln] = dq

                    @pl.when(n > 0)
                    def _():
                        dk_ref[sl, ln] = ck[sl, ln] + dkp
                        dv_ref[sl, ln] = cv[sl, ln] + dvp

                    ck[sl, ln] = dkc
                    cv[sl, ln] = dvc

                @pl.when(n == nb)
                def _(ln=ln):
                    dk_ref[sl, ln] = ck[sl, ln]
                    dv_ref[sl, ln] = cv[sl, ln]

            return carry

        lax.fori_loop(0, dil, residue, 0)

    bufs = [a.reshape(bl, SEQ, QKV) for a in (dq_buf, dk_buf, dv_buf)]
    o_specs = [
        pl.BlockSpec(blk, lambda b, p, n: (b, jnp.minimum(n, nb - 1), kg + p)),
        pl.BlockSpec(blk, lambda b, p, n: (b, jnp.maximum(n - 1, 0), kg + p)),
        pl.BlockSpec(blk, lambda b, p, n: (b, jnp.maximum(n - 1, 0), kg + p)),
    ]
    dq, dk, dv = pl.pallas_call(
        body, name=name, grid=(bl, ATT_OUT // w, nb + 1), in_specs=specs + [ct_spec, ct_spec, ANY, ANY, ANY],
        out_specs=o_specs, out_shape=[jax.ShapeDtypeStruct(a.shape, a.dtype) for a in bufs],
        input_output_aliases={7: 0, 8: 1, 9: 2},
        scratch_shapes=[pltpu.VMEM((BAND * dil, w), F32), pltpu.VMEM((BAND * dil, w), F32)],
        compiler_params=_cp(("arbitrary",) * 3),
    )(qk3, qk3, qk3, proj3, proj3, do3, dl3, *bufs)
    return dq.reshape(dq_buf.shape), dk.reshape(dk_buf.shape), dv.reshape(dv_buf.shape)


def _attmix(name, os_, ls_):
    t = os_[0].shape[0]
    blk = ((TM, ATT_OUT), lambda i: (i, 0))
    return _ew(name, _attmix_fn, (t // TM,), [(a, *blk) for a in (*os_, *ls_)], [((t, ATT_OUT), BF16, *blk)])[0]


def _attmix_bwd(name, os_, ls_, datt):
    t = os_[0].shape[0]
    blk = ((TM, ATT_OUT), lambda i: (i, 0))
    return _ew_bwd(name, _attmix_fn, (t // TM,), [(a, *blk) for a in (*os_, *ls_)], [(datt, *blk)],
                   [dict(idx=k, kind="tile", shape=(t, ATT_OUT), dtype=F32, block=blk[0], imap=blk[1]) for k in range(6)])


CONV_TC = 256


def _conv_operands(proj3, conv_w, conv_b):
    c0 = X0 // CONV_TC
    ins = [(proj3, (None, SEQ, CONV_TC), lambda j, b: (b, 0, c0 + j))]
    for k in range(4):
        ins.append((conv_w, (None, 1, CONV_TC), lambda j, b, k=k: (k, 0, j)))
    ins.append((conv_b, (1, CONV_TC), lambda j, b: (0, j)))
    return ins


def _conv(name, proj3, conv_w, conv_b):
    bl = proj3.shape[0]
    return _ew(name, _conv_fn, (XBC // CONV_TC, bl), _conv_operands(proj3, conv_w, conv_b),
               [((bl, SEQ, XBC), F32, (None, SEQ, CONV_TC), lambda j, b: (b, 0, j))])[0]


def _conv_bwd(name, proj3, conv_w, conv_b, dxs3, db3, dc3, dproj3):
    bl = proj3.shape[0]
    nx = D_INNER // CONV_TC
    nb_ = N_SSM_GROUPS * D_STATE // CONV_TC
    blk = (None, SEQ, CONV_TC)
    cts = [(dxs3, blk, lambda j, b: (b, 0, jnp.minimum(j, nx - 1))),
           (db3, blk, lambda j, b: (b, 0, jnp.clip(j - nx, 0, nb_ - 1))),
           (dc3, blk, lambda j, b: (b, 0, jnp.clip(j - nx - nb_, 0, nb_ - 1)))]

    def pick(cx, cb, cc):
        j = pl.program_id(0)
        return (jnp.where(j < nx, cx, jnp.where(j < nx + nb_, cb, cc)),)

    first = lambda: pl.program_id(1) == 0
    wrt = [dict(idx=0, kind="tile", shape=dproj3.shape, dtype=dproj3.dtype, block=blk,
                imap=lambda j, b: (b, 0, X0 // CONV_TC + j), into=dproj3)]
    for k in range(4):
        wrt.append(dict(idx=1 + k, kind="acc", shape=(1, XBC), dtype=F32, block=(1, CONV_TC),
                        imap=lambda j, b: (0, j), first=first))
    wrt.append(dict(idx=5, kind="acc", shape=(1, XBC), dtype=F32, block=(1, CONV_TC), imap=lambda j, b: (0, j), first=first))
    return _ew_bwd(name, _conv_fn, (XBC // CONV_TC, bl), _conv_operands(proj3, conv_w, conv_b), cts, wrt, ct_fn=pick)


SSD_TM = 256


def _ssdpre_operands(proj, dt_bias, a_log, ex):
    return [(proj, (SSD_TM, DTW), lambda i: (i, DT0 // DTW)), (dt_bias, *_full((1, DTW))), (a_log, *_full((1, DTW))),
            (ex, *_full((LANES, D_INNER)))]


def _ssdpre(name, proj, dt_bias, a_log, ex):
    t = proj.shape[0]
    blk = ((SSD_TM, D_INNER), lambda i: (i, 0))
    return _ew(name, _ssdpre_fn, (t // SSD_TM,), _ssdpre_operands(proj, dt_bias, a_log, ex),
               [((t, D_INNER), F32, *blk), ((t, D_INNER), F32, *blk)])


def _ssdpre_bwd(name, proj, dt_bias, a_log, ex, ddtb, ddab):
    t = proj.shape[0]
    blk = ((SSD_TM, D_INNER), lambda i: (i, 0))
    first = lambda: pl.program_id(0) == 0
    return _ew_bwd(name, _ssdpre_fn, (t // SSD_TM,), _ssdpre_operands(proj, dt_bias, a_log, ex),
                   [(ddtb, *blk), (ddab, *blk)],
                   [dict(idx=0, kind="tile", shape=(t, DTW), dtype=BF16, block=(SSD_TM, DTW), imap=lambda i: (i, 0)),
                    dict(idx=1, kind="acc", shape=(1, DTW), dtype=F32, block=(1, DTW), imap=lambda i: (0, 0), first=first),
                    dict(idx=2, kind="acc", shape=(1, DTW), dtype=F32, block=(1, DTW), imap=lambda i: (0, 0), first=first)])


def _ssd_in_specs(rev):
    nc = SEQ // CHUNK

    def c_(c):
        return nc - 1 - c if rev else c

    wide = (None, CHUNK, 4 * LANES)
    nar = (None, CHUNK, D_STATE)
    xb = D_INNER // D_STATE
    return [
        pl.BlockSpec(wide, lambda b, g, c: (b, c_(c), g)),
        pl.BlockSpec(wide, lambda b, g, c: (b, c_(c), g)),
        pl.BlockSpec(wide, lambda b, g, c: (b, c_(c), g)),
        pl.BlockSpec(nar, lambda b, g, c: (b, c_(c), xb + g)),
        pl.BlockSpec(nar, lambda b, g, c: (b, c_(c), xb + N_SSM_GROUPS + g)),
        pl.BlockSpec((CHUNK, CHUNK), lambda b, g, c: (0, 0)),
    ], c_


def _ssd_fwd(name, xc3, dtb3, dab3, ltri):
    bl = xc3.shape[0]
    nc = SEQ // CHUNK
    specs, _ = _ssd_in_specs(False)

    def body(x, dtb, dab, bm, cm, lt, y_ref, st_ref, st):
        @pl.when(pl.program_id(2) == 0)
        def _():
            st[...] = jnp.zeros_like(st)

        s0 = st[...]
        st_ref[...] = s0
        new_st, y = _ssd_step(s0, x[...], dtb[...], dab[...], bm[...], cm[...], lt[...])
        y_ref[...] = y
        st[...] = new_st

    return pl.pallas_call(
        body, name=name, grid=(bl, N_SSM_GROUPS, nc), in_specs=specs,
        out_specs=[pl.BlockSpec((None, CHUNK, 4 * LANES), lambda b, g, c: (b, c, g)),
                   pl.BlockSpec((None, None, None, D_STATE, 4 * LANES), lambda b, g, c: (b, g, c, 0, 0))],
        out_shape=[jax.ShapeDtypeStruct((bl, SEQ, D_INNER), F32),
                   jax.ShapeDtypeStruct((bl, N_SSM_GROUPS, nc, D_STATE, 4 * LANES), F32)],
        scratch_shapes=[pltpu.VMEM((D_STATE, 4 * LANES), F32)],
        compiler_params=_cp(("arbitrary",) * 3),
    )(xc3, dtb3, dab3, xc3, xc3, ltri)


def _ssd_bwd(name, xc3, dtb3, dab3, ltri, states, dy3, dxs_part3):
    bl = xc3.shape[0]
    nc = SEQ // CHUNK
    specs, c_ = _ssd_in_specs(True)
    wide = pl.BlockSpec((None, CHUNK, 4 * LANES), lambda b, g, c: (b, c_(c), g))
    nar = pl.BlockSpec((None, CHUNK, D_STATE), lambda b, g, c: (b, c_(c), g))
    st_spec = pl.BlockSpec((None, None, None, D_STATE, 4 * LANES), lambda b, g, c: (b, g, c_(c), 0, 0))

    def body(x, dtb, dab, bm, cm, lt, st_ref, dy, dxp, dx_ref, ddtb_ref, ddab_ref, dbm_ref, dcm_ref, dst):
        @pl.when(pl.program_id(2) == 0)
        def _():
            dst[...] = jnp.zeros_like(dst)

        ltv = lt[...]
        _, vjp = jax.vjp(lambda *a: _ssd_step(*a, ltv), st_ref[...], x[...], dtb[...], dab[...], bm[...], cm[...])
        d_st, d_x, d_dtb, d_dab, d_bm, d_cm = vjp((dst[...], dy[...]))
        dst[...] = d_st
        dx_ref[...] = d_x + dxp[...]
        ddtb_ref[...] = d_dtb
        ddab_ref[...] = d_dab
        dbm_ref[...] = d_bm
        dcm_ref[...] = d_cm

    big = jax.ShapeDtypeStruct((bl, SEQ, D_INNER), F32)
    small = jax.ShapeDtypeStruct((bl, SEQ, N_SSM_GROUPS * D_STATE), F32)
    return pl.pallas_call(
        body, name=name, grid=(bl, N_SSM_GROUPS, nc), in_specs=specs + [st_spec, wide, wide],
        out_specs=[wide, wide, wide, nar, nar], out_shape=[big, big, big, small, small],
        scratch_shapes=[pltpu.VMEM((D_STATE, 4 * LANES), F32)],
        compiler_params=_cp(("arbitrary",) * 3),
    )(xc3, dtb3, dab3, xc3, xc3, ltri, states, dy3, dxs_part3)


def _ssdpost_operands(y, xc, proj, d_skip, ex, nw):
    w = 4 * LANES
    return [(y, (SSD_TM, w), lambda j, i: (i, j)), (xc, (SSD_TM, w), lambda j, i: (i, j)),
            (proj, (SSD_TM, w), lambda j, i: (i, Z0 // w + j)), (d_skip, (1, DTW), lambda j, i: (0, 0)),
            (ex, (LANES, w), lambda j, i: (0, j)), (nw, (1, w), lambda j, i: (0, j))]


def _ssdpost(name, y, xc, proj, d_skip, ex, nw):
    t = y.shape[0]
    w = 4 * LANES
    return _ew(name, _ssdpost_fn, (D_INNER // w, t // SSD_TM), _ssdpost_operands(y, xc, proj, d_skip, ex, nw),
               [((t, D_INNER), BF16, (SSD_TM, w), lambda j, i: (i, j))])[0]


def _ssdpost_bwd(name, y, xc, proj, d_skip, ex, nw, dysn, dproj):
    t = y.shape[0]
    w = 4 * LANES
    blk = ((SSD_TM, w), lambda j, i: (i, j))
    return _ew_bwd(name, _ssdpost_fn, (D_INNER // w, t // SSD_TM), _ssdpost_operands(y, xc, proj, d_skip, ex, nw),
                   [(dysn, *blk)],
                   [dict(idx=0, kind="tile", shape=(t, D_INNER), dtype=F32, block=blk[0], imap=blk[1]),
                    dict(idx=1, kind="tile", shape=(t, D_INNER), dtype=F32, block=blk[0], imap=blk[1]),
                    dict(idx=2, kind="tile", shape=dproj.shape, dtype=dproj.dtype, block=blk[0],
                         imap=lambda j, i: (i, Z0 // w + j), into=dproj),
                    dict(idx=3, kind="acc", shape=(1, DTW), dtype=F32, block=(1, DTW), imap=lambda j, i: (0, 0),
                         first=lambda: (pl.program_id(0) == 0) & (pl.program_id(1) == 0)),
                    dict(idx=5, kind="acc", shape=(1, D_INNER), dtype=F32, block=(1, w), imap=lambda j, i: (0, j),
                         first=lambda: pl.program_id(1) == 0)])


def _merge_operands(ya, ys, proj, b_gates):
    w = 4 * LANES
    g0 = G0 // w
    nh = D_MODEL // w
    return [(ya, (TM, w), lambda j, i: (i, j)), (ys, (TM, w), lambda j, i: (i, j)),
            (proj, (TM, w), lambda j, i: (i, g0 + j)), (proj, (TM, w), lambda j, i: (i, g0 + nh + j)),
            (b_gates, (1, w), lambda j, i: (0, j)), (b_gates, (1, w), lambda j, i: (0, nh + j))]


def _merge(name, ya, ys, proj, b_gates):
    t = ya.shape[0]
    w = 4 * LANES
    return _ew(name, _merge_fn, (D_MODEL // w, t // TM), _merge_operands(ya, ys, proj, b_gates),
               [((t, D_MODEL), BF16, (TM, w), lambda j, i: (i, j))])[0]


def _merge_bwd(name, ya, ys, proj, b_gates, dmixed):
    t = ya.shape[0]
    w = 4 * LANES
    blk = ((TM, w), lambda j, i: (i, j))
    first = lambda: pl.program_id(1) == 0
    tile = lambda k, dt: dict(idx=k, kind="tile", shape=(t, D_MODEL), dtype=dt, block=blk[0], imap=blk[1])
    acc = lambda k: dict(idx=k, kind="acc", shape=(1, D_MODEL), dtype=F32, block=(1, w), imap=lambda j, i: (0, j), first=first)
    return _ew_bwd(name, _merge_fn, (D_MODEL // w, t // TM), _merge_operands(ya, ys, proj, b_gates), [(dmixed, *blk)],
                   [tile(0, BF16), tile(1, BF16), tile(2, BF16), tile(3, BF16), acc(4), acc(5)])


def _loss(name, y, tgt):
    t = y.shape[0]
    blk = pl.BlockSpec((TM, D_MODEL), lambda i: (i, 0))

    def body(y_ref, t_ref, dy_ref, l_ref):
        e = y_ref[...] - t_ref[...]
        dy_ref[...] = e * (1.0 / D_MODEL)
        part = jnp.sum(jnp.sum(e * e, axis=-1, keepdims=True), axis=0, keepdims=True) * (0.5 / D_MODEL)
        part = jnp.broadcast_to(part, (8, LANES))

        @pl.when(pl.program_id(0) == 0)
        def _():
            l_ref[...] = part

        @pl.when(pl.program_id(0) > 0)
        def _():
            l_ref[...] += part

    return pl.pallas_call(
        body, name=name, grid=(t // TM,), in_specs=[blk, blk],
        out_specs=[blk, pl.BlockSpec((8, LANES), lambda i: (0, 0))],
        out_shape=[jax.ShapeDtypeStruct((t, D_MODEL), F32), jax.ShapeDtypeStruct((8, LANES), F32)],
        compiler_params=_cp(("arbitrary",)),
    )(y, tgt)


def _adamw_fn(w, g, m, v):
    m2 = B1 * m + (1.0 - B1) * g
    v2 = B2 * v + (1.0 - B2) * (g * g)
    m_hat = m2 / (1.0 - B1 ** STEP)
    v_hat = v2 / (1.0 - B2 ** STEP)
    return -LR * (m_hat / (jnp.sqrt(v_hat) + ADAM_EPS) + WD * w), m2, v2


def _adamw(name, w, g, m, v):
    rows, cols = w.shape
    tm = rows
    for cand in (512, 256, 128, 64, 32, 16, 8):
        if rows % cand == 0 and cand * cols * 4 <= (1 << 21):
            tm = cand
            break
    blk = ((tm, cols), lambda i: (i, 0))
    return _ew(name, _adamw_fn, (rows // tm,), [(a, *blk) for a in (w, g, m, v)], [((rows, cols), F32, *blk)] * 3)


NCH = 4
TMM = 1024
TKK = 1024


def _ffn_fwd(tag, x, nw, wg, wu, wd, li):
    t, fc = x.shape[0], wg.shape[-1]
    h = _rmsnorm(tag + "_norm", x, nw)

    def up_body(h_ref, wg_ref, wu_ref, g_ref, u_ref, a_ref):
        hv = h_ref[...]
        g = _dg(hv, wg_ref[...], 1, 0).astype(BF16)
        u = _dg(hv, wu_ref[...], 1, 0).astype(BF16)
        g_ref[...] = g
        u_ref[...] = u
        a_ref[...] = _swiglu_fn(g.astype(F32), u.astype(F32))[0].astype(BF16)

    w_spec = pl.BlockSpec((None, None, D_MODEL, fc), lambda k, i: (k, li, 0, 0))
    o_spec = pl.BlockSpec((None, TMM, fc), lambda k, i: (k, i, 0))
    g, u, a = pl.pallas_call(
        up_body, name=tag + "_up_act", grid=(NCH, t // TMM),
        in_specs=[pl.BlockSpec((TMM, D_MODEL), lambda k, i: (i, 0)), w_spec, w_spec], out_specs=[o_spec] * 3,
        out_shape=[jax.ShapeDtypeStruct((NCH, t, fc), BF16)] * 3, compiler_params=_cp(("parallel", "parallel")),
    )(h, wg, wu)
    row = ((TMM, D_MODEL), lambda i, j, k: (i, 0))
    y = _mmx(tag + "_down", (t // TMM, 1, NCH),
             (a, (None, TMM, fc), lambda i, j, k: (k, i, 0)),
             (wd, (None, None, fc, D_MODEL), lambda i, j, k: (k, li, 0, 0)),
             ((t, D_MODEL), F32, *row), (1, 0), alpha=0.5, res=(x, *row))
    return y, (x, h, g, u, a)


def _ffn_bwd(tag, saved, nw, wg, wu, wd, li, dy, bufs, gl):
    x, h, g, u, a = saved
    t, fc = x.shape[0], wg.shape[-1]
    bg, bu, bd = bufs
    def dact_body(dy_ref, wd_ref, g_ref, u_ref, dg_ref, du_ref):
        da = _dg(dy_ref[...], wd_ref[...], 1, 1) * 0.5
        _, vjp = jax.vjp(_swiglu_fn, g_ref[...].astype(F32), u_ref[...].astype(F32))
        dg, du = vjp((da,))
        dg_ref[...] = dg.astype(BF16)
        du_ref[...] = du.astype(BF16)

    c_spec = pl.BlockSpec((None, TMM, fc), lambda k, i: (k, i, 0))
    dg, du = pl.pallas_call(
        dact_body, name=tag + "_down_dx_act", grid=(NCH, t // TMM),
        in_specs=[pl.BlockSpec((TMM, D_MODEL), lambda k, i: (i, 0)),
                  pl.BlockSpec((None, None, fc, D_MODEL), lambda k, i: (k, li, 0, 0)), c_spec, c_spec],
        out_specs=[c_spec] * 2, out_shape=[jax.ShapeDtypeStruct((NCH, t, fc), BF16)] * 2,
        compiler_params=_cp(("parallel", "parallel")),
    )(dy, wd, g, u)
    bd = _mmx(tag + "_down_dw", (NCH, 1, t // TKK),
              (a, (None, TKK, fc), lambda k, j, kk: (k, kk, 0)),
              (dy, (TKK, D_MODEL), lambda k, j, kk: (kk, 0)),
              (bd.shape, BF16, (None, None, fc, D_MODEL), lambda k, j, kk: (gl, k, 0, 0)), (0, 0), alpha=0.5, into=bd)
    def dw(name, d, buf):
        return _mmx(name, (NCH, 1, t // TKK),
                    (h, (TKK, D_MODEL), lambda k, i, kk: (kk, 0)),
                    (d, (None, TKK, fc), lambda k, i, kk: (k, kk, 0)),
                    (buf.shape, BF16, (None, None, D_MODEL, fc), lambda k, i, kk: (gl, k, 0, 0)), (0, 0), into=buf)

    bg, bu = dw(tag + "_gate_dw", dg, bg), dw(tag + "_up_dw", du, bu)
    row = ((TMM, D_MODEL), lambda i, j, k: (i, 0))

    def dx_(name, d, w, res):
        return _mmx(name, (t // TMM, 1, NCH),
                    (d, (None, TMM, fc), lambda i, j, k: (k, i, 0)),
                    (w, (None, None, D_MODEL, fc), lambda i, j, k: (k, li, 0, 0)),
                    ((t, D_MODEL), F32, *row), (1, 1), res=None if res is None else (res, *row))

    dh = dx_(tag + "_up_dx", du, wu, dx_(tag + "_gate_dx", dg, wg, None))
    dx, dnw = _rmsnorm_bwd(tag + "_norm_bwd", x, nw, dh, dy)
    return dx, dnw, (bg, bu, bd)


def _mixer_fwd(tag, x, p, c):
    t = x.shape[0]
    bl = t // SEQ
    h = _rmsnorm(tag + "_norm", x, p["mix_norm_w"])
    proj = _mm(tag + "_in", h, p["w_in"], "nn")
    qk = _qkprep(tag + "_qk", proj, p["qkw"], c["cos"], c["sin"], c)
    os_, ls_ = [], []
    for g in range(3):
        o, l = _att_fwd(f"{tag}_att{g}", qk, proj, g)
        os_.append(o)
        ls_.append(l)
    att = _attmix(tag + "_attmix", os_, ls_)
    li = p["layer"]
    wa, ws, wo = p["w_att_proj"], p["w_ssm_proj"], p["w_out"]
    ca, cs, co = wa.shape[-1], ws.shape[-2], wo.shape[-2]
    row = ((TMM, D_MODEL), lambda i, j, k: (i, 0))
    ya = _mmx(tag + "_attproj", (t // TMM, NCH, 1),
              (att, (TMM, ATT_OUT), lambda i, k, kk: (i, 0)),
              (wa, (None, None, ATT_OUT, ca), lambda i, k, kk: (k, li, 0, 0)),
              ((t, D_MODEL), F32, (TMM, ca), lambda i, k, kk: (i, k)), (1, 0))
    proj3 = proj.reshape(bl, SEQ, NP)
    xc3 = _conv(tag + "_conv", proj3, p["conv_w"], p["conv_b"])
    xc = xc3.reshape(t, XBC)
    dtb, dab = _ssdpre(tag + "_ssdpre", proj, p["dt_bias"], p["a_log"], c["ex"])
    dtb3, dab3 = dtb.reshape(bl, SEQ, D_INNER), dab.reshape(bl, SEQ, D_INNER)
    y3, states = _ssd_fwd(tag + "_ssd", xc3, dtb3, dab3, c["ltri"])
    y = y3.reshape(t, D_INNER)
    ysn = _ssdpost(tag + "_ssdpost", y, xc, proj, p["d_skip"], c["ex"], p["ssm_norm_w"])
    ys = _mmx(tag + "_ssmproj", (t // TMM, 1, NCH),
              (ysn, (TMM, cs), lambda i, j, k: (i, k)),
              (ws, (None, None, cs, D_MODEL), lambda i, j, k: (k, li, 0, 0)),
              ((t, D_MODEL), F32, *row), (1, 0))
    mixed = _merge(tag + "_merge", ya, ys, proj, p["b_gates"])
    out = _mmx(tag + "_out", (t // TMM, 1, NCH),
               (mixed, (TMM, co), lambda i, j, k: (i, k)),
               (wo, (None, None, co, D_MODEL), lambda i, j, k: (k, li, 0, 0)),
               ((t, D_MODEL), F32, *row), (1, 0), res=(x, *row))
    return out, (x, h, proj, qk, os_, ls_, att, ya, xc3, dtb3, dab3, states, y, ysn, ys, mixed)


def _mixer_bwd(tag, saved, p, c, dout, bufs):
    x, h, proj, qk, os_, ls_, att, ya, xc3, dtb3, dab3, states, y, ysn, ys, mixed = saved
    t = x.shape[0]
    bl = t // SEQ
    xc = xc3.reshape(t, XBC)
    proj3 = proj.reshape(bl, SEQ, NP)
    gr = {}
    li, gl = p["layer"], p["global_layer"]
    wa, ws, wo = p["w_att_proj"], p["w_ssm_proj"], p["w_out"]
    ca, cs, co = wa.shape[-1], ws.shape[-2], wo.shape[-2]
    b_att, b_ssm, b_out = bufs

    def chunk_dx(name, d, w, cw):
        return _mmx(name, (t // TMM, NCH, D_MODEL // TKK),
                    (d, (TMM, TKK), lambda i, k, kk: (i, kk)),
                    (w, (None, None, cw, TKK), lambda i, k, kk: (k, li, 0, kk)),
                    ((t, NCH * cw), F32, (TMM, cw), lambda i, k, kk: (i, k)), (1, 1))

    def full_dw(name, a_, d, buf):
        kdim = a_.shape[1]
        tm = min(kdim, 1024)
        return _mmx(name, (kdim // tm, 1, t // TKK),
                    (a_, (TKK, tm), lambda i, j, kk: (kk, i)),
                    (d, (TKK, D_MODEL), lambda i, j, kk: (kk, 0)),
                    (buf.shape, BF16, (None, tm, D_MODEL), lambda i, j, kk: (gl, i, 0)), (0, 0), into=buf)

    dmixed = chunk_dx(tag + "_out_dx", dout, wo, co)
    b_out = full_dw(tag + "_out_dw", mixed, dout, b_out)
    dya, dys, dga, dgs, dba, dbs = _merge_bwd(tag + "_merge_bwd", ya, ys, proj, p["b_gates"], dmixed)
    gr["b_gates"] = jnp.concatenate([dba, dbs], axis=1)
    datt = _mmx(tag + "_attproj_dx", (t // TMM, 1, NCH),
                (dya, (TMM, ca), lambda i, j, k: (i, k)),
                (wa, (None, None, ATT_OUT, ca), lambda i, j, k: (k, li, 0, 0)),
                ((t, ATT_OUT), F32, (TMM, ATT_OUT), lambda i, j, k: (i, 0)), (1, 1))
    b_att = _mmx(tag + "_attproj_dw", (NCH, 1, t // TKK),
                 (att, (TKK, ATT_OUT), lambda k, j, kk: (kk, 0)),
                 (dya, (TKK, ca), lambda k, j, kk: (kk, k)),
                 (b_att.shape, BF16, (None, None, ATT_OUT, ca), lambda k, j, kk: (gl, k, 0, 0)), (0, 0), into=b_att)
    dysn = chunk_dx(tag + "_ssmproj_dx", dys, ws, cs)
    b_ssm = full_dw(tag + "_ssmproj_dw", ysn, dys, b_ssm)
    gr["bufs"] = (b_att, b_ssm, b_out)
    dmix = _attmix_bwd(tag + "_attmix_bwd", os_, ls_, datt)
    dq = dk = dv = jnp.zeros((t, QKV), F32)
    for g in range(3):
        dq, dk, dv = _att_bwd(f"{tag}_att{g}_bwd", qk, proj, g, dmix[g], dmix[3 + g], dq, dk, dv)
    dproj = jnp.zeros((t, NP), BF16)
    dproj = lax.dynamic_update_slice(dproj, dv.astype(BF16), (0, V0))
    dproj = lax.dynamic_update_slice(dproj, dga, (0, G0))
    dproj = lax.dynamic_update_slice(dproj, dgs, (0, G0 + D_MODEL))
    dproj, gr["qkw"] = _qkprep_bwd(tag + "_qk_bwd", proj, p["qkw"], c["cos"], c["sin"], c, dq, dk, dproj)
    dy, dxs_part, dproj, gr["d_skip"], gr["ssm_norm_w"] = _ssdpost_bwd(
        tag + "_ssdpost_bwd", y, xc, proj, p["d_skip"], c["ex"], p["ssm_norm_w"], dysn, dproj)
    dxs3, ddtb3, ddab3, db3, dc3 = _ssd_bwd(
        tag + "_ssd_bwd", xc3, dtb3, dab3, c["ltri"], states, dy.reshape(bl, SEQ, D_INNER), dxs_part.reshape(bl, SEQ, D_INNER))
    ddt, gr["dt_bias"], gr["a_log"] = _ssdpre_bwd(
        tag + "_ssdpre_bwd", proj, p["dt_bias"], p["a_log"], c["ex"], ddtb3.reshape(t, D_INNER), ddab3.reshape(t, D_INNER))
    dproj = lax.dynamic_update_slice(dproj, ddt, (0, DT0))
    dproj3, dcw0, dcw1, dcw2, dcw3, gr["conv_b"] = _conv_bwd(
        tag + "_conv_bwd", proj3, p["conv_w"], p["conv_b"], dxs3, db3, dc3, dproj.reshape(bl, SEQ, NP))
    dproj = dproj3.reshape(t, NP)
    gr["conv_w"] = jnp.concatenate([dcw0, dcw1, dcw2, dcw3], axis=0)
    gr["w_in"] = _mm(tag + "_in_dw", h, dproj, "tn", out_dtype=BF16)
    dh = _mm(tag + "_in_dx", dproj, p["w_in"], "nt")
    dx, gr["mix_norm_w"] = _rmsnorm_bwd(tag + "_norm_bwd", x, p["mix_norm_w"], dh, dout)
    return dx, gr


def _constants():
    cos, sin = _rope_tables()
    return dict(cos=cos, sin=sin, hmean=_head_mean_mat(),
                ex=_head_expand_mat(), ltri=_ltri_mat())


ANY = pl.BlockSpec(memory_space=pl.ANY)


def _mesh_pos():
    return lax.axis_index("x"), lax.axis_index("y"), lax.axis_index("c")


def _other_chips(x, y):
    return [(1 - x, y), (x, 1 - y), (1 - x, 1 - y)]


def _gather_exchange(srcs, outs, send_sems, recv_sems):
    n = len(srcs)
    x, y, c = _mesh_pos()
    chips = _other_chips(x, y)

    def part(a, chip, hf):
        h = srcs[a].shape[1] // 2
        return outs[a].at[2 * chip[0] + chip[1], :, pl.ds(hf * h, h), :]

    def mine(a):
        h = srcs[a].shape[1] // 2
        return srcs[a].at[:, pl.ds(c * h, h), :]

    def copy(a, k, src_ref, dst_ref, to):
        return pltpu.make_async_remote_copy(src_ref=src_ref, dst_ref=dst_ref, send_sem=send_sems.at[6 * a + k],
                                            recv_sem=recv_sems.at[6 * a + k], device_id=to, device_id_type=MESH)

    first = [copy(a, j, mine(a), part(a, (x, y), c), (*chip, c)) for a in range(n) for j, chip in enumerate(chips)]
    for cp in first:
        cp.start()
    passed = []
    for a in range(n):
        for j, chip in enumerate(chips):
            copy(a, j, part(a, chip, c), part(a, chip, c), (x, y, c)).wait_recv()
            fw = copy(a, 3 + j, part(a, chip, c), part(a, chip, c), (x, y, 1 - c))
            fw.start()
            passed.append(fw)
    for a in range(n):
        for j, chip in enumerate(chips):
            copy(a, 3 + j, part(a, chip, 1 - c), part(a, chip, 1 - c), (x, y, c)).wait_recv()
    for cp in first + passed:
        cp.wait_send()


def _gather_inits(ws, chip_idx):
    return [lax.dynamic_update_slice(jnp.zeros((NCH, *w.shape), w.dtype), w[None], (chip_idx[0], 0, 0, 0)) for w in ws]


def _all_gather_weights(ws, chip_idx):
    n = len(ws)
    inits = _gather_inits(ws, chip_idx)

    def body(*refs):
        _gather_exchange(refs[:n], refs[2 * n:3 * n], refs[3 * n], refs[3 * n + 1])

    return pl.pallas_call(
        body, name="all_gather_weights", out_shape=[jax.ShapeDtypeStruct(i.shape, i.dtype) for i in inits],
        in_specs=[ANY] * (2 * n), out_specs=[ANY] * n, input_output_aliases={n + a: a for a in range(n)},
        scratch_shapes=[pltpu.SemaphoreType.DMA((6 * n,)), pltpu.SemaphoreType.DMA((6 * n,))],
    )(*ws, *inits)


def _all_gather_weights_beside(ws, chip_idx):
    n = len(ws)
    src_refs = [jax.new_ref(w, memory_space=pltpu.MemorySpace.HBM) for w in ws]
    out_refs = [jax.new_ref(i, memory_space=pltpu.MemorySpace.HBM) for i in _gather_inits(ws, chip_idx)]

    @pl.kernel(mesh=plsc.ScalarSubcoreMesh(axis_name="sequencer", num_cores=1), name="all_gather_weights_beside",
               scratch_types=(pltpu.SemaphoreType.DMA((6 * n,)), pltpu.SemaphoreType.DMA((6 * n,))),
               compiler_params=pltpu.CompilerParams(collective_id=1))
    def launch(send_sems, recv_sems):
        x, y, c = _mesh_pos()
        barrier = pltpu.get_barrier_semaphore()
        for peer in [(x, y, 1 - c)] + [(*chip, c) for chip in _other_chips(x, y)]:
            pl.semaphore_signal(barrier, inc=1, device_id=peer, device_id_type=MESH)
        pl.semaphore_wait(barrier, 4)
        _gather_exchange(src_refs, out_refs, send_sems, recv_sems)

    launch()
    return [r[...] for r in out_refs]


def _pair_exchange(name, gs):
    n = len(gs)

    def body(*refs):
        srcs, outs, send_sems, recv_sems = refs[:n], refs[n:2 * n], refs[2 * n], refs[2 * n + 1]
        x, y, c = _mesh_pos()
        cps = []
        for a in range(n):
            h = gs[a].shape[2] // 2
            cps.append(pltpu.make_async_remote_copy(
                src_ref=srcs[a].at[:, :, pl.ds((1 - c) * h, h), :], dst_ref=outs[a], send_sem=send_sems.at[a],
                recv_sem=recv_sems.at[a], device_id=(x, y, 1 - c), device_id_type=MESH))
        for cp in cps:
            cp.start()
        for cp in cps:
            cp.wait()

    return pl.pallas_call(
        body, name=name,
        out_shape=[jax.ShapeDtypeStruct((g.shape[0], g.shape[1], g.shape[2] // 2, g.shape[3]), g.dtype) for g in gs],
        in_specs=[ANY] * n, out_specs=[ANY] * n,
        scratch_shapes=[pltpu.SemaphoreType.DMA((n,)), pltpu.SemaphoreType.DMA((n,))],
    )(*gs)


def _chip_exchange_copies(srcs, outs, send_sems, recv_sems):
    x, y, c = _mesh_pos()
    cps = [pltpu.make_async_remote_copy(
        src_ref=srcs[a].at[:, 2 * chip[0] + chip[1]], dst_ref=outs[a].at[j], send_sem=send_sems.at[3 * a + j],
        recv_sem=recv_sems.at[3 * a + j], device_id=(*chip, c), device_id_type=MESH)
        for a in range(len(srcs)) for j, chip in enumerate(_other_chips(x, y))]
    for cp in cps:
        cp.start()
    for cp in cps:
        cp.wait()


def _chip_exchange_shapes(hs):
    return [jax.ShapeDtypeStruct((3, h.shape[0], h.shape[2], h.shape[3]), h.dtype) for h in hs]


def _chip_exchange(hs):
    n = len(hs)

    def body(*refs):
        _chip_exchange_copies(refs[:n], refs[n:2 * n], refs[2 * n], refs[2 * n + 1])

    return pl.pallas_call(
        body, name="grad_chip_exchange", out_shape=_chip_exchange_shapes(hs),
        in_specs=[ANY] * n, out_specs=[ANY] * n,
        scratch_shapes=[pltpu.SemaphoreType.DMA((3 * n,)), pltpu.SemaphoreType.DMA((3 * n,))],
    )(*hs)


def _chip_exchange_beside(hs):
    n = len(hs)
    src_refs = [jax.new_ref(h, memory_space=pltpu.MemorySpace.HBM) for h in hs]
    out_refs = [jax.empty_ref(s, memory_space=pltpu.MemorySpace.HBM) for s in _chip_exchange_shapes(hs)]

    @pl.kernel(mesh=plsc.ScalarSubcoreMesh(axis_name="sequencer", num_cores=1), name="grad_chip_exchange_beside",
               scratch_types=(pltpu.SemaphoreType.DMA((3 * n,)), pltpu.SemaphoreType.DMA((3 * n,))),
               compiler_params=pltpu.CompilerParams(collective_id=2))
    def launch(send_sems, recv_sems):
        x, y, c = _mesh_pos()
        barrier = pltpu.get_barrier_semaphore()
        for chip in _other_chips(x, y):
            pl.semaphore_signal(barrier, inc=1, device_id=(*chip, c), device_id_type=MESH)
        pl.semaphore_wait(barrier, 3)
        _chip_exchange_copies(src_refs, out_refs, send_sems, recv_sems)

    launch()
    return [r[...] for r in out_refs]


def _pair_share(rs):
    n = len(rs)

    def body(*refs):
        outs, send_sems, recv_sems = refs[n:2 * n], refs[2 * n], refs[2 * n + 1]
        x, y, c = _mesh_pos()
        cps = [pltpu.make_async_remote_copy(src_ref=outs[a].at[:, c], dst_ref=outs[a].at[:, c], send_sem=send_sems.at[a],
                                            recv_sem=recv_sems.at[a], device_id=(x, y, 1 - c), device_id_type=MESH)
               for a in range(n)]
        for cp in cps:
            cp.start()
        for a in range(n):
            pltpu.make_async_remote_copy(src_ref=outs[a].at[:, 1 - c], dst_ref=outs[a].at[:, 1 - c],
                                         send_sem=send_sems.at[a], recv_sem=recv_sems.at[a], device_id=(x, y, c),
                                         device_id_type=MESH).wait_recv()
        for cp in cps:
            cp.wait_send()

    return pl.pallas_call(
        body, name="grad_pair_share", out_shape=[jax.ShapeDtypeStruct(r.shape, r.dtype) for r in rs],
        in_specs=[ANY] * n, out_specs=[ANY] * n, input_output_aliases={a: a for a in range(n)},
        scratch_shapes=[pltpu.SemaphoreType.DMA((n,)), pltpu.SemaphoreType.DMA((n,))],
    )(*rs)


def _pair_sum(name, g, recv, c_idx):
    d, k, h, b = recv.shape
    g5 = g.reshape(d * k, 2, h, b)

    def body(c_ref, a_ref, b_ref, o_ref):
        o_ref[...] = (a_ref[...].astype(F32) + b_ref[...].astype(F32)).astype(o_ref.dtype)

    out = pl.pallas_call(
        body, name=name,
        grid_spec=pltpu.PrefetchScalarGridSpec(
            num_scalar_prefetch=1, grid=(d * k,),
            in_specs=[pl.BlockSpec((None, None, h, b), lambda i, c: (i, c[0], 0, 0)),
                      pl.BlockSpec((None, h, b), lambda i, c: (i, 0, 0))],
            out_specs=pl.BlockSpec((None, h, b), lambda i, c: (i, 0, 0))),
        out_shape=jax.ShapeDtypeStruct((d * k, h, b), BF16),
        compiler_params=_cp(("arbitrary",)),
    )(c_idx, g5, recv.reshape(d * k, h, b))
    return out.reshape(d, k, h, b)


def _chip_sum(name, ha, recv, chip_idx, c_idx, depth, l0, into=None):
    d, _, h, b = ha.shape

    def body(k_ref, c_ref, a_ref, r0, r1, r2, *rest):
        rest[-1][...] = ((a_ref[...].astype(F32) + r0[...].astype(F32)) + r1[...].astype(F32)) + r2[...].astype(F32)

    blk = (None, None, h, b)
    extra = [] if into is None else [into]
    return pl.pallas_call(
        body, name=name,
        grid_spec=pltpu.PrefetchScalarGridSpec(
            num_scalar_prefetch=2, grid=(d,),
            in_specs=[pl.BlockSpec(blk, lambda l, k, c: (l, k[0], 0, 0))] +
                     [pl.BlockSpec(blk, lambda l, k, c, j=j: (j, l, 0, 0)) for j in range(3)] + [ANY] * len(extra),
            out_specs=pl.BlockSpec(blk, lambda l, k, c: (l0 + l, c[0], 0, 0))),
        out_shape=jax.ShapeDtypeStruct((depth, 2, h, b), F32),
        input_output_aliases={} if into is None else {6: 0},
        compiler_params=_cp(("arbitrary",)),
    )(chip_idx, c_idx, ha, recv, recv, recv, *extra)


def _all_sum_small(name, vec):
    rows = vec.shape[0]

    def body(v_ref, o_ref, buf, send_sems, recv_sems):
        x, y, c = _mesh_pos()
        me, sibling = (x, y, c), (x, y, 1 - c)
        chips = _other_chips(x, y)

        def slot(p):
            return buf.at[4 * p[0] + 2 * p[1] + p[2]]

        def copy(k, block, to, src=None):
            return pltpu.make_async_remote_copy(src_ref=slot(block) if src is None else src, dst_ref=slot(block),
                                                send_sem=send_sems.at[k], recv_sem=recv_sems.at[k],
                                                device_id=to, device_id_type=MESH)

        first = [copy(0, me, sibling, src=v_ref)]
        first += [copy(1 + j, me, (*chip, c), src=v_ref) for j, chip in enumerate(chips)]
        for cp in first:
            cp.start()
        passed = [copy(4 + j, (*chip, c), sibling) for j, chip in enumerate(chips)]
        for j, chip in enumerate(chips):
            copy(1 + j, (*chip, c), me).wait_recv()
            passed[j].start()
        copy(0, sibling, me).wait_recv()
        for j, chip in enumerate(chips):
            copy(4 + j, (*chip, 1 - c), me).wait_recv()
        for cp in first + passed:
            cp.wait_send()
        slot(me)[...] = v_ref[...]
        acc = buf[0]
        for k in range(1, 8):
            acc = acc + buf[k]
        o_ref[...] = acc

    vm = pl.BlockSpec(memory_space=pltpu.VMEM)
    return pl.pallas_call(
        body, name=name, out_shape=jax.ShapeDtypeStruct((rows, LANES), F32),
        in_specs=[vm], out_specs=vm, compiler_params=pltpu.CompilerParams(vmem_limit_bytes=VMEM_LIMIT),
        scratch_shapes=[pltpu.VMEM((8, rows, LANES), F32), pltpu.SemaphoreType.DMA((7,)), pltpu.SemaphoreType.DMA((7,))],
    )(vec)


def _pad_lanes(v, n=LANES):
    return jnp.pad(v, (0, n - v.shape[0]))[None, :]


def _w_in_to_kernel(w):
    return jnp.concatenate([w[:, :6656], w[:, 9760:N_IN], w[:, 6656:9728], w[:, 9728:9760],
                            jnp.zeros((w.shape[0], NP - N_IN), w.dtype)], axis=1)


def _w_in_from_kernel(w):
    return jnp.concatenate([w[:, :6656], w[:, X0:DT0], w[:, DT0:DT0 + 32], w[:, G0:X0]], axis=1)


def _layer_params(big, small, i):
    p = {k: big[k][i][0] for k in GRAD_BUFS}
    li = big["w_in"][i][1]
    p["layer"], p["global_layer"] = li, i
    w_in = big["w_in"][i][0]
    cw = w_in.shape[-1]
    pieces = []
    for lo, hi in ((0, 6656), (9760, N_IN), (6656, 9728), (9728, 9760)):
        for k in range(NCH):
            a, b = max(lo, k * cw), min(hi, (k + 1) * cw)
            if a < b:
                pieces.append(w_in[k, li, :, a - k * cw:b - k * cw])
    p["w_in"] = jnp.concatenate(pieces + [jnp.zeros((D_MODEL, NP - N_IN), w_in.dtype)], axis=1)
    p["conv_w"] = big["conv_w"][i][:, None, :]
    for k in ("ffn1_norm_w", "mix_norm_w", "ffn2_norm_w", "b_gates", "conv_b", "ssm_norm_w"):
        p[k] = small[k][i][None, :]
    for k in ("dt_bias", "a_log", "d_skip"):
        p[k] = _pad_lanes(small[k][i])
    p["qkw"] = jnp.stack([_pad_lanes(small["q_norm_w"][i]), _pad_lanes(small["k_norm_w"][i])])
    return p


GRAD_BUFS = ("ffn1_w_gate", "ffn1_w_up", "ffn1_w_down", "w_att_proj", "w_ssm_proj", "w_out",
             "ffn2_w_gate", "ffn2_w_up", "ffn2_w_down")


def _local_step(x, tgt, layers, c, exchange_rest=None):
    depth = len(layers)
    ffn = {f: tuple(f + s for s in ("_w_gate", "_w_up", "_w_down")) for f in ("ffn1", "ffn2")}
    saved = []
    for i, p in enumerate(layers):
        x, s1 = _ffn_fwd(f"L{i}_ffn1", x, p["ffn1_norm_w"], *[p[n] for n in ffn["ffn1"]], p["layer"])
        x, s2 = _mixer_fwd(f"L{i}_mix", x, p, c)
        x, s3 = _ffn_fwd(f"L{i}_ffn2", x, p["ffn2_norm_w"], *[p[n] for n in ffn["ffn2"]], p["layer"])
        saved.append((s1, s2, s3))
    dx, loss_blk = _loss("loss", x, tgt)

    def new_bufs(nl):
        out = {}
        for n in GRAD_BUFS:
            a, b = layers[0][n].shape[-2:]
            out[n] = jnp.zeros((nl, NCH * a, b) if n in ("w_ssm_proj", "w_out") else (nl, NCH, a, b), BF16)
        return out

    def finished(buf):
        out = dict(buf)
        for n in ("w_ssm_proj", "w_out"):
            a, b = layers[0][n].shape[-2:]
            out[n] = buf[n].reshape(-1, NCH, a, b)
        return out

    bufs = [new_bufs(1), new_bufs(depth - 1) if depth > 1 else None]
    grads = [None] * depth
    rest_out = None
    for i in reversed(range(depth)):
        p = layers[i]
        s1, s2, s3 = saved[i]
        buf, gl = (bufs[0], 0) if i == 0 else (bufs[1], i - 1)
        p = dict(p, global_layer=gl)

        def ffn_bwd(f, s, dy):
            names = ffn[f]
            d, dn, new = _ffn_bwd(f"L{i}_{f}", s, p[f + "_norm_w"], *[p[n] for n in names], p["layer"], dy,
                                  tuple(buf[n] for n in names), gl)
            buf.update(zip(names, new))
            return d, dn

        dx, dn2 = ffn_bwd("ffn2", s3, dx)
        dx, gr = _mixer_bwd(f"L{i}_mix", s2, p, c, dx, (buf["w_att_proj"], buf["w_ssm_proj"], buf["w_out"]))
        buf["w_att_proj"], buf["w_ssm_proj"], buf["w_out"] = gr.pop("bufs")
        dx, dn1 = ffn_bwd("ffn1", s1, dx)
        gr.update(ffn1_norm_w=dn1, ffn2_norm_w=dn2)
        grads[i] = gr
        if i == 1 and exchange_rest is not None:
            rest_out = exchange_rest(finished(bufs[1]), grads[1:])
    return loss_blk, dx, grads, finished(bufs[0]), rest_out


WEIGHTS = ["ffn1_norm_w", "ffn1_w_gate", "ffn1_w_up", "ffn1_w_down", "mix_norm_w", "w_in", "b_gates", "q_norm_w",
           "k_norm_w", "conv_w", "conv_b", "dt_bias", "a_log", "d_skip", "ssm_norm_w", "w_att_proj", "w_ssm_proj",
           "w_out", "ffn2_norm_w", "ffn2_w_gate", "ffn2_w_up", "ffn2_w_down"]
SHARD_AXIS = {"ffn1_w_gate": 2, "ffn1_w_up": 2, "ffn1_w_down": 1, "w_in": 2, "conv_w": 2, "w_att_proj": 2,
              "w_ssm_proj": 1, "w_out": 1, "ffn2_w_gate": 2, "ffn2_w_up": 2, "ffn2_w_down": 1}
BIG = [n for n in WEIGHTS if n in SHARD_AXIS]
SMALL = [n for n in WEIGHTS if n not in SHARD_AXIS]
def _from_flat(flat, shapes):
    v = flat.reshape(-1)
    out, off = [], 0
    for s in shapes:
        n = math.prod(s)
        out.append(v[off:off + n].reshape(s))
        off += n
    return out


def _pack_small(parts):
    v = jnp.concatenate([p.astype(F32).reshape(-1) for p in parts])
    rows = -(-v.shape[0] // (8 * LANES)) * 8
    return jnp.pad(v, (0, rows * LANES - v.shape[0])).reshape(rows, LANES)


def kernel(x, ffn1_norm_w, ffn1_w_gate, ffn1_w_up, ffn1_w_down, mix_norm_w, w_in, b_gates, q_norm_w, k_norm_w, conv_w, conv_b, dt_bias, a_log, d_skip, ssm_norm_w, w_att_proj, w_ssm_proj, w_out, ffn2_norm_w, ffn2_w_gate, ffn2_w_up, ffn2_w_down, loss_target, m_ffn1_norm_w, m_ffn1_w_gate, m_ffn1_w_up, m_ffn1_w_down, m_mix_norm_w, m_w_in, m_b_gates, m_q_norm_w, m_k_norm_w, m_conv_w, m_conv_b, m_dt_bias, m_a_log, m_d_skip, m_ssm_norm_w, m_w_att_proj, m_w_ssm_proj, m_w_out, m_ffn2_norm_w, m_ffn2_w_gate, m_ffn2_w_up, m_ffn2_w_down, v_ffn1_norm_w, v_ffn1_w_gate, v_ffn1_w_up, v_ffn1_w_down, v_mix_norm_w, v_w_in, v_b_gates, v_q_norm_w, v_k_norm_w, v_conv_w, v_conv_b, v_dt_bias, v_a_log, v_d_skip, v_ssm_norm_w, v_w_att_proj, v_w_ssm_proj, v_w_out, v_ffn2_norm_w, v_ffn2_w_gate, v_ffn2_w_up, v_ffn2_w_down):
    w = dict(zip(WEIGHTS, (ffn1_norm_w, ffn1_w_gate, ffn1_w_up, ffn1_w_down, mix_norm_w, w_in, b_gates, q_norm_w, k_norm_w, conv_w, conv_b, dt_bias, a_log, d_skip, ssm_norm_w, w_att_proj, w_ssm_proj, w_out, ffn2_norm_w, ffn2_w_gate, ffn2_w_up, ffn2_w_down)))
    m = dict(zip(WEIGHTS, (m_ffn1_norm_w, m_ffn1_w_gate, m_ffn1_w_up, m_ffn1_w_down, m_mix_norm_w, m_w_in, m_b_gates, m_q_norm_w, m_k_norm_w, m_conv_w, m_conv_b, m_dt_bias, m_a_log, m_d_skip, m_ssm_norm_w, m_w_att_proj, m_w_ssm_proj, m_w_out, m_ffn2_norm_w, m_ffn2_w_gate, m_ffn2_w_up, m_ffn2_w_down)))
    v = dict(zip(WEIGHTS, (v_ffn1_norm_w, v_ffn1_w_gate, v_ffn1_w_up, v_ffn1_w_down, v_mix_norm_w, v_w_in, v_b_gates, v_q_norm_w, v_k_norm_w, v_conv_w, v_conv_b, v_dt_bias, v_a_log, v_d_skip, v_ssm_norm_w, v_w_att_proj, v_w_ssm_proj, v_w_out, v_ffn2_norm_w, v_ffn2_w_gate, v_ffn2_w_up, v_ffn2_w_down)))
    depth = ffn1_norm_w.shape[0]
    bl = x.shape[0]
    t = bl * SEQ
    mx, my, mc = lax.axis_index("x"), lax.axis_index("y"), lax.axis_index("c")
    c_idx = mc.astype(jnp.int32).reshape(1)
    chip_idx = (2 * mx + my).astype(jnp.int32).reshape(1)

    cw_width = conv_w.shape[2]
    slots = lax.dynamic_update_slice(jnp.zeros((NCH, *conv_w.shape), F32), jnp.where(mc == 0, conv_w, 0.0)[None],
                                     (chip_idx[0], 0, 0, 0))
    conv_all = _all_sum_small("conv_gather", slots.reshape(-1, LANES)).reshape(NCH, *conv_w.shape)
    big = {"conv_w": jnp.concatenate([conv_all[k] for k in range(NCH)], axis=2)}

    mm_names = [n for n in BIG if n != "conv_w"]
    first = _all_gather_weights([w[n][:1].astype(BF16) for n in mm_names], chip_idx)
    bits = lax.bitcast_convert_type(first[0][0, 0, 0, 0], jnp.uint16).astype(jnp.int32)
    zero = jnp.minimum(bits, 0).astype(F32)
    rest = _all_gather_weights_beside([(w[n][1:] + zero).astype(BF16) for n in mm_names], chip_idx) if depth > 1 else first
    for n, a0, a1 in zip(mm_names, first, rest):
        big[n] = [(a0, 0)] + [(a1, i - 1) for i in range(1, depth)]
    small = {n: w[n] for n in SMALL}

    def exchange(tag, buf, layer_grads, beside):
        buf = dict(buf)
        buf["w_in"] = jnp.stack([_w_in_from_kernel(g["w_in"]).reshape(D_MODEL, NCH, -1).transpose(1, 0, 2)
                                 for g in layer_grads])
        mine = [buf[n] for n in mm_names]
        from_sibling = _pair_exchange("grad_pair_exchange_" + tag, mine)
        pairs = [_pair_sum(f"grad_pair_sum_{tag}_{n}", g, r, c_idx) for n, g, r in zip(mm_names, mine, from_sibling)]
        return pairs, (_chip_exchange_beside if beside else _chip_exchange)(pairs)

    c = _constants()
    layers = [_layer_params(big, small, i) for i in range(depth)]
    loss_blk, dx, grads, buf0, rest_out = _local_step(
        x.reshape(t, D_MODEL), loss_target.reshape(t, D_MODEL), layers, c,
        (lambda buf, gs: exchange("rest", buf, gs, True)) if depth > 1 else None)
    grad_x = dx.reshape(bl, SEQ, D_MODEL)
    pairs0, recv0 = exchange("first", buf0, grads[:1], False)
    halves = []
    for j, n in enumerate(mm_names):
        acc = None
        if depth > 1:
            acc = _chip_sum("grad_chip_sum_rest_" + n, rest_out[0][j], rest_out[1][j], chip_idx, c_idx, depth, 1)
        halves.append(_chip_sum("grad_chip_sum_first_" + n, pairs0[j], recv0[j], chip_idx, c_idx, depth, 0, into=acc))
    g_big = {n: r.reshape(w[n].shape) for n, r in zip(mm_names, _pair_share(halves))}

    def small_grad(n):
        if n == "q_norm_w":
            return jnp.stack([g["qkw"][0, 0, :64] for g in grads])
        if n == "k_norm_w":
            return jnp.stack([g["qkw"][1, 0, :64] for g in grads])
        return jnp.stack([g[n][0, :w[n].shape[1]] for g in grads])

    small_shapes = [w[n].shape for n in SMALL]
    conv_shape = (depth, conv_w.shape[1], NCH * cw_width)
    tot = _all_sum_small("small_all_sum", _pack_small(
        [small_grad(n) for n in SMALL] + [jnp.stack([g["conv_w"] for g in grads]), loss_blk[0, :1]]))
    unpacked = _from_flat(tot, small_shapes + [conv_shape, (1,)])
    g_small = dict(zip(SMALL, unpacked[:-2]))
    g_big["conv_w"] = lax.dynamic_slice_in_dim(unpacked[-2], chip_idx[0] * cw_width, cw_width, axis=2)
    loss = unpacked[-1][0]

    grad, delta, new_m, new_v = {}, {}, {}, {}
    for n in BIG:
        shp = w[n].shape
        two_d = (shp[0] * shp[1], shp[2])
        d_, m_, v_ = _adamw("adamw_" + n, w[n].reshape(two_d), g_big[n].reshape(two_d), m[n].reshape(two_d), v[n].reshape(two_d))
        grad[n], delta[n], new_m[n], new_v[n] = g_big[n], d_.reshape(shp), m_.reshape(shp), v_.reshape(shp)
    d_, m_, v_ = _adamw("adamw_small", _pack_small([w[n] for n in SMALL]), _pack_small([g_small[n] for n in SMALL]),
                        _pack_small([m[n] for n in SMALL]), _pack_small([v[n] for n in SMALL]))
    for n, a, b, c_ in zip(SMALL, _from_flat(d_, small_shapes), _from_flat(m_, small_shapes), _from_flat(v_, small_shapes)):
        grad[n], delta[n], new_m[n], new_v[n] = g_small[n], a, b, c_
    return (loss, grad_x, *[grad[n] for n in WEIGHTS], *[delta[n] for n in WEIGHTS],
            *[new_m[n] for n in WEIGHTS], *[new_v[n] for n in WEIGHTS])
```

```python
import functools
import math

import numpy as np
import jax
import jax.numpy as jnp
from jax import lax
from jax.experimental import pallas as pl
from jax.experimental.pallas import tpu as pltpu
from jax.experimental.pallas import tpu_sc as plsc

F32 = jnp.float32
BF16 = jnp.bfloat16
HI = lax.Precision.HIGHEST
MESH = pl.DeviceIdType.MESH

D_MODEL = 1024
SEQ = 2048
DEPTH = 4
D_FF = 2816
ATT_DILATIONS = (1, 4, 16)
BAND = 128
ATT_OUT = 512
QKV = 1536
D_INNER = 2048
N_SSM_HEADS = 32
N_SSM_GROUPS = 4
D_STATE = 128
XBC = 3072
CHUNK = 128
N_IN = 11808
EPS = 1e-6
ROPE_THETA = 10000.0
NP = 12288
Q0, K0, V0, Z0, G0, X0, DT0 = 0, 1536, 3072, 4608, 6656, 8704, 11776
DTW = 128
LR, B1, B2, ADAM_EPS, WD, STEP = 0.001, 0.9, 0.999, 1e-08, 0.01, 10

LANES = 128
VMEM_LIMIT = 48 * 1024 * 1024
NEG = -1e30


def _cp(sem=None, **kw):
    return pltpu.CompilerParams(dimension_semantics=sem, vmem_limit_bytes=VMEM_LIMIT, **kw)


def _dg(a, b, ca, cb):
    return lax.dot_general(a.astype(BF16), b.astype(BF16), (((ca,), (cb,)), ((), ())), preferred_element_type=F32)


@jax.custom_vjp
def dot_nn(a, b):
    return _dg(a, b, 1, 0)


def _dot_nn_fwd(a, b):
    return _dg(a, b, 1, 0), (a, b)


def _dot_nn_bwd(r, g):
    a, b = r
    return _dg(g, b, 1, 1).astype(a.dtype), _dg(a, g, 0, 0).astype(b.dtype)


dot_nn.defvjp(_dot_nn_fwd, _dot_nn_bwd)


@jax.custom_vjp
def dot_nt(a, b):
    return _dg(a, b, 1, 1)


def _dot_nt_fwd(a, b):
    return _dg(a, b, 1, 1), (a, b)


def _dot_nt_bwd(r, g):
    a, b = r
    return _dg(g, b, 1, 0).astype(a.dtype), _dg(g, a, 0, 0).astype(b.dtype)


dot_nt.defvjp(_dot_nt_fwd, _dot_nt_bwd)


@jax.custom_vjp
def dot_tn(a, b):
    return _dg(a, b, 0, 0)


def _dot_tn_fwd(a, b):
    return _dg(a, b, 0, 0), (a, b)


def _dot_tn_bwd(r, g):
    a, b = r
    return _dg(b, g, 1, 1).astype(a.dtype), _dg(a, g, 1, 0).astype(b.dtype)


dot_tn.defvjp(_dot_tn_fwd, _dot_tn_bwd)


def _dot2_raw(a, e, ce):
    hi = a.astype(BF16)
    lo = (a - hi.astype(F32)).astype(BF16)
    return _dg(hi, e, 1, ce) + _dg(lo, e, 1, ce)


@jax.custom_vjp
def dot2(a, e):
    return _dot2_raw(a, e, 0)


def _dot2_fwd(a, e):
    return _dot2_raw(a, e, 0), e


def _dot2_bwd(e, g):
    return _dot2_raw(g, e, 1), jnp.zeros_like(e)


dot2.defvjp(_dot2_fwd, _dot2_bwd)


def _tri2_raw(l, x, cl):
    hi = x.astype(BF16)
    lo = (x - hi.astype(F32)).astype(BF16)
    return _dg(l, hi, cl, 0) + _dg(l, lo, cl, 0)


@jax.custom_vjp
def tri_matmul(l, x):
    return _tri2_raw(l, x, 1)


def _tri_fwd(l, x):
    return _tri2_raw(l, x, 1), l


def _tri_bwd(l, g):
    return jnp.zeros_like(l), _tri2_raw(l, g, 0)


tri_matmul.defvjp(_tri_fwd, _tri_bwd)


def _dup64_raw(w):
    return w + pltpu.roll(w, 64, 1)


@jax.custom_vjp
def dup64(w):
    return _dup64_raw(w)


def _dup64_fwd(w):
    return _dup64_raw(w), None


def _dup64_bwd(_, g):
    lane = lax.broadcasted_iota(jnp.int32, g.shape, 1)
    return (jnp.where(lane < 64, _dup64_raw(g), 0.0),)


dup64.defvjp(_dup64_fwd, _dup64_bwd)


def _rope_rot_raw(y, sign):
    lane = lax.broadcasted_iota(jnp.int32, y.shape, 1)
    first_half = (lane & 32) == 0
    return sign * jnp.where(first_half, -pltpu.roll(y, LANES - 32, 1), pltpu.roll(y, 32, 1))


@jax.custom_vjp
def rope_rot(y):
    return _rope_rot_raw(y, 1.0)


def _rope_rot_fwd(y):
    return _rope_rot_raw(y, 1.0), None


def _rope_rot_bwd(_, g):
    return (_rope_rot_raw(g, -1.0),)


rope_rot.defvjp(_rope_rot_fwd, _rope_rot_bwd)


def _shift_rows_raw(x, s):
    n = x.shape[0]
    r = pltpu.roll(x, s % n, 0)
    rows = lax.broadcasted_iota(jnp.int32, x.shape, 0)
    keep = rows >= s if s > 0 else rows < n + s
    return jnp.where(keep, r, 0.0)


@functools.partial(jax.custom_vjp, nondiff_argnums=(1,))
def shift_rows(x, s):
    return _shift_rows_raw(x, s)


def _shift_fwd(x, s):
    return _shift_rows_raw(x, s), None


def _shift_bwd(s, _, g):
    return (_shift_rows_raw(g, -s),)


shift_rows.defvjp(_shift_fwd, _shift_bwd)


def _sigmoid(x):
    return 1.0 / (1.0 + jnp.exp(-x))


def _silu(x):
    return x * _sigmoid(x)


def _softplus(x):
    return jnp.maximum(x, 0.0) + jnp.log(1.0 + jnp.exp(-jnp.abs(x)))


def _head_mean_mat():
    i = np.arange(LANES)
    return jnp.asarray((i[:, None] // 64 == i[None, :] // 64).astype(np.float32) / 64.0)


def _head_expand_mat():
    e = np.zeros((LANES, D_INNER), np.float32)
    for l in range(D_INNER):
        e[l // 64, l] = 1.0
    return jnp.asarray(e)


def _ltri_mat():
    i = np.arange(CHUNK)
    return jnp.asarray((i[:, None] >= i[None, :]).astype(np.float32))


def _rope_tables():
    pos = jnp.arange(SEQ, dtype=F32)
    inv_freq = 1.0 / (ROPE_THETA ** (jnp.arange(0, 64, 2, dtype=F32) / 64))
    ang = pos[:, None] * inv_freq[None, :]
    return jnp.tile(jnp.cos(ang), (1, 4)), jnp.tile(jnp.sin(ang), (1, 4))


def _pick(n, cap):
    best = None
    for t in range(LANES, min(n, cap) + 1, LANES):
        if n % t == 0:
            best = t
    return best if best is not None else n


def _mm(name, a, b, mode, out_dtype=F32, alpha=None, res=None):
    if mode == "nn":
        (m, k), n = a.shape, b.shape[1]
    elif mode == "nt":
        (m, k), n = a.shape, b.shape[0]
    else:
        (k, m), n = a.shape, b.shape[1]
    tm, tn, tk = _pick(m, 1408), _pick(n, 1408), _pick(k, 1024)
    nk = k // tk
    ca, cb = {"nn": (1, 0), "nt": (1, 1), "tn": (0, 0)}[mode]
    a_spec = pl.BlockSpec((tk, tm), lambda i, j, kk: (kk, i)) if mode == "tn" else pl.BlockSpec((tm, tk), lambda i, j, kk: (i, kk))
    b_spec = pl.BlockSpec((tn, tk), lambda i, j, kk: (j, kk)) if mode == "nt" else pl.BlockSpec((tk, tn), lambda i, j, kk: (kk, j))
    o_spec = pl.BlockSpec((tm, tn), lambda i, j, kk: (i, j))
    has_res = res is not None

    def finish(acc, res_ref, o_ref):
        if alpha is not None:
            acc = acc * alpha
        if has_res:
            acc = acc + res_ref[...].astype(F32)
        o_ref[...] = acc.astype(o_ref.dtype)

    def body(*refs):
        a_ref, b_ref = refs[0], refs[1]
        res_ref = refs[2] if has_res else None
        o_ref = refs[3] if has_res else refs[2]
        part = _dg(a_ref[...], b_ref[...], ca, cb)
        if nk == 1:
            finish(part, res_ref, o_ref)
            return
        acc_ref = refs[-1]
        kk = pl.program_id(2)

        @pl.when(kk == 0)
        def _():
            acc_ref[...] = part

        @pl.when(kk > 0)
        def _():
            acc_ref[...] += part

        @pl.when(kk == nk - 1)
        def _():
            finish(acc_ref[...], res_ref, o_ref)

    ins = [a, b] + ([res] if has_res else [])
    in_specs = [a_spec, b_spec] + ([o_spec] if has_res else [])
    return pl.pallas_call(
        body, name=name, grid=(m // tm, n // tn, nk), in_specs=in_specs, out_specs=o_spec,
        out_shape=jax.ShapeDtypeStruct((m, n), out_dtype),
        scratch_shapes=[pltpu.VMEM((tm, tn), F32)] if nk > 1 else [],
        compiler_params=_cp(("parallel", "parallel", "arbitrary")),
    )(*ins)


def _mmx(name, grid, a, b, out, contract, *, alpha=None, res=None, into=None):
    nk = grid[-1]
    has_res, has_into = res is not None, into is not None
    n_in = 2 + has_res + has_into

    def finish(acc, res_ref, o_ref):
        if alpha is not None:
            acc = acc * alpha
        if has_res:
            acc = acc + res_ref[...].astype(F32)
        o_ref[...] = acc.astype(o_ref.dtype)

    def body(*refs):
        res_ref = refs[2] if has_res else None
        o_ref = refs[n_in]
        part = _dg(refs[0][...], refs[1][...], *contract)
        if nk == 1:
            finish(part, res_ref, o_ref)
            return
        acc_ref = refs[-1]
        kk = pl.program_id(len(grid) - 1)

        @pl.when(kk == 0)
        def _():
            acc_ref[...] = part

        @pl.when(kk > 0)
        def _():
            acc_ref[...] += part

        @pl.when(kk == nk - 1)
        def _():
            finish(acc_ref[...], res_ref, o_ref)

    operands = [a, b] + ([res] if has_res else [])
    in_specs = [pl.BlockSpec(blk, im) for _, blk, im in operands] + ([ANY] if has_into else [])
    acc_shape = tuple(d for d in out[2] if d is not None)
    return pl.pallas_call(
        body, name=name, grid=grid, in_specs=in_specs, out_specs=pl.BlockSpec(out[2], out[3]),
        out_shape=jax.ShapeDtypeStruct(out[0], out[1]),
        scratch_shapes=[pltpu.VMEM(acc_shape, F32)] if nk > 1 else [],
        input_output_aliases={n_in - 1: 0} if has_into else {},
        compiler_params=_cp(("parallel",) * (len(grid) - 1) + ("arbitrary",)),
    )(*[o[0] for o in operands], *([into] if has_into else []))


def _ew(name, fn, grid, ins, outs, scratch=()):
    n_in, n_out = len(ins), len(outs)

    def body(*refs):
        vals = [r[...] for r in refs[:n_in]]
        res = fn(*vals, *refs[n_in + n_out:])
        for r, v in zip(refs[n_in:n_in + n_out], res):
            r[...] = v.astype(r.dtype)

    res = pl.pallas_call(
        body, name=name, grid=grid,
        in_specs=[pl.BlockSpec(b, m) for _, b, m in ins],
        out_specs=[pl.BlockSpec(b, m) for _, _, b, m in outs],
        out_shape=[jax.ShapeDtypeStruct(s, d) for s, d, _, _ in outs],
        scratch_shapes=list(scratch),
        compiler_params=_cp(("arbitrary",) * len(grid)),
    )(*[a for a, _, _ in ins])
    return res


def _ew_bwd(name, fn, grid, ins, cts, wrt, adds=(), ct_fn=None):
    n_in, n_ct, n_add = len(ins), len(cts), len(adds)
    idxs = [w["idx"] for w in wrt]
    intos = [(k, w["into"]) for k, w in enumerate(wrt) if w.get("into") is not None]

    def body(*refs):
        prim = [r[...] for r in refs[:n_in]]
        ct = [r[...].astype(F32) for r in refs[n_in:n_in + n_ct]]
        addv = [r[...] for r in refs[n_in + n_ct:n_in + n_ct + n_add]]
        orefs = refs[n_in + n_ct + n_add + len(intos):]

        def f(*sel):
            full = list(prim)
            for i, s in zip(idxs, sel):
                full[i] = s
            return fn(*full)

        _, vjp = jax.vjp(f, *[prim[i].astype(F32) for i in idxs])
        grads = vjp(tuple(ct) if ct_fn is None else ct_fn(*ct))
        for w, g, r in zip(wrt, grads, orefs):
            if w["kind"] == "tile":
                if w.get("add") is not None:
                    g = g + addv[w["add"]].astype(F32)
                r[...] = g.astype(r.dtype)
            else:
                first = w["first"]()

                @pl.when(first)
                def _(r=r, g=g):
                    r[...] = g.astype(r.dtype)

                @pl.when(jnp.logical_not(first))
                def _(r=r, g=g):
                    r[...] += g.astype(r.dtype)

    allin = list(ins) + list(cts) + list(adds)
    return pl.pallas_call(
        body, name=name, grid=grid,
        in_specs=[pl.BlockSpec(b, m) for _, b, m in allin] + [ANY] * len(intos),
        out_specs=[pl.BlockSpec(w["block"], w["imap"]) for w in wrt],
        out_shape=[jax.ShapeDtypeStruct(w["shape"], w["dtype"]) for w in wrt],
        input_output_aliases={len(allin) + q: k for q, (k, _) in enumerate(intos)},
        compiler_params=_cp(("arbitrary",) * len(grid)),
    )(*[a for a, _, _ in allin], *[a for _, a in intos])


def _rmsnorm_fn(x, w):
    return (x * lax.rsqrt(jnp.mean(x * x, axis=-1, keepdims=True) + EPS) * w,)


def _swiglu_fn(g, u):
    return (_silu(g) * u,)


def _qkprep_fn(t, w64, cos, sin, hmean):
    w = jnp.sum(dup64(jnp.broadcast_to(w64, (8, LANES))), axis=0, keepdims=True) * 0.125
    y = t * lax.rsqrt(dot2(t * t, hmean) + EPS) * w
    return (y * cos + rope_rot(y) * sin,)


def _att_fn(q, kp, kc, vp, vc, first):
    iq = lax.broadcasted_iota(jnp.int32, (BAND, 2 * BAND), 0)
    ik = lax.broadcasted_iota(jnp.int32, (BAND, 2 * BAND), 1)
    rel = BAND + iq - ik
    ok = (rel >= 0) & (rel <= BAND) & ((ik >= BAND) | jnp.logical_not(first))
    lane = lax.broadcasted_iota(jnp.int32, (1, LANES), 1)
    kcat = jnp.concatenate([kp, kc], axis=0)
    vcat = jnp.concatenate([vp, vc], axis=0)
    o_pair = jnp.zeros((BAND, LANES), F32)
    l_pair = jnp.zeros((BAND, LANES), F32)
    for hh in range(2):
        lm = (lane // 64 == hh).astype(F32)
        s = dot_nt(q * lm, kcat) * 0.125
        s = jnp.where(ok, s, NEG)
        mx = jnp.max(s, axis=-1, keepdims=True)
        e = jnp.exp(s - mx)
        den = jnp.sum(e, axis=-1, keepdims=True)
        o_pair = o_pair + dot_nn(e / den, vcat) * lm
        l_pair = l_pair + (mx + jnp.log(den)) * lm
    return o_pair, l_pair


def _attmix_fn(o0, o1, o2, l0, l1, l2):
    m = jnp.maximum(jnp.maximum(l0, l1), l2)
    e0, e1, e2 = jnp.exp(l0 - m), jnp.exp(l1 - m), jnp.exp(l2 - m)
    return ((e0 * o0 + e1 * o1 + e2 * o2) / (e0 + e1 + e2),)


def _conv_fn(x, w0, w1, w2, w3, b):
    pre = x * w3 + shift_rows(x, 1) * w2 + shift_rows(x, 2) * w1 + shift_rows(x, 3) * w0 + b
    return (_silu(pre),)


def _ssdpre_fn(dtraw, bias, alog, ex):
    dt = _softplus(dtraw + bias)
    da = dt * (-jnp.exp(alog))
    return dot2(dt, ex), dot2(da, ex)


def _ssd_step(st, x, dtb, dab, bm, cm, ltri):
    cum = tri_matmul(ltri, dab)
    cum_t = cum.T
    xdt = x * dtb
    cb = dot_nt(cm, bm)
    ri = lax.broadcasted_iota(jnp.int32, (CHUNK, CHUNK), 0)
    ci = lax.broadcasted_iota(jnp.int32, (CHUNK, CHUNK), 1)
    causal = ri >= ci
    lane = lax.broadcasted_iota(jnp.int32, (1, LANES), 1)
    rowi = lax.broadcasted_iota(jnp.int32, (LANES, 1), 0)
    ys = []
    for p in range(4):
        sl = slice(p * LANES, (p + 1) * LANES)
        cum_p, cum_tp, xdt_p = cum[:, sl], cum_t[sl, :], xdt[:, sl]
        acc = jnp.zeros((CHUNK, LANES), F32)
        for hh in range(2):
            col = jnp.sum(cum_p * (lane == 64 * hh).astype(F32), axis=1, keepdims=True)
            row = jnp.sum(cum_tp * (rowi == 64 * hh).astype(F32), axis=0, keepdims=True)
            dec = jnp.exp(jnp.where(causal, col - row, NEG))
            acc = acc + dot_nn(cb * dec, xdt_p * (lane // 64 == hh).astype(F32))
        ys.append(acc)
    y_diag = jnp.concatenate(ys, axis=1)
    y_off = dot_nn(cm, st) * jnp.exp(cum)
    last_row = (lax.broadcasted_iota(jnp.int32, (CHUNK, 1), 0) == CHUNK - 1).astype(F32)
    last = jnp.sum(cum * last_row, axis=0, keepdims=True)
    new_st = st * jnp.exp(last) + dot_tn(bm, xdt * jnp.exp(last - cum))
    return new_st, y_diag + y_off


def _ssdpost_fn(y, xs, z, dskip, ex, nw):
    db = jnp.sum(dot2(jnp.broadcast_to(dskip, (8, LANES)), ex), axis=0, keepdims=True) * 0.125
    y2 = (y + db * xs) * _silu(z)
    return (y2 * lax.rsqrt(jnp.mean(y2 * y2, axis=-1, keepdims=True) + EPS) * nw,)


def _merge_fn(ya, ys, ga, gs, ba, bs):
    return (_sigmoid(ga + ba) * ya + _sigmoid(gs + bs) * ys,)


TM = 512


def _full(shape):
    nd = len(shape)
    return (shape, lambda *_: (0,) * nd)


def _rmsnorm(name, x, w):
    t = x.shape[0]
    return _ew(name, _rmsnorm_fn, (t // TM,),
               [(x, (TM, D_MODEL), lambda i: (i, 0)), (w, (1, D_MODEL), lambda i: (0, 0))],
               [((t, D_MODEL), BF16, (TM, D_MODEL), lambda i: (i, 0))])[0]


def _rmsnorm_bwd(name, x, w, dh, dres):
    t = x.shape[0]
    row = ((TM, D_MODEL), lambda i: (i, 0))
    return _ew_bwd(name, _rmsnorm_fn, (t // TM,),
                   [(x, *row), (w, (1, D_MODEL), lambda i: (0, 0))], [(dh, *row)],
                   [dict(idx=0, kind="tile", shape=(t, D_MODEL), dtype=F32, block=row[0], imap=row[1], add=0),
                    dict(idx=1, kind="acc", shape=(1, D_MODEL), dtype=F32, block=(1, D_MODEL), imap=lambda i: (0, 0),
                         first=lambda: pl.program_id(0) == 0)],
                   adds=[(dres, *row)])


def _qk_operands(proj, qkw, cos, sin, consts, tm):
    nrow = SEQ // tm
    c = ((LANES, LANES), lambda j, i: (0, 0))
    return [(proj, (tm, LANES), lambda j, i: (i, j)),
            (qkw, (None, 1, LANES), lambda j, i: (j // 12, 0, 0)),
            (cos, (tm, LANES), lambda j, i: (i % nrow, 0)),
            (sin, (tm, LANES), lambda j, i: (i % nrow, 0)),
            (consts["hmean"], *c)]


def _qkprep(name, proj, qkw, cos, sin, consts):
    t = proj.shape[0]
    return _ew(name, _qkprep_fn, (2 * QKV // LANES, t // TM), _qk_operands(proj, qkw, cos, sin, consts, TM),
               [((t, 2 * QKV), F32, (TM, LANES), lambda j, i: (i, j))])[0]


def _qkprep_bwd(name, proj, qkw, cos, sin, consts, dq, dk, dproj):
    t = proj.shape[0]
    nq = QKV // LANES

    def pick(cq, ck):
        return (jnp.where(pl.program_id(0) < nq, cq, ck),)

    return _ew_bwd(name, _qkprep_fn, (2 * nq, t // TM), _qk_operands(proj, qkw, cos, sin, consts, TM),
                   [(d, (TM, LANES), lambda j, i: (i, j % nq)) for d in (dq, dk)],
                   [dict(idx=0, kind="tile", shape=dproj.shape, dtype=dproj.dtype, block=(TM, LANES),
                         imap=lambda j, i: (i, j), into=dproj),
                    dict(idx=1, kind="acc", shape=(2, 1, LANES), dtype=F32, block=(None, 1, LANES),
                         imap=lambda j, i: (j // 12, 0, 0),
                         first=lambda: (pl.program_id(0) % 12 == 0) & (pl.program_id(1) == 0))],
                   ct_fn=pick)


def _att_specs(dil, g):
    nb = SEQ // dil // BAND
    pt = 4 if dil == 1 else 1
    w = pt * LANES
    blk = (None, BAND * dil, w)
    kq, kk, kv = g * ATT_OUT // w, (QKV + g * ATT_OUT) // w, (V0 + g * ATT_OUT) // w

    def cur(n):
        return jnp.minimum(n, nb - 1)

    def prev(n):
        return jnp.maximum(jnp.minimum(n, nb - 1) - 1, 0)

    return nb, pt, blk, [
        pl.BlockSpec(blk, lambda b, p, n: (b, cur(n), kq + p)),
        pl.BlockSpec(blk, lambda b, p, n: (b, prev(n), kk + p)),
        pl.BlockSpec(blk, lambda b, p, n: (b, cur(n), kk + p)),
        pl.BlockSpec(blk, lambda b, p, n: (b, prev(n), kv + p)),
        pl.BlockSpec(blk, lambda b, p, n: (b, cur(n), kv + p)),
    ]


def _att_rows(r, dil):
    return pl.ds(r, BAND, stride=dil) if dil > 1 else pl.ds(0, BAND)


def _att_fwd(name, qk, proj, g):
    bl = qk.shape[0] // SEQ
    dil = ATT_DILATIONS[g]
    nb, pt, blk, specs = _att_specs(dil, g)
    qk3 = qk.reshape(bl, SEQ, 2 * QKV)
    proj3 = proj.reshape(bl, SEQ, NP)
    o_spec = pl.BlockSpec(blk, lambda b, p, n: (b, n, p))

    def body(q, kp, kc, vp, vc, o_ref, l_ref):
        first = pl.program_id(2) == 0

        def residue(r, carry):
            sl = _att_rows(r, dil)
            for p in range(pt):
                ln = pl.ds(p * LANES, LANES)
                o, l = _att_fn(q[sl, ln], kp[sl, ln], kc[sl, ln], vp[sl, ln], vc[sl, ln], first)
                o_ref[sl, ln] = o
                l_ref[sl, ln] = l
            return carry

        lax.fori_loop(0, dil, residue, 0)

    o, l = pl.pallas_call(
        body, name=name, grid=(bl, ATT_OUT // (pt * LANES), nb), in_specs=specs, out_specs=[o_spec, o_spec],
        out_shape=[jax.ShapeDtypeStruct((bl, SEQ, ATT_OUT), F32)] * 2,
        compiler_params=_cp(("arbitrary",) * 3),
    )(qk3, qk3, qk3, proj3, proj3)
    return o.reshape(bl * SEQ, ATT_OUT), l.reshape(bl * SEQ, ATT_OUT)


def _att_bwd(name, qk, proj, g, do, dl, dq_buf, dk_buf, dv_buf):
    bl = qk.shape[0] // SEQ
    dil = ATT_DILATIONS[g]
    nb, pt, blk, specs = _att_specs(dil, g)
    w = pt * LANES
    qk3 = qk.reshape(bl, SEQ, 2 * QKV)
    proj3 = proj.reshape(bl, SEQ, NP)
    ct_spec = pl.BlockSpec(blk, lambda b, p, n: (b, jnp.minimum(n, nb - 1), p))
    do3 = do.reshape(bl, SEQ, ATT_OUT)
    dl3 = dl.reshape(bl, SEQ, ATT_OUT)
    kg = g * ATT_OUT // w

    def body(q, kp, kc, vp, vc, do_ref, dl_ref, _a, _b, _c, d_ref, dk_ref, dv_ref, ck, cv):
        n = pl.program_id(2)

        def residue(r, carry):
            sl = _att_rows(r, dil)
            for p in range(pt):
                ln = pl.ds(p * LANES, LANES)

                @pl.when(n < nb)
                def _(ln=ln):
                    first = n == 0
                    prim = [ref[sl, ln] for ref in (q, kp, kc, vp, vc)]
                    _, vjp = jax.vjp(lambda *a: _att_fn(*a, first), *prim)
                    dq, dkp, dkc, dvp, dvc = vjp((do_ref[sl, ln], dl_ref[sl, ln]))
                    d_ref[sl, ln] = dq

                    @pl.when(n > 0)
                    def _():
                        dk_ref[sl, ln] = ck[sl, ln] + dkp
                        dv_ref[sl, ln] = cv[sl, ln] + dvp

                    ck[sl, ln] = dkc
                    cv[sl, ln] = dvc

                @pl.when(n == nb)
                def _(ln=ln):
                    dk_ref[sl, ln] = ck[sl, ln]
                    dv_ref[sl, ln] = cv[sl, ln]

            return carry

        lax.fori_loop(0, dil, residue, 0)

    bufs = [a.reshape(bl, SEQ, QKV) for a in (dq_buf, dk_buf, dv_buf)]
    o_specs = [
        pl.BlockSpec(blk, lambda b, p, n: (b, jnp.minimum(n, nb - 1), kg + p)),
        pl.BlockSpec(blk, lambda b, p, n: (b, jnp.maximum(n - 1, 0), kg + p)),
        pl.BlockSpec(blk, lambda b, p, n: (b, jnp.maximum(n - 1, 0), kg + p)),
    ]
    dq, dk, dv = pl.pallas_call(
        body, name=name, grid=(bl, ATT_OUT // w, nb + 1), in_specs=specs + [ct_spec, ct_spec, ANY, ANY, ANY],
        out_specs=o_specs, out_shape=[jax.ShapeDtypeStruct(a.shape, a.dtype) for a in bufs],
        input_output_aliases={7: 0, 8: 1, 9: 2},
        scratch_shapes=[pltpu.VMEM((BAND * dil, w), F32), pltpu.VMEM((BAND * dil, w), F32)],
        compiler_params=_cp(("arbitrary",) * 3),
    )(qk3, qk3, qk3, proj3, proj3, do3, dl3, *bufs)
    return dq.reshape(dq_buf.shape), dk.reshape(dk_buf.shape), dv.reshape(dv_buf.shape)


def _attmix(name, os_, ls_):
    t = os_[0].shape[0]
    blk = ((TM, ATT_OUT), lambda i: (i, 0))
    return _ew(name, _attmix_fn, (t // TM,), [(a, *blk) for a in (*os_, *ls_)], [((t, ATT_OUT), BF16, *blk)])[0]


def _attmix_bwd(name, os_, ls_, datt):
    t = os_[0].shape[0]
    blk = ((TM, ATT_OUT), lambda i: (i, 0))
    return _ew_bwd(name, _attmix_fn, (t // TM,), [(a, *blk) for a in (*os_, *ls_)], [(datt, *blk)],
                   [dict(idx=k, kind="tile", shape=(t, ATT_OUT), dtype=F32, block=blk[0], imap=blk[1]) for k in range(6)])


CONV_TC = 256


def _conv_operands(proj3, conv_w, conv_b):
    c0 = X0 // CONV_TC
    ins = [(proj3, (None, SEQ, CONV_TC), lambda j, b: (b, 0, c0 + j))]
    for k in range(4):
        ins.append((conv_w, (None, 1, CONV_TC), lambda j, b, k=k: (k, 0, j)))
    ins.append((conv_b, (1, CONV_TC), lambda j, b: (0, j)))
    return ins


def _conv(name, proj3, conv_w, conv_b):
    bl = proj3.shape[0]
    return _ew(name, _conv_fn, (XBC // CONV_TC, bl), _conv_operands(proj3, conv_w, conv_b),
               [((bl, SEQ, XBC), F32, (None, SEQ, CONV_TC), lambda j, b: (b, 0, j))])[0]


def _conv_bwd(name, proj3, conv_w, conv_b, dxs3, db3, dc3, dproj3):
    bl = proj3.shape[0]
    nx = D_INNER // CONV_TC
    nb_ = N_SSM_GROUPS * D_STATE // CONV_TC
    blk = (None, SEQ, CONV_TC)
    cts = [(dxs3, blk, lambda j, b: (b, 0, jnp.minimum(j, nx - 1))),
           (db3, blk, lambda j, b: (b, 0, jnp.clip(j - nx, 0, nb_ - 1))),
           (dc3, blk, lambda j, b: (b, 0, jnp.clip(j - nx - nb_, 0, nb_ - 1)))]

    def pick(cx, cb, cc):
        j = pl.program_id(0)
        return (jnp.where(j < nx, cx, jnp.where(j < nx + nb_, cb, cc)),)

    first = lambda: pl.program_id(1) == 0
    wrt = [dict(idx=0, kind="tile", shape=dproj3.shape, dtype=dproj3.dtype, block=blk,
                imap=lambda j, b: (b, 0, X0 // CONV_TC + j), into=dproj3)]
    for k in range(4):
        wrt.append(dict(idx=1 + k, kind="acc", shape=(1, XBC), dtype=F32, block=(1, CONV_TC),
                        imap=lambda j, b: (0, j), first=first))
    wrt.append(dict(idx=5, kind="acc", shape=(1, XBC), dtype=F32, block=(1, CONV_TC), imap=lambda j, b: (0, j), first=first))
    return _ew_bwd(name, _conv_fn, (XBC // CONV_TC, bl), _conv_operands(proj3, conv_w, conv_b), cts, wrt, ct_fn=pick)


SSD_TM = 256


def _ssdpre_operands(proj, dt_bias, a_log, ex):
    return [(proj, (SSD_TM, DTW), lambda i: (i, DT0 // DTW)), (dt_bias, *_full((1, DTW))), (a_log, *_full((1, DTW))),
            (ex, *_full((LANES, D_INNER)))]


def _ssdpre(name, proj, dt_bias, a_log, ex):
    t = proj.shape[0]
    blk = ((SSD_TM, D_INNER), lambda i: (i, 0))
    return _ew(name, _ssdpre_fn, (t // SSD_TM,), _ssdpre_operands(proj, dt_bias, a_log, ex),
               [((t, D_INNER), F32, *blk), ((t, D_INNER), F32, *blk)])


def _ssdpre_bwd(name, proj, dt_bias, a_log, ex, ddtb, ddab):
    t = proj.shape[0]
    blk = ((SSD_TM, D_INNER), lambda i: (i, 0))
    first = lambda: pl.program_id(0) == 0
    return _ew_bwd(name, _ssdpre_fn, (t // SSD_TM,), _ssdpre_operands(proj, dt_bias, a_log, ex),
                   [(ddtb, *blk), (ddab, *blk)],
                   [dict(idx=0, kind="tile", shape=(t, DTW), dtype=BF16, block=(SSD_TM, DTW), imap=lambda i: (i, 0)),
                    dict(idx=1, kind="acc", shape=(1, DTW), dtype=F32, block=(1, DTW), imap=lambda i: (0, 0), first=first),
                    dict(idx=2, kind="acc", shape=(1, DTW), dtype=F32, block=(1, DTW), imap=lambda i: (0, 0), first=first)])


def _ssd_in_specs(rev):
    nc = SEQ // CHUNK

    def c_(c):
        return nc - 1 - c if rev else c

    wide = (None, CHUNK, D_INNER)
    nar = (None, CHUNK, N_SSM_GROUPS * D_STATE)
    nb_ = D_INNER // (N_SSM_GROUPS * D_STATE)
    return [
        pl.BlockSpec(wide, lambda b, c: (b, c_(c), 0)),
        pl.BlockSpec(wide, lambda b, c: (b, c_(c), 0)),
        pl.BlockSpec(wide, lambda b, c: (b, c_(c), 0)),
        pl.BlockSpec(nar, lambda b, c: (b, c_(c), nb_)),
        pl.BlockSpec(nar, lambda b, c: (b, c_(c), nb_ + 1)),
        pl.BlockSpec((CHUNK, CHUNK), lambda b, c: (0, 0)),
    ], c_


def _ssd_cols(g):
    return pl.ds(g * 4 * LANES, 4 * LANES), pl.ds(g * D_STATE, D_STATE)


def _ssd_fwd(name, xc3, dtb3, dab3, ltri):
    bl = xc3.shape[0]
    nc = SEQ // CHUNK
    specs, _ = _ssd_in_specs(False)

    def body(x, dtb, dab, bm, cm, lt, y_ref, st_ref, st):
        @pl.when(pl.program_id(1) == 0)
        def _():
            st[...] = jnp.zeros_like(st)

        ltv = lt[...]
        for g in range(N_SSM_GROUPS):
            wl, nl = _ssd_cols(g)
            s0 = st[g]
            st_ref[g] = s0
            new_st, y = _ssd_step(s0, x[:, wl], dtb[:, wl], dab[:, wl], bm[:, nl], cm[:, nl], ltv)
            y_ref[:, wl] = y
            st[g] = new_st

    return pl.pallas_call(
        body, name=name, grid=(bl, nc), in_specs=specs,
        out_specs=[pl.BlockSpec((None, CHUNK, D_INNER), lambda b, c: (b, c, 0)),
                   pl.BlockSpec((None, N_SSM_GROUPS, None, D_STATE, 4 * LANES), lambda b, c: (b, 0, c, 0, 0))],
        out_shape=[jax.ShapeDtypeStruct((bl, SEQ, D_INNER), F32),
                   jax.ShapeDtypeStruct((bl, N_SSM_GROUPS, nc, D_STATE, 4 * LANES), F32)],
        scratch_shapes=[pltpu.VMEM((N_SSM_GROUPS, D_STATE, 4 * LANES), F32)],
        compiler_params=_cp(("arbitrary",) * 2),
    )(xc3, dtb3, dab3, xc3, xc3, ltri)


def _ssd_bwd(name, xc3, dtb3, dab3, ltri, states, dy3, dxs_part3):
    bl = xc3.shape[0]
    nc = SEQ // CHUNK
    specs, c_ = _ssd_in_specs(True)
    wide = pl.BlockSpec((None, CHUNK, D_INNER), lambda b, c: (b, c_(c), 0))
    nar = pl.BlockSpec((None, CHUNK, N_SSM_GROUPS * D_STATE), lambda b, c: (b, c_(c), 0))
    st_spec = pl.BlockSpec((None, N_SSM_GROUPS, None, D_STATE, 4 * LANES), lambda b, c: (b, 0, c_(c), 0, 0))

    def body(x, dtb, dab, bm, cm, lt, st_ref, dy, dxp, dx_ref, ddtb_ref, ddab_ref, dbm_ref, dcm_ref, dst):
        @pl.when(pl.program_id(1) == 0)
        def _():
            dst[...] = jnp.zeros_like(dst)

        ltv = lt[...]
        for g in range(N_SSM_GROUPS):
            wl, nl = _ssd_cols(g)
            _, vjp = jax.vjp(lambda *a: _ssd_step(*a, ltv), st_ref[g], x[:, wl], dtb[:, wl], dab[:, wl], bm[:, nl], cm[:, nl])
            d_st, d_x, d_dtb, d_dab, d_bm, d_cm = vjp((dst[g], dy[:, wl]))
            dst[g] = d_st
            dx_ref[:, wl] = d_x + dxp[:, wl]
            ddtb_ref[:, wl] = d_dtb
            ddab_ref[:, wl] = d_dab
            dbm_ref[:, nl] = d_bm
            dcm_ref[:, nl] = d_cm

    big = jax.ShapeDtypeStruct((bl, SEQ, D_INNER), F32)
    small = jax.ShapeDtypeStruct((bl, SEQ, N_SSM_GROUPS * D_STATE), F32)
    return pl.pallas_call(
        body, name=name, grid=(bl, nc), in_specs=specs + [st_spec, wide, wide],
        out_specs=[wide, wide, wide, nar, nar], out_shape=[big, big, big, small, small],
        scratch_shapes=[pltpu.VMEM((N_SSM_GROUPS, D_STATE, 4 * LANES), F32)],
        compiler_params=_cp(("arbitrary",) * 2),
    )(xc3, dtb3, dab3, xc3, xc3, ltri, states, dy3, dxs_part3)


def _ssdpost_operands(y, xc, proj, d_skip, ex, nw):
    w = 4 * LANES
    return [(y, (SSD_TM, w), lambda j, i: (i, j)), (xc, (SSD_TM, w), lambda j, i: (i, j)),
            (proj, (SSD_TM, w), lambda j, i: (i, Z0 // w + j)), (d_skip, (1, DTW), lambda j, i: (0, 0)),
            (ex, (LANES, w), lambda j, i: (0, j)), (nw, (1, w), lambda j, i: (0, j))]


def _ssdpost(name, y, xc, proj, d_skip, ex, nw):
    t = y.shape[0]
    w = 4 * LANES
    return _ew(name, _ssdpost_fn, (D_INNER // w, t // SSD_TM), _ssdpost_operands(y, xc, proj, d_skip, ex, nw),
               [((t, D_INNER), BF16, (SSD_TM, w), lambda j, i: (i, j))])[0]


def _ssdpost_bwd(name, y, xc, proj, d_skip, ex, nw, dysn, dproj):
    t = y.shape[0]
    w = 4 * LANES
    blk = ((SSD_TM, w), lambda j, i: (i, j))
    return _ew_bwd(name, _ssdpost_fn, (D_INNER // w, t // SSD_TM), _ssdpost_operands(y, xc, proj, d_skip, ex, nw),
                   [(dysn, *blk)],
                   [dict(idx=0, kind="tile", shape=(t, D_INNER), dtype=F32, block=blk[0], imap=blk[1]),
                    dict(idx=1, kind="tile", shape=(t, D_INNER), dtype=F32, block=blk[0], imap=blk[1]),
                    dict(idx=2, kind="tile", shape=dproj.shape, dtype=dproj.dtype, block=blk[0],
                         imap=lambda j, i: (i, Z0 // w + j), into=dproj),
                    dict(idx=3, kind="acc", shape=(1, DTW), dtype=F32, block=(1, DTW), imap=lambda j, i: (0, 0),
                         first=lambda: (pl.program_id(0) == 0) & (pl.program_id(1) == 0)),
                    dict(idx=5, kind="acc", shape=(1, D_INNER), dtype=F32, block=(1, w), imap=lambda j, i: (0, j),
                         first=lambda: pl.program_id(1) == 0)])


def _merge_operands(ya, ys, proj, b_gates):
    w = 4 * LANES
    g0 = G0 // w
    nh = D_MODEL // w
    return [(ya, (TM, w), lambda j, i: (i, j)), (ys, (TM, w), lambda j, i: (i, j)),
            (proj, (TM, w), lambda j, i: (i, g0 + j)), (proj, (TM, w), lambda j, i: (i, g0 + nh + j)),
            (b_gates, (1, w), lambda j, i: (0, j)), (b_gates, (1, w), lambda j, i: (0, nh + j))]


def _merge(name, ya, ys, proj, b_gates):
    t = ya.shape[0]
    w = 4 * LANES
    return _ew(name, _merge_fn, (D_MODEL // w, t // TM), _merge_operands(ya, ys, proj, b_gates),
               [((t, D_MODEL), BF16, (TM, w), lambda j, i: (i, j))])[0]


def _merge_bwd(name, ya, ys, proj, b_gates, dmixed):
    t = ya.shape[0]
    w = 4 * LANES
    blk = ((TM, w), lambda j, i: (i, j))
    first = lambda: pl.program_id(1) == 0
    tile = lambda k, dt: dict(idx=k, kind="tile", shape=(t, D_MODEL), dtype=dt, block=blk[0], imap=blk[1])
    acc = lambda k: dict(idx=k, kind="acc", shape=(1, D_MODEL), dtype=F32, block=(1, w), imap=lambda j, i: (0, j), first=first)
    return _ew_bwd(name, _merge_fn, (D_MODEL // w, t // TM), _merge_operands(ya, ys, proj, b_gates), [(dmixed, *blk)],
                   [tile(0, BF16), tile(1, BF16), tile(2, BF16), tile(3, BF16), acc(4), acc(5)])


def _loss(name, y, tgt):
    t = y.shape[0]
    blk = pl.BlockSpec((TM, D_MODEL), lambda i: (i, 0))

    def body(y_ref, t_ref, dy_ref, l_ref):
        e = y_ref[...] - t_ref[...]
        dy_ref[...] = e * (1.0 / D_MODEL)
        part = jnp.sum(jnp.sum(e * e, axis=-1, keepdims=True), axis=0, keepdims=True) * (0.5 / D_MODEL)
        part = jnp.broadcast_to(part, (8, LANES))

        @pl.when(pl.program_id(0) == 0)
        def _():
            l_ref[...] = part

        @pl.when(pl.program_id(0) > 0)
        def _():
            l_ref[...] += part

    return pl.pallas_call(
        body, name=name, grid=(t // TM,), in_specs=[blk, blk],
        out_specs=[blk, pl.BlockSpec((8, LANES), lambda i: (0, 0))],
        out_shape=[jax.ShapeDtypeStruct((t, D_MODEL), F32), jax.ShapeDtypeStruct((8, LANES), F32)],
        compiler_params=_cp(("arbitrary",)),
    )(y, tgt)


def _adamw_fn(w, g, m, v):
    m2 = B1 * m + (1.0 - B1) * g
    v2 = B2 * v + (1.0 - B2) * (g * g)
    m_hat = m2 / (1.0 - B1 ** STEP)
    v_hat = v2 / (1.0 - B2 ** STEP)
    return -LR * (m_hat / (jnp.sqrt(v_hat) + ADAM_EPS) + WD * w), m2, v2


def _adamw(name, w, g, m, v):
    rows, cols = w.shape
    tm = rows
    for cand in (512, 256, 128, 64, 32, 16, 8):
        if rows % cand == 0 and cand * cols * 4 <= (1 << 21):
            tm = cand
            break
    blk = ((tm, cols), lambda i: (i, 0))
    return _ew(name, _adamw_fn, (rows // tm,), [(a, *blk) for a in (w, g, m, v)], [((rows, cols), F32, *blk)] * 3)


NCH = 4
TMM = 1024
TKK = 1024


def _ffn_fwd(tag, x, nw, wg, wu, wd, li):
    t, fc = x.shape[0], wg.shape[-1]
    h = _rmsnorm(tag + "_norm", x, nw)

    def up_body(h_ref, wg_ref, wu_ref, g_ref, u_ref, a_ref):
        hv = h_ref[...]
        g = _dg(hv, wg_ref[...], 1, 0).astype(BF16)
        u = _dg(hv, wu_ref[...], 1, 0).astype(BF16)
        g_ref[...] = g
        u_ref[...] = u
        a_ref[...] = _swiglu_fn(g.astype(F32), u.astype(F32))[0].astype(BF16)

    w_spec = pl.BlockSpec((None, None, D_MODEL, fc), lambda k, i: (k, li, 0, 0))
    o_spec = pl.BlockSpec((None, TMM, fc), lambda k, i: (k, i, 0))
    g, u, a = pl.pallas_call(
        up_body, name=tag + "_up_act", grid=(NCH, t // TMM),
        in_specs=[pl.BlockSpec((TMM, D_MODEL), lambda k, i: (i, 0)), w_spec, w_spec], out_specs=[o_spec] * 3,
        out_shape=[jax.ShapeDtypeStruct((NCH, t, fc), BF16)] * 3, compiler_params=_cp(("parallel", "parallel")),
    )(h, wg, wu)
    row = ((TMM, D_MODEL), lambda i, j, k: (i, 0))
    y = _mmx(tag + "_down", (t // TMM, 1, NCH),
             (a, (None, TMM, fc), lambda i, j, k: (k, i, 0)),
             (wd, (None, None, fc, D_MODEL), lambda i, j, k: (k, li, 0, 0)),
             ((t, D_MODEL), F32, *row), (1, 0), alpha=0.5, res=(x, *row))
    return y, (x, h, g, u, a)


def _ffn_bwd(tag, saved, nw, wg, wu, wd, li, dy, bufs, gl):
    x, h, g, u, a = saved
    t, fc = x.shape[0], wg.shape[-1]
    bg, bu, bd = bufs
    def dact_body(dy_ref, wd_ref, g_ref, u_ref, dg_ref, du_ref):
        da = _dg(dy_ref[...], wd_ref[...], 1, 1) * 0.5
        _, vjp = jax.vjp(_swiglu_fn, g_ref[...].astype(F32), u_ref[...].astype(F32))
        dg, du = vjp((da,))
        dg_ref[...] = dg.astype(BF16)
        du_ref[...] = du.astype(BF16)

    c_spec = pl.BlockSpec((None, TMM, fc), lambda k, i: (k, i, 0))
    dg, du = pl.pallas_call(
        dact_body, name=tag + "_down_dx_act", grid=(NCH, t // TMM),
        in_specs=[pl.BlockSpec((TMM, D_MODEL), lambda k, i: (i, 0)),
                  pl.BlockSpec((None, None, fc, D_MODEL), lambda k, i: (k, li, 0, 0)), c_spec, c_spec],
        out_specs=[c_spec] * 2, out_shape=[jax.ShapeDtypeStruct((NCH, t, fc), BF16)] * 2,
        compiler_params=_cp(("parallel", "parallel")),
    )(dy, wd, g, u)
    bd = _mmx(tag + "_down_dw", (NCH, 1, t // TKK),
              (a, (None, TKK, fc), lambda k, j, kk: (k, kk, 0)),
              (dy, (TKK, D_MODEL), lambda k, j, kk: (kk, 0)),
              (bd.shape, BF16, (None, None, fc, D_MODEL), lambda k, j, kk: (gl, k, 0, 0)), (0, 0), alpha=0.5, into=bd)
    def dw(name, d, buf):
        return _mmx(name, (NCH, 1, t // TKK),
                    (h, (TKK, D_MODEL), lambda k, i, kk: (kk, 0)),
                    (d, (None, TKK, fc), lambda k, i, kk: (k, kk, 0)),
                    (buf.shape, BF16, (None, None, D_MODEL, fc), lambda k, i, kk: (gl, k, 0, 0)), (0, 0), into=buf)

    bg, bu = dw(tag + "_gate_dw", dg, bg), dw(tag + "_up_dw", du, bu)
    row = ((TMM, D_MODEL), lambda i, j, k: (i, 0))

    def dx_(name, d, w, res):
        return _mmx(name, (t // TMM, 1, NCH),
                    (d, (None, TMM, fc), lambda i, j, k: (k, i, 0)),
                    (w, (None, None, D_MODEL, fc), lambda i, j, k: (k, li, 0, 0)),
                    ((t, D_MODEL), F32, *row), (1, 1), res=None if res is None else (res, *row))

    dh = dx_(tag + "_up_dx", du, wu, dx_(tag + "_gate_dx", dg, wg, None))
    dx, dnw = _rmsnorm_bwd(tag + "_norm_bwd", x, nw, dh, dy)
    return dx, dnw, (bg, bu, bd)


def _mixer_fwd(tag, x, p, c):
    t = x.shape[0]
    bl = t // SEQ
    h = _rmsnorm(tag + "_norm", x, p["mix_norm_w"])
    proj = _mm(tag + "_in", h, p["w_in"], "nn")
    qk = _qkprep(tag + "_qk", proj, p["qkw"], c["cos"], c["sin"], c)
    os_, ls_ = [], []
    for g in range(3):
        o, l = _att_fwd(f"{tag}_att{g}", qk, proj, g)
        os_.append(o)
        ls_.append(l)
    att = _attmix(tag + "_attmix", os_, ls_)
    li = p["layer"]
    wa, ws, wo = p["w_att_proj"], p["w_ssm_proj"], p["w_out"]
    ca, cs, co = wa.shape[-1], ws.shape[-2], wo.shape[-2]
    row = ((TMM, D_MODEL), lambda i, j, k: (i, 0))
    ya = _mmx(tag + "_attproj", (t // TMM, NCH, 1),
              (att, (TMM, ATT_OUT), lambda i, k, kk: (i, 0)),
              (wa, (None, None, ATT_OUT, ca), lambda i, k, kk: (k, li, 0, 0)),
              ((t, D_MODEL), F32, (TMM, ca), lambda i, k, kk: (i, k)), (1, 0))
    proj3 = proj.reshape(bl, SEQ, NP)
    xc3 = _conv(tag + "_conv", proj3, p["conv_w"], p["conv_b"])
    xc = xc3.reshape(t, XBC)
    dtb, dab = _ssdpre(tag + "_ssdpre", proj, p["dt_bias"], p["a_log"], c["ex"])
    dtb3, dab3 = dtb.reshape(bl, SEQ, D_INNER), dab.reshape(bl, SEQ, D_INNER)
    y3, states = _ssd_fwd(tag + "_ssd", xc3, dtb3, dab3, c["ltri"])
    y = y3.reshape(t, D_INNER)
    ysn = _ssdpost(tag + "_ssdpost", y, xc, proj, p["d_skip"], c["ex"], p["ssm_norm_w"])
    ys = _mmx(tag + "_ssmproj", (t // TMM, 1, NCH),
              (ysn, (TMM, cs), lambda i, j, k: (i, k)),
              (ws, (None, None, cs, D_MODEL), lambda i, j, k: (k, li, 0, 0)),
              ((t, D_MODEL), F32, *row), (1, 0))
    mixed = _merge(tag + "_merge", ya, ys, proj, p["b_gates"])
    out = _mmx(tag + "_out", (t // TMM, 1, NCH),
               (mixed, (TMM, co), lambda i, j, k: (i, k)),
               (wo, (None, None, co, D_MODEL), lambda i, j, k: (k, li, 0, 0)),
               ((t, D_MODEL), F32, *row), (1, 0), res=(x, *row))
    return out, (x, h, proj, qk, os_, ls_, att, ya, xc3, dtb3, dab3, states, y, ysn, ys, mixed)


def _mixer_bwd(tag, saved, p, c, dout, bufs):
    x, h, proj, qk, os_, ls_, att, ya, xc3, dtb3, dab3, states, y, ysn, ys, mixed = saved
    t = x.shape[0]
    bl = t // SEQ
    xc = xc3.reshape(t, XBC)
    proj3 = proj.reshape(bl, SEQ, NP)
    gr = {}
    li, gl = p["layer"], p["global_layer"]
    wa, ws, wo = p["w_att_proj"], p["w_ssm_proj"], p["w_out"]
    ca, cs, co = wa.shape[-1], ws.shape[-2], wo.shape[-2]
    b_att, b_ssm, b_out = bufs

    def chunk_dx(name, d, w, cw):
        return _mmx(name, (t // TMM, NCH, D_MODEL // TKK),
                    (d, (TMM, TKK), lambda i, k, kk: (i, kk)),
                    (w, (None, None, cw, TKK), lambda i, k, kk: (k, li, 0, kk)),
                    ((t, NCH * cw), F32, (TMM, cw), lambda i, k, kk: (i, k)), (1, 1))

    def full_dw(name, a_, d, buf):
        kdim = a_.shape[1]
        tm = min(kdim, 1024)
        return _mmx(name, (kdim // tm, 1, t // TKK),
                    (a_, (TKK, tm), lambda i, j, kk: (kk, i)),
                    (d, (TKK, D_MODEL), lambda i, j, kk: (kk, 0)),
                    (buf.shape, BF16, (None, tm, D_MODEL), lambda i, j, kk: (gl, i, 0)), (0, 0), into=buf)

    dmixed = chunk_dx(tag + "_out_dx", dout, wo, co)
    b_out = full_dw(tag + "_out_dw", mixed, dout, b_out)
    dya, dys, dga, dgs, dba, dbs = _merge_bwd(tag + "_merge_bwd", ya, ys, proj, p["b_gates"], dmixed)
    gr["b_gates"] = jnp.concatenate([dba, dbs], axis=1)
    datt = _mmx(tag + "_attproj_dx", (t // TMM, 1, NCH),
                (dya, (TMM, ca), lambda i, j, k: (i, k)),
                (wa, (None, None, ATT_OUT, ca), lambda i, j, k: (k, li, 0, 0)),
                ((t, ATT_OUT), F32, (TMM, ATT_OUT), lambda i, j, k: (i, 0)), (1, 1))
    b_att = _mmx(tag + "_attproj_dw", (NCH, 1, t // TKK),
                 (att, (TKK, ATT_OUT), lambda k, j, kk: (kk, 0)),
                 (dya, (TKK, ca), lambda k, j, kk: (kk, k)),
                 (b_att.shape, BF16, (None, None, ATT_OUT, ca), lambda k, j, kk: (gl, k, 0, 0)), (0, 0), into=b_att)
    dysn = chunk_dx(tag + "_ssmproj_dx", dys, ws, cs)
    b_ssm = full_dw(tag + "_ssmproj_dw", ysn, dys, b_ssm)
    gr["bufs"] = (b_att, b_ssm, b_out)
    dmix = _attmix_bwd(tag + "_attmix_bwd", os_, ls_, datt)
    dq = dk = dv = jnp.zeros((t, QKV), F32)
    for g in range(3):
        dq, dk, dv = _att_bwd(f"{tag}_att{g}_bwd", qk, proj, g, dmix[g], dmix[3 + g], dq, dk, dv)
    dproj = jnp.zeros((t, NP), BF16)
    dproj = lax.dynamic_update_slice(dproj, dv.astype(BF16), (0, V0))
    dproj = lax.dynamic_update_slice(dproj, dga, (0, G0))
    dproj = lax.dynamic_update_slice(dproj, dgs, (0, G0 + D_MODEL))
    dproj, gr["qkw"] = _qkprep_bwd(tag + "_qk_bwd", proj, p["qkw"], c["cos"], c["sin"], c, dq, dk, dproj)
    dy, dxs_part, dproj, gr["d_skip"], gr["ssm_norm_w"] = _ssdpost_bwd(
        tag + "_ssdpost_bwd", y, xc, proj, p["d_skip"], c["ex"], p["ssm_norm_w"], dysn, dproj)
    dxs3, ddtb3, ddab3, db3, dc3 = _ssd_bwd(
        tag + "_ssd_bwd", xc3, dtb3, dab3, c["ltri"], states, dy.reshape(bl, SEQ, D_INNER), dxs_part.reshape(bl, SEQ, D_INNER))
    ddt, gr["dt_bias"], gr["a_log"] = _ssdpre_bwd(
        tag + "_ssdpre_bwd", proj, p["dt_bias"], p["a_log"], c["ex"], ddtb3.reshape(t, D_INNER), ddab3.reshape(t, D_INNER))
    dproj = lax.dynamic_update_slice(dproj, ddt, (0, DT0))
    dproj3, dcw0, dcw1, dcw2, dcw3, gr["conv_b"] = _conv_bwd(
        tag + "_conv_bwd", proj3, p["conv_w"], p["conv_b"], dxs3, db3, dc3, dproj.reshape(bl, SEQ, NP))
    dproj = dproj3.reshape(t, NP)
    gr["conv_w"] = jnp.concatenate([dcw0, dcw1, dcw2, dcw3], axis=0)
    gr["w_in"] = _mm(tag + "_in_dw", h, dproj, "tn", out_dtype=BF16)
    dh = _mm(tag + "_in_dx", dproj, p["w_in"], "nt")
    dx, gr["mix_norm_w"] = _rmsnorm_bwd(tag + "_norm_bwd", x, p["mix_norm_w"], dh, dout)
    return dx, gr


def _constants():
    cos, sin = _rope_tables()
    return dict(cos=cos, sin=sin, hmean=_head_mean_mat(),
                ex=_head_expand_mat(), ltri=_ltri_mat())


ANY = pl.BlockSpec(memory_space=pl.ANY)


def _mesh_pos():
    return lax.axis_index("x"), lax.axis_index("y"), lax.axis_index("c")


def _other_chips(x, y):
    return [(1 - x, y), (x, 1 - y), (1 - x, 1 - y)]


def _gather_exchange(srcs, outs, send_sems, recv_sems):
    n = len(srcs)
    x, y, c = _mesh_pos()
    chips = _other_chips(x, y)

    def part(a, chip, hf):
        h = srcs[a].shape[1] // 2
        return outs[a].at[2 * chip[0] + chip[1], :, pl.ds(hf * h, h), :]

    def mine(a):
        h = srcs[a].shape[1] // 2
        return srcs[a].at[:, pl.ds(c * h, h), :]

    def copy(a, k, src_ref, dst_ref, to):
        return pltpu.make_async_remote_copy(src_ref=src_ref, dst_ref=dst_ref, send_sem=send_sems.at[6 * a + k],
                                            recv_sem=recv_sems.at[6 * a + k], device_id=to, device_id_type=MESH)

    first = [copy(a, j, mine(a), part(a, (x, y), c), (*chip, c)) for a in range(n) for j, chip in enumerate(chips)]
    for cp in first:
        cp.start()
    passed = []
    for a in range(n):
        for j, chip in enumerate(chips):
            copy(a, j, part(a, chip, c), part(a, chip, c), (x, y, c)).wait_recv()
            fw = copy(a, 3 + j, part(a, chip, c), part(a, chip, c), (x, y, 1 - c))
            fw.start()
            passed.append(fw)
    for a in range(n):
        for j, chip in enumerate(chips):
            copy(a, 3 + j, part(a, chip, 1 - c), part(a, chip, 1 - c), (x, y, c)).wait_recv()
    for cp in first + passed:
        cp.wait_send()


def _gather_inits(ws, chip_idx):
    return [lax.dynamic_update_slice(jnp.zeros((NCH, *w.shape), w.dtype), w[None], (chip_idx[0], 0, 0, 0)) for w in ws]


def _all_gather_weights(ws, chip_idx):
    n = len(ws)
    inits = _gather_inits(ws, chip_idx)

    def body(*refs):
        _gather_exchange(refs[:n], refs[2 * n:3 * n], refs[3 * n], refs[3 * n + 1])

    return pl.pallas_call(
        body, name="all_gather_weights", out_shape=[jax.ShapeDtypeStruct(i.shape, i.dtype) for i in inits],
        in_specs=[ANY] * (2 * n), out_specs=[ANY] * n, input_output_aliases={n + a: a for a in range(n)},
        scratch_shapes=[pltpu.SemaphoreType.DMA((6 * n,)), pltpu.SemaphoreType.DMA((6 * n,))],
    )(*ws, *inits)


def _all_gather_weights_beside(ws, chip_idx):
    n = len(ws)
    src_refs = [jax.new_ref(w, memory_space=pltpu.MemorySpace.HBM) for w in ws]
    out_refs = [jax.new_ref(i, memory_space=pltpu.MemorySpace.HBM) for i in _gather_inits(ws, chip_idx)]

    @pl.kernel(mesh=plsc.ScalarSubcoreMesh(axis_name="sequencer", num_cores=1), name="all_gather_weights_beside",
               scratch_types=(pltpu.SemaphoreType.DMA((6 * n,)), pltpu.SemaphoreType.DMA((6 * n,))),
               compiler_params=pltpu.CompilerParams(collective_id=1))
    def launch(send_sems, recv_sems):
        x, y, c = _mesh_pos()
        barrier = pltpu.get_barrier_semaphore()
        for peer in [(x, y, 1 - c)] + [(*chip, c) for chip in _other_chips(x, y)]:
            pl.semaphore_signal(barrier, inc=1, device_id=peer, device_id_type=MESH)
        pl.semaphore_wait(barrier, 4)
        _gather_exchange(src_refs, out_refs, send_sems, recv_sems)

    launch()
    return [r[...] for r in out_refs]


def _pair_exchange(name, gs):
    n = len(gs)

    def body(*refs):
        srcs, outs, send_sems, recv_sems = refs[:n], refs[n:2 * n], refs[2 * n], refs[2 * n + 1]
        x, y, c = _mesh_pos()
        cps = []
        for a in range(n):
            h = gs[a].shape[2] // 2
            cps.append(pltpu.make_async_remote_copy(
                src_ref=srcs[a].at[:, :, pl.ds((1 - c) * h, h), :], dst_ref=outs[a], send_sem=send_sems.at[a],
                recv_sem=recv_sems.at[a], device_id=(x, y, 1 - c), device_id_type=MESH))
        for cp in cps:
            cp.start()
        for cp in cps:
            cp.wait()

    return pl.pallas_call(
        body, name=name,
        out_shape=[jax.ShapeDtypeStruct((g.shape[0], g.shape[1], g.shape[2] // 2, g.shape[3]), g.dtype) for g in gs],
        in_specs=[ANY] * n, out_specs=[ANY] * n,
        scratch_shapes=[pltpu.SemaphoreType.DMA((n,)), pltpu.SemaphoreType.DMA((n,))],
    )(*gs)


def _chip_exchange_copies(srcs, outs, send_sems, recv_sems):
    x, y, c = _mesh_pos()
    cps = [pltpu.make_async_remote_copy(
        src_ref=srcs[a].at[:, 2 * chip[0] + chip[1]], dst_ref=outs[a].at[j], send_sem=send_sems.at[3 * a + j],
        recv_sem=recv_sems.at[3 * a + j], device_id=(*chip, c), device_id_type=MESH)
        for a in range(len(srcs)) for j, chip in enumerate(_other_chips(x, y))]
    for cp in cps:
        cp.start()
    for cp in cps:
        cp.wait()


def _chip_exchange_shapes(hs):
    return [jax.ShapeDtypeStruct((3, h.shape[0], h.shape[2], h.shape[3]), h.dtype) for h in hs]


def _chip_exchange(hs):
    n = len(hs)

    def body(*refs):
        _chip_exchange_copies(refs[:n], refs[n:2 * n], refs[2 * n], refs[2 * n + 1])

    return pl.pallas_call(
        body, name="grad_chip_exchange", out_shape=_chip_exchange_shapes(hs),
        in_specs=[ANY] * n, out_specs=[ANY] * n,
        scratch_shapes=[pltpu.SemaphoreType.DMA((3 * n,)), pltpu.SemaphoreType.DMA((3 * n,))],
    )(*hs)


def _chip_exchange_beside(hs):
    n = len(hs)
    src_refs = [jax.new_ref(h, memory_space=pltpu.MemorySpace.HBM) for h in hs]
    out_refs = [jax.empty_ref(s, memory_space=pltpu.MemorySpace.HBM) for s in _chip_exchange_shapes(hs)]

    @pl.kernel(mesh=plsc.ScalarSubcoreMesh(axis_name="sequencer", num_cores=1), name="grad_chip_exchange_beside",
               scratch_types=(pltpu.SemaphoreType.DMA((3 * n,)), pltpu.SemaphoreType.DMA((3 * n,))),
               compiler_params=pltpu.CompilerParams(collective_id=2))
    def launch(send_sems, recv_sems):
        x, y, c = _mesh_pos()
        barrier = pltpu.get_barrier_semaphore()
        for chip in _other_chips(x, y):
            pl.semaphore_signal(barrier, inc=1, device_id=(*chip, c), device_id_type=MESH)
        pl.semaphore_wait(barrier, 3)
        _chip_exchange_copies(src_refs, out_refs, send_sems, recv_sems)

    launch()
    return [r[...] for r in out_refs]


def _pair_share(rs):
    n = len(rs)

    def body(*refs):
        outs, send_sems, recv_sems = refs[n:2 * n], refs[2 * n], refs[2 * n + 1]
        x, y, c = _mesh_pos()
        cps = [pltpu.make_async_remote_copy(src_ref=outs[a].at[:, c], dst_ref=outs[a].at[:, c], send_sem=send_sems.at[a],
                                            recv_sem=recv_sems.at[a], device_id=(x, y, 1 - c), device_id_type=MESH)
               for a in range(n)]
        for cp in cps:
            cp.start()
        for a in range(n):
            pltpu.make_async_remote_copy(src_ref=outs[a].at[:, 1 - c], dst_ref=outs[a].at[:, 1 - c],
                                         send_sem=send_sems.at[a], recv_sem=recv_sems.at[a], device_id=(x, y, c),
                                         device_id_type=MESH).wait_recv()
        for cp in cps:
            cp.wait_send()

    return pl.pallas_call(
        body, name="grad_pair_share", out_shape=[jax.ShapeDtypeStruct(r.shape, r.dtype) for r in rs],
        in_specs=[ANY] * n, out_specs=[ANY] * n, input_output_aliases={a: a for a in range(n)},
        scratch_shapes=[pltpu.SemaphoreType.DMA((n,)), pltpu.SemaphoreType.DMA((n,))],
    )(*rs)


def _pair_sum(name, g, recv, c_idx):
    d, k, h, b = recv.shape
    g5 = g.reshape(d * k, 2, h, b)

    def body(c_ref, a_ref, b_ref, o_ref):
        o_ref[...] = (a_ref[...].astype(F32) + b_ref[...].astype(F32)).astype(o_ref.dtype)

    out = pl.pallas_call(
        body, name=name,
        grid_spec=pltpu.PrefetchScalarGridSpec(
            num_scalar_prefetch=1, grid=(d * k,),
            in_specs=[pl.BlockSpec((None, None, h, b), lambda i, c: (i, c[0], 0, 0)),
                      pl.BlockSpec((None, h, b), lambda i, c: (i, 0, 0))],
            out_specs=pl.BlockSpec((None, h, b), lambda i, c: (i, 0, 0))),
        out_shape=jax.ShapeDtypeStruct((d * k, h, b), BF16),
        compiler_params=_cp(("arbitrary",)),
    )(c_idx, g5, recv.reshape(d * k, h, b))
    return out.reshape(d, k, h, b)


def _chip_sum(name, ha, recv, chip_idx, c_idx, depth, l0, into=None):
    d, _, h, b = ha.shape

    def body(k_ref, c_ref, a_ref, r0, r1, r2, *rest):
        rest[-1][...] = ((a_ref[...].astype(F32) + r0[...].astype(F32)) + r1[...].astype(F32)) + r2[...].astype(F32)

    blk = (None, None, h, b)
    extra = [] if into is None else [into]
    return pl.pallas_call(
        body, name=name,
        grid_spec=pltpu.PrefetchScalarGridSpec(
            num_scalar_prefetch=2, grid=(d,),
            in_specs=[pl.BlockSpec(blk, lambda l, k, c: (l, k[0], 0, 0))] +
                     [pl.BlockSpec(blk, lambda l, k, c, j=j: (j, l, 0, 0)) for j in range(3)] + [ANY] * len(extra),
            out_specs=pl.BlockSpec(blk, lambda l, k, c: (l0 + l, c[0], 0, 0))),
        out_shape=jax.ShapeDtypeStruct((depth, 2, h, b), F32),
        input_output_aliases={} if into is None else {6: 0},
        compiler_params=_cp(("arbitrary",)),
    )(chip_idx, c_idx, ha, recv, recv, recv, *extra)


def _all_sum_small(name, vec):
    rows = vec.shape[0]

    def body(v_ref, o_ref, buf, send_sems, recv_sems):
        x, y, c = _mesh_pos()
        me, sibling = (x, y, c), (x, y, 1 - c)
        chips = _other_chips(x, y)

        def slot(p):
            return buf.at[4 * p[0] + 2 * p[1] + p[2]]

        def copy(k, block, to, src=None):
            return pltpu.make_async_remote_copy(src_ref=slot(block) if src is None else src, dst_ref=slot(block),
                                                send_sem=send_sems.at[k], recv_sem=recv_sems.at[k],
                                                device_id=to, device_id_type=MESH)

        first = [copy(0, me, sibling, src=v_ref)]
        first += [copy(1 + j, me, (*chip, c), src=v_ref) for j, chip in enumerate(chips)]
        for cp in first:
            cp.start()
        passed = [copy(4 + j, (*chip, c), sibling) for j, chip in enumerate(chips)]
        for j, chip in enumerate(chips):
            copy(1 + j, (*chip, c), me).wait_recv()
            passed[j].start()
        copy(0, sibling, me).wait_recv()
        for j, chip in enumerate(chips):
            copy(4 + j, (*chip, 1 - c), me).wait_recv()
        for cp in first + passed:
            cp.wait_send()
        slot(me)[...] = v_ref[...]
        acc = buf[0]
        for k in range(1, 8):
            acc = acc + buf[k]
        o_ref[...] = acc

    vm = pl.BlockSpec(memory_space=pltpu.VMEM)
    return pl.pallas_call(
        body, name=name, out_shape=jax.ShapeDtypeStruct((rows, LANES), F32),
        in_specs=[vm], out_specs=vm, compiler_params=pltpu.CompilerParams(vmem_limit_bytes=VMEM_LIMIT),
        scratch_shapes=[pltpu.VMEM((8, rows, LANES), F32), pltpu.SemaphoreType.DMA((7,)), pltpu.SemaphoreType.DMA((7,))],
    )(vec)


def _pad_lanes(v, n=LANES):
    return jnp.pad(v, (0, n - v.shape[0]))[None, :]


def _w_in_to_kernel(w):
    return jnp.concatenate([w[:, :6656], w[:, 9760:N_IN], w[:, 6656:9728], w[:, 9728:9760],
                            jnp.zeros((w.shape[0], NP - N_IN), w.dtype)], axis=1)


def _w_in_from_kernel(w):
    return jnp.concatenate([w[:, :6656], w[:, X0:DT0], w[:, DT0:DT0 + 32], w[:, G0:X0]], axis=1)


def _layer_params(big, small, i):
    p = {k: big[k][i][0] for k in GRAD_BUFS}
    li = big["w_in"][i][1]
    p["layer"], p["global_layer"] = li, i
    w_in = big["w_in"][i][0]
    cw = w_in.shape[-1]
    pieces = []
    for lo, hi in ((0, 6656), (9760, N_IN), (6656, 9728), (9728, 9760)):
        for k in range(NCH):
            a, b = max(lo, k * cw), min(hi, (k + 1) * cw)
            if a < b:
                pieces.append(w_in[k, li, :, a - k * cw:b - k * cw])
    p["w_in"] = jnp.concatenate(pieces + [jnp.zeros((D_MODEL, NP - N_IN), w_in.dtype)], axis=1)
    p["conv_w"] = big["conv_w"][i][:, None, :]
    for k in ("ffn1_norm_w", "mix_norm_w", "ffn2_norm_w", "b_gates", "conv_b", "ssm_norm_w"):
        p[k] = small[k][i][None, :]
    for k in ("dt_bias", "a_log", "d_skip"):
        p[k] = _pad_lanes(small[k][i])
    p["qkw"] = jnp.stack([_pad_lanes(small["q_norm_w"][i]), _pad_lanes(small["k_norm_w"][i])])
    return p


GRAD_BUFS = ("ffn1_w_gate", "ffn1_w_up", "ffn1_w_down", "w_att_proj", "w_ssm_proj", "w_out",
             "ffn2_w_gate", "ffn2_w_up", "ffn2_w_down")


def _local_step(x, tgt, layers, c, exchange_rest=None):
    depth = len(layers)
    ffn = {f: tuple(f + s for s in ("_w_gate", "_w_up", "_w_down")) for f in ("ffn1", "ffn2")}
    saved = []
    for i, p in enumerate(layers):
        x, s1 = _ffn_fwd(f"L{i}_ffn1", x, p["ffn1_norm_w"], *[p[n] for n in ffn["ffn1"]], p["layer"])
        x, s2 = _mixer_fwd(f"L{i}_mix", x, p, c)
        x, s3 = _ffn_fwd(f"L{i}_ffn2", x, p["ffn2_norm_w"], *[p[n] for n in ffn["ffn2"]], p["layer"])
        saved.append((s1, s2, s3))
    dx, loss_blk = _loss("loss", x, tgt)

    def new_bufs(nl):
        out = {}
        for n in GRAD_BUFS:
            a, b = layers[0][n].shape[-2:]
            out[n] = jnp.zeros((nl, NCH * a, b) if n in ("w_ssm_proj", "w_out") else (nl, NCH, a, b), BF16)
        return out

    def finished(buf):
        out = dict(buf)
        for n in ("w_ssm_proj", "w_out"):
            a, b = layers[0][n].shape[-2:]
            out[n] = buf[n].reshape(-1, NCH, a, b)
        return out

    bufs = [new_bufs(1), new_bufs(depth - 1) if depth > 1 else None]
    grads = [None] * depth
    rest_out = None
    for i in reversed(range(depth)):
        p = layers[i]
        s1, s2, s3 = saved[i]
        buf, gl = (bufs[0], 0) if i == 0 else (bufs[1], i - 1)
        p = dict(p, global_layer=gl)

        def ffn_bwd(f, s, dy):
            names = ffn[f]
            d, dn, new = _ffn_bwd(f"L{i}_{f}", s, p[f + "_norm_w"], *[p[n] for n in names], p["layer"], dy,
                                  tuple(buf[n] for n in names), gl)
            buf.update(zip(names, new))
            return d, dn

        dx, dn2 = ffn_bwd("ffn2", s3, dx)
        dx, gr = _mixer_bwd(f"L{i}_mix", s2, p, c, dx, (buf["w_att_proj"], buf["w_ssm_proj"], buf["w_out"]))
        buf["w_att_proj"], buf["w_ssm_proj"], buf["w_out"] = gr.pop("bufs")
        dx, dn1 = ffn_bwd("ffn1", s1, dx)
        gr.update(ffn1_norm_w=dn1, ffn2_norm_w=dn2)
        grads[i] = gr
        if i == 1 and exchange_rest is not None:
            rest_out, ready = exchange_rest(finished(bufs[1]), grads[1:])
            bits = lax.bitcast_convert_type(ready, jnp.uint16).astype(jnp.int32)
            dx = dx + jnp.minimum(bits, 0).astype(F32)
    return loss_blk, dx, grads, finished(bufs[0]), rest_out


WEIGHTS = ["ffn1_norm_w", "ffn1_w_gate", "ffn1_w_up", "ffn1_w_down", "mix_norm_w", "w_in", "b_gates", "q_norm_w",
           "k_norm_w", "conv_w", "conv_b", "dt_bias", "a_log", "d_skip", "ssm_norm_w", "w_att_proj", "w_ssm_proj",
           "w_out", "ffn2_norm_w", "ffn2_w_gate", "ffn2_w_up", "ffn2_w_down"]
SHARD_AXIS = {"ffn1_w_gate": 2, "ffn1_w_up": 2, "ffn1_w_down": 1, "w_in": 2, "conv_w": 2, "w_att_proj": 2,
              "w_ssm_proj": 1, "w_out": 1, "ffn2_w_gate": 2, "ffn2_w_up": 2, "ffn2_w_down": 1}
BIG = [n for n in WEIGHTS if n in SHARD_AXIS]
SMALL = [n for n in WEIGHTS if n not in SHARD_AXIS]
def _from_flat(flat, shapes):
    v = flat.reshape(-1)
    out, off = [], 0
    for s in shapes:
        n = math.prod(s)
        out.append(v[off:off + n].reshape(s))
        off += n
    return out


def _pack_small(parts):
    v = jnp.concatenate([p.astype(F32).reshape(-1) for p in parts])
    rows = -(-v.shape[0] // (8 * LANES)) * 8
    return jnp.pad(v, (0, rows * LANES - v.shape[0])).reshape(rows, LANES)


def kernel(x, ffn1_norm_w, ffn1_w_gate, ffn1_w_up, ffn1_w_down, mix_norm_w, w_in, b_gates, q_norm_w, k_norm_w, conv_w, conv_b, dt_bias, a_log, d_skip, ssm_norm_w, w_att_proj, w_ssm_proj, w_out, ffn2_norm_w, ffn2_w_gate, ffn2_w_up, ffn2_w_down, loss_target, m_ffn1_norm_w, m_ffn1_w_gate, m_ffn1_w_up, m_ffn1_w_down, m_mix_norm_w, m_w_in, m_b_gates, m_q_norm_w, m_k_norm_w, m_conv_w, m_conv_b, m_dt_bias, m_a_log, m_d_skip, m_ssm_norm_w, m_w_att_proj, m_w_ssm_proj, m_w_out, m_ffn2_norm_w, m_ffn2_w_gate, m_ffn2_w_up, m_ffn2_w_down, v_ffn1_norm_w, v_ffn1_w_gate, v_ffn1_w_up, v_ffn1_w_down, v_mix_norm_w, v_w_in, v_b_gates, v_q_norm_w, v_k_norm_w, v_conv_w, v_conv_b, v_dt_bias, v_a_log, v_d_skip, v_ssm_norm_w, v_w_att_proj, v_w_ssm_proj, v_w_out, v_ffn2_norm_w, v_ffn2_w_gate, v_ffn2_w_up, v_ffn2_w_down):
    w = dict(zip(WEIGHTS, (ffn1_norm_w, ffn1_w_gate, ffn1_w_up, ffn1_w_down, mix_norm_w, w_in, b_gates, q_norm_w, k_norm_w, conv_w, conv_b, dt_bias, a_log, d_skip, ssm_norm_w, w_att_proj, w_ssm_proj, w_out, ffn2_norm_w, ffn2_w_gate, ffn2_w_up, ffn2_w_down)))
    m = dict(zip(WEIGHTS, (m_ffn1_norm_w, m_ffn1_w_gate, m_ffn1_w_up, m_ffn1_w_down, m_mix_norm_w, m_w_in, m_b_gates, m_q_norm_w, m_k_norm_w, m_conv_w, m_conv_b, m_dt_bias, m_a_log, m_d_skip, m_ssm_norm_w, m_w_att_proj, m_w_ssm_proj, m_w_out, m_ffn2_norm_w, m_ffn2_w_gate, m_ffn2_w_up, m_ffn2_w_down)))
    v = dict(zip(WEIGHTS, (v_ffn1_norm_w, v_ffn1_w_gate, v_ffn1_w_up, v_ffn1_w_down, v_mix_norm_w, v_w_in, v_b_gates, v_q_norm_w, v_k_norm_w, v_conv_w, v_conv_b, v_dt_bias, v_a_log, v_d_skip, v_ssm_norm_w, v_w_att_proj, v_w_ssm_proj, v_w_out, v_ffn2_norm_w, v_ffn2_w_gate, v_ffn2_w_up, v_ffn2_w_down)))
    depth = ffn1_norm_w.shape[0]
    bl = x.shape[0]
    t = bl * SEQ
    mx, my, mc = lax.axis_index("x"), lax.axis_index("y"), lax.axis_index("c")
    c_idx = mc.astype(jnp.int32).reshape(1)
    chip_idx = (2 * mx + my).astype(jnp.int32).reshape(1)

    cw_width = conv_w.shape[2]
    slots = lax.dynamic_update_slice(jnp.zeros((NCH, *conv_w.shape), F32), jnp.where(mc == 0, conv_w, 0.0)[None],
                                     (chip_idx[0], 0, 0, 0))
    conv_all = _all_sum_small("conv_gather", slots.reshape(-1, LANES)).reshape(NCH, *conv_w.shape)
    big = {"conv_w": jnp.concatenate([conv_all[k] for k in range(NCH)], axis=2)}

    mm_names = [n for n in BIG if n != "conv_w"]
    first = _all_gather_weights([w[n][:1].astype(BF16) for n in mm_names], chip_idx)
    bits = lax.bitcast_convert_type(first[0][0, 0, 0, 0], jnp.uint16).astype(jnp.int32)
    zero = jnp.minimum(bits, 0).astype(F32)
    rest = _all_gather_weights_beside([(w[n][1:] + zero).astype(BF16) for n in mm_names], chip_idx) if depth > 1 else first
    for n, a0, a1 in zip(mm_names, first, rest):
        big[n] = [(a0, 0)] + [(a1, i - 1) for i in range(1, depth)]
    small = {n: w[n] for n in SMALL}

    def exchange(tag, buf, layer_grads, beside):
        buf = dict(buf)
        buf["w_in"] = jnp.stack([_w_in_from_kernel(g["w_in"]).reshape(D_MODEL, NCH, -1).transpose(1, 0, 2)
                                 for g in layer_grads])
        mine = [buf[n] for n in mm_names]
        from_sibling = _pair_exchange("grad_pair_exchange_" + tag, mine)
        pairs = [_pair_sum(f"grad_pair_sum_{tag}_{n}", g, r, c_idx) for n, g, r in zip(mm_names, mine, from_sibling)]
        return pairs, (_chip_exchange_beside if beside else _chip_exchange)(pairs)

    c = _constants()
    layers = [_layer_params(big, small, i) for i in range(depth)]
    def exchange_rest(buf, layer_grads):
        pairs, recv = exchange("rest", buf, layer_grads, True)
        return (pairs, recv), sum(p[0, 0, 0, 0] for p in pairs)

    loss_blk, dx, grads, buf0, rest_out = _local_step(
        x.reshape(t, D_MODEL), loss_target.reshape(t, D_MODEL), layers, c, exchange_rest if depth > 1 else None)
    grad_x = dx.reshape(bl, SEQ, D_MODEL)
    pairs0, recv0 = exchange("first", buf0, grads[:1], False)
    halves = []
    for j, n in enumerate(mm_names):
        acc = None
        if depth > 1:
            acc = _chip_sum("grad_chip_sum_rest_" + n, rest_out[0][j], rest_out[1][j], chip_idx, c_idx, depth, 1)
        halves.append(_chip_sum("grad_chip_sum_first_" + n, pairs0[j], recv0[j], chip_idx, c_idx, depth, 0, into=acc))
    g_big = {n: r.reshape(w[n].shape) for n, r in zip(mm_names, _pair_share(halves))}

    def small_grad(n):
        if n == "q_norm_w":
            return jnp.stack([g["qkw"][0, 0, :64] for g in grads])
        if n == "k_norm_w":
            return jnp.stack([g["qkw"][1, 0, :64] for g in grads])
        return jnp.stack([g[n][0, :w[n].shape[1]] for g in grads])

    small_shapes = [w[n].shape for n in SMALL]
    conv_shape = (depth, conv_w.shape[1], NCH * cw_width)
    tot = _all_sum_small("small_all_sum", _pack_small(
        [small_grad(n) for n in SMALL] + [jnp.stack([g["conv_w"] for g in grads]), loss_blk[0, :1]]))
    unpacked = _from_flat(tot, small_shapes + [conv_shape, (1,)])
    g_small = dict(zip(SMALL, unpacked[:-2]))
    g_big["conv_w"] = lax.dynamic_slice_in_dim(unpacked[-2], chip_idx[0] * cw_width, cw_width, axis=2)
    loss = unpacked[-1][0]

    grad, delta, new_m, new_v = {}, {}, {}, {}
    for n in BIG:
        shp = w[n].shape
        two_d = (shp[0] * shp[1], shp[2])
        d_, m_, v_ = _adamw("adamw_" + n, w[n].reshape(two_d), g_big[n].reshape(two_d), m[n].reshape(two_d), v[n].reshape(two_d))
        grad[n], delta[n], new_m[n], new_v[n] = g_big[n], d_.reshape(shp), m_.reshape(shp), v_.reshape(shp)
    d_, m_, v_ = _adamw("adamw_small", _pack_small([w[n] for n in SMALL]), _pack_small([g_small[n] for n in SMALL]),
                        _pack_small([m[n] for n in SMALL]), _pack_small([v[n] for n in SMALL]))
    for n, a, b, c_ in zip(SMALL, _from_flat(d_, small_shapes), _from_flat(m_, small_shapes), _from_flat(v_, small_shapes)):
        grad[n], delta[n], new_m[n], new_v[n] = g_small[n], a, b, c_
    return (loss, grad_x, *[grad[n] for n in WEIGHTS], *[delta[n] for n in WEIGHTS],
            *[new_m[n] for n in WEIGHTS], *[new_v[n] for n in WEIGHTS])
```

```python
import functools
import math

import numpy as np
import jax
import jax.numpy as jnp
from jax import lax
from jax.experimental import pallas as pl
from jax.experimental.pallas import tpu as pltpu
from jax.experimental.pallas import tpu_sc as plsc

F32 = jnp.float32
BF16 = jnp.bfloat16
HI = lax.Precision.HIGHEST
MESH = pl.DeviceIdType.MESH

D_MODEL = 1024
SEQ = 2048
DEPTH = 4
D_FF = 2816
ATT_DILATIONS = (1, 4, 16)
BAND = 128
ATT_OUT = 512
QKV = 1536
D_INNER = 2048
N_SSM_HEADS = 32
N_SSM_GROUPS = 4
D_STATE = 128
XBC = 3072
CHUNK = 128
N_IN = 11808
EPS = 1e-6
ROPE_THETA = 10000.0
NP = 12288
Q0, K0, V0, Z0, G0, X0, DT0 = 0, 1536, 3072, 4608, 6656, 8704, 11776
DTW = 128
LR, B1, B2, ADAM_EPS, WD, STEP = 0.001, 0.9, 0.999, 1e-08, 0.01, 10

LANES = 128
VMEM_LIMIT = 48 * 1024 * 1024
NEG = -1e30


def _cp(sem=None, **kw):
    return pltpu.CompilerParams(dimension_semantics=sem, vmem_limit_bytes=VMEM_LIMIT, **kw)


def _dg(a, b, ca, cb):
    return lax.dot_general(a.astype(BF16), b.astype(BF16), (((ca,), (cb,)), ((), ())), preferred_element_type=F32)


@jax.custom_vjp
def dot_nn(a, b):
    return _dg(a, b, 1, 0)


def _dot_nn_fwd(a, b):
    return _dg(a, b, 1, 0), (a, b)


def _dot_nn_bwd(r, g):
    a, b = r
    return _dg(g, b, 1, 1).astype(a.dtype), _dg(a, g, 0, 0).astype(b.dtype)


dot_nn.defvjp(_dot_nn_fwd, _dot_nn_bwd)


@jax.custom_vjp
def dot_nt(a, b):
    return _dg(a, b, 1, 1)


def _dot_nt_fwd(a, b):
    return _dg(a, b, 1, 1), (a, b)


def _dot_nt_bwd(r, g):
    a, b = r
    return _dg(g, b, 1, 0).astype(a.dtype), _dg(g, a, 0, 0).astype(b.dtype)


dot_nt.defvjp(_dot_nt_fwd, _dot_nt_bwd)


@jax.custom_vjp
def dot_tn(a, b):
    return _dg(a, b, 0, 0)


def _dot_tn_fwd(a, b):
    return _dg(a, b, 0, 0), (a, b)


def _dot_tn_bwd(r, g):
    a, b = r
    return _dg(b, g, 1, 1).astype(a.dtype), _dg(a, g, 1, 0).astype(b.dtype)


dot_tn.defvjp(_dot_tn_fwd, _dot_tn_bwd)


def _dot2_raw(a, e, ce):
    hi = a.astype(BF16)
    lo = (a - hi.astype(F32)).astype(BF16)
    return _dg(hi, e, 1, ce) + _dg(lo, e, 1, ce)


@jax.custom_vjp
def dot2(a, e):
    return _dot2_raw(a, e, 0)


def _dot2_fwd(a, e):
    return _dot2_raw(a, e, 0), e


def _dot2_bwd(e, g):
    return _dot2_raw(g, e, 1), jnp.zeros_like(e)


dot2.defvjp(_dot2_fwd, _dot2_bwd)


def _tri2_raw(l, x, cl):
    hi = x.astype(BF16)
    lo = (x - hi.astype(F32)).astype(BF16)
    return _dg(l, hi, cl, 0) + _dg(l, lo, cl, 0)


@jax.custom_vjp
def tri_matmul(l, x):
    return _tri2_raw(l, x, 1)


def _tri_fwd(l, x):
    return _tri2_raw(l, x, 1), l


def _tri_bwd(l, g):
    return jnp.zeros_like(l), _tri2_raw(l, g, 0)


tri_matmul.defvjp(_tri_fwd, _tri_bwd)


def _dup64_raw(w):
    return w + pltpu.roll(w, 64, 1)


@jax.custom_vjp
def dup64(w):
    return _dup64_raw(w)


def _dup64_fwd(w):
    return _dup64_raw(w), None


def _dup64_bwd(_, g):
    lane = lax.broadcasted_iota(jnp.int32, g.shape, 1)
    return (jnp.where(lane < 64, _dup64_raw(g), 0.0),)


dup64.defvjp(_dup64_fwd, _dup64_bwd)


def _rope_rot_raw(y, sign):
    lane = lax.broadcasted_iota(jnp.int32, y.shape, 1)
    first_half = (lane & 32) == 0
    return sign * jnp.where(first_half, -pltpu.roll(y, LANES - 32, 1), pltpu.roll(y, 32, 1))


@jax.custom_vjp
def rope_rot(y):
    return _rope_rot_raw(y, 1.0)


def _rope_rot_fwd(y):
    return _rope_rot_raw(y, 1.0), None


def _rope_rot_bwd(_, g):
    return (_rope_rot_raw(g, -1.0),)


rope_rot.defvjp(_rope_rot_fwd, _rope_rot_bwd)


def _shift_rows_raw(x, s):
    n = x.shape[0]
    r = pltpu.roll(x, s % n, 0)
    rows = lax.broadcasted_iota(jnp.int32, x.shape, 0)
    keep = rows >= s if s > 0 else rows < n + s
    return jnp.where(keep, r, 0.0)


@functools.partial(jax.custom_vjp, nondiff_argnums=(1,))
def shift_rows(x, s):
    return _shift_rows_raw(x, s)


def _shift_fwd(x, s):
    return _shift_rows_raw(x, s), None


def _shift_bwd(s, _, g):
    return (_shift_rows_raw(g, -s),)


shift_rows.defvjp(_shift_fwd, _shift_bwd)


def _sigmoid(x):
    return 1.0 / (1.0 + jnp.exp(-x))


def _silu(x):
    return x * _sigmoid(x)


def _softplus(x):
    return jnp.maximum(x, 0.0) + jnp.log(1.0 + jnp.exp(-jnp.abs(x)))


def _head_mean_mat():
    i = np.arange(LANES)
    return jnp.asarray((i[:, None] // 64 == i[None, :] // 64).astype(np.float32) / 64.0)


def _head_expand_mat():
    e = np.zeros((LANES, D_INNER), np.float32)
    for l in range(D_INNER):
        e[l // 64, l] = 1.0
    return jnp.asarray(e)


def _ltri_mat():
    i = np.arange(CHUNK)
    return jnp.asarray((i[:, None] >= i[None, :]).astype(np.float32))


def _rope_tables():
    pos = jnp.arange(SEQ, dtype=F32)
    inv_freq = 1.0 / (ROPE_THETA ** (jnp.arange(0, 64, 2, dtype=F32) / 64))
    ang = pos[:, None] * inv_freq[None, :]
    return jnp.tile(jnp.cos(ang), (1, 4)), jnp.tile(jnp.sin(ang), (1, 4))


def _pick(n, cap):
    best = None
    for t in range(LANES, min(n, cap) + 1, LANES):
        if n % t == 0:
            best = t
    return best if best is not None else n


def _mm(name, a, b, mode, out_dtype=F32, alpha=None, res=None):
    if mode == "nn":
        (m, k), n = a.shape, b.shape[1]
    elif mode == "nt":
        (m, k), n = a.shape, b.shape[0]
    else:
        (k, m), n = a.shape, b.shape[1]
    tm, tn, tk = _pick(m, 1408), _pick(n, 1408), _pick(k, 1024)
    nk = k // tk
    ca, cb = {"nn": (1, 0), "nt": (1, 1), "tn": (0, 0)}[mode]
    a_spec = pl.BlockSpec((tk, tm), lambda i, j, kk: (kk, i)) if mode == "tn" else pl.BlockSpec((tm, tk), lambda i, j, kk: (i, kk))
    b_spec = pl.BlockSpec((tn, tk), lambda i, j, kk: (j, kk)) if mode == "nt" else pl.BlockSpec((tk, tn), lambda i, j, kk: (kk, j))
    o_spec = pl.BlockSpec((tm, tn), lambda i, j, kk: (i, j))
    has_res = res is not None

    def finish(acc, res_ref, o_ref):
        if alpha is not None:
            acc = acc * alpha
        if has_res:
            acc = acc + res_ref[...].astype(F32)
        o_ref[...] = acc.astype(o_ref.dtype)

    def body(*refs):
        a_ref, b_ref = refs[0], refs[1]
        res_ref = refs[2] if has_res else None
        o_ref = refs[3] if has_res else refs[2]
        part = _dg(a_ref[...], b_ref[...], ca, cb)
        if nk == 1:
            finish(part, res_ref, o_ref)
            return
        acc_ref = refs[-1]
        kk = pl.program_id(2)

        @pl.when(kk == 0)
        def _():
            acc_ref[...] = part

        @pl.when(kk > 0)
        def _():
            acc_ref[...] += part

        @pl.when(kk == nk - 1)
        def _():
            finish(acc_ref[...], res_ref, o_ref)

    ins = [a, b] + ([res] if has_res else [])
    in_specs = [a_spec, b_spec] + ([o_spec] if has_res else [])
    return pl.pallas_call(
        body, name=name, grid=(m // tm, n // tn, nk), in_specs=in_specs, out_specs=o_spec,
        out_shape=jax.ShapeDtypeStruct((m, n), out_dtype),
        scratch_shapes=[pltpu.VMEM((tm, tn), F32)] if nk > 1 else [],
        compiler_params=_cp(("parallel", "parallel", "arbitrary")),
    )(*ins)


def _mmx(name, grid, a, b, out, contract, *, alpha=None, res=None, into=None):
    nk = grid[-1]
    has_res, has_into = res is not None, into is not None
    n_in = 2 + has_res + has_into

    def finish(acc, res_ref, o_ref):
        if alpha is not None:
            acc = acc * alpha
        if has_res:
            acc = acc + res_ref[...].astype(F32)
        o_ref[...] = acc.astype(o_ref.dtype)

    def body(*refs):
        res_ref = refs[2] if has_res else None
        o_ref = refs[n_in]
        part = _dg(refs[0][...], refs[1][...], *contract)
        if nk == 1:
            finish(part, res_ref, o_ref)
            return
        acc_ref = refs[-1]
        kk = pl.program_id(len(grid) - 1)

        @pl.when(kk == 0)
        def _():
            acc_ref[...] = part

        @pl.when(kk > 0)
        def _():
            acc_ref[...] += part

        @pl.when(kk == nk - 1)
        def _():
            finish(acc_ref[...], res_ref, o_ref)

    operands = [a, b] + ([res] if has_res else [])
    in_specs = [pl.BlockSpec(blk, im) for _, blk, im in operands] + ([ANY] if has_into else [])
    acc_shape = tuple(d for d in out[2] if d is not None)
    return pl.pallas_call(
        body, name=name, grid=grid, in_specs=in_specs, out_specs=pl.BlockSpec(out[2], out[3]),
        out_shape=jax.ShapeDtypeStruct(out[0], out[1]),
        scratch_shapes=[pltpu.VMEM(acc_shape, F32)] if nk > 1 else [],
        input_output_aliases={n_in - 1: 0} if has_into else {},
        compiler_params=_cp(("parallel",) * (len(grid) - 1) + ("arbitrary",)),
    )(*[o[0] for o in operands], *([into] if has_into else []))


def _ew(name, fn, grid, ins, outs, scratch=()):
    n_in, n_out = len(ins), len(outs)

    def body(*refs):
        vals = [r[...] for r in refs[:n_in]]
        res = fn(*vals, *refs[n_in + n_out:])
        for r, v in zip(refs[n_in:n_in + n_out], res):
            r[...] = v.astype(r.dtype)

    res = pl.pallas_call(
        body, name=name, grid=grid,
        in_specs=[pl.BlockSpec(b, m) for _, b, m in ins],
        out_specs=[pl.BlockSpec(b, m) for _, _, b, m in outs],
        out_shape=[jax.ShapeDtypeStruct(s, d) for s, d, _, _ in outs],
        scratch_shapes=list(scratch),
        compiler_params=_cp(("arbitrary",) * len(grid)),
    )(*[a for a, _, _ in ins])
    return res


def _ew_bwd(name, fn, grid, ins, cts, wrt, adds=(), ct_fn=None):
    n_in, n_ct, n_add = len(ins), len(cts), len(adds)
    idxs = [w["idx"] for w in wrt]
    intos = [(k, w["into"]) for k, w in enumerate(wrt) if w.get("into") is not None]

    def body(*refs):
        prim = [r[...] for r in refs[:n_in]]
        ct = [r[...].astype(F32) for r in refs[n_in:n_in + n_ct]]
        addv = [r[...] for r in refs[n_in + n_ct:n_in + n_ct + n_add]]
        orefs = refs[n_in + n_ct + n_add + len(intos):]

        def f(*sel):
            full = list(prim)
            for i, s in zip(idxs, sel):
                full[i] = s
            return fn(*full)

        _, vjp = jax.vjp(f, *[prim[i].astype(F32) for i in idxs])
        grads = vjp(tuple(ct) if ct_fn is None else ct_fn(*ct))
        for w, g, r in zip(wrt, grads, orefs):
            if w["kind"] == "tile":
                if w.get("add") is not None:
                    g = g + addv[w["add"]].astype(F32)
                r[...] = g.astype(r.dtype)
            else:
                first = w["first"]()

                @pl.when(first)
                def _(r=r, g=g):
                    r[...] = g.astype(r.dtype)

                @pl.when(jnp.logical_not(first))
                def _(r=r, g=g):
                    r[...] += g.astype(r.dtype)

    allin = list(ins) + list(cts) + list(adds)
    return pl.pallas_call(
        body, name=name, grid=grid,
        in_specs=[pl.BlockSpec(b, m) for _, b, m in allin] + [ANY] * len(intos),
        out_specs=[pl.BlockSpec(w["block"], w["imap"]) for w in wrt],
        out_shape=[jax.ShapeDtypeStruct(w["shape"], w["dtype"]) for w in wrt],
        input_output_aliases={len(allin) + q: k for q, (k, _) in enumerate(intos)},
        compiler_params=_cp(("arbitrary",) * len(grid)),
    )(*[a for a, _, _ in allin], *[a for _, a in intos])


def _rmsnorm_fn(x, w):
    return (x * lax.rsqrt(jnp.mean(x * x, axis=-1, keepdims=True) + EPS) * w,)


def _swiglu_fn(g, u):
    return (_silu(g) * u,)


def _qkprep_fn(t, w64, cos, sin, hmean):
    w = jnp.sum(dup64(jnp.broadcast_to(w64, (8, LANES))), axis=0, keepdims=True) * 0.125
    y = t * lax.rsqrt(dot2(t * t, hmean) + EPS) * w
    return (y * cos + rope_rot(y) * sin,)


def _att_fn(q, kp, kc, vp, vc, first):
    iq = lax.broadcasted_iota(jnp.int32, (BAND, 2 * BAND), 0)
    ik = lax.broadcasted_iota(jnp.int32, (BAND, 2 * BAND), 1)
    rel = BAND + iq - ik
    ok = (rel >= 0) & (rel <= BAND) & ((ik >= BAND) | jnp.logical_not(first))
    lane = lax.broadcasted_iota(jnp.int32, (1, LANES), 1)
    kcat = jnp.concatenate([kp, kc], axis=0)
    vcat = jnp.concatenate([vp, vc], axis=0)
    o_pair = jnp.zeros((BAND, LANES), F32)
    l_pair = jnp.zeros((BAND, LANES), F32)
    for hh in range(2):
        lm = (lane // 64 == hh).astype(F32)
        s = dot_nt(q * lm, kcat) * 0.125
        s = jnp.where(ok, s, NEG)
        mx = jnp.max(s, axis=-1, keepdims=True)
        e = jnp.exp(s - mx)
        den = jnp.sum(e, axis=-1, keepdims=True)
        o_pair = o_pair + dot_nn(e / den, vcat) * lm
        l_pair = l_pair + (mx + jnp.log(den)) * lm
    return o_pair, l_pair


def _attmix_fn(o0, o1, o2, l0, l1, l2):
    m = jnp.maximum(jnp.maximum(l0, l1), l2)
    e0, e1, e2 = jnp.exp(l0 - m), jnp.exp(l1 - m), jnp.exp(l2 - m)
    return ((e0 * o0 + e1 * o1 + e2 * o2) / (e0 + e1 + e2),)


def _conv_fn(x, w0, w1, w2, w3, b):
    pre = x * w3 + shift_rows(x, 1) * w2 + shift_rows(x, 2) * w1 + shift_rows(x, 3) * w0 + b
    return (_silu(pre),)


def _ssdpre_fn(dtraw, bias, alog, ex):
    dt = _softplus(dtraw + bias)
    da = dt * (-jnp.exp(alog))
    return dot2(dt, ex), dot2(da, ex)


def _ssd_step(st, x, dtb, dab, bm, cm, ltri):
    cum = tri_matmul(ltri, dab)
    cum_t = cum.T
    xdt = x * dtb
    cb = dot_nt(cm, bm)
    ri = lax.broadcasted_iota(jnp.int32, (CHUNK, CHUNK), 0)
    ci = lax.broadcasted_iota(jnp.int32, (CHUNK, CHUNK), 1)
    causal = ri >= ci
    lane = lax.broadcasted_iota(jnp.int32, (1, LANES), 1)
    rowi = lax.broadcasted_iota(jnp.int32, (LANES, 1), 0)
    ys = []
    for p in range(4):
        sl = slice(p * LANES, (p + 1) * LANES)
        cum_p, cum_tp, xdt_p = cum[:, sl], cum_t[sl, :], xdt[:, sl]
        acc = jnp.zeros((CHUNK, LANES), F32)
        for hh in range(2):
            col = jnp.sum(cum_p * (lane == 64 * hh).astype(F32), axis=1, keepdims=True)
            row = jnp.sum(cum_tp * (rowi == 64 * hh).astype(F32), axis=0, keepdims=True)
            dec = jnp.exp(jnp.where(causal, col - row, NEG))
            acc = acc + dot_nn(cb * dec, xdt_p * (lane // 64 == hh).astype(F32))
        ys.append(acc)
    y_diag = jnp.concatenate(ys, axis=1)
    y_off = dot_nn(cm, st) * jnp.exp(cum)
    last_row = (lax.broadcasted_iota(jnp.int32, (CHUNK, 1), 0) == CHUNK - 1).astype(F32)
    last = jnp.sum(cum * last_row, axis=0, keepdims=True)
    new_st = st * jnp.exp(last) + dot_tn(bm, xdt * jnp.exp(last - cum))
    return new_st, y_diag + y_off


def _ssdpost_fn(y, xs, z, dskip, ex, nw):
    db = jnp.sum(dot2(jnp.broadcast_to(dskip, (8, LANES)), ex), axis=0, keepdims=True) * 0.125
    y2 = (y + db * xs) * _silu(z)
    return (y2 * lax.rsqrt(jnp.mean(y2 * y2, axis=-1, keepdims=True) + EPS) * nw,)


def _merge_fn(ya, ys, ga, gs, ba, bs):
    return (_sigmoid(ga + ba) * ya + _sigmoid(gs + bs) * ys,)


TM = 512


def _full(shape):
    nd = len(shape)
    return (shape, lambda *_: (0,) * nd)


def _rmsnorm(name, x, w):
    t = x.shape[0]
    return _ew(name, _rmsnorm_fn, (t // TM,),
               [(x, (TM, D_MODEL), lambda i: (i, 0)), (w, (1, D_MODEL), lambda i: (0, 0))],
               [((t, D_MODEL), BF16, (TM, D_MODEL), lambda i: (i, 0))])[0]


def _rmsnorm_bwd(name, x, w, dh, dres):
    t = x.shape[0]
    row = ((TM, D_MODEL), lambda i: (i, 0))
    return _ew_bwd(name, _rmsnorm_fn, (t // TM,),
                   [(x, *row), (w, (1, D_MODEL), lambda i: (0, 0))], [(dh, *row)],
                   [dict(idx=0, kind="tile", shape=(t, D_MODEL), dtype=F32, block=row[0], imap=row[1], add=0),
                    dict(idx=1, kind="acc", shape=(1, D_MODEL), dtype=F32, block=(1, D_MODEL), imap=lambda i: (0, 0),
                         first=lambda: pl.program_id(0) == 0)],
                   adds=[(dres, *row)])


def _qk_operands(proj, qkw, cos, sin, consts, tm):
    nrow = SEQ // tm
    c = ((LANES, LANES), lambda j, i: (0, 0))
    return [(proj, (tm, LANES), lambda j, i: (i, j)),
            (qkw, (None, 1, LANES), lambda j, i: (j // 12, 0, 0)),
            (cos, (tm, LANES), lambda j, i: (i % nrow, 0)),
            (sin, (tm, LANES), lambda j, i: (i % nrow, 0)),
            (consts["hmean"], *c)]


def _qkprep(name, proj, qkw, cos, sin, consts):
    t = proj.shape[0]
    return _ew(name, _qkprep_fn, (2 * QKV // LANES, t // TM), _qk_operands(proj, qkw, cos, sin, consts, TM),
               [((t, 2 * QKV), F32, (TM, LANES), lambda j, i: (i, j))])[0]


def _qkprep_bwd(name, proj, qkw, cos, sin, consts, dq, dk, dproj):
    t = proj.shape[0]
    nq = QKV // LANES

    def pick(cq, ck):
        return (jnp.where(pl.program_id(0) < nq, cq, ck),)

    return _ew_bwd(name, _qkprep_fn, (2 * nq, t // TM), _qk_operands(proj, qkw, cos, sin, consts, TM),
                   [(d, (TM, LANES), lambda j, i: (i, j % nq)) for d in (dq, dk)],
                   [dict(idx=0, kind="tile", shape=dproj.shape, dtype=dproj.dtype, block=(TM, LANES),
                         imap=lambda j, i: (i, j), into=dproj),
                    dict(idx=1, kind="acc", shape=(2, 1, LANES), dtype=F32, block=(None, 1, LANES),
                         imap=lambda j, i: (j // 12, 0, 0),
                         first=lambda: (pl.program_id(0) % 12 == 0) & (pl.program_id(1) == 0))],
                   ct_fn=pick)


def _att_specs(dil, g):
    nb = SEQ // dil // BAND
    pt = 4 if dil == 1 else 1
    w = pt * LANES
    blk = (None, BAND * dil, w)
    kq, kk, kv = g * ATT_OUT // w, (QKV + g * ATT_OUT) // w, (V0 + g * ATT_OUT) // w

    def cur(n):
        return jnp.minimum(n, nb - 1)

    def prev(n):
        return jnp.maximum(jnp.minimum(n, nb - 1) - 1, 0)

    return nb, pt, blk, [
        pl.BlockSpec(blk, lambda b, p, n: (b, cur(n), kq + p)),
        pl.BlockSpec(blk, lambda b, p, n: (b, prev(n), kk + p)),
        pl.BlockSpec(blk, lambda b, p, n: (b, cur(n), kk + p)),
        pl.BlockSpec(blk, lambda b, p, n: (b, prev(n), kv + p)),
        pl.BlockSpec(blk, lambda b, p, n: (b, cur(n), kv + p)),
    ]


def _att_rows(r, dil):
    return pl.ds(r, BAND, stride=dil) if dil > 1 else pl.ds(0, BAND)


def _att_fwd(name, qk, proj, g):
    bl = qk.shape[0] // SEQ
    dil = ATT_DILATIONS[g]
    nb, pt, blk, specs = _att_specs(dil, g)
    qk3 = qk.reshape(bl, SEQ, 2 * QKV)
    proj3 = proj.reshape(bl, SEQ, NP)
    o_spec = pl.BlockSpec(blk, lambda b, p, n: (b, n, p))

    def body(q, kp, kc, vp, vc, o_ref, l_ref):
        first = pl.program_id(2) == 0

        def residue(r, carry):
            sl = _att_rows(r, dil)
            for p in range(pt):
                ln = pl.ds(p * LANES, LANES)
                o, l = _att_fn(q[sl, ln], kp[sl, ln], kc[sl, ln], vp[sl, ln], vc[sl, ln], first)
                o_ref[sl, ln] = o
                l_ref[sl, ln] = l
            return carry

        lax.fori_loop(0, dil, residue, 0)

    o, l = pl.pallas_call(
        body, name=name, grid=(bl, ATT_OUT // (pt * LANES), nb), in_specs=specs, out_specs=[o_spec, o_spec],
        out_shape=[jax.ShapeDtypeStruct((bl, SEQ, ATT_OUT), F32)] * 2,
        compiler_params=_cp(("arbitrary",) * 3),
    )(qk3, qk3, qk3, proj3, proj3)
    return o.reshape(bl * SEQ, ATT_OUT), l.reshape(bl * SEQ, ATT_OUT)


def _att_bwd(name, qk, proj, g, do, dl, dq_buf, dk_buf, dv_buf):
    bl = qk.shape[0] // SEQ
    dil = ATT_DILATIONS[g]
    nb, pt, blk, specs = _att_specs(dil, g)
    w = pt * LANES
    qk3 = qk.reshape(bl, SEQ, 2 * QKV)
    proj3 = proj.reshape(bl, SEQ, NP)
    ct_spec = pl.BlockSpec(blk, lambda b, p, n: (b, jnp.minimum(n, nb - 1), p))
    do3 = do.reshape(bl, SEQ, ATT_OUT)
    dl3 = dl.reshape(bl, SEQ, ATT_OUT)
    kg = g * ATT_OUT // w

    def body(q, kp, kc, vp, vc, do_ref, dl_ref, _a, _b, _c, d_ref, dk_ref, dv_ref, ck, cv):
        n = pl.program_id(2)

        def residue(r, carry):
            sl = _att_rows(r, dil)
            for p in range(pt):
                ln = pl.ds(p * LANES, LANES)

                @pl.when(n < nb)
                def _(ln=ln):
                    first = n == 0
                    prim = [ref[sl, ln] for ref in (q, kp, kc, vp, vc)]
                    _, vjp = jax.vjp(lambda *a: _att_fn(*a, first), *prim)
                    dq, dkp, dkc, dvp, dvc = vjp((do_ref[sl, ln], dl_ref[sl, ln]))
                    d_ref[sl, ln] = dq

                    @pl.when(n > 0)
                    def _():
                        dk_ref[sl, ln] = ck[sl, ln] + dkp
                        dv_ref[sl, ln] = cv[sl, ln] + dvp

                    ck[sl, ln] = dkc
                    cv[sl, ln] = dvc

                @pl.when(n == nb)
                def _(ln=ln):
                    dk_ref[sl, ln] = ck[sl, ln]
                    dv_ref[sl, ln] = cv[sl, ln]

            return carry

        lax.fori_loop(0, dil, residue, 0)

    bufs = [a.reshape(bl, SEQ, QKV) for a in (dq_buf, dk_buf, dv_buf)]
    o_specs = [
        pl.BlockSpec(blk, lambda b, p, n: (b, jnp.minimum(n, nb - 1), kg + p)),
        pl.BlockSpec(blk, lambda b, p, n: (b, jnp.maximum(n - 1, 0), kg + p)),
        pl.BlockSpec(blk, lambda b, p, n: (b, jnp.maximum(n - 1, 0), kg + p)),
    ]
    dq, dk, dv = pl.pallas_call(
        body, name=name, grid=(bl, ATT_OUT // w, nb + 1), in_specs=specs + [ct_spec, ct_spec, ANY, ANY, ANY],
        out_specs=o_specs, out_shape=[jax.ShapeDtypeStruct(a.shape, a.dtype) for a in bufs],
        input_output_aliases={7: 0, 8: 1, 9: 2},
        scratch_shapes=[pltpu.VMEM((BAND * dil, w), F32), pltpu.VMEM((BAND * dil, w), F32)],
        compiler_params=_cp(("arbitrary",) * 3),
    )(qk3, qk3, qk3, proj3, proj3, do3, dl3, *bufs)
    return dq.reshape(dq_buf.shape), dk.reshape(dk_buf.shape), dv.reshape(dv_buf.shape)


def _attmix(name, os_, ls_):
    t = os_[0].shape[0]
    blk = ((TM, ATT_OUT), lambda i: (i, 0))
    return _ew(name, _attmix_fn, (t // TM,), [(a, *blk) for a in (*os_, *ls_)], [((t, ATT_OUT), BF16, *blk)])[0]


def _attmix_bwd(name, os_, ls_, datt):
    t = os_[0].shape[0]
    blk = ((TM, ATT_OUT), lambda i: (i, 0))
    return _ew_bwd(name, _attmix_fn, (t // TM,), [(a, *blk) for a in (*os_, *ls_)], [(datt, *blk)],
                   [dict(idx=k, kind="tile", shape=(t, ATT_OUT), dtype=F32, block=blk[0], imap=blk[1]) for k in range(6)])


CONV_TC = 256


def _conv_operands(proj3, conv_w, conv_b):
    c0 = X0 // CONV_TC
    ins = [(proj3, (None, SEQ, CONV_TC), lambda j, b: (b, 0, c0 + j))]
    for k in range(4):
        ins.append((conv_w, (None, 1, CONV_TC), lambda j, b, k=k: (k, 0, j)))
    ins.append((conv_b, (1, CONV_TC), lambda j, b: (0, j)))
    return ins


def _conv(name, proj3, conv_w, conv_b):
    bl = proj3.shape[0]
    return _ew(name, _conv_fn, (XBC // CONV_TC, bl), _conv_operands(proj3, conv_w, conv_b),
               [((bl, SEQ, XBC), F32, (None, SEQ, CONV_TC), lambda j, b: (b, 0, j))])[0]


def _conv_bwd(name, proj3, conv_w, conv_b, dxs3, db3, dc3, dproj3):
    bl = proj3.shape[0]
    nx = D_INNER // CONV_TC
    nb_ = N_SSM_GROUPS * D_STATE // CONV_TC
    blk = (None, SEQ, CONV_TC)
    cts = [(dxs3, blk, lambda j, b: (b, 0, jnp.minimum(j, nx - 1))),
           (db3, blk, lambda j, b: (b, 0, jnp.clip(j - nx, 0, nb_ - 1))),
           (dc3, blk, lambda j, b: (b, 0, jnp.clip(j - nx - nb_, 0, nb_ - 1)))]

    def pick(cx, cb, cc):
        j = pl.program_id(0)
        return (jnp.where(j < nx, cx, jnp.where(j < nx + nb_, cb, cc)),)

    first = lambda: pl.program_id(1) == 0
    wrt = [dict(idx=0, kind="tile", shape=dproj3.shape, dtype=dproj3.dtype, block=blk,
                imap=lambda j, b: (b, 0, X0 // CONV_TC + j), into=dproj3)]
    for k in range(4):
        wrt.append(dict(idx=1 + k, kind="acc", shape=(1, XBC), dtype=F32, block=(1, CONV_TC),
                        imap=lambda j, b: (0, j), first=first))
    wrt.append(dict(idx=5, kind="acc", shape=(1, XBC), dtype=F32, block=(1, CONV_TC), imap=lambda j, b: (0, j), first=first))
    return _ew_bwd(name, _conv_fn, (XBC // CONV_TC, bl), _conv_operands(proj3, conv_w, conv_b), cts, wrt, ct_fn=pick)


SSD_TM = 256


def _ssdpre_operands(proj, dt_bias, a_log, ex):
    return [(proj, (SSD_TM, DTW), lambda i: (i, DT0 // DTW)), (dt_bias, *_full((1, DTW))), (a_log, *_full((1, DTW))),
            (ex, *_full((LANES, D_INNER)))]


def _ssdpre(name, proj, dt_bias, a_log, ex):
    t = proj.shape[0]
    blk = ((SSD_TM, D_INNER), lambda i: (i, 0))
    return _ew(name, _ssdpre_fn, (t // SSD_TM,), _ssdpre_operands(proj, dt_bias, a_log, ex),
               [((t, D_INNER), F32, *blk), ((t, D_INNER), F32, *blk)])


def _ssdpre_bwd(name, proj, dt_bias, a_log, ex, ddtb, ddab):
    t = proj.shape[0]
    blk = ((SSD_TM, D_INNER), lambda i: (i, 0))
    first = lambda: pl.program_id(0) == 0
    return _ew_bwd(name, _ssdpre_fn, (t // SSD_TM,), _ssdpre_operands(proj, dt_bias, a_log, ex),
                   [(ddtb, *blk), (ddab, *blk)],
                   [dict(idx=0, kind="tile", shape=(t, DTW), dtype=BF16, block=(SSD_TM, DTW), imap=lambda i: (i, 0)),
                    dict(idx=1, kind="acc", shape=(1, DTW), dtype=F32, block=(1, DTW), imap=lambda i: (0, 0), first=first),
                    dict(idx=2, kind="acc", shape=(1, DTW), dtype=F32, block=(1, DTW), imap=lambda i: (0, 0), first=first)])


def _ssd_in_specs(rev):
    nc = SEQ // CHUNK

    def c_(c):
        return nc - 1 - c if rev else c

    wide = (None, CHUNK, D_INNER)
    nar = (None, CHUNK, N_SSM_GROUPS * D_STATE)
    nb_ = D_INNER // (N_SSM_GROUPS * D_STATE)
    return [
        pl.BlockSpec(wide, lambda b, c: (b, c_(c), 0)),
        pl.BlockSpec(wide, lambda b, c: (b, c_(c), 0)),
        pl.BlockSpec(wide, lambda b, c: (b, c_(c), 0)),
        pl.BlockSpec(nar, lambda b, c: (b, c_(c), nb_)),
        pl.BlockSpec(nar, lambda b, c: (b, c_(c), nb_ + 1)),
        pl.BlockSpec((CHUNK, CHUNK), lambda b, c: (0, 0)),
    ], c_


def _ssd_cols(g):
    return pl.ds(g * 4 * LANES, 4 * LANES), pl.ds(g * D_STATE, D_STATE)


def _ssd_fwd(name, xc3, dtb3, dab3, ltri):
    bl = xc3.shape[0]
    nc = SEQ // CHUNK
    specs, _ = _ssd_in_specs(False)

    def body(x, dtb, dab, bm, cm, lt, y_ref, st_ref, st):
        @pl.when(pl.program_id(1) == 0)
        def _():
            st[...] = jnp.zeros_like(st)

        ltv = lt[...]
        for g in range(N_SSM_GROUPS):
            wl, nl = _ssd_cols(g)
            s0 = st[g]
            st_ref[g] = s0
            new_st, y = _ssd_step(s0, x[:, wl], dtb[:, wl], dab[:, wl], bm[:, nl], cm[:, nl], ltv)
            y_ref[:, wl] = y
            st[g] = new_st

    return pl.pallas_call(
        body, name=name, grid=(bl, nc), in_specs=specs,
        out_specs=[pl.BlockSpec((None, CHUNK, D_INNER), lambda b, c: (b, c, 0)),
                   pl.BlockSpec((None, N_SSM_GROUPS, None, D_STATE, 4 * LANES), lambda b, c: (b, 0, c, 0, 0))],
        out_shape=[jax.ShapeDtypeStruct((bl, SEQ, D_INNER), F32),
                   jax.ShapeDtypeStruct((bl, N_SSM_GROUPS, nc, D_STATE, 4 * LANES), F32)],
        scratch_shapes=[pltpu.VMEM((N_SSM_GROUPS, D_STATE, 4 * LANES), F32)],
        compiler_params=_cp(("arbitrary",) * 2),
    )(xc3, dtb3, dab3, xc3, xc3, ltri)


def _ssd_bwd(name, xc3, dtb3, dab3, ltri, states, dy3, dxs_part3):
    bl = xc3.shape[0]
    nc = SEQ // CHUNK
    specs, c_ = _ssd_in_specs(True)
    wide = pl.BlockSpec((None, CHUNK, D_INNER), lambda b, c: (b, c_(c), 0))
    nar = pl.BlockSpec((None, CHUNK, N_SSM_GROUPS * D_STATE), lambda b, c: (b, c_(c), 0))
    st_spec = pl.BlockSpec((None, N_SSM_GROUPS, None, D_STATE, 4 * LANES), lambda b, c: (b, 0, c_(c), 0, 0))

    def body(x, dtb, dab, bm, cm, lt, st_ref, dy, dxp, dx_ref, ddtb_ref, ddab_ref, dbm_ref, dcm_ref, dst):
        @pl.when(pl.program_id(1) == 0)
        def _():
            dst[...] = jnp.zeros_like(dst)

        ltv = lt[...]
        for g in range(N_SSM_GROUPS):
            wl, nl = _ssd_cols(g)
            _, vjp = jax.vjp(lambda *a: _ssd_step(*a, ltv), st_ref[g], x[:, wl], dtb[:, wl], dab[:, wl], bm[:, nl], cm[:, nl])
            d_st, d_x, d_dtb, d_dab, d_bm, d_cm = vjp((dst[g], dy[:, wl]))
            dst[g] = d_st
            dx_ref[:, wl] = d_x + dxp[:, wl]
            ddtb_ref[:, wl] = d_dtb
            ddab_ref[:, wl] = d_dab
            dbm_ref[:, nl] = d_bm
            dcm_ref[:, nl] = d_cm

    big = jax.ShapeDtypeStruct((bl, SEQ, D_INNER), F32)
    small = jax.ShapeDtypeStruct((bl, SEQ, N_SSM_GROUPS * D_STATE), F32)
    return pl.pallas_call(
        body, name=name, grid=(bl, nc), in_specs=specs + [st_spec, wide, wide],
        out_specs=[wide, wide, wide, nar, nar], out_shape=[big, big, big, small, small],
        scratch_shapes=[pltpu.VMEM((N_SSM_GROUPS, D_STATE, 4 * LANES), F32)],
        compiler_params=_cp(("arbitrary",) * 2),
    )(xc3, dtb3, dab3, xc3, xc3, ltri, states, dy3, dxs_part3)


def _ssdpost_operands(y, xc, proj, d_skip, ex, nw):
    w = 4 * LANES
    return [(y, (SSD_TM, w), lambda j, i: (i, j)), (xc, (SSD_TM, w), lambda j, i: (i, j)),
            (proj, (SSD_TM, w), lambda j, i: (i, Z0 // w + j)), (d_skip, (1, DTW), lambda j, i: (0, 0)),
            (ex, (LANES, w), lambda j, i: (0, j)), (nw, (1, w), lambda j, i: (0, j))]


def _ssdpost(name, y, xc, proj, d_skip, ex, nw):
    t = y.shape[0]
    w = 4 * LANES
    return _ew(name, _ssdpost_fn, (D_INNER // w, t // SSD_TM), _ssdpost_operands(y, xc, proj, d_skip, ex, nw),
               [((t, D_INNER), BF16, (SSD_TM, w), lambda j, i: (i, j))])[0]


def _ssdpost_bwd(name, y, xc, proj, d_skip, ex, nw, dysn, dproj):
    t = y.shape[0]
    w = 4 * LANES
    blk = ((SSD_TM, w), lambda j, i: (i, j))
    return _ew_bwd(name, _ssdpost_fn, (D_INNER // w, t // SSD_TM), _ssdpost_operands(y, xc, proj, d_skip, ex, nw),
                   [(dysn, *blk)],
                   [dict(idx=0, kind="tile", shape=(t, D_INNER), dtype=F32, block=blk[0], imap=blk[1]),
                    dict(idx=1, kind="tile", shape=(t, D_INNER), dtype=F32, block=blk[0], imap=blk[1]),
                    dict(idx=2, kind="tile", shape=dproj.shape, dtype=dproj.dtype, block=blk[0],
                         imap=lambda j, i: (i, Z0 // w + j), into=dproj),
                    dict(idx=3, kind="acc", shape=(1, DTW), dtype=F32, block=(1, DTW), imap=lambda j, i: (0, 0),
                         first=lambda: (pl.program_id(0) == 0) & (pl.program_id(1) == 0)),
                    dict(idx=5, kind="acc", shape=(1, D_INNER), dtype=F32, block=(1, w), imap=lambda j, i: (0, j),
                         first=lambda: pl.program_id(1) == 0)])


def _merge_operands(ya, ys, proj, b_gates):
    w = 4 * LANES
    g0 = G0 // w
    nh = D_MODEL // w
    return [(ya, (TM, w), lambda j, i: (i, j)), (ys, (TM, w), lambda j, i: (i, j)),
            (proj, (TM, w), lambda j, i: (i, g0 + j)), (proj, (TM, w), lambda j, i: (i, g0 + nh + j)),
            (b_gates, (1, w), lambda j, i: (0, j)), (b_gates, (1, w), lambda j, i: (0, nh + j))]


def _merge(name, ya, ys, proj, b_gates):
    t = ya.shape[0]
    w = 4 * LANES
    return _ew(name, _merge_fn, (D_MODEL // w, t // TM), _merge_operands(ya, ys, proj, b_gates),
               [((t, D_MODEL), BF16, (TM, w), lambda j, i: (i, j))])[0]


def _merge_bwd(name, ya, ys, proj, b_gates, dmixed):
    t = ya.shape[0]
    w = 4 * LANES
    blk = ((TM, w), lambda j, i: (i, j))
    first = lambda: pl.program_id(1) == 0
    tile = lambda k, dt: dict(idx=k, kind="tile", shape=(t, D_MODEL), dtype=dt, block=blk[0], imap=blk[1])
    acc = lambda k: dict(idx=k, kind="acc", shape=(1, D_MODEL), dtype=F32, block=(1, w), imap=lambda j, i: (0, j), first=first)
    return _ew_bwd(name, _merge_fn, (D_MODEL // w, t // TM), _merge_operands(ya, ys, proj, b_gates), [(dmixed, *blk)],
                   [tile(0, BF16), tile(1, BF16), tile(2, BF16), tile(3, BF16), acc(4), acc(5)])


def _loss(name, y, tgt):
    t = y.shape[0]
    blk = pl.BlockSpec((TM, D_MODEL), lambda i: (i, 0))

    def body(y_ref, t_ref, dy_ref, l_ref):
        e = y_ref[...] - t_ref[...]
        dy_ref[...] = e * (1.0 / D_MODEL)
        part = jnp.sum(jnp.sum(e * e, axis=-1, keepdims=True), axis=0, keepdims=True) * (0.5 / D_MODEL)
        part = jnp.broadcast_to(part, (8, LANES))

        @pl.when(pl.program_id(0) == 0)
        def _():
            l_ref[...] = part

        @pl.when(pl.program_id(0) > 0)
        def _():
            l_ref[...] += part

    return pl.pallas_call(
        body, name=name, grid=(t // TM,), in_specs=[blk, blk],
        out_specs=[blk, pl.BlockSpec((8, LANES), lambda i: (0, 0))],
        out_shape=[jax.ShapeDtypeStruct((t, D_MODEL), F32), jax.ShapeDtypeStruct((8, LANES), F32)],
        compiler_params=_cp(("arbitrary",)),
    )(y, tgt)


def _adamw_fn(w, g, m, v):
    m2 = B1 * m + (1.0 - B1) * g
    v2 = B2 * v + (1.0 - B2) * (g * g)
    m_hat = m2 / (1.0 - B1 ** STEP)
    v_hat = v2 / (1.0 - B2 ** STEP)
    return -LR * (m_hat / (jnp.sqrt(v_hat) + ADAM_EPS) + WD * w), m2, v2


def _adamw(name, w, g, m, v):
    rows, cols = w.shape
    tm = rows
    for cand in (512, 256, 128, 64, 32, 16, 8):
        if rows % cand == 0 and cand * cols * 4 <= (1 << 21):
            tm = cand
            break
    blk = ((tm, cols), lambda i: (i, 0))
    return _ew(name, _adamw_fn, (rows // tm,), [(a, *blk) for a in (w, g, m, v)], [((rows, cols), F32, *blk)] * 3)


NCH = 4
TMM = 1024
TKK = 1024


def _ffn_fwd(tag, x, nw, wg, wu, wd, li):
    t, fc = x.shape[0], wg.shape[-1]
    h = _rmsnorm(tag + "_norm", x, nw)

    def up_body(h_ref, wg_ref, wu_ref, g_ref, u_ref, a_ref):
        hv = h_ref[...]
        g = _dg(hv, wg_ref[...], 1, 0).astype(BF16)
        u = _dg(hv, wu_ref[...], 1, 0).astype(BF16)
        g_ref[...] = g
        u_ref[...] = u
        a_ref[...] = _swiglu_fn(g.astype(F32), u.astype(F32))[0].astype(BF16)

    w_spec = pl.BlockSpec((None, None, D_MODEL, fc), lambda k, i: (k, li, 0, 0))
    o_spec = pl.BlockSpec((None, TMM, fc), lambda k, i: (k, i, 0))
    g, u, a = pl.pallas_call(
        up_body, name=tag + "_up_act", grid=(NCH, t // TMM),
        in_specs=[pl.BlockSpec((TMM, D_MODEL), lambda k, i: (i, 0)), w_spec, w_spec], out_specs=[o_spec] * 3,
        out_shape=[jax.ShapeDtypeStruct((NCH, t, fc), BF16)] * 3, compiler_params=_cp(("parallel", "parallel")),
    )(h, wg, wu)
    row = ((TMM, D_MODEL), lambda i, j, k: (i, 0))
    y = _mmx(tag + "_down", (t // TMM, 1, NCH),
             (a, (None, TMM, fc), lambda i, j, k: (k, i, 0)),
             (wd, (None, None, fc, D_MODEL), lambda i, j, k: (k, li, 0, 0)),
             ((t, D_MODEL), F32, *row), (1, 0), alpha=0.5, res=(x, *row))
    return y, (x, h, g, u, a)


def _ffn_bwd(tag, saved, nw, wg, wu, wd, li, dy, bufs, gl):
    x, h, g, u, a = saved
    t, fc = x.shape[0], wg.shape[-1]
    bg, bu, bd = bufs
    def dact_body(dy_ref, wd_ref, g_ref, u_ref, dg_ref, du_ref):
        da = _dg(dy_ref[...], wd_ref[...], 1, 1) * 0.5
        _, vjp = jax.vjp(_swiglu_fn, g_ref[...].astype(F32), u_ref[...].astype(F32))
        dg, du = vjp((da,))
        dg_ref[...] = dg.astype(BF16)
        du_ref[...] = du.astype(BF16)

    c_spec = pl.BlockSpec((None, TMM, fc), lambda k, i: (k, i, 0))
    dg, du = pl.pallas_call(
        dact_body, name=tag + "_down_dx_act", grid=(NCH, t // TMM),
        in_specs=[pl.BlockSpec((TMM, D_MODEL), lambda k, i: (i, 0)),
                  pl.BlockSpec((None, None, fc, D_MODEL), lambda k, i: (k, li, 0, 0)), c_spec, c_spec],
        out_specs=[c_spec] * 2, out_shape=[jax.ShapeDtypeStruct((NCH, t, fc), BF16)] * 2,
        compiler_params=_cp(("parallel", "parallel")),
    )(dy, wd, g, u)
    bd = _mmx(tag + "_down_dw", (NCH, 1, t // TKK),
              (a, (None, TKK, fc), lambda k, j, kk: (k, kk, 0)),
              (dy, (TKK, D_MODEL), lambda k, j, kk: (kk, 0)),
              (bd.shape, BF16, (None, None, fc, D_MODEL), lambda k, j, kk: (gl, k, 0, 0)), (0, 0), alpha=0.5, into=bd)
    def dw(name, d, buf):
        return _mmx(name, (NCH, 1, t // TKK),
                    (h, (TKK, D_MODEL), lambda k, i, kk: (kk, 0)),
                    (d, (None, TKK, fc), lambda k, i, kk: (k, kk, 0)),
                    (buf.shape, BF16, (None, None, D_MODEL, fc), lambda k, i, kk: (gl, k, 0, 0)), (0, 0), into=buf)

    bg, bu = dw(tag + "_gate_dw", dg, bg), dw(tag + "_up_dw", du, bu)
    row = ((TMM, D_MODEL), lambda i, j, k: (i, 0))

    def dx_(name, d, w, res):
        return _mmx(name, (t // TMM, 1, NCH),
                    (d, (None, TMM, fc), lambda i, j, k: (k, i, 0)),
                    (w, (None, None, D_MODEL, fc), lambda i, j, k: (k, li, 0, 0)),
                    ((t, D_MODEL), F32, *row), (1, 1), res=None if res is None else (res, *row))

    dh = dx_(tag + "_up_dx", du, wu, dx_(tag + "_gate_dx", dg, wg, None))
    dx, dnw = _rmsnorm_bwd(tag + "_norm_bwd", x, nw, dh, dy)
    return dx, dnw, (bg, bu, bd)


def _mixer_fwd(tag, x, p, c):
    t = x.shape[0]
    bl = t // SEQ
    h = _rmsnorm(tag + "_norm", x, p["mix_norm_w"])
    proj = _mm(tag + "_in", h, p["w_in"], "nn")
    qk = _qkprep(tag + "_qk", proj, p["qkw"], c["cos"], c["sin"], c)
    os_, ls_ = [], []
    for g in range(3):
        o, l = _att_fwd(f"{tag}_att{g}", qk, proj, g)
        os_.append(o)
        ls_.append(l)
    att = _attmix(tag + "_attmix", os_, ls_)
    li = p["layer"]
    wa, ws, wo = p["w_att_proj"], p["w_ssm_proj"], p["w_out"]
    ca, cs, co = wa.shape[-1], ws.shape[-2], wo.shape[-2]
    row = ((TMM, D_MODEL), lambda i, j, k: (i, 0))
    ya = _mmx(tag + "_attproj", (t // TMM, NCH, 1),
              (att, (TMM, ATT_OUT), lambda i, k, kk: (i, 0)),
              (wa, (None, None, ATT_OUT, ca), lambda i, k, kk: (k, li, 0, 0)),
              ((t, D_MODEL), F32, (TMM, ca), lambda i, k, kk: (i, k)), (1, 0))
    proj3 = proj.reshape(bl, SEQ, NP)
    xc3 = _conv(tag + "_conv", proj3, p["conv_w"], p["conv_b"])
    xc = xc3.reshape(t, XBC)
    dtb, dab = _ssdpre(tag + "_ssdpre", proj, p["dt_bias"], p["a_log"], c["ex"])
    dtb3, dab3 = dtb.reshape(bl, SEQ, D_INNER), dab.reshape(bl, SEQ, D_INNER)
    y3, states = _ssd_fwd(tag + "_ssd", xc3, dtb3, dab3, c["ltri"])
    y = y3.reshape(t, D_INNER)
    ysn = _ssdpost(tag + "_ssdpost", y, xc, proj, p["d_skip"], c["ex"], p["ssm_norm_w"])
    ys = _mmx(tag + "_ssmproj", (t // TMM, 1, NCH),
              (ysn, (TMM, cs), lambda i, j, k: (i, k)),
              (ws, (None, None, cs, D_MODEL), lambda i, j, k: (k, li, 0, 0)),
              ((t, D_MODEL), F32, *row), (1, 0))
    mixed = _merge(tag + "_merge", ya, ys, proj, p["b_gates"])
    out = _mmx(tag + "_out", (t // TMM, 1, NCH),
               (mixed, (TMM, co), lambda i, j, k: (i, k)),
               (wo, (None, None, co, D_MODEL), lambda i, j, k: (k, li, 0, 0)),
               ((t, D_MODEL), F32, *row), (1, 0), res=(x, *row))
    return out, (x, h, proj, qk, os_, ls_, att, ya, xc3, dtb3, dab3, states, y, ysn, ys, mixed)


def _mixer_bwd(tag, saved, p, c, dout, bufs):
    x, h, proj, qk, os_, ls_, att, ya, xc3, dtb3, dab3, states, y, ysn, ys, mixed = saved
    t = x.shape[0]
    bl = t // SEQ
    xc = xc3.reshape(t, XBC)
    proj3 = proj.reshape(bl, SEQ, NP)
    gr = {}
    li, gl = p["layer"], p["global_layer"]
    wa, ws, wo = p["w_att_proj"], p["w_ssm_proj"], p["w_out"]
    ca, cs, co = wa.shape[-1], ws.shape[-2], wo.shape[-2]
    b_att, b_ssm, b_out = bufs

    def chunk_dx(name, d, w, cw):
        return _mmx(name, (t // TMM, NCH, D_MODEL // TKK),
                    (d, (TMM, TKK), lambda i, k, kk: (i, kk)),
                    (w, (None, None, cw, TKK), lambda i, k, kk: (k, li, 0, kk)),
                    ((t, NCH * cw), F32, (TMM, cw), lambda i, k, kk: (i, k)), (1, 1))

    def full_dw(name, a_, d, buf):
        kdim = a_.shape[1]
        tm = min(kdim, 1024)
        return _mmx(name, (kdim // tm, 1, t // TKK),
                    (a_, (TKK, tm), lambda i, j, kk: (kk, i)),
                    (d, (TKK, D_MODEL), lambda i, j, kk: (kk, 0)),
                    (buf.shape, BF16, (None, tm, D_MODEL), lambda i, j, kk: (gl, i, 0)), (0, 0), into=buf)

    dmixed = chunk_dx(tag + "_out_dx", dout, wo, co)
    b_out = full_dw(tag + "_out_dw", mixed, dout, b_out)
    dya, dys, dga, dgs, dba, dbs = _merge_bwd(tag + "_merge_bwd", ya, ys, proj, p["b_gates"], dmixed)
    gr["b_gates"] = jnp.concatenate([dba, dbs], axis=1)
    datt = _mmx(tag + "_attproj_dx", (t // TMM, 1, NCH),
                (dya, (TMM, ca), lambda i, j, k: (i, k)),
                (wa, (None, None, ATT_OUT, ca), lambda i, j, k: (k, li, 0, 0)),
                ((t, ATT_OUT), F32, (TMM, ATT_OUT), lambda i, j, k: (i, 0)), (1, 1))
    b_att = _mmx(tag + "_attproj_dw", (NCH, 1, t // TKK),
                 (att, (TKK, ATT_OUT), lambda k, j, kk: (kk, 0)),
                 (dya, (TKK, ca), lambda k, j, kk: (kk, k)),
                 (b_att.shape, BF16, (None, None, ATT_OUT, ca), lambda k, j, kk: (gl, k, 0, 0)), (0, 0), into=b_att)
    dysn = chunk_dx(tag + "_ssmproj_dx", dys, ws, cs)
    b_ssm = full_dw(tag + "_ssmproj_dw", ysn, dys, b_ssm)
    gr["bufs"] = (b_att, b_ssm, b_out)
    dmix = _attmix_bwd(tag + "_attmix_bwd", os_, ls_, datt)
    dq = dk = dv = jnp.zeros((t, QKV), F32)
    for g in range(3):
        dq, dk, dv = _att_bwd(f"{tag}_att{g}_bwd", qk, proj, g, dmix[g], dmix[3 + g], dq, dk, dv)
    dproj = jnp.zeros((t, NP), BF16)
    dproj = lax.dynamic_update_slice(dproj, dv.astype(BF16), (0, V0))
    dproj = lax.dynamic_update_slice(dproj, dga, (0, G0))
    dproj = lax.dynamic_update_slice(dproj, dgs, (0, G0 + D_MODEL))
    dproj, gr["qkw"] = _qkprep_bwd(tag + "_qk_bwd", proj, p["qkw"], c["cos"], c["sin"], c, dq, dk, dproj)
    dy, dxs_part, dproj, gr["d_skip"], gr["ssm_norm_w"] = _ssdpost_bwd(
        tag + "_ssdpost_bwd", y, xc, proj, p["d_skip"], c["ex"], p["ssm_norm_w"], dysn, dproj)
    dxs3, ddtb3, ddab3, db3, dc3 = _ssd_bwd(
        tag + "_ssd_bwd", xc3, dtb3, dab3, c["ltri"], states, dy.reshape(bl, SEQ, D_INNER), dxs_part.reshape(bl, SEQ, D_INNER))
    ddt, gr["dt_bias"], gr["a_log"] = _ssdpre_bwd(
        tag + "_ssdpre_bwd", proj, p["dt_bias"], p["a_log"], c["ex"], ddtb3.reshape(t, D_INNER), ddab3.reshape(t, D_INNER))
    dproj = lax.dynamic_update_slice(dproj, ddt, (0, DT0))
    dproj3, dcw0, dcw1, dcw2, dcw3, gr["conv_b"] = _conv_bwd(
        tag + "_conv_bwd", proj3, p["conv_w"], p["conv_b"], dxs3, db3, dc3, dproj.reshape(bl, SEQ, NP))
    dproj = dproj3.reshape(t, NP)
    gr["conv_w"] = jnp.concatenate([dcw0, dcw1, dcw2, dcw3], axis=0)
    gr["w_in"] = _mm(tag + "_in_dw", h, dproj, "tn", out_dtype=BF16)
    dh = _mm(tag + "_in_dx", dproj, p["w_in"], "nt")
    dx, gr["mix_norm_w"] = _rmsnorm_bwd(tag + "_norm_bwd", x, p["mix_norm_w"], dh, dout)
    return dx, gr


def _constants():
    cos, sin = _rope_tables()
    return dict(cos=cos, sin=sin, hmean=_head_mean_mat(),
                ex=_head_expand_mat(), ltri=_ltri_mat())


ANY = pl.BlockSpec(memory_space=pl.ANY)


def _mesh_pos():
    return lax.axis_index("x"), lax.axis_index("y"), lax.axis_index("c")


def _other_chips(x, y):
    return [(1 - x, y), (x, 1 - y), (1 - x, 1 - y)]


def _gather_exchange(srcs, outs, send_sems, recv_sems):
    n = len(srcs)
    x, y, c = _mesh_pos()
    chips = _other_chips(x, y)

    def part(a, chip, hf):
        h = srcs[a].shape[1] // 2
        return outs[a].at[2 * chip[0] + chip[1], :, pl.ds(hf * h, h), :]

    def mine(a):
        h = srcs[a].shape[1] // 2
        return srcs[a].at[:, pl.ds(c * h, h), :]

    def copy(a, k, src_ref, dst_ref, to):
        return pltpu.make_async_remote_copy(src_ref=src_ref, dst_ref=dst_ref, send_sem=send_sems.at[6 * a + k],
                                            recv_sem=recv_sems.at[6 * a + k], device_id=to, device_id_type=MESH)

    first = [copy(a, j, mine(a), part(a, (x, y), c), (*chip, c)) for a in range(n) for j, chip in enumerate(chips)]
    for cp in first:
        cp.start()
    passed = []
    for a in range(n):
        for j, chip in enumerate(chips):
            copy(a, j, part(a, chip, c), part(a, chip, c), (x, y, c)).wait_recv()
            fw = copy(a, 3 + j, part(a, chip, c), part(a, chip, c), (x, y, 1 - c))
            fw.start()
            passed.append(fw)
    for a in range(n):
        for j, chip in enumerate(chips):
            copy(a, 3 + j, part(a, chip, 1 - c), part(a, chip, 1 - c), (x, y, c)).wait_recv()
    for cp in first + passed:
        cp.wait_send()


def _gather_inits(ws, chip_idx):
    return [lax.dynamic_update_slice(jnp.zeros((NCH, *w.shape), w.dtype), w[None], (chip_idx[0], 0, 0, 0)) for w in ws]


def _all_gather_weights(ws, chip_idx):
    n = len(ws)
    inits = _gather_inits(ws, chip_idx)

    def body(*refs):
        _gather_exchange(refs[:n], refs[2 * n:3 * n], refs[3 * n], refs[3 * n + 1])

    return pl.pallas_call(
        body, name="all_gather_weights", out_shape=[jax.ShapeDtypeStruct(i.shape, i.dtype) for i in inits],
        in_specs=[ANY] * (2 * n), out_specs=[ANY] * n, input_output_aliases={n + a: a for a in range(n)},
        scratch_shapes=[pltpu.SemaphoreType.DMA((6 * n,)), pltpu.SemaphoreType.DMA((6 * n,))],
    )(*ws, *inits)


def _all_gather_weights_beside(ws, chip_idx):
    n = len(ws)
    src_refs = [jax.new_ref(w, memory_space=pltpu.MemorySpace.HBM) for w in ws]
    out_refs = [jax.new_ref(i, memory_space=pltpu.MemorySpace.HBM) for i in _gather_inits(ws, chip_idx)]

    @pl.kernel(mesh=plsc.ScalarSubcoreMesh(axis_name="sequencer", num_cores=1), name="all_gather_weights_beside",
               scratch_types=(pltpu.SemaphoreType.DMA((6 * n,)), pltpu.SemaphoreType.DMA((6 * n,))),
               compiler_params=pltpu.CompilerParams(collective_id=1))
    def launch(send_sems, recv_sems):
        x, y, c = _mesh_pos()
        barrier = pltpu.get_barrier_semaphore()
        for peer in [(x, y, 1 - c)] + [(*chip, c) for chip in _other_chips(x, y)]:
            pl.semaphore_signal(barrier, inc=1, device_id=peer, device_id_type=MESH)
        pl.semaphore_wait(barrier, 4)
        _gather_exchange(src_refs, out_refs, send_sems, recv_sems)

    launch()
    return [r[...] for r in out_refs]


def _pair_exchange(name, gs):
    n = len(gs)

    def body(*refs):
        srcs, outs, send_sems, recv_sems = refs[:n], refs[n:2 * n], refs[2 * n], refs[2 * n + 1]
        x, y, c = _mesh_pos()
        cps = []
        for a in range(n):
            h = gs[a].shape[2] // 2
            cps.append(pltpu.make_async_remote_copy(
                src_ref=srcs[a].at[:, :, pl.ds((1 - c) * h, h), :], dst_ref=outs[a], send_sem=send_sems.at[a],
                recv_sem=recv_sems.at[a], device_id=(x, y, 1 - c), device_id_type=MESH))
        for cp in cps:
            cp.start()
        for cp in cps:
            cp.wait()

    return pl.pallas_call(
        body, name=name,
        out_shape=[jax.ShapeDtypeStruct((g.shape[0], g.shape[1], g.shape[2] // 2, g.shape[3]), g.dtype) for g in gs],
        in_specs=[ANY] * n, out_specs=[ANY] * n,
        scratch_shapes=[pltpu.SemaphoreType.DMA((n,)), pltpu.SemaphoreType.DMA((n,))],
    )(*gs)


def _chip_exchange_copies(srcs, outs, send_sems, recv_sems):
    x, y, c = _mesh_pos()
    cps = [pltpu.make_async_remote_copy(
        src_ref=srcs[a].at[:, 2 * chip[0] + chip[1]], dst_ref=outs[a].at[j], send_sem=send_sems.at[3 * a + j],
        recv_sem=recv_sems.at[3 * a + j], device_id=(*chip, c), device_id_type=MESH)
        for a in range(len(srcs)) for j, chip in enumerate(_other_chips(x, y))]
    for cp in cps:
        cp.start()
    for cp in cps:
        cp.wait()


def _chip_exchange_shapes(hs):
    return [jax.ShapeDtypeStruct((3, h.shape[0], h.shape[2], h.shape[3]), h.dtype) for h in hs]


def _chip_exchange(hs):
    n = len(hs)

    def body(*refs):
        _chip_exchange_copies(refs[:n], refs[n:2 * n], refs[2 * n], refs[2 * n + 1])

    return pl.pallas_call(
        body, name="grad_chip_exchange", out_shape=_chip_exchange_shapes(hs),
        in_specs=[ANY] * n, out_specs=[ANY] * n,
        scratch_shapes=[pltpu.SemaphoreType.DMA((3 * n,)), pltpu.SemaphoreType.DMA((3 * n,))],
    )(*hs)


def _zero_after(v):
    return jnp.minimum(lax.bitcast_convert_type(v, jnp.uint16).astype(jnp.int32), 0).astype(F32)


def _chip_exchange_beside(name, hs):
    n = len(hs)
    src_refs = [jax.new_ref(h, memory_space=pltpu.MemorySpace.HBM) for h in hs]
    out_refs = [jax.empty_ref(s, memory_space=pltpu.MemorySpace.HBM) for s in _chip_exchange_shapes(hs)]

    @pl.kernel(mesh=plsc.ScalarSubcoreMesh(axis_name="sequencer", num_cores=1), name=name,
               scratch_types=(pltpu.SemaphoreType.DMA((3 * n,)), pltpu.SemaphoreType.DMA((3 * n,))),
               compiler_params=pltpu.CompilerParams(collective_id=2))
    def launch(send_sems, recv_sems):
        x, y, c = _mesh_pos()
        barrier = pltpu.get_barrier_semaphore()
        for chip in _other_chips(x, y):
            pl.semaphore_signal(barrier, inc=1, device_id=(*chip, c), device_id_type=MESH)
        pl.semaphore_wait(barrier, 3)
        _chip_exchange_copies(src_refs, out_refs, send_sems, recv_sems)

    launch()
    return [r[...] for r in out_refs]


def _pair_share(rs):
    n = len(rs)

    def body(*refs):
        outs, send_sems, recv_sems = refs[n:2 * n], refs[2 * n], refs[2 * n + 1]
        x, y, c = _mesh_pos()
        cps = [pltpu.make_async_remote_copy(src_ref=outs[a].at[:, c], dst_ref=outs[a].at[:, c], send_sem=send_sems.at[a],
                                            recv_sem=recv_sems.at[a], device_id=(x, y, 1 - c), device_id_type=MESH)
               for a in range(n)]
        for cp in cps:
            cp.start()
        for a in range(n):
            pltpu.make_async_remote_copy(src_ref=outs[a].at[:, 1 - c], dst_ref=outs[a].at[:, 1 - c],
                                         send_sem=send_sems.at[a], recv_sem=recv_sems.at[a], device_id=(x, y, c),
                                         device_id_type=MESH).wait_recv()
        for cp in cps:
            cp.wait_send()

    return pl.pallas_call(
        body, name="grad_pair_share", out_shape=[jax.ShapeDtypeStruct(r.shape, r.dtype) for r in rs],
        in_specs=[ANY] * n, out_specs=[ANY] * n, input_output_aliases={a: a for a in range(n)},
        scratch_shapes=[pltpu.SemaphoreType.DMA((n,)), pltpu.SemaphoreType.DMA((n,))],
    )(*rs)


def _pair_sum(name, g, recv, c_idx):
    d, k, h, b = recv.shape
    g5 = g.reshape(d * k, 2, h, b)

    def body(c_ref, a_ref, b_ref, o_ref):
        o_ref[...] = (a_ref[...].astype(F32) + b_ref[...].astype(F32)).astype(o_ref.dtype)

    out = pl.pallas_call(
        body, name=name,
        grid_spec=pltpu.PrefetchScalarGridSpec(
            num_scalar_prefetch=1, grid=(d * k,),
            in_specs=[pl.BlockSpec((None, None, h, b), lambda i, c: (i, c[0], 0, 0)),
                      pl.BlockSpec((None, h, b), lambda i, c: (i, 0, 0))],
            out_specs=pl.BlockSpec((None, h, b), lambda i, c: (i, 0, 0))),
        out_shape=jax.ShapeDtypeStruct((d * k, h, b), BF16),
        compiler_params=_cp(("arbitrary",)),
    )(c_idx, g5, recv.reshape(d * k, h, b))
    return out.reshape(d, k, h, b)


def _chip_sum(name, ha, recv, chip_idx, c_idx, depth, l0, into=None):
    d, _, h, b = ha.shape

    def body(k_ref, c_ref, a_ref, r0, r1, r2, *rest):
        rest[-1][...] = ((a_ref[...].astype(F32) + r0[...].astype(F32)) + r1[...].astype(F32)) + r2[...].astype(F32)

    blk = (None, None, h, b)
    extra = [] if into is None else [into]
    return pl.pallas_call(
        body, name=name,
        grid_spec=pltpu.PrefetchScalarGridSpec(
            num_scalar_prefetch=2, grid=(d,),
            in_specs=[pl.BlockSpec(blk, lambda l, k, c: (l, k[0], 0, 0))] +
                     [pl.BlockSpec(blk, lambda l, k, c, j=j: (j, l, 0, 0)) for j in range(3)] + [ANY] * len(extra),
            out_specs=pl.BlockSpec(blk, lambda l, k, c: (l0 + l, c[0], 0, 0))),
        out_shape=jax.ShapeDtypeStruct((depth, 2, h, b), F32),
        input_output_aliases={} if into is None else {6: 0},
        compiler_params=_cp(("arbitrary",)),
    )(chip_idx, c_idx, ha, recv, recv, recv, *extra)


def _all_sum_small(name, vec):
    rows = vec.shape[0]

    def body(v_ref, o_ref, buf, send_sems, recv_sems):
        x, y, c = _mesh_pos()
        me, sibling = (x, y, c), (x, y, 1 - c)
        chips = _other_chips(x, y)

        def slot(p):
            return buf.at[4 * p[0] + 2 * p[1] + p[2]]

        def copy(k, block, to, src=None):
            return pltpu.make_async_remote_copy(src_ref=slot(block) if src is None else src, dst_ref=slot(block),
                                                send_sem=send_sems.at[k], recv_sem=recv_sems.at[k],
                                                device_id=to, device_id_type=MESH)

        first = [copy(0, me, sibling, src=v_ref)]
        first += [copy(1 + j, me, (*chip, c), src=v_ref) for j, chip in enumerate(chips)]
        for cp in first:
            cp.start()
        passed = [copy(4 + j, (*chip, c), sibling) for j, chip in enumerate(chips)]
        for j, chip in enumerate(chips):
            copy(1 + j, (*chip, c), me).wait_recv()
            passed[j].start()
        copy(0, sibling, me).wait_recv()
        for j, chip in enumerate(chips):
            copy(4 + j, (*chip, 1 - c), me).wait_recv()
        for cp in first + passed:
            cp.wait_send()
        slot(me)[...] = v_ref[...]
        acc = buf[0]
        for k in range(1, 8):
            acc = acc + buf[k]
        o_ref[...] = acc

    vm = pl.BlockSpec(memory_space=pltpu.VMEM)
    return pl.pallas_call(
        body, name=name, out_shape=jax.ShapeDtypeStruct((rows, LANES), F32),
        in_specs=[vm], out_specs=vm, compiler_params=pltpu.CompilerParams(vmem_limit_bytes=VMEM_LIMIT),
        scratch_shapes=[pltpu.VMEM((8, rows, LANES), F32), pltpu.SemaphoreType.DMA((7,)), pltpu.SemaphoreType.DMA((7,))],
    )(vec)


def _pad_lanes(v, n=LANES):
    return jnp.pad(v, (0, n - v.shape[0]))[None, :]


def _w_in_to_kernel(w):
    return jnp.concatenate([w[:, :6656], w[:, 9760:N_IN], w[:, 6656:9728], w[:, 9728:9760],
                            jnp.zeros((w.shape[0], NP - N_IN), w.dtype)], axis=1)


def _w_in_from_kernel(w):
    return jnp.concatenate([w[:, :6656], w[:, X0:DT0], w[:, DT0:DT0 + 32], w[:, G0:X0]], axis=1)


def _layer_params(big, small, i):
    p = {k: big[k][i][0] for k in GRAD_BUFS}
    li = big["w_in"][i][1]
    p["layer"], p["global_layer"] = li, i
    w_in = big["w_in"][i][0]
    cw = w_in.shape[-1]
    pieces = []
    for lo, hi in ((0, 6656), (9760, N_IN), (6656, 9728), (9728, 9760)):
        for k in range(NCH):
            a, b = max(lo, k * cw), min(hi, (k + 1) * cw)
            if a < b:
                pieces.append(w_in[k, li, :, a - k * cw:b - k * cw])
    p["w_in"] = jnp.concatenate(pieces + [jnp.zeros((D_MODEL, NP - N_IN), w_in.dtype)], axis=1)
    p["conv_w"] = big["conv_w"][i][:, None, :]
    for k in ("ffn1_norm_w", "mix_norm_w", "ffn2_norm_w", "b_gates", "conv_b", "ssm_norm_w"):
        p[k] = small[k][i][None, :]
    for k in ("dt_bias", "a_log", "d_skip"):
        p[k] = _pad_lanes(small[k][i])
    p["qkw"] = jnp.stack([_pad_lanes(small["q_norm_w"][i]), _pad_lanes(small["k_norm_w"][i])])
    return p


GRAD_BUFS = ("ffn1_w_gate", "ffn1_w_up", "ffn1_w_down", "w_att_proj", "w_ssm_proj", "w_out",
             "ffn2_w_gate", "ffn2_w_up", "ffn2_w_down")


def _local_step(x, tgt, layers, c, exchange_rest=None):
    depth = len(layers)
    ffn = {f: tuple(f + s for s in ("_w_gate", "_w_up", "_w_down")) for f in ("ffn1", "ffn2")}
    saved = []
    for i, p in enumerate(layers):
        x, s1 = _ffn_fwd(f"L{i}_ffn1", x, p["ffn1_norm_w"], *[p[n] for n in ffn["ffn1"]], p["layer"])
        x, s2 = _mixer_fwd(f"L{i}_mix", x, p, c)
        x, s3 = _ffn_fwd(f"L{i}_ffn2", x, p["ffn2_norm_w"], *[p[n] for n in ffn["ffn2"]], p["layer"])
        saved.append((s1, s2, s3))
    dx, loss_blk = _loss("loss", x, tgt)

    def new_bufs(nl):
        out = {}
        for n in GRAD_BUFS:
            a, b = layers[0][n].shape[-2:]
            out[n] = jnp.zeros((nl, NCH * a, b) if n in ("w_ssm_proj", "w_out") else (nl, NCH, a, b), BF16)
        return out

    def finished(buf):
        out = dict(buf)
        for n in ("w_ssm_proj", "w_out"):
            a, b = layers[0][n].shape[-2:]
            out[n] = buf[n].reshape(-1, NCH, a, b)
        return out

    grads = [None] * depth
    rest_out = [None] * depth
    for i in reversed(range(depth)):
        p = layers[i]
        s1, s2, s3 = saved[i]
        buf, gl = new_bufs(1), 0
        p = dict(p, global_layer=gl)
        if i + 2 < depth and exchange_rest is not None:
            dx = dx + _zero_after(rest_out[i + 2][1][0][0, 0, 0, 0])

        def ffn_bwd(f, s, dy):
            names = ffn[f]
            d, dn, new = _ffn_bwd(f"L{i}_{f}", s, p[f + "_norm_w"], *[p[n] for n in names], p["layer"], dy,
                                  tuple(buf[n] for n in names), gl)
            buf.update(zip(names, new))
            return d, dn

        dx, dn2 = ffn_bwd("ffn2", s3, dx)
        dx, gr = _mixer_bwd(f"L{i}_mix", s2, p, c, dx, (buf["w_att_proj"], buf["w_ssm_proj"], buf["w_out"]))
        buf["w_att_proj"], buf["w_ssm_proj"], buf["w_out"] = gr.pop("bufs")
        dx, dn1 = ffn_bwd("ffn1", s1, dx)
        gr.update(ffn1_norm_w=dn1, ffn2_norm_w=dn2)
        grads[i] = gr
        if i >= 1 and exchange_rest is not None:
            rest_out[i], ready = exchange_rest(i, finished(buf), grads[i:i + 1])
            dx = dx + _zero_after(ready)
    return loss_blk, dx, grads, finished(buf), rest_out


WEIGHTS = ["ffn1_norm_w", "ffn1_w_gate", "ffn1_w_up", "ffn1_w_down", "mix_norm_w", "w_in", "b_gates", "q_norm_w",
           "k_norm_w", "conv_w", "conv_b", "dt_bias", "a_log", "d_skip", "ssm_norm_w", "w_att_proj", "w_ssm_proj",
           "w_out", "ffn2_norm_w", "ffn2_w_gate", "ffn2_w_up", "ffn2_w_down"]
SHARD_AXIS = {"ffn1_w_gate": 2, "ffn1_w_up": 2, "ffn1_w_down": 1, "w_in": 2, "conv_w": 2, "w_att_proj": 2,
              "w_ssm_proj": 1, "w_out": 1, "ffn2_w_gate": 2, "ffn2_w_up": 2, "ffn2_w_down": 1}
BIG = [n for n in WEIGHTS if n in SHARD_AXIS]
SMALL = [n for n in WEIGHTS if n not in SHARD_AXIS]
def _from_flat(flat, shapes):
    v = flat.reshape(-1)
    out, off = [], 0
    for s in shapes:
        n = math.prod(s)
        out.append(v[off:off + n].reshape(s))
        off += n
    return out


def _pack_small(parts):
    v = jnp.concatenate([p.astype(F32).reshape(-1) for p in parts])
    rows = -(-v.shape[0] // (8 * LANES)) * 8
    return jnp.pad(v, (0, rows * LANES - v.shape[0])).reshape(rows, LANES)


def kernel(x, ffn1_norm_w, ffn1_w_gate, ffn1_w_up, ffn1_w_down, mix_norm_w, w_in, b_gates, q_norm_w, k_norm_w, conv_w, conv_b, dt_bias, a_log, d_skip, ssm_norm_w, w_att_proj, w_ssm_proj, w_out, ffn2_norm_w, ffn2_w_gate, ffn2_w_up, ffn2_w_down, loss_target, m_ffn1_norm_w, m_ffn1_w_gate, m_ffn1_w_up, m_ffn1_w_down, m_mix_norm_w, m_w_in, m_b_gates, m_q_norm_w, m_k_norm_w, m_conv_w, m_conv_b, m_dt_bias, m_a_log, m_d_skip, m_ssm_norm_w, m_w_att_proj, m_w_ssm_proj, m_w_out, m_ffn2_norm_w, m_ffn2_w_gate, m_ffn2_w_up, m_ffn2_w_down, v_ffn1_norm_w, v_ffn1_w_gate, v_ffn1_w_up, v_ffn1_w_down, v_mix_norm_w, v_w_in, v_b_gates, v_q_norm_w, v_k_norm_w, v_conv_w, v_conv_b, v_dt_bias, v_a_log, v_d_skip, v_ssm_norm_w, v_w_att_proj, v_w_ssm_proj, v_w_out, v_ffn2_norm_w, v_ffn2_w_gate, v_ffn2_w_up, v_ffn2_w_down):
    w = dict(zip(WEIGHTS, (ffn1_norm_w, ffn1_w_gate, ffn1_w_up, ffn1_w_down, mix_norm_w, w_in, b_gates, q_norm_w, k_norm_w, conv_w, conv_b, dt_bias, a_log, d_skip, ssm_norm_w, w_att_proj, w_ssm_proj, w_out, ffn2_norm_w, ffn2_w_gate, ffn2_w_up, ffn2_w_down)))
    m = dict(zip(WEIGHTS, (m_ffn1_norm_w, m_ffn1_w_gate, m_ffn1_w_up, m_ffn1_w_down, m_mix_norm_w, m_w_in, m_b_gates, m_q_norm_w, m_k_norm_w, m_conv_w, m_conv_b, m_dt_bias, m_a_log, m_d_skip, m_ssm_norm_w, m_w_att_proj, m_w_ssm_proj, m_w_out, m_ffn2_norm_w, m_ffn2_w_gate, m_ffn2_w_up, m_ffn2_w_down)))
    v = dict(zip(WEIGHTS, (v_ffn1_norm_w, v_ffn1_w_gate, v_ffn1_w_up, v_ffn1_w_down, v_mix_norm_w, v_w_in, v_b_gates, v_q_norm_w, v_k_norm_w, v_conv_w, v_conv_b, v_dt_bias, v_a_log, v_d_skip, v_ssm_norm_w, v_w_att_proj, v_w_ssm_proj, v_w_out, v_ffn2_norm_w, v_ffn2_w_gate, v_ffn2_w_up, v_ffn2_w_down)))
    depth = ffn1_norm_w.shape[0]
    bl = x.shape[0]
    t = bl * SEQ
    mx, my, mc = lax.axis_index("x"), lax.axis_index("y"), lax.axis_index("c")
    c_idx = mc.astype(jnp.int32).reshape(1)
    chip_idx = (2 * mx + my).astype(jnp.int32).reshape(1)

    cw_width = conv_w.shape[2]
    slots = lax.dynamic_update_slice(jnp.zeros((NCH, *conv_w.shape), F32), jnp.where(mc == 0, conv_w, 0.0)[None],
                                     (chip_idx[0], 0, 0, 0))
    conv_all = _all_sum_small("conv_gather", slots.reshape(-1, LANES)).reshape(NCH, *conv_w.shape)
    big = {"conv_w": jnp.concatenate([conv_all[k] for k in range(NCH)], axis=2)}

    mm_names = [n for n in BIG if n != "conv_w"]
    first = _all_gather_weights([w[n][:1].astype(BF16) for n in mm_names], chip_idx)
    bits = lax.bitcast_convert_type(first[0][0, 0, 0, 0], jnp.uint16).astype(jnp.int32)
    zero = jnp.minimum(bits, 0).astype(F32)
    rest = _all_gather_weights_beside([(w[n][1:] + zero).astype(BF16) for n in mm_names], chip_idx) if depth > 1 else first
    for n, a0, a1 in zip(mm_names, first, rest):
        big[n] = [(a0, 0)] + [(a1, i - 1) for i in range(1, depth)]
    small = {n: w[n] for n in SMALL}

    def exchange(tag, buf, layer_grads, beside, after=None):
        buf = dict(buf)
        w_in_grad = jnp.stack([_w_in_from_kernel(g["w_in"]).reshape(D_MODEL, NCH, -1).transpose(1, 0, 2)
                               for g in layer_grads])
        buf["w_in"] = w_in_grad if after is None else w_in_grad + _zero_after(after).astype(BF16)
        mine = [buf[n] for n in mm_names]
        from_sibling = _pair_exchange("grad_pair_exchange_" + tag, mine)
        pairs = [_pair_sum(f"grad_pair_sum_{tag}_{n}", g, r, c_idx) for n, g, r in zip(mm_names, mine, from_sibling)]
        if beside:
            return pairs, _chip_exchange_beside("grad_chip_exchange_beside_" + tag, pairs)
        return pairs, _chip_exchange(pairs)

    c = _constants()
    layers = [_layer_params(big, small, i) for i in range(depth)]

    def exchange_rest(i, buf, layer_grads):
        pairs, recv = exchange(f"L{i}", buf, layer_grads, True)
        return (pairs, recv), sum(p[0, 0, 0, 0] for p in pairs)

    loss_blk, dx, grads, buf0, rest_out = _local_step(
        x.reshape(t, D_MODEL), loss_target.reshape(t, D_MODEL), layers, c, exchange_rest if depth > 1 else None)
    grad_x = dx.reshape(bl, SEQ, D_MODEL)
    pairs0, recv0 = exchange("L0", buf0, grads[:1], False, rest_out[1][1][0][0, 0, 0, 0] if depth > 1 else None)
    halves = []
    for j, n in enumerate(mm_names):
        acc = None
        for i in range(depth - 1, 0, -1):
            acc = _chip_sum(f"grad_chip_sum_L{i}_{n}", rest_out[i][0][j], rest_out[i][1][j], chip_idx, c_idx, depth, i,
                            into=acc)
        halves.append(_chip_sum("grad_chip_sum_L0_" + n, pairs0[j], recv0[j], chip_idx, c_idx, depth, 0, into=acc))
    g_big = {n: r.reshape(w[n].shape) for n, r in zip(mm_names, _pair_share(halves))}

    def small_grad(n):
        if n == "q_norm_w":
            return jnp.stack([g["qkw"][0, 0, :64] for g in grads])
        if n == "k_norm_w":
            return jnp.stack([g["qkw"][1, 0, :64] for g in grads])
        return jnp.stack([g[n][0, :w[n].shape[1]] for g in grads])

    small_shapes = [w[n].shape for n in SMALL]
    conv_shape = (depth, conv_w.shape[1], NCH * cw_width)
    tot = _all_sum_small("small_all_sum", _pack_small(
        [small_grad(n) for n in SMALL] + [jnp.stack([g["conv_w"] for g in grads]), loss_blk[0, :1]]))
    unpacked = _from_flat(tot, small_shapes + [conv_shape, (1,)])
    g_small = dict(zip(SMALL, unpacked[:-2]))
    g_big["conv_w"] = lax.dynamic_slice_in_dim(unpacked[-2], chip_idx[0] * cw_width, cw_width, axis=2)
    loss = unpacked[-1][0]

    grad, delta, new_m, new_v = {}, {}, {}, {}
    for n in BIG:
        shp = w[n].shape
        two_d = (shp[0] * shp[1], shp[2])
        d_, m_, v_ = _adamw("adamw_" + n, w[n].reshape(two_d), g_big[n].reshape(two_d), m[n].reshape(two_d), v[n].reshape(two_d))
        grad[n], delta[n], new_m[n], new_v[n] = g_big[n], d_.reshape(shp), m_.reshape(shp), v_.reshape(shp)
    d_, m_, v_ = _adamw("adamw_small", _pack_small([w[n] for n in SMALL]), _pack_small([g_small[n] for n in SMALL]),
                        _pack_small([m[n] for n in SMALL]), _pack_small([v[n] for n in SMALL]))
    for n, a, b, c_ in zip(SMALL, _from_flat(d_, small_shapes), _from_flat(m_, small_shapes), _from_flat(v_, small_shapes)):
        grad[n], delta[n], new_m[n], new_v[n] = g_small[n], a, b, c_
    return (loss, grad_x, *[grad[n] for n in WEIGHTS], *[delta[n] for n in WEIGHTS],
            *[new_m[n] for n in WEIGHTS], *[new_v[n] for n in WEIGHTS])
```

```python
import functools
import math

import numpy as np
import jax
import jax.numpy as jnp
from jax import lax
from jax.experimental import pallas as pl
from jax.experimental.pallas import tpu as pltpu
from jax.experimental.pallas import tpu_sc as plsc

F32 = jnp.float32
BF16 = jnp.bfloat16
HI = lax.Precision.HIGHEST
MESH = pl.DeviceIdType.MESH

D_MODEL = 1024
SEQ = 2048
DEPTH = 4
D_FF = 2816
ATT_DILATIONS = (1, 4, 16)
BAND = 128
ATT_OUT = 512
QKV = 1536
D_INNER = 2048
N_SSM_HEADS = 32
N_SSM_GROUPS = 4
D_STATE = 128
XBC = 3072
CHUNK = 128
N_IN = 11808
EPS = 1e-6
ROPE_THETA = 10000.0
NP = 12288
Q0, K0, V0, Z0, G0, X0, DT0 = 0, 1536, 3072, 4608, 6656, 8704, 11776
DTW = 128
LR, B1, B2, ADAM_EPS, WD, STEP = 0.001, 0.9, 0.999, 1e-08, 0.01, 10

LANES = 128
VMEM_LIMIT = 48 * 1024 * 1024
NEG = -1e30


def _cp(sem=None, **kw):
    return pltpu.CompilerParams(dimension_semantics=sem, vmem_limit_bytes=VMEM_LIMIT, **kw)


def _dg(a, b, ca, cb):
    return lax.dot_general(a.astype(BF16), b.astype(BF16), (((ca,), (cb,)), ((), ())), preferred_element_type=F32)


@jax.custom_vjp
def dot_nn(a, b):
    return _dg(a, b, 1, 0)


def _dot_nn_fwd(a, b):
    return _dg(a, b, 1, 0), (a, b)


def _dot_nn_bwd(r, g):
    a, b = r
    return _dg(g, b, 1, 1).astype(a.dtype), _dg(a, g, 0, 0).astype(b.dtype)


dot_nn.defvjp(_dot_nn_fwd, _dot_nn_bwd)


@jax.custom_vjp
def dot_nt(a, b):
    return _dg(a, b, 1, 1)


def _dot_nt_fwd(a, b):
    return _dg(a, b, 1, 1), (a, b)


def _dot_nt_bwd(r, g):
    a, b = r
    return _dg(g, b, 1, 0).astype(a.dtype), _dg(g, a, 0, 0).astype(b.dtype)


dot_nt.defvjp(_dot_nt_fwd, _dot_nt_bwd)


@jax.custom_vjp
def dot_tn(a, b):
    return _dg(a, b, 0, 0)


def _dot_tn_fwd(a, b):
    return _dg(a, b, 0, 0), (a, b)


def _dot_tn_bwd(r, g):
    a, b = r
    return _dg(b, g, 1, 1).astype(a.dtype), _dg(a, g, 1, 0).astype(b.dtype)


dot_tn.defvjp(_dot_tn_fwd, _dot_tn_bwd)


def _dot2_raw(a, e, ce):
    hi = a.astype(BF16)
    lo = (a - hi.astype(F32)).astype(BF16)
    return _dg(hi, e, 1, ce) + _dg(lo, e, 1, ce)


@jax.custom_vjp
def dot2(a, e):
    return _dot2_raw(a, e, 0)


def _dot2_fwd(a, e):
    return _dot2_raw(a, e, 0), e


def _dot2_bwd(e, g):
    return _dot2_raw(g, e, 1), jnp.zeros_like(e)


dot2.defvjp(_dot2_fwd, _dot2_bwd)


def _tri2_raw(l, x, cl):
    hi = x.astype(BF16)
    lo = (x - hi.astype(F32)).astype(BF16)
    return _dg(l, hi, cl, 0) + _dg(l, lo, cl, 0)


@jax.custom_vjp
def tri_matmul(l, x):
    return _tri2_raw(l, x, 1)


def _tri_fwd(l, x):
    return _tri2_raw(l, x, 1), l


def _tri_bwd(l, g):
    return jnp.zeros_like(l), _tri2_raw(l, g, 0)


tri_matmul.defvjp(_tri_fwd, _tri_bwd)


def _dup64_raw(w):
    return w + pltpu.roll(w, 64, 1)


@jax.custom_vjp
def dup64(w):
    return _dup64_raw(w)


def _dup64_fwd(w):
    return _dup64_raw(w), None


def _dup64_bwd(_, g):
    lane = lax.broadcasted_iota(jnp.int32, g.shape, 1)
    return (jnp.where(lane < 64, _dup64_raw(g), 0.0),)


dup64.defvjp(_dup64_fwd, _dup64_bwd)


def _rope_rot_raw(y, sign):
    lane = lax.broadcasted_iota(jnp.int32, y.shape, 1)
    first_half = (lane & 32) == 0
    return sign * jnp.where(first_half, -pltpu.roll(y, LANES - 32, 1), pltpu.roll(y, 32, 1))


@jax.custom_vjp
def rope_rot(y):
    return _rope_rot_raw(y, 1.0)


def _rope_rot_fwd(y):
    return _rope_rot_raw(y, 1.0), None


def _rope_rot_bwd(_, g):
    return (_rope_rot_raw(g, -1.0),)


rope_rot.defvjp(_rope_rot_fwd, _rope_rot_bwd)


def _shift_rows_raw(x, s):
    n = x.shape[0]
    r = pltpu.roll(x, s % n, 0)
    rows = lax.broadcasted_iota(jnp.int32, x.shape, 0)
    keep = rows >= s if s > 0 else rows < n + s
    return jnp.where(keep, r, 0.0)


@functools.partial(jax.custom_vjp, nondiff_argnums=(1,))
def shift_rows(x, s):
    return _shift_rows_raw(x, s)


def _shift_fwd(x, s):
    return _shift_rows_raw(x, s), None


def _shift_bwd(s, _, g):
    return (_shift_rows_raw(g, -s),)


shift_rows.defvjp(_shift_fwd, _shift_bwd)


def _sigmoid(x):
    return 1.0 / (1.0 + jnp.exp(-x))


def _silu(x):
    return x * _sigmoid(x)


def _softplus(x):
    return jnp.maximum(x, 0.0) + jnp.log(1.0 + jnp.exp(-jnp.abs(x)))


def _head_mean_mat():
    i = np.arange(LANES)
    return jnp.asarray((i[:, None] // 64 == i[None, :] // 64).astype(np.float32) / 64.0)


def _head_expand_mat():
    e = np.zeros((LANES, D_INNER), np.float32)
    for l in range(D_INNER):
        e[l // 64, l] = 1.0
    return jnp.asarray(e)


def _ltri_mat():
    i = np.arange(CHUNK)
    return jnp.asarray((i[:, None] >= i[None, :]).astype(np.float32))


def _rope_tables():
    pos = jnp.arange(SEQ, dtype=F32)
    inv_freq = 1.0 / (ROPE_THETA ** (jnp.arange(0, 64, 2, dtype=F32) / 64))
    ang = pos[:, None] * inv_freq[None, :]
    return jnp.tile(jnp.cos(ang), (1, 4)), jnp.tile(jnp.sin(ang), (1, 4))


def _pick(n, cap):
    best = None
    for t in range(LANES, min(n, cap) + 1, LANES):
        if n % t == 0:
            best = t
    return best if best is not None else n


def _mm(name, a, b, mode, out_dtype=F32, alpha=None, res=None):
    if mode == "nn":
        (m, k), n = a.shape, b.shape[1]
    elif mode == "nt":
        (m, k), n = a.shape, b.shape[0]
    else:
        (k, m), n = a.shape, b.shape[1]
    tm, tn, tk = _pick(m, 1408), _pick(n, 1408), _pick(k, 1024)
    nk = k // tk
    ca, cb = {"nn": (1, 0), "nt": (1, 1), "tn": (0, 0)}[mode]
    a_spec = pl.BlockSpec((tk, tm), lambda i, j, kk: (kk, i)) if mode == "tn" else pl.BlockSpec((tm, tk), lambda i, j, kk: (i, kk))
    b_spec = pl.BlockSpec((tn, tk), lambda i, j, kk: (j, kk)) if mode == "nt" else pl.BlockSpec((tk, tn), lambda i, j, kk: (kk, j))
    o_spec = pl.BlockSpec((tm, tn), lambda i, j, kk: (i, j))
    has_res = res is not None

    def finish(acc, res_ref, o_ref):
        if alpha is not None:
            acc = acc * alpha
        if has_res:
            acc = acc + res_ref[...].astype(F32)
        o_ref[...] = acc.astype(o_ref.dtype)

    def body(*refs):
        a_ref, b_ref = refs[0], refs[1]
        res_ref = refs[2] if has_res else None
        o_ref = refs[3] if has_res else refs[2]
        part = _dg(a_ref[...], b_ref[...], ca, cb)
        if nk == 1:
            finish(part, res_ref, o_ref)
            return
        acc_ref = refs[-1]
        kk = pl.program_id(2)

        @pl.when(kk == 0)
        def _():
            acc_ref[...] = part

        @pl.when(kk > 0)
        def _():
            acc_ref[...] += part

        @pl.when(kk == nk - 1)
        def _():
            finish(acc_ref[...], res_ref, o_ref)

    ins = [a, b] + ([res] if has_res else [])
    in_specs = [a_spec, b_spec] + ([o_spec] if has_res else [])
    return pl.pallas_call(
        body, name=name, grid=(m // tm, n // tn, nk), in_specs=in_specs, out_specs=o_spec,
        out_shape=jax.ShapeDtypeStruct((m, n), out_dtype),
        scratch_shapes=[pltpu.VMEM((tm, tn), F32)] if nk > 1 else [],
        compiler_params=_cp(("parallel", "parallel", "arbitrary")),
    )(*ins)


def _mmx(name, grid, a, b, out, contract, *, alpha=None, res=None, into=None):
    nk = grid[-1]
    has_res, has_into = res is not None, into is not None
    n_in = 2 + has_res + has_into

    def finish(acc, res_ref, o_ref):
        if alpha is not None:
            acc = acc * alpha
        if has_res:
            acc = acc + res_ref[...].astype(F32)
        o_ref[...] = acc.astype(o_ref.dtype)

    def body(*refs):
        res_ref = refs[2] if has_res else None
        o_ref = refs[n_in]
        part = _dg(refs[0][...], refs[1][...], *contract)
        if nk == 1:
            finish(part, res_ref, o_ref)
            return
        acc_ref = refs[-1]
        kk = pl.program_id(len(grid) - 1)

        @pl.when(kk == 0)
        def _():
            acc_ref[...] = part

        @pl.when(kk > 0)
        def _():
            acc_ref[...] += part

        @pl.when(kk == nk - 1)
        def _():
            finish(acc_ref[...], res_ref, o_ref)

    operands = [a, b] + ([res] if has_res else [])
    in_specs = [pl.BlockSpec(blk, im) for _, blk, im in operands] + ([ANY] if has_into else [])
    acc_shape = tuple(d for d in out[2] if d is not None)
    return pl.pallas_call(
        body, name=name, grid=grid, in_specs=in_specs, out_specs=pl.BlockSpec(out[2], out[3]),
        out_shape=jax.ShapeDtypeStruct(out[0], out[1]),
        scratch_shapes=[pltpu.VMEM(acc_shape, F32)] if nk > 1 else [],
        input_output_aliases={n_in - 1: 0} if has_into else {},
        compiler_params=_cp(("parallel",) * (len(grid) - 1) + ("arbitrary",)),
    )(*[o[0] for o in operands], *([into] if has_into else []))


def _ew(name, fn, grid, ins, outs, scratch=()):
    n_in, n_out = len(ins), len(outs)

    def body(*refs):
        vals = [r[...] for r in refs[:n_in]]
        res = fn(*vals, *refs[n_in + n_out:])
        for r, v in zip(refs[n_in:n_in + n_out], res):
            r[...] = v.astype(r.dtype)

    res = pl.pallas_call(
        body, name=name, grid=grid,
        in_specs=[pl.BlockSpec(b, m) for _, b, m in ins],
        out_specs=[pl.BlockSpec(b, m) for _, _, b, m in outs],
        out_shape=[jax.ShapeDtypeStruct(s, d) for s, d, _, _ in outs],
        scratch_shapes=list(scratch),
        compiler_params=_cp(("arbitrary",) * len(grid)),
    )(*[a for a, _, _ in ins])
    return res


def _ew_bwd(name, fn, grid, ins, cts, wrt, adds=(), ct_fn=None):
    n_in, n_ct, n_add = len(ins), len(cts), len(adds)
    idxs = [w["idx"] for w in wrt]
    intos = [(k, w["into"]) for k, w in enumerate(wrt) if w.get("into") is not None]

    def body(*refs):
        prim = [r[...] for r in refs[:n_in]]
        ct = [r[...].astype(F32) for r in refs[n_in:n_in + n_ct]]
        addv = [r[...] for r in refs[n_in + n_ct:n_in + n_ct + n_add]]
        orefs = refs[n_in + n_ct + n_add + len(intos):]

        def f(*sel):
            full = list(prim)
            for i, s in zip(idxs, sel):
                full[i] = s
            return fn(*full)

        _, vjp = jax.vjp(f, *[prim[i].astype(F32) for i in idxs])
        grads = vjp(tuple(ct) if ct_fn is None else ct_fn(*ct))
        for w, g, r in zip(wrt, grads, orefs):
            if w["kind"] == "tile":
                if w.get("add") is not None:
                    g = g + addv[w["add"]].astype(F32)
                r[...] = g.astype(r.dtype)
            else:
                first = w["first"]()

                @pl.when(first)
                def _(r=r, g=g):
                    r[...] = g.astype(r.dtype)

                @pl.when(jnp.logical_not(first))
                def _(r=r, g=g):
                    r[...] += g.astype(r.dtype)

    allin = list(ins) + list(cts) + list(adds)
    return pl.pallas_call(
        body, name=name, grid=grid,
        in_specs=[pl.BlockSpec(b, m) for _, b, m in allin] + [ANY] * len(intos),
        out_specs=[pl.BlockSpec(w["block"], w["imap"]) for w in wrt],
        out_shape=[jax.ShapeDtypeStruct(w["shape"], w["dtype"]) for w in wrt],
        input_output_aliases={len(allin) + q: k for q, (k, _) in enumerate(intos)},
        compiler_params=_cp(("arbitrary",) * len(grid)),
    )(*[a for a, _, _ in allin], *[a for _, a in intos])


def _rmsnorm_fn(x, w):
    return (x * lax.rsqrt(jnp.mean(x * x, axis=-1, keepdims=True) + EPS) * w,)


def _swiglu_fn(g, u):
    return (_silu(g) * u,)


def _qkprep_fn(t, w64, cos, sin, hmean):
    w = jnp.sum(dup64(jnp.broadcast_to(w64, (8, LANES))), axis=0, keepdims=True) * 0.125
    y = t * lax.rsqrt(dot2(t * t, hmean) + EPS) * w
    return (y * cos + rope_rot(y) * sin,)


def _att_fn(q, kp, kc, vp, vc, first):
    iq = lax.broadcasted_iota(jnp.int32, (BAND, 2 * BAND), 0)
    ik = lax.broadcasted_iota(jnp.int32, (BAND, 2 * BAND), 1)
    rel = BAND + iq - ik
    ok = (rel >= 0) & (rel <= BAND) & ((ik >= BAND) | jnp.logical_not(first))
    lane = lax.broadcasted_iota(jnp.int32, (1, LANES), 1)
    kcat = jnp.concatenate([kp, kc], axis=0)
    vcat = jnp.concatenate([vp, vc], axis=0)
    o_pair = jnp.zeros((BAND, LANES), F32)
    l_pair = jnp.zeros((BAND, LANES), F32)
    for hh in range(2):
        lm = (lane // 64 == hh).astype(F32)
        s = dot_nt(q * lm, kcat) * 0.125
        s = jnp.where(ok, s, NEG)
        mx = jnp.max(s, axis=-1, keepdims=True)
        e = jnp.exp(s - mx)
        den = jnp.sum(e, axis=-1, keepdims=True)
        o_pair = o_pair + dot_nn(e / den, vcat) * lm
        l_pair = l_pair + (mx + jnp.log(den)) * lm
    return o_pair, l_pair


def _attmix_fn(o0, o1, o2, l0, l1, l2):
    m = jnp.maximum(jnp.maximum(l0, l1), l2)
    e0, e1, e2 = jnp.exp(l0 - m), jnp.exp(l1 - m), jnp.exp(l2 - m)
    return ((e0 * o0 + e1 * o1 + e2 * o2) / (e0 + e1 + e2),)


def _conv_fn(x, w0, w1, w2, w3, b):
    pre = x * w3 + shift_rows(x, 1) * w2 + shift_rows(x, 2) * w1 + shift_rows(x, 3) * w0 + b
    return (_silu(pre),)


def _ssdpre_fn(dtraw, bias, alog, ex):
    dt = _softplus(dtraw + bias)
    da = dt * (-jnp.exp(alog))
    return dot2(dt, ex), dot2(da, ex)


def _ssd_step(st, x, dtb, dab, bm, cm, ltri):
    cum = tri_matmul(ltri, dab)
    cum_t = cum.T
    xdt = x * dtb
    cb = dot_nt(cm, bm)
    ri = lax.broadcasted_iota(jnp.int32, (CHUNK, CHUNK), 0)
    ci = lax.broadcasted_iota(jnp.int32, (CHUNK, CHUNK), 1)
    causal = ri >= ci
    lane = lax.broadcasted_iota(jnp.int32, (1, LANES), 1)
    rowi = lax.broadcasted_iota(jnp.int32, (LANES, 1), 0)
    ys = []
    for p in range(4):
        sl = slice(p * LANES, (p + 1) * LANES)
        cum_p, cum_tp, xdt_p = cum[:, sl], cum_t[sl, :], xdt[:, sl]
        acc = jnp.zeros((CHUNK, LANES), F32)
        for hh in range(2):
            col = jnp.sum(cum_p * (lane == 64 * hh).astype(F32), axis=1, keepdims=True)
            row = jnp.sum(cum_tp * (rowi == 64 * hh).astype(F32), axis=0, keepdims=True)
            dec = jnp.exp(jnp.where(causal, col - row, NEG))
            acc = acc + dot_nn(cb * dec, xdt_p * (lane // 64 == hh).astype(F32))
        ys.append(acc)
    y_diag = jnp.concatenate(ys, axis=1)
    y_off = dot_nn(cm, st) * jnp.exp(cum)
    last_row = (lax.broadcasted_iota(jnp.int32, (CHUNK, 1), 0) == CHUNK - 1).astype(F32)
    last = jnp.sum(cum * last_row, axis=0, keepdims=True)
    new_st = st * jnp.exp(last) + dot_tn(bm, xdt * jnp.exp(last - cum))
    return new_st, y_diag + y_off


def _ssdpost_fn(y, xs, z, dskip, ex, nw):
    db = jnp.sum(dot2(jnp.broadcast_to(dskip, (8, LANES)), ex), axis=0, keepdims=True) * 0.125
    y2 = (y + db * xs) * _silu(z)
    return (y2 * lax.rsqrt(jnp.mean(y2 * y2, axis=-1, keepdims=True) + EPS) * nw,)


def _merge_fn(ya, ys, ga, gs, ba, bs):
    return (_sigmoid(ga + ba) * ya + _sigmoid(gs + bs) * ys,)


TM = 512


def _full(shape):
    nd = len(shape)
    return (shape, lambda *_: (0,) * nd)


def _rmsnorm(name, x, w):
    t = x.shape[0]
    return _ew(name, _rmsnorm_fn, (t // TM,),
               [(x, (TM, D_MODEL), lambda i: (i, 0)), (w, (1, D_MODEL), lambda i: (0, 0))],
               [((t, D_MODEL), BF16, (TM, D_MODEL), lambda i: (i, 0))])[0]


def _rmsnorm_bwd(name, x, w, dh, dres):
    t = x.shape[0]
    row = ((TM, D_MODEL), lambda i: (i, 0))
    return _ew_bwd(name, _rmsnorm_fn, (t // TM,),
                   [(x, *row), (w, (1, D_MODEL), lambda i: (0, 0))], [(dh, *row)],
                   [dict(idx=0, kind="tile", shape=(t, D_MODEL), dtype=F32, block=row[0], imap=row[1], add=0),
                    dict(idx=1, kind="acc", shape=(1, D_MODEL), dtype=F32, block=(1, D_MODEL), imap=lambda i: (0, 0),
                         first=lambda: pl.program_id(0) == 0)],
                   adds=[(dres, *row)])


def _qk_operands(proj, qkw, cos, sin, consts, tm):
    nrow = SEQ // tm
    c = ((LANES, LANES), lambda j, i: (0, 0))
    return [(proj, (tm, LANES), lambda j, i: (i, j)),
            (qkw, (None, 1, LANES), lambda j, i: (j // 12, 0, 0)),
            (cos, (tm, LANES), lambda j, i: (i % nrow, 0)),
            (sin, (tm, LANES), lambda j, i: (i % nrow, 0)),
            (consts["hmean"], *c)]


def _qkprep(name, proj, qkw, cos, sin, consts):
    t = proj.shape[0]
    return _ew(name, _qkprep_fn, (2 * QKV // LANES, t // TM), _qk_operands(proj, qkw, cos, sin, consts, TM),
               [((t, 2 * QKV), F32, (TM, LANES), lambda j, i: (i, j))])[0]


def _qkprep_bwd(name, proj, qkw, cos, sin, consts, dq, dk, dproj):
    t = proj.shape[0]
    nq = QKV // LANES

    def pick(cq, ck):
        return (jnp.where(pl.program_id(0) < nq, cq, ck),)

    return _ew_bwd(name, _qkprep_fn, (2 * nq, t // TM), _qk_operands(proj, qkw, cos, sin, consts, TM),
                   [(d, (TM, LANES), lambda j, i: (i, j % nq)) for d in (dq, dk)],
                   [dict(idx=0, kind="tile", shape=dproj.shape, dtype=dproj.dtype, block=(TM, LANES),
                         imap=lambda j, i: (i, j), into=dproj),
                    dict(idx=1, kind="acc", shape=(2, 1, LANES), dtype=F32, block=(None, 1, LANES),
                         imap=lambda j, i: (j // 12, 0, 0),
                         first=lambda: (pl.program_id(0) % 12 == 0) & (pl.program_id(1) == 0))],
                   ct_fn=pick)


def _att_specs(dil, g):
    nb = SEQ // dil // BAND
    pt = 4 if dil == 1 else 1
    w = pt * LANES
    blk = (None, BAND * dil, w)
    kq, kk, kv = g * ATT_OUT // w, (QKV + g * ATT_OUT) // w, (V0 + g * ATT_OUT) // w

    def cur(n):
        return jnp.minimum(n, nb - 1)

    def prev(n):
        return jnp.maximum(jnp.minimum(n, nb - 1) - 1, 0)

    return nb, pt, blk, [
        pl.BlockSpec(blk, lambda b, p, n: (b, cur(n), kq + p)),
        pl.BlockSpec(blk, lambda b, p, n: (b, prev(n), kk + p)),
        pl.BlockSpec(blk, lambda b, p, n: (b, cur(n), kk + p)),
        pl.BlockSpec(blk, lambda b, p, n: (b, prev(n), kv + p)),
        pl.BlockSpec(blk, lambda b, p, n: (b, cur(n), kv + p)),
    ]


def _att_rows(r, dil):
    return pl.ds(r, BAND, stride=dil) if dil > 1 else pl.ds(0, BAND)


def _att_fwd(name, qk, proj, g):
    bl = qk.shape[0] // SEQ
    dil = ATT_DILATIONS[g]
    nb, pt, blk, specs = _att_specs(dil, g)
    qk3 = qk.reshape(bl, SEQ, 2 * QKV)
    proj3 = proj.reshape(bl, SEQ, NP)
    o_spec = pl.BlockSpec(blk, lambda b, p, n: (b, n, p))

    def body(q, kp, kc, vp, vc, o_ref, l_ref):
        first = pl.program_id(2) == 0

        def residue(r, carry):
            sl = _att_rows(r, dil)
            for p in range(pt):
                ln = pl.ds(p * LANES, LANES)
                o, l = _att_fn(q[sl, ln], kp[sl, ln], kc[sl, ln], vp[sl, ln], vc[sl, ln], first)
                o_ref[sl, ln] = o
                l_ref[sl, ln] = l
            return carry

        lax.fori_loop(0, dil, residue, 0)

    o, l = pl.pallas_call(
        body, name=name, grid=(bl, ATT_OUT // (pt * LANES), nb), in_specs=specs, out_specs=[o_spec, o_spec],
        out_shape=[jax.ShapeDtypeStruct((bl, SEQ, ATT_OUT), F32)] * 2,
        compiler_params=_cp(("arbitrary",) * 3),
    )(qk3, qk3, qk3, proj3, proj3)
    return o.reshape(bl * SEQ, ATT_OUT), l.reshape(bl * SEQ, ATT_OUT)


def _att_bwd(name, qk, proj, g, do, dl, dq_buf, dk_buf, dv_buf):
    bl = qk.shape[0] // SEQ
    dil = ATT_DILATIONS[g]
    nb, pt, blk, specs = _att_specs(dil, g)
    w = pt * LANES
    qk3 = qk.reshape(bl, SEQ, 2 * QKV)
    proj3 = proj.reshape(bl, SEQ, NP)
    ct_spec = pl.BlockSpec(blk, lambda b, p, n: (b, jnp.minimum(n, nb - 1), p))
    do3 = do.reshape(bl, SEQ, ATT_OUT)
    dl3 = dl.reshape(bl, SEQ, ATT_OUT)
    kg = g * ATT_OUT // w

    def body(q, kp, kc, vp, vc, do_ref, dl_ref, _a, _b, _c, d_ref, dk_ref, dv_ref, ck, cv):
        n = pl.program_id(2)

        def residue(r, carry):
            sl = _att_rows(r, dil)
            for p in range(pt):
                ln = pl.ds(p * LANES, LANES)

                @pl.when(n < nb)
                def _(ln=ln):
                    first = n == 0
                    prim = [ref[sl, ln] for ref in (q, kp, kc, vp, vc)]
                    _, vjp = jax.vjp(lambda *a: _att_fn(*a, first), *prim)
                    dq, dkp, dkc, dvp, dvc = vjp((do_ref[sl, ln], dl_ref[sl, ln]))
                    d_ref[sl, ln] = dq

                    @pl.when(n > 0)
                    def _():
                        dk_ref[sl, ln] = ck[sl, ln] + dkp
                        dv_ref[sl, ln] = cv[sl, ln] + dvp

                    ck[sl, ln] = dkc
                    cv[sl, ln] = dvc

                @pl.when(n == nb)
                def _(ln=ln):
                    dk_ref[sl, ln] = ck[sl, ln]
                    dv_ref[sl, ln] = cv[sl, ln]

            return carry

        lax.fori_loop(0, dil, residue, 0)

    bufs = [a.reshape(bl, SEQ, QKV) for a in (dq_buf, dk_buf, dv_buf)]
    o_specs = [
        pl.BlockSpec(blk, lambda b, p, n: (b, jnp.minimum(n, nb - 1), kg + p)),
        pl.BlockSpec(blk, lambda b, p, n: (b, jnp.maximum(n - 1, 0), kg + p)),
        pl.BlockSpec(blk, lambda b, p, n: (b, jnp.maximum(n - 1, 0), kg + p)),
    ]
    dq, dk, dv = pl.pallas_call(
        body, name=name, grid=(bl, ATT_OUT // w, nb + 1), in_specs=specs + [ct_spec, ct_spec, ANY, ANY, ANY],
        out_specs=o_specs, out_shape=[jax.ShapeDtypeStruct(a.shape, a.dtype) for a in bufs],
        input_output_aliases={7: 0, 8: 1, 9: 2},
        scratch_shapes=[pltpu.VMEM((BAND * dil, w), F32), pltpu.VMEM((BAND * dil, w), F32)],
        compiler_params=_cp(("arbitrary",) * 3),
    )(qk3, qk3, qk3, proj3, proj3, do3, dl3, *bufs)
    return dq.reshape(dq_buf.shape), dk.reshape(dk_buf.shape), dv.reshape(dv_buf.shape)


def _attmix(name, os_, ls_):
    t = os_[0].shape[0]
    blk = ((TM, ATT_OUT), lambda i: (i, 0))
    return _ew(name, _attmix_fn, (t // TM,), [(a, *blk) for a in (*os_, *ls_)], [((t, ATT_OUT), BF16, *blk)])[0]


def _attmix_bwd(name, os_, ls_, datt):
    t = os_[0].shape[0]
    blk = ((TM, ATT_OUT), lambda i: (i, 0))
    return _ew_bwd(name, _attmix_fn, (t // TM,), [(a, *blk) for a in (*os_, *ls_)], [(datt, *blk)],
                   [dict(idx=k, kind="tile", shape=(t, ATT_OUT), dtype=F32, block=blk[0], imap=blk[1]) for k in range(6)])


CONV_TC = 256


def _conv_operands(proj3, conv_w, conv_b):
    c0 = X0 // CONV_TC
    ins = [(proj3, (None, SEQ, CONV_TC), lambda j, b: (b, 0, c0 + j))]
    for k in range(4):
        ins.append((conv_w, (None, 1, CONV_TC), lambda j, b, k=k: (k, 0, j)))
    ins.append((conv_b, (1, CONV_TC), lambda j, b: (0, j)))
    return ins


def _conv(name, proj3, conv_w, conv_b):
    bl = proj3.shape[0]
    return _ew(name, _conv_fn, (XBC // CONV_TC, bl), _conv_operands(proj3, conv_w, conv_b),
               [((bl, SEQ, XBC), F32, (None, SEQ, CONV_TC), lambda j, b: (b, 0, j))])[0]


def _conv_bwd(name, proj3, conv_w, conv_b, dxs3, db3, dc3, dproj3):
    bl = proj3.shape[0]
    nx = D_INNER // CONV_TC
    nb_ = N_SSM_GROUPS * D_STATE // CONV_TC
    blk = (None, SEQ, CONV_TC)
    cts = [(dxs3, blk, lambda j, b: (b, 0, jnp.minimum(j, nx - 1))),
           (db3, blk, lambda j, b: (b, 0, jnp.clip(j - nx, 0, nb_ - 1))),
           (dc3, blk, lambda j, b: (b, 0, jnp.clip(j - nx - nb_, 0, nb_ - 1)))]

    def pick(cx, cb, cc):
        j = pl.program_id(0)
        return (jnp.where(j < nx, cx, jnp.where(j < nx + nb_, cb, cc)),)

    first = lambda: pl.program_id(1) == 0
    wrt = [dict(idx=0, kind="tile", shape=dproj3.shape, dtype=dproj3.dtype, block=blk,
                imap=lambda j, b: (b, 0, X0 // CONV_TC + j), into=dproj3)]
    for k in range(4):
        wrt.append(dict(idx=1 + k, kind="acc", shape=(1, XBC), dtype=F32, block=(1, CONV_TC),
                        imap=lambda j, b: (0, j), first=first))
    wrt.append(dict(idx=5, kind="acc", shape=(1, XBC), dtype=F32, block=(1, CONV_TC), imap=lambda j, b: (0, j), first=first))
    return _ew_bwd(name, _conv_fn, (XBC // CONV_TC, bl), _conv_operands(proj3, conv_w, conv_b), cts, wrt, ct_fn=pick)


SSD_TM = 256


def _ssdpre_operands(proj, dt_bias, a_log, ex):
    return [(proj, (SSD_TM, DTW), lambda i: (i, DT0 // DTW)), (dt_bias, *_full((1, DTW))), (a_log, *_full((1, DTW))),
            (ex, *_full((LANES, D_INNER)))]


def _ssdpre(name, proj, dt_bias, a_log, ex):
    t = proj.shape[0]
    blk = ((SSD_TM, D_INNER), lambda i: (i, 0))
    return _ew(name, _ssdpre_fn, (t // SSD_TM,), _ssdpre_operands(proj, dt_bias, a_log, ex),
               [((t, D_INNER), F32, *blk), ((t, D_INNER), F32, *blk)])


def _ssdpre_bwd(name, proj, dt_bias, a_log, ex, ddtb, ddab):
    t = proj.shape[0]
    blk = ((SSD_TM, D_INNER), lambda i: (i, 0))
    first = lambda: pl.program_id(0) == 0
    return _ew_bwd(name, _ssdpre_fn, (t // SSD_TM,), _ssdpre_operands(proj, dt_bias, a_log, ex),
                   [(ddtb, *blk), (ddab, *blk)],
                   [dict(idx=0, kind="tile", shape=(t, DTW), dtype=BF16, block=(SSD_TM, DTW), imap=lambda i: (i, 0)),
                    dict(idx=1, kind="acc", shape=(1, DTW), dtype=F32, block=(1, DTW), imap=lambda i: (0, 0), first=first),
                    dict(idx=2, kind="acc", shape=(1, DTW), dtype=F32, block=(1, DTW), imap=lambda i: (0, 0), first=first)])


def _ssd_in_specs(rev):
    nc = SEQ // CHUNK

    def c_(c):
        return nc - 1 - c if rev else c

    wide = (None, CHUNK, D_INNER)
    nar = (None, CHUNK, N_SSM_GROUPS * D_STATE)
    nb_ = D_INNER // (N_SSM_GROUPS * D_STATE)
    return [
        pl.BlockSpec(wide, lambda b, c: (b, c_(c), 0)),
        pl.BlockSpec(wide, lambda b, c: (b, c_(c), 0)),
        pl.BlockSpec(wide, lambda b, c: (b, c_(c), 0)),
        pl.BlockSpec(nar, lambda b, c: (b, c_(c), nb_)),
        pl.BlockSpec(nar, lambda b, c: (b, c_(c), nb_ + 1)),
        pl.BlockSpec((CHUNK, CHUNK), lambda b, c: (0, 0)),
    ], c_


def _ssd_cols(g):
    return pl.ds(g * 4 * LANES, 4 * LANES), pl.ds(g * D_STATE, D_STATE)


def _ssd_fwd(name, xc3, dtb3, dab3, ltri):
    bl = xc3.shape[0]
    nc = SEQ // CHUNK
    specs, _ = _ssd_in_specs(False)

    def body(x, dtb, dab, bm, cm, lt, y_ref, st_ref, st):
        @pl.when(pl.program_id(1) == 0)
        def _():
            st[...] = jnp.zeros_like(st)

        ltv = lt[...]
        for g in range(N_SSM_GROUPS):
            wl, nl = _ssd_cols(g)
            s0 = st[g]
            st_ref[g] = s0
            new_st, y = _ssd_step(s0, x[:, wl], dtb[:, wl], dab[:, wl], bm[:, nl], cm[:, nl], ltv)
            y_ref[:, wl] = y
            st[g] = new_st

    return pl.pallas_call(
        body, name=name, grid=(bl, nc), in_specs=specs,
        out_specs=[pl.BlockSpec((None, CHUNK, D_INNER), lambda b, c: (b, c, 0)),
                   pl.BlockSpec((None, N_SSM_GROUPS, None, D_STATE, 4 * LANES), lambda b, c: (b, 0, c, 0, 0))],
        out_shape=[jax.ShapeDtypeStruct((bl, SEQ, D_INNER), F32),
                   jax.ShapeDtypeStruct((bl, N_SSM_GROUPS, nc, D_STATE, 4 * LANES), F32)],
        scratch_shapes=[pltpu.VMEM((N_SSM_GROUPS, D_STATE, 4 * LANES), F32)],
        compiler_params=_cp(("arbitrary",) * 2),
    )(xc3, dtb3, dab3, xc3, xc3, ltri)


def _ssd_bwd(name, xc3, dtb3, dab3, ltri, states, dy3, dxs_part3):
    bl = xc3.shape[0]
    nc = SEQ // CHUNK
    specs, c_ = _ssd_in_specs(True)
    wide = pl.BlockSpec((None, CHUNK, D_INNER), lambda b, c: (b, c_(c), 0))
    nar = pl.BlockSpec((None, CHUNK, N_SSM_GROUPS * D_STATE), lambda b, c: (b, c_(c), 0))
    st_spec = pl.BlockSpec((None, N_SSM_GROUPS, None, D_STATE, 4 * LANES), lambda b, c: (b, 0, c_(c), 0, 0))

    def body(x, dtb, dab, bm, cm, lt, st_ref, dy, dxp, dx_ref, ddtb_ref, ddab_ref, dbm_ref, dcm_ref, dst):
        @pl.when(pl.program_id(1) == 0)
        def _():
            dst[...] = jnp.zeros_like(dst)

        ltv = lt[...]
        for g in range(N_SSM_GROUPS):
            wl, nl = _ssd_cols(g)
            _, vjp = jax.vjp(lambda *a: _ssd_step(*a, ltv), st_ref[g], x[:, wl], dtb[:, wl], dab[:, wl], bm[:, nl], cm[:, nl])
            d_st, d_x, d_dtb, d_dab, d_bm, d_cm = vjp((dst[g], dy[:, wl]))
            dst[g] = d_st
            dx_ref[:, wl] = d_x + dxp[:, wl]
            ddtb_ref[:, wl] = d_dtb
            ddab_ref[:, wl] = d_dab
            dbm_ref[:, nl] = d_bm
            dcm_ref[:, nl] = d_cm

    big = jax.ShapeDtypeStruct((bl, SEQ, D_INNER), F32)
    small = jax.ShapeDtypeStruct((bl, SEQ, N_SSM_GROUPS * D_STATE), F32)
    return pl.pallas_call(
        body, name=name, grid=(bl, nc), in_specs=specs + [st_spec, wide, wide],
        out_specs=[wide, wide, wide, nar, nar], out_shape=[big, big, big, small, small],
        scratch_shapes=[pltpu.VMEM((N_SSM_GROUPS, D_STATE, 4 * LANES), F32)],
        compiler_params=_cp(("arbitrary",) * 2),
    )(xc3, dtb3, dab3, xc3, xc3, ltri, states, dy3, dxs_part3)


def _ssdpost_operands(y, xc, proj, d_skip, ex, nw):
    w = 4 * LANES
    return [(y, (SSD_TM, w), lambda j, i: (i, j)), (xc, (SSD_TM, w), lambda j, i: (i, j)),
            (proj, (SSD_TM, w), lambda j, i: (i, Z0 // w + j)), (d_skip, (1, DTW), lambda j, i: (0, 0)),
            (ex, (LANES, w), lambda j, i: (0, j)), (nw, (1, w), lambda j, i: (0, j))]


def _ssdpost(name, y, xc, proj, d_skip, ex, nw):
    t = y.shape[0]
    w = 4 * LANES
    return _ew(name, _ssdpost_fn, (D_INNER // w, t // SSD_TM), _ssdpost_operands(y, xc, proj, d_skip, ex, nw),
               [((t, D_INNER), BF16, (SSD_TM, w), lambda j, i: (i, j))])[0]


def _ssdpost_bwd(name, y, xc, proj, d_skip, ex, nw, dysn, dproj):
    t = y.shape[0]
    w = 4 * LANES
    blk = ((SSD_TM, w), lambda j, i: (i, j))
    return _ew_bwd(name, _ssdpost_fn, (D_INNER // w, t // SSD_TM), _ssdpost_operands(y, xc, proj, d_skip, ex, nw),
                   [(dysn, *blk)],
                   [dict(idx=0, kind="tile", shape=(t, D_INNER), dtype=F32, block=blk[0], imap=blk[1]),
                    dict(idx=1, kind="tile", shape=(t, D_INNER), dtype=F32, block=blk[0], imap=blk[1]),
                    dict(idx=2, kind="tile", shape=dproj.shape, dtype=dproj.dtype, block=blk[0],
                         imap=lambda j, i: (i, Z0 // w + j), into=dproj),
                    dict(idx=3, kind="acc", shape=(1, DTW), dtype=F32, block=(1, DTW), imap=lambda j, i: (0, 0),
                         first=lambda: (pl.program_id(0) == 0) & (pl.program_id(1) == 0)),
                    dict(idx=5, kind="acc", shape=(1, D_INNER), dtype=F32, block=(1, w), imap=lambda j, i: (0, j),
                         first=lambda: pl.program_id(1) == 0)])


def _merge_operands(ya, ys, proj, b_gates):
    w = 4 * LANES
    g0 = G0 // w
    nh = D_MODEL // w
    return [(ya, (TM, w), lambda j, i: (i, j)), (ys, (TM, w), lambda j, i: (i, j)),
            (proj, (TM, w), lambda j, i: (i, g0 + j)), (proj, (TM, w), lambda j, i: (i, g0 + nh + j)),
            (b_gates, (1, w), lambda j, i: (0, j)), (b_gates, (1, w), lambda j, i: (0, nh + j))]


def _merge(name, ya, ys, proj, b_gates):
    t = ya.shape[0]
    w = 4 * LANES
    return _ew(name, _merge_fn, (D_MODEL // w, t // TM), _merge_operands(ya, ys, proj, b_gates),
               [((t, D_MODEL), BF16, (TM, w), lambda j, i: (i, j))])[0]


def _merge_bwd(name, ya, ys, proj, b_gates, dmixed):
    t = ya.shape[0]
    w = 4 * LANES
    blk = ((TM, w), lambda j, i: (i, j))
    first = lambda: pl.program_id(1) == 0
    tile = lambda k, dt: dict(idx=k, kind="tile", shape=(t, D_MODEL), dtype=dt, block=blk[0], imap=blk[1])
    acc = lambda k: dict(idx=k, kind="acc", shape=(1, D_MODEL), dtype=F32, block=(1, w), imap=lambda j, i: (0, j), first=first)
    return _ew_bwd(name, _merge_fn, (D_MODEL // w, t // TM), _merge_operands(ya, ys, proj, b_gates), [(dmixed, *blk)],
                   [tile(0, BF16), tile(1, BF16), tile(2, BF16), tile(3, BF16), acc(4), acc(5)])


def _loss(name, y, tgt):
    t = y.shape[0]
    blk = pl.BlockSpec((TM, D_MODEL), lambda i: (i, 0))

    def body(y_ref, t_ref, dy_ref, l_ref):
        e = y_ref[...] - t_ref[...]
        dy_ref[...] = e * (1.0 / D_MODEL)
        part = jnp.sum(jnp.sum(e * e, axis=-1, keepdims=True), axis=0, keepdims=True) * (0.5 / D_MODEL)
        part = jnp.broadcast_to(part, (8, LANES))

        @pl.when(pl.program_id(0) == 0)
        def _():
            l_ref[...] = part

        @pl.when(pl.program_id(0) > 0)
        def _():
            l_ref[...] += part

    return pl.pallas_call(
        body, name=name, grid=(t // TM,), in_specs=[blk, blk],
        out_specs=[blk, pl.BlockSpec((8, LANES), lambda i: (0, 0))],
        out_shape=[jax.ShapeDtypeStruct((t, D_MODEL), F32), jax.ShapeDtypeStruct((8, LANES), F32)],
        compiler_params=_cp(("arbitrary",)),
    )(y, tgt)


def _adamw_fn(w, g, m, v):
    m2 = B1 * m + (1.0 - B1) * g
    v2 = B2 * v + (1.0 - B2) * (g * g)
    m_hat = m2 / (1.0 - B1 ** STEP)
    v_hat = v2 / (1.0 - B2 ** STEP)
    return -LR * (m_hat / (jnp.sqrt(v_hat) + ADAM_EPS) + WD * w), m2, v2


def _adamw(name, w, g, m, v):
    rows, cols = w.shape
    tm = rows
    for cand in (512, 256, 128, 64, 32, 16, 8):
        if rows % cand == 0 and cand * cols * 4 <= (1 << 21):
            tm = cand
            break
    blk = ((tm, cols), lambda i: (i, 0))
    return _ew(name, _adamw_fn, (rows // tm,), [(a, *blk) for a in (w, g, m, v)], [((rows, cols), F32, *blk)] * 3)


NCH = 4
TMM = 1024
TKK = 1024


def _ffn_fwd(tag, x, nw, wg, wu, wd, li):
    t, fc = x.shape[0], wg.shape[-1]
    h = _rmsnorm(tag + "_norm", x, nw)

    def up_body(h_ref, wg_ref, wu_ref, g_ref, u_ref, a_ref):
        hv = h_ref[...]
        g = _dg(hv, wg_ref[...], 1, 0).astype(BF16)
        u = _dg(hv, wu_ref[...], 1, 0).astype(BF16)
        g_ref[...] = g
        u_ref[...] = u
        a_ref[...] = _swiglu_fn(g.astype(F32), u.astype(F32))[0].astype(BF16)

    w_spec = pl.BlockSpec((None, None, D_MODEL, fc), lambda k, i: (k, li, 0, 0))
    o_spec = pl.BlockSpec((None, TMM, fc), lambda k, i: (k, i, 0))
    g, u, a = pl.pallas_call(
        up_body, name=tag + "_up_act", grid=(NCH, t // TMM),
        in_specs=[pl.BlockSpec((TMM, D_MODEL), lambda k, i: (i, 0)), w_spec, w_spec], out_specs=[o_spec] * 3,
        out_shape=[jax.ShapeDtypeStruct((NCH, t, fc), BF16)] * 3, compiler_params=_cp(("parallel", "parallel")),
    )(h, wg, wu)
    row = ((TMM, D_MODEL), lambda i, j, k: (i, 0))
    y = _mmx(tag + "_down", (t // TMM, 1, NCH),
             (a, (None, TMM, fc), lambda i, j, k: (k, i, 0)),
             (wd, (None, None, fc, D_MODEL), lambda i, j, k: (k, li, 0, 0)),
             ((t, D_MODEL), F32, *row), (1, 0), alpha=0.5, res=(x, *row))
    return y, (x, h, g, u, a)


def _ffn_bwd(tag, saved, nw, wg, wu, wd, li, dy, bufs, gl):
    x, h, g, u, a = saved
    t, fc = x.shape[0], wg.shape[-1]
    bg, bu, bd = bufs
    def dact_body(dy_ref, wd_ref, g_ref, u_ref, dg_ref, du_ref):
        da = _dg(dy_ref[...], wd_ref[...], 1, 1) * 0.5
        _, vjp = jax.vjp(_swiglu_fn, g_ref[...].astype(F32), u_ref[...].astype(F32))
        dg, du = vjp((da,))
        dg_ref[...] = dg.astype(BF16)
        du_ref[...] = du.astype(BF16)

    c_spec = pl.BlockSpec((None, TMM, fc), lambda k, i: (k, i, 0))
    dg, du = pl.pallas_call(
        dact_body, name=tag + "_down_dx_act", grid=(NCH, t // TMM),
        in_specs=[pl.BlockSpec((TMM, D_MODEL), lambda k, i: (i, 0)),
                  pl.BlockSpec((None, None, fc, D_MODEL), lambda k, i: (k, li, 0, 0)), c_spec, c_spec],
        out_specs=[c_spec] * 2, out_shape=[jax.ShapeDtypeStruct((NCH, t, fc), BF16)] * 2,
        compiler_params=_cp(("parallel", "parallel")),
    )(dy, wd, g, u)
    bd = _mmx(tag + "_down_dw", (NCH, 1, t // TKK),
              (a, (None, TKK, fc), lambda k, j, kk: (k, kk, 0)),
              (dy, (TKK, D_MODEL), lambda k, j, kk: (kk, 0)),
              (bd.shape, BF16, (None, None, fc, D_MODEL), lambda k, j, kk: (gl, k, 0, 0)), (0, 0), alpha=0.5, into=bd)
    def dw(name, d, buf):
        return _mmx(name, (NCH, 1, t // TKK),
                    (h, (TKK, D_MODEL), lambda k, i, kk: (kk, 0)),
                    (d, (None, TKK, fc), lambda k, i, kk: (k, kk, 0)),
                    (buf.shape, BF16, (None, None, D_MODEL, fc), lambda k, i, kk: (gl, k, 0, 0)), (0, 0), into=buf)

    bg, bu = dw(tag + "_gate_dw", dg, bg), dw(tag + "_up_dw", du, bu)
    row = ((TMM, D_MODEL), lambda i, j, k: (i, 0))

    def dx_(name, d, w, res):
        return _mmx(name, (t // TMM, 1, NCH),
                    (d, (None, TMM, fc), lambda i, j, k: (k, i, 0)),
                    (w, (None, None, D_MODEL, fc), lambda i, j, k: (k, li, 0, 0)),
                    ((t, D_MODEL), F32, *row), (1, 1), res=None if res is None else (res, *row))

    dh = dx_(tag + "_up_dx", du, wu, dx_(tag + "_gate_dx", dg, wg, None))
    dx, dnw = _rmsnorm_bwd(tag + "_norm_bwd", x, nw, dh, dy)
    return dx, dnw, (bg, bu, bd)


def _mixer_fwd(tag, x, p, c):
    t = x.shape[0]
    bl = t // SEQ
    h = _rmsnorm(tag + "_norm", x, p["mix_norm_w"])
    proj = _mm(tag + "_in", h, p["w_in"], "nn")
    qk = _qkprep(tag + "_qk", proj, p["qkw"], c["cos"], c["sin"], c)
    os_, ls_ = [], []
    for g in range(3):
        o, l = _att_fwd(f"{tag}_att{g}", qk, proj, g)
        os_.append(o)
        ls_.append(l)
    att = _attmix(tag + "_attmix", os_, ls_)
    li = p["layer"]
    wa, ws, wo = p["w_att_proj"], p["w_ssm_proj"], p["w_out"]
    ca, cs, co = wa.shape[-1], ws.shape[-2], wo.shape[-2]
    row = ((TMM, D_MODEL), lambda i, j, k: (i, 0))
    ya = _mmx(tag + "_attproj", (t // TMM, NCH, 1),
              (att, (TMM, ATT_OUT), lambda i, k, kk: (i, 0)),
              (wa, (None, None, ATT_OUT, ca), lambda i, k, kk: (k, li, 0, 0)),
              ((t, D_MODEL), F32, (TMM, ca), lambda i, k, kk: (i, k)), (1, 0))
    proj3 = proj.reshape(bl, SEQ, NP)
    xc3 = _conv(tag + "_conv", proj3, p["conv_w"], p["conv_b"])
    xc = xc3.reshape(t, XBC)
    dtb, dab = _ssdpre(tag + "_ssdpre", proj, p["dt_bias"], p["a_log"], c["ex"])
    dtb3, dab3 = dtb.reshape(bl, SEQ, D_INNER), dab.reshape(bl, SEQ, D_INNER)
    y3, states = _ssd_fwd(tag + "_ssd", xc3, dtb3, dab3, c["ltri"])
    y = y3.reshape(t, D_INNER)
    ysn = _ssdpost(tag + "_ssdpost", y, xc, proj, p["d_skip"], c["ex"], p["ssm_norm_w"])
    ys = _mmx(tag + "_ssmproj", (t // TMM, 1, NCH),
              (ysn, (TMM, cs), lambda i, j, k: (i, k)),
              (ws, (None, None, cs, D_MODEL), lambda i, j, k: (k, li, 0, 0)),
              ((t, D_MODEL), F32, *row), (1, 0))
    mixed = _merge(tag + "_merge", ya, ys, proj, p["b_gates"])
    out = _mmx(tag + "_out", (t // TMM, 1, NCH),
               (mixed, (TMM, co), lambda i, j, k: (i, k)),
               (wo, (None, None, co, D_MODEL), lambda i, j, k: (k, li, 0, 0)),
               ((t, D_MODEL), F32, *row), (1, 0), res=(x, *row))
    return out, (x, h, proj, qk, os_, ls_, att, ya, xc3, dtb3, dab3, states, y, ysn, ys, mixed)


def _mixer_bwd(tag, saved, p, c, dout, bufs):
    x, h, proj, qk, os_, ls_, att, ya, xc3, dtb3, dab3, states, y, ysn, ys, mixed = saved
    t = x.shape[0]
    bl = t // SEQ
    xc = xc3.reshape(t, XBC)
    proj3 = proj.reshape(bl, SEQ, NP)
    gr = {}
    li, gl = p["layer"], p["global_layer"]
    wa, ws, wo = p["w_att_proj"], p["w_ssm_proj"], p["w_out"]
    ca, cs, co = wa.shape[-1], ws.shape[-2], wo.shape[-2]
    b_att, b_ssm, b_out = bufs

    def chunk_dx(name, d, w, cw):
        return _mmx(name, (t // TMM, NCH, D_MODEL // TKK),
                    (d, (TMM, TKK), lambda i, k, kk: (i, kk)),
                    (w, (None, None, cw, TKK), lambda i, k, kk: (k, li, 0, kk)),
                    ((t, NCH * cw), F32, (TMM, cw), lambda i, k, kk: (i, k)), (1, 1))

    def full_dw(name, a_, d, buf):
        kdim = a_.shape[1]
        tm = min(kdim, 1024)
        return _mmx(name, (kdim // tm, 1, t // TKK),
                    (a_, (TKK, tm), lambda i, j, kk: (kk, i)),
                    (d, (TKK, D_MODEL), lambda i, j, kk: (kk, 0)),
                    (buf.shape, BF16, (None, tm, D_MODEL), lambda i, j, kk: (gl, i, 0)), (0, 0), into=buf)

    dmixed = chunk_dx(tag + "_out_dx", dout, wo, co)
    b_out = full_dw(tag + "_out_dw", mixed, dout, b_out)
    dya, dys, dga, dgs, dba, dbs = _merge_bwd(tag + "_merge_bwd", ya, ys, proj, p["b_gates"], dmixed)
    gr["b_gates"] = jnp.concatenate([dba, dbs], axis=1)
    datt = _mmx(tag + "_attproj_dx", (t // TMM, 1, NCH),
                (dya, (TMM, ca), lambda i, j, k: (i, k)),
                (wa, (None, None, ATT_OUT, ca), lambda i, j, k: (k, li, 0, 0)),
                ((t, ATT_OUT), F32, (TMM, ATT_OUT), lambda i, j, k: (i, 0)), (1, 1))
    b_att = _mmx(tag + "_attproj_dw", (NCH, 1, t // TKK),
                 (att, (TKK, ATT_OUT), lambda k, j, kk: (kk, 0)),
                 (dya, (TKK, ca), lambda k, j, kk: (kk, k)),
                 (b_att.shape, BF16, (None, None, ATT_OUT, ca), lambda k, j, kk: (gl, k, 0, 0)), (0, 0), into=b_att)
    dysn = chunk_dx(tag + "_ssmproj_dx", dys, ws, cs)
    b_ssm = full_dw(tag + "_ssmproj_dw", ysn, dys, b_ssm)
    gr["bufs"] = (b_att, b_ssm, b_out)
    dmix = _attmix_bwd(tag + "_attmix_bwd", os_, ls_, datt)
    dq = dk = dv = jnp.zeros((t, QKV), F32)
    for g in range(3):
        dq, dk, dv = _att_bwd(f"{tag}_att{g}_bwd", qk, proj, g, dmix[g], dmix[3 + g], dq, dk, dv)
    dproj = jnp.zeros((t, NP), BF16)
    dproj = lax.dynamic_update_slice(dproj, dv.astype(BF16), (0, V0))
    dproj = lax.dynamic_update_slice(dproj, dga, (0, G0))
    dproj = lax.dynamic_update_slice(dproj, dgs, (0, G0 + D_MODEL))
    dproj, gr["qkw"] = _qkprep_bwd(tag + "_qk_bwd", proj, p["qkw"], c["cos"], c["sin"], c, dq, dk, dproj)
    dy, dxs_part, dproj, gr["d_skip"], gr["ssm_norm_w"] = _ssdpost_bwd(
        tag + "_ssdpost_bwd", y, xc, proj, p["d_skip"], c["ex"], p["ssm_norm_w"], dysn, dproj)
    dxs3, ddtb3, ddab3, db3, dc3 = _ssd_bwd(
        tag + "_ssd_bwd", xc3, dtb3, dab3, c["ltri"], states, dy.reshape(bl, SEQ, D_INNER), dxs_part.reshape(bl, SEQ, D_INNER))
    ddt, gr["dt_bias"], gr["a_log"] = _ssdpre_bwd(
        tag + "_ssdpre_bwd", proj, p["dt_bias"], p["a_log"], c["ex"], ddtb3.reshape(t, D_INNER), ddab3.reshape(t, D_INNER))
    dproj = lax.dynamic_update_slice(dproj, ddt, (0, DT0))
    dproj3, dcw0, dcw1, dcw2, dcw3, gr["conv_b"] = _conv_bwd(
        tag + "_conv_bwd", proj3, p["conv_w"], p["conv_b"], dxs3, db3, dc3, dproj.reshape(bl, SEQ, NP))
    dproj = dproj3.reshape(t, NP)
    gr["conv_w"] = jnp.concatenate([dcw0, dcw1, dcw2, dcw3], axis=0)
    gr["w_in"] = _mm(tag + "_in_dw", h, dproj, "tn", out_dtype=BF16)
    dh = _mm(tag + "_in_dx", dproj, p["w_in"], "nt")
    dx, gr["mix_norm_w"] = _rmsnorm_bwd(tag + "_norm_bwd", x, p["mix_norm_w"], dh, dout)
    return dx, gr


def _constants():
    cos, sin = _rope_tables()
    return dict(cos=cos, sin=sin, hmean=_head_mean_mat(),
                ex=_head_expand_mat(), ltri=_ltri_mat())


ANY = pl.BlockSpec(memory_space=pl.ANY)


def _mesh_pos():
    return lax.axis_index("x"), lax.axis_index("y"), lax.axis_index("c")


def _other_chips(x, y):
    return [(1 - x, y), (x, 1 - y), (1 - x, 1 - y)]


def _gather_exchange(srcs, outs, send_sems, recv_sems):
    n = len(srcs)
    x, y, c = _mesh_pos()
    chips = _other_chips(x, y)

    def part(a, chip, hf):
        h = srcs[a].shape[1] // 2
        return outs[a].at[2 * chip[0] + chip[1], :, pl.ds(hf * h, h), :]

    def mine(a):
        h = srcs[a].shape[1] // 2
        return srcs[a].at[:, pl.ds(c * h, h), :]

    def copy(a, k, src_ref, dst_ref, to):
        return pltpu.make_async_remote_copy(src_ref=src_ref, dst_ref=dst_ref, send_sem=send_sems.at[6 * a + k],
                                            recv_sem=recv_sems.at[6 * a + k], device_id=to, device_id_type=MESH)

    first = [copy(a, j, mine(a), part(a, (x, y), c), (*chip, c)) for a in range(n) for j, chip in enumerate(chips)]
    for cp in first:
        cp.start()
    passed = []
    for a in range(n):
        for j, chip in enumerate(chips):
            copy(a, j, part(a, chip, c), part(a, chip, c), (x, y, c)).wait_recv()
            fw = copy(a, 3 + j, part(a, chip, c), part(a, chip, c), (x, y, 1 - c))
            fw.start()
            passed.append(fw)
    for a in range(n):
        for j, chip in enumerate(chips):
            copy(a, 3 + j, part(a, chip, 1 - c), part(a, chip, 1 - c), (x, y, c)).wait_recv()
    for cp in first + passed:
        cp.wait_send()


def _gather_inits(ws, chip_idx):
    return [lax.dynamic_update_slice(jnp.zeros((NCH, *w.shape), w.dtype), w[None], (chip_idx[0], 0, 0, 0)) for w in ws]


def _all_gather_weights(ws, chip_idx):
    n = len(ws)
    inits = _gather_inits(ws, chip_idx)

    def body(*refs):
        _gather_exchange(refs[:n], refs[2 * n:3 * n], refs[3 * n], refs[3 * n + 1])

    return pl.pallas_call(
        body, name="all_gather_weights", out_shape=[jax.ShapeDtypeStruct(i.shape, i.dtype) for i in inits],
        in_specs=[ANY] * (2 * n), out_specs=[ANY] * n, input_output_aliases={n + a: a for a in range(n)},
        scratch_shapes=[pltpu.SemaphoreType.DMA((6 * n,)), pltpu.SemaphoreType.DMA((6 * n,))],
    )(*ws, *inits)


def _all_gather_weights_beside(name, ws, chip_idx):
    n = len(ws)
    src_refs = [jax.new_ref(w, memory_space=pltpu.MemorySpace.HBM) for w in ws]
    out_refs = [jax.new_ref(i, memory_space=pltpu.MemorySpace.HBM) for i in _gather_inits(ws, chip_idx)]

    @pl.kernel(mesh=plsc.ScalarSubcoreMesh(axis_name="sequencer", num_cores=1), name=name,
               scratch_types=(pltpu.SemaphoreType.DMA((6 * n,)), pltpu.SemaphoreType.DMA((6 * n,))),
               compiler_params=pltpu.CompilerParams(collective_id=1))
    def launch(send_sems, recv_sems):
        x, y, c = _mesh_pos()
        barrier = pltpu.get_barrier_semaphore()
        for peer in [(x, y, 1 - c)] + [(*chip, c) for chip in _other_chips(x, y)]:
            pl.semaphore_signal(barrier, inc=1, device_id=peer, device_id_type=MESH)
        pl.semaphore_wait(barrier, 4)
        _gather_exchange(src_refs, out_refs, send_sems, recv_sems)

    launch()
    return [r[...] for r in out_refs]


def _pair_exchange(name, gs):
    n = len(gs)

    def body(*refs):
        srcs, outs, send_sems, recv_sems = refs[:n], refs[n:2 * n], refs[2 * n], refs[2 * n + 1]
        x, y, c = _mesh_pos()
        cps = []
        for a in range(n):
            h = gs[a].shape[2] // 2
            cps.append(pltpu.make_async_remote_copy(
                src_ref=srcs[a].at[:, :, pl.ds((1 - c) * h, h), :], dst_ref=outs[a], send_sem=send_sems.at[a],
                recv_sem=recv_sems.at[a], device_id=(x, y, 1 - c), device_id_type=MESH))
        for cp in cps:
            cp.start()
        for cp in cps:
            cp.wait()

    return pl.pallas_call(
        body, name=name,
        out_shape=[jax.ShapeDtypeStruct((g.shape[0], g.shape[1], g.shape[2] // 2, g.shape[3]), g.dtype) for g in gs],
        in_specs=[ANY] * n, out_specs=[ANY] * n,
        scratch_shapes=[pltpu.SemaphoreType.DMA((n,)), pltpu.SemaphoreType.DMA((n,))],
    )(*gs)


def _chip_exchange_copies(srcs, outs, send_sems, recv_sems):
    x, y, c = _mesh_pos()
    cps = [pltpu.make_async_remote_copy(
        src_ref=srcs[a].at[:, 2 * chip[0] + chip[1]], dst_ref=outs[a].at[j], send_sem=send_sems.at[3 * a + j],
        recv_sem=recv_sems.at[3 * a + j], device_id=(*chip, c), device_id_type=MESH)
        for a in range(len(srcs)) for j, chip in enumerate(_other_chips(x, y))]
    for cp in cps:
        cp.start()
    for cp in cps:
        cp.wait()


def _chip_exchange_shapes(hs):
    return [jax.ShapeDtypeStruct((3, h.shape[0], h.shape[2], h.shape[3]), h.dtype) for h in hs]


def _chip_exchange(hs):
    n = len(hs)

    def body(*refs):
        _chip_exchange_copies(refs[:n], refs[n:2 * n], refs[2 * n], refs[2 * n + 1])

    return pl.pallas_call(
        body, name="grad_chip_exchange", out_shape=_chip_exchange_shapes(hs),
        in_specs=[ANY] * n, out_specs=[ANY] * n,
        scratch_shapes=[pltpu.SemaphoreType.DMA((3 * n,)), pltpu.SemaphoreType.DMA((3 * n,))],
    )(*hs)


def _zero_after(v):
    return jnp.minimum(lax.bitcast_convert_type(v, jnp.uint16).astype(jnp.int32), 0).astype(F32)


def _chip_exchange_beside(name, hs):
    n = len(hs)
    src_refs = [jax.new_ref(h, memory_space=pltpu.MemorySpace.HBM) for h in hs]
    out_refs = [jax.empty_ref(s, memory_space=pltpu.MemorySpace.HBM) for s in _chip_exchange_shapes(hs)]

    @pl.kernel(mesh=plsc.ScalarSubcoreMesh(axis_name="sequencer", num_cores=1), name=name,
               scratch_types=(pltpu.SemaphoreType.DMA((3 * n,)), pltpu.SemaphoreType.DMA((3 * n,))),
               compiler_params=pltpu.CompilerParams(collective_id=2))
    def launch(send_sems, recv_sems):
        x, y, c = _mesh_pos()
        barrier = pltpu.get_barrier_semaphore()
        for chip in _other_chips(x, y):
            pl.semaphore_signal(barrier, inc=1, device_id=(*chip, c), device_id_type=MESH)
        pl.semaphore_wait(barrier, 3)
        _chip_exchange_copies(src_refs, out_refs, send_sems, recv_sems)

    launch()
    return [r[...] for r in out_refs]


def _pair_share(rs):
    n = len(rs)

    def body(*refs):
        outs, send_sems, recv_sems = refs[n:2 * n], refs[2 * n], refs[2 * n + 1]
        x, y, c = _mesh_pos()
        cps = [pltpu.make_async_remote_copy(src_ref=outs[a].at[:, c], dst_ref=outs[a].at[:, c], send_sem=send_sems.at[a],
                                            recv_sem=recv_sems.at[a], device_id=(x, y, 1 - c), device_id_type=MESH)
               for a in range(n)]
        for cp in cps:
            cp.start()
        for a in range(n):
            pltpu.make_async_remote_copy(src_ref=outs[a].at[:, 1 - c], dst_ref=outs[a].at[:, 1 - c],
                                         send_sem=send_sems.at[a], recv_sem=recv_sems.at[a], device_id=(x, y, c),
                                         device_id_type=MESH).wait_recv()
        for cp in cps:
            cp.wait_send()

    return pl.pallas_call(
        body, name="grad_pair_share", out_shape=[jax.ShapeDtypeStruct(r.shape, r.dtype) for r in rs],
        in_specs=[ANY] * n, out_specs=[ANY] * n, input_output_aliases={a: a for a in range(n)},
        scratch_shapes=[pltpu.SemaphoreType.DMA((n,)), pltpu.SemaphoreType.DMA((n,))],
    )(*rs)


def _pair_sum(name, g, recv, c_idx):
    d, k, h, b = recv.shape
    g5 = g.reshape(d * k, 2, h, b)

    def body(c_ref, a_ref, b_ref, o_ref):
        o_ref[...] = (a_ref[...].astype(F32) + b_ref[...].astype(F32)).astype(o_ref.dtype)

    out = pl.pallas_call(
        body, name=name,
        grid_spec=pltpu.PrefetchScalarGridSpec(
            num_scalar_prefetch=1, grid=(d * k,),
            in_specs=[pl.BlockSpec((None, None, h, b), lambda i, c: (i, c[0], 0, 0)),
                      pl.BlockSpec((None, h, b), lambda i, c: (i, 0, 0))],
            out_specs=pl.BlockSpec((None, h, b), lambda i, c: (i, 0, 0))),
        out_shape=jax.ShapeDtypeStruct((d * k, h, b), BF16),
        compiler_params=_cp(("arbitrary",)),
    )(c_idx, g5, recv.reshape(d * k, h, b))
    return out.reshape(d, k, h, b)


def _chip_sum(name, ha, recv, chip_idx, c_idx, depth, l0, into=None):
    d, _, h, b = ha.shape

    def body(k_ref, c_ref, a_ref, r0, r1, r2, *rest):
        rest[-1][...] = ((a_ref[...].astype(F32) + r0[...].astype(F32)) + r1[...].astype(F32)) + r2[...].astype(F32)

    blk = (None, None, h, b)
    extra = [] if into is None else [into]
    return pl.pallas_call(
        body, name=name,
        grid_spec=pltpu.PrefetchScalarGridSpec(
            num_scalar_prefetch=2, grid=(d,),
            in_specs=[pl.BlockSpec(blk, lambda l, k, c: (l, k[0], 0, 0))] +
                     [pl.BlockSpec(blk, lambda l, k, c, j=j: (j, l, 0, 0)) for j in range(3)] + [ANY] * len(extra),
            out_specs=pl.BlockSpec(blk, lambda l, k, c: (l0 + l, c[0], 0, 0))),
        out_shape=jax.ShapeDtypeStruct((depth, 2, h, b), F32),
        input_output_aliases={} if into is None else {6: 0},
        compiler_params=_cp(("arbitrary",)),
    )(chip_idx, c_idx, ha, recv, recv, recv, *extra)


def _all_sum_small(name, vec):
    rows = vec.shape[0]

    def body(v_ref, o_ref, buf, send_sems, recv_sems):
        x, y, c = _mesh_pos()
        me, sibling = (x, y, c), (x, y, 1 - c)
        chips = _other_chips(x, y)

        def slot(p):
            return buf.at[4 * p[0] + 2 * p[1] + p[2]]

        def copy(k, block, to, src=None):
            return pltpu.make_async_remote_copy(src_ref=slot(block) if src is None else src, dst_ref=slot(block),
                                                send_sem=send_sems.at[k], recv_sem=recv_sems.at[k],
                                                device_id=to, device_id_type=MESH)

        first = [copy(0, me, sibling, src=v_ref)]
        first += [copy(1 + j, me, (*chip, c), src=v_ref) for j, chip in enumerate(chips)]
        for cp in first:
            cp.start()
        passed = [copy(4 + j, (*chip, c), sibling) for j, chip in enumerate(chips)]
        for j, chip in enumerate(chips):
            copy(1 + j, (*chip, c), me).wait_recv()
            passed[j].start()
        copy(0, sibling, me).wait_recv()
        for j, chip in enumerate(chips):
            copy(4 + j, (*chip, 1 - c), me).wait_recv()
        for cp in first + passed:
            cp.wait_send()
        slot(me)[...] = v_ref[...]
        acc = buf[0]
        for k in range(1, 8):
            acc = acc + buf[k]
        o_ref[...] = acc

    vm = pl.BlockSpec(memory_space=pltpu.VMEM)
    return pl.pallas_call(
        body, name=name, out_shape=jax.ShapeDtypeStruct((rows, LANES), F32),
        in_specs=[vm], out_specs=vm, compiler_params=pltpu.CompilerParams(vmem_limit_bytes=VMEM_LIMIT),
        scratch_shapes=[pltpu.VMEM((8, rows, LANES), F32), pltpu.SemaphoreType.DMA((7,)), pltpu.SemaphoreType.DMA((7,))],
    )(vec)


def _pad_lanes(v, n=LANES):
    return jnp.pad(v, (0, n - v.shape[0]))[None, :]


def _w_in_to_kernel(w):
    return jnp.concatenate([w[:, :6656], w[:, 9760:N_IN], w[:, 6656:9728], w[:, 9728:9760],
                            jnp.zeros((w.shape[0], NP - N_IN), w.dtype)], axis=1)


def _w_in_from_kernel(w):
    return jnp.concatenate([w[:, :6656], w[:, X0:DT0], w[:, DT0:DT0 + 32], w[:, G0:X0]], axis=1)


def _layer_params(big, small, i):
    p = {k: big[k][i][0] for k in GRAD_BUFS}
    li = big["w_in"][i][1]
    p["layer"], p["global_layer"] = li, i
    w_in = big["w_in"][i][0]
    cw = w_in.shape[-1]
    pieces = []
    for lo, hi in ((0, 6656), (9760, N_IN), (6656, 9728), (9728, 9760)):
        for k in range(NCH):
            a, b = max(lo, k * cw), min(hi, (k + 1) * cw)
            if a < b:
                pieces.append(w_in[k, li, :, a - k * cw:b - k * cw])
    p["w_in"] = jnp.concatenate(pieces + [jnp.zeros((D_MODEL, NP - N_IN), w_in.dtype)], axis=1)
    p["conv_w"] = big["conv_w"][i][:, None, :]
    for k in ("ffn1_norm_w", "mix_norm_w", "ffn2_norm_w", "b_gates", "conv_b", "ssm_norm_w"):
        p[k] = small[k][i][None, :]
    for k in ("dt_bias", "a_log", "d_skip"):
        p[k] = _pad_lanes(small[k][i])
    p["qkw"] = jnp.stack([_pad_lanes(small["q_norm_w"][i]), _pad_lanes(small["k_norm_w"][i])])
    return p


GRAD_BUFS = ("ffn1_w_gate", "ffn1_w_up", "ffn1_w_down", "w_att_proj", "w_ssm_proj", "w_out",
             "ffn2_w_gate", "ffn2_w_up", "ffn2_w_down")


def _local_step(x, tgt, layers, c, exchange_rest=None):
    depth = len(layers)
    ffn = {f: tuple(f + s for s in ("_w_gate", "_w_up", "_w_down")) for f in ("ffn1", "ffn2")}
    saved = []
    for i, p in enumerate(layers):
        x, s1 = _ffn_fwd(f"L{i}_ffn1", x, p["ffn1_norm_w"], *[p[n] for n in ffn["ffn1"]], p["layer"])
        x, s2 = _mixer_fwd(f"L{i}_mix", x, p, c)
        x, s3 = _ffn_fwd(f"L{i}_ffn2", x, p["ffn2_norm_w"], *[p[n] for n in ffn["ffn2"]], p["layer"])
        saved.append((s1, s2, s3))
    dx, loss_blk = _loss("loss", x, tgt)

    def new_bufs(nl):
        out = {}
        for n in GRAD_BUFS:
            a, b = layers[0][n].shape[-2:]
            out[n] = jnp.zeros((nl, NCH * a, b) if n in ("w_ssm_proj", "w_out") else (nl, NCH, a, b), BF16)
        return out

    def finished(buf):
        out = dict(buf)
        for n in ("w_ssm_proj", "w_out"):
            a, b = layers[0][n].shape[-2:]
            out[n] = buf[n].reshape(-1, NCH, a, b)
        return out

    grads = [None] * depth
    rest_out = [None] * depth
    for i in reversed(range(depth)):
        p = layers[i]
        s1, s2, s3 = saved[i]
        buf, gl = new_bufs(1), 0
        p = dict(p, global_layer=gl)
        if i + 2 < depth and exchange_rest is not None:
            dx = dx + _zero_after(rest_out[i + 2][1][0][0, 0, 0, 0])

        def ffn_bwd(f, s, dy):
            names = ffn[f]
            d, dn, new = _ffn_bwd(f"L{i}_{f}", s, p[f + "_norm_w"], *[p[n] for n in names], p["layer"], dy,
                                  tuple(buf[n] for n in names), gl)
            buf.update(zip(names, new))
            return d, dn

        dx, dn2 = ffn_bwd("ffn2", s3, dx)
        dx, gr = _mixer_bwd(f"L{i}_mix", s2, p, c, dx, (buf["w_att_proj"], buf["w_ssm_proj"], buf["w_out"]))
        buf["w_att_proj"], buf["w_ssm_proj"], buf["w_out"] = gr.pop("bufs")
        if i == 0 and depth > 1 and exchange_rest is not None:
            rest_out[0], ready = exchange_rest(0, finished(buf), [gr], rest_out[1][1][0][0, 0, 0, 0])
            dx = dx + _zero_after(ready)
        dx, dn1 = ffn_bwd("ffn1", s1, dx)
        gr.update(ffn1_norm_w=dn1, ffn2_norm_w=dn2)
        grads[i] = gr
        if i >= 1 and exchange_rest is not None:
            rest_out[i], ready = exchange_rest(i, finished(buf), grads[i:i + 1], None)
            dx = dx + _zero_after(ready)
    return loss_blk, dx, grads, finished(buf), rest_out


WEIGHTS = ["ffn1_norm_w", "ffn1_w_gate", "ffn1_w_up", "ffn1_w_down", "mix_norm_w", "w_in", "b_gates", "q_norm_w",
           "k_norm_w", "conv_w", "conv_b", "dt_bias", "a_log", "d_skip", "ssm_norm_w", "w_att_proj", "w_ssm_proj",
           "w_out", "ffn2_norm_w", "ffn2_w_gate", "ffn2_w_up", "ffn2_w_down"]
SHARD_AXIS = {"ffn1_w_gate": 2, "ffn1_w_up": 2, "ffn1_w_down": 1, "w_in": 2, "conv_w": 2, "w_att_proj": 2,
              "w_ssm_proj": 1, "w_out": 1, "ffn2_w_gate": 2, "ffn2_w_up": 2, "ffn2_w_down": 1}
BIG = [n for n in WEIGHTS if n in SHARD_AXIS]
SMALL = [n for n in WEIGHTS if n not in SHARD_AXIS]
def _from_flat(flat, shapes):
    v = flat.reshape(-1)
    out, off = [], 0
    for s in shapes:
        n = math.prod(s)
        out.append(v[off:off + n].reshape(s))
        off += n
    return out


def _pack_small(parts):
    v = jnp.concatenate([p.astype(F32).reshape(-1) for p in parts])
    rows = -(-v.shape[0] // (8 * LANES)) * 8
    return jnp.pad(v, (0, rows * LANES - v.shape[0])).reshape(rows, LANES)


def kernel(x, ffn1_norm_w, ffn1_w_gate, ffn1_w_up, ffn1_w_down, mix_norm_w, w_in, b_gates, q_norm_w, k_norm_w, conv_w, conv_b, dt_bias, a_log, d_skip, ssm_norm_w, w_att_proj, w_ssm_proj, w_out, ffn2_norm_w, ffn2_w_gate, ffn2_w_up, ffn2_w_down, loss_target, m_ffn1_norm_w, m_ffn1_w_gate, m_ffn1_w_up, m_ffn1_w_down, m_mix_norm_w, m_w_in, m_b_gates, m_q_norm_w, m_k_norm_w, m_conv_w, m_conv_b, m_dt_bias, m_a_log, m_d_skip, m_ssm_norm_w, m_w_att_proj, m_w_ssm_proj, m_w_out, m_ffn2_norm_w, m_ffn2_w_gate, m_ffn2_w_up, m_ffn2_w_down, v_ffn1_norm_w, v_ffn1_w_gate, v_ffn1_w_up, v_ffn1_w_down, v_mix_norm_w, v_w_in, v_b_gates, v_q_norm_w, v_k_norm_w, v_conv_w, v_conv_b, v_dt_bias, v_a_log, v_d_skip, v_ssm_norm_w, v_w_att_proj, v_w_ssm_proj, v_w_out, v_ffn2_norm_w, v_ffn2_w_gate, v_ffn2_w_up, v_ffn2_w_down):
    w = dict(zip(WEIGHTS, (ffn1_norm_w, ffn1_w_gate, ffn1_w_up, ffn1_w_down, mix_norm_w, w_in, b_gates, q_norm_w, k_norm_w, conv_w, conv_b, dt_bias, a_log, d_skip, ssm_norm_w, w_att_proj, w_ssm_proj, w_out, ffn2_norm_w, ffn2_w_gate, ffn2_w_up, ffn2_w_down)))
    m = dict(zip(WEIGHTS, (m_ffn1_norm_w, m_ffn1_w_gate, m_ffn1_w_up, m_ffn1_w_down, m_mix_norm_w, m_w_in, m_b_gates, m_q_norm_w, m_k_norm_w, m_conv_w, m_conv_b, m_dt_bias, m_a_log, m_d_skip, m_ssm_norm_w, m_w_att_proj, m_w_ssm_proj, m_w_out, m_ffn2_norm_w, m_ffn2_w_gate, m_ffn2_w_up, m_ffn2_w_down)))
    v = dict(zip(WEIGHTS, (v_ffn1_norm_w, v_ffn1_w_gate, v_ffn1_w_up, v_ffn1_w_down, v_mix_norm_w, v_w_in, v_b_gates, v_q_norm_w, v_k_norm_w, v_conv_w, v_conv_b, v_dt_bias, v_a_log, v_d_skip, v_ssm_norm_w, v_w_att_proj, v_w_ssm_proj, v_w_out, v_ffn2_norm_w, v_ffn2_w_gate, v_ffn2_w_up, v_ffn2_w_down)))
    depth = ffn1_norm_w.shape[0]
    bl = x.shape[0]
    t = bl * SEQ
    mx, my, mc = lax.axis_index("x"), lax.axis_index("y"), lax.axis_index("c")
    c_idx = mc.astype(jnp.int32).reshape(1)
    chip_idx = (2 * mx + my).astype(jnp.int32).reshape(1)

    cw_width = conv_w.shape[2]
    slots = lax.dynamic_update_slice(jnp.zeros((NCH, *conv_w.shape), F32), jnp.where(mc == 0, conv_w, 0.0)[None],
                                     (chip_idx[0], 0, 0, 0))
    conv_all = _all_sum_small("conv_gather", slots.reshape(-1, LANES)).reshape(NCH, *conv_w.shape)
    big = {"conv_w": jnp.concatenate([conv_all[k] for k in range(NCH)], axis=2)}

    mm_names = [n for n in BIG if n != "conv_w"]
    head = [n for n in mm_names if n.startswith("ffn1")]
    tail = [n for n in mm_names if n not in head]
    got_head = _all_gather_weights([w[n][:1].astype(BF16) for n in head], chip_idx)
    zero = _zero_after(got_head[0][0, 0, 0, 0])
    got_tail = _all_gather_weights_beside("all_gather_weights_beside_L0", [(w[n][:1] + zero).astype(BF16) for n in tail],
                                          chip_idx)
    first = dict(zip(head + tail, list(got_head) + list(got_tail)))
    if depth > 1:
        zero = _zero_after(got_tail[0][0, 0, 0, 0])
        rest = dict(zip(mm_names, _all_gather_weights_beside(
            "all_gather_weights_beside_rest", [(w[n][1:] + zero).astype(BF16) for n in mm_names], chip_idx)))
    for n in mm_names:
        big[n] = [(first[n], 0)] + [(rest[n], i - 1) for i in range(1, depth)]
    small = {n: w[n] for n in SMALL}

    def exchange(tag, names, buf, layer_grads, beside, after=None):
        buf = dict(buf)
        if "w_in" in names:
            buf["w_in"] = jnp.stack([_w_in_from_kernel(g["w_in"]).reshape(D_MODEL, NCH, -1).transpose(1, 0, 2)
                                     for g in layer_grads])
        mine = [buf[n] for n in names]
        if after is not None:
            mine[0] = mine[0] + _zero_after(after).astype(BF16)
        from_sibling = _pair_exchange("grad_pair_exchange_" + tag, mine)
        pairs = [_pair_sum(f"grad_pair_sum_{tag}_{n}", g, r, c_idx) for n, g, r in zip(names, mine, from_sibling)]
        recv = _chip_exchange_beside("grad_chip_exchange_beside_" + tag, pairs) if beside else _chip_exchange(pairs)
        return dict(zip(names, zip(pairs, recv)))

    c = _constants()
    layers = [_layer_params(big, small, i) for i in range(depth)]

    def exchange_rest(i, buf, layer_grads, after):
        names = mm_names if i >= 1 else tail
        out = exchange(f"L{i}", names, buf, layer_grads, True, after)
        return (None, [out[names[0]][1]], out), sum(out[n][0][0, 0, 0, 0] for n in names)

    loss_blk, dx, grads, buf0, rest_out = _local_step(
        x.reshape(t, D_MODEL), loss_target.reshape(t, D_MODEL), layers, c, exchange_rest if depth > 1 else None)
    grad_x = dx.reshape(bl, SEQ, D_MODEL)
    if depth > 1:
        last = exchange("L0_ffn1", head, buf0, [], False, rest_out[0][1][0][0, 0, 0, 0])
        layer0 = {**rest_out[0][2], **last}
    else:
        layer0 = exchange("L0", mm_names, buf0, grads[:1], False)
    halves = []
    for n in mm_names:
        acc = None
        for i in range(depth - 1, 0, -1):
            acc = _chip_sum(f"grad_chip_sum_L{i}_{n}", *rest_out[i][2][n], chip_idx, c_idx, depth, i, into=acc)
        halves.append(_chip_sum("grad_chip_sum_L0_" + n, *layer0[n], chip_idx, c_idx, depth, 0, into=acc))
    g_big = {n: r.reshape(w[n].shape) for n, r in zip(mm_names, _pair_share(halves))}

    def small_grad(n):
        if n == "q_norm_w":
            return jnp.stack([g["qkw"][0, 0, :64] for g in grads])
        if n == "k_norm_w":
            return jnp.stack([g["qkw"][1, 0, :64] for g in grads])
        return jnp.stack([g[n][0, :w[n].shape[1]] for g in grads])

    small_shapes = [w[n].shape for n in SMALL]
    conv_shape = (depth, conv_w.shape[1], NCH * cw_width)
    tot = _all_sum_small("small_all_sum", _pack_small(
        [small_grad(n) for n in SMALL] + [jnp.stack([g["conv_w"] for g in grads]), loss_blk[0, :1]]))
    unpacked = _from_flat(tot, small_shapes + [conv_shape, (1,)])
    g_small = dict(zip(SMALL, unpacked[:-2]))
    g_big["conv_w"] = lax.dynamic_slice_in_dim(unpacked[-2], chip_idx[0] * cw_width, cw_width, axis=2)
    loss = unpacked[-1][0]

    grad, delta, new_m, new_v = {}, {}, {}, {}
    for n in BIG:
        shp = w[n].shape
        two_d = (shp[0] * shp[1], shp[2])
        d_, m_, v_ = _adamw("adamw_" + n, w[n].reshape(two_d), g_big[n].reshape(two_d), m[n].reshape(two_d), v[n].reshape(two_d))
        grad[n], delta[n], new_m[n], new_v[n] = g_big[n], d_.reshape(shp), m_.reshape(shp), v_.reshape(shp)
    d_, m_, v_ = _adamw("adamw_small", _pack_small([w[n] for n in SMALL]), _pack_small([g_small[n] for n in SMALL]),
                        _pack_small([m[n] for n in SMALL]), _pack_small([v[n] for n in SMALL]))
    for n, a, b, c_ in zip(SMALL, _from_flat(d_, small_shapes), _from_flat(m_, small_shapes), _from_flat(v_, small_shapes)):
        grad[n], delta[n], new_m[n], new_v[n] = g_small[n], a, b, c_
    return (loss, grad_x, *[grad[n] for n in WEIGHTS], *[delta[n] for n in WEIGHTS],
            *[new_m[n] for n in WEIGHTS], *[new_v[n] for n in WEIGHTS])
```

```python
import functools
import math

import numpy as np
import jax
import jax.numpy as jnp
from jax import lax
from jax.experimental import pallas as pl
from jax.experimental.pallas import tpu as pltpu
from jax.experimental.pallas import tpu_sc as plsc

F32 = jnp.float32
BF16 = jnp.bfloat16
MESH = pl.DeviceIdType.MESH

D_MODEL = 1024
SEQ = 2048
DEPTH = 4
D_FF = 2816
ATT_DILATIONS = (1, 4, 16)
BAND = 128
ATT_OUT = 512
QKV = 1536
D_INNER = 2048
N_SSM_HEADS = 32
N_SSM_GROUPS = 4
D_STATE = 128
XBC = 3072
CHUNK = 128
N_IN = 11808
EPS = 1e-6
ROPE_THETA = 10000.0
NP = 12288
Q0, K0, V0, Z0, G0, X0, DT0 = 0, 1536, 3072, 4608, 6656, 8704, 11776
DTW = 128
LR, B1, B2, ADAM_EPS, WD, STEP = 0.001, 0.9, 0.999, 1e-08, 0.01, 10

LANES = 128
VMEM_LIMIT = 48 * 1024 * 1024
NEG = -1e30


def _cp(sem=None, **kw):
    return pltpu.CompilerParams(dimension_semantics=sem, vmem_limit_bytes=VMEM_LIMIT, **kw)


def _dg(a, b, ca, cb):
    return lax.dot_general(a.astype(BF16), b.astype(BF16), (((ca,), (cb,)), ((), ())), preferred_element_type=F32)


@jax.custom_vjp
def dot_nn(a, b):
    return _dg(a, b, 1, 0)


def _dot_nn_fwd(a, b):
    return _dg(a, b, 1, 0), (a, b)


def _dot_nn_bwd(r, g):
    a, b = r
    return _dg(g, b, 1, 1).astype(a.dtype), _dg(a, g, 0, 0).astype(b.dtype)


dot_nn.defvjp(_dot_nn_fwd, _dot_nn_bwd)


@jax.custom_vjp
def dot_nt(a, b):
    return _dg(a, b, 1, 1)


def _dot_nt_fwd(a, b):
    return _dg(a, b, 1, 1), (a, b)


def _dot_nt_bwd(r, g):
    a, b = r
    return _dg(g, b, 1, 0).astype(a.dtype), _dg(g, a, 0, 0).astype(b.dtype)


dot_nt.defvjp(_dot_nt_fwd, _dot_nt_bwd)


@jax.custom_vjp
def dot_tn(a, b):
    return _dg(a, b, 0, 0)


def _dot_tn_fwd(a, b):
    return _dg(a, b, 0, 0), (a, b)


def _dot_tn_bwd(r, g):
    a, b = r
    return _dg(b, g, 1, 1).astype(a.dtype), _dg(a, g, 1, 0).astype(b.dtype)


dot_tn.defvjp(_dot_tn_fwd, _dot_tn_bwd)


def _dot2_raw(a, e, ce):
    hi = a.astype(BF16)
    lo = (a - hi.astype(F32)).astype(BF16)
    return _dg(hi, e, 1, ce) + _dg(lo, e, 1, ce)


@jax.custom_vjp
def dot2(a, e):
    return _dot2_raw(a, e, 0)


def _dot2_fwd(a, e):
    return _dot2_raw(a, e, 0), e


def _dot2_bwd(e, g):
    return _dot2_raw(g, e, 1), jnp.zeros_like(e)


dot2.defvjp(_dot2_fwd, _dot2_bwd)


def _tri2_raw(l, x, cl):
    hi = x.astype(BF16)
    lo = (x - hi.astype(F32)).astype(BF16)
    return _dg(l, hi, cl, 0) + _dg(l, lo, cl, 0)


@jax.custom_vjp
def tri_matmul(l, x):
    return _tri2_raw(l, x, 1)


def _tri_fwd(l, x):
    return _tri2_raw(l, x, 1), l


def _tri_bwd(l, g):
    return jnp.zeros_like(l), _tri2_raw(l, g, 0)


tri_matmul.defvjp(_tri_fwd, _tri_bwd)


def _dup64_raw(w):
    return w + pltpu.roll(w, 64, 1)


@jax.custom_vjp
def dup64(w):
    return _dup64_raw(w)


def _dup64_fwd(w):
    return _dup64_raw(w), None


def _dup64_bwd(_, g):
    lane = lax.broadcasted_iota(jnp.int32, g.shape, 1)
    return (jnp.where(lane < 64, _dup64_raw(g), 0.0),)


dup64.defvjp(_dup64_fwd, _dup64_bwd)


def _rope_rot_raw(y, sign):
    lane = lax.broadcasted_iota(jnp.int32, y.shape, 1)
    first_half = (lane & 32) == 0
    return sign * jnp.where(first_half, -pltpu.roll(y, LANES - 32, 1), pltpu.roll(y, 32, 1))


@jax.custom_vjp
def rope_rot(y):
    return _rope_rot_raw(y, 1.0)


def _rope_rot_fwd(y):
    return _rope_rot_raw(y, 1.0), None


def _rope_rot_bwd(_, g):
    return (_rope_rot_raw(g, -1.0),)


rope_rot.defvjp(_rope_rot_fwd, _rope_rot_bwd)


def _shift_rows_raw(x, s):
    n = x.shape[0]
    r = pltpu.roll(x, s % n, 0)
    rows = lax.broadcasted_iota(jnp.int32, x.shape, 0)
    keep = rows >= s if s > 0 else rows < n + s
    return jnp.where(keep, r, 0.0)


@functools.partial(jax.custom_vjp, nondiff_argnums=(1,))
def shift_rows(x, s):
    return _shift_rows_raw(x, s)


def _shift_fwd(x, s):
    return _shift_rows_raw(x, s), None


def _shift_bwd(s, _, g):
    return (_shift_rows_raw(g, -s),)


shift_rows.defvjp(_shift_fwd, _shift_bwd)


def _sigmoid(x):
    return 1.0 / (1.0 + jnp.exp(-x))


def _silu(x):
    return x * _sigmoid(x)


def _softplus(x):
    return jnp.maximum(x, 0.0) + jnp.log(1.0 + jnp.exp(-jnp.abs(x)))


def _head_mean_mat():
    i = np.arange(LANES)
    return jnp.asarray((i[:, None] // 64 == i[None, :] // 64).astype(np.float32) / 64.0)


def _head_expand_mat():
    e = np.zeros((LANES, D_INNER), np.float32)
    for l in range(D_INNER):
        e[l // 64, l] = 1.0
    return jnp.asarray(e)


def _ltri_mat():
    i = np.arange(CHUNK)
    return jnp.asarray((i[:, None] >= i[None, :]).astype(np.float32))


def _rope_tables():
    pos = jnp.arange(SEQ, dtype=F32)
    inv_freq = 1.0 / (ROPE_THETA ** (jnp.arange(0, 64, 2, dtype=F32) / 64))
    ang = pos[:, None] * inv_freq[None, :]
    return jnp.tile(jnp.cos(ang), (1, 4)), jnp.tile(jnp.sin(ang), (1, 4))


def _pick(n, cap):
    best = None
    for t in range(LANES, min(n, cap) + 1, LANES):
        if n % t == 0:
            best = t
    return best if best is not None else n


def _mm(name, a, b, mode, out_dtype=F32, alpha=None, res=None):
    if mode == "nn":
        (m, k), n = a.shape, b.shape[1]
    elif mode == "nt":
        (m, k), n = a.shape, b.shape[0]
    else:
        (k, m), n = a.shape, b.shape[1]
    tm, tn, tk = _pick(m, 1408), _pick(n, 1408), _pick(k, 1024)
    nk = k // tk
    ca, cb = {"nn": (1, 0), "nt": (1, 1), "tn": (0, 0)}[mode]
    a_spec = pl.BlockSpec((tk, tm), lambda i, j, kk: (kk, i)) if mode == "tn" else pl.BlockSpec((tm, tk), lambda i, j, kk: (i, kk))
    b_spec = pl.BlockSpec((tn, tk), lambda i, j, kk: (j, kk)) if mode == "nt" else pl.BlockSpec((tk, tn), lambda i, j, kk: (kk, j))
    o_spec = pl.BlockSpec((tm, tn), lambda i, j, kk: (i, j))
    has_res = res is not None

    def finish(acc, res_ref, o_ref):
        if alpha is not None:
            acc = acc * alpha
        if has_res:
            acc = acc + res_ref[...].astype(F32)
        o_ref[...] = acc.astype(o_ref.dtype)

    def body(*refs):
        a_ref, b_ref = refs[0], refs[1]
        res_ref = refs[2] if has_res else None
        o_ref = refs[3] if has_res else refs[2]
        part = _dg(a_ref[...], b_ref[...], ca, cb)
        if nk == 1:
            finish(part, res_ref, o_ref)
            return
        acc_ref = refs[-1]
        kk = pl.program_id(2)

        @pl.when(kk == 0)
        def _():
            acc_ref[...] = part

        @pl.when(kk > 0)
        def _():
            acc_ref[...] += part

        @pl.when(kk == nk - 1)
        def _():
            finish(acc_ref[...], res_ref, o_ref)

    ins = [a, b] + ([res] if has_res else [])
    in_specs = [a_spec, b_spec] + ([o_spec] if has_res else [])
    return pl.pallas_call(
        body, name=name, grid=(m // tm, n // tn, nk), in_specs=in_specs, out_specs=o_spec,
        out_shape=jax.ShapeDtypeStruct((m, n), out_dtype),
        scratch_shapes=[pltpu.VMEM((tm, tn), F32)] if nk > 1 else [],
        compiler_params=_cp(("parallel", "parallel", "arbitrary")),
    )(*ins)


def _mmx(name, grid, a, b, out, contract, *, alpha=None, res=None, into=None):
    nk = grid[-1]
    has_res, has_into = res is not None, into is not None
    n_in = 2 + has_res + has_into

    def finish(acc, res_ref, o_ref):
        if alpha is not None:
            acc = acc * alpha
        if has_res:
            acc = acc + res_ref[...].astype(F32)
        o_ref[...] = acc.astype(o_ref.dtype)

    def body(*refs):
        res_ref = refs[2] if has_res else None
        o_ref = refs[n_in]
        part = _dg(refs[0][...], refs[1][...], *contract)
        if nk == 1:
            finish(part, res_ref, o_ref)
            return
        acc_ref = refs[-1]
        kk = pl.program_id(len(grid) - 1)

        @pl.when(kk == 0)
        def _():
            acc_ref[...] = part

        @pl.when(kk > 0)
        def _():
            acc_ref[...] += part

        @pl.when(kk == nk - 1)
        def _():
            finish(acc_ref[...], res_ref, o_ref)

    operands = [a, b] + ([res] if has_res else [])
    in_specs = [pl.BlockSpec(blk, im) for _, blk, im in operands] + ([ANY] if has_into else [])
    acc_shape = tuple(d for d in out[2] if d is not None)
    return pl.pallas_call(
        body, name=name, grid=grid, in_specs=in_specs, out_specs=pl.BlockSpec(out[2], out[3]),
        out_shape=jax.ShapeDtypeStruct(out[0], out[1]),
        scratch_shapes=[pltpu.VMEM(acc_shape, F32)] if nk > 1 else [],
        input_output_aliases={n_in - 1: 0} if has_into else {},
        compiler_params=_cp(("parallel",) * (len(grid) - 1) + ("arbitrary",)),
    )(*[o[0] for o in operands], *([into] if has_into else []))


def _ew(name, fn, grid, ins, outs, scratch=()):
    n_in, n_out = len(ins), len(outs)

    def body(*refs):
        vals = [r[...] for r in refs[:n_in]]
        res = fn(*vals, *refs[n_in + n_out:])
        for r, v in zip(refs[n_in:n_in + n_out], res):
            r[...] = v.astype(r.dtype)

    res = pl.pallas_call(
        body, name=name, grid=grid,
        in_specs=[pl.BlockSpec(b, m) for _, b, m in ins],
        out_specs=[pl.BlockSpec(b, m) for _, _, b, m in outs],
        out_shape=[jax.ShapeDtypeStruct(s, d) for s, d, _, _ in outs],
        scratch_shapes=list(scratch),
        compiler_params=_cp(("arbitrary",) * len(grid)),
    )(*[a for a, _, _ in ins])
    return res


def _ew_bwd(name, fn, grid, ins, cts, wrt, adds=(), ct_fn=None):
    n_in, n_ct, n_add = len(ins), len(cts), len(adds)
    idxs = [w["idx"] for w in wrt]
    intos = [(k, w["into"]) for k, w in enumerate(wrt) if w.get("into") is not None]

    def body(*refs):
        prim = [r[...] for r in refs[:n_in]]
        ct = [r[...].astype(F32) for r in refs[n_in:n_in + n_ct]]
        addv = [r[...] for r in refs[n_in + n_ct:n_in + n_ct + n_add]]
        orefs = refs[n_in + n_ct + n_add + len(intos):]

        def f(*sel):
            full = list(prim)
            for i, s in zip(idxs, sel):
                full[i] = s
            return fn(*full)

        _, vjp = jax.vjp(f, *[prim[i].astype(F32) for i in idxs])
        grads = vjp(tuple(ct) if ct_fn is None else ct_fn(*ct))
        for w, g, r in zip(wrt, grads, orefs):
            if w["kind"] == "tile":
                if w.get("add") is not None:
                    g = g + addv[w["add"]].astype(F32)
                r[...] = g.astype(r.dtype)
            else:
                first = w["first"]()

                @pl.when(first)
                def _(r=r, g=g):
                    r[...] = g.astype(r.dtype)

                @pl.when(jnp.logical_not(first))
                def _(r=r, g=g):
                    r[...] += g.astype(r.dtype)

    allin = list(ins) + list(cts) + list(adds)
    return pl.pallas_call(
        body, name=name, grid=grid,
        in_specs=[pl.BlockSpec(b, m) for _, b, m in allin] + [ANY] * len(intos),
        out_specs=[pl.BlockSpec(w["block"], w["imap"]) for w in wrt],
        out_shape=[jax.ShapeDtypeStruct(w["shape"], w["dtype"]) for w in wrt],
        input_output_aliases={len(allin) + q: k for q, (k, _) in enumerate(intos)},
        compiler_params=_cp(("arbitrary",) * len(grid)),
    )(*[a for a, _, _ in allin], *[a for _, a in intos])


def _rmsnorm_fn(x, w):
    return (x * lax.rsqrt(jnp.mean(x * x, axis=-1, keepdims=True) + EPS) * w,)


def _swiglu_fn(g, u):
    return (_silu(g) * u,)


def _qkprep_fn(t, w64, cos, sin, hmean):
    w = jnp.sum(dup64(jnp.broadcast_to(w64, (8, LANES))), axis=0, keepdims=True) * 0.125
    y = t * lax.rsqrt(dot2(t * t, hmean) + EPS) * w
    return (y * cos + rope_rot(y) * sin,)


def _att_fn(q, kp, kc, vp, vc, first):
    iq = lax.broadcasted_iota(jnp.int32, (BAND, 2 * BAND), 0)
    ik = lax.broadcasted_iota(jnp.int32, (BAND, 2 * BAND), 1)
    rel = BAND + iq - ik
    ok = (rel >= 0) & (rel <= BAND) & ((ik >= BAND) | jnp.logical_not(first))
    lane = lax.broadcasted_iota(jnp.int32, (1, LANES), 1)
    kcat = jnp.concatenate([kp, kc], axis=0)
    vcat = jnp.concatenate([vp, vc], axis=0)
    o_pair = jnp.zeros((BAND, LANES), F32)
    l_pair = jnp.zeros((BAND, LANES), F32)
    for hh in range(2):
        lm = (lane // 64 == hh).astype(F32)
        s = dot_nt(q * lm, kcat) * 0.125
        s = jnp.where(ok, s, NEG)
        mx = jnp.max(s, axis=-1, keepdims=True)
        e = jnp.exp(s - mx)
        den = jnp.sum(e, axis=-1, keepdims=True)
        o_pair = o_pair + dot_nn(e / den, vcat) * lm
        l_pair = l_pair + (mx + jnp.log(den)) * lm
    return o_pair, l_pair


def _attmix_fn(o0, o1, o2, l0, l1, l2):
    m = jnp.maximum(jnp.maximum(l0, l1), l2)
    e0, e1, e2 = jnp.exp(l0 - m), jnp.exp(l1 - m), jnp.exp(l2 - m)
    return ((e0 * o0 + e1 * o1 + e2 * o2) / (e0 + e1 + e2),)


def _conv_fn(x, w0, w1, w2, w3, b):
    pre = x * w3 + shift_rows(x, 1) * w2 + shift_rows(x, 2) * w1 + shift_rows(x, 3) * w0 + b
    return (_silu(pre),)


def _ssdpre_fn(dtraw, bias, alog, ex):
    dt = _softplus(dtraw + bias)
    da = dt * (-jnp.exp(alog))
    return dot2(dt, ex), dot2(da, ex)


def _ssd_step(st, x, dtb, dab, bm, cm, ltri):
    cum = tri_matmul(ltri, dab)
    cum_t = cum.T
    xdt = x * dtb
    cb = dot_nt(cm, bm)
    ri = lax.broadcasted_iota(jnp.int32, (CHUNK, CHUNK), 0)
    ci = lax.broadcasted_iota(jnp.int32, (CHUNK, CHUNK), 1)
    causal = ri >= ci
    lane = lax.broadcasted_iota(jnp.int32, (1, LANES), 1)
    rowi = lax.broadcasted_iota(jnp.int32, (LANES, 1), 0)
    ys = []
    for p in range(4):
        sl = slice(p * LANES, (p + 1) * LANES)
        cum_p, cum_tp, xdt_p = cum[:, sl], cum_t[sl, :], xdt[:, sl]
        acc = jnp.zeros((CHUNK, LANES), F32)
        for hh in range(2):
            col = jnp.sum(cum_p * (lane == 64 * hh).astype(F32), axis=1, keepdims=True)
            row = jnp.sum(cum_tp * (rowi == 64 * hh).astype(F32), axis=0, keepdims=True)
            dec = jnp.exp(jnp.where(causal, col - row, NEG))
            acc = acc + dot_nn(cb * dec, xdt_p * (lane // 64 == hh).astype(F32))
        ys.append(acc)
    y_diag = jnp.concatenate(ys, axis=1)
    y_off = dot_nn(cm, st) * jnp.exp(cum)
    last_row = (lax.broadcasted_iota(jnp.int32, (CHUNK, 1), 0) == CHUNK - 1).astype(F32)
    last = jnp.sum(cum * last_row, axis=0, keepdims=True)
    new_st = st * jnp.exp(last) + dot_tn(bm, xdt * jnp.exp(last - cum))
    return new_st, y_diag + y_off


def _ssdpost_fn(y, xs, z, dskip, ex, nw):
    db = jnp.sum(dot2(jnp.broadcast_to(dskip, (8, LANES)), ex), axis=0, keepdims=True) * 0.125
    y2 = (y + db * xs) * _silu(z)
    return (y2 * lax.rsqrt(jnp.mean(y2 * y2, axis=-1, keepdims=True) + EPS) * nw,)


def _merge_fn(ya, ys, ga, gs, ba, bs):
    return (_sigmoid(ga + ba) * ya + _sigmoid(gs + bs) * ys,)


TM = 512


def _full(shape):
    nd = len(shape)
    return (shape, lambda *_: (0,) * nd)


def _rmsnorm(name, x, w):
    t = x.shape[0]
    return _ew(name, _rmsnorm_fn, (t // TM,),
               [(x, (TM, D_MODEL), lambda i: (i, 0)), (w, (1, D_MODEL), lambda i: (0, 0))],
               [((t, D_MODEL), BF16, (TM, D_MODEL), lambda i: (i, 0))])[0]


def _rmsnorm_bwd(name, x, w, dh, dres):
    t = x.shape[0]
    row = ((TM, D_MODEL), lambda i: (i, 0))
    return _ew_bwd(name, _rmsnorm_fn, (t // TM,),
                   [(x, *row), (w, (1, D_MODEL), lambda i: (0, 0))], [(dh, *row)],
                   [dict(idx=0, kind="tile", shape=(t, D_MODEL), dtype=F32, block=row[0], imap=row[1], add=0),
                    dict(idx=1, kind="acc", shape=(1, D_MODEL), dtype=F32, block=(1, D_MODEL), imap=lambda i: (0, 0),
                         first=lambda: pl.program_id(0) == 0)],
                   adds=[(dres, *row)])


def _qk_operands(proj, qkw, cos, sin, consts, tm):
    nrow = SEQ // tm
    c = ((LANES, LANES), lambda j, i: (0, 0))
    return [(proj, (tm, LANES), lambda j, i: (i, j)),
            (qkw, (None, 1, LANES), lambda j, i: (j // 12, 0, 0)),
            (cos, (tm, LANES), lambda j, i: (i % nrow, 0)),
            (sin, (tm, LANES), lambda j, i: (i % nrow, 0)),
            (consts["hmean"], *c)]


QK_TM = 1024


def _qkprep(name, proj, qkw, cos, sin, consts):
    t = proj.shape[0]
    return _ew(name, _qkprep_fn, (2 * QKV // LANES, t // QK_TM), _qk_operands(proj, qkw, cos, sin, consts, QK_TM),
               [((t, 2 * QKV), F32, (QK_TM, LANES), lambda j, i: (i, j))])[0]


def _qkprep_bwd(name, proj, qkw, cos, sin, consts, dq, dk, dproj):
    t = proj.shape[0]
    nq = QKV // LANES

    def pick(cq, ck):
        return (jnp.where(pl.program_id(0) < nq, cq, ck),)

    return _ew_bwd(name, _qkprep_fn, (2 * nq, t // QK_TM), _qk_operands(proj, qkw, cos, sin, consts, QK_TM),
                   [(d, (QK_TM, LANES), lambda j, i: (i, j % nq)) for d in (dq, dk)],
                   [dict(idx=0, kind="tile", shape=dproj.shape, dtype=dproj.dtype, block=(QK_TM, LANES),
                         imap=lambda j, i: (i, j), into=dproj),
                    dict(idx=1, kind="acc", shape=(2, 1, LANES), dtype=F32, block=(None, 1, LANES),
                         imap=lambda j, i: (j // 12, 0, 0),
                         first=lambda: (pl.program_id(0) % 12 == 0) & (pl.program_id(1) == 0))],
                   ct_fn=pick)


def _att_specs(dil, g):
    nb = SEQ // dil // BAND
    pt = 4 if dil == 1 else 1
    w = pt * LANES
    blk = (None, BAND * dil, w)
    kq, kk, kv = g * ATT_OUT // w, (QKV + g * ATT_OUT) // w, (V0 + g * ATT_OUT) // w

    def cur(n):
        return jnp.minimum(n, nb - 1)

    def prev(n):
        return jnp.maximum(jnp.minimum(n, nb - 1) - 1, 0)

    return nb, pt, blk, [
        pl.BlockSpec(blk, lambda b, p, n: (b, cur(n), kq + p)),
        pl.BlockSpec(blk, lambda b, p, n: (b, prev(n), kk + p)),
        pl.BlockSpec(blk, lambda b, p, n: (b, cur(n), kk + p)),
        pl.BlockSpec(blk, lambda b, p, n: (b, prev(n), kv + p)),
        pl.BlockSpec(blk, lambda b, p, n: (b, cur(n), kv + p)),
    ]


def _att_rows(r, dil):
    return pl.ds(r, BAND, stride=dil) if dil > 1 else pl.ds(0, BAND)


def _att_fwd(name, qk, proj, g):
    bl = qk.shape[0] // SEQ
    dil = ATT_DILATIONS[g]
    nb, pt, blk, specs = _att_specs(dil, g)
    qk3 = qk.reshape(bl, SEQ, 2 * QKV)
    proj3 = proj.reshape(bl, SEQ, NP)
    o_spec = pl.BlockSpec(blk, lambda b, p, n: (b, n, p))

    def body(q, kp, kc, vp, vc, o_ref, l_ref):
        first = pl.program_id(2) == 0

        def residue(r, carry):
            sl = _att_rows(r, dil)
            for p in range(pt):
                ln = pl.ds(p * LANES, LANES)
                o, l = _att_fn(q[sl, ln], kp[sl, ln], kc[sl, ln], vp[sl, ln], vc[sl, ln], first)
                o_ref[sl, ln] = o
                l_ref[sl, ln] = l
            return carry

        lax.fori_loop(0, dil, residue, 0)

    o, l = pl.pallas_call(
        body, name=name, grid=(bl, ATT_OUT // (pt * LANES), nb), in_specs=specs, out_specs=[o_spec, o_spec],
        out_shape=[jax.ShapeDtypeStruct((bl, SEQ, ATT_OUT), F32)] * 2,
        compiler_params=_cp(("arbitrary",) * 3),
    )(qk3, qk3, qk3, proj3, proj3)
    return o.reshape(bl * SEQ, ATT_OUT), l.reshape(bl * SEQ, ATT_OUT)


def _att_bwd(name, qk, proj, g, do, dl, dq_buf, dk_buf, dv_buf):
    bl = qk.shape[0] // SEQ
    dil = ATT_DILATIONS[g]
    nb, pt, blk, specs = _att_specs(dil, g)
    w = pt * LANES
    qk3 = qk.reshape(bl, SEQ, 2 * QKV)
    proj3 = proj.reshape(bl, SEQ, NP)
    ct_spec = pl.BlockSpec(blk, lambda b, p, n: (b, jnp.minimum(n, nb - 1), p))
    do3 = do.reshape(bl, SEQ, ATT_OUT)
    dl3 = dl.reshape(bl, SEQ, ATT_OUT)
    kg = g * ATT_OUT // w

    def body(q, kp, kc, vp, vc, do_ref, dl_ref, _a, _b, _c, d_ref, dk_ref, dv_ref, ck, cv):
        n = pl.program_id(2)

        def residue(r, carry):
            sl = _att_rows(r, dil)
            for p in range(pt):
                ln = pl.ds(p * LANES, LANES)

                @pl.when(n < nb)
                def _(ln=ln):
                    first = n == 0
                    prim = [ref[sl, ln] for ref in (q, kp, kc, vp, vc)]
                    _, vjp = jax.vjp(lambda *a: _att_fn(*a, first), *prim)
                    dq, dkp, dkc, dvp, dvc = vjp((do_ref[sl, ln], dl_ref[sl, ln]))
                    d_ref[sl, ln] = dq

                    @pl.when(n > 0)
                    def _():
                        dk_ref[sl, ln] = ck[sl, ln] + dkp
                        dv_ref[sl, ln] = cv[sl, ln] + dvp

                    ck[sl, ln] = dkc
                    cv[sl, ln] = dvc

                @pl.when(n == nb)
                def _(ln=ln):
                    dk_ref[sl, ln] = ck[sl, ln]
                    dv_ref[sl, ln] = cv[sl, ln]

            return carry

        lax.fori_loop(0, dil, residue, 0)

    bufs = [a.reshape(bl, SEQ, QKV) for a in (dq_buf, dk_buf, dv_buf)]
    o_specs = [
        pl.BlockSpec(blk, lambda b, p, n: (b, jnp.minimum(n, nb - 1), kg + p)),
        pl.BlockSpec(blk, lambda b, p, n: (b, jnp.maximum(n - 1, 0), kg + p)),
        pl.BlockSpec(blk, lambda b, p, n: (b, jnp.maximum(n - 1, 0), kg + p)),
    ]
    dq, dk, dv = pl.pallas_call(
        body, name=name, grid=(bl, ATT_OUT // w, nb + 1), in_specs=specs + [ct_spec, ct_spec, ANY, ANY, ANY],
        out_specs=o_specs, out_shape=[jax.ShapeDtypeStruct(a.shape, a.dtype) for a in bufs],
        input_output_aliases={7: 0, 8: 1, 9: 2},
        scratch_shapes=[pltpu.VMEM((BAND * dil, w), F32), pltpu.VMEM((BAND * dil, w), F32)],
        compiler_params=_cp(("arbitrary",) * 3),
    )(qk3, qk3, qk3, proj3, proj3, do3, dl3, *bufs)
    return dq.reshape(dq_buf.shape), dk.reshape(dk_buf.shape), dv.reshape(dv_buf.shape)


def _attmix(name, os_, ls_):
    t = os_[0].shape[0]
    blk = ((TM, ATT_OUT), lambda i: (i, 0))
    return _ew(name, _attmix_fn, (t // TM,), [(a, *blk) for a in (*os_, *ls_)], [((t, ATT_OUT), BF16, *blk)])[0]


def _attmix_bwd(name, os_, ls_, datt):
    t = os_[0].shape[0]
    blk = ((TM, ATT_OUT), lambda i: (i, 0))
    return _ew_bwd(name, _attmix_fn, (t // TM,), [(a, *blk) for a in (*os_, *ls_)], [(datt, *blk)],
                   [dict(idx=k, kind="tile", shape=(t, ATT_OUT), dtype=F32, block=blk[0], imap=blk[1]) for k in range(6)])


CONV_TC = 256


def _conv_operands(proj3, conv_w, conv_b):
    c0 = X0 // CONV_TC
    ins = [(proj3, (None, SEQ, CONV_TC), lambda j, b: (b, 0, c0 + j))]
    for k in range(4):
        ins.append((conv_w, (None, 1, CONV_TC), lambda j, b, k=k: (k, 0, j)))
    ins.append((conv_b, (1, CONV_TC), lambda j, b: (0, j)))
    return ins


def _conv(name, proj3, conv_w, conv_b):
    bl = proj3.shape[0]
    return _ew(name, _conv_fn, (XBC // CONV_TC, bl), _conv_operands(proj3, conv_w, conv_b),
               [((bl, SEQ, XBC), F32, (None, SEQ, CONV_TC), lambda j, b: (b, 0, j))])[0]


def _conv_bwd(name, proj3, conv_w, conv_b, dxs3, db3, dc3, dproj3):
    bl = proj3.shape[0]
    nx = D_INNER // CONV_TC
    nb_ = N_SSM_GROUPS * D_STATE // CONV_TC
    blk = (None, SEQ, CONV_TC)
    cts = [(dxs3, blk, lambda j, b: (b, 0, jnp.minimum(j, nx - 1))),
           (db3, blk, lambda j, b: (b, 0, jnp.clip(j - nx, 0, nb_ - 1))),
           (dc3, blk, lambda j, b: (b, 0, jnp.clip(j - nx - nb_, 0, nb_ - 1)))]

    def pick(cx, cb, cc):
        j = pl.program_id(0)
        return (jnp.where(j < nx, cx, jnp.where(j < nx + nb_, cb, cc)),)

    first = lambda: pl.program_id(1) == 0
    wrt = [dict(idx=0, kind="tile", shape=dproj3.shape, dtype=dproj3.dtype, block=blk,
                imap=lambda j, b: (b, 0, X0 // CONV_TC + j), into=dproj3)]
    for k in range(4):
        wrt.append(dict(idx=1 + k, kind="acc", shape=(1, XBC), dtype=F32, block=(1, CONV_TC),
                        imap=lambda j, b: (0, j), first=first))
    wrt.append(dict(idx=5, kind="acc", shape=(1, XBC), dtype=F32, block=(1, CONV_TC), imap=lambda j, b: (0, j), first=first))
    return _ew_bwd(name, _conv_fn, (XBC // CONV_TC, bl), _conv_operands(proj3, conv_w, conv_b), cts, wrt, ct_fn=pick)


SSD_TM = 256


def _ssdpre_operands(proj, dt_bias, a_log, ex):
    return [(proj, (SSD_TM, DTW), lambda i: (i, DT0 // DTW)), (dt_bias, *_full((1, DTW))), (a_log, *_full((1, DTW))),
            (ex, *_full((LANES, D_INNER)))]


def _ssdpre(name, proj, dt_bias, a_log, ex):
    t = proj.shape[0]
    blk = ((SSD_TM, D_INNER), lambda i: (i, 0))
    return _ew(name, _ssdpre_fn, (t // SSD_TM,), _ssdpre_operands(proj, dt_bias, a_log, ex),
               [((t, D_INNER), F32, *blk), ((t, D_INNER), F32, *blk)])


def _ssdpre_bwd(name, proj, dt_bias, a_log, ex, ddtb, ddab):
    t = proj.shape[0]
    blk = ((SSD_TM, D_INNER), lambda i: (i, 0))
    first = lambda: pl.program_id(0) == 0
    return _ew_bwd(name, _ssdpre_fn, (t // SSD_TM,), _ssdpre_operands(proj, dt_bias, a_log, ex),
                   [(ddtb, *blk), (ddab, *blk)],
                   [dict(idx=0, kind="tile", shape=(t, DTW), dtype=BF16, block=(SSD_TM, DTW), imap=lambda i: (i, 0)),
                    dict(idx=1, kind="acc", shape=(1, DTW), dtype=F32, block=(1, DTW), imap=lambda i: (0, 0), first=first),
                    dict(idx=2, kind="acc", shape=(1, DTW), dtype=F32, block=(1, DTW), imap=lambda i: (0, 0), first=first)])


def _ssd_in_specs(rev):
    nc = SEQ // CHUNK

    def c_(c):
        return nc - 1 - c if rev else c

    wide = (None, CHUNK, D_INNER)
    nar = (None, CHUNK, N_SSM_GROUPS * D_STATE)
    nb_ = D_INNER // (N_SSM_GROUPS * D_STATE)
    return [
        pl.BlockSpec(wide, lambda b, c: (b, c_(c), 0)),
        pl.BlockSpec(wide, lambda b, c: (b, c_(c), 0)),
        pl.BlockSpec(wide, lambda b, c: (b, c_(c), 0)),
        pl.BlockSpec(nar, lambda b, c: (b, c_(c), nb_)),
        pl.BlockSpec(nar, lambda b, c: (b, c_(c), nb_ + 1)),
        pl.BlockSpec((CHUNK, CHUNK), lambda b, c: (0, 0)),
    ], c_


def _ssd_cols(g):
    return pl.ds(g * 4 * LANES, 4 * LANES), pl.ds(g * D_STATE, D_STATE)


def _ssd_fwd(name, xc3, dtb3, dab3, ltri):
    bl = xc3.shape[0]
    nc = SEQ // CHUNK
    specs, _ = _ssd_in_specs(False)

    def body(x, dtb, dab, bm, cm, lt, y_ref, st_ref, st):
        @pl.when(pl.program_id(1) == 0)
        def _():
            st[...] = jnp.zeros_like(st)

        ltv = lt[...]
        for g in range(N_SSM_GROUPS):
            wl, nl = _ssd_cols(g)
            s0 = st[g]
            st_ref[g] = s0
            new_st, y = _ssd_step(s0, x[:, wl], dtb[:, wl], dab[:, wl], bm[:, nl], cm[:, nl], ltv)
            y_ref[:, wl] = y
            st[g] = new_st

    return pl.pallas_call(
        body, name=name, grid=(bl, nc), in_specs=specs,
        out_specs=[pl.BlockSpec((None, CHUNK, D_INNER), lambda b, c: (b, c, 0)),
                   pl.BlockSpec((None, N_SSM_GROUPS, None, D_STATE, 4 * LANES), lambda b, c: (b, 0, c, 0, 0))],
        out_shape=[jax.ShapeDtypeStruct((bl, SEQ, D_INNER), F32),
                   jax.ShapeDtypeStruct((bl, N_SSM_GROUPS, nc, D_STATE, 4 * LANES), F32)],
        scratch_shapes=[pltpu.VMEM((N_SSM_GROUPS, D_STATE, 4 * LANES), F32)],
        compiler_params=_cp(("arbitrary",) * 2),
    )(xc3, dtb3, dab3, xc3, xc3, ltri)


def _ssd_bwd(name, xc3, dtb3, dab3, ltri, states, dy3, dxs_part3):
    bl = xc3.shape[0]
    nc = SEQ // CHUNK
    specs, c_ = _ssd_in_specs(True)
    wide = pl.BlockSpec((None, CHUNK, D_INNER), lambda b, c: (b, c_(c), 0))
    nar = pl.BlockSpec((None, CHUNK, N_SSM_GROUPS * D_STATE), lambda b, c: (b, c_(c), 0))
    st_spec = pl.BlockSpec((None, N_SSM_GROUPS, None, D_STATE, 4 * LANES), lambda b, c: (b, 0, c_(c), 0, 0))

    def body(x, dtb, dab, bm, cm, lt, st_ref, dy, dxp, dx_ref, ddtb_ref, ddab_ref, dbm_ref, dcm_ref, dst):
        @pl.when(pl.program_id(1) == 0)
        def _():
            dst[...] = jnp.zeros_like(dst)

        ltv = lt[...]
        for g in range(N_SSM_GROUPS):
            wl, nl = _ssd_cols(g)
            _, vjp = jax.vjp(lambda *a: _ssd_step(*a, ltv), st_ref[g], x[:, wl], dtb[:, wl], dab[:, wl], bm[:, nl], cm[:, nl])
            d_st, d_x, d_dtb, d_dab, d_bm, d_cm = vjp((dst[g], dy[:, wl]))
            dst[g] = d_st
            dx_ref[:, wl] = d_x + dxp[:, wl]
            ddtb_ref[:, wl] = d_dtb
            ddab_ref[:, wl] = d_dab
            dbm_ref[:, nl] = d_bm
            dcm_ref[:, nl] = d_cm

    big = jax.ShapeDtypeStruct((bl, SEQ, D_INNER), F32)
    small = jax.ShapeDtypeStruct((bl, SEQ, N_SSM_GROUPS * D_STATE), F32)
    return pl.pallas_call(
        body, name=name, grid=(bl, nc), in_specs=specs + [st_spec, wide, wide],
        out_specs=[wide, wide, wide, nar, nar], out_shape=[big, big, big, small, small],
        scratch_shapes=[pltpu.VMEM((N_SSM_GROUPS, D_STATE, 4 * LANES), F32)],
        compiler_params=_cp(("arbitrary",) * 2),
    )(xc3, dtb3, dab3, xc3, xc3, ltri, states, dy3, dxs_part3)


def _ssdpost_operands(y, xc, proj, d_skip, ex, nw):
    w = 4 * LANES
    return [(y, (SSD_TM, w), lambda j, i: (i, j)), (xc, (SSD_TM, w), lambda j, i: (i, j)),
            (proj, (SSD_TM, w), lambda j, i: (i, Z0 // w + j)), (d_skip, (1, DTW), lambda j, i: (0, 0)),
            (ex, (LANES, w), lambda j, i: (0, j)), (nw, (1, w), lambda j, i: (0, j))]


def _ssdpost(name, y, xc, proj, d_skip, ex, nw):
    t = y.shape[0]
    w = 4 * LANES
    return _ew(name, _ssdpost_fn, (D_INNER // w, t // SSD_TM), _ssdpost_operands(y, xc, proj, d_skip, ex, nw),
               [((t, D_INNER), BF16, (SSD_TM, w), lambda j, i: (i, j))])[0]


def _ssdpost_bwd(name, y, xc, proj, d_skip, ex, nw, dysn, dproj):
    t = y.shape[0]
    w = 4 * LANES
    blk = ((SSD_TM, w), lambda j, i: (i, j))
    return _ew_bwd(name, _ssdpost_fn, (D_INNER // w, t // SSD_TM), _ssdpost_operands(y, xc, proj, d_skip, ex, nw),
                   [(dysn, *blk)],
                   [dict(idx=0, kind="tile", shape=(t, D_INNER), dtype=F32, block=blk[0], imap=blk[1]),
                    dict(idx=1, kind="tile", shape=(t, D_INNER), dtype=F32, block=blk[0], imap=blk[1]),
                    dict(idx=2, kind="tile", shape=dproj.shape, dtype=dproj.dtype, block=blk[0],
                         imap=lambda j, i: (i, Z0 // w + j), into=dproj),
                    dict(idx=3, kind="acc", shape=(1, DTW), dtype=F32, block=(1, DTW), imap=lambda j, i: (0, 0),
                         first=lambda: (pl.program_id(0) == 0) & (pl.program_id(1) == 0)),
                    dict(idx=5, kind="acc", shape=(1, D_INNER), dtype=F32, block=(1, w), imap=lambda j, i: (0, j),
                         first=lambda: pl.program_id(1) == 0)])


def _merge_operands(ya, ys, proj, b_gates):
    w = 4 * LANES
    g0 = G0 // w
    nh = D_MODEL // w
    return [(ya, (TM, w), lambda j, i: (i, j)), (ys, (TM, w), lambda j, i: (i, j)),
            (proj, (TM, w), lambda j, i: (i, g0 + j)), (proj, (TM, w), lambda j, i: (i, g0 + nh + j)),
            (b_gates, (1, w), lambda j, i: (0, j)), (b_gates, (1, w), lambda j, i: (0, nh + j))]


def _merge(name, ya, ys, proj, b_gates):
    t = ya.shape[0]
    w = 4 * LANES
    return _ew(name, _merge_fn, (D_MODEL // w, t // TM), _merge_operands(ya, ys, proj, b_gates),
               [((t, D_MODEL), BF16, (TM, w), lambda j, i: (i, j))])[0]


def _merge_bwd(name, ya, ys, proj, b_gates, dmixed):
    t = ya.shape[0]
    w = 4 * LANES
    blk = ((TM, w), lambda j, i: (i, j))
    first = lambda: pl.program_id(1) == 0
    tile = lambda k, dt: dict(idx=k, kind="tile", shape=(t, D_MODEL), dtype=dt, block=blk[0], imap=blk[1])
    acc = lambda k: dict(idx=k, kind="acc", shape=(1, D_MODEL), dtype=F32, block=(1, w), imap=lambda j, i: (0, j), first=first)
    return _ew_bwd(name, _merge_fn, (D_MODEL // w, t // TM), _merge_operands(ya, ys, proj, b_gates), [(dmixed, *blk)],
                   [tile(0, BF16), tile(1, BF16), tile(2, BF16), tile(3, BF16), acc(4), acc(5)])


def _loss(name, y, tgt):
    t = y.shape[0]
    blk = pl.BlockSpec((TM, D_MODEL), lambda i: (i, 0))

    def body(y_ref, t_ref, dy_ref, l_ref):
        e = y_ref[...] - t_ref[...]
        dy_ref[...] = e * (1.0 / D_MODEL)
        part = jnp.sum(jnp.sum(e * e, axis=-1, keepdims=True), axis=0, keepdims=True) * (0.5 / D_MODEL)
        part = jnp.broadcast_to(part, (8, LANES))

        @pl.when(pl.program_id(0) == 0)
        def _():
            l_ref[...] = part

        @pl.when(pl.program_id(0) > 0)
        def _():
            l_ref[...] += part

    return pl.pallas_call(
        body, name=name, grid=(t // TM,), in_specs=[blk, blk],
        out_specs=[blk, pl.BlockSpec((8, LANES), lambda i: (0, 0))],
        out_shape=[jax.ShapeDtypeStruct((t, D_MODEL), F32), jax.ShapeDtypeStruct((8, LANES), F32)],
        compiler_params=_cp(("arbitrary",)),
    )(y, tgt)


def _adamw_fn(w, g, m, v):
    m2 = B1 * m + (1.0 - B1) * g
    v2 = B2 * v + (1.0 - B2) * (g * g)
    m_hat = m2 / (1.0 - B1 ** STEP)
    v_hat = v2 / (1.0 - B2 ** STEP)
    return -LR * (m_hat / (jnp.sqrt(v_hat) + ADAM_EPS) + WD * w), m2, v2


def _adamw(name, w, g, m, v):
    rows, cols = w.shape
    tm = rows
    for cand in (512, 256, 128, 64, 32, 16, 8):
        if rows % cand == 0 and cand * cols * 4 <= (1 << 21):
            tm = cand
            break
    blk = ((tm, cols), lambda i: (i, 0))
    return _ew(name, _adamw_fn, (rows // tm,), [(a, *blk) for a in (w, g, m, v)], [((rows, cols), F32, *blk)] * 3)


NCH = 4
TMM = 1024
TKK = 1024


def _ffn_fwd(tag, x, nw, wg, wu, wd, li):
    t, fc = x.shape[0], wg.shape[-1]
    h = _rmsnorm(tag + "_norm", x, nw)

    def up_body(h_ref, wg_ref, wu_ref, g_ref, u_ref, a_ref):
        hv = h_ref[...]
        g = _dg(hv, wg_ref[...], 1, 0).astype(BF16)
        u = _dg(hv, wu_ref[...], 1, 0).astype(BF16)
        g_ref[...] = g
        u_ref[...] = u
        a_ref[...] = _swiglu_fn(g.astype(F32), u.astype(F32))[0].astype(BF16)

    w_spec = pl.BlockSpec((None, None, D_MODEL, fc), lambda k, i: (k, li, 0, 0))
    o_spec = pl.BlockSpec((None, TMM, fc), lambda k, i: (k, i, 0))
    g, u, a = pl.pallas_call(
        up_body, name=tag + "_up_act", grid=(NCH, t // TMM),
        in_specs=[pl.BlockSpec((TMM, D_MODEL), lambda k, i: (i, 0)), w_spec, w_spec], out_specs=[o_spec] * 3,
        out_shape=[jax.ShapeDtypeStruct((NCH, t, fc), BF16)] * 3, compiler_params=_cp(("parallel", "parallel")),
    )(h, wg, wu)
    row = ((TMM, D_MODEL), lambda i, j, k: (i, 0))
    y = _mmx(tag + "_down", (t // TMM, 1, NCH),
             (a, (None, TMM, fc), lambda i, j, k: (k, i, 0)),
             (wd, (None, None, fc, D_MODEL), lambda i, j, k: (k, li, 0, 0)),
             ((t, D_MODEL), F32, *row), (1, 0), alpha=0.5, res=(x, *row))
    return y, (x, h, g, u, a)


def _ffn_bwd(tag, saved, nw, wg, wu, wd, li, dy, bufs, gl):
    x, h, g, u, a = saved
    t, fc = x.shape[0], wg.shape[-1]
    bg, bu, bd = bufs
    def dact_body(dy_ref, wd_ref, g_ref, u_ref, dg_ref, du_ref):
        da = _dg(dy_ref[...], wd_ref[...], 1, 1) * 0.5
        _, vjp = jax.vjp(_swiglu_fn, g_ref[...].astype(F32), u_ref[...].astype(F32))
        dg, du = vjp((da,))
        dg_ref[...] = dg.astype(BF16)
        du_ref[...] = du.astype(BF16)

    c_spec = pl.BlockSpec((None, TMM, fc), lambda k, i: (k, i, 0))
    dg, du = pl.pallas_call(
        dact_body, name=tag + "_down_dx_act", grid=(NCH, t // TMM),
        in_specs=[pl.BlockSpec((TMM, D_MODEL), lambda k, i: (i, 0)),
                  pl.BlockSpec((None, None, fc, D_MODEL), lambda k, i: (k, li, 0, 0)), c_spec, c_spec],
        out_specs=[c_spec] * 2, out_shape=[jax.ShapeDtypeStruct((NCH, t, fc), BF16)] * 2,
        compiler_params=_cp(("parallel", "parallel")),
    )(dy, wd, g, u)
    bd = _mmx(tag + "_down_dw", (NCH, 1, t // TKK),
              (a, (None, TKK, fc), lambda k, j, kk: (k, kk, 0)),
              (dy, (TKK, D_MODEL), lambda k, j, kk: (kk, 0)),
              (bd.shape, BF16, (None, None, fc, D_MODEL), lambda k, j, kk: (gl, k, 0, 0)), (0, 0), alpha=0.5, into=bd)
    def dw(name, d, buf):
        return _mmx(name, (NCH, 1, t // TKK),
                    (h, (TKK, D_MODEL), lambda k, i, kk: (kk, 0)),
                    (d, (None, TKK, fc), lambda k, i, kk: (k, kk, 0)),
                    (buf.shape, BF16, (None, None, D_MODEL, fc), lambda k, i, kk: (gl, k, 0, 0)), (0, 0), into=buf)

    bg, bu = dw(tag + "_gate_dw", dg, bg), dw(tag + "_up_dw", du, bu)
    row = ((TMM, D_MODEL), lambda i, j, k: (i, 0))

    def dx_(name, d, w, res):
        return _mmx(name, (t // TMM, 1, NCH),
                    (d, (None, TMM, fc), lambda i, j, k: (k, i, 0)),
                    (w, (None, None, D_MODEL, fc), lambda i, j, k: (k, li, 0, 0)),
                    ((t, D_MODEL), F32, *row), (1, 1), res=None if res is None else (res, *row))

    dh = dx_(tag + "_up_dx", du, wu, dx_(tag + "_gate_dx", dg, wg, None))
    dx, dnw = _rmsnorm_bwd(tag + "_norm_bwd", x, nw, dh, dy)
    return dx, dnw, (bg, bu, bd)


def _mixer_fwd(tag, x, p, c):
    t = x.shape[0]
    bl = t // SEQ
    h = _rmsnorm(tag + "_norm", x, p["mix_norm_w"])
    proj = _mm(tag + "_in", h, p["w_in"], "nn")
    qk = _qkprep(tag + "_qk", proj, p["qkw"], c["cos"], c["sin"], c)
    os_, ls_ = [], []
    for g in range(3):
        o, l = _att_fwd(f"{tag}_att{g}", qk, proj, g)
        os_.append(o)
        ls_.append(l)
    att = _attmix(tag + "_attmix", os_, ls_)
    li = p["layer"]
    wa, ws, wo = p["w_att_proj"], p["w_ssm_proj"], p["w_out"]
    ca, cs, co = wa.shape[-1], ws.shape[-2], wo.shape[-2]
    row = ((TMM, D_MODEL), lambda i, j, k: (i, 0))
    ya = _mmx(tag + "_attproj", (t // TMM, NCH, 1),
              (att, (TMM, ATT_OUT), lambda i, k, kk: (i, 0)),
              (wa, (None, None, ATT_OUT, ca), lambda i, k, kk: (k, li, 0, 0)),
              ((t, D_MODEL), F32, (TMM, ca), lambda i, k, kk: (i, k)), (1, 0))
    proj3 = proj.reshape(bl, SEQ, NP)
    xc3 = _conv(tag + "_conv", proj3, p["conv_w"], p["conv_b"])
    xc = xc3.reshape(t, XBC)
    dtb, dab = _ssdpre(tag + "_ssdpre", proj, p["dt_bias"], p["a_log"], c["ex"])
    dtb3, dab3 = dtb.reshape(bl, SEQ, D_INNER), dab.reshape(bl, SEQ, D_INNER)
    y3, states = _ssd_fwd(tag + "_ssd", xc3, dtb3, dab3, c["ltri"])
    y = y3.reshape(t, D_INNER)
    ysn = _ssdpost(tag + "_ssdpost", y, xc, proj, p["d_skip"], c["ex"], p["ssm_norm_w"])
    ys = _mmx(tag + "_ssmproj", (t // TMM, 1, NCH),
              (ysn, (TMM, cs), lambda i, j, k: (i, k)),
              (ws, (None, None, cs, D_MODEL), lambda i, j, k: (k, li, 0, 0)),
              ((t, D_MODEL), F32, *row), (1, 0))
    mixed = _merge(tag + "_merge", ya, ys, proj, p["b_gates"])
    out = _mmx(tag + "_out", (t // TMM, 1, NCH),
               (mixed, (TMM, co), lambda i, j, k: (i, k)),
               (wo, (None, None, co, D_MODEL), lambda i, j, k: (k, li, 0, 0)),
               ((t, D_MODEL), F32, *row), (1, 0), res=(x, *row))
    return out, (x, h, proj, qk, os_, ls_, att, ya, xc3, dtb3, dab3, states, y, ysn, ys, mixed)


def _mixer_bwd(tag, saved, p, c, dout, bufs):
    x, h, proj, qk, os_, ls_, att, ya, xc3, dtb3, dab3, states, y, ysn, ys, mixed = saved
    t = x.shape[0]
    bl = t // SEQ
    xc = xc3.reshape(t, XBC)
    proj3 = proj.reshape(bl, SEQ, NP)
    gr = {}
    li, gl = p["layer"], p["global_layer"]
    wa, ws, wo = p["w_att_proj"], p["w_ssm_proj"], p["w_out"]
    ca, cs, co = wa.shape[-1], ws.shape[-2], wo.shape[-2]
    b_att, b_ssm, b_out = bufs

    def chunk_dx(name, d, w, cw):
        return _mmx(name, (t // TMM, NCH, D_MODEL // TKK),
                    (d, (TMM, TKK), lambda i, k, kk: (i, kk)),
                    (w, (None, None, cw, TKK), lambda i, k, kk: (k, li, 0, kk)),
                    ((t, NCH * cw), F32, (TMM, cw), lambda i, k, kk: (i, k)), (1, 1))

    def full_dw(name, a_, d, buf):
        kdim = a_.shape[1]
        tm = min(kdim, 1024)
        return _mmx(name, (kdim // tm, 1, t // TKK),
                    (a_, (TKK, tm), lambda i, j, kk: (kk, i)),
                    (d, (TKK, D_MODEL), lambda i, j, kk: (kk, 0)),
                    (buf.shape, BF16, (None, tm, D_MODEL), lambda i, j, kk: (gl, i, 0)), (0, 0), into=buf)

    dmixed = chunk_dx(tag + "_out_dx", dout, wo, co)
    b_out = full_dw(tag + "_out_dw", mixed, dout, b_out)
    dya, dys, dga, dgs, dba, dbs = _merge_bwd(tag + "_merge_bwd", ya, ys, proj, p["b_gates"], dmixed)
    gr["b_gates"] = jnp.concatenate([dba, dbs], axis=1)
    datt = _mmx(tag + "_attproj_dx", (t // TMM, 1, NCH),
                (dya, (TMM, ca), lambda i, j, k: (i, k)),
                (wa, (None, None, ATT_OUT, ca), lambda i, j, k: (k, li, 0, 0)),
                ((t, ATT_OUT), F32, (TMM, ATT_OUT), lambda i, j, k: (i, 0)), (1, 1))
    b_att = _mmx(tag + "_attproj_dw", (NCH, 1, t // TKK),
                 (att, (TKK, ATT_OUT), lambda k, j, kk: (kk, 0)),
                 (dya, (TKK, ca), lambda k, j, kk: (kk, k)),
                 (b_att.shape, BF16, (None, None, ATT_OUT, ca), lambda k, j, kk: (gl, k, 0, 0)), (0, 0), into=b_att)
    dysn = chunk_dx(tag + "_ssmproj_dx", dys, ws, cs)
    b_ssm = full_dw(tag + "_ssmproj_dw", ysn, dys, b_ssm)
    gr["bufs"] = (b_att, b_ssm, b_out)
    dmix = _attmix_bwd(tag + "_attmix_bwd", os_, ls_, datt)
    dq = dk = dv = jnp.zeros((t, QKV), F32)
    for g in range(3):
        dq, dk, dv = _att_bwd(f"{tag}_att{g}_bwd", qk, proj, g, dmix[g], dmix[3 + g], dq, dk, dv)
    dproj = jnp.zeros((t, NP), BF16)
    dproj = lax.dynamic_update_slice(dproj, dv.astype(BF16), (0, V0))
    dproj = lax.dynamic_update_slice(dproj, dga, (0, G0))
    dproj = lax.dynamic_update_slice(dproj, dgs, (0, G0 + D_MODEL))
    dproj, gr["qkw"] = _qkprep_bwd(tag + "_qk_bwd", proj, p["qkw"], c["cos"], c["sin"], c, dq, dk, dproj)
    dy, dxs_part, dproj, gr["d_skip"], gr["ssm_norm_w"] = _ssdpost_bwd(
        tag + "_ssdpost_bwd", y, xc, proj, p["d_skip"], c["ex"], p["ssm_norm_w"], dysn, dproj)
    dxs3, ddtb3, ddab3, db3, dc3 = _ssd_bwd(
        tag + "_ssd_bwd", xc3, dtb3, dab3, c["ltri"], states, dy.reshape(bl, SEQ, D_INNER), dxs_part.reshape(bl, SEQ, D_INNER))
    ddt, gr["dt_bias"], gr["a_log"] = _ssdpre_bwd(
        tag + "_ssdpre_bwd", proj, p["dt_bias"], p["a_log"], c["ex"], ddtb3.reshape(t, D_INNER), ddab3.reshape(t, D_INNER))
    dproj = lax.dynamic_update_slice(dproj, ddt, (0, DT0))
    dproj3, dcw0, dcw1, dcw2, dcw3, gr["conv_b"] = _conv_bwd(
        tag + "_conv_bwd", proj3, p["conv_w"], p["conv_b"], dxs3, db3, dc3, dproj.reshape(bl, SEQ, NP))
    dproj = dproj3.reshape(t, NP)
    gr["conv_w"] = jnp.concatenate([dcw0, dcw1, dcw2, dcw3], axis=0)
    gr["w_in"] = _mm(tag + "_in_dw", h, dproj, "tn", out_dtype=BF16)
    dh = _mm(tag + "_in_dx", dproj, p["w_in"], "nt")
    dx, gr["mix_norm_w"] = _rmsnorm_bwd(tag + "_norm_bwd", x, p["mix_norm_w"], dh, dout)
    return dx, gr


def _constants():
    cos, sin = _rope_tables()
    return dict(cos=cos, sin=sin, hmean=_head_mean_mat(),
                ex=_head_expand_mat(), ltri=_ltri_mat())


ANY = pl.BlockSpec(memory_space=pl.ANY)


def _mesh_pos():
    return lax.axis_index("x"), lax.axis_index("y"), lax.axis_index("c")


def _other_chips(x, y):
    return [(1 - x, y), (x, 1 - y), (1 - x, 1 - y)]


def _gather_exchange(srcs, outs, send_sems, recv_sems):
    n = len(srcs)
    x, y, c = _mesh_pos()
    chips = _other_chips(x, y)

    def part(a, chip, hf):
        h = srcs[a].shape[1] // 2
        return outs[a].at[2 * chip[0] + chip[1], :, pl.ds(hf * h, h), :]

    def mine(a):
        h = srcs[a].shape[1] // 2
        return srcs[a].at[:, pl.ds(c * h, h), :]

    def copy(a, k, src_ref, dst_ref, to):
        return pltpu.make_async_remote_copy(src_ref=src_ref, dst_ref=dst_ref, send_sem=send_sems.at[6 * a + k],
                                            recv_sem=recv_sems.at[6 * a + k], device_id=to, device_id_type=MESH)

    first = [copy(a, j, mine(a), part(a, (x, y), c), (*chip, c)) for a in range(n) for j, chip in enumerate(chips)]
    for cp in first:
        cp.start()
    passed = []
    for a in range(n):
        for j, chip in enumerate(chips):
            copy(a, j, part(a, chip, c), part(a, chip, c), (x, y, c)).wait_recv()
            fw = copy(a, 3 + j, part(a, chip, c), part(a, chip, c), (x, y, 1 - c))
            fw.start()
            passed.append(fw)
    for a in range(n):
        for j, chip in enumerate(chips):
            copy(a, 3 + j, part(a, chip, 1 - c), part(a, chip, 1 - c), (x, y, c)).wait_recv()
    for cp in first + passed:
        cp.wait_send()


def _gather_inits(ws, chip_idx):
    return [lax.dynamic_update_slice(jnp.zeros((NCH, *w.shape), w.dtype), w[None], (chip_idx[0], 0, 0, 0)) for w in ws]


def _all_gather_weights(ws, chip_idx):
    n = len(ws)
    inits = _gather_inits(ws, chip_idx)

    def body(*refs):
        _gather_exchange(refs[:n], refs[2 * n:3 * n], refs[3 * n], refs[3 * n + 1])

    return pl.pallas_call(
        body, name="all_gather_weights", out_shape=[jax.ShapeDtypeStruct(i.shape, i.dtype) for i in inits],
        in_specs=[ANY] * (2 * n), out_specs=[ANY] * n, input_output_aliases={n + a: a for a in range(n)},
        scratch_shapes=[pltpu.SemaphoreType.DMA((6 * n,)), pltpu.SemaphoreType.DMA((6 * n,))],
    )(*ws, *inits)


def _all_gather_weights_beside(name, ws, chip_idx):
    n = len(ws)
    src_refs = [jax.new_ref(w, memory_space=pltpu.MemorySpace.HBM) for w in ws]
    out_refs = [jax.new_ref(i, memory_space=pltpu.MemorySpace.HBM) for i in _gather_inits(ws, chip_idx)]

    @pl.kernel(mesh=plsc.ScalarSubcoreMesh(axis_name="sequencer", num_cores=1), name=name,
               scratch_types=(pltpu.SemaphoreType.DMA((6 * n,)), pltpu.SemaphoreType.DMA((6 * n,))),
               compiler_params=pltpu.CompilerParams(collective_id=1))
    def launch(send_sems, recv_sems):
        x, y, c = _mesh_pos()
        barrier = pltpu.get_barrier_semaphore()
        for peer in [(x, y, 1 - c)] + [(*chip, c) for chip in _other_chips(x, y)]:
            pl.semaphore_signal(barrier, inc=1, device_id=peer, device_id_type=MESH)
        pl.semaphore_wait(barrier, 4)
        _gather_exchange(src_refs, out_refs, send_sems, recv_sems)

    launch()
    return [r[...] for r in out_refs]


def _pair_exchange(name, gs):
    n = len(gs)

    def body(*refs):
        srcs, outs, send_sems, recv_sems = refs[:n], refs[n:2 * n], refs[2 * n], refs[2 * n + 1]
        x, y, c = _mesh_pos()
        cps = []
        for a in range(n):
            h = gs[a].shape[2] // 2
            cps.append(pltpu.make_async_remote_copy(
                src_ref=srcs[a].at[:, :, pl.ds((1 - c) * h, h), :], dst_ref=outs[a], send_sem=send_sems.at[a],
                recv_sem=recv_sems.at[a], device_id=(x, y, 1 - c), device_id_type=MESH))
        for cp in cps:
            cp.start()
        for cp in cps:
            cp.wait()

    return pl.pallas_call(
        body, name=name,
        out_shape=[jax.ShapeDtypeStruct((g.shape[0], g.shape[1], g.shape[2] // 2, g.shape[3]), g.dtype) for g in gs],
        in_specs=[ANY] * n, out_specs=[ANY] * n,
        scratch_shapes=[pltpu.SemaphoreType.DMA((n,)), pltpu.SemaphoreType.DMA((n,))],
    )(*gs)


def _chip_exchange_copies(srcs, outs, send_sems, recv_sems):
    x, y, c = _mesh_pos()
    cps = [pltpu.make_async_remote_copy(
        src_ref=srcs[a].at[:, 2 * chip[0] + chip[1]], dst_ref=outs[a].at[j], send_sem=send_sems.at[3 * a + j],
        recv_sem=recv_sems.at[3 * a + j], device_id=(*chip, c), device_id_type=MESH)
        for a in range(len(srcs)) for j, chip in enumerate(_other_chips(x, y))]
    for cp in cps:
        cp.start()
    for cp in cps:
        cp.wait()


def _chip_exchange_shapes(hs):
    return [jax.ShapeDtypeStruct((3, h.shape[0], h.shape[2], h.shape[3]), h.dtype) for h in hs]


def _chip_exchange(hs):
    n = len(hs)

    def body(*refs):
        _chip_exchange_copies(refs[:n], refs[n:2 * n], refs[2 * n], refs[2 * n + 1])

    return pl.pallas_call(
        body, name="grad_chip_exchange", out_shape=_chip_exchange_shapes(hs),
        in_specs=[ANY] * n, out_specs=[ANY] * n,
        scratch_shapes=[pltpu.SemaphoreType.DMA((3 * n,)), pltpu.SemaphoreType.DMA((3 * n,))],
    )(*hs)


def _zero_after(v):
    return jnp.minimum(lax.bitcast_convert_type(v, jnp.uint16).astype(jnp.int32), 0).astype(F32)


def _chip_exchange_beside(name, hs):
    n = len(hs)
    src_refs = [jax.new_ref(h, memory_space=pltpu.MemorySpace.HBM) for h in hs]
    out_refs = [jax.empty_ref(s, memory_space=pltpu.MemorySpace.HBM) for s in _chip_exchange_shapes(hs)]

    @pl.kernel(mesh=plsc.ScalarSubcoreMesh(axis_name="sequencer", num_cores=1), name=name,
               scratch_types=(pltpu.SemaphoreType.DMA((3 * n,)), pltpu.SemaphoreType.DMA((3 * n,))),
               compiler_params=pltpu.CompilerParams(collective_id=2))
    def launch(send_sems, recv_sems):
        x, y, c = _mesh_pos()
        barrier = pltpu.get_barrier_semaphore()
        for chip in _other_chips(x, y):
            pl.semaphore_signal(barrier, inc=1, device_id=(*chip, c), device_id_type=MESH)
        pl.semaphore_wait(barrier, 3)
        _chip_exchange_copies(src_refs, out_refs, send_sems, recv_sems)

    launch()
    return [r[...] for r in out_refs]


def _pair_share(rs):
    n = len(rs)

    def body(*refs):
        outs, send_sems, recv_sems = refs[n:2 * n], refs[2 * n], refs[2 * n + 1]
        x, y, c = _mesh_pos()
        cps = [pltpu.make_async_remote_copy(src_ref=outs[a].at[:, c], dst_ref=outs[a].at[:, c], send_sem=send_sems.at[a],
                                            recv_sem=recv_sems.at[a], device_id=(x, y, 1 - c), device_id_type=MESH)
               for a in range(n)]
        for cp in cps:
            cp.start()
        for a in range(n):
            pltpu.make_async_remote_copy(src_ref=outs[a].at[:, 1 - c], dst_ref=outs[a].at[:, 1 - c],
                                         send_sem=send_sems.at[a], recv_sem=recv_sems.at[a], device_id=(x, y, c),
                                         device_id_type=MESH).wait_recv()
        for cp in cps:
            cp.wait_send()

    return pl.pallas_call(
        body, name="grad_pair_share", out_shape=[jax.ShapeDtypeStruct(r.shape, r.dtype) for r in rs],
        in_specs=[ANY] * n, out_specs=[ANY] * n, input_output_aliases={a: a for a in range(n)},
        scratch_shapes=[pltpu.SemaphoreType.DMA((n,)), pltpu.SemaphoreType.DMA((n,))],
    )(*rs)


def _pair_sum(name, g, recv, c_idx):
    d, k, h, b = recv.shape
    g5 = g.reshape(d * k, 2, h, b)

    def body(c_ref, a_ref, b_ref, o_ref):
        o_ref[...] = (a_ref[...].astype(F32) + b_ref[...].astype(F32)).astype(o_ref.dtype)

    out = pl.pallas_call(
        body, name=name,
        grid_spec=pltpu.PrefetchScalarGridSpec(
            num_scalar_prefetch=1, grid=(d * k,),
            in_specs=[pl.BlockSpec((None, None, h, b), lambda i, c: (i, c[0], 0, 0)),
                      pl.BlockSpec((None, h, b), lambda i, c: (i, 0, 0))],
            out_specs=pl.BlockSpec((None, h, b), lambda i, c: (i, 0, 0))),
        out_shape=jax.ShapeDtypeStruct((d * k, h, b), BF16),
        compiler_params=_cp(("arbitrary",)),
    )(c_idx, g5, recv.reshape(d * k, h, b))
    return out.reshape(d, k, h, b)


def _chip_sum(name, ha, recv, chip_idx, c_idx, depth, l0, into=None):
    d, _, h, b = ha.shape

    def body(k_ref, c_ref, a_ref, r0, r1, r2, *rest):
        rest[-1][...] = ((a_ref[...].astype(F32) + r0[...].astype(F32)) + r1[...].astype(F32)) + r2[...].astype(F32)

    blk = (None, None, h, b)
    extra = [] if into is None else [into]
    return pl.pallas_call(
        body, name=name,
        grid_spec=pltpu.PrefetchScalarGridSpec(
            num_scalar_prefetch=2, grid=(d,),
            in_specs=[pl.BlockSpec(blk, lambda l, k, c: (l, k[0], 0, 0))] +
                     [pl.BlockSpec(blk, lambda l, k, c, j=j: (j, l, 0, 0)) for j in range(3)] + [ANY] * len(extra),
            out_specs=pl.BlockSpec(blk, lambda l, k, c: (l0 + l, c[0], 0, 0))),
        out_shape=jax.ShapeDtypeStruct((depth, 2, h, b), F32),
        input_output_aliases={} if into is None else {6: 0},
        compiler_params=_cp(("arbitrary",)),
    )(chip_idx, c_idx, ha, recv, recv, recv, *extra)


def _all_sum_small(name, vec):
    rows = vec.shape[0]

    def body(v_ref, o_ref, buf, send_sems, recv_sems):
        x, y, c = _mesh_pos()
        me, sibling = (x, y, c), (x, y, 1 - c)
        chips = _other_chips(x, y)

        def slot(p):
            return buf.at[4 * p[0] + 2 * p[1] + p[2]]

        def copy(k, block, to, src=None):
            return pltpu.make_async_remote_copy(src_ref=slot(block) if src is None else src, dst_ref=slot(block),
                                                send_sem=send_sems.at[k], recv_sem=recv_sems.at[k],
                                                device_id=to, device_id_type=MESH)

        first = [copy(0, me, sibling, src=v_ref)]
        first += [copy(1 + j, me, (*chip, c), src=v_ref) for j, chip in enumerate(chips)]
        for cp in first:
            cp.start()
        passed = [copy(4 + j, (*chip, c), sibling) for j, chip in enumerate(chips)]
        for j, chip in enumerate(chips):
            copy(1 + j, (*chip, c), me).wait_recv()
            passed[j].start()
        copy(0, sibling, me).wait_recv()
        for j, chip in enumerate(chips):
            copy(4 + j, (*chip, 1 - c), me).wait_recv()
        for cp in first + passed:
            cp.wait_send()
        slot(me)[...] = v_ref[...]
        acc = buf[0]
        for k in range(1, 8):
            acc = acc + buf[k]
        o_ref[...] = acc

    vm = pl.BlockSpec(memory_space=pltpu.VMEM)
    return pl.pallas_call(
        body, name=name, out_shape=jax.ShapeDtypeStruct((rows, LANES), F32),
        in_specs=[vm], out_specs=vm, compiler_params=pltpu.CompilerParams(vmem_limit_bytes=VMEM_LIMIT),
        scratch_shapes=[pltpu.VMEM((8, rows, LANES), F32), pltpu.SemaphoreType.DMA((7,)), pltpu.SemaphoreType.DMA((7,))],
    )(vec)


def _pad_lanes(v, n=LANES):
    return jnp.pad(v, (0, n - v.shape[0]))[None, :]


def _w_in_from_kernel(w):
    return jnp.concatenate([w[:, :6656], w[:, X0:DT0], w[:, DT0:DT0 + 32], w[:, G0:X0]], axis=1)


def _layer_params(big, small, i):
    p = {k: big[k][i][0] for k in GRAD_BUFS}
    li = big["w_in"][i][1]
    p["layer"], p["global_layer"] = li, i
    w_in = big["w_in"][i][0]
    cw = w_in.shape[-1]
    pieces = []
    for lo, hi in ((0, 6656), (9760, N_IN), (6656, 9728), (9728, 9760)):
        for k in range(NCH):
            a, b = max(lo, k * cw), min(hi, (k + 1) * cw)
            if a < b:
                pieces.append(w_in[k, li, :, a - k * cw:b - k * cw])
    p["w_in"] = jnp.concatenate(pieces + [jnp.zeros((D_MODEL, NP - N_IN), w_in.dtype)], axis=1)
    p["conv_w"] = big["conv_w"][i][:, None, :]
    for k in ("ffn1_norm_w", "mix_norm_w", "ffn2_norm_w", "b_gates", "conv_b", "ssm_norm_w"):
        p[k] = small[k][i][None, :]
    for k in ("dt_bias", "a_log", "d_skip"):
        p[k] = _pad_lanes(small[k][i])
    p["qkw"] = jnp.stack([_pad_lanes(small["q_norm_w"][i]), _pad_lanes(small["k_norm_w"][i])])
    return p


GRAD_BUFS = ("ffn1_w_gate", "ffn1_w_up", "ffn1_w_down", "w_att_proj", "w_ssm_proj", "w_out",
             "ffn2_w_gate", "ffn2_w_up", "ffn2_w_down")


def _local_step(x, tgt, layers, c, exchange_rest=None):
    depth = len(layers)
    ffn = {f: tuple(f + s for s in ("_w_gate", "_w_up", "_w_down")) for f in ("ffn1", "ffn2")}
    saved = []
    for i, p in enumerate(layers):
        x, s1 = _ffn_fwd(f"L{i}_ffn1", x, p["ffn1_norm_w"], *[p[n] for n in ffn["ffn1"]], p["layer"])
        x, s2 = _mixer_fwd(f"L{i}_mix", x, p, c)
        x, s3 = _ffn_fwd(f"L{i}_ffn2", x, p["ffn2_norm_w"], *[p[n] for n in ffn["ffn2"]], p["layer"])
        saved.append((s1, s2, s3))
    dx, loss_blk = _loss("loss", x, tgt)

    def new_bufs(nl):
        out = {}
        for n in GRAD_BUFS:
            a, b = layers[0][n].shape[-2:]
            out[n] = jnp.zeros((nl, NCH * a, b) if n in ("w_ssm_proj", "w_out") else (nl, NCH, a, b), BF16)
        return out

    def finished(buf):
        out = dict(buf)
        for n in ("w_ssm_proj", "w_out"):
            a, b = layers[0][n].shape[-2:]
            out[n] = buf[n].reshape(-1, NCH, a, b)
        return out

    grads = [None] * depth
    rest_out = [None] * depth
    for i in reversed(range(depth)):
        p = layers[i]
        s1, s2, s3 = saved[i]
        buf, gl = new_bufs(1), 0
        p = dict(p, global_layer=gl)
        if i + 2 < depth and exchange_rest is not None:
            dx = dx + _zero_after(rest_out[i + 2][1][0][0, 0, 0, 0])

        def ffn_bwd(f, s, dy):
            names = ffn[f]
            d, dn, new = _ffn_bwd(f"L{i}_{f}", s, p[f + "_norm_w"], *[p[n] for n in names], p["layer"], dy,
                                  tuple(buf[n] for n in names), gl)
            buf.update(zip(names, new))
            return d, dn

        dx, dn2 = ffn_bwd("ffn2", s3, dx)
        dx, gr = _mixer_bwd(f"L{i}_mix", s2, p, c, dx, (buf["w_att_proj"], buf["w_ssm_proj"], buf["w_out"]))
        buf["w_att_proj"], buf["w_ssm_proj"], buf["w_out"] = gr.pop("bufs")
        if i == 0 and depth > 1 and exchange_rest is not None:
            rest_out[0], ready = exchange_rest(0, finished(buf), [gr], rest_out[1][1][0][0, 0, 0, 0])
            dx = dx + _zero_after(ready)
        dx, dn1 = ffn_bwd("ffn1", s1, dx)
        gr.update(ffn1_norm_w=dn1, ffn2_norm_w=dn2)
        grads[i] = gr
        if i >= 1 and exchange_rest is not None:
            rest_out[i], ready = exchange_rest(i, finished(buf), grads[i:i + 1], None)
            dx = dx + _zero_after(ready)
    return loss_blk, dx, grads, finished(buf), rest_out


WEIGHTS = ["ffn1_norm_w", "ffn1_w_gate", "ffn1_w_up", "ffn1_w_down", "mix_norm_w", "w_in", "b_gates", "q_norm_w",
           "k_norm_w", "conv_w", "conv_b", "dt_bias", "a_log", "d_skip", "ssm_norm_w", "w_att_proj", "w_ssm_proj",
           "w_out", "ffn2_norm_w", "ffn2_w_gate", "ffn2_w_up", "ffn2_w_down"]
SHARD_AXIS = {"ffn1_w_gate": 2, "ffn1_w_up": 2, "ffn1_w_down": 1, "w_in": 2, "conv_w": 2, "w_att_proj": 2,
              "w_ssm_proj": 1, "w_out": 1, "ffn2_w_gate": 2, "ffn2_w_up": 2, "ffn2_w_down": 1}
BIG = [n for n in WEIGHTS if n in SHARD_AXIS]
SMALL = [n for n in WEIGHTS if n not in SHARD_AXIS]
def _from_flat(flat, shapes):
    v = flat.reshape(-1)
    out, off = [], 0
    for s in shapes:
        n = math.prod(s)
        out.append(v[off:off + n].reshape(s))
        off += n
    return out


def _pack_small(parts):
    v = jnp.concatenate([p.astype(F32).reshape(-1) for p in parts])
    rows = -(-v.shape[0] // (8 * LANES)) * 8
    return jnp.pad(v, (0, rows * LANES - v.shape[0])).reshape(rows, LANES)


def kernel(x, ffn1_norm_w, ffn1_w_gate, ffn1_w_up, ffn1_w_down, mix_norm_w, w_in, b_gates, q_norm_w, k_norm_w, conv_w, conv_b, dt_bias, a_log, d_skip, ssm_norm_w, w_att_proj, w_ssm_proj, w_out, ffn2_norm_w, ffn2_w_gate, ffn2_w_up, ffn2_w_down, loss_target, m_ffn1_norm_w, m_ffn1_w_gate, m_ffn1_w_up, m_ffn1_w_down, m_mix_norm_w, m_w_in, m_b_gates, m_q_norm_w, m_k_norm_w, m_conv_w, m_conv_b, m_dt_bias, m_a_log, m_d_skip, m_ssm_norm_w, m_w_att_proj, m_w_ssm_proj, m_w_out, m_ffn2_norm_w, m_ffn2_w_gate, m_ffn2_w_up, m_ffn2_w_down, v_ffn1_norm_w, v_ffn1_w_gate, v_ffn1_w_up, v_ffn1_w_down, v_mix_norm_w, v_w_in, v_b_gates, v_q_norm_w, v_k_norm_w, v_conv_w, v_conv_b, v_dt_bias, v_a_log, v_d_skip, v_ssm_norm_w, v_w_att_proj, v_w_ssm_proj, v_w_out, v_ffn2_norm_w, v_ffn2_w_gate, v_ffn2_w_up, v_ffn2_w_down):
    w = dict(zip(WEIGHTS, (ffn1_norm_w, ffn1_w_gate, ffn1_w_up, ffn1_w_down, mix_norm_w, w_in, b_gates, q_norm_w, k_norm_w, conv_w, conv_b, dt_bias, a_log, d_skip, ssm_norm_w, w_att_proj, w_ssm_proj, w_out, ffn2_norm_w, ffn2_w_gate, ffn2_w_up, ffn2_w_down)))
    m = dict(zip(WEIGHTS, (m_ffn1_norm_w, m_ffn1_w_gate, m_ffn1_w_up, m_ffn1_w_down, m_mix_norm_w, m_w_in, m_b_gates, m_q_norm_w, m_k_norm_w, m_conv_w, m_conv_b, m_dt_bias, m_a_log, m_d_skip, m_ssm_norm_w, m_w_att_proj, m_w_ssm_proj, m_w_out, m_ffn2_norm_w, m_ffn2_w_gate, m_ffn2_w_up, m_ffn2_w_down)))
    v = dict(zip(WEIGHTS, (v_ffn1_norm_w, v_ffn1_w_gate, v_ffn1_w_up, v_ffn1_w_down, v_mix_norm_w, v_w_in, v_b_gates, v_q_norm_w, v_k_norm_w, v_conv_w, v_conv_b, v_dt_bias, v_a_log, v_d_skip, v_ssm_norm_w, v_w_att_proj, v_w_ssm_proj, v_w_out, v_ffn2_norm_w, v_ffn2_w_gate, v_ffn2_w_up, v_ffn2_w_down)))
    depth = ffn1_norm_w.shape[0]
    bl = x.shape[0]
    t = bl * SEQ
    mx, my, mc = lax.axis_index("x"), lax.axis_index("y"), lax.axis_index("c")
    c_idx = mc.astype(jnp.int32).reshape(1)
    chip_idx = (2 * mx + my).astype(jnp.int32).reshape(1)

    cw_width = conv_w.shape[2]
    slots = lax.dynamic_update_slice(jnp.zeros((NCH, *conv_w.shape), F32), jnp.where(mc == 0, conv_w, 0.0)[None],
                                     (chip_idx[0], 0, 0, 0))
    conv_all = _all_sum_small("conv_gather", slots.reshape(-1, LANES)).reshape(NCH, *conv_w.shape)
    big = {"conv_w": jnp.concatenate([conv_all[k] for k in range(NCH)], axis=2)}

    mm_names = [n for n in BIG if n != "conv_w"]
    head = [n for n in mm_names if n.startswith("ffn1")]
    tail = [n for n in mm_names if n not in head]
    got_head = _all_gather_weights([w[n][:1].astype(BF16) for n in head], chip_idx)
    zero = _zero_after(got_head[0][0, 0, 0, 0])
    got_tail = _all_gather_weights_beside("all_gather_weights_beside_L0", [(w[n][:1] + zero).astype(BF16) for n in tail],
                                          chip_idx)
    first = dict(zip(head + tail, list(got_head) + list(got_tail)))
    if depth > 1:
        zero = _zero_after(got_tail[0][0, 0, 0, 0])
        rest = dict(zip(mm_names, _all_gather_weights_beside(
            "all_gather_weights_beside_rest", [(w[n][1:] + zero).astype(BF16) for n in mm_names], chip_idx)))
    for n in mm_names:
        big[n] = [(first[n], 0)] + [(rest[n], i - 1) for i in range(1, depth)]
    small = {n: w[n] for n in SMALL}

    def exchange(tag, names, buf, layer_grads, beside, after=None):
        buf = dict(buf)
        if "w_in" in names:
            buf["w_in"] = jnp.stack([_w_in_from_kernel(g["w_in"]).reshape(D_MODEL, NCH, -1).transpose(1, 0, 2)
                                     for g in layer_grads])
        mine = [buf[n] for n in names]
        if after is not None:
            mine[0] = mine[0] + _zero_after(after).astype(BF16)
        from_sibling = _pair_exchange("grad_pair_exchange_" + tag, mine)
        pairs = [_pair_sum(f"grad_pair_sum_{tag}_{n}", g, r, c_idx) for n, g, r in zip(names, mine, from_sibling)]
        recv = _chip_exchange_beside("grad_chip_exchange_beside_" + tag, pairs) if beside else _chip_exchange(pairs)
        return dict(zip(names, zip(pairs, recv)))

    c = _constants()
    layers = [_layer_params(big, small, i) for i in range(depth)]

    def exchange_rest(i, buf, layer_grads, after):
        names = mm_names if i >= 1 else tail
        out = exchange(f"L{i}", names, buf, layer_grads, True, after)
        return (None, [out[names[0]][1]], out), sum(out[n][0][0, 0, 0, 0] for n in names)

    loss_blk, dx, grads, buf0, rest_out = _local_step(
        x.reshape(t, D_MODEL), loss_target.reshape(t, D_MODEL), layers, c, exchange_rest if depth > 1 else None)
    grad_x = dx.reshape(bl, SEQ, D_MODEL)
    if depth > 1:
        last = exchange("L0_ffn1", head, buf0, [], False, rest_out[0][1][0][0, 0, 0, 0])
        layer0 = {**rest_out[0][2], **last}
    else:
        layer0 = exchange("L0", mm_names, buf0, grads[:1], False)
    halves = []
    for n in mm_names:
        acc = None
        for i in range(depth - 1, 0, -1):
            acc = _chip_sum(f"grad_chip_sum_L{i}_{n}", *rest_out[i][2][n], chip_idx, c_idx, depth, i, into=acc)
        halves.append(_chip_sum("grad_chip_sum_L0_" + n, *layer0[n], chip_idx, c_idx, depth, 0, into=acc))
    g_big = {n: r.reshape(w[n].shape) for n, r in zip(mm_names, _pair_share(halves))}

    def small_grad(n):
        if n == "q_norm_w":
            return jnp.stack([g["qkw"][0, 0, :64] for g in grads])
        if n == "k_norm_w":
            return jnp.stack([g["qkw"][1, 0, :64] for g in grads])
        return jnp.stack([g[n][0, :w[n].shape[1]] for g in grads])

    small_shapes = [w[n].shape for n in SMALL]
    conv_shape = (depth, conv_w.shape[1], NCH * cw_width)
    tot = _all_sum_small("small_all_sum", _pack_small(
        [small_grad(n) for n in SMALL] + [jnp.stack([g["conv_w"] for g in grads]), loss_blk[0, :1]]))
    unpacked = _from_flat(tot, small_shapes + [conv_shape, (1,)])
    g_small = dict(zip(SMALL, unpacked[:-2]))
    g_big["conv_w"] = lax.dynamic_slice_in_dim(unpacked[-2], chip_idx[0] * cw_width, cw_width, axis=2)
    loss = unpacked[-1][0]

    grad, delta, new_m, new_v = {}, {}, {}, {}
    for n in BIG:
        shp = w[n].shape
        two_d = (shp[0] * shp[1], shp[2])
        d_, m_, v_ = _adamw("adamw_" + n, w[n].reshape(two_d), g_big[n].reshape(two_d), m[n].reshape(two_d), v[n].reshape(two_d))
        grad[n], delta[n], new_m[n], new_v[n] = g_big[n], d_.reshape(shp), m_.reshape(shp), v_.reshape(shp)
    d_, m_, v_ = _adamw("adamw_small", _pack_small([w[n] for n in SMALL]), _pack_small([g_small[n] for n in SMALL]),
                        _pack_small([m[n] for n in SMALL]), _pack_small([v[n] for n in SMALL]))
    for n, a, b, c_ in zip(SMALL, _from_flat(d_, small_shapes), _from_flat(m_, small_shapes), _from_flat(v_, small_shapes)):
        grad[n], delta[n], new_m[n], new_v[n] = g_small[n], a, b, c_
    return (loss, grad_x, *[grad[n] for n in WEIGHTS], *[delta[n] for n in WEIGHTS],
            *[new_m[n] for n in WEIGHTS], *[new_v[n] for n in WEIGHTS])
```

```python
import functools
import math

import numpy as np
import jax
import jax.numpy as jnp
from jax import lax
from jax.experimental import pallas as pl
from jax.experimental.pallas import tpu as pltpu
from jax.experimental.pallas import tpu_sc as plsc

F32 = jnp.float32
BF16 = jnp.bfloat16
MESH = pl.DeviceIdType.MESH

D_MODEL = 1024
SEQ = 2048
DEPTH = 4
D_FF = 2816
ATT_DILATIONS = (1, 4, 16)
BAND = 128
ATT_OUT = 512
QKV = 1536
D_INNER = 2048
N_SSM_HEADS = 32
N_SSM_GROUPS = 4
D_STATE = 128
XBC = 3072
CHUNK = 128
N_IN = 11808
EPS = 1e-6
ROPE_THETA = 10000.0
NP = 12288
Q0, K0, V0, Z0, G0, X0, DT0 = 0, 1536, 3072, 4608, 6656, 8704, 11776
DTW = 128
LR, B1, B2, ADAM_EPS, WD, STEP = 0.001, 0.9, 0.999, 1e-08, 0.01, 10

LANES = 128
VMEM_LIMIT = 48 * 1024 * 1024
NEG = -1e30


def _cp(sem=None, **kw):
    return pltpu.CompilerParams(dimension_semantics=sem, vmem_limit_bytes=VMEM_LIMIT, **kw)


def _dg(a, b, ca, cb):
    return lax.dot_general(a.astype(BF16), b.astype(BF16), (((ca,), (cb,)), ((), ())), preferred_element_type=F32)


@jax.custom_vjp
def dot_nn(a, b):
    return _dg(a, b, 1, 0)


def _dot_nn_fwd(a, b):
    return _dg(a, b, 1, 0), (a, b)


def _dot_nn_bwd(r, g):
    a, b = r
    return _dg(g, b, 1, 1).astype(a.dtype), _dg(a, g, 0, 0).astype(b.dtype)


dot_nn.defvjp(_dot_nn_fwd, _dot_nn_bwd)


@jax.custom_vjp
def dot_nt(a, b):
    return _dg(a, b, 1, 1)


def _dot_nt_fwd(a, b):
    return _dg(a, b, 1, 1), (a, b)


def _dot_nt_bwd(r, g):
    a, b = r
    return _dg(g, b, 1, 0).astype(a.dtype), _dg(g, a, 0, 0).astype(b.dtype)


dot_nt.defvjp(_dot_nt_fwd, _dot_nt_bwd)


@jax.custom_vjp
def dot_tn(a, b):
    return _dg(a, b, 0, 0)


def _dot_tn_fwd(a, b):
    return _dg(a, b, 0, 0), (a, b)


def _dot_tn_bwd(r, g):
    a, b = r
    return _dg(b, g, 1, 1).astype(a.dtype), _dg(a, g, 1, 0).astype(b.dtype)


dot_tn.defvjp(_dot_tn_fwd, _dot_tn_bwd)


def _dot2_raw(a, e, ce):
    hi = a.astype(BF16)
    lo = (a - hi.astype(F32)).astype(BF16)
    return _dg(hi, e, 1, ce) + _dg(lo, e, 1, ce)


@jax.custom_vjp
def dot2(a, e):
    return _dot2_raw(a, e, 0)


def _dot2_fwd(a, e):
    return _dot2_raw(a, e, 0), e


def _dot2_bwd(e, g):
    return _dot2_raw(g, e, 1), jnp.zeros_like(e)


dot2.defvjp(_dot2_fwd, _dot2_bwd)


def _tri2_raw(l, x, cl):
    hi = x.astype(BF16)
    lo = (x - hi.astype(F32)).astype(BF16)
    return _dg(l, hi, cl, 0) + _dg(l, lo, cl, 0)


@jax.custom_vjp
def tri_matmul(l, x):
    return _tri2_raw(l, x, 1)


def _tri_fwd(l, x):
    return _tri2_raw(l, x, 1), l


def _tri_bwd(l, g):
    return jnp.zeros_like(l), _tri2_raw(l, g, 0)


tri_matmul.defvjp(_tri_fwd, _tri_bwd)


def _dup64_raw(w):
    return w + pltpu.roll(w, 64, 1)


@jax.custom_vjp
def dup64(w):
    return _dup64_raw(w)


def _dup64_fwd(w):
    return _dup64_raw(w), None


def _dup64_bwd(_, g):
    lane = lax.broadcasted_iota(jnp.int32, g.shape, 1)
    return (jnp.where(lane < 64, _dup64_raw(g), 0.0),)


dup64.defvjp(_dup64_fwd, _dup64_bwd)


def _rope_rot_raw(y, sign):
    lane = lax.broadcasted_iota(jnp.int32, y.shape, 1)
    first_half = (lane & 32) == 0
    return sign * jnp.where(first_half, -pltpu.roll(y, LANES - 32, 1), pltpu.roll(y, 32, 1))


@jax.custom_vjp
def rope_rot(y):
    return _rope_rot_raw(y, 1.0)


def _rope_rot_fwd(y):
    return _rope_rot_raw(y, 1.0), None


def _rope_rot_bwd(_, g):
    return (_rope_rot_raw(g, -1.0),)


rope_rot.defvjp(_rope_rot_fwd, _rope_rot_bwd)


def _shift_rows_raw(x, s):
    n = x.shape[0]
    r = pltpu.roll(x, s % n, 0)
    rows = lax.broadcasted_iota(jnp.int32, x.shape, 0)
    keep = rows >= s if s > 0 else rows < n + s
    return jnp.where(keep, r, 0.0)


@functools.partial(jax.custom_vjp, nondiff_argnums=(1,))
def shift_rows(x, s):
    return _shift_rows_raw(x, s)


def _shift_fwd(x, s):
    return _shift_rows_raw(x, s), None


def _shift_bwd(s, _, g):
    return (_shift_rows_raw(g, -s),)


shift_rows.defvjp(_shift_fwd, _shift_bwd)


def _sigmoid(x):
    return 1.0 / (1.0 + jnp.exp(-x))


def _silu(x):
    return x * _sigmoid(x)


def _softplus(x):
    return jnp.maximum(x, 0.0) + jnp.log(1.0 + jnp.exp(-jnp.abs(x)))


def _head_mean_mat():
    i = np.arange(LANES)
    return jnp.asarray((i[:, None] // 64 == i[None, :] // 64).astype(np.float32) / 64.0)


def _head_expand_mat():
    e = np.zeros((LANES, D_INNER), np.float32)
    for l in range(D_INNER):
        e[l // 64, l] = 1.0
    return jnp.asarray(e)


def _ltri_mat():
    i = np.arange(CHUNK)
    return jnp.asarray((i[:, None] >= i[None, :]).astype(np.float32))


def _rope_tables():
    pos = jnp.arange(SEQ, dtype=F32)
    inv_freq = 1.0 / (ROPE_THETA ** (jnp.arange(0, 64, 2, dtype=F32) / 64))
    ang = pos[:, None] * inv_freq[None, :]
    return jnp.tile(jnp.cos(ang), (1, 4)), jnp.tile(jnp.sin(ang), (1, 4))


def _pick(n, cap):
    best = None
    for t in range(LANES, min(n, cap) + 1, LANES):
        if n % t == 0:
            best = t
    return best if best is not None else n


def _mm(name, a, b, mode, out_dtype=F32, alpha=None, res=None):
    if mode == "nn":
        (m, k), n = a.shape, b.shape[1]
    elif mode == "nt":
        (m, k), n = a.shape, b.shape[0]
    else:
        (k, m), n = a.shape, b.shape[1]
    tm, tn, tk = _pick(m, 1408), _pick(n, 1408), _pick(k, 1024)
    nk = k // tk
    ca, cb = {"nn": (1, 0), "nt": (1, 1), "tn": (0, 0)}[mode]
    a_spec = pl.BlockSpec((tk, tm), lambda i, j, kk: (kk, i)) if mode == "tn" else pl.BlockSpec((tm, tk), lambda i, j, kk: (i, kk))
    b_spec = pl.BlockSpec((tn, tk), lambda i, j, kk: (j, kk)) if mode == "nt" else pl.BlockSpec((tk, tn), lambda i, j, kk: (kk, j))
    o_spec = pl.BlockSpec((tm, tn), lambda i, j, kk: (i, j))
    has_res = res is not None

    def finish(acc, res_ref, o_ref):
        if alpha is not None:
            acc = acc * alpha
        if has_res:
            acc = acc + res_ref[...].astype(F32)
        o_ref[...] = acc.astype(o_ref.dtype)

    def body(*refs):
        a_ref, b_ref = refs[0], refs[1]
        res_ref = refs[2] if has_res else None
        o_ref = refs[3] if has_res else refs[2]
        part = _dg(a_ref[...], b_ref[...], ca, cb)
        if nk == 1:
            finish(part, res_ref, o_ref)
            return
        acc_ref = refs[-1]
        kk = pl.program_id(2)

        @pl.when(kk == 0)
        def _():
            acc_ref[...] = part

        @pl.when(kk > 0)
        def _():
            acc_ref[...] += part

        @pl.when(kk == nk - 1)
        def _():
            finish(acc_ref[...], res_ref, o_ref)

    ins = [a, b] + ([res] if has_res else [])
    in_specs = [a_spec, b_spec] + ([o_spec] if has_res else [])
    return pl.pallas_call(
        body, name=name, grid=(m // tm, n // tn, nk), in_specs=in_specs, out_specs=o_spec,
        out_shape=jax.ShapeDtypeStruct((m, n), out_dtype),
        scratch_shapes=[pltpu.VMEM((tm, tn), F32)] if nk > 1 else [],
        compiler_params=_cp(("parallel", "parallel", "arbitrary")),
    )(*ins)


def _mmx(name, grid, a, b, out, contract, *, alpha=None, res=None, into=None):
    nk = grid[-1]
    has_res, has_into = res is not None, into is not None
    n_in = 2 + has_res + has_into

    def finish(acc, res_ref, o_ref):
        if alpha is not None:
            acc = acc * alpha
        if has_res:
            acc = acc + res_ref[...].astype(F32)
        o_ref[...] = acc.astype(o_ref.dtype)

    def body(*refs):
        res_ref = refs[2] if has_res else None
        o_ref = refs[n_in]
        part = _dg(refs[0][...], refs[1][...], *contract)
        if nk == 1:
            finish(part, res_ref, o_ref)
            return
        acc_ref = refs[-1]
        kk = pl.program_id(len(grid) - 1)

        @pl.when(kk == 0)
        def _():
            acc_ref[...] = part

        @pl.when(kk > 0)
        def _():
            acc_ref[...] += part

        @pl.when(kk == nk - 1)
        def _():
            finish(acc_ref[...], res_ref, o_ref)

    operands = [a, b] + ([res] if has_res else [])
    in_specs = [pl.BlockSpec(blk, im) for _, blk, im in operands] + ([ANY] if has_into else [])
    acc_shape = tuple(d for d in out[2] if d is not None)
    return pl.pallas_call(
        body, name=name, grid=grid, in_specs=in_specs, out_specs=pl.BlockSpec(out[2], out[3]),
        out_shape=jax.ShapeDtypeStruct(out[0], out[1]),
        scratch_shapes=[pltpu.VMEM(acc_shape, F32)] if nk > 1 else [],
        input_output_aliases={n_in - 1: 0} if has_into else {},
        compiler_params=_cp(("parallel",) * (len(grid) - 1) + ("arbitrary",)),
    )(*[o[0] for o in operands], *([into] if has_into else []))


def _ew(name, fn, grid, ins, outs, scratch=()):
    n_in, n_out = len(ins), len(outs)

    def body(*refs):
        vals = [r[...] for r in refs[:n_in]]
        res = fn(*vals, *refs[n_in + n_out:])
        for r, v in zip(refs[n_in:n_in + n_out], res):
            r[...] = v.astype(r.dtype)

    res = pl.pallas_call(
        body, name=name, grid=grid,
        in_specs=[pl.BlockSpec(b, m) for _, b, m in ins],
        out_specs=[pl.BlockSpec(b, m) for _, _, b, m in outs],
        out_shape=[jax.ShapeDtypeStruct(s, d) for s, d, _, _ in outs],
        scratch_shapes=list(scratch),
        compiler_params=_cp(("arbitrary",) * len(grid)),
    )(*[a for a, _, _ in ins])
    return res


def _ew_bwd(name, fn, grid, ins, cts, wrt, adds=(), ct_fn=None):
    n_in, n_ct, n_add = len(ins), len(cts), len(adds)
    idxs = [w["idx"] for w in wrt]
    intos = [(k, w["into"]) for k, w in enumerate(wrt) if w.get("into") is not None]

    def body(*refs):
        prim = [r[...] for r in refs[:n_in]]
        ct = [r[...].astype(F32) for r in refs[n_in:n_in + n_ct]]
        addv = [r[...] for r in refs[n_in + n_ct:n_in + n_ct + n_add]]
        orefs = refs[n_in + n_ct + n_add + len(intos):]

        def f(*sel):
            full = list(prim)
            for i, s in zip(idxs, sel):
                full[i] = s
            return fn(*full)

        _, vjp = jax.vjp(f, *[prim[i].astype(F32) for i in idxs])
        grads = vjp(tuple(ct) if ct_fn is None else ct_fn(*ct))
        for w, g, r in zip(wrt, grads, orefs):
            if w["kind"] == "tile":
                if w.get("add") is not None:
                    g = g + addv[w["add"]].astype(F32)
                r[...] = g.astype(r.dtype)
            else:
                first = w["first"]()

                @pl.when(first)
                def _(r=r, g=g):
                    r[...] = g.astype(r.dtype)

                @pl.when(jnp.logical_not(first))
                def _(r=r, g=g):
                    r[...] += g.astype(r.dtype)

    allin = list(ins) + list(cts) + list(adds)
    return pl.pallas_call(
        body, name=name, grid=grid,
        in_specs=[pl.BlockSpec(b, m) for _, b, m in allin] + [ANY] * len(intos),
        out_specs=[pl.BlockSpec(w["block"], w["imap"]) for w in wrt],
        out_shape=[jax.ShapeDtypeStruct(w["shape"], w["dtype"]) for w in wrt],
        input_output_aliases={len(allin) + q: k for q, (k, _) in enumerate(intos)},
        compiler_params=_cp(("arbitrary",) * len(grid)),
    )(*[a for a, _, _ in allin], *[a for _, a in intos])


def _rmsnorm_fn(x, w):
    return (x * lax.rsqrt(jnp.mean(x * x, axis=-1, keepdims=True) + EPS) * w,)


def _swiglu_fn(g, u):
    return (_silu(g) * u,)


def _qkprep_fn(t, w64, cos, sin, hmean):
    w = jnp.sum(dup64(jnp.broadcast_to(w64, (8, LANES))), axis=0, keepdims=True) * 0.125
    y = t * lax.rsqrt(dot2(t * t, hmean) + EPS) * w
    return (y * cos + rope_rot(y) * sin,)


def _att_fn(q, kp, kc, vp, vc, first):
    iq = lax.broadcasted_iota(jnp.int32, (BAND, 2 * BAND), 0)
    ik = lax.broadcasted_iota(jnp.int32, (BAND, 2 * BAND), 1)
    rel = BAND + iq - ik
    ok = (rel >= 0) & (rel <= BAND) & ((ik >= BAND) | jnp.logical_not(first))
    lane = lax.broadcasted_iota(jnp.int32, (1, LANES), 1)
    kcat = jnp.concatenate([kp, kc], axis=0)
    vcat = jnp.concatenate([vp, vc], axis=0)
    o_pair = jnp.zeros((BAND, LANES), F32)
    l_pair = jnp.zeros((BAND, LANES), F32)
    for hh in range(2):
        lm = (lane // 64 == hh).astype(F32)
        s = dot_nt(q * lm, kcat) * 0.125
        s = jnp.where(ok, s, NEG)
        mx = jnp.max(s, axis=-1, keepdims=True)
        e = jnp.exp(s - mx)
        den = jnp.sum(e, axis=-1, keepdims=True)
        o_pair = o_pair + dot_nn(e / den, vcat) * lm
        l_pair = l_pair + (mx + jnp.log(den)) * lm
    return o_pair, l_pair


def _attmix_fn(o0, o1, o2, l0, l1, l2):
    m = jnp.maximum(jnp.maximum(l0, l1), l2)
    e0, e1, e2 = jnp.exp(l0 - m), jnp.exp(l1 - m), jnp.exp(l2 - m)
    return ((e0 * o0 + e1 * o1 + e2 * o2) / (e0 + e1 + e2),)


def _conv_fn(x, w0, w1, w2, w3, b):
    pre = x * w3 + shift_rows(x, 1) * w2 + shift_rows(x, 2) * w1 + shift_rows(x, 3) * w0 + b
    return (_silu(pre),)


def _ssdpre_fn(dtraw, bias, alog, ex):
    dt = _softplus(dtraw + bias)
    da = dt * (-jnp.exp(alog))
    return dot2(dt, ex), dot2(da, ex)


def _ssd_step(st, x, dtb, dab, bm, cm, ltri):
    cum = tri_matmul(ltri, dab)
    cum_t = cum.T
    xdt = x * dtb
    cb = dot_nt(cm, bm)
    ri = lax.broadcasted_iota(jnp.int32, (CHUNK, CHUNK), 0)
    ci = lax.broadcasted_iota(jnp.int32, (CHUNK, CHUNK), 1)
    causal = ri >= ci
    lane = lax.broadcasted_iota(jnp.int32, (1, LANES), 1)
    rowi = lax.broadcasted_iota(jnp.int32, (LANES, 1), 0)
    ys = []
    for p in range(4):
        sl = slice(p * LANES, (p + 1) * LANES)
        cum_p, cum_tp, xdt_p = cum[:, sl], cum_t[sl, :], xdt[:, sl]
        acc = jnp.zeros((CHUNK, LANES), F32)
        for hh in range(2):
            col = jnp.sum(cum_p * (lane == 64 * hh).astype(F32), axis=1, keepdims=True)
            row = jnp.sum(cum_tp * (rowi == 64 * hh).astype(F32), axis=0, keepdims=True)
            dec = jnp.exp(jnp.where(causal, col - row, NEG))
            acc = acc + dot_nn(cb * dec, xdt_p * (lane // 64 == hh).astype(F32))
        ys.append(acc)
    y_diag = jnp.concatenate(ys, axis=1)
    y_off = dot_nn(cm, st) * jnp.exp(cum)
    last_row = (lax.broadcasted_iota(jnp.int32, (CHUNK, 1), 0) == CHUNK - 1).astype(F32)
    last = jnp.sum(cum * last_row, axis=0, keepdims=True)
    new_st = st * jnp.exp(last) + dot_tn(bm, xdt * jnp.exp(last - cum))
    return new_st, y_diag + y_off


def _ssdpost_fn(y, xs, z, dskip, ex, nw):
    db = jnp.sum(dot2(jnp.broadcast_to(dskip, (8, LANES)), ex), axis=0, keepdims=True) * 0.125
    y2 = (y + db * xs) * _silu(z)
    return (y2 * lax.rsqrt(jnp.mean(y2 * y2, axis=-1, keepdims=True) + EPS) * nw,)


def _merge_fn(ya, ys, ga, gs, ba, bs):
    return (_sigmoid(ga + ba) * ya + _sigmoid(gs + bs) * ys,)


TM = 512


def _full(shape):
    nd = len(shape)
    return (shape, lambda *_: (0,) * nd)


def _rmsnorm(name, x, w):
    t = x.shape[0]
    return _ew(name, _rmsnorm_fn, (t // TM,),
               [(x, (TM, D_MODEL), lambda i: (i, 0)), (w, (1, D_MODEL), lambda i: (0, 0))],
               [((t, D_MODEL), BF16, (TM, D_MODEL), lambda i: (i, 0))])[0]


def _rmsnorm_bwd(name, x, w, dh, dres):
    t = x.shape[0]
    row = ((TM, D_MODEL), lambda i: (i, 0))
    return _ew_bwd(name, _rmsnorm_fn, (t // TM,),
                   [(x, *row), (w, (1, D_MODEL), lambda i: (0, 0))], [(dh, *row)],
                   [dict(idx=0, kind="tile", shape=(t, D_MODEL), dtype=F32, block=row[0], imap=row[1], add=0),
                    dict(idx=1, kind="acc", shape=(1, D_MODEL), dtype=F32, block=(1, D_MODEL), imap=lambda i: (0, 0),
                         first=lambda: pl.program_id(0) == 0)],
                   adds=[(dres, *row)])


def _qk_operands(proj, qkw, cos, sin, consts, tm):
    nrow = SEQ // tm
    c = ((LANES, LANES), lambda j, i: (0, 0))
    return [(proj, (tm, LANES), lambda j, i: (i, j)),
            (qkw, (None, 1, LANES), lambda j, i: (j // 12, 0, 0)),
            (cos, (tm, LANES), lambda j, i: (i % nrow, 0)),
            (sin, (tm, LANES), lambda j, i: (i % nrow, 0)),
            (consts["hmean"], *c)]


QK_TM = 1024


def _qkprep(name, proj, qkw, cos, sin, consts):
    t = proj.shape[0]
    return _ew(name, _qkprep_fn, (2 * QKV // LANES, t // QK_TM), _qk_operands(proj, qkw, cos, sin, consts, QK_TM),
               [((t, 2 * QKV), F32, (QK_TM, LANES), lambda j, i: (i, j))])[0]


def _qkprep_bwd(name, proj, qkw, cos, sin, consts, dq, dk, dproj):
    t = proj.shape[0]
    nq = QKV // LANES

    def pick(cq, ck):
        return (jnp.where(pl.program_id(0) < nq, cq, ck),)

    return _ew_bwd(name, _qkprep_fn, (2 * nq, t // QK_TM), _qk_operands(proj, qkw, cos, sin, consts, QK_TM),
                   [(d, (QK_TM, LANES), lambda j, i: (i, j % nq)) for d in (dq, dk)],
                   [dict(idx=0, kind="tile", shape=dproj.shape, dtype=dproj.dtype, block=(QK_TM, LANES),
                         imap=lambda j, i: (i, j), into=dproj),
                    dict(idx=1, kind="acc", shape=(2, 1, LANES), dtype=F32, block=(None, 1, LANES),
                         imap=lambda j, i: (j // 12, 0, 0),
                         first=lambda: (pl.program_id(0) % 12 == 0) & (pl.program_id(1) == 0))],
                   ct_fn=pick)


def _att_specs(dil, g):
    nb = SEQ // dil // BAND
    pt = 4 if dil == 1 else 1
    w = pt * LANES
    blk = (None, BAND * dil, w)
    kq, kk, kv = g * ATT_OUT // w, (QKV + g * ATT_OUT) // w, (V0 + g * ATT_OUT) // w

    def cur(n):
        return jnp.minimum(n, nb - 1)

    def prev(n):
        return jnp.maximum(jnp.minimum(n, nb - 1) - 1, 0)

    return nb, pt, blk, [
        pl.BlockSpec(blk, lambda b, p, n: (b, cur(n), kq + p)),
        pl.BlockSpec(blk, lambda b, p, n: (b, prev(n), kk + p)),
        pl.BlockSpec(blk, lambda b, p, n: (b, cur(n), kk + p)),
        pl.BlockSpec(blk, lambda b, p, n: (b, prev(n), kv + p)),
        pl.BlockSpec(blk, lambda b, p, n: (b, cur(n), kv + p)),
    ]


def _att_rows(r, dil):
    return pl.ds(r, BAND, stride=dil) if dil > 1 else pl.ds(0, BAND)


def _att_fwd(name, qk, proj, g):
    bl = qk.shape[0] // SEQ
    dil = ATT_DILATIONS[g]
    nb, pt, blk, specs = _att_specs(dil, g)
    qk3 = qk.reshape(bl, SEQ, 2 * QKV)
    proj3 = proj.reshape(bl, SEQ, NP)
    o_spec = pl.BlockSpec(blk, lambda b, p, n: (b, n, p))

    def body(q, kp, kc, vp, vc, o_ref, l_ref):
        first = pl.program_id(2) == 0

        def residue(r, carry):
            sl = _att_rows(r, dil)
            for p in range(pt):
                ln = pl.ds(p * LANES, LANES)
                o, l = _att_fn(q[sl, ln], kp[sl, ln], kc[sl, ln], vp[sl, ln], vc[sl, ln], first)
                o_ref[sl, ln] = o
                l_ref[sl, ln] = l
            return carry

        lax.fori_loop(0, dil, residue, 0)

    o, l = pl.pallas_call(
        body, name=name, grid=(bl, ATT_OUT // (pt * LANES), nb), in_specs=specs, out_specs=[o_spec, o_spec],
        out_shape=[jax.ShapeDtypeStruct((bl, SEQ, ATT_OUT), F32)] * 2,
        compiler_params=_cp(("arbitrary",) * 3),
    )(qk3, qk3, qk3, proj3, proj3)
    return o.reshape(bl * SEQ, ATT_OUT), l.reshape(bl * SEQ, ATT_OUT)


def _att_bwd(name, qk, proj, g, do, dl, dq_buf, dk_buf, dv_buf):
    bl = qk.shape[0] // SEQ
    dil = ATT_DILATIONS[g]
    nb, pt, blk, specs = _att_specs(dil, g)
    w = pt * LANES
    qk3 = qk.reshape(bl, SEQ, 2 * QKV)
    proj3 = proj.reshape(bl, SEQ, NP)
    ct_spec = pl.BlockSpec(blk, lambda b, p, n: (b, jnp.minimum(n, nb - 1), p))
    do3 = do.reshape(bl, SEQ, ATT_OUT)
    dl3 = dl.reshape(bl, SEQ, ATT_OUT)
    kg = g * ATT_OUT // w

    def body(q, kp, kc, vp, vc, do_ref, dl_ref, _a, _b, _c, d_ref, dk_ref, dv_ref, ck, cv):
        n = pl.program_id(2)

        def residue(r, carry):
            sl = _att_rows(r, dil)
            for p in range(pt):
                ln = pl.ds(p * LANES, LANES)

                @pl.when(n < nb)
                def _(ln=ln):
                    first = n == 0
                    prim = [ref[sl, ln] for ref in (q, kp, kc, vp, vc)]
                    _, vjp = jax.vjp(lambda *a: _att_fn(*a, first), *prim)
                    dq, dkp, dkc, dvp, dvc = vjp((do_ref[sl, ln], dl_ref[sl, ln]))
                    d_ref[sl, ln] = dq

                    @pl.when(n > 0)
                    def _():
                        dk_ref[sl, ln] = ck[sl, ln] + dkp
                        dv_ref[sl, ln] = cv[sl, ln] + dvp

                    ck[sl, ln] = dkc
                    cv[sl, ln] = dvc

                @pl.when(n == nb)
                def _(ln=ln):
                    dk_ref[sl, ln] = ck[sl, ln]
                    dv_ref[sl, ln] = cv[sl, ln]

            return carry

        lax.fori_loop(0, dil, residue, 0)

    bufs = [a.reshape(bl, SEQ, QKV) for a in (dq_buf, dk_buf, dv_buf)]
    o_specs = [
        pl.BlockSpec(blk, lambda b, p, n: (b, jnp.minimum(n, nb - 1), kg + p)),
        pl.BlockSpec(blk, lambda b, p, n: (b, jnp.maximum(n - 1, 0), kg + p)),
        pl.BlockSpec(blk, lambda b, p, n: (b, jnp.maximum(n - 1, 0), kg + p)),
    ]
    dq, dk, dv = pl.pallas_call(
        body, name=name, grid=(bl, ATT_OUT // w, nb + 1), in_specs=specs + [ct_spec, ct_spec, ANY, ANY, ANY],
        out_specs=o_specs, out_shape=[jax.ShapeDtypeStruct(a.shape, a.dtype) for a in bufs],
        input_output_aliases={7: 0, 8: 1, 9: 2},
        scratch_shapes=[pltpu.VMEM((BAND * dil, w), F32), pltpu.VMEM((BAND * dil, w), F32)],
        compiler_params=_cp(("arbitrary",) * 3),
    )(qk3, qk3, qk3, proj3, proj3, do3, dl3, *bufs)
    return dq.reshape(dq_buf.shape), dk.reshape(dk_buf.shape), dv.reshape(dv_buf.shape)


def _attmix(name, os_, ls_):
    t = os_[0].shape[0]
    blk = ((TM, ATT_OUT), lambda i: (i, 0))
    return _ew(name, _attmix_fn, (t // TM,), [(a, *blk) for a in (*os_, *ls_)], [((t, ATT_OUT), BF16, *blk)])[0]


def _attmix_bwd(name, os_, ls_, datt):
    t = os_[0].shape[0]
    blk = ((TM, ATT_OUT), lambda i: (i, 0))
    return _ew_bwd(name, _attmix_fn, (t // TM,), [(a, *blk) for a in (*os_, *ls_)], [(datt, *blk)],
                   [dict(idx=k, kind="tile", shape=(t, ATT_OUT), dtype=F32, block=blk[0], imap=blk[1]) for k in range(6)])


CONV_TC = 256


def _conv_operands(proj3, conv_w, conv_b):
    c0 = X0 // CONV_TC
    ins = [(proj3, (None, SEQ, CONV_TC), lambda j, b: (b, 0, c0 + j))]
    for k in range(4):
        ins.append((conv_w, (None, 1, CONV_TC), lambda j, b, k=k: (k, 0, j)))
    ins.append((conv_b, (1, CONV_TC), lambda j, b: (0, j)))
    return ins


def _conv(name, proj3, conv_w, conv_b):
    bl = proj3.shape[0]
    return _ew(name, _conv_fn, (XBC // CONV_TC, bl), _conv_operands(proj3, conv_w, conv_b),
               [((bl, SEQ, XBC), F32, (None, SEQ, CONV_TC), lambda j, b: (b, 0, j))])[0]


def _conv_bwd(name, proj3, conv_w, conv_b, dxs3, db3, dc3, dproj3):
    bl = proj3.shape[0]
    nx = D_INNER // CONV_TC
    nb_ = N_SSM_GROUPS * D_STATE // CONV_TC
    blk = (None, SEQ, CONV_TC)
    cts = [(dxs3, blk, lambda j, b: (b, 0, jnp.minimum(j, nx - 1))),
           (db3, blk, lambda j, b: (b, 0, jnp.clip(j - nx, 0, nb_ - 1))),
           (dc3, blk, lambda j, b: (b, 0, jnp.clip(j - nx - nb_, 0, nb_ - 1)))]

    def pick(cx, cb, cc):
        j = pl.program_id(0)
        return (jnp.where(j < nx, cx, jnp.where(j < nx + nb_, cb, cc)),)

    first = lambda: pl.program_id(1) == 0
    wrt = [dict(idx=0, kind="tile", shape=dproj3.shape, dtype=dproj3.dtype, block=blk,
                imap=lambda j, b: (b, 0, X0 // CONV_TC + j), into=dproj3)]
    for k in range(4):
        wrt.append(dict(idx=1 + k, kind="acc", shape=(1, XBC), dtype=F32, block=(1, CONV_TC),
                        imap=lambda j, b: (0, j), first=first))
    wrt.append(dict(idx=5, kind="acc", shape=(1, XBC), dtype=F32, block=(1, CONV_TC), imap=lambda j, b: (0, j), first=first))
    return _ew_bwd(name, _conv_fn, (XBC // CONV_TC, bl), _conv_operands(proj3, conv_w, conv_b), cts, wrt, ct_fn=pick)


SSD_TM = 256
POST_TM = 512


def _ssdpre_operands(proj, dt_bias, a_log, ex):
    return [(proj, (SSD_TM, DTW), lambda i: (i, DT0 // DTW)), (dt_bias, *_full((1, DTW))), (a_log, *_full((1, DTW))),
            (ex, *_full((LANES, D_INNER)))]


def _ssdpre(name, proj, dt_bias, a_log, ex):
    t = proj.shape[0]
    blk = ((SSD_TM, D_INNER), lambda i: (i, 0))
    return _ew(name, _ssdpre_fn, (t // SSD_TM,), _ssdpre_operands(proj, dt_bias, a_log, ex),
               [((t, D_INNER), F32, *blk), ((t, D_INNER), F32, *blk)])


def _ssdpre_bwd(name, proj, dt_bias, a_log, ex, ddtb, ddab):
    t = proj.shape[0]
    blk = ((SSD_TM, D_INNER), lambda i: (i, 0))
    first = lambda: pl.program_id(0) == 0
    return _ew_bwd(name, _ssdpre_fn, (t // SSD_TM,), _ssdpre_operands(proj, dt_bias, a_log, ex),
                   [(ddtb, *blk), (ddab, *blk)],
                   [dict(idx=0, kind="tile", shape=(t, DTW), dtype=BF16, block=(SSD_TM, DTW), imap=lambda i: (i, 0)),
                    dict(idx=1, kind="acc", shape=(1, DTW), dtype=F32, block=(1, DTW), imap=lambda i: (0, 0), first=first),
                    dict(idx=2, kind="acc", shape=(1, DTW), dtype=F32, block=(1, DTW), imap=lambda i: (0, 0), first=first)])


def _ssd_in_specs(rev):
    nc = SEQ // CHUNK

    def c_(c):
        return nc - 1 - c if rev else c

    wide = (None, CHUNK, D_INNER)
    nar = (None, CHUNK, N_SSM_GROUPS * D_STATE)
    nb_ = D_INNER // (N_SSM_GROUPS * D_STATE)
    return [
        pl.BlockSpec(wide, lambda b, c: (b, c_(c), 0)),
        pl.BlockSpec(wide, lambda b, c: (b, c_(c), 0)),
        pl.BlockSpec(wide, lambda b, c: (b, c_(c), 0)),
        pl.BlockSpec(nar, lambda b, c: (b, c_(c), nb_)),
        pl.BlockSpec(nar, lambda b, c: (b, c_(c), nb_ + 1)),
        pl.BlockSpec((CHUNK, CHUNK), lambda b, c: (0, 0)),
    ], c_


def _ssd_cols(g):
    return pl.ds(g * 4 * LANES, 4 * LANES), pl.ds(g * D_STATE, D_STATE)


def _ssd_fwd(name, xc3, dtb3, dab3, ltri):
    bl = xc3.shape[0]
    nc = SEQ // CHUNK
    specs, _ = _ssd_in_specs(False)

    def body(x, dtb, dab, bm, cm, lt, y_ref, st_ref, st):
        @pl.when(pl.program_id(1) == 0)
        def _():
            st[...] = jnp.zeros_like(st)

        ltv = lt[...]
        for g in range(N_SSM_GROUPS):
            wl, nl = _ssd_cols(g)
            s0 = st[g]
            st_ref[g] = s0
            new_st, y = _ssd_step(s0, x[:, wl], dtb[:, wl], dab[:, wl], bm[:, nl], cm[:, nl], ltv)
            y_ref[:, wl] = y
            st[g] = new_st

    return pl.pallas_call(
        body, name=name, grid=(bl, nc), in_specs=specs,
        out_specs=[pl.BlockSpec((None, CHUNK, D_INNER), lambda b, c: (b, c, 0)),
                   pl.BlockSpec((None, N_SSM_GROUPS, None, D_STATE, 4 * LANES), lambda b, c: (b, 0, c, 0, 0))],
        out_shape=[jax.ShapeDtypeStruct((bl, SEQ, D_INNER), F32),
                   jax.ShapeDtypeStruct((bl, N_SSM_GROUPS, nc, D_STATE, 4 * LANES), F32)],
        scratch_shapes=[pltpu.VMEM((N_SSM_GROUPS, D_STATE, 4 * LANES), F32)],
        compiler_params=_cp(("arbitrary",) * 2),
    )(xc3, dtb3, dab3, xc3, xc3, ltri)


def _ssd_bwd(name, xc3, dtb3, dab3, ltri, states, dy3, dxs_part3):
    bl = xc3.shape[0]
    nc = SEQ // CHUNK
    specs, c_ = _ssd_in_specs(True)
    wide = pl.BlockSpec((None, CHUNK, D_INNER), lambda b, c: (b, c_(c), 0))
    nar = pl.BlockSpec((None, CHUNK, N_SSM_GROUPS * D_STATE), lambda b, c: (b, c_(c), 0))
    st_spec = pl.BlockSpec((None, N_SSM_GROUPS, None, D_STATE, 4 * LANES), lambda b, c: (b, 0, c_(c), 0, 0))

    def body(x, dtb, dab, bm, cm, lt, st_ref, dy, dxp, dx_ref, ddtb_ref, ddab_ref, dbm_ref, dcm_ref, dst):
        @pl.when(pl.program_id(1) == 0)
        def _():
            dst[...] = jnp.zeros_like(dst)

        ltv = lt[...]
        for g in range(N_SSM_GROUPS):
            wl, nl = _ssd_cols(g)
            _, vjp = jax.vjp(lambda *a: _ssd_step(*a, ltv), st_ref[g], x[:, wl], dtb[:, wl], dab[:, wl], bm[:, nl], cm[:, nl])
            d_st, d_x, d_dtb, d_dab, d_bm, d_cm = vjp((dst[g], dy[:, wl]))
            dst[g] = d_st
            dx_ref[:, wl] = d_x + dxp[:, wl]
            ddtb_ref[:, wl] = d_dtb
            ddab_ref[:, wl] = d_dab
            dbm_ref[:, nl] = d_bm
            dcm_ref[:, nl] = d_cm

    big = jax.ShapeDtypeStruct((bl, SEQ, D_INNER), F32)
    small = jax.ShapeDtypeStruct((bl, SEQ, N_SSM_GROUPS * D_STATE), F32)
    return pl.pallas_call(
        body, name=name, grid=(bl, nc), in_specs=specs + [st_spec, wide, wide],
        out_specs=[wide, wide, wide, nar, nar], out_shape=[big, big, big, small, small],
        scratch_shapes=[pltpu.VMEM((N_SSM_GROUPS, D_STATE, 4 * LANES), F32)],
        compiler_params=_cp(("arbitrary",) * 2),
    )(xc3, dtb3, dab3, xc3, xc3, ltri, states, dy3, dxs_part3)


def _ssdpost_operands(y, xc, proj, d_skip, ex, nw):
    w = 4 * LANES
    return [(y, (POST_TM, w), lambda j, i: (i, j)), (xc, (POST_TM, w), lambda j, i: (i, j)),
            (proj, (POST_TM, w), lambda j, i: (i, Z0 // w + j)), (d_skip, (1, DTW), lambda j, i: (0, 0)),
            (ex, (LANES, w), lambda j, i: (0, j)), (nw, (1, w), lambda j, i: (0, j))]


def _ssdpost(name, y, xc, proj, d_skip, ex, nw):
    t = y.shape[0]
    w = 4 * LANES
    return _ew(name, _ssdpost_fn, (D_INNER // w, t // POST_TM), _ssdpost_operands(y, xc, proj, d_skip, ex, nw),
               [((t, D_INNER), BF16, (POST_TM, w), lambda j, i: (i, j))])[0]


def _ssdpost_bwd(name, y, xc, proj, d_skip, ex, nw, dysn, dproj):
    t = y.shape[0]
    w = 4 * LANES
    blk = ((POST_TM, w), lambda j, i: (i, j))
    return _ew_bwd(name, _ssdpost_fn, (D_INNER // w, t // POST_TM), _ssdpost_operands(y, xc, proj, d_skip, ex, nw),
                   [(dysn, *blk)],
                   [dict(idx=0, kind="tile", shape=(t, D_INNER), dtype=F32, block=blk[0], imap=blk[1]),
                    dict(idx=1, kind="tile", shape=(t, D_INNER), dtype=F32, block=blk[0], imap=blk[1]),
                    dict(idx=2, kind="tile", shape=dproj.shape, dtype=dproj.dtype, block=blk[0],
                         imap=lambda j, i: (i, Z0 // w + j), into=dproj),
                    dict(idx=3, kind="acc", shape=(1, DTW), dtype=F32, block=(1, DTW), imap=lambda j, i: (0, 0),
                         first=lambda: (pl.program_id(0) == 0) & (pl.program_id(1) == 0)),
                    dict(idx=5, kind="acc", shape=(1, D_INNER), dtype=F32, block=(1, w), imap=lambda j, i: (0, j),
                         first=lambda: pl.program_id(1) == 0)])


def _merge_operands(ya, ys, proj, b_gates):
    w = 4 * LANES
    g0 = G0 // w
    nh = D_MODEL // w
    return [(ya, (TM, w), lambda j, i: (i, j)), (ys, (TM, w), lambda j, i: (i, j)),
            (proj, (TM, w), lambda j, i: (i, g0 + j)), (proj, (TM, w), lambda j, i: (i, g0 + nh + j)),
            (b_gates, (1, w), lambda j, i: (0, j)), (b_gates, (1, w), lambda j, i: (0, nh + j))]


def _merge(name, ya, ys, proj, b_gates):
    t = ya.shape[0]
    w = 4 * LANES
    return _ew(name, _merge_fn, (D_MODEL // w, t // TM), _merge_operands(ya, ys, proj, b_gates),
               [((t, D_MODEL), BF16, (TM, w), lambda j, i: (i, j))])[0]


def _merge_bwd(name, ya, ys, proj, b_gates, dmixed):
    t = ya.shape[0]
    w = 4 * LANES
    blk = ((TM, w), lambda j, i: (i, j))
    first = lambda: pl.program_id(1) == 0
    tile = lambda k, dt: dict(idx=k, kind="tile", shape=(t, D_MODEL), dtype=dt, block=blk[0], imap=blk[1])
    acc = lambda k: dict(idx=k, kind="acc", shape=(1, D_MODEL), dtype=F32, block=(1, w), imap=lambda j, i: (0, j), first=first)
    return _ew_bwd(name, _merge_fn, (D_MODEL // w, t // TM), _merge_operands(ya, ys, proj, b_gates), [(dmixed, *blk)],
                   [tile(0, BF16), tile(1, BF16), tile(2, BF16), tile(3, BF16), acc(4), acc(5)])


def _loss(name, y, tgt):
    t = y.shape[0]
    blk = pl.BlockSpec((TM, D_MODEL), lambda i: (i, 0))

    def body(y_ref, t_ref, dy_ref, l_ref):
        e = y_ref[...] - t_ref[...]
        dy_ref[...] = e * (1.0 / D_MODEL)
        part = jnp.sum(jnp.sum(e * e, axis=-1, keepdims=True), axis=0, keepdims=True) * (0.5 / D_MODEL)
        part = jnp.broadcast_to(part, (8, LANES))

        @pl.when(pl.program_id(0) == 0)
        def _():
            l_ref[...] = part

        @pl.when(pl.program_id(0) > 0)
        def _():
            l_ref[...] += part

    return pl.pallas_call(
        body, name=name, grid=(t // TM,), in_specs=[blk, blk],
        out_specs=[blk, pl.BlockSpec((8, LANES), lambda i: (0, 0))],
        out_shape=[jax.ShapeDtypeStruct((t, D_MODEL), F32), jax.ShapeDtypeStruct((8, LANES), F32)],
        compiler_params=_cp(("arbitrary",)),
    )(y, tgt)


def _adamw_fn(w, g, m, v):
    m2 = B1 * m + (1.0 - B1) * g
    v2 = B2 * v + (1.0 - B2) * (g * g)
    m_hat = m2 / (1.0 - B1 ** STEP)
    v_hat = v2 / (1.0 - B2 ** STEP)
    return -LR * (m_hat / (jnp.sqrt(v_hat) + ADAM_EPS) + WD * w), m2, v2


def _adamw(name, w, g, m, v):
    rows, cols = w.shape
    tm = rows
    for cand in (512, 256, 128, 64, 32, 16, 8):
        if rows % cand == 0 and cand * cols * 4 <= (1 << 21):
            tm = cand
            break
    blk = ((tm, cols), lambda i: (i, 0))
    return _ew(name, _adamw_fn, (rows // tm,), [(a, *blk) for a in (w, g, m, v)], [((rows, cols), F32, *blk)] * 3)


NCH = 4
TMM = 1024
TKK = 1024


def _ffn_fwd(tag, x, nw, wg, wu, wd, li):
    t, fc = x.shape[0], wg.shape[-1]
    h = _rmsnorm(tag + "_norm", x, nw)

    def up_body(h_ref, wg_ref, wu_ref, g_ref, u_ref, a_ref):
        hv = h_ref[...]
        g = _dg(hv, wg_ref[...], 1, 0).astype(BF16)
        u = _dg(hv, wu_ref[...], 1, 0).astype(BF16)
        g_ref[...] = g
        u_ref[...] = u
        a_ref[...] = _swiglu_fn(g.astype(F32), u.astype(F32))[0].astype(BF16)

    w_spec = pl.BlockSpec((None, None, D_MODEL, fc), lambda k, i: (k, li, 0, 0))
    o_spec = pl.BlockSpec((None, TMM, fc), lambda k, i: (k, i, 0))
    g, u, a = pl.pallas_call(
        up_body, name=tag + "_up_act", grid=(NCH, t // TMM),
        in_specs=[pl.BlockSpec((TMM, D_MODEL), lambda k, i: (i, 0)), w_spec, w_spec], out_specs=[o_spec] * 3,
        out_shape=[jax.ShapeDtypeStruct((NCH, t, fc), BF16)] * 3, compiler_params=_cp(("parallel", "parallel")),
    )(h, wg, wu)
    row = ((TMM, D_MODEL), lambda i, j, k: (i, 0))
    y = _mmx(tag + "_down", (t // TMM, 1, NCH),
             (a, (None, TMM, fc), lambda i, j, k: (k, i, 0)),
             (wd, (None, None, fc, D_MODEL), lambda i, j, k: (k, li, 0, 0)),
             ((t, D_MODEL), F32, *row), (1, 0), alpha=0.5, res=(x, *row))
    return y, (x, h, g, u, a)


def _ffn_bwd(tag, saved, nw, wg, wu, wd, li, dy, bufs, gl):
    x, h, g, u, a = saved
    t, fc = x.shape[0], wg.shape[-1]
    bg, bu, bd = bufs
    def dact_body(dy_ref, wd_ref, g_ref, u_ref, dg_ref, du_ref):
        da = _dg(dy_ref[...], wd_ref[...], 1, 1) * 0.5
        _, vjp = jax.vjp(_swiglu_fn, g_ref[...].astype(F32), u_ref[...].astype(F32))
        dg, du = vjp((da,))
        dg_ref[...] = dg.astype(BF16)
        du_ref[...] = du.astype(BF16)

    c_spec = pl.BlockSpec((None, TMM, fc), lambda k, i: (k, i, 0))
    dg, du = pl.pallas_call(
        dact_body, name=tag + "_down_dx_act", grid=(NCH, t // TMM),
        in_specs=[pl.BlockSpec((TMM, D_MODEL), lambda k, i: (i, 0)),
                  pl.BlockSpec((None, None, fc, D_MODEL), lambda k, i: (k, li, 0, 0)), c_spec, c_spec],
        out_specs=[c_spec] * 2, out_shape=[jax.ShapeDtypeStruct((NCH, t, fc), BF16)] * 2,
        compiler_params=_cp(("parallel", "parallel")),
    )(dy, wd, g, u)
    bd = _mmx(tag + "_down_dw", (NCH, 1, t // TKK),
              (a, (None, TKK, fc), lambda k, j, kk: (k, kk, 0)),
              (dy, (TKK, D_MODEL), lambda k, j, kk: (kk, 0)),
              (bd.shape, BF16, (None, None, fc, D_MODEL), lambda k, j, kk: (gl, k, 0, 0)), (0, 0), alpha=0.5, into=bd)
    def dw(name, d, buf):
        return _mmx(name, (NCH, 1, t // TKK),
                    (h, (TKK, D_MODEL), lambda k, i, kk: (kk, 0)),
                    (d, (None, TKK, fc), lambda k, i, kk: (k, kk, 0)),
                    (buf.shape, BF16, (None, None, D_MODEL, fc), lambda k, i, kk: (gl, k, 0, 0)), (0, 0), into=buf)

    bg, bu = dw(tag + "_gate_dw", dg, bg), dw(tag + "_up_dw", du, bu)
    row = ((TMM, D_MODEL), lambda i, j, k: (i, 0))

    def dx_(name, d, w, res):
        return _mmx(name, (t // TMM, 1, NCH),
                    (d, (None, TMM, fc), lambda i, j, k: (k, i, 0)),
                    (w, (None, None, D_MODEL, fc), lambda i, j, k: (k, li, 0, 0)),
                    ((t, D_MODEL), F32, *row), (1, 1), res=None if res is None else (res, *row))

    dh = dx_(tag + "_up_dx", du, wu, dx_(tag + "_gate_dx", dg, wg, None))
    dx, dnw = _rmsnorm_bwd(tag + "_norm_bwd", x, nw, dh, dy)
    return dx, dnw, (bg, bu, bd)


def _mixer_fwd(tag, x, p, c):
    t = x.shape[0]
    bl = t // SEQ
    h = _rmsnorm(tag + "_norm", x, p["mix_norm_w"])
    proj = _mm(tag + "_in", h, p["w_in"], "nn")
    qk = _qkprep(tag + "_qk", proj, p["qkw"], c["cos"], c["sin"], c)
    os_, ls_ = [], []
    for g in range(3):
        o, l = _att_fwd(f"{tag}_att{g}", qk, proj, g)
        os_.append(o)
        ls_.append(l)
    att = _attmix(tag + "_attmix", os_, ls_)
    li = p["layer"]
    wa, ws, wo = p["w_att_proj"], p["w_ssm_proj"], p["w_out"]
    ca, cs, co = wa.shape[-1], ws.shape[-2], wo.shape[-2]
    row = ((TMM, D_MODEL), lambda i, j, k: (i, 0))
    ya = _mmx(tag + "_attproj", (t // TMM, NCH, 1),
              (att, (TMM, ATT_OUT), lambda i, k, kk: (i, 0)),
              (wa, (None, None, ATT_OUT, ca), lambda i, k, kk: (k, li, 0, 0)),
              ((t, D_MODEL), F32, (TMM, ca), lambda i, k, kk: (i, k)), (1, 0))
    proj3 = proj.reshape(bl, SEQ, NP)
    xc3 = _conv(tag + "_conv", proj3, p["conv_w"], p["conv_b"])
    xc = xc3.reshape(t, XBC)
    dtb, dab = _ssdpre(tag + "_ssdpre", proj, p["dt_bias"], p["a_log"], c["ex"])
    dtb3, dab3 = dtb.reshape(bl, SEQ, D_INNER), dab.reshape(bl, SEQ, D_INNER)
    y3, states = _ssd_fwd(tag + "_ssd", xc3, dtb3, dab3, c["ltri"])
    y = y3.reshape(t, D_INNER)
    ysn = _ssdpost(tag + "_ssdpost", y, xc, proj, p["d_skip"], c["ex"], p["ssm_norm_w"])
    ys = _mmx(tag + "_ssmproj", (t // TMM, 1, NCH),
              (ysn, (TMM, cs), lambda i, j, k: (i, k)),
              (ws, (None, None, cs, D_MODEL), lambda i, j, k: (k, li, 0, 0)),
              ((t, D_MODEL), F32, *row), (1, 0))
    mixed = _merge(tag + "_merge", ya, ys, proj, p["b_gates"])
    out = _mmx(tag + "_out", (t // TMM, 1, NCH),
               (mixed, (TMM, co), lambda i, j, k: (i, k)),
               (wo, (None, None, co, D_MODEL), lambda i, j, k: (k, li, 0, 0)),
               ((t, D_MODEL), F32, *row), (1, 0), res=(x, *row))
    return out, (x, h, proj, qk, os_, ls_, att, ya, xc3, dtb3, dab3, states, y, ysn, ys, mixed)


def _mixer_bwd(tag, saved, p, c, dout, bufs):
    x, h, proj, qk, os_, ls_, att, ya, xc3, dtb3, dab3, states, y, ysn, ys, mixed = saved
    t = x.shape[0]
    bl = t // SEQ
    xc = xc3.reshape(t, XBC)
    proj3 = proj.reshape(bl, SEQ, NP)
    gr = {}
    li, gl = p["layer"], p["global_layer"]
    wa, ws, wo = p["w_att_proj"], p["w_ssm_proj"], p["w_out"]
    ca, cs, co = wa.shape[-1], ws.shape[-2], wo.shape[-2]
    b_att, b_ssm, b_out = bufs

    def chunk_dx(name, d, w, cw):
        return _mmx(name, (t // TMM, NCH, D_MODEL // TKK),
                    (d, (TMM, TKK), lambda i, k, kk: (i, kk)),
                    (w, (None, None, cw, TKK), lambda i, k, kk: (k, li, 0, kk)),
                    ((t, NCH * cw), F32, (TMM, cw), lambda i, k, kk: (i, k)), (1, 1))

    def full_dw(name, a_, d, buf):
        kdim = a_.shape[1]
        tm = min(kdim, 1024)
        return _mmx(name, (kdim // tm, 1, t // TKK),
                    (a_, (TKK, tm), lambda i, j, kk: (kk, i)),
                    (d, (TKK, D_MODEL), lambda i, j, kk: (kk, 0)),
                    (buf.shape, BF16, (None, tm, D_MODEL), lambda i, j, kk: (gl, i, 0)), (0, 0), into=buf)

    dmixed = chunk_dx(tag + "_out_dx", dout, wo, co)
    b_out = full_dw(tag + "_out_dw", mixed, dout, b_out)
    dya, dys, dga, dgs, dba, dbs = _merge_bwd(tag + "_merge_bwd", ya, ys, proj, p["b_gates"], dmixed)
    gr["b_gates"] = jnp.concatenate([dba, dbs], axis=1)
    datt = _mmx(tag + "_attproj_dx", (t // TMM, 1, NCH),
                (dya, (TMM, ca), lambda i, j, k: (i, k)),
                (wa, (None, None, ATT_OUT, ca), lambda i, j, k: (k, li, 0, 0)),
                ((t, ATT_OUT), F32, (TMM, ATT_OUT), lambda i, j, k: (i, 0)), (1, 1))
    b_att = _mmx(tag + "_attproj_dw", (NCH, 1, t // TKK),
                 (att, (TKK, ATT_OUT), lambda k, j, kk: (kk, 0)),
                 (dya, (TKK, ca), lambda k, j, kk: (kk, k)),
                 (b_att.shape, BF16, (None, None, ATT_OUT, ca), lambda k, j, kk: (gl, k, 0, 0)), (0, 0), into=b_att)
    dysn = chunk_dx(tag + "_ssmproj_dx", dys, ws, cs)
    b_ssm = full_dw(tag + "_ssmproj_dw", ysn, dys, b_ssm)
    gr["bufs"] = (b_att, b_ssm, b_out)
    dmix = _attmix_bwd(tag + "_attmix_bwd", os_, ls_, datt)
    dq = dk = dv = jnp.zeros((t, QKV), F32)
    for g in range(3):
        dq, dk, dv = _att_bwd(f"{tag}_att{g}_bwd", qk, proj, g, dmix[g], dmix[3 + g], dq, dk, dv)
    dproj = jnp.zeros((t, NP), BF16)
    dproj = lax.dynamic_update_slice(dproj, dv.astype(BF16), (0, V0))
    dproj = lax.dynamic_update_slice(dproj, dga, (0, G0))
    dproj = lax.dynamic_update_slice(dproj, dgs, (0, G0 + D_MODEL))
    dproj, gr["qkw"] = _qkprep_bwd(tag + "_qk_bwd", proj, p["qkw"], c["cos"], c["sin"], c, dq, dk, dproj)
    dy, dxs_part, dproj, gr["d_skip"], gr["ssm_norm_w"] = _ssdpost_bwd(
        tag + "_ssdpost_bwd", y, xc, proj, p["d_skip"], c["ex"], p["ssm_norm_w"], dysn, dproj)
    dxs3, ddtb3, ddab3, db3, dc3 = _ssd_bwd(
        tag + "_ssd_bwd", xc3, dtb3, dab3, c["ltri"], states, dy.reshape(bl, SEQ, D_INNER), dxs_part.reshape(bl, SEQ, D_INNER))
    ddt, gr["dt_bias"], gr["a_log"] = _ssdpre_bwd(
        tag + "_ssdpre_bwd", proj, p["dt_bias"], p["a_log"], c["ex"], ddtb3.reshape(t, D_INNER), ddab3.reshape(t, D_INNER))
    dproj = lax.dynamic_update_slice(dproj, ddt, (0, DT0))
    dproj3, dcw0, dcw1, dcw2, dcw3, gr["conv_b"] = _conv_bwd(
        tag + "_conv_bwd", proj3, p["conv_w"], p["conv_b"], dxs3, db3, dc3, dproj.reshape(bl, SEQ, NP))
    dproj = dproj3.reshape(t, NP)
    gr["conv_w"] = jnp.concatenate([dcw0, dcw1, dcw2, dcw3], axis=0)
    gr["w_in"] = _mm(tag + "_in_dw", h, dproj, "tn", out_dtype=BF16)
    dh = _mm(tag + "_in_dx", dproj, p["w_in"], "nt")
    dx, gr["mix_norm_w"] = _rmsnorm_bwd(tag + "_norm_bwd", x, p["mix_norm_w"], dh, dout)
    return dx, gr


def _constants():
    cos, sin = _rope_tables()
    return dict(cos=cos, sin=sin, hmean=_head_mean_mat(),
                ex=_head_expand_mat(), ltri=_ltri_mat())


ANY = pl.BlockSpec(memory_space=pl.ANY)


def _mesh_pos():
    return lax.axis_index("x"), lax.axis_index("y"), lax.axis_index("c")


def _other_chips(x, y):
    return [(1 - x, y), (x, 1 - y), (1 - x, 1 - y)]


def _gather_exchange(srcs, outs, send_sems, recv_sems):
    n = len(srcs)
    x, y, c = _mesh_pos()
    chips = _other_chips(x, y)

    def part(a, chip, hf):
        h = srcs[a].shape[1] // 2
        return outs[a].at[2 * chip[0] + chip[1], :, pl.ds(hf * h, h), :]

    def mine(a):
        h = srcs[a].shape[1] // 2
        return srcs[a].at[:, pl.ds(c * h, h), :]

    def copy(a, k, src_ref, dst_ref, to):
        return pltpu.make_async_remote_copy(src_ref=src_ref, dst_ref=dst_ref, send_sem=send_sems.at[6 * a + k],
                                            recv_sem=recv_sems.at[6 * a + k], device_id=to, device_id_type=MESH)

    first = [copy(a, j, mine(a), part(a, (x, y), c), (*chip, c)) for a in range(n) for j, chip in enumerate(chips)]
    for cp in first:
        cp.start()
    passed = []
    for a in range(n):
        for j, chip in enumerate(chips):
            copy(a, j, part(a, chip, c), part(a, chip, c), (x, y, c)).wait_recv()
            fw = copy(a, 3 + j, part(a, chip, c), part(a, chip, c), (x, y, 1 - c))
            fw.start()
            passed.append(fw)
    for a in range(n):
        for j, chip in enumerate(chips):
            copy(a, 3 + j, part(a, chip, 1 - c), part(a, chip, 1 - c), (x, y, c)).wait_recv()
    for cp in first + passed:
        cp.wait_send()


def _gather_inits(ws, chip_idx):
    return [lax.dynamic_update_slice(jnp.zeros((NCH, *w.shape), w.dtype), w[None], (chip_idx[0], 0, 0, 0)) for w in ws]


def _all_gather_weights(ws, chip_idx):
    n = len(ws)
    inits = _gather_inits(ws, chip_idx)

    def body(*refs):
        _gather_exchange(refs[:n], refs[2 * n:3 * n], refs[3 * n], refs[3 * n + 1])

    return pl.pallas_call(
        body, name="all_gather_weights", out_shape=[jax.ShapeDtypeStruct(i.shape, i.dtype) for i in inits],
        in_specs=[ANY] * (2 * n), out_specs=[ANY] * n, input_output_aliases={n + a: a for a in range(n)},
        scratch_shapes=[pltpu.SemaphoreType.DMA((6 * n,)), pltpu.SemaphoreType.DMA((6 * n,))],
    )(*ws, *inits)


def _all_gather_weights_beside(name, ws, chip_idx):
    n = len(ws)
    src_refs = [jax.new_ref(w, memory_space=pltpu.MemorySpace.HBM) for w in ws]
    out_refs = [jax.new_ref(i, memory_space=pltpu.MemorySpace.HBM) for i in _gather_inits(ws, chip_idx)]

    @pl.kernel(mesh=plsc.ScalarSubcoreMesh(axis_name="sequencer", num_cores=1), name=name,
               scratch_types=(pltpu.SemaphoreType.DMA((6 * n,)), pltpu.SemaphoreType.DMA((6 * n,))),
               compiler_params=pltpu.CompilerParams(collective_id=1))
    def launch(send_sems, recv_sems):
        x, y, c = _mesh_pos()
        barrier = pltpu.get_barrier_semaphore()
        for peer in [(x, y, 1 - c)] + [(*chip, c) for chip in _other_chips(x, y)]:
            pl.semaphore_signal(barrier, inc=1, device_id=peer, device_id_type=MESH)
        pl.semaphore_wait(barrier, 4)
        _gather_exchange(src_refs, out_refs, send_sems, recv_sems)

    launch()
    return [r[...] for r in out_refs]


def _pair_exchange(name, gs):
    n = len(gs)

    def body(*refs):
        srcs, outs, send_sems, recv_sems = refs[:n], refs[n:2 * n], refs[2 * n], refs[2 * n + 1]
        x, y, c = _mesh_pos()
        cps = []
        for a in range(n):
            h = gs[a].shape[2] // 2
            cps.append(pltpu.make_async_remote_copy(
                src_ref=srcs[a].at[:, :, pl.ds((1 - c) * h, h), :], dst_ref=outs[a], send_sem=send_sems.at[a],
                recv_sem=recv_sems.at[a], device_id=(x, y, 1 - c), device_id_type=MESH))
        for cp in cps:
            cp.start()
        for cp in cps:
            cp.wait()

    return pl.pallas_call(
        body, name=name,
        out_shape=[jax.ShapeDtypeStruct((g.shape[0], g.shape[1], g.shape[2] // 2, g.shape[3]), g.dtype) for g in gs],
        in_specs=[ANY] * n, out_specs=[ANY] * n,
        scratch_shapes=[pltpu.SemaphoreType.DMA((n,)), pltpu.SemaphoreType.DMA((n,))],
    )(*gs)


def _chip_exchange_copies(srcs, outs, send_sems, recv_sems):
    x, y, c = _mesh_pos()
    cps = [pltpu.make_async_remote_copy(
        src_ref=srcs[a].at[:, 2 * chip[0] + chip[1]], dst_ref=outs[a].at[j], send_sem=send_sems.at[3 * a + j],
        recv_sem=recv_sems.at[3 * a + j], device_id=(*chip, c), device_id_type=MESH)
        for a in range(len(srcs)) for j, chip in enumerate(_other_chips(x, y))]
    for cp in cps:
        cp.start()
    for cp in cps:
        cp.wait()


def _chip_exchange_shapes(hs):
    return [jax.ShapeDtypeStruct((3, h.shape[0], h.shape[2], h.shape[3]), h.dtype) for h in hs]


def _chip_exchange(hs):
    n = len(hs)

    def body(*refs):
        _chip_exchange_copies(refs[:n], refs[n:2 * n], refs[2 * n], refs[2 * n + 1])

    return pl.pallas_call(
        body, name="grad_chip_exchange", out_shape=_chip_exchange_shapes(hs),
        in_specs=[ANY] * n, out_specs=[ANY] * n,
        scratch_shapes=[pltpu.SemaphoreType.DMA((3 * n,)), pltpu.SemaphoreType.DMA((3 * n,))],
    )(*hs)


def _zero_after(v):
    return jnp.minimum(lax.bitcast_convert_type(v, jnp.uint16).astype(jnp.int32), 0).astype(F32)


def _chip_exchange_beside(name, hs):
    n = len(hs)
    src_refs = [jax.new_ref(h, memory_space=pltpu.MemorySpace.HBM) for h in hs]
    out_refs = [jax.empty_ref(s, memory_space=pltpu.MemorySpace.HBM) for s in _chip_exchange_shapes(hs)]

    @pl.kernel(mesh=plsc.ScalarSubcoreMesh(axis_name="sequencer", num_cores=1), name=name,
               scratch_types=(pltpu.SemaphoreType.DMA((3 * n,)), pltpu.SemaphoreType.DMA((3 * n,))),
               compiler_params=pltpu.CompilerParams(collective_id=2))
    def launch(send_sems, recv_sems):
        x, y, c = _mesh_pos()
        barrier = pltpu.get_barrier_semaphore()
        for chip in _other_chips(x, y):
            pl.semaphore_signal(barrier, inc=1, device_id=(*chip, c), device_id_type=MESH)
        pl.semaphore_wait(barrier, 3)
        _chip_exchange_copies(src_refs, out_refs, send_sems, recv_sems)

    launch()
    return [r[...] for r in out_refs]


def _pair_share(rs):
    n = len(rs)

    def body(*refs):
        outs, send_sems, recv_sems = refs[n:2 * n], refs[2 * n], refs[2 * n + 1]
        x, y, c = _mesh_pos()
        cps = [pltpu.make_async_remote_copy(src_ref=outs[a].at[:, c], dst_ref=outs[a].at[:, c], send_sem=send_sems.at[a],
                                            recv_sem=recv_sems.at[a], device_id=(x, y, 1 - c), device_id_type=MESH)
               for a in range(n)]
        for cp in cps:
            cp.start()
        for a in range(n):
            pltpu.make_async_remote_copy(src_ref=outs[a].at[:, 1 - c], dst_ref=outs[a].at[:, 1 - c],
                                         send_sem=send_sems.at[a], recv_sem=recv_sems.at[a], device_id=(x, y, c),
                                         device_id_type=MESH).wait_recv()
        for cp in cps:
            cp.wait_send()

    return pl.pallas_call(
        body, name="grad_pair_share", out_shape=[jax.ShapeDtypeStruct(r.shape, r.dtype) for r in rs],
        in_specs=[ANY] * n, out_specs=[ANY] * n, input_output_aliases={a: a for a in range(n)},
        scratch_shapes=[pltpu.SemaphoreType.DMA((n,)), pltpu.SemaphoreType.DMA((n,))],
    )(*rs)


def _pair_sum(name, g, recv, c_idx):
    d, k, h, b = recv.shape
    g5 = g.reshape(d * k, 2, h, b)

    def body(c_ref, a_ref, b_ref, o_ref):
        o_ref[...] = (a_ref[...].astype(F32) + b_ref[...].astype(F32)).astype(o_ref.dtype)

    out = pl.pallas_call(
        body, name=name,
        grid_spec=pltpu.PrefetchScalarGridSpec(
            num_scalar_prefetch=1, grid=(d * k,),
            in_specs=[pl.BlockSpec((None, None, h, b), lambda i, c: (i, c[0], 0, 0)),
                      pl.BlockSpec((None, h, b), lambda i, c: (i, 0, 0))],
            out_specs=pl.BlockSpec((None, h, b), lambda i, c: (i, 0, 0))),
        out_shape=jax.ShapeDtypeStruct((d * k, h, b), BF16),
        compiler_params=_cp(("arbitrary",)),
    )(c_idx, g5, recv.reshape(d * k, h, b))
    return out.reshape(d, k, h, b)


def _chip_sum(name, ha, recv, chip_idx, c_idx, depth, l0, into=None):
    d, _, h, b = ha.shape

    def body(k_ref, c_ref, a_ref, r0, r1, r2, *rest):
        rest[-1][...] = ((a_ref[...].astype(F32) + r0[...].astype(F32)) + r1[...].astype(F32)) + r2[...].astype(F32)

    blk = (None, None, h, b)
    extra = [] if into is None else [into]
    return pl.pallas_call(
        body, name=name,
        grid_spec=pltpu.PrefetchScalarGridSpec(
            num_scalar_prefetch=2, grid=(d,),
            in_specs=[pl.BlockSpec(blk, lambda l, k, c: (l, k[0], 0, 0))] +
                     [pl.BlockSpec(blk, lambda l, k, c, j=j: (j, l, 0, 0)) for j in range(3)] + [ANY] * len(extra),
            out_specs=pl.BlockSpec(blk, lambda l, k, c: (l0 + l, c[0], 0, 0))),
        out_shape=jax.ShapeDtypeStruct((depth, 2, h, b), F32),
        input_output_aliases={} if into is None else {6: 0},
        compiler_params=_cp(("arbitrary",)),
    )(chip_idx, c_idx, ha, recv, recv, recv, *extra)


def _all_sum_small(name, vec):
    rows = vec.shape[0]

    def body(v_ref, o_ref, buf, send_sems, recv_sems):
        x, y, c = _mesh_pos()
        me, sibling = (x, y, c), (x, y, 1 - c)
        chips = _other_chips(x, y)

        def slot(p):
            return buf.at[4 * p[0] + 2 * p[1] + p[2]]

        def copy(k, block, to, src=None):
            return pltpu.make_async_remote_copy(src_ref=slot(block) if src is None else src, dst_ref=slot(block),
                                                send_sem=send_sems.at[k], recv_sem=recv_sems.at[k],
                                                device_id=to, device_id_type=MESH)

        first = [copy(0, me, sibling, src=v_ref)]
        first += [copy(1 + j, me, (*chip, c), src=v_ref) for j, chip in enumerate(chips)]
        for cp in first:
            cp.start()
        passed = [copy(4 + j, (*chip, c), sibling) for j, chip in enumerate(chips)]
        for j, chip in enumerate(chips):
            copy(1 + j, (*chip, c), me).wait_recv()
            passed[j].start()
        copy(0, sibling, me).wait_recv()
        for j, chip in enumerate(chips):
            copy(4 + j, (*chip, 1 - c), me).wait_recv()
        for cp in first + passed:
            cp.wait_send()
        slot(me)[...] = v_ref[...]
        acc = buf[0]
        for k in range(1, 8):
            acc = acc + buf[k]
        o_ref[...] = acc

    vm = pl.BlockSpec(memory_space=pltpu.VMEM)
    return pl.pallas_call(
        body, name=name, out_shape=jax.ShapeDtypeStruct((rows, LANES), F32),
        in_specs=[vm], out_specs=vm, compiler_params=pltpu.CompilerParams(vmem_limit_bytes=VMEM_LIMIT),
        scratch_shapes=[pltpu.VMEM((8, rows, LANES), F32), pltpu.SemaphoreType.DMA((7,)), pltpu.SemaphoreType.DMA((7,))],
    )(vec)


def _pad_lanes(v, n=LANES):
    return jnp.pad(v, (0, n - v.shape[0]))[None, :]


def _w_in_from_kernel(w):
    return jnp.concatenate([w[:, :6656], w[:, X0:DT0], w[:, DT0:DT0 + 32], w[:, G0:X0]], axis=1)


def _layer_params(big, small, i):
    p = {k: big[k][i][0] for k in GRAD_BUFS}
    li = big["w_in"][i][1]
    p["layer"], p["global_layer"] = li, i
    w_in = big["w_in"][i][0]
    cw = w_in.shape[-1]
    pieces = []
    for lo, hi in ((0, 6656), (9760, N_IN), (6656, 9728), (9728, 9760)):
        for k in range(NCH):
            a, b = max(lo, k * cw), min(hi, (k + 1) * cw)
            if a < b:
                pieces.append(w_in[k, li, :, a - k * cw:b - k * cw])
    p["w_in"] = jnp.concatenate(pieces + [jnp.zeros((D_MODEL, NP - N_IN), w_in.dtype)], axis=1)
    p["conv_w"] = big["conv_w"][i][:, None, :]
    for k in ("ffn1_norm_w", "mix_norm_w", "ffn2_norm_w", "b_gates", "conv_b", "ssm_norm_w"):
        p[k] = small[k][i][None, :]
    for k in ("dt_bias", "a_log", "d_skip"):
        p[k] = _pad_lanes(small[k][i])
    p["qkw"] = jnp.stack([_pad_lanes(small["q_norm_w"][i]), _pad_lanes(small["k_norm_w"][i])])
    return p


GRAD_BUFS = ("ffn1_w_gate", "ffn1_w_up", "ffn1_w_down", "w_att_proj", "w_ssm_proj", "w_out",
             "ffn2_w_gate", "ffn2_w_up", "ffn2_w_down")


def _local_step(x, tgt, layers, c, exchange_rest=None):
    depth = len(layers)
    ffn = {f: tuple(f + s for s in ("_w_gate", "_w_up", "_w_down")) for f in ("ffn1", "ffn2")}
    saved = []
    for i, p in enumerate(layers):
        x, s1 = _ffn_fwd(f"L{i}_ffn1", x, p["ffn1_norm_w"], *[p[n] for n in ffn["ffn1"]], p["layer"])
        x, s2 = _mixer_fwd(f"L{i}_mix", x, p, c)
        x, s3 = _ffn_fwd(f"L{i}_ffn2", x, p["ffn2_norm_w"], *[p[n] for n in ffn["ffn2"]], p["layer"])
        saved.append((s1, s2, s3))
    dx, loss_blk = _loss("loss", x, tgt)

    def new_bufs(nl):
        out = {}
        for n in GRAD_BUFS:
            a, b = layers[0][n].shape[-2:]
            out[n] = jnp.zeros((nl, NCH * a, b) if n in ("w_ssm_proj", "w_out") else (nl, NCH, a, b), BF16)
        return out

    def finished(buf):
        out = dict(buf)
        for n in ("w_ssm_proj", "w_out"):
            a, b = layers[0][n].shape[-2:]
            out[n] = buf[n].reshape(-1, NCH, a, b)
        return out

    grads = [None] * depth
    rest_out = [None] * depth
    for i in reversed(range(depth)):
        p = layers[i]
        s1, s2, s3 = saved[i]
        buf, gl = new_bufs(1), 0
        p = dict(p, global_layer=gl)
        if i + 2 < depth and exchange_rest is not None:
            dx = dx + _zero_after(rest_out[i + 2][1][0][0, 0, 0, 0])

        def ffn_bwd(f, s, dy):
            names = ffn[f]
            d, dn, new = _ffn_bwd(f"L{i}_{f}", s, p[f + "_norm_w"], *[p[n] for n in names], p["layer"], dy,
                                  tuple(buf[n] for n in names), gl)
            buf.update(zip(names, new))
            return d, dn

        dx, dn2 = ffn_bwd("ffn2", s3, dx)
        dx, gr = _mixer_bwd(f"L{i}_mix", s2, p, c, dx, (buf["w_att_proj"], buf["w_ssm_proj"], buf["w_out"]))
        buf["w_att_proj"], buf["w_ssm_proj"], buf["w_out"] = gr.pop("bufs")
        if i == 0 and depth > 1 and exchange_rest is not None:
            rest_out[0], ready = exchange_rest(0, finished(buf), [gr], rest_out[1][1][0][0, 0, 0, 0])
            dx = dx + _zero_after(ready)
        dx, dn1 = ffn_bwd("ffn1", s1, dx)
        gr.update(ffn1_norm_w=dn1, ffn2_norm_w=dn2)
        grads[i] = gr
        if i >= 1 and exchange_rest is not None:
            rest_out[i], ready = exchange_rest(i, finished(buf), grads[i:i + 1], None)
            dx = dx + _zero_after(ready)
    return loss_blk, dx, grads, finished(buf), rest_out


WEIGHTS = ["ffn1_norm_w", "ffn1_w_gate", "ffn1_w_up", "ffn1_w_down", "mix_norm_w", "w_in", "b_gates", "q_norm_w",
           "k_norm_w", "conv_w", "conv_b", "dt_bias", "a_log", "d_skip", "ssm_norm_w", "w_att_proj", "w_ssm_proj",
           "w_out", "ffn2_norm_w", "ffn2_w_gate", "ffn2_w_up", "ffn2_w_down"]
SHARD_AXIS = {"ffn1_w_gate": 2, "ffn1_w_up": 2, "ffn1_w_down": 1, "w_in": 2, "conv_w": 2, "w_att_proj": 2,
              "w_ssm_proj": 1, "w_out": 1, "ffn2_w_gate": 2, "ffn2_w_up": 2, "ffn2_w_down": 1}
BIG = [n for n in WEIGHTS if n in SHARD_AXIS]
SMALL = [n for n in WEIGHTS if n not in SHARD_AXIS]
def _from_flat(flat, shapes):
    v = flat.reshape(-1)
    out, off = [], 0
    for s in shapes:
        n = math.prod(s)
        out.append(v[off:off + n].reshape(s))
        off += n
    return out


def _pack_small(parts):
    v = jnp.concatenate([p.astype(F32).reshape(-1) for p in parts])
    rows = -(-v.shape[0] // (8 * LANES)) * 8
    return jnp.pad(v, (0, rows * LANES - v.shape[0])).reshape(rows, LANES)


def kernel(x, ffn1_norm_w, ffn1_w_gate, ffn1_w_up, ffn1_w_down, mix_norm_w, w_in, b_gates, q_norm_w, k_norm_w, conv_w, conv_b, dt_bias, a_log, d_skip, ssm_norm_w, w_att_proj, w_ssm_proj, w_out, ffn2_norm_w, ffn2_w_gate, ffn2_w_up, ffn2_w_down, loss_target, m_ffn1_norm_w, m_ffn1_w_gate, m_ffn1_w_up, m_ffn1_w_down, m_mix_norm_w, m_w_in, m_b_gates, m_q_norm_w, m_k_norm_w, m_conv_w, m_conv_b, m_dt_bias, m_a_log, m_d_skip, m_ssm_norm_w, m_w_att_proj, m_w_ssm_proj, m_w_out, m_ffn2_norm_w, m_ffn2_w_gate, m_ffn2_w_up, m_ffn2_w_down, v_ffn1_norm_w, v_ffn1_w_gate, v_ffn1_w_up, v_ffn1_w_down, v_mix_norm_w, v_w_in, v_b_gates, v_q_norm_w, v_k_norm_w, v_conv_w, v_conv_b, v_dt_bias, v_a_log, v_d_skip, v_ssm_norm_w, v_w_att_proj, v_w_ssm_proj, v_w_out, v_ffn2_norm_w, v_ffn2_w_gate, v_ffn2_w_up, v_ffn2_w_down):
    w = dict(zip(WEIGHTS, (ffn1_norm_w, ffn1_w_gate, ffn1_w_up, ffn1_w_down, mix_norm_w, w_in, b_gates, q_norm_w, k_norm_w, conv_w, conv_b, dt_bias, a_log, d_skip, ssm_norm_w, w_att_proj, w_ssm_proj, w_out, ffn2_norm_w, ffn2_w_gate, ffn2_w_up, ffn2_w_down)))
    m = dict(zip(WEIGHTS, (m_ffn1_norm_w, m_ffn1_w_gate, m_ffn1_w_up, m_ffn1_w_down, m_mix_norm_w, m_w_in, m_b_gates, m_q_norm_w, m_k_norm_w, m_conv_w, m_conv_b, m_dt_bias, m_a_log, m_d_skip, m_ssm_norm_w, m_w_att_proj, m_w_ssm_proj, m_w_out, m_ffn2_norm_w, m_ffn2_w_gate, m_ffn2_w_up, m_ffn2_w_down)))
    v = dict(zip(WEIGHTS, (v_ffn1_norm_w, v_ffn1_w_gate, v_ffn1_w_up, v_ffn1_w_down, v_mix_norm_w, v_w_in, v_b_gates, v_q_norm_w, v_k_norm_w, v_conv_w, v_conv_b, v_dt_bias, v_a_log, v_d_skip, v_ssm_norm_w, v_w_att_proj, v_w_ssm_proj, v_w_out, v_ffn2_norm_w, v_ffn2_w_gate, v_ffn2_w_up, v_ffn2_w_down)))
    depth = ffn1_norm_w.shape[0]
    bl = x.shape[0]
    t = bl * SEQ
    mx, my, mc = lax.axis_index("x"), lax.axis_index("y"), lax.axis_index("c")
    c_idx = mc.astype(jnp.int32).reshape(1)
    chip_idx = (2 * mx + my).astype(jnp.int32).reshape(1)

    cw_width = conv_w.shape[2]
    slots = lax.dynamic_update_slice(jnp.zeros((NCH, *conv_w.shape), F32), jnp.where(mc == 0, conv_w, 0.0)[None],
                                     (chip_idx[0], 0, 0, 0))
    conv_all = _all_sum_small("conv_gather", slots.reshape(-1, LANES)).reshape(NCH, *conv_w.shape)
    big = {"conv_w": jnp.concatenate([conv_all[k] for k in range(NCH)], axis=2)}

    mm_names = [n for n in BIG if n != "conv_w"]
    head = [n for n in mm_names if n.startswith("ffn1")]
    tail = [n for n in mm_names if n not in head]
    got_head = _all_gather_weights([w[n][:1].astype(BF16) for n in head], chip_idx)
    zero = _zero_after(got_head[0][0, 0, 0, 0])
    got_tail = _all_gather_weights_beside("all_gather_weights_beside_L0", [(w[n][:1] + zero).astype(BF16) for n in tail],
                                          chip_idx)
    first = dict(zip(head + tail, list(got_head) + list(got_tail)))
    if depth > 1:
        zero = _zero_after(got_tail[0][0, 0, 0, 0])
        rest = dict(zip(mm_names, _all_gather_weights_beside(
            "all_gather_weights_beside_rest", [(w[n][1:] + zero).astype(BF16) for n in mm_names], chip_idx)))
    for n in mm_names:
        big[n] = [(first[n], 0)] + [(rest[n], i - 1) for i in range(1, depth)]
    small = {n: w[n] for n in SMALL}

    def exchange(tag, names, buf, layer_grads, beside, after=None):
        buf = dict(buf)
        if "w_in" in names:
            buf["w_in"] = jnp.stack([_w_in_from_kernel(g["w_in"]).reshape(D_MODEL, NCH, -1).transpose(1, 0, 2)
                                     for g in layer_grads])
        mine = [buf[n] for n in names]
        if after is not None:
            mine[0] = mine[0] + _zero_after(after).astype(BF16)
        from_sibling = _pair_exchange("grad_pair_exchange_" + tag, mine)
        pairs = [_pair_sum(f"grad_pair_sum_{tag}_{n}", g, r, c_idx) for n, g, r in zip(names, mine, from_sibling)]
        recv = _chip_exchange_beside("grad_chip_exchange_beside_" + tag, pairs) if beside else _chip_exchange(pairs)
        return dict(zip(names, zip(pairs, recv)))

    c = _constants()
    layers = [_layer_params(big, small, i) for i in range(depth)]

    def exchange_rest(i, buf, layer_grads, after):
        names = mm_names if i >= 1 else tail
        out = exchange(f"L{i}", names, buf, layer_grads, True, after)
        return (None, [out[names[0]][1]], out), sum(out[n][0][0, 0, 0, 0] for n in names)

    loss_blk, dx, grads, buf0, rest_out = _local_step(
        x.reshape(t, D_MODEL), loss_target.reshape(t, D_MODEL), layers, c, exchange_rest if depth > 1 else None)
    grad_x = dx.reshape(bl, SEQ, D_MODEL)
    if depth > 1:
        last = exchange("L0_ffn1", head, buf0, [], False, rest_out[0][1][0][0, 0, 0, 0])
        layer0 = {**rest_out[0][2], **last}
    else:
        layer0 = exchange("L0", mm_names, buf0, grads[:1], False)
    halves = []
    for n in mm_names:
        acc = None
        for i in range(depth - 1, 0, -1):
            acc = _chip_sum(f"grad_chip_sum_L{i}_{n}", *rest_out[i][2][n], chip_idx, c_idx, depth, i, into=acc)
        halves.append(_chip_sum("grad_chip_sum_L0_" + n, *layer0[n], chip_idx, c_idx, depth, 0, into=acc))
    g_big = {n: r.reshape(w[n].shape) for n, r in zip(mm_names, _pair_share(halves))}

    def small_grad(n):
        if n == "q_norm_w":
            return jnp.stack([g["qkw"][0, 0, :64] for g in grads])
        if n == "k_norm_w":
            return jnp.stack([g["qkw"][1, 0, :64] for g in grads])
        return jnp.stack([g[n][0, :w[n].shape[1]] for g in grads])

    small_shapes = [w[n].shape for n in SMALL]
    conv_shape = (depth, conv_w.shape[1], NCH * cw_width)
    tot = _all_sum_small("small_all_sum", _pack_small(
        [small_grad(n) for n in SMALL] + [jnp.stack([g["conv_w"] for g in grads]), loss_blk[0, :1]]))
    unpacked = _from_flat(tot, small_shapes + [conv_shape, (1,)])
    g_small = dict(zip(SMALL, unpacked[:-2]))
    g_big["conv_w"] = lax.dynamic_slice_in_dim(unpacked[-2], chip_idx[0] * cw_width, cw_width, axis=2)
    loss = unpacked[-1][0]

    grad, delta, new_m, new_v = {}, {}, {}, {}
    for n in BIG:
        shp = w[n].shape
        two_d = (shp[0] * shp[1], shp[2])
        d_, m_, v_ = _adamw("adamw_" + n, w[n].reshape(two_d), g_big[n].reshape(two_d), m[n].reshape(two_d), v[n].reshape(two_d))
        grad[n], delta[n], new_m[n], new_v[n] = g_big[n], d_.reshape(shp), m_.reshape(shp), v_.reshape(shp)
    d_, m_, v_ = _adamw("adamw_small", _pack_small([w[n] for n in SMALL]), _pack_small([g_small[n] for n in SMALL]),
                        _pack_small([m[n] for n in SMALL]), _pack_small([v[n] for n in SMALL]))
    for n, a, b, c_ in zip(SMALL, _from_flat(d_, small_shapes), _from_flat(m_, small_shapes), _from_flat(v_, small_shapes)):
        grad[n], delta[n], new_m[n], new_v[n] = g_small[n], a, b, c_
    return (loss, grad_x, *[grad[n] for n in WEIGHTS], *[delta[n] for n in WEIGHTS],
            *[new_m[n] for n in WEIGHTS], *[new_v[n] for n in WEIGHTS])
```
